```python
import math
import jax, jax.numpy as jnp
from jax import lax
import numpy as np

D_MODEL = 2048
BATCH = 4
SEQ = 2048
DEPTH = 1

MEM_LEN = 256
D_ATTN = D_MODEL // 2
D_SSM = D_MODEL - D_ATTN
DA_HEADS = 4
DA_QK_DIM = D_ATTN // DA_HEADS // 2
DA_V_DIM = 2 * DA_QK_DIM
Q_BLOCK = 128
SSM_GROUP = 16
SSM_GROUPS = D_SSM // SSM_GROUP
SSM_STATE = 64
X_HEADS = 4
X_HEAD_DIM = D_MODEL // X_HEADS
MOE_GROUPS = 8
EXP_PER_GROUP = 8
N_EXPERTS = MOE_GROUPS * EXP_PER_GROUP
TOP_K_FINE = 2
D_FF_EXPERT = D_MODEL // 4
MOE_BLOCK = 128
EPS = 1e-6
Q_COLS = DA_HEADS * 2 * DA_QK_DIM
K_COLS = DA_HEADS * 2 * DA_QK_DIM
V_COLS = DA_HEADS * DA_V_DIM
U_COLS = D_SSM
D_IN_PROJ = Q_COLS + K_COLS + V_COLS + U_COLS

kernel_name = "hymba_diffattn_s5_hmoe_layer"


def rms_norm(x, w):
    xf = x.astype(jnp.float32)
    y = xf * lax.rsqrt(jnp.mean(xf * xf, axis=-1, keepdims=True) + EPS)
    return (y * w.astype(jnp.float32)).astype(x.dtype)


def diff_attention(q, k, v, lam):
    bsz, seq = q.shape[0], q.shape[1]
    nb = seq // Q_BLOCK
    qb = q.reshape(bsz, nb, Q_BLOCK, DA_HEADS, 2, DA_QK_DIM).transpose(1, 0, 2, 3, 4, 5)
    kpos = jnp.arange(seq)
    scale = DA_QK_DIM ** -0.5
    neg = jnp.finfo(jnp.float32).min

    def block(args):
        qi, i = args
        s = jnp.einsum('bqhcd,bkhcd->bhcqk', qi, k).astype(jnp.float32) * scale
        qpos = i * Q_BLOCK + jnp.arange(Q_BLOCK)
        mask = kpos[None, :] <= qpos[:, None]
        p = jax.nn.softmax(jnp.where(mask, s, neg), axis=-1)
        p = p[:, :, 0] - lam * p[:, :, 1]
        return jnp.einsum('bhqk,bkhe->bqhe', p.astype(v.dtype), v)

    out = lax.map(block, (qb, jnp.arange(nb)))
    return out.transpose(1, 0, 2, 3, 4).reshape(bsz, seq, DA_HEADS, DA_V_DIM)


def _ssm_combine(e1, e2):
    a1r, a1i, b1r, b1i = e1
    a2r, a2i, b2r, b2i = e2
    return (a2r * a1r - a2i * a1i,
            a2r * a1i + a2i * a1r,
            a2r * b1r - a2i * b1i + b2r,
            a2r * b1i + a2i * b1r + b2i)


def s5_branch(u, lam_re, lam_im, log_dt, b_re, b_im, c_re, c_im, d_skip, glu_w, glu_b):
    bsz, seq = u.shape[0], u.shape[1]
    f32 = jnp.float32
    uf = u.astype(f32).reshape(bsz, seq, SSM_GROUPS, SSM_GROUP)
    lr = jnp.minimum(lam_re.astype(f32), -1e-4)
    li = lam_im.astype(f32)
    dt = jnp.exp(log_dt.astype(f32))[:, None]
    mag = jnp.exp(lr * dt)
    lb_re, lb_im = mag * jnp.cos(li * dt), mag * jnp.sin(li * dt)
    den = lr * lr + li * li
    coef_re = ((lb_re - 1.0) * lr + lb_im * li) / den
    coef_im = (lb_im * lr - (lb_re - 1.0) * li) / den
    br, bi = b_re.astype(f32), b_im.astype(f32)
    bb_re = coef_re[..., None] * br - coef_im[..., None] * bi
    bb_im = coef_re[..., None] * bi + coef_im[..., None] * br
    bu_re = jnp.einsum('gph,blgh->blgp', bb_re, uf)
    bu_im = jnp.einsum('gph,blgh->blgp', bb_im, uf)
    shape_a = (1, seq, SSM_GROUPS, SSM_STATE)
    a_re = jnp.broadcast_to(lb_re, shape_a)
    a_im = jnp.broadcast_to(lb_im, shape_a)
    _, _, xr, xi = lax.associative_scan(_ssm_combine, (a_re, a_im, bu_re, bu_im), axis=1)
    y = (jnp.einsum('ghp,blgp->blgh', c_re.astype(f32), xr)
         - jnp.einsum('ghp,blgp->blgh', c_im.astype(f32), xi)
         + d_skip.astype(f32) * uf)
    y = jax.nn.gelu(y.reshape(bsz, seq, D_SSM))
    y = y * jax.nn.sigmoid(y @ glu_w.astype(f32) + glu_b.astype(f32))
    return y.astype(u.dtype)


def memory_cross_attention(hn, mn, xq_w, xkv_w, xq_norm_w, xk_norm_w, xo_w):
    bsz, seq = hn.shape[0], hn.shape[1]
    m = mn.shape[1]
    q = rms_norm((hn @ xq_w).reshape(bsz, seq, X_HEADS, X_HEAD_DIM), xq_norm_w)
    kv = (mn @ xkv_w).reshape(bsz, m, 2, X_HEADS, X_HEAD_DIM)
    k = rms_norm(kv[:, :, 0], xk_norm_w)
    v = kv[:, :, 1]
    s = jnp.einsum('bqhd,bmhd->bhqm', q, k).astype(jnp.float32) * (X_HEAD_DIM ** -0.5)
    p = jax.nn.softmax(s, axis=-1)
    o = jnp.einsum('bhqm,bmhd->bqhd', p.astype(v.dtype), v).reshape(bsz, seq, D_MODEL)
    return o @ xo_w


def hier_moe(xn, w_c, b_c, w_f, b_f, w_g, w_u, w_d):
    bsz, seq, d = xn.shape
    t = bsz * seq
    xt = xn.reshape(t, d)
    p_c = jax.nn.softmax((xt @ w_c + b_c).astype(jnp.float32), axis=-1)
    grp = jnp.argmax(p_c, axis=-1).astype(jnp.int32)
    p_grp = jnp.max(p_c, axis=-1)
    fl = (xt @ w_f + b_f).astype(jnp.float32).reshape(t, MOE_GROUPS, EXP_PER_GROUP)
    fl = jnp.take_along_axis(fl, grp[:, None, None], axis=1)[:, 0]
    w_top, j_top = lax.top_k(jax.nn.softmax(fl, axis=-1), TOP_K_FINE)
    w_top = w_top / jnp.sum(w_top, axis=-1, keepdims=True) * p_grp[:, None]
    expert = grp[:, None] * EXP_PER_GROUP + j_top.astype(jnp.int32)
    n_assign = t * TOP_K_FINE
    e_flat = expert.reshape(n_assign)
    t_flat = jnp.repeat(jnp.arange(t, dtype=jnp.int32), TOP_K_FINE)
    w_flat = w_top.reshape(n_assign)
    order = jnp.argsort(e_flat)
    e_s, t_s, w_s = e_flat[order], t_flat[order], w_flat[order]
    counts = jnp.bincount(e_flat, length=N_EXPERTS)
    starts = jnp.cumsum(counts) - counts
    padded = (counts + MOE_BLOCK - 1) // MOE_BLOCK * MOE_BLOCK
    pad_end = jnp.cumsum(padded)
    pad_start = pad_end - padded
    dest = pad_start[e_s] + jnp.arange(n_assign) - starts[e_s]
    n_rows = (n_assign + N_EXPERTS * (MOE_BLOCK - 1) + MOE_BLOCK - 1) // MOE_BLOCK * MOE_BLOCK
    n_blocks = n_rows // MOE_BLOCK
    row_tok = jnp.zeros((n_rows,), jnp.int32).at[dest].set(t_s)
    row_w = jnp.zeros((n_rows,), jnp.float32).at[dest].set(w_s)
    blk_exp = jnp.minimum(
        jnp.searchsorted(pad_end, jnp.arange(n_blocks) * MOE_BLOCK, side='right'),
        N_EXPERTS - 1)
    xs = xt[row_tok].reshape(n_blocks, MOE_BLOCK, d)

    def run(args):
        xb, e = args
        return (jax.nn.silu(xb @ w_g[e]) * (xb @ w_u[e])) @ w_d[e]

    ys = lax.map(run, (xs, blk_exp)).reshape(n_rows, d)
    out = jnp.zeros((t, d), ys.dtype).at[row_tok].add(ys * row_w[:, None].astype(ys.dtype))
    return out.reshape(bsz, seq, d).astype(xn.dtype)


def setup_inputs(seed: int = 0) -> dict:
    key = jax.random.key(seed)
    ks = iter(jax.random.split(key, 64))

    def nrm(shape, scale):
        return jax.random.normal(next(ks), shape, jnp.float32) * scale

    def gain(shape):
        return 1.0 + 0.02 * jax.random.normal(next(ks), shape, jnp.float32)

    L, D, G, P, H = DEPTH, D_MODEL, SSM_GROUPS, SSM_STATE, SSM_GROUP
    n_idx = jnp.arange(P, dtype=jnp.float32)
    inp = {}
    inp['x'] = nrm((BATCH, SEQ, D), 1.0)
    inp['mem'] = nrm((BATCH, MEM_LEN, D), 1.0)
    inp['norm1_w'] = gain((L, D))
    inp['w_in'] = nrm((L, D, D_IN_PROJ), D ** -0.5)
    inp['q_norm_w'] = gain((L, DA_QK_DIM))
    inp['k_norm_w'] = gain((L, DA_QK_DIM))
    inp['lambda_q1'] = nrm((L, DA_QK_DIM), 0.1)
    inp['lambda_k1'] = nrm((L, DA_QK_DIM), 0.1)
    inp['lambda_q2'] = nrm((L, DA_QK_DIM), 0.1)
    inp['lambda_k2'] = nrm((L, DA_QK_DIM), 0.1)
    inp['subln_w'] = gain((L, DA_V_DIM))
    inp['ssm_lambda_re'] = -0.5 + nrm((L, G, P), 0.01)
    inp['ssm_lambda_im'] = math.pi * n_idx + nrm((L, G, P), 0.01)
    inp['ssm_log_dt'] = jax.random.uniform(next(ks), (L, G), jnp.float32,
                                           math.log(0.001), math.log(0.1))
    inp['ssm_b_re'] = nrm((L, G, P, H), (2.0 * H) ** -0.5)
    inp['ssm_b_im'] = nrm((L, G, P, H), (2.0 * H) ** -0.5)
    inp['ssm_c_re'] = nrm((L, G, H, P), (2.0 * P) ** -0.5)
    inp['ssm_c_im'] = nrm((L, G, H, P), (2.0 * P) ** -0.5)
    inp['ssm_d'] = nrm((L, G, H), 1.0)
    inp['ssm_glu_w'] = nrm((L, D_SSM, D_SSM), D_SSM ** -0.5)
    inp['ssm_glu_b'] = nrm((L, D_SSM), 0.01)
    inp['ssm_out_norm_w'] = gain((L, D_SSM))
    inp['w_out'] = nrm((L, D, D), D ** -0.5)
    inp['norm2_w'] = gain((L, D))
    inp['mem_norm_w'] = gain((L, D))
    inp['xq_w'] = nrm((L, D, D), D ** -0.5)
    inp['xkv_w'] = nrm((L, D, 2 * D), D ** -0.5)
    inp['xq_norm_w'] = gain((L, X_HEAD_DIM))
    inp['xk_norm_w'] = gain((L, X_HEAD_DIM))
    inp['xo_w'] = nrm((L, D, D), D ** -0.5)
    inp['norm3_w'] = gain((L, D))
    inp['router_coarse_w'] = nrm((L, D, MOE_GROUPS), D ** -0.5)
    inp['router_coarse_b'] = nrm((L, MOE_GROUPS), 0.01)
    inp['router_fine_w'] = nrm((L, D, N_EXPERTS), D ** -0.5)
    inp['router_fine_b'] = nrm((L, N_EXPERTS), 0.01)
    inp['expert_w_gate'] = nrm((L, N_EXPERTS, D, D_FF_EXPERT), D ** -0.5)
    inp['expert_w_up'] = nrm((L, N_EXPERTS, D, D_FF_EXPERT), D ** -0.5)
    inp['expert_w_down'] = nrm((L, N_EXPERTS, D_FF_EXPERT, D), D_FF_EXPERT ** -0.5)
    return inp


def reference(x, mem, norm1_w, w_in, q_norm_w, k_norm_w, lambda_q1, lambda_k1,
              lambda_q2, lambda_k2, subln_w, ssm_lambda_re, ssm_lambda_im, ssm_log_dt,
              ssm_b_re, ssm_b_im, ssm_c_re, ssm_c_im, ssm_d, ssm_glu_w, ssm_glu_b,
              ssm_out_norm_w, w_out, norm2_w, mem_norm_w, xq_w, xkv_w, xq_norm_w,
              xk_norm_w, xo_w, norm3_w, router_coarse_w, router_coarse_b,
              router_fine_w, router_fine_b, expert_w_gate, expert_w_up, expert_w_down):
    bsz, seq = x.shape[0], x.shape[1]
    h = x
    for l in range(DEPTH):
        lam_init = 0.8 - 0.6 * math.exp(-0.3 * l)
        xn = rms_norm(h, norm1_w[l])
        proj = xn @ w_in[l]
        q, k, v, u = jnp.split(proj, [Q_COLS, Q_COLS + K_COLS, Q_COLS + K_COLS + V_COLS], axis=-1)
        q = rms_norm(q.reshape(bsz, seq, DA_HEADS, 2, DA_QK_DIM), q_norm_w[l])
        k = rms_norm(k.reshape(bsz, seq, DA_HEADS, 2, DA_QK_DIM), k_norm_w[l])
        v = v.reshape(bsz, seq, DA_HEADS, DA_V_DIM)
        f32 = jnp.float32
        lam = (jnp.exp(jnp.sum(lambda_q1[l].astype(f32) * lambda_k1[l].astype(f32)))
               - jnp.exp(jnp.sum(lambda_q2[l].astype(f32) * lambda_k2[l].astype(f32)))
               + lam_init)
        a = diff_attention(q, k, v, lam)
        a = (rms_norm(a, subln_w[l]) * (1.0 - lam_init)).reshape(bsz, seq, D_ATTN)
        s = s5_branch(u, ssm_lambda_re[l], ssm_lambda_im[l], ssm_log_dt[l], ssm_b_re[l],
                      ssm_b_im[l], ssm_c_re[l], ssm_c_im[l], ssm_d[l], ssm_glu_w[l], ssm_glu_b[l])
        s = rms_norm(s, ssm_out_norm_w[l])
        h = h + (jnp.concatenate([a, s.astype(a.dtype)], axis=-1) @ w_out[l]).astype(h.dtype)
        hn = rms_norm(h, norm2_w[l])
        mn = rms_norm(mem, mem_norm_w[l])
        h = h + memory_cross_attention(hn, mn, xq_w[l], xkv_w[l], xq_norm_w[l],
                                       xk_norm_w[l], xo_w[l]).astype(h.dtype)
        hn = rms_norm(h, norm3_w[l])
        h = h + hier_moe(hn, router_coarse_w[l], router_coarse_b[l], router_fine_w[l],
                         router_fine_b[l], expert_w_gate[l], expert_w_up[l],
                         expert_w_down[l]).astype(h.dtype)
    return h
```

```python
import functools
import math

import jax
import jax.numpy as jnp
from jax import lax
from jax.experimental import pallas as pl
from jax.experimental.pallas import tpu as pltpu

F32 = jnp.float32
BF16 = jnp.bfloat16

EPS = 1e-6
DA_HEADS = 4
DA_QK_DIM = 128
DA_V_DIM = 256
SSM_GROUP = 16
SSM_STATE = 64
X_HEADS = 4
MOE_GROUPS = 8
EXP_PER_GROUP = 8
N_EXPERTS = MOE_GROUPS * EXP_PER_GROUP
TOP_K_FINE = 2

LANES = 128
SUBLANES = 8
VMEM_LIMIT = 56 * 1024 * 1024
NEG = -1e30

SSM_CHUNK_GROUPS = LANES // SSM_GROUP
SSM_CHUNK_STATE = SSM_CHUNK_GROUPS * SSM_STATE
SCAN_SEGS = SUBLANES
MOE_BLOCK = 128
ROUTE_LANES = LANES


def _rms(x, eps=EPS):
    return x * lax.rsqrt(jnp.mean(x * x, axis=-1, keepdims=True) + eps)


def _dot(a, b):
    return jnp.dot(a, b, preferred_element_type=F32)


def _dot_nt(a, b):
    return lax.dot_general(a, b, (((1,), (1,)), ((), ())), preferred_element_type=F32)


def _norm_matmul_kernel(x_ref, nw_ref, w_ref, g_ref, o_ref, xn_ref, *, n_norm_tiles, chunk):
    j = pl.program_id(1)

    @pl.when(j == 0)
    def _():
        x = x_ref[...].astype(F32)
        xn_ref[...] = (_rms(x) * nw_ref[...]).astype(BF16)

    acc = _dot(xn_ref[...], w_ref[...])
    tn = acc.shape[1]

    def normed():
        parts = [_rms(acc[:, c * chunk:(c + 1) * chunk]) for c in range(tn // chunk)]
        y = parts[0] if len(parts) == 1 else jnp.concatenate(parts, axis=-1)
        o_ref[...] = (y * g_ref[...]).astype(o_ref.dtype)

    def plain():
        o_ref[...] = acc.astype(o_ref.dtype)

    pl.when(j < n_norm_tiles)(normed)
    pl.when(j >= n_norm_tiles)(plain)


def _norm_matmul(x, norm_w, w_bf16, gain, *, n_norm_cols, chunk, tm, tn, name):
    m, k = x.shape
    n = w_bf16.shape[1]
    assert m % tm == 0 and n % tn == 0 and tn % chunk == 0 and n_norm_cols % tn == 0
    kern = functools.partial(_norm_matmul_kernel, n_norm_tiles=n_norm_cols // tn, chunk=chunk)
    return pl.pallas_call(
        kern,
        grid=(m // tm, n // tn),
        in_specs=[
            pl.BlockSpec((tm, k), lambda i, j: (i, 0)),
            pl.BlockSpec((1, k), lambda i, j: (0, 0)),
            pl.BlockSpec((k, tn), lambda i, j: (0, j)),
            pl.BlockSpec((1, tn), lambda i, j: (0, j)),
        ],
        out_specs=pl.BlockSpec((tm, tn), lambda i, j: (i, j)),
        out_shape=jax.ShapeDtypeStruct((m, n), BF16),
        scratch_shapes=[pltpu.VMEM((tm, k), BF16)],
        compiler_params=pltpu.CompilerParams(
            dimension_semantics=("parallel", "arbitrary"), vmem_limit_bytes=VMEM_LIMIT),
        name=name,
    )(x, norm_w.reshape(1, k).astype(F32), w_bf16, gain.reshape(1, n).astype(F32))


def _diff_attn_kernel(lam_ref, q_ref, k_ref, v_ref, g_ref, o_ref, m_ref, l_ref, acc_ref, *, tq):
    qi = pl.program_id(2)
    m_ref[...] = jnp.full(m_ref.shape, NEG, F32)
    l_ref[...] = jnp.zeros(l_ref.shape, F32)
    acc_ref[...] = jnp.zeros(acc_ref.shape, F32)

    def block(j, masked):
        start = pl.multiple_of(j * tq, tq)
        ks = k_ref[pl.ds(start, tq), :]
        vs = v_ref[pl.ds(start, tq), :]
        for c in range(2):
            lo, hi = c * DA_QK_DIM, (c + 1) * DA_QK_DIM
            s = _dot_nt(q_ref[:, lo:hi], ks[:, lo:hi])
            if masked:
                row = lax.broadcasted_iota(jnp.int32, s.shape, 0)
                col = lax.broadcasted_iota(jnp.int32, s.shape, 1)
                s = jnp.where(col <= row, s, NEG)
            m_old = m_ref[c]
            m_new = jnp.maximum(m_old, jnp.max(s, axis=-1, keepdims=True))
            p = jnp.exp(s - m_new)
            alpha = jnp.exp(m_old - m_new)
            l_ref[c] = alpha * l_ref[c] + jnp.sum(p, axis=-1, keepdims=True)
            acc_ref[c] = alpha * acc_ref[c] + _dot(p.astype(BF16), vs)
            m_ref[c] = m_new

    def off_diag(j, carry):
        block(j, False)
        return carry

    lax.fori_loop(0, qi, off_diag, 0)
    block(qi, True)

    lam = lam_ref[0]
    o = acc_ref[0] / l_ref[0] - lam * (acc_ref[1] / l_ref[1])
    o_ref[...] = (_rms(o) * g_ref[...]).astype(o_ref.dtype)


def _diff_attn(proj, lam, gain, *, batch, seq, tq):
    t = batch * seq
    nq = seq // tq
    width = 2 * DA_QK_DIM
    k_blk0 = DA_HEADS
    v_blk0 = 2 * DA_HEADS
    kern = functools.partial(_diff_attn_kernel, tq=tq)
    return pl.pallas_call(
        kern,
        grid_spec=pltpu.PrefetchScalarGridSpec(
            num_scalar_prefetch=1,
            grid=(batch, DA_HEADS, nq),
            in_specs=[
                pl.BlockSpec((tq, width), lambda b, h, i, lam: (b * nq + i, h)),
                pl.BlockSpec((seq, width), lambda b, h, i, lam: (b, k_blk0 + h)),
                pl.BlockSpec((seq, width), lambda b, h, i, lam: (b, v_blk0 + h)),
                pl.BlockSpec((1, DA_V_DIM), lambda b, h, i, lam: (0, 0)),
            ],
            out_specs=pl.BlockSpec((tq, DA_V_DIM), lambda b, h, i, lam: (b * nq + i, h)),
            scratch_shapes=[
                pltpu.VMEM((2, tq, 1), F32),
                pltpu.VMEM((2, tq, 1), F32),
                pltpu.VMEM((2, tq, DA_V_DIM), F32),
            ],
        ),
        out_shape=jax.ShapeDtypeStruct((t, DA_HEADS * DA_V_DIM), BF16),
        compiler_params=pltpu.CompilerParams(
            dimension_semantics=("parallel", "parallel", "arbitrary"), vmem_limit_bytes=VMEM_LIMIT),
        name="diff_attn",
    )(lam, proj, proj, proj, gain)


def _s5_kernel(u_ref, bd_ref, ar_ref, ai_ref, cd_ref, d_ref, o_ref, xs_ref, *, seq, rows):
    ns = SSM_CHUNK_STATE
    seg_len = seq // SCAN_SEGS
    n_row_blk = seq // rows

    def in_map(r, carry):
        r0 = pl.multiple_of(r * rows, rows)
        xs_ref[pl.ds(r0, rows), :] = _dot(u_ref[pl.ds(r0, rows), :], bd_ref[...])
        return carry

    lax.fori_loop(0, n_row_blk, in_map, 0)

    ar = jnp.broadcast_to(ar_ref[...], (SCAN_SEGS, ns))
    ai = jnp.broadcast_to(ai_ref[...], (SCAN_SEGS, ns))

    def advance(t, sr, si):
        t0 = pl.multiple_of(t * SCAN_SEGS, SCAN_SEGS)
        br = xs_ref[pl.ds(t0, SCAN_SEGS), 0:ns]
        bi = xs_ref[pl.ds(t0, SCAN_SEGS), ns:2 * ns]
        return t0, ar * sr - ai * si + br, ar * si + ai * sr + bi

    def local_step(t, carry):
        _, nr, ni = advance(t, *carry)
        return nr, ni

    zero = jnp.zeros((SCAN_SEGS, ns), F32)
    fr, fi = lax.fori_loop(0, seg_len, local_step, (zero, zero))

    pr, pi = ar, ai
    for _ in range(int(math.log2(seg_len))):
        pr, pi = pr * pr - pi * pi, 2.0 * pr * pi
    seg = lax.broadcasted_iota(jnp.int32, (SCAN_SEGS, ns), 0)

    def shifted(x, k):
        return jnp.where(seg >= k, pltpu.roll(x, k, 0), 0.0)

    k = 1
    while k < SCAN_SEGS:
        gr, gi = shifted(fr, k), shifted(fi, k)
        fr, fi = fr + pr * gr - pi * gi, fi + pr * gi + pi * gr
        pr, pi = pr * pr - pi * pi, 2.0 * pr * pi
        k *= 2
    sr0, si0 = shifted(fr, 1), shifted(fi, 1)

    def global_step(t, carry):
        t0, nr, ni = advance(t, *carry)
        xs_ref[pl.ds(t0, SCAN_SEGS), 0:ns] = nr
        xs_ref[pl.ds(t0, SCAN_SEGS), ns:2 * ns] = ni
        return nr, ni

    lax.fori_loop(0, seg_len, global_step, (sr0, si0))

    def out_map(r, carry):
        r0 = pl.multiple_of(r * rows, rows)
        x = xs_ref[pl.ds(r0, rows), :].astype(BF16)
        y = _dot(x, cd_ref[...]) + d_ref[...] * u_ref[pl.ds(r0, rows), :].astype(F32)
        o_ref[pl.ds(r0, rows), :] = jax.nn.gelu(y).astype(o_ref.dtype)
        return carry

    lax.fori_loop(0, n_row_blk, out_map, 0)


def _s5(u_perm, bd, a_re, a_im, cd, d_skip, *, rows=256):
    batch, seq, d_ssm = u_perm.shape
    n_chunks = d_ssm // LANES
    kern = functools.partial(_s5_kernel, seq=seq, rows=rows)
    return pl.pallas_call(
        kern,
        grid=(batch, n_chunks),
        in_specs=[
            pl.BlockSpec((None, seq, LANES), lambda b, c: (b, 0, c)),
            pl.BlockSpec((None, LANES, 2 * SSM_CHUNK_STATE), lambda b, c: (c, 0, 0)),
            pl.BlockSpec((None, 1, SSM_CHUNK_STATE), lambda b, c: (c, 0, 0)),
            pl.BlockSpec((None, 1, SSM_CHUNK_STATE), lambda b, c: (c, 0, 0)),
            pl.BlockSpec((None, 2 * SSM_CHUNK_STATE, LANES), lambda b, c: (c, 0, 0)),
            pl.BlockSpec((None, 1, LANES), lambda b, c: (c, 0, 0)),
        ],
        out_specs=pl.BlockSpec((None, seq, LANES), lambda b, c: (b, 0, c)),
        out_shape=jax.ShapeDtypeStruct((batch, seq, d_ssm), BF16),
        scratch_shapes=[pltpu.VMEM((seq, 2 * SSM_CHUNK_STATE), F32)],
        compiler_params=pltpu.CompilerParams(
            dimension_semantics=("parallel", "parallel"), vmem_limit_bytes=VMEM_LIMIT),
        name="s5_scan",
    )(u_perm, bd, a_re, a_im, cd, d_skip)


def _s5_params(lam_re, lam_im, log_dt, b_re, b_im, c_re, c_im, d_skip):
    g = lam_re.shape[0]
    nc = g // SSM_CHUNK_GROUPS
    lr = jnp.minimum(lam_re.astype(F32), -1e-4)
    li = lam_im.astype(F32)
    dt = jnp.exp(log_dt.astype(F32))[:, None]
    mag = jnp.exp(lr * dt)
    lb_re, lb_im = mag * jnp.cos(li * dt), mag * jnp.sin(li * dt)
    den = lr * lr + li * li
    coef_re = ((lb_re - 1.0) * lr + lb_im * li) / den
    coef_im = (lb_im * lr - (lb_re - 1.0) * li) / den
    br, bi = b_re.astype(F32), b_im.astype(F32)
    bb_re = coef_re[..., None] * br - coef_im[..., None] * bi
    bb_im = coef_re[..., None] * bi + coef_im[..., None] * br
    eye = jnp.eye(SSM_CHUNK_GROUPS, dtype=F32)

    def pack_in(bb):
        bb = bb.reshape(nc, SSM_CHUNK_GROUPS, SSM_STATE, SSM_GROUP)
        return jnp.einsum('cgph,gk->cghkp', bb, eye).reshape(nc, LANES, SSM_CHUNK_STATE)

    def pack_out(cc):
        cc = cc.astype(F32).reshape(nc, SSM_CHUNK_GROUPS, SSM_GROUP, SSM_STATE)
        return jnp.einsum('cghp,gk->ckpgh', cc, eye).reshape(nc, SSM_CHUNK_STATE, LANES)

    bd = jnp.concatenate([pack_in(bb_re), pack_in(bb_im)], axis=-1).astype(BF16)
    cd = jnp.concatenate([pack_out(c_re), -pack_out(c_im)], axis=1).astype(BF16)
    a_re = lb_re.reshape(nc, 1, SSM_CHUNK_STATE)
    a_im = lb_im.reshape(nc, 1, SSM_CHUNK_STATE)
    dd = d_skip.astype(F32).reshape(nc, 1, LANES)
    return bd, a_re, a_im, cd, dd


def _mix_out_kernel(a_ref, y_ref, x_ref, gw_ref, gb_ref, nw_ref, wo_ref, o_ref):
    d_attn = a_ref.shape[1]
    y = y_ref[...]
    gate = _dot(y, gw_ref[...]) + gb_ref[...]
    s = y.astype(F32) * jax.nn.sigmoid(gate)
    sn = (_rms(s) * nw_ref[...]).astype(BF16)
    acc = _dot(a_ref[...], wo_ref[0:d_attn, :]) + _dot(sn, wo_ref[d_attn:, :])
    o_ref[...] = x_ref[...] + acc


def _mix_out(a, y, x, glu_w, glu_b, norm_w, w_out, *, tm):
    t, d = x.shape
    d_attn, d_ssm = a.shape[1], y.shape[1]
    const = lambda i: (0, 0)
    return pl.pallas_call(
        _mix_out_kernel,
        grid=(t // tm,),
        in_specs=[
            pl.BlockSpec((tm, d_attn), lambda i: (i, 0)),
            pl.BlockSpec((tm, d_ssm), lambda i: (i, 0)),
            pl.BlockSpec((tm, d), lambda i: (i, 0)),
            pl.BlockSpec((d_ssm, d_ssm), const),
            pl.BlockSpec((1, d_ssm), const),
            pl.BlockSpec((1, d_ssm), const),
            pl.BlockSpec((d, d), const),
        ],
        out_specs=pl.BlockSpec((tm, d), lambda i: (i, 0)),
        out_shape=jax.ShapeDtypeStruct((t, d), F32),
        compiler_params=pltpu.CompilerParams(
            dimension_semantics=("parallel",), vmem_limit_bytes=VMEM_LIMIT),
        name="mix_out",
    )(a, y, x, glu_w, glu_b.reshape(1, d_ssm).astype(F32), norm_w.reshape(1, d_ssm).astype(F32), w_out)


def _xattn_route_kernel(q_ref, k_ref, v_ref, h_ref, xo_ref, nw_ref, rhi_ref, rlo_ref, rb_ref,
                        h2_ref, hn_ref, eid_ref, wts_ref):
    d = h_ref.shape[1]
    hd = d // X_HEADS
    parts = []
    for h in range(X_HEADS):
        sl = slice(h * hd, (h + 1) * hd)
        s = _dot_nt(q_ref[:, sl], k_ref[:, sl])
        p = jnp.exp(s - jnp.max(s, axis=-1, keepdims=True))
        p = p / jnp.sum(p, axis=-1, keepdims=True)
        parts.append(_dot(p.astype(BF16), v_ref[:, sl]))
    o = jnp.concatenate(parts, axis=-1).astype(BF16)
    h2 = h_ref[...] + _dot(o, xo_ref[...])
    h2_ref[...] = h2
    hn = _rms(h2) * nw_ref[...]
    hn_ref[...] = hn

    hi = hn.astype(BF16)
    lo = (hn - hi.astype(F32)).astype(BF16)
    logits = (_dot(hi, rhi_ref[...]) + _dot(hi, rlo_ref[...]) + _dot(lo, rhi_ref[...])) + rb_ref[...]

    lane = lax.broadcasted_iota(jnp.int32, logits.shape, 1)
    big = jnp.int32(ROUTE_LANES)

    def first_lane(cond):
        return jnp.min(jnp.where(cond, lane, big), axis=-1, keepdims=True)

    c_mask = lane < MOE_GROUPS
    lc = jnp.where(c_mask, logits, NEG)
    mc = jnp.max(lc, axis=-1, keepdims=True)
    ec = jnp.exp(lc - mc)
    p_c = ec / jnp.sum(ec, axis=-1, keepdims=True)
    p_grp = jnp.max(p_c, axis=-1, keepdims=True)
    grp = first_lane(c_mask & (p_c == p_grp))
    f_lo = MOE_GROUPS + grp * EXP_PER_GROUP
    f_mask = (lane >= f_lo) & (lane < f_lo + EXP_PER_GROUP)
    lf = jnp.where(f_mask, logits, NEG)
    mf = jnp.max(lf, axis=-1, keepdims=True)
    ef = jnp.exp(lf - mf)
    pf = ef / jnp.sum(ef, axis=-1, keepdims=True)
    v1 = jnp.max(jnp.where(f_mask, pf, -1.0), axis=-1, keepdims=True)
    i1 = first_lane(f_mask & (pf == v1))
    rest = f_mask & (lane != i1)
    v2 = jnp.max(jnp.where(rest, pf, -1.0), axis=-1, keepdims=True)
    i2 = first_lane(rest & (pf == v2))
    tot = v1 + v2
    w1 = v1 / tot * p_grp
    w2 = v2 / tot * p_grp
    eid_ref[...] = jnp.where(lane == 0, i1 - MOE_GROUPS, jnp.where(lane == 1, i2 - MOE_GROUPS, 0))
    wts_ref[...] = jnp.where(lane == 0, w1, jnp.where(lane == 1, w2, 0.0))


def _xattn_route(q, kv, h1, xo_w, norm_w, r_hi, r_lo, r_b, *, batch, seq, mem_len, tm):
    t, d = h1.shape
    n = seq // tm
    const = lambda b, i: (0, 0)
    row = lambda b, i: (b * n + i, 0)
    return pl.pallas_call(
        _xattn_route_kernel,
        grid=(batch, n),
        in_specs=[
            pl.BlockSpec((tm, d), row),
            pl.BlockSpec((mem_len, d), lambda b, i: (b, 0)),
            pl.BlockSpec((mem_len, d), lambda b, i: (b, 1)),
            pl.BlockSpec((tm, d), row),
            pl.BlockSpec((d, d), const),
            pl.BlockSpec((1, d), const),
            pl.BlockSpec((d, ROUTE_LANES), const),
            pl.BlockSpec((d, ROUTE_LANES), const),
            pl.BlockSpec((1, ROUTE_LANES), const),
        ],
        out_specs=[
            pl.BlockSpec((tm, d), row),
            pl.BlockSpec((tm, d), row),
            pl.BlockSpec((tm, ROUTE_LANES), row),
            pl.BlockSpec((tm, ROUTE_LANES), row),
        ],
        out_shape=[
            jax.ShapeDtypeStruct((t, d), F32),
            jax.ShapeDtypeStruct((t, d), F32),
            jax.ShapeDtypeStruct((t, ROUTE_LANES), jnp.int32),
            jax.ShapeDtypeStruct((t, ROUTE_LANES), F32),
        ],
        compiler_params=pltpu.CompilerParams(
            dimension_semantics=("parallel", "parallel"), vmem_limit_bytes=VMEM_LIMIT),
        name="xattn_route",
    )(q, kv, kv, h1, xo_w, norm_w.reshape(1, d).astype(F32), r_hi, r_lo, r_b)


def _moe_kernel(be_ref, rt_ref, nu_ref, hn_hbm, wg_ref, wu_ref, wd_ref, o_ref, xbuf, sem):
    b = pl.program_id(0)
    n_used = nu_ref[0]

    def start_gather(blk, slot):
        def body(i, carry):
            tok = rt_ref[blk * MOE_BLOCK + i]
            pltpu.make_async_copy(hn_hbm.at[pl.ds(tok, 1)], xbuf.at[slot, pl.ds(i, 1)],
                                  sem.at[slot]).start()
            return carry
        lax.fori_loop(0, MOE_BLOCK, body, 0)

    @pl.when(b == 0)
    def _():
        start_gather(0, 0)

    @pl.when(b + 1 < n_used)
    def _():
        start_gather(b + 1, (b + 1) % 2)

    @pl.when(b < n_used)
    def _():
        slot = b % 2
        pltpu.make_async_copy(hn_hbm.at[pl.ds(0, MOE_BLOCK)], xbuf.at[slot], sem.at[slot]).wait()
        x = xbuf[slot].astype(BF16)
        g = _dot(x, wg_ref[...].astype(BF16))
        u = _dot(x, wu_ref[...].astype(BF16))
        mid = (jax.nn.silu(g) * u).astype(BF16)
        o_ref[...] = _dot(mid, wd_ref[...].astype(BF16))

    @pl.when(b >= n_used)
    def _():
        o_ref[...] = jnp.zeros(o_ref.shape, o_ref.dtype)


def _moe_experts(hn, w_gate, w_up, w_down, blk_exp, row_tok, n_used):
    t, d = hn.shape
    n_rows = row_tok.shape[0]
    n_blocks = n_rows // MOE_BLOCK
    d_ff = w_gate.shape[2]
    w_in_map = lambda b, be, rt, nu: (be[b], 0, 0)
    return pl.pallas_call(
        _moe_kernel,
        grid_spec=pltpu.PrefetchScalarGridSpec(
            num_scalar_prefetch=3,
            grid=(n_blocks,),
            in_specs=[
                pl.BlockSpec(memory_space=pl.ANY),
                pl.BlockSpec((None, d, d_ff), w_in_map),
                pl.BlockSpec((None, d, d_ff), w_in_map),
                pl.BlockSpec((None, d_ff, d), w_in_map),
            ],
            out_specs=pl.BlockSpec((MOE_BLOCK, d), lambda b, be, rt, nu: (b, 0)),
            scratch_shapes=[
                pltpu.VMEM((2, MOE_BLOCK, d), F32),
                pltpu.SemaphoreType.DMA((2,)),
            ],
        ),
        out_shape=jax.ShapeDtypeStruct((n_rows, d), F32),
        compiler_params=pltpu.CompilerParams(
            dimension_semantics=("arbitrary",), vmem_limit_bytes=VMEM_LIMIT),
        name="moe_experts",
    )(blk_exp, row_tok, n_used, hn, w_gate, w_up, w_down)


def _combine_kernel(pos_ref, ys_hbm, h_ref, w_ref, o_ref, ybuf, sem, *, tm):
    i = pl.program_id(0)
    n = pl.num_programs(0)

    def start_gather(tile, slot):
        def body(r, carry):
            base = (tile * tm + r) * TOP_K_FINE
            for k in range(TOP_K_FINE):
                pltpu.make_async_copy(ys_hbm.at[pl.ds(pos_ref[base + k], 1)],
                                      ybuf.at[slot, k, pl.ds(r, 1)], sem.at[slot]).start()
            return carry
        lax.fori_loop(0, tm, body, 0)

    @pl.when(i == 0)
    def _():
        start_gather(0, 0)

    @pl.when(i + 1 < n)
    def _():
        start_gather(i + 1, (i + 1) % 2)

    slot = i % 2
    for k in range(TOP_K_FINE):
        pltpu.make_async_copy(ys_hbm.at[pl.ds(0, tm)], ybuf.at[slot, k], sem.at[slot]).wait()
    w = w_ref[...]
    moe = w[:, 0:1] * ybuf[slot, 0] + w[:, 1:2] * ybuf[slot, 1]
    o_ref[...] = h_ref[...] + moe


def _combine(ys, h2, wts, pos, *, tm):
    t, d = h2.shape
    kern = functools.partial(_combine_kernel, tm=tm)
    return pl.pallas_call(
        kern,
        grid_spec=pltpu.PrefetchScalarGridSpec(
            num_scalar_prefetch=1,
            grid=(t // tm,),
            in_specs=[
                pl.BlockSpec(memory_space=pl.ANY),
                pl.BlockSpec((tm, d), lambda i, pos: (i, 0)),
                pl.BlockSpec((tm, ROUTE_LANES), lambda i, pos: (i, 0)),
            ],
            out_specs=pl.BlockSpec((tm, d), lambda i, pos: (i, 0)),
            scratch_shapes=[
                pltpu.VMEM((2, TOP_K_FINE, tm, d), F32),
                pltpu.SemaphoreType.DMA((2,)),
            ],
        ),
        out_shape=jax.ShapeDtypeStruct((t, d), F32),
        compiler_params=pltpu.CompilerParams(
            dimension_semantics=("arbitrary",), vmem_limit_bytes=VMEM_LIMIT),
        name="moe_combine",
    )(pos, ys, h2, wts)


def _dispatch(eid, n_tokens):
    n_assign = n_tokens * TOP_K_FINE
    e_flat = eid.reshape(n_assign)
    order = jnp.argsort(e_flat)
    e_s = e_flat[order]
    t_s = (order // TOP_K_FINE).astype(jnp.int32)
    counts = jnp.bincount(e_flat, length=N_EXPERTS).astype(jnp.int32)
    starts = jnp.cumsum(counts) - counts
    padded = (counts + MOE_BLOCK - 1) // MOE_BLOCK * MOE_BLOCK
    pad_end = jnp.cumsum(padded)
    pad_start = pad_end - padded
    dest = pad_start[e_s] + jnp.arange(n_assign, dtype=jnp.int32) - starts[e_s]
    n_rows = (n_assign + N_EXPERTS * (MOE_BLOCK - 1) + MOE_BLOCK - 1) // MOE_BLOCK * MOE_BLOCK
    n_blocks = n_rows // MOE_BLOCK
    row_tok = jnp.zeros((n_rows,), jnp.int32).at[dest].set(t_s)
    blk_exp = jnp.minimum(
        jnp.searchsorted(pad_end, jnp.arange(n_blocks, dtype=jnp.int32) * MOE_BLOCK, side='right'),
        N_EXPERTS - 1).astype(jnp.int32)
    pos = jnp.zeros((n_assign,), jnp.int32).at[order].set(dest.astype(jnp.int32))
    n_used = (pad_end[-1] // MOE_BLOCK).astype(jnp.int32).reshape(1)
    return blk_exp, row_tok, n_used, pos


def kernel(x, mem, norm1_w, w_in, q_norm_w, k_norm_w, lambda_q1, lambda_k1, lambda_q2, lambda_k2, subln_w, ssm_lambda_re, ssm_lambda_im, ssm_log_dt, ssm_b_re, ssm_b_im, ssm_c_re, ssm_c_im, ssm_d, ssm_glu_w, ssm_glu_b, ssm_out_norm_w, w_out, norm2_w, mem_norm_w, xq_w, xkv_w, xq_norm_w, xk_norm_w, xo_w, norm3_w, router_coarse_w, router_coarse_b, router_fine_w, router_fine_b, expert_w_gate, expert_w_up, expert_w_down):
    batch, seq, d = x.shape
    mem_len = mem.shape[1]
    t = batch * seq
    depth = norm1_w.shape[0]
    d_attn = DA_HEADS * DA_V_DIM
    d_ssm = d - d_attn
    qk_cols = DA_HEADS * 2 * DA_QK_DIM
    x_hd = d // X_HEADS
    h = x.reshape(t, d)
    mem2 = mem.reshape(batch * mem_len, d)

    for l in range(depth):
        lam_init = 0.8 - 0.6 * math.exp(-0.3 * l)
        lam = (jnp.exp(jnp.sum(lambda_q1[l].astype(F32) * lambda_k1[l].astype(F32)))
               - jnp.exp(jnp.sum(lambda_q2[l].astype(F32) * lambda_k2[l].astype(F32)))
               + lam_init).reshape(1)

        n_rep = qk_cols // DA_QK_DIM
        in_gain = jnp.concatenate([
            jnp.tile(q_norm_w[l].astype(F32) * (DA_QK_DIM ** -0.5), n_rep),
            jnp.tile(k_norm_w[l].astype(F32), n_rep),
            jnp.ones((d_attn + d_ssm,), F32)])
        proj = _norm_matmul(h, norm1_w[l], w_in[l].astype(BF16), in_gain,
                            n_norm_cols=2 * qk_cols, chunk=DA_QK_DIM, tm=512, tn=512, name="in_proj")
        sub_gain = (subln_w[l].astype(F32) * (1.0 - lam_init)).reshape(1, DA_V_DIM)
        a = _diff_attn(proj, lam, sub_gain, batch=batch, seq=seq, tq=256)

        seg_len = seq // SCAN_SEGS
        u = proj[:, 2 * qk_cols + d_attn:]
        u_perm = u.reshape(batch, SCAN_SEGS, seg_len, d_ssm).transpose(0, 2, 1, 3).reshape(batch, seq, d_ssm)
        bd, a_re, a_im, cd, dd = _s5_params(ssm_lambda_re[l], ssm_lambda_im[l], ssm_log_dt[l],
                                            ssm_b_re[l], ssm_b_im[l], ssm_c_re[l], ssm_c_im[l], ssm_d[l])
        y_perm = _s5(u_perm, bd, a_re, a_im, cd, dd)
        y = y_perm.reshape(batch, seg_len, SCAN_SEGS, d_ssm).transpose(0, 2, 1, 3).reshape(t, d_ssm)
        h = _mix_out(a, y, h, ssm_glu_w[l].astype(BF16), ssm_glu_b[l], ssm_out_norm_w[l],
                     w_out[l].astype(BF16), tm=256)

        kv_gain = jnp.concatenate([jnp.tile(xk_norm_w[l].astype(F32), X_HEADS), jnp.ones((d,), F32)])
        kv = _norm_matmul(mem2, mem_norm_w[l], xkv_w[l].astype(BF16), kv_gain,
                          n_norm_cols=d, chunk=x_hd, tm=512, tn=512, name="kv_proj")
        q_gain = jnp.tile(xq_norm_w[l].astype(F32) * (x_hd ** -0.5), X_HEADS)
        q = _norm_matmul(h, norm2_w[l], xq_w[l].astype(BF16), q_gain,
                         n_norm_cols=d, chunk=x_hd, tm=512, tn=512, name="xq_proj")
        r_w = jnp.concatenate([router_coarse_w[l].astype(F32), router_fine_w[l].astype(F32)], axis=1)
        r_w = jnp.pad(r_w, ((0, 0), (0, ROUTE_LANES - r_w.shape[1])))
        r_hi = r_w.astype(BF16)
        r_lo = (r_w - r_hi.astype(F32)).astype(BF16)
        r_b = jnp.concatenate([router_coarse_b[l].astype(F32), router_fine_b[l].astype(F32)])
        r_b = jnp.pad(r_b, (0, ROUTE_LANES - r_b.shape[0])).reshape(1, ROUTE_LANES)
        h2, hn3, eid, wts = _xattn_route(q, kv, h, xo_w[l].astype(BF16), norm3_w[l], r_hi, r_lo, r_b,
                                         batch=batch, seq=seq, mem_len=mem_len, tm=256)

        blk_exp, row_tok, n_used, pos = _dispatch(eid[:, :TOP_K_FINE], t)
        ys = _moe_experts(hn3, expert_w_gate[l], expert_w_up[l], expert_w_down[l],
                          blk_exp, row_tok, n_used)
        h = _combine(ys, h2, wts, pos, tm=256)

    return h.reshape(batch, seq, d)
```

```python
import functools
import math

import jax
import jax.numpy as jnp
from jax import lax
from jax.experimental import pallas as pl
from jax.experimental.pallas import tpu as pltpu

F32 = jnp.float32
BF16 = jnp.bfloat16

EPS = 1e-6
DA_HEADS = 4
DA_QK_DIM = 128
DA_V_DIM = 256
SSM_GROUP = 16
SSM_STATE = 64
X_HEADS = 4
MOE_GROUPS = 8
EXP_PER_GROUP = 8
N_EXPERTS = MOE_GROUPS * EXP_PER_GROUP
TOP_K_FINE = 2

LANES = 128
SUBLANES = 8
VMEM_LIMIT = 56 * 1024 * 1024
NEG = -1e30
LOG2E = math.log2(math.e)

SSM_CHUNK_GROUPS = LANES // SSM_GROUP
SSM_CHUNK_STATE = SSM_CHUNK_GROUPS * SSM_STATE
SCAN_SEGS = SUBLANES
MOE_BLOCK = 256
ROUTE_LANES = LANES


def _rms(x, eps=EPS):
    return x * lax.rsqrt(jnp.mean(x * x, axis=-1, keepdims=True) + eps)


def _dot(a, b):
    return jnp.dot(a, b, preferred_element_type=F32)


def _dot_nt(a, b):
    return lax.dot_general(a, b, (((1,), (1,)), ((), ())), preferred_element_type=F32)


def _norm_matmul_kernel(x_ref, nw_ref, w_ref, g_ref, o_ref, xn_ref, *, n_norm_tiles, chunk):
    j = pl.program_id(1)

    @pl.when(j == 0)
    def _():
        x = x_ref[...].astype(F32)
        xn_ref[...] = (_rms(x) * nw_ref[...]).astype(BF16)

    acc = _dot(xn_ref[...], w_ref[...])
    tn = acc.shape[1]

    def normed():
        parts = [_rms(acc[:, c * chunk:(c + 1) * chunk]) for c in range(tn // chunk)]
        y = parts[0] if len(parts) == 1 else jnp.concatenate(parts, axis=-1)
        o_ref[...] = (y * g_ref[...]).astype(o_ref.dtype)

    def plain():
        o_ref[...] = acc.astype(o_ref.dtype)

    pl.when(j < n_norm_tiles)(normed)
    pl.when(j >= n_norm_tiles)(plain)


def _norm_matmul(x, norm_w, w_bf16, gain, *, n_norm_cols, chunk, tm, tn, name):
    m, k = x.shape
    n = w_bf16.shape[1]
    assert m % tm == 0 and n % tn == 0 and tn % chunk == 0 and n_norm_cols % tn == 0
    kern = functools.partial(_norm_matmul_kernel, n_norm_tiles=n_norm_cols // tn, chunk=chunk)
    return pl.pallas_call(
        kern,
        grid=(m // tm, n // tn),
        in_specs=[
            pl.BlockSpec((tm, k), lambda i, j: (i, 0)),
            pl.BlockSpec((1, k), lambda i, j: (0, 0)),
            pl.BlockSpec((k, tn), lambda i, j: (0, j)),
            pl.BlockSpec((1, tn), lambda i, j: (0, j)),
        ],
        out_specs=pl.BlockSpec((tm, tn), lambda i, j: (i, j)),
        out_shape=jax.ShapeDtypeStruct((m, n), BF16),
        scratch_shapes=[pltpu.VMEM((tm, k), BF16)],
        compiler_params=pltpu.CompilerParams(
            dimension_semantics=("parallel", "arbitrary"), vmem_limit_bytes=VMEM_LIMIT),
        name=name,
    )(x, norm_w.reshape(1, k).astype(F32), w_bf16, gain.reshape(1, n).astype(F32))


def _diff_attn_kernel(lam_ref, q_ref, k_ref, v_ref, g_ref, o_ref,
                      m1_ref, l1_ref, acc1_ref, m2_ref, l2_ref, acc2_ref, *, tq):
    qi = pl.program_id(2)
    stats = ((m1_ref, l1_ref, acc1_ref), (m2_ref, l2_ref, acc2_ref))
    for m_ref, l_ref, acc_ref in stats:
        m_ref[...] = jnp.full(m_ref.shape, NEG, F32)
        l_ref[...] = jnp.zeros(l_ref.shape, F32)
        acc_ref[...] = jnp.zeros(acc_ref.shape, F32)

    def block(j, masked):
        start = pl.multiple_of(j * tq, tq)
        for c, (m_ref, l_ref, acc_ref) in enumerate(stats):
            lo, hi = c * DA_QK_DIM, (c + 1) * DA_QK_DIM
            s = _dot_nt(q_ref[:, lo:hi], k_ref[pl.ds(start, tq), lo:hi])
            if masked:
                row = lax.broadcasted_iota(jnp.int32, s.shape, 0)
                col = lax.broadcasted_iota(jnp.int32, s.shape, 1)
                s = jnp.where(col <= row, s, NEG)
            m_old = m_ref[...]
            m_new = jnp.maximum(m_old, jnp.max(s, axis=-1, keepdims=True))
            p = jnp.exp2(s - m_new)
            alpha = jnp.exp2(m_old - m_new)
            l_ref[...] = alpha * l_ref[...] + jnp.sum(p, axis=-1, keepdims=True)
            acc_ref[...] = alpha * acc_ref[...] + _dot(p.astype(BF16), v_ref[pl.ds(start, tq), :])
            m_ref[...] = m_new

    def off_diag(j, carry):
        block(j, False)
        return carry

    lax.fori_loop(0, qi, off_diag, 0)
    block(qi, True)

    lam = lam_ref[0]
    o = acc1_ref[...] / l1_ref[...] - lam * (acc2_ref[...] / l2_ref[...])
    o_ref[...] = (_rms(o) * g_ref[...]).astype(o_ref.dtype)


def _diff_attn(proj, lam, gain, *, batch, seq, tq):
    t = batch * seq
    nq = seq // tq
    width = 2 * DA_QK_DIM
    k_blk0 = DA_HEADS
    v_blk0 = 2 * DA_HEADS
    kern = functools.partial(_diff_attn_kernel, tq=tq)
    return pl.pallas_call(
        kern,
        grid_spec=pltpu.PrefetchScalarGridSpec(
            num_scalar_prefetch=1,
            grid=(batch, DA_HEADS, nq),
            in_specs=[
                pl.BlockSpec((tq, width), lambda b, h, i, lam: (b * nq + i, h)),
                pl.BlockSpec((seq, width), lambda b, h, i, lam: (b, k_blk0 + h)),
                pl.BlockSpec((seq, width), lambda b, h, i, lam: (b, v_blk0 + h)),
                pl.BlockSpec((1, DA_V_DIM), lambda b, h, i, lam: (0, 0)),
            ],
            out_specs=pl.BlockSpec((tq, DA_V_DIM), lambda b, h, i, lam: (b * nq + i, h)),
            scratch_shapes=[
                pltpu.VMEM((tq, 1), F32), pltpu.VMEM((tq, 1), F32), pltpu.VMEM((tq, DA_V_DIM), F32),
                pltpu.VMEM((tq, 1), F32), pltpu.VMEM((tq, 1), F32), pltpu.VMEM((tq, DA_V_DIM), F32),
            ],
        ),
        out_shape=jax.ShapeDtypeStruct((t, DA_HEADS * DA_V_DIM), BF16),
        compiler_params=pltpu.CompilerParams(
            dimension_semantics=("parallel", "parallel", "arbitrary"), vmem_limit_bytes=VMEM_LIMIT),
        name="diff_attn",
    )(lam, proj, proj, proj, gain)


def _s5_kernel(u_ref, bd_ref, ar_ref, ai_ref, cd_ref, d_ref, o_ref, xs_ref, *, seq, rows):
    ns = SSM_CHUNK_STATE
    seg_len = seq // SCAN_SEGS
    n_row_blk = seq // rows

    def in_map(r, carry):
        r0 = pl.multiple_of(r * rows, rows)
        xs_ref[pl.ds(r0, rows), :] = _dot(u_ref[pl.ds(r0, rows), :], bd_ref[...])
        return carry

    lax.fori_loop(0, n_row_blk, in_map, 0)

    ar = jnp.broadcast_to(ar_ref[...], (SCAN_SEGS, ns))
    ai = jnp.broadcast_to(ai_ref[...], (SCAN_SEGS, ns))

    def advance(t, sr, si):
        t0 = pl.multiple_of(t * SCAN_SEGS, SCAN_SEGS)
        br = xs_ref[pl.ds(t0, SCAN_SEGS), 0:ns]
        bi = xs_ref[pl.ds(t0, SCAN_SEGS), ns:2 * ns]
        return t0, ar * sr - ai * si + br, ar * si + ai * sr + bi

    def local_step(t, carry):
        _, nr, ni = advance(t, *carry)
        return nr, ni

    zero = jnp.zeros((SCAN_SEGS, ns), F32)
    fr, fi = lax.fori_loop(0, seg_len, local_step, (zero, zero))

    pr, pi = ar, ai
    for _ in range(int(math.log2(seg_len))):
        pr, pi = pr * pr - pi * pi, 2.0 * pr * pi
    seg = lax.broadcasted_iota(jnp.int32, (SCAN_SEGS, ns), 0)

    def shifted(x, k):
        return jnp.where(seg >= k, pltpu.roll(x, k, 0), 0.0)

    k = 1
    while k < SCAN_SEGS:
        gr, gi = shifted(fr, k), shifted(fi, k)
        fr, fi = fr + pr * gr - pi * gi, fi + pr * gi + pi * gr
        pr, pi = pr * pr - pi * pi, 2.0 * pr * pi
        k *= 2
    sr0, si0 = shifted(fr, 1), shifted(fi, 1)

    def global_step(t, carry):
        t0, nr, ni = advance(t, *carry)
        xs_ref[pl.ds(t0, SCAN_SEGS), 0:ns] = nr
        xs_ref[pl.ds(t0, SCAN_SEGS), ns:2 * ns] = ni
        return nr, ni

    lax.fori_loop(0, seg_len, global_step, (sr0, si0))

    def out_map(r, carry):
        r0 = pl.multiple_of(r * rows, rows)
        x = xs_ref[pl.ds(r0, rows), :].astype(BF16)
        y = _dot(x, cd_ref[...]) + d_ref[...] * u_ref[pl.ds(r0, rows), :].astype(F32)
        o_ref[pl.ds(r0, rows), :] = jax.nn.gelu(y).astype(o_ref.dtype)
        return carry

    lax.fori_loop(0, n_row_blk, out_map, 0)


def _s5(u_perm, bd, a_re, a_im, cd, d_skip, *, rows=256):
    batch, seq, d_ssm = u_perm.shape
    n_chunks = d_ssm // LANES
    kern = functools.partial(_s5_kernel, seq=seq, rows=rows)
    return pl.pallas_call(
        kern,
        grid=(batch, n_chunks),
        in_specs=[
            pl.BlockSpec((None, seq, LANES), lambda b, c: (b, 0, c)),
            pl.BlockSpec((None, LANES, 2 * SSM_CHUNK_STATE), lambda b, c: (c, 0, 0)),
            pl.BlockSpec((None, 1, SSM_CHUNK_STATE), lambda b, c: (c, 0, 0)),
            pl.BlockSpec((None, 1, SSM_CHUNK_STATE), lambda b, c: (c, 0, 0)),
            pl.BlockSpec((None, 2 * SSM_CHUNK_STATE, LANES), lambda b, c: (c, 0, 0)),
            pl.BlockSpec((None, 1, LANES), lambda b, c: (c, 0, 0)),
        ],
        out_specs=pl.BlockSpec((None, seq, LANES), lambda b, c: (b, 0, c)),
        out_shape=jax.ShapeDtypeStruct((batch, seq, d_ssm), BF16),
        scratch_shapes=[pltpu.VMEM((seq, 2 * SSM_CHUNK_STATE), F32)],
        compiler_params=pltpu.CompilerParams(
            dimension_semantics=("parallel", "parallel"), vmem_limit_bytes=VMEM_LIMIT),
        name="s5_scan",
    )(u_perm, bd, a_re, a_im, cd, d_skip)


def _s5_params(lam_re, lam_im, log_dt, b_re, b_im, c_re, c_im, d_skip):
    g = lam_re.shape[0]
    nc = g // SSM_CHUNK_GROUPS
    lr = jnp.minimum(lam_re.astype(F32), -1e-4)
    li = lam_im.astype(F32)
    dt = jnp.exp(log_dt.astype(F32))[:, None]
    mag = jnp.exp(lr * dt)
    lb_re, lb_im = mag * jnp.cos(li * dt), mag * jnp.sin(li * dt)
    den = lr * lr + li * li
    coef_re = ((lb_re - 1.0) * lr + lb_im * li) / den
    coef_im = (lb_im * lr - (lb_re - 1.0) * li) / den
    br, bi = b_re.astype(F32), b_im.astype(F32)
    bb_re = coef_re[..., None] * br - coef_im[..., None] * bi
    bb_im = coef_re[..., None] * bi + coef_im[..., None] * br
    eye = jnp.eye(SSM_CHUNK_GROUPS, dtype=F32)

    def pack_in(bb):
        bb = bb.reshape(nc, SSM_CHUNK_GROUPS, SSM_STATE, SSM_GROUP)
        return jnp.einsum('cgph,gk->cghkp', bb, eye).reshape(nc, LANES, SSM_CHUNK_STATE)

    def pack_out(cc):
        cc = cc.astype(F32).reshape(nc, SSM_CHUNK_GROUPS, SSM_GROUP, SSM_STATE)
        return jnp.einsum('cghp,gk->ckpgh', cc, eye).reshape(nc, SSM_CHUNK_STATE, LANES)

    bd = jnp.concatenate([pack_in(bb_re), pack_in(bb_im)], axis=-1).astype(BF16)
    cd = jnp.concatenate([pack_out(c_re), -pack_out(c_im)], axis=1).astype(BF16)
    a_re = lb_re.reshape(nc, 1, SSM_CHUNK_STATE)
    a_im = lb_im.reshape(nc, 1, SSM_CHUNK_STATE)
    dd = d_skip.astype(F32).reshape(nc, 1, LANES)
    return bd, a_re, a_im, cd, dd


def _mix_out_kernel(a_ref, y_ref, x_ref, gw_ref, gb_ref, nw_ref, wo_ref, o_ref):
    d_attn = a_ref.shape[1]
    y = y_ref[...]
    gate = _dot(y, gw_ref[...]) + gb_ref[...]
    s = y.astype(F32) * jax.nn.sigmoid(gate)
    sn = (_rms(s) * nw_ref[...]).astype(BF16)
    acc = _dot(a_ref[...], wo_ref[0:d_attn, :]) + _dot(sn, wo_ref[d_attn:, :])
    o_ref[...] = x_ref[...] + acc


def _mix_out(a, y, x, glu_w, glu_b, norm_w, w_out, *, tm):
    t, d = x.shape
    d_attn, d_ssm = a.shape[1], y.shape[1]
    const = lambda i: (0, 0)
    return pl.pallas_call(
        _mix_out_kernel,
        grid=(t // tm,),
        in_specs=[
            pl.BlockSpec((tm, d_attn), lambda i: (i, 0)),
            pl.BlockSpec((tm, d_ssm), lambda i: (i, 0)),
            pl.BlockSpec((tm, d), lambda i: (i, 0)),
            pl.BlockSpec((d_ssm, d_ssm), const),
            pl.BlockSpec((1, d_ssm), const),
            pl.BlockSpec((1, d_ssm), const),
            pl.BlockSpec((d, d), const),
        ],
        out_specs=pl.BlockSpec((tm, d), lambda i: (i, 0)),
        out_shape=jax.ShapeDtypeStruct((t, d), F32),
        compiler_params=pltpu.CompilerParams(
            dimension_semantics=("parallel",), vmem_limit_bytes=VMEM_LIMIT),
        name="mix_out",
    )(a, y, x, glu_w, glu_b.reshape(1, d_ssm).astype(F32), norm_w.reshape(1, d_ssm).astype(F32), w_out)


def _xattn_route_kernel(q_ref, k_ref, v_ref, h_ref, xo_ref, nw_ref, rhi_ref, rlo_ref, rb_ref,
                        h2_ref, hn_ref, eid_ref, wts_ref):
    d = h_ref.shape[1]
    hd = d // X_HEADS
    parts = []
    for h in range(X_HEADS):
        sl = slice(h * hd, (h + 1) * hd)
        s = _dot_nt(q_ref[:, sl], k_ref[:, sl])
        p = jnp.exp(s - jnp.max(s, axis=-1, keepdims=True))
        p = p / jnp.sum(p, axis=-1, keepdims=True)
        parts.append(_dot(p.astype(BF16), v_ref[:, sl]))
    o = jnp.concatenate(parts, axis=-1).astype(BF16)
    h2 = h_ref[...] + _dot(o, xo_ref[...])
    h2_ref[...] = h2
    hn = _rms(h2) * nw_ref[...]
    hn_ref[...] = hn

    hi = hn.astype(BF16)
    lo = (hn - hi.astype(F32)).astype(BF16)
    logits = (_dot(hi, rhi_ref[...]) + _dot(hi, rlo_ref[...]) + _dot(lo, rhi_ref[...])) + rb_ref[...]

    lane = lax.broadcasted_iota(jnp.int32, logits.shape, 1)
    big = jnp.int32(ROUTE_LANES)

    def first_lane(cond):
        return jnp.min(jnp.where(cond, lane, big), axis=-1, keepdims=True)

    c_mask = lane < MOE_GROUPS
    lc = jnp.where(c_mask, logits, NEG)
    mc = jnp.max(lc, axis=-1, keepdims=True)
    ec = jnp.exp(lc - mc)
    p_c = ec / jnp.sum(ec, axis=-1, keepdims=True)
    p_grp = jnp.max(p_c, axis=-1, keepdims=True)
    grp = first_lane(c_mask & (p_c == p_grp))
    f_lo = MOE_GROUPS + grp * EXP_PER_GROUP
    f_mask = (lane >= f_lo) & (lane < f_lo + EXP_PER_GROUP)
    lf = jnp.where(f_mask, logits, NEG)
    mf = jnp.max(lf, axis=-1, keepdims=True)
    ef = jnp.exp(lf - mf)
    pf = ef / jnp.sum(ef, axis=-1, keepdims=True)
    v1 = jnp.max(jnp.where(f_mask, pf, -1.0), axis=-1, keepdims=True)
    i1 = first_lane(f_mask & (pf == v1))
    rest = f_mask & (lane != i1)
    v2 = jnp.max(jnp.where(rest, pf, -1.0), axis=-1, keepdims=True)
    i2 = first_lane(rest & (pf == v2))
    tot = v1 + v2
    w1 = v1 / tot * p_grp
    w2 = v2 / tot * p_grp
    eid_ref[...] = jnp.where(lane == 0, i1 - MOE_GROUPS, jnp.where(lane == 1, i2 - MOE_GROUPS, 0))
    wts_ref[...] = jnp.where(lane == 0, w1, jnp.where(lane == 1, w2, 0.0))


def _xattn_route(q, kv, h1, xo_w, norm_w, r_hi, r_lo, r_b, *, batch, seq, mem_len, tm):
    t, d = h1.shape
    n = seq // tm
    const = lambda b, i: (0, 0)
    row = lambda b, i: (b * n + i, 0)
    return pl.pallas_call(
        _xattn_route_kernel,
        grid=(batch, n),
        in_specs=[
            pl.BlockSpec((tm, d), row),
            pl.BlockSpec((mem_len, d), lambda b, i: (b, 0)),
            pl.BlockSpec((mem_len, d), lambda b, i: (b, 1)),
            pl.BlockSpec((tm, d), row),
            pl.BlockSpec((d, d), const),
            pl.BlockSpec((1, d), const),
            pl.BlockSpec((d, ROUTE_LANES), const),
            pl.BlockSpec((d, ROUTE_LANES), const),
            pl.BlockSpec((1, ROUTE_LANES), const),
        ],
        out_specs=[
            pl.BlockSpec((tm, d), row),
            pl.BlockSpec((tm, d), row),
            pl.BlockSpec((tm, ROUTE_LANES), row),
            pl.BlockSpec((tm, ROUTE_LANES), row),
        ],
        out_shape=[
            jax.ShapeDtypeStruct((t, d), F32),
            jax.ShapeDtypeStruct((t, d), F32),
            jax.ShapeDtypeStruct((t, ROUTE_LANES), jnp.int32),
            jax.ShapeDtypeStruct((t, ROUTE_LANES), F32),
        ],
        compiler_params=pltpu.CompilerParams(
            dimension_semantics=("parallel", "parallel"), vmem_limit_bytes=VMEM_LIMIT),
        name="xattn_route",
    )(q, kv, kv, h1, xo_w, norm_w.reshape(1, d).astype(F32), r_hi, r_lo, r_b)


def _moe_kernel(be_ref, par_ref, first_ref, nxt_ref, base_ref, nval_ref, ord_ref, nu_ref,
                hn_hbm, wg_hbm, wu_hbm, wd_hbm, y_hbm,
                xbuf, ybuf, wgb, wub, wdb, gsem, ssem, wsem, *, n_tokens):
    b = pl.program_id(0)
    n_used = nu_ref[0]

    def weight_copies(e, slot):
        return (pltpu.make_async_copy(wg_hbm.at[e], wgb.at[slot], wsem.at[slot]),
                pltpu.make_async_copy(wu_hbm.at[e], wub.at[slot], wsem.at[slot]),
                pltpu.make_async_copy(wd_hbm.at[e], wdb.at[slot], wsem.at[slot]))

    def start_gather(blk, slot):
        base = base_ref[blk]

        def body(i, carry):
            tok = lax.shift_right_logical(ord_ref[base + i], 1)
            pltpu.make_async_copy(hn_hbm.at[pl.ds(tok, 1)], xbuf.at[slot, pl.ds(i, 1)],
                                  gsem.at[slot]).start()
            return carry
        lax.fori_loop(0, nval_ref[blk], body, 0)

    def wait_rows(n, copy_of):
        n8 = pl.multiple_of((n // SUBLANES) * SUBLANES, SUBLANES)

        @pl.when(n8 > 0)
        def _():
            copy_of(n8).wait()

        def one(i, carry):
            copy_of(1).wait()
            return carry
        lax.fori_loop(0, n - n8, one, 0)

    def wait_gather(blk, slot):
        wait_rows(nval_ref[blk], lambda n: pltpu.make_async_copy(
            hn_hbm.at[pl.ds(0, n)], xbuf.at[slot, pl.ds(0, n)], gsem.at[slot]))

    def start_scatter(blk, slot):
        base = base_ref[blk]

        def body(i, carry):
            a = ord_ref[base + i]
            row = (a & 1) * n_tokens + lax.shift_right_logical(a, 1)
            pltpu.make_async_copy(ybuf.at[slot, pl.ds(i, 1)], y_hbm.at[pl.ds(row, 1)],
                                  ssem.at[slot]).start()
            return carry
        lax.fori_loop(0, nval_ref[blk], body, 0)

    def wait_scatter(blk, slot):
        wait_rows(nval_ref[blk], lambda n: pltpu.make_async_copy(
            ybuf.at[slot, pl.ds(0, n)], y_hbm.at[pl.ds(0, n)], ssem.at[slot]))

    @pl.when(b == 0)
    def _():
        xbuf[...] = jnp.zeros(xbuf.shape, xbuf.dtype)
        for c in weight_copies(be_ref[0], par_ref[0]):
            c.start()
        start_gather(0, 0)

    @pl.when(b < n_used)
    def _():
        slot = b % 2
        wslot = par_ref[b]
        is_first = first_ref[b] == 1

        @pl.when(is_first & (nxt_ref[b] >= 0))
        def _():
            for c in weight_copies(nxt_ref[b], 1 - wslot):
                c.start()

        @pl.when(b + 1 < n_used)
        def _():
            start_gather(b + 1, 1 - slot)

        @pl.when(b >= 2)
        def _():
            wait_scatter(b - 2, slot)

        @pl.when(is_first)
        def _():
            for c in weight_copies(0, wslot):
                c.wait()

        wait_gather(b, slot)
        x = xbuf[slot].astype(BF16)
        g = _dot(x, wgb[wslot].astype(BF16))
        u = _dot(x, wub[wslot].astype(BF16))
        mid = (jax.nn.silu(g) * u).astype(BF16)
        ybuf[slot] = _dot(mid, wdb[wslot].astype(BF16))
        start_scatter(b, slot)

        @pl.when(b == n_used - 1)
        def _():
            wait_scatter(b, slot)

            @pl.when(b >= 1)
            def _():
                wait_scatter(b - 1, 1 - slot)


def _moe_experts(hn, w_gate, w_up, w_down, meta, order, n_used):
    t, d = hn.shape
    d_ff = w_gate.shape[2]
    blk_exp, par, first, nxt, base, nval = meta
    n_blocks = blk_exp.shape[0]
    kern = functools.partial(_moe_kernel, n_tokens=t)
    any_spec = pl.BlockSpec(memory_space=pl.ANY)
    return pl.pallas_call(
        kern,
        grid_spec=pltpu.PrefetchScalarGridSpec(
            num_scalar_prefetch=8,
            grid=(n_blocks,),
            in_specs=[any_spec, any_spec, any_spec, any_spec],
            out_specs=any_spec,
            scratch_shapes=[
                pltpu.VMEM((2, MOE_BLOCK, d), F32),
                pltpu.VMEM((2, MOE_BLOCK, d), F32),
                pltpu.VMEM((2, d, d_ff), F32),
                pltpu.VMEM((2, d, d_ff), F32),
                pltpu.VMEM((2, d_ff, d), F32),
                pltpu.SemaphoreType.DMA((2,)),
                pltpu.SemaphoreType.DMA((2,)),
                pltpu.SemaphoreType.DMA((2,)),
            ],
        ),
        out_shape=jax.ShapeDtypeStruct((TOP_K_FINE * t, d), F32),
        compiler_params=pltpu.CompilerParams(
            dimension_semantics=("arbitrary",), vmem_limit_bytes=VMEM_LIMIT),
        name="moe_experts",
    )(blk_exp, par, first, nxt, base, nval, order, n_used, hn, w_gate, w_up, w_down)


def _combine_kernel(y0_ref, y1_ref, h_ref, w_ref, o_ref):
    w = w_ref[...]
    o_ref[...] = h_ref[...] + (w[:, 0:1] * y0_ref[...] + w[:, 1:2] * y1_ref[...])


def _combine(y, h2, wts, *, tm):
    t, d = h2.shape
    n = t // tm
    return pl.pallas_call(
        _combine_kernel,
        grid=(n,),
        in_specs=[
            pl.BlockSpec((tm, d), lambda i: (i, 0)),
            pl.BlockSpec((tm, d), lambda i: (n + i, 0)),
            pl.BlockSpec((tm, d), lambda i: (i, 0)),
            pl.BlockSpec((tm, ROUTE_LANES), lambda i: (i, 0)),
        ],
        out_specs=pl.BlockSpec((tm, d), lambda i: (i, 0)),
        out_shape=jax.ShapeDtypeStruct((t, d), F32),
        compiler_params=pltpu.CompilerParams(
            dimension_semantics=("parallel",), vmem_limit_bytes=VMEM_LIMIT),
        name="moe_combine",
    )(y, y, h2, wts)


def _lookup(table, idx):
    sel = idx[:, None] == jnp.arange(table.shape[0], dtype=jnp.int32)[None, :]
    return jnp.sum(jnp.where(sel, table[None, :], 0), axis=1).astype(jnp.int32)


def _dispatch(eid, n_tokens):
    n_assign = n_tokens * TOP_K_FINE
    experts = jnp.arange(N_EXPERTS, dtype=jnp.int32)
    e_flat = eid.reshape(n_assign)
    _, order = lax.sort_key_val(e_flat, jnp.arange(n_assign, dtype=jnp.int32))
    counts = jnp.sum((e_flat[:, None] == experts[None, :]).astype(jnp.int32), axis=0)
    starts = jnp.cumsum(counts) - counts
    nb = (counts + MOE_BLOCK - 1) // MOE_BLOCK
    blk_end = jnp.cumsum(nb)
    blk_start = blk_end - nb
    n_used = blk_end[-1]
    n_blocks = (n_assign + N_EXPERTS * (MOE_BLOCK - 1)) // MOE_BLOCK
    b_ids = jnp.arange(n_blocks, dtype=jnp.int32)
    used = b_ids < n_used
    blk_exp = jnp.minimum(jnp.sum((blk_end[None, :] <= b_ids[:, None]).astype(jnp.int32), axis=1),
                          N_EXPERTS - 1)
    j = b_ids - _lookup(blk_start, blk_exp)
    base = jnp.where(used, _lookup(starts, blk_exp) + j * MOE_BLOCK, 0)
    nval = jnp.where(used, jnp.clip(_lookup(counts, blk_exp) - j * MOE_BLOCK, 0, MOE_BLOCK), 0)
    first = (used & (j == 0)).astype(jnp.int32)
    active = counts > 0
    par = _lookup(jnp.cumsum(active.astype(jnp.int32)) - 1, blk_exp) & 1
    later = lax.cummin(jnp.where(active, experts, N_EXPERTS), reverse=True)
    nxt_e = jnp.concatenate([later[1:], jnp.full((1,), N_EXPERTS, jnp.int32)])
    nxt = _lookup(jnp.where(nxt_e == N_EXPERTS, -1, nxt_e), blk_exp)
    meta = tuple(v.astype(jnp.int32) for v in (blk_exp, par, first, nxt, base, nval))
    return meta, order, n_used.astype(jnp.int32).reshape(1)


def kernel(x, mem, norm1_w, w_in, q_norm_w, k_norm_w, lambda_q1, lambda_k1, lambda_q2, lambda_k2, subln_w, ssm_lambda_re, ssm_lambda_im, ssm_log_dt, ssm_b_re, ssm_b_im, ssm_c_re, ssm_c_im, ssm_d, ssm_glu_w, ssm_glu_b, ssm_out_norm_w, w_out, norm2_w, mem_norm_w, xq_w, xkv_w, xq_norm_w, xk_norm_w, xo_w, norm3_w, router_coarse_w, router_coarse_b, router_fine_w, router_fine_b, expert_w_gate, expert_w_up, expert_w_down):
    batch, seq, d = x.shape
    mem_len = mem.shape[1]
    t = batch * seq
    depth = norm1_w.shape[0]
    d_attn = DA_HEADS * DA_V_DIM
    d_ssm = d - d_attn
    qk_cols = DA_HEADS * 2 * DA_QK_DIM
    x_hd = d // X_HEADS
    h = x.reshape(t, d)
    mem2 = mem.reshape(batch * mem_len, d)

    for l in range(depth):
        lam_init = 0.8 - 0.6 * math.exp(-0.3 * l)
        lam = (jnp.exp(jnp.sum(lambda_q1[l].astype(F32) * lambda_k1[l].astype(F32)))
               - jnp.exp(jnp.sum(lambda_q2[l].astype(F32) * lambda_k2[l].astype(F32)))
               + lam_init).reshape(1)

        n_rep = qk_cols // DA_QK_DIM
        in_gain = jnp.concatenate([
            jnp.tile(q_norm_w[l].astype(F32) * (DA_QK_DIM ** -0.5 * LOG2E), n_rep),
            jnp.tile(k_norm_w[l].astype(F32), n_rep),
            jnp.ones((d_attn + d_ssm,), F32)])
        proj = _norm_matmul(h, norm1_w[l], w_in[l].astype(BF16), in_gain,
                            n_norm_cols=2 * qk_cols, chunk=DA_QK_DIM, tm=512, tn=512, name="in_proj")
        sub_gain = (subln_w[l].astype(F32) * (1.0 - lam_init)).reshape(1, DA_V_DIM)
        a = _diff_attn(proj, lam, sub_gain, batch=batch, seq=seq, tq=512)

        seg_len = seq // SCAN_SEGS
        u = proj[:, 2 * qk_cols + d_attn:]
        u_perm = u.reshape(batch, SCAN_SEGS, seg_len, d_ssm).transpose(0, 2, 1, 3).reshape(batch, seq, d_ssm)
        bd, a_re, a_im, cd, dd = _s5_params(ssm_lambda_re[l], ssm_lambda_im[l], ssm_log_dt[l],
                                            ssm_b_re[l], ssm_b_im[l], ssm_c_re[l], ssm_c_im[l], ssm_d[l])
        y_perm = _s5(u_perm, bd, a_re, a_im, cd, dd)
        y = y_perm.reshape(batch, seg_len, SCAN_SEGS, d_ssm).transpose(0, 2, 1, 3).reshape(t, d_ssm)
        h = _mix_out(a, y, h, ssm_glu_w[l].astype(BF16), ssm_glu_b[l], ssm_out_norm_w[l],
                     w_out[l].astype(BF16), tm=256)

        kv_gain = jnp.concatenate([jnp.tile(xk_norm_w[l].astype(F32), X_HEADS), jnp.ones((d,), F32)])
        kv = _norm_matmul(mem2, mem_norm_w[l], xkv_w[l].astype(BF16), kv_gain,
                          n_norm_cols=d, chunk=x_hd, tm=512, tn=512, name="kv_proj")
        q_gain = jnp.tile(xq_norm_w[l].astype(F32) * (x_hd ** -0.5), X_HEADS)
        q = _norm_matmul(h, norm2_w[l], xq_w[l].astype(BF16), q_gain,
                         n_norm_cols=d, chunk=x_hd, tm=512, tn=512, name="xq_proj")
        r_w = jnp.concatenate([router_coarse_w[l].astype(F32), router_fine_w[l].astype(F32)], axis=1)
        r_w = jnp.pad(r_w, ((0, 0), (0, ROUTE_LANES - r_w.shape[1])))
        r_hi = r_w.astype(BF16)
        r_lo = (r_w - r_hi.astype(F32)).astype(BF16)
        r_b = jnp.concatenate([router_coarse_b[l].astype(F32), router_fine_b[l].astype(F32)])
        r_b = jnp.pad(r_b, (0, ROUTE_LANES - r_b.shape[0])).reshape(1, ROUTE_LANES)
        h2, hn3, eid, wts = _xattn_route(q, kv, h, xo_w[l].astype(BF16), norm3_w[l], r_hi, r_lo, r_b,
                                         batch=batch, seq=seq, mem_len=mem_len, tm=256)

        meta, order, n_used = _dispatch(eid[:, :TOP_K_FINE], t)
        y = _moe_experts(hn3, expert_w_gate[l], expert_w_up[l], expert_w_down[l], meta, order, n_used)
        h = _combine(y, h2, wts, tm=512)

    return h.reshape(batch, seq, d)
```

```python
import functools
import math

import jax
import jax.numpy as jnp
from jax import lax
from jax.experimental import pallas as pl
from jax.experimental.pallas import tpu as pltpu

F32 = jnp.float32
BF16 = jnp.bfloat16

EPS = 1e-6
DA_HEADS = 4
DA_QK_DIM = 128
DA_V_DIM = 256
SSM_GROUP = 16
SSM_STATE = 64
X_HEADS = 4
MOE_GROUPS = 8
EXP_PER_GROUP = 8
N_EXPERTS = MOE_GROUPS * EXP_PER_GROUP
TOP_K_FINE = 2

LANES = 128
SUBLANES = 8
MXU_TILE = 256
VMEM_LIMIT = 56 * 1024 * 1024
NEG = -1e30
LOG2E = math.log2(math.e)

SSM_CHUNK_GROUPS = LANES // SSM_GROUP
SSM_CHUNK_STATE = SSM_CHUNK_GROUPS * SSM_STATE
SCAN_SEGS = SUBLANES
MOE_BLOCK = 256
ROW_GROUP = SUBLANES
ROUTE_LANES = LANES


def _rms(x, eps=EPS):
    return x * lax.rsqrt(jnp.mean(x * x, axis=-1, keepdims=True) + eps)


def _dot(a, b):
    return jnp.dot(a, b, preferred_element_type=F32)


def _dot_nt(a, b):
    return lax.dot_general(a, b, (((1,), (1,)), ((), ())), preferred_element_type=F32)


def _norm_matmul_kernel(x_ref, nw_ref, w_ref, g_ref, o_ref, xn_ref, *, n_norm_tiles, chunk):
    j = pl.program_id(1)

    @pl.when(j == 0)
    def _():
        x = x_ref[...].astype(F32)
        xn_ref[...] = (_rms(x) * nw_ref[...]).astype(BF16)

    normed = j < n_norm_tiles
    tn = w_ref.shape[1]
    sub = max(chunk, MXU_TILE)
    for s in range(tn // sub):
        acc = _dot(xn_ref[...], w_ref[:, s * sub:(s + 1) * sub])
        for c in range(sub // chunk):
            lo = s * sub + c * chunk
            a = acc[:, c * chunk:(c + 1) * chunk]
            inv = lax.rsqrt(jnp.mean(a * a, axis=-1, keepdims=True) + EPS)
            scale = jnp.where(normed, inv, 1.0)
            o_ref[:, lo:lo + chunk] = (a * scale * g_ref[:, lo:lo + chunk]).astype(o_ref.dtype)


def _norm_matmul(x, norm_w, w_bf16, gain, *, n_norm_cols, chunk, tm, tn, name):
    m, k = x.shape
    n = w_bf16.shape[1]
    assert m % tm == 0 and n % tn == 0 and tn % max(chunk, MXU_TILE) == 0 and n_norm_cols % tn == 0
    kern = functools.partial(_norm_matmul_kernel, n_norm_tiles=n_norm_cols // tn, chunk=chunk)
    return pl.pallas_call(
        kern,
        grid=(m // tm, n // tn),
        in_specs=[
            pl.BlockSpec((tm, k), lambda i, j: (i, 0)),
            pl.BlockSpec((1, k), lambda i, j: (0, 0)),
            pl.BlockSpec((k, tn), lambda i, j: (0, j)),
            pl.BlockSpec((1, tn), lambda i, j: (0, j)),
        ],
        out_specs=pl.BlockSpec((tm, tn), lambda i, j: (i, j)),
        out_shape=jax.ShapeDtypeStruct((m, n), BF16),
        scratch_shapes=[pltpu.VMEM((tm, k), BF16)],
        compiler_params=pltpu.CompilerParams(
            dimension_semantics=("parallel", "arbitrary"), vmem_limit_bytes=VMEM_LIMIT),
        name=name,
    )(x, norm_w.reshape(1, k).astype(F32), w_bf16, gain.reshape(1, n).astype(F32))


def _diff_attn_kernel(lam_ref, q_ref, k_ref, v_ref, g_ref, o_ref,
                      m1_ref, l1_ref, acc1_ref, m2_ref, l2_ref, acc2_ref, *, tq):
    qi = pl.program_id(2)
    stats = ((m1_ref, l1_ref, acc1_ref), (m2_ref, l2_ref, acc2_ref))
    for m_ref, l_ref, acc_ref in stats:
        m_ref[...] = jnp.full(m_ref.shape, NEG, F32)
        l_ref[...] = jnp.zeros(l_ref.shape, F32)
        acc_ref[...] = jnp.zeros(acc_ref.shape, F32)

    def block(j, masked):
        start = pl.multiple_of(j * tq, tq)
        for c, (m_ref, l_ref, acc_ref) in enumerate(stats):
            lo, hi = c * DA_QK_DIM, (c + 1) * DA_QK_DIM
            s = _dot_nt(q_ref[:, lo:hi], k_ref[pl.ds(start, tq), lo:hi])
            if masked:
                row = lax.broadcasted_iota(jnp.int32, s.shape, 0)
                col = lax.broadcasted_iota(jnp.int32, s.shape, 1)
                s = jnp.where(col <= row, s, NEG)
            m_old = m_ref[...]
            m_new = jnp.maximum(m_old, jnp.max(s, axis=-1, keepdims=True))
            p = jnp.exp2(s - m_new)
            alpha = jnp.exp2(m_old - m_new)
            l_ref[...] = alpha * l_ref[...] + jnp.sum(p, axis=-1, keepdims=True)
            acc_ref[...] = alpha * acc_ref[...] + _dot(p.astype(BF16), v_ref[pl.ds(start, tq), :])
            m_ref[...] = m_new

    def off_diag(j, carry):
        block(j, False)
        return carry

    lax.fori_loop(0, qi, off_diag, 0)
    block(qi, True)

    lam = lam_ref[0]
    o = acc1_ref[...] / l1_ref[...] - lam * (acc2_ref[...] / l2_ref[...])
    o_ref[...] = (_rms(o) * g_ref[...]).astype(o_ref.dtype)


def _diff_attn(proj, lam, gain, *, batch, seq, tq):
    t = batch * seq
    nq = seq // tq
    width = 2 * DA_QK_DIM
    k_blk0 = DA_HEADS
    v_blk0 = 2 * DA_HEADS
    kern = functools.partial(_diff_attn_kernel, tq=tq)
    return pl.pallas_call(
        kern,
        grid_spec=pltpu.PrefetchScalarGridSpec(
            num_scalar_prefetch=1,
            grid=(batch, DA_HEADS, nq),
            in_specs=[
                pl.BlockSpec((tq, width), lambda b, h, i, lam: (b * nq + i, h)),
                pl.BlockSpec((seq, width), lambda b, h, i, lam: (b, k_blk0 + h)),
                pl.BlockSpec((seq, width), lambda b, h, i, lam: (b, v_blk0 + h)),
                pl.BlockSpec((1, DA_V_DIM), lambda b, h, i, lam: (0, 0)),
            ],
            out_specs=pl.BlockSpec((tq, DA_V_DIM), lambda b, h, i, lam: (b * nq + i, h)),
            scratch_shapes=[
                pltpu.VMEM((tq, 1), F32), pltpu.VMEM((tq, 1), F32), pltpu.VMEM((tq, DA_V_DIM), F32),
                pltpu.VMEM((tq, 1), F32), pltpu.VMEM((tq, 1), F32), pltpu.VMEM((tq, DA_V_DIM), F32),
            ],
        ),
        out_shape=jax.ShapeDtypeStruct((t, DA_HEADS * DA_V_DIM), BF16),
        compiler_params=pltpu.CompilerParams(
            dimension_semantics=("parallel", "parallel", "arbitrary"), vmem_limit_bytes=VMEM_LIMIT),
        name="diff_attn",
    )(lam, proj, proj, proj, gain)


def _s5_kernel(u_ref, bd_ref, ar_ref, ai_ref, cd_ref, d_ref, o_ref, xs_ref, *, seq, rows):
    ns = SSM_CHUNK_STATE
    seg_len = seq // SCAN_SEGS
    n_row_blk = seq // rows

    def in_map(r, carry):
        r0 = pl.multiple_of(r * rows, rows)
        xs_ref[pl.ds(r0, rows), :] = _dot(u_ref[pl.ds(r0, rows), :], bd_ref[...])
        return carry

    lax.fori_loop(0, n_row_blk, in_map, 0)

    ar = jnp.broadcast_to(ar_ref[...], (SCAN_SEGS, ns))
    ai = jnp.broadcast_to(ai_ref[...], (SCAN_SEGS, ns))

    def advance(t, sr, si):
        t0 = pl.multiple_of(t * SCAN_SEGS, SCAN_SEGS)
        br = xs_ref[pl.ds(t0, SCAN_SEGS), 0:ns]
        bi = xs_ref[pl.ds(t0, SCAN_SEGS), ns:2 * ns]
        return t0, ar * sr - ai * si + br, ar * si + ai * sr + bi

    def local_step(t, carry):
        _, nr, ni = advance(t, *carry)
        return nr, ni

    zero = jnp.zeros((SCAN_SEGS, ns), F32)
    fr, fi = lax.fori_loop(0, seg_len, local_step, (zero, zero))

    pr, pi = ar, ai
    for _ in range(int(math.log2(seg_len))):
        pr, pi = pr * pr - pi * pi, 2.0 * pr * pi
    seg = lax.broadcasted_iota(jnp.int32, (SCAN_SEGS, ns), 0)

    def shifted(x, k):
        return jnp.where(seg >= k, pltpu.roll(x, k, 0), 0.0)

    k = 1
    while k < SCAN_SEGS:
        gr, gi = shifted(fr, k), shifted(fi, k)
        fr, fi = fr + pr * gr - pi * gi, fi + pr * gi + pi * gr
        pr, pi = pr * pr - pi * pi, 2.0 * pr * pi
        k *= 2
    sr0, si0 = shifted(fr, 1), shifted(fi, 1)

    def global_step(t, carry):
        t0, nr, ni = advance(t, *carry)
        xs_ref[pl.ds(t0, SCAN_SEGS), 0:ns] = nr
        xs_ref[pl.ds(t0, SCAN_SEGS), ns:2 * ns] = ni
        return nr, ni

    lax.fori_loop(0, seg_len, global_step, (sr0, si0))

    def out_map(r, carry):
        r0 = pl.multiple_of(r * rows, rows)
        x = xs_ref[pl.ds(r0, rows), :].astype(BF16)
        y = _dot(x, cd_ref[...]) + d_ref[...] * u_ref[pl.ds(r0, rows), :].astype(F32)
        o_ref[pl.ds(r0, rows), :] = jax.nn.gelu(y).astype(o_ref.dtype)
        return carry

    lax.fori_loop(0, n_row_blk, out_map, 0)


def _s5(u_perm, bd, a_re, a_im, cd, d_skip, *, rows=256):
    batch, seq, d_ssm = u_perm.shape
    n_chunks = d_ssm // LANES
    kern = functools.partial(_s5_kernel, seq=seq, rows=rows)
    return pl.pallas_call(
        kern,
        grid=(batch, n_chunks),
        in_specs=[
            pl.BlockSpec((None, seq, LANES), lambda b, c: (b, 0, c)),
            pl.BlockSpec((None, LANES, 2 * SSM_CHUNK_STATE), lambda b, c: (c, 0, 0)),
            pl.BlockSpec((None, 1, SSM_CHUNK_STATE), lambda b, c: (c, 0, 0)),
            pl.BlockSpec((None, 1, SSM_CHUNK_STATE), lambda b, c: (c, 0, 0)),
            pl.BlockSpec((None, 2 * SSM_CHUNK_STATE, LANES), lambda b, c: (c, 0, 0)),
            pl.BlockSpec((None, 1, LANES), lambda b, c: (c, 0, 0)),
        ],
        out_specs=pl.BlockSpec((None, seq, LANES), lambda b, c: (b, 0, c)),
        out_shape=jax.ShapeDtypeStruct((batch, seq, d_ssm), BF16),
        scratch_shapes=[pltpu.VMEM((seq, 2 * SSM_CHUNK_STATE), F32)],
        compiler_params=pltpu.CompilerParams(
            dimension_semantics=("parallel", "parallel"), vmem_limit_bytes=VMEM_LIMIT),
        name="s5_scan",
    )(u_perm, bd, a_re, a_im, cd, d_skip)


def _s5_params(lam_re, lam_im, log_dt, b_re, b_im, c_re, c_im, d_skip):
    g = lam_re.shape[0]
    nc = g // SSM_CHUNK_GROUPS
    lr = jnp.minimum(lam_re.astype(F32), -1e-4)
    li = lam_im.astype(F32)
    dt = jnp.exp(log_dt.astype(F32))[:, None]
    mag = jnp.exp(lr * dt)
    lb_re, lb_im = mag * jnp.cos(li * dt), mag * jnp.sin(li * dt)
    den = lr * lr + li * li
    coef_re = ((lb_re - 1.0) * lr + lb_im * li) / den
    coef_im = (lb_im * lr - (lb_re - 1.0) * li) / den
    br, bi = b_re.astype(F32), b_im.astype(F32)
    bb_re = coef_re[..., None] * br - coef_im[..., None] * bi
    bb_im = coef_re[..., None] * bi + coef_im[..., None] * br
    eye = jnp.eye(SSM_CHUNK_GROUPS, dtype=F32)

    def pack_in(bb):
        bb = bb.reshape(nc, SSM_CHUNK_GROUPS, SSM_STATE, SSM_GROUP)
        return jnp.einsum('cgph,gk->cghkp', bb, eye).reshape(nc, LANES, SSM_CHUNK_STATE)

    def pack_out(cc):
        cc = cc.astype(F32).reshape(nc, SSM_CHUNK_GROUPS, SSM_GROUP, SSM_STATE)
        return jnp.einsum('cghp,gk->ckpgh', cc, eye).reshape(nc, SSM_CHUNK_STATE, LANES)

    bd = jnp.concatenate([pack_in(bb_re), pack_in(bb_im)], axis=-1).astype(BF16)
    cd = jnp.concatenate([pack_out(c_re), -pack_out(c_im)], axis=1).astype(BF16)
    a_re = lb_re.reshape(nc, 1, SSM_CHUNK_STATE)
    a_im = lb_im.reshape(nc, 1, SSM_CHUNK_STATE)
    dd = d_skip.astype(F32).reshape(nc, 1, LANES)
    return bd, a_re, a_im, cd, dd


def _mix_out_kernel(a_ref, y_ref, x_ref, gw_ref, gb_ref, nw_ref, wo_ref, o_ref):
    d_attn = a_ref.shape[1]
    y = y_ref[...]
    gate = _dot(y, gw_ref[...]) + gb_ref[...]
    s = y.astype(F32) * jax.nn.sigmoid(gate)
    sn = (_rms(s) * nw_ref[...]).astype(BF16)
    acc = _dot(a_ref[...], wo_ref[0:d_attn, :]) + _dot(sn, wo_ref[d_attn:, :])
    o_ref[...] = x_ref[...] + acc


def _mix_out(a, y, x, glu_w, glu_b, norm_w, w_out, *, tm):
    t, d = x.shape
    d_attn, d_ssm = a.shape[1], y.shape[1]
    const = lambda i: (0, 0)
    return pl.pallas_call(
        _mix_out_kernel,
        grid=(t // tm,),
        in_specs=[
            pl.BlockSpec((tm, d_attn), lambda i: (i, 0)),
            pl.BlockSpec((tm, d_ssm), lambda i: (i, 0)),
            pl.BlockSpec((tm, d), lambda i: (i, 0)),
            pl.BlockSpec((d_ssm, d_ssm), const),
            pl.BlockSpec((1, d_ssm), const),
            pl.BlockSpec((1, d_ssm), const),
            pl.BlockSpec((d, d), const),
        ],
        out_specs=pl.BlockSpec((tm, d), lambda i: (i, 0)),
        out_shape=jax.ShapeDtypeStruct((t, d), F32),
        compiler_params=pltpu.CompilerParams(
            dimension_semantics=("parallel",), vmem_limit_bytes=VMEM_LIMIT),
        name="mix_out",
    )(a, y, x, glu_w, glu_b.reshape(1, d_ssm).astype(F32), norm_w.reshape(1, d_ssm).astype(F32), w_out)


def _xattn_route_kernel(q_ref, k_ref, v_ref, h_ref, xo_ref, nw_ref, rhi_ref, rlo_ref, rb_ref,
                        h2_ref, hn_ref, eid_ref, wts_ref):
    d = h_ref.shape[1]
    hd = d // X_HEADS
    parts = []
    for h in range(X_HEADS):
        sl = slice(h * hd, (h + 1) * hd)
        s = _dot_nt(q_ref[:, sl], k_ref[:, sl])
        p = jnp.exp(s - jnp.max(s, axis=-1, keepdims=True))
        p = p / jnp.sum(p, axis=-1, keepdims=True)
        parts.append(_dot(p.astype(BF16), v_ref[:, sl]))
    o = jnp.concatenate(parts, axis=-1).astype(BF16)
    h2 = h_ref[...] + _dot(o, xo_ref[...])
    h2_ref[...] = h2
    hn = _rms(h2) * nw_ref[...]
    hn_ref[...] = hn

    hi = hn.astype(BF16)
    lo = (hn - hi.astype(F32)).astype(BF16)
    logits = (_dot(hi, rhi_ref[...]) + _dot(hi, rlo_ref[...]) + _dot(lo, rhi_ref[...])) + rb_ref[...]

    lane = lax.broadcasted_iota(jnp.int32, logits.shape, 1)
    big = jnp.int32(ROUTE_LANES)

    def first_lane(cond):
        return jnp.min(jnp.where(cond, lane, big), axis=-1, keepdims=True)

    c_mask = lane < MOE_GROUPS
    lc = jnp.where(c_mask, logits, NEG)
    mc = jnp.max(lc, axis=-1, keepdims=True)
    ec = jnp.exp(lc - mc)
    p_c = ec / jnp.sum(ec, axis=-1, keepdims=True)
    p_grp = jnp.max(p_c, axis=-1, keepdims=True)
    grp = first_lane(c_mask & (p_c == p_grp))
    f_lo = MOE_GROUPS + grp * EXP_PER_GROUP
    f_mask = (lane >= f_lo) & (lane < f_lo + EXP_PER_GROUP)
    lf = jnp.where(f_mask, logits, NEG)
    mf = jnp.max(lf, axis=-1, keepdims=True)
    ef = jnp.exp(lf - mf)
    pf = ef / jnp.sum(ef, axis=-1, keepdims=True)
    v1 = jnp.max(jnp.where(f_mask, pf, -1.0), axis=-1, keepdims=True)
    i1 = first_lane(f_mask & (pf == v1))
    rest = f_mask & (lane != i1)
    v2 = jnp.max(jnp.where(rest, pf, -1.0), axis=-1, keepdims=True)
    i2 = first_lane(rest & (pf == v2))
    tot = v1 + v2
    w1 = v1 / tot * p_grp
    w2 = v2 / tot * p_grp
    eid_ref[...] = jnp.where(lane == 0, i1 - MOE_GROUPS, jnp.where(lane == 1, i2 - MOE_GROUPS, 0))
    wts_ref[...] = jnp.where(lane == 0, w1, jnp.where(lane == 1, w2, 0.0))


def _xattn_route(q, kv, h1, xo_w, norm_w, r_hi, r_lo, r_b, *, batch, seq, mem_len, tm):
    t, d = h1.shape
    n = seq // tm
    const = lambda b, i: (0, 0)
    row = lambda b, i: (b * n + i, 0)
    return pl.pallas_call(
        _xattn_route_kernel,
        grid=(batch, n),
        in_specs=[
            pl.BlockSpec((tm, d), row),
            pl.BlockSpec((mem_len, d), lambda b, i: (b, 0)),
            pl.BlockSpec((mem_len, d), lambda b, i: (b, 1)),
            pl.BlockSpec((tm, d), row),
            pl.BlockSpec((d, d), const),
            pl.BlockSpec((1, d), const),
            pl.BlockSpec((d, ROUTE_LANES), const),
            pl.BlockSpec((d, ROUTE_LANES), const),
            pl.BlockSpec((1, ROUTE_LANES), const),
        ],
        out_specs=[
            pl.BlockSpec((tm, d), row),
            pl.BlockSpec((tm, d), row),
            pl.BlockSpec((tm, ROUTE_LANES), row),
            pl.BlockSpec((tm, ROUTE_LANES), row),
        ],
        out_shape=[
            jax.ShapeDtypeStruct((t, d), F32),
            jax.ShapeDtypeStruct((t, d), F32),
            jax.ShapeDtypeStruct((t, ROUTE_LANES), jnp.int32),
            jax.ShapeDtypeStruct((t, ROUTE_LANES), F32),
        ],
        compiler_params=pltpu.CompilerParams(
            dimension_semantics=("parallel", "parallel"), vmem_limit_bytes=VMEM_LIMIT),
        name="xattn_route",
    )(q, kv, kv, h1, xo_w, norm_w.reshape(1, d).astype(F32), r_hi, r_lo, r_b)


def _moe_kernel(be_ref, par_ref, first_ref, nxt_ref, base_ref, nval_ref, ord_ref, nu_ref,
                hn_hbm, wg_hbm, wu_hbm, wd_hbm, y_hbm,
                xbuf, ybuf, wgb, wub, wdb, gsem, ssem, wsem, *, n_tokens):
    b = pl.program_id(0)
    n_used = nu_ref[0]

    def weight_copies(e, slot):
        return (pltpu.make_async_copy(wg_hbm.at[e], wgb.at[slot], wsem.at[slot]),
                pltpu.make_async_copy(wu_hbm.at[e], wub.at[slot], wsem.at[slot]),
                pltpu.make_async_copy(wd_hbm.at[e], wdb.at[slot], wsem.at[slot]))

    def groups(blk):
        return (nval_ref[blk] + ROW_GROUP - 1) // ROW_GROUP

    def rows_issued(blk):
        return pl.multiple_of(groups(blk) * ROW_GROUP, ROW_GROUP)

    def start_gather(blk, slot):
        base = base_ref[blk]
        last = nval_ref[blk] - 1

        def body(g, carry):
            for r in range(ROW_GROUP):
                i = g * ROW_GROUP + r
                tok = lax.shift_right_logical(ord_ref[base + jnp.minimum(i, last)], 1)
                pltpu.make_async_copy(hn_hbm.at[pl.ds(tok, 1)], xbuf.at[slot, pl.ds(i, 1)],
                                      gsem.at[slot]).start()
            return carry
        lax.fori_loop(0, groups(blk), body, 0)

    def wait_gather(blk, slot):
        n = rows_issued(blk)
        pltpu.make_async_copy(hn_hbm.at[pl.ds(0, n)], xbuf.at[slot, pl.ds(0, n)], gsem.at[slot]).wait()

    def start_scatter(blk, slot):
        base = base_ref[blk]
        last = nval_ref[blk] - 1
        spare = TOP_K_FINE * n_tokens + slot * ROW_GROUP

        def body(g, carry):
            for r in range(ROW_GROUP):
                i = g * ROW_GROUP + r
                a = ord_ref[base + jnp.minimum(i, last)]
                row = jnp.where(i <= last, (a & 1) * n_tokens + lax.shift_right_logical(a, 1), spare + r)
                pltpu.make_async_copy(ybuf.at[slot, pl.ds(i, 1)], y_hbm.at[pl.ds(row, 1)],
                                      ssem.at[slot]).start()
            return carry
        lax.fori_loop(0, groups(blk), body, 0)

    def wait_scatter(blk, slot):
        n = rows_issued(blk)
        pltpu.make_async_copy(ybuf.at[slot, pl.ds(0, n)], y_hbm.at[pl.ds(0, n)], ssem.at[slot]).wait()

    @pl.when(b == 0)
    def _():
        xbuf[...] = jnp.zeros(xbuf.shape, xbuf.dtype)
        n_spare = 2 * ROW_GROUP
        spare_init = pltpu.make_async_copy(xbuf.at[0, pl.ds(0, n_spare)],
                                           y_hbm.at[pl.ds(TOP_K_FINE * n_tokens, n_spare)], ssem.at[0])
        spare_init.start()
        spare_init.wait()
        for c in weight_copies(be_ref[0], par_ref[0]):
            c.start()
        start_gather(0, 0)

    @pl.when(b < n_used)
    def _():
        slot = b % 2
        wslot = par_ref[b]
        is_first = first_ref[b] == 1

        @pl.when(is_first & (nxt_ref[b] >= 0))
        def _():
            for c in weight_copies(nxt_ref[b], 1 - wslot):
                c.start()

        @pl.when(b + 1 < n_used)
        def _():
            start_gather(b + 1, 1 - slot)

        @pl.when(b >= 2)
        def _():
            wait_scatter(b - 2, slot)

        @pl.when(is_first)
        def _():
            for c in weight_copies(0, wslot):
                c.wait()

        wait_gather(b, slot)
        x = xbuf[slot].astype(BF16)
        g = _dot(x, wgb[wslot].astype(BF16))
        u = _dot(x, wub[wslot].astype(BF16))
        mid = (jax.nn.silu(g) * u).astype(BF16)
        ybuf[slot] = _dot(mid, wdb[wslot].astype(BF16))
        start_scatter(b, slot)

        @pl.when(b == n_used - 1)
        def _():
            wait_scatter(b, slot)

            @pl.when(b >= 1)
            def _():
                wait_scatter(b - 1, 1 - slot)


def _moe_experts(hn, w_gate, w_up, w_down, meta, order, n_used):
    t, d = hn.shape
    d_ff = w_gate.shape[2]
    blk_exp, par, first, nxt, base, nval = meta
    n_blocks = blk_exp.shape[0]
    kern = functools.partial(_moe_kernel, n_tokens=t)
    any_spec = pl.BlockSpec(memory_space=pl.ANY)
    return pl.pallas_call(
        kern,
        grid_spec=pltpu.PrefetchScalarGridSpec(
            num_scalar_prefetch=8,
            grid=(n_blocks,),
            in_specs=[any_spec, any_spec, any_spec, any_spec],
            out_specs=any_spec,
            scratch_shapes=[
                pltpu.VMEM((2, MOE_BLOCK, d), F32),
                pltpu.VMEM((2, MOE_BLOCK, d), F32),
                pltpu.VMEM((2, d, d_ff), F32),
                pltpu.VMEM((2, d, d_ff), F32),
                pltpu.VMEM((2, d_ff, d), F32),
                pltpu.SemaphoreType.DMA((2,)),
                pltpu.SemaphoreType.DMA((2,)),
                pltpu.SemaphoreType.DMA((2,)),
            ],
        ),
        out_shape=jax.ShapeDtypeStruct((TOP_K_FINE * t + 2 * ROW_GROUP, d), F32),
        compiler_params=pltpu.CompilerParams(
            dimension_semantics=("arbitrary",), vmem_limit_bytes=VMEM_LIMIT),
        name="moe_experts",
    )(blk_exp, par, first, nxt, base, nval, order, n_used, hn, w_gate, w_up, w_down)


def _combine_kernel(y0_ref, y1_ref, h_ref, w_ref, o_ref):
    w = w_ref[...]
    o_ref[...] = h_ref[...] + (w[:, 0:1] * y0_ref[...] + w[:, 1:2] * y1_ref[...])


def _combine(y, h2, wts, *, tm):
    t, d = h2.shape
    n = t // tm
    return pl.pallas_call(
        _combine_kernel,
        grid=(n,),
        in_specs=[
            pl.BlockSpec((tm, d), lambda i: (i, 0)),
            pl.BlockSpec((tm, d), lambda i: (n + i, 0)),
            pl.BlockSpec((tm, d), lambda i: (i, 0)),
            pl.BlockSpec((tm, ROUTE_LANES), lambda i: (i, 0)),
        ],
        out_specs=pl.BlockSpec((tm, d), lambda i: (i, 0)),
        out_shape=jax.ShapeDtypeStruct((t, d), F32),
        compiler_params=pltpu.CompilerParams(
            dimension_semantics=("parallel",), vmem_limit_bytes=VMEM_LIMIT),
        name="moe_combine",
    )(y, y, h2, wts)


def _lookup(table, idx):
    sel = idx[:, None] == jnp.arange(table.shape[0], dtype=jnp.int32)[None, :]
    return jnp.sum(jnp.where(sel, table[None, :], 0), axis=1).astype(jnp.int32)


def _dispatch(eid, n_tokens):
    n_assign = n_tokens * TOP_K_FINE
    experts = jnp.arange(N_EXPERTS, dtype=jnp.int32)
    e_flat = eid.reshape(n_assign)
    _, order = lax.sort_key_val(e_flat, jnp.arange(n_assign, dtype=jnp.int32))
    counts = jnp.sum((e_flat[:, None] == experts[None, :]).astype(jnp.int32), axis=0)
    starts = jnp.cumsum(counts) - counts
    nb = (counts + MOE_BLOCK - 1) // MOE_BLOCK
    blk_end = jnp.cumsum(nb)
    blk_start = blk_end - nb
    n_used = blk_end[-1]
    n_blocks = (n_assign + N_EXPERTS * (MOE_BLOCK - 1)) // MOE_BLOCK
    b_ids = jnp.arange(n_blocks, dtype=jnp.int32)
    used = b_ids < n_used
    blk_exp = jnp.minimum(jnp.sum((blk_end[None, :] <= b_ids[:, None]).astype(jnp.int32), axis=1),
                          N_EXPERTS - 1)
    j = b_ids - _lookup(blk_start, blk_exp)
    base = jnp.where(used, _lookup(starts, blk_exp) + j * MOE_BLOCK, 0)
    nval = jnp.where(used, jnp.clip(_lookup(counts, blk_exp) - j * MOE_BLOCK, 0, MOE_BLOCK), 0)
    first = (used & (j == 0)).astype(jnp.int32)
    active = counts > 0
    par = _lookup(jnp.cumsum(active.astype(jnp.int32)) - 1, blk_exp) & 1
    later = lax.cummin(jnp.where(active, experts, N_EXPERTS), reverse=True)
    nxt_e = jnp.concatenate([later[1:], jnp.full((1,), N_EXPERTS, jnp.int32)])
    nxt = _lookup(jnp.where(nxt_e == N_EXPERTS, -1, nxt_e), blk_exp)
    meta = tuple(v.astype(jnp.int32) for v in (blk_exp, par, first, nxt, base, nval))
    return meta, order, n_used.astype(jnp.int32).reshape(1)


def kernel(x, mem, norm1_w, w_in, q_norm_w, k_norm_w, lambda_q1, lambda_k1, lambda_q2, lambda_k2, subln_w, ssm_lambda_re, ssm_lambda_im, ssm_log_dt, ssm_b_re, ssm_b_im, ssm_c_re, ssm_c_im, ssm_d, ssm_glu_w, ssm_glu_b, ssm_out_norm_w, w_out, norm2_w, mem_norm_w, xq_w, xkv_w, xq_norm_w, xk_norm_w, xo_w, norm3_w, router_coarse_w, router_coarse_b, router_fine_w, router_fine_b, expert_w_gate, expert_w_up, expert_w_down):
    batch, seq, d = x.shape
    mem_len = mem.shape[1]
    t = batch * seq
    depth = norm1_w.shape[0]
    d_attn = DA_HEADS * DA_V_DIM
    d_ssm = d - d_attn
    qk_cols = DA_HEADS * 2 * DA_QK_DIM
    x_hd = d // X_HEADS
    h = x.reshape(t, d)
    mem2 = mem.reshape(batch * mem_len, d)

    for l in range(depth):
        lam_init = 0.8 - 0.6 * math.exp(-0.3 * l)
        lam = (jnp.exp(jnp.sum(lambda_q1[l].astype(F32) * lambda_k1[l].astype(F32)))
               - jnp.exp(jnp.sum(lambda_q2[l].astype(F32) * lambda_k2[l].astype(F32)))
               + lam_init).reshape(1)

        n_rep = qk_cols // DA_QK_DIM
        in_gain = jnp.concatenate([
            jnp.tile(q_norm_w[l].astype(F32) * (DA_QK_DIM ** -0.5 * LOG2E), n_rep),
            jnp.tile(k_norm_w[l].astype(F32), n_rep),
            jnp.ones((d_attn + d_ssm,), F32)])
        proj = _norm_matmul(h, norm1_w[l], w_in[l].astype(BF16), in_gain,
                            n_norm_cols=2 * qk_cols, chunk=DA_QK_DIM, tm=1024, tn=1024, name="in_proj")
        sub_gain = (subln_w[l].astype(F32) * (1.0 - lam_init)).reshape(1, DA_V_DIM)
        a = _diff_attn(proj, lam, sub_gain, batch=batch, seq=seq, tq=512)

        seg_len = seq // SCAN_SEGS
        u = proj[:, 2 * qk_cols + d_attn:]
        u_perm = u.reshape(batch, SCAN_SEGS, seg_len, d_ssm).transpose(0, 2, 1, 3).reshape(batch, seq, d_ssm)
        bd, a_re, a_im, cd, dd = _s5_params(ssm_lambda_re[l], ssm_lambda_im[l], ssm_log_dt[l],
                                            ssm_b_re[l], ssm_b_im[l], ssm_c_re[l], ssm_c_im[l], ssm_d[l])
        y_perm = _s5(u_perm, bd, a_re, a_im, cd, dd)
        y = y_perm.reshape(batch, seg_len, SCAN_SEGS, d_ssm).transpose(0, 2, 1, 3).reshape(t, d_ssm)
        h = _mix_out(a, y, h, ssm_glu_w[l].astype(BF16), ssm_glu_b[l], ssm_out_norm_w[l],
                     w_out[l].astype(BF16), tm=256)

        kv_gain = jnp.concatenate([jnp.tile(xk_norm_w[l].astype(F32), X_HEADS), jnp.ones((d,), F32)])
        kv = _norm_matmul(mem2, mem_norm_w[l], xkv_w[l].astype(BF16), kv_gain,
                          n_norm_cols=d, chunk=x_hd, tm=512, tn=512, name="kv_proj")
        q_gain = jnp.tile(xq_norm_w[l].astype(F32) * (x_hd ** -0.5), X_HEADS)
        q = _norm_matmul(h, norm2_w[l], xq_w[l].astype(BF16), q_gain,
                         n_norm_cols=d, chunk=x_hd, tm=1024, tn=1024, name="xq_proj")
        r_w = jnp.concatenate([router_coarse_w[l].astype(F32), router_fine_w[l].astype(F32)], axis=1)
        r_w = jnp.pad(r_w, ((0, 0), (0, ROUTE_LANES - r_w.shape[1])))
        r_hi = r_w.astype(BF16)
        r_lo = (r_w - r_hi.astype(F32)).astype(BF16)
        r_b = jnp.concatenate([router_coarse_b[l].astype(F32), router_fine_b[l].astype(F32)])
        r_b = jnp.pad(r_b, (0, ROUTE_LANES - r_b.shape[0])).reshape(1, ROUTE_LANES)
        h2, hn3, eid, wts = _xattn_route(q, kv, h, xo_w[l].astype(BF16), norm3_w[l], r_hi, r_lo, r_b,
                                         batch=batch, seq=seq, mem_len=mem_len, tm=256)

        meta, order, n_used = _dispatch(eid[:, :TOP_K_FINE], t)
        y = _moe_experts(hn3, expert_w_gate[l], expert_w_up[l], expert_w_down[l], meta, order, n_used)
        h = _combine(y, h2, wts, tm=512)

    return h.reshape(batch, seq, d)
```

```python
import functools
import math

import jax
import jax.numpy as jnp
from jax import lax
from jax.experimental import pallas as pl
from jax.experimental.pallas import tpu as pltpu

F32 = jnp.float32
BF16 = jnp.bfloat16

EPS = 1e-6
DA_HEADS = 4
DA_QK_DIM = 128
DA_V_DIM = 256
SSM_GROUP = 16
SSM_STATE = 64
X_HEADS = 4
MOE_GROUPS = 8
EXP_PER_GROUP = 8
N_EXPERTS = MOE_GROUPS * EXP_PER_GROUP
TOP_K_FINE = 2

LANES = 128
SUBLANES = 8
MXU_TILE = 256
VMEM_LIMIT = 56 * 1024 * 1024
NEG = -1e30
LOG2E = math.log2(math.e)

SSM_CHUNK_GROUPS = LANES // SSM_GROUP
SSM_CHUNK_STATE = SSM_CHUNK_GROUPS * SSM_STATE
SCAN_SEGS = SUBLANES
MOE_BLOCK = 256
ROW_GROUP = SUBLANES
ROUTE_LANES = LANES


def _rms(x, eps=EPS):
    return x * lax.rsqrt(jnp.mean(x * x, axis=-1, keepdims=True) + eps)


def _dot(a, b):
    return jnp.dot(a, b, preferred_element_type=F32)


def _dot_nt(a, b):
    return lax.dot_general(a, b, (((1,), (1,)), ((), ())), preferred_element_type=F32)


def _norm_matmul_kernel(x_ref, nw_ref, w_ref, g_ref, o_ref, xn_ref, *, n_norm_tiles, chunk):
    j = pl.program_id(1)

    @pl.when(j == 0)
    def _():
        x = x_ref[...].astype(F32)
        xn_ref[...] = (_rms(x) * nw_ref[...]).astype(BF16)

    normed = j < n_norm_tiles
    tn = w_ref.shape[1]
    sub = max(chunk, MXU_TILE)
    for s in range(tn // sub):
        acc = _dot(xn_ref[...], w_ref[:, s * sub:(s + 1) * sub])
        for c in range(sub // chunk):
            lo = s * sub + c * chunk
            a = acc[:, c * chunk:(c + 1) * chunk]
            inv = lax.rsqrt(jnp.mean(a * a, axis=-1, keepdims=True) + EPS)
            scale = jnp.where(normed, inv, 1.0)
            o_ref[:, lo:lo + chunk] = (a * scale * g_ref[:, lo:lo + chunk]).astype(o_ref.dtype)


def _norm_matmul(x, norm_w, w_bf16, gain, *, n_norm_cols, chunk, tm, tn, name):
    m, k = x.shape
    n = w_bf16.shape[1]
    assert m % tm == 0 and n % tn == 0 and tn % max(chunk, MXU_TILE) == 0 and n_norm_cols % tn == 0
    kern = functools.partial(_norm_matmul_kernel, n_norm_tiles=n_norm_cols // tn, chunk=chunk)
    return pl.pallas_call(
        kern,
        grid=(m // tm, n // tn),
        in_specs=[
            pl.BlockSpec((tm, k), lambda i, j: (i, 0)),
            pl.BlockSpec((1, k), lambda i, j: (0, 0)),
            pl.BlockSpec((k, tn), lambda i, j: (0, j)),
            pl.BlockSpec((1, tn), lambda i, j: (0, j)),
        ],
        out_specs=pl.BlockSpec((tm, tn), lambda i, j: (i, j)),
        out_shape=jax.ShapeDtypeStruct((m, n), BF16),
        scratch_shapes=[pltpu.VMEM((tm, k), BF16)],
        compiler_params=pltpu.CompilerParams(
            dimension_semantics=("parallel", "arbitrary"), vmem_limit_bytes=VMEM_LIMIT),
        name=name,
    )(x, norm_w.reshape(1, k).astype(F32), w_bf16, gain.reshape(1, n).astype(F32))


def _diff_attn_kernel(lam_ref, q_ref, k_ref, v_ref, g_ref, o_ref,
                      m1_ref, l1_ref, acc1_ref, m2_ref, l2_ref, acc2_ref, *, tq):
    qi = pl.program_id(2)
    stats = ((m1_ref, l1_ref, acc1_ref), (m2_ref, l2_ref, acc2_ref))
    for m_ref, l_ref, acc_ref in stats:
        m_ref[...] = jnp.full(m_ref.shape, NEG, F32)
        l_ref[...] = jnp.zeros(l_ref.shape, F32)
        acc_ref[...] = jnp.zeros(acc_ref.shape, F32)

    def block(j, masked):
        start = pl.multiple_of(j * tq, tq)
        for c, (m_ref, l_ref, acc_ref) in enumerate(stats):
            lo, hi = c * DA_QK_DIM, (c + 1) * DA_QK_DIM
            s = _dot_nt(q_ref[:, lo:hi], k_ref[pl.ds(start, tq), lo:hi])
            if masked:
                row = lax.broadcasted_iota(jnp.int32, s.shape, 0)
                col = lax.broadcasted_iota(jnp.int32, s.shape, 1)
                s = jnp.where(col <= row, s, NEG)
            m_old = m_ref[...]
            m_new = jnp.maximum(m_old, jnp.max(s, axis=-1, keepdims=True))
            p = jnp.exp2(s - m_new)
            alpha = jnp.exp2(m_old - m_new)
            l_ref[...] = alpha * l_ref[...] + jnp.sum(p, axis=-1, keepdims=True)
            acc_ref[...] = alpha * acc_ref[...] + _dot(p.astype(BF16), v_ref[pl.ds(start, tq), :])
            m_ref[...] = m_new

    def off_diag(j, carry):
        block(j, False)
        return carry

    lax.fori_loop(0, qi, off_diag, 0)
    block(qi, True)

    lam = lam_ref[0]
    o = acc1_ref[...] / l1_ref[...] - lam * (acc2_ref[...] / l2_ref[...])
    o_ref[...] = (_rms(o) * g_ref[...]).astype(o_ref.dtype)


def _diff_attn(proj, lam, gain, *, batch, seq, tq):
    t = batch * seq
    nq = seq // tq
    width = 2 * DA_QK_DIM
    k_blk0 = DA_HEADS
    v_blk0 = 2 * DA_HEADS
    kern = functools.partial(_diff_attn_kernel, tq=tq)
    return pl.pallas_call(
        kern,
        grid_spec=pltpu.PrefetchScalarGridSpec(
            num_scalar_prefetch=1,
            grid=(batch, DA_HEADS, nq),
            in_specs=[
                pl.BlockSpec((tq, width), lambda b, h, i, lam: (b * nq + i, h)),
                pl.BlockSpec((seq, width), lambda b, h, i, lam: (b, k_blk0 + h)),
                pl.BlockSpec((seq, width), lambda b, h, i, lam: (b, v_blk0 + h)),
                pl.BlockSpec((1, DA_V_DIM), lambda b, h, i, lam: (0, 0)),
            ],
            out_specs=pl.BlockSpec((tq, DA_V_DIM), lambda b, h, i, lam: (b * nq + i, h)),
            scratch_shapes=[
                pltpu.VMEM((tq, 1), F32), pltpu.VMEM((tq, 1), F32), pltpu.VMEM((tq, DA_V_DIM), F32),
                pltpu.VMEM((tq, 1), F32), pltpu.VMEM((tq, 1), F32), pltpu.VMEM((tq, DA_V_DIM), F32),
            ],
        ),
        out_shape=jax.ShapeDtypeStruct((t, DA_HEADS * DA_V_DIM), BF16),
        compiler_params=pltpu.CompilerParams(
            dimension_semantics=("parallel", "parallel", "arbitrary"), vmem_limit_bytes=VMEM_LIMIT),
        name="diff_attn",
    )(lam, proj, proj, proj, gain)


def _s5_kernel(u_ref, bd_ref, ar_ref, ai_ref, cd_ref, d_ref, o_ref, xs_ref, *, seq, rows):
    ns = SSM_CHUNK_STATE
    seg_len = seq // SCAN_SEGS
    n_row_blk = seq // rows

    def in_map(r, carry):
        r0 = pl.multiple_of(r * rows, rows)
        xs_ref[pl.ds(r0, rows), :] = _dot(u_ref[pl.ds(r0, rows), :], bd_ref[...])
        return carry

    lax.fori_loop(0, n_row_blk, in_map, 0)

    ar = jnp.broadcast_to(ar_ref[...], (SCAN_SEGS, ns))
    ai = jnp.broadcast_to(ai_ref[...], (SCAN_SEGS, ns))

    def advance(t, sr, si):
        t0 = pl.multiple_of(t * SCAN_SEGS, SCAN_SEGS)
        br = xs_ref[pl.ds(t0, SCAN_SEGS), 0:ns]
        bi = xs_ref[pl.ds(t0, SCAN_SEGS), ns:2 * ns]
        return t0, ar * sr - ai * si + br, ar * si + ai * sr + bi

    def local_step(t, carry):
        _, nr, ni = advance(t, *carry)
        return nr, ni

    zero = jnp.zeros((SCAN_SEGS, ns), F32)
    fr, fi = lax.fori_loop(0, seg_len, local_step, (zero, zero))

    pr, pi = ar, ai
    for _ in range(int(math.log2(seg_len))):
        pr, pi = pr * pr - pi * pi, 2.0 * pr * pi
    seg = lax.broadcasted_iota(jnp.int32, (SCAN_SEGS, ns), 0)

    def shifted(x, k):
        return jnp.where(seg >= k, pltpu.roll(x, k, 0), 0.0)

    k = 1
    while k < SCAN_SEGS:
        gr, gi = shifted(fr, k), shifted(fi, k)
        fr, fi = fr + pr * gr - pi * gi, fi + pr * gi + pi * gr
        pr, pi = pr * pr - pi * pi, 2.0 * pr * pi
        k *= 2
    sr0, si0 = shifted(fr, 1), shifted(fi, 1)

    def global_step(t, carry):
        t0, nr, ni = advance(t, *carry)
        xs_ref[pl.ds(t0, SCAN_SEGS), 0:ns] = nr
        xs_ref[pl.ds(t0, SCAN_SEGS), ns:2 * ns] = ni
        return nr, ni

    lax.fori_loop(0, seg_len, global_step, (sr0, si0))

    def out_map(r, carry):
        r0 = pl.multiple_of(r * rows, rows)
        x = xs_ref[pl.ds(r0, rows), :].astype(BF16)
        y = _dot(x, cd_ref[...]) + d_ref[...] * u_ref[pl.ds(r0, rows), :].astype(F32)
        o_ref[pl.ds(r0, rows), :] = jax.nn.gelu(y).astype(o_ref.dtype)
        return carry

    lax.fori_loop(0, n_row_blk, out_map, 0)


def _s5(u_perm, bd, a_re, a_im, cd, d_skip, *, rows=256):
    batch, seq, d_ssm = u_perm.shape
    n_chunks = d_ssm // LANES
    kern = functools.partial(_s5_kernel, seq=seq, rows=rows)
    return pl.pallas_call(
        kern,
        grid=(batch, n_chunks),
        in_specs=[
            pl.BlockSpec((None, seq, LANES), lambda b, c: (b, 0, c)),
            pl.BlockSpec((None, LANES, 2 * SSM_CHUNK_STATE), lambda b, c: (c, 0, 0)),
            pl.BlockSpec((None, 1, SSM_CHUNK_STATE), lambda b, c: (c, 0, 0)),
            pl.BlockSpec((None, 1, SSM_CHUNK_STATE), lambda b, c: (c, 0, 0)),
            pl.BlockSpec((None, 2 * SSM_CHUNK_STATE, LANES), lambda b, c: (c, 0, 0)),
            pl.BlockSpec((None, 1, LANES), lambda b, c: (c, 0, 0)),
        ],
        out_specs=pl.BlockSpec((None, seq, LANES), lambda b, c: (b, 0, c)),
        out_shape=jax.ShapeDtypeStruct((batch, seq, d_ssm), BF16),
        scratch_shapes=[pltpu.VMEM((seq, 2 * SSM_CHUNK_STATE), F32)],
        compiler_params=pltpu.CompilerParams(
            dimension_semantics=("parallel", "parallel"), vmem_limit_bytes=VMEM_LIMIT),
        name="s5_scan",
    )(u_perm, bd, a_re, a_im, cd, d_skip)


def _s5_params(lam_re, lam_im, log_dt, b_re, b_im, c_re, c_im, d_skip):
    g = lam_re.shape[0]
    nc = g // SSM_CHUNK_GROUPS
    lr = jnp.minimum(lam_re.astype(F32), -1e-4)
    li = lam_im.astype(F32)
    dt = jnp.exp(log_dt.astype(F32))[:, None]
    mag = jnp.exp(lr * dt)
    lb_re, lb_im = mag * jnp.cos(li * dt), mag * jnp.sin(li * dt)
    den = lr * lr + li * li
    coef_re = ((lb_re - 1.0) * lr + lb_im * li) / den
    coef_im = (lb_im * lr - (lb_re - 1.0) * li) / den
    br, bi = b_re.astype(F32), b_im.astype(F32)
    bb_re = coef_re[..., None] * br - coef_im[..., None] * bi
    bb_im = coef_re[..., None] * bi + coef_im[..., None] * br
    eye = jnp.eye(SSM_CHUNK_GROUPS, dtype=F32)

    def pack_in(bb):
        bb = bb.reshape(nc, SSM_CHUNK_GROUPS, SSM_STATE, SSM_GROUP)
        return jnp.einsum('cgph,gk->cghkp', bb, eye).reshape(nc, LANES, SSM_CHUNK_STATE)

    def pack_out(cc):
        cc = cc.astype(F32).reshape(nc, SSM_CHUNK_GROUPS, SSM_GROUP, SSM_STATE)
        return jnp.einsum('cghp,gk->ckpgh', cc, eye).reshape(nc, SSM_CHUNK_STATE, LANES)

    bd = jnp.concatenate([pack_in(bb_re), pack_in(bb_im)], axis=-1).astype(BF16)
    cd = jnp.concatenate([pack_out(c_re), -pack_out(c_im)], axis=1).astype(BF16)
    a_re = lb_re.reshape(nc, 1, SSM_CHUNK_STATE)
    a_im = lb_im.reshape(nc, 1, SSM_CHUNK_STATE)
    dd = d_skip.astype(F32).reshape(nc, 1, LANES)
    return bd, a_re, a_im, cd, dd


def _mix_out_kernel(a_ref, y_ref, x_ref, gw_ref, gb_ref, nw_ref, wo_ref, o_ref):
    d_attn = a_ref.shape[1]
    y = y_ref[...]
    gate = _dot(y, gw_ref[...]) + gb_ref[...]
    s = y.astype(F32) * jax.nn.sigmoid(gate)
    sn = (_rms(s) * nw_ref[...]).astype(BF16)
    acc = _dot(a_ref[...], wo_ref[0:d_attn, :]) + _dot(sn, wo_ref[d_attn:, :])
    o_ref[...] = x_ref[...] + acc


def _mix_out(a, y, x, glu_w, glu_b, norm_w, w_out, *, tm):
    t, d = x.shape
    d_attn, d_ssm = a.shape[1], y.shape[1]
    const = lambda i: (0, 0)
    return pl.pallas_call(
        _mix_out_kernel,
        grid=(t // tm,),
        in_specs=[
            pl.BlockSpec((tm, d_attn), lambda i: (i, 0)),
            pl.BlockSpec((tm, d_ssm), lambda i: (i, 0)),
            pl.BlockSpec((tm, d), lambda i: (i, 0)),
            pl.BlockSpec((d_ssm, d_ssm), const),
            pl.BlockSpec((1, d_ssm), const),
            pl.BlockSpec((1, d_ssm), const),
            pl.BlockSpec((d, d), const),
        ],
        out_specs=pl.BlockSpec((tm, d), lambda i: (i, 0)),
        out_shape=jax.ShapeDtypeStruct((t, d), F32),
        compiler_params=pltpu.CompilerParams(
            dimension_semantics=("parallel",), vmem_limit_bytes=VMEM_LIMIT),
        name="mix_out",
    )(a, y, x, glu_w, glu_b.reshape(1, d_ssm).astype(F32), norm_w.reshape(1, d_ssm).astype(F32), w_out)


def _xattn_route_kernel(q_ref, k_ref, v_ref, h_ref, xo_ref, nw_ref, rhi_ref, rlo_ref, rb_ref,
                        h2_ref, hn_ref, eid_ref, wts_ref):
    d = h_ref.shape[1]
    hd = d // X_HEADS
    parts = []
    for h in range(X_HEADS):
        sl = slice(h * hd, (h + 1) * hd)
        s = _dot_nt(q_ref[:, sl], k_ref[:, sl])
        p = jnp.exp(s - jnp.max(s, axis=-1, keepdims=True))
        p = p / jnp.sum(p, axis=-1, keepdims=True)
        parts.append(_dot(p.astype(BF16), v_ref[:, sl]))
    o = jnp.concatenate(parts, axis=-1).astype(BF16)
    h2 = h_ref[...] + _dot(o, xo_ref[...])
    h2_ref[...] = h2
    hn = _rms(h2) * nw_ref[...]
    hn_ref[...] = hn

    hi = hn.astype(BF16)
    lo = (hn - hi.astype(F32)).astype(BF16)
    logits = (_dot(hi, rhi_ref[...]) + _dot(hi, rlo_ref[...]) + _dot(lo, rhi_ref[...])) + rb_ref[...]

    lane = lax.broadcasted_iota(jnp.int32, logits.shape, 1)
    big = jnp.int32(ROUTE_LANES)

    def first_lane(cond):
        return jnp.min(jnp.where(cond, lane, big), axis=-1, keepdims=True)

    c_mask = lane < MOE_GROUPS
    lc = jnp.where(c_mask, logits, NEG)
    mc = jnp.max(lc, axis=-1, keepdims=True)
    ec = jnp.exp(lc - mc)
    p_c = ec / jnp.sum(ec, axis=-1, keepdims=True)
    p_grp = jnp.max(p_c, axis=-1, keepdims=True)
    grp = first_lane(c_mask & (p_c == p_grp))
    f_lo = MOE_GROUPS + grp * EXP_PER_GROUP
    f_mask = (lane >= f_lo) & (lane < f_lo + EXP_PER_GROUP)
    lf = jnp.where(f_mask, logits, NEG)
    mf = jnp.max(lf, axis=-1, keepdims=True)
    ef = jnp.exp(lf - mf)
    pf = ef / jnp.sum(ef, axis=-1, keepdims=True)
    v1 = jnp.max(jnp.where(f_mask, pf, -1.0), axis=-1, keepdims=True)
    i1 = first_lane(f_mask & (pf == v1))
    rest = f_mask & (lane != i1)
    v2 = jnp.max(jnp.where(rest, pf, -1.0), axis=-1, keepdims=True)
    i2 = first_lane(rest & (pf == v2))
    tot = v1 + v2
    w1 = v1 / tot * p_grp
    w2 = v2 / tot * p_grp
    eid_ref[...] = jnp.where(lane == 0, i1 - MOE_GROUPS, jnp.where(lane == 1, i2 - MOE_GROUPS, 0))
    wts_ref[...] = jnp.where(lane == 0, w1, jnp.where(lane == 1, w2, 0.0))


def _xattn_route(q, kv, h1, xo_w, norm_w, r_hi, r_lo, r_b, *, batch, seq, mem_len, tm):
    t, d = h1.shape
    n = seq // tm
    const = lambda b, i: (0, 0)
    row = lambda b, i: (b * n + i, 0)
    return pl.pallas_call(
        _xattn_route_kernel,
        grid=(batch, n),
        in_specs=[
            pl.BlockSpec((tm, d), row),
            pl.BlockSpec((mem_len, d), lambda b, i: (b, 0)),
            pl.BlockSpec((mem_len, d), lambda b, i: (b, 1)),
            pl.BlockSpec((tm, d), row),
            pl.BlockSpec((d, d), const),
            pl.BlockSpec((1, d), const),
            pl.BlockSpec((d, ROUTE_LANES), const),
            pl.BlockSpec((d, ROUTE_LANES), const),
            pl.BlockSpec((1, ROUTE_LANES), const),
        ],
        out_specs=[
            pl.BlockSpec((tm, d), row),
            pl.BlockSpec((tm, d), row),
            pl.BlockSpec((tm, ROUTE_LANES), row),
            pl.BlockSpec((tm, ROUTE_LANES), row),
        ],
        out_shape=[
            jax.ShapeDtypeStruct((t, d), F32),
            jax.ShapeDtypeStruct((t, d), F32),
            jax.ShapeDtypeStruct((t, ROUTE_LANES), jnp.int32),
            jax.ShapeDtypeStruct((t, ROUTE_LANES), F32),
        ],
        compiler_params=pltpu.CompilerParams(
            dimension_semantics=("parallel", "parallel"), vmem_limit_bytes=VMEM_LIMIT),
        name="xattn_route",
    )(q, kv, kv, h1, xo_w, norm_w.reshape(1, d).astype(F32), r_hi, r_lo, r_b)


def _moe_kernel(be_ref, par_ref, first_ref, nxt_ref, base_ref, nval_ref, tok_ref, nu_ref,
                hn_hbm, wg_hbm, wu_hbm, wd_hbm, o_ref,
                xbuf, wgb, wub, wdb, gsem, wsem):
    b = pl.program_id(0)
    n_used = nu_ref[0]

    def weight_copies(e, slot):
        return (pltpu.make_async_copy(wg_hbm.at[e], wgb.at[slot], wsem.at[slot]),
                pltpu.make_async_copy(wu_hbm.at[e], wub.at[slot], wsem.at[slot]),
                pltpu.make_async_copy(wd_hbm.at[e], wdb.at[slot], wsem.at[slot]))

    def groups(blk):
        return (nval_ref[blk] + ROW_GROUP - 1) // ROW_GROUP

    def start_gather(blk, slot):
        base = base_ref[blk]

        def body(g, carry):
            for r in range(ROW_GROUP):
                tok = tok_ref[base + g * ROW_GROUP + r]
                pltpu.make_async_copy(hn_hbm.at[pl.ds(tok, 1)], xbuf.at[slot, g, pl.ds(r, 1)],
                                      gsem.at[slot]).start()
            return carry
        lax.fori_loop(0, groups(blk), body, 0)

    def wait_gather(blk, slot):
        filled = xbuf.at[slot, pl.ds(0, groups(blk))]
        pltpu.make_async_copy(filled, filled, gsem.at[slot]).wait()

    @pl.when(b == 0)
    def _():
        xbuf[...] = jnp.zeros(xbuf.shape, xbuf.dtype)
        for c in weight_copies(be_ref[0], par_ref[0]):
            c.start()
        start_gather(0, 0)

    @pl.when(b < n_used)
    def _():
        slot = b % 2
        wslot = par_ref[b]
        is_first = first_ref[b] == 1

        @pl.when(is_first & (nxt_ref[b] >= 0))
        def _():
            for c in weight_copies(nxt_ref[b], 1 - wslot):
                c.start()

        @pl.when(b + 1 < n_used)
        def _():
            start_gather(b + 1, 1 - slot)

        @pl.when(is_first)
        def _():
            for c in weight_copies(0, wslot):
                c.wait()

        wait_gather(b, slot)
        x = xbuf[slot].reshape(MOE_BLOCK, xbuf.shape[-1]).astype(BF16)
        g = _dot(x, wgb[wslot].astype(BF16))
        u = _dot(x, wub[wslot].astype(BF16))
        mid = (jax.nn.silu(g) * u).astype(BF16)
        o_ref[...] = _dot(mid, wdb[wslot].astype(BF16))

    @pl.when(b >= n_used)
    def _():
        o_ref[...] = jnp.zeros(o_ref.shape, o_ref.dtype)


def _moe_experts(hn, w_gate, w_up, w_down, meta, tok_sorted, n_used):
    t, d = hn.shape
    d_ff = w_gate.shape[2]
    blk_exp, par, first, nxt, base, nval = meta
    n_blocks = blk_exp.shape[0]
    any_spec = pl.BlockSpec(memory_space=pl.ANY)
    return pl.pallas_call(
        _moe_kernel,
        grid_spec=pltpu.PrefetchScalarGridSpec(
            num_scalar_prefetch=8,
            grid=(n_blocks,),
            in_specs=[any_spec, any_spec, any_spec, any_spec],
            out_specs=pl.BlockSpec((MOE_BLOCK, d), lambda b, *_: (b, 0)),
            scratch_shapes=[
                pltpu.VMEM((2, MOE_BLOCK // ROW_GROUP, ROW_GROUP, d), F32),
                pltpu.VMEM((2, d, d_ff), F32),
                pltpu.VMEM((2, d, d_ff), F32),
                pltpu.VMEM((2, d_ff, d), F32),
                pltpu.SemaphoreType.DMA((2,)),
                pltpu.SemaphoreType.DMA((2,)),
            ],
        ),
        out_shape=jax.ShapeDtypeStruct((n_blocks * MOE_BLOCK, d), F32),
        compiler_params=pltpu.CompilerParams(
            dimension_semantics=("arbitrary",), vmem_limit_bytes=VMEM_LIMIT),
        name="moe_experts",
    )(blk_exp, par, first, nxt, base, nval, tok_sorted, n_used, hn, w_gate, w_up, w_down)


def _combine_kernel(pos_ref, y_hbm, h_ref, w_ref, o_ref, ybuf, sem, *, tm):
    i = pl.program_id(0)
    n_groups = tm // ROW_GROUP

    def start_gather(tile, slot):
        base = tile * (tm * TOP_K_FINE)

        def body(g, carry):
            for r in range(ROW_GROUP):
                for k in range(TOP_K_FINE):
                    row = pos_ref[base + (g * ROW_GROUP + r) * TOP_K_FINE + k]
                    pltpu.make_async_copy(y_hbm.at[pl.ds(row, 1)], ybuf.at[slot, k, g, pl.ds(r, 1)],
                                          sem.at[slot]).start()
            return carry
        lax.fori_loop(0, n_groups, body, 0)

    @pl.when(i == 0)
    def _():
        start_gather(0, 0)

    @pl.when(i + 1 < pl.num_programs(0))
    def _():
        start_gather(i + 1, (i + 1) % 2)

    slot = i % 2
    pltpu.make_async_copy(ybuf.at[slot], ybuf.at[slot], sem.at[slot]).wait()
    w = w_ref[...]
    d = o_ref.shape[1]
    y0 = ybuf[slot, 0].reshape(tm, d)
    y1 = ybuf[slot, 1].reshape(tm, d)
    o_ref[...] = h_ref[...] + (w[:, 0:1] * y0 + w[:, 1:2] * y1)


def _combine(y, h2, wts, pos, *, tm):
    t, d = h2.shape
    kern = functools.partial(_combine_kernel, tm=tm)
    return pl.pallas_call(
        kern,
        grid_spec=pltpu.PrefetchScalarGridSpec(
            num_scalar_prefetch=1,
            grid=(t // tm,),
            in_specs=[
                pl.BlockSpec(memory_space=pl.ANY),
                pl.BlockSpec((tm, d), lambda i, pos: (i, 0)),
                pl.BlockSpec((tm, ROUTE_LANES), lambda i, pos: (i, 0)),
            ],
            out_specs=pl.BlockSpec((tm, d), lambda i, pos: (i, 0)),
            scratch_shapes=[
                pltpu.VMEM((2, TOP_K_FINE, tm // ROW_GROUP, ROW_GROUP, d), F32),
                pltpu.SemaphoreType.DMA((2,)),
            ],
        ),
        out_shape=jax.ShapeDtypeStruct((t, d), F32),
        compiler_params=pltpu.CompilerParams(
            dimension_semantics=("arbitrary",), vmem_limit_bytes=VMEM_LIMIT),
        name="moe_combine",
    )(pos, y, h2, wts)


def _lookup(table, idx):
    sel = idx[:, None] == jnp.arange(table.shape[0], dtype=jnp.int32)[None, :]
    return jnp.sum(jnp.where(sel, table[None, :], 0), axis=1).astype(jnp.int32)


def _dispatch(eid, n_tokens):
    n_assign = n_tokens * TOP_K_FINE
    experts = jnp.arange(N_EXPERTS, dtype=jnp.int32)
    e_flat = eid.reshape(n_assign)
    a_ids = jnp.arange(n_assign, dtype=jnp.int32)
    e_s, order = lax.sort_key_val(e_flat, a_ids)
    counts = jnp.sum((e_flat[:, None] == experts[None, :]).astype(jnp.int32), axis=0)
    starts = jnp.cumsum(counts) - counts
    nb = (counts + MOE_BLOCK - 1) // MOE_BLOCK
    blk_end = jnp.cumsum(nb)
    blk_start = blk_end - nb
    n_used = blk_end[-1]
    n_blocks = (n_assign + N_EXPERTS * (MOE_BLOCK - 1)) // MOE_BLOCK
    b_ids = jnp.arange(n_blocks, dtype=jnp.int32)
    used = b_ids < n_used
    blk_exp = jnp.minimum(jnp.sum((blk_end[None, :] <= b_ids[:, None]).astype(jnp.int32), axis=1),
                          N_EXPERTS - 1)
    j = b_ids - _lookup(blk_start, blk_exp)
    base = jnp.where(used, _lookup(starts, blk_exp) + j * MOE_BLOCK, 0)
    nval = jnp.where(used, jnp.clip(_lookup(counts, blk_exp) - j * MOE_BLOCK, 0, MOE_BLOCK), 0)
    first = (used & (j == 0)).astype(jnp.int32)
    active = counts > 0
    par = _lookup(jnp.cumsum(active.astype(jnp.int32)) - 1, blk_exp) & 1
    later = lax.cummin(jnp.where(active, experts, N_EXPERTS), reverse=True)
    nxt_e = jnp.concatenate([later[1:], jnp.full((1,), N_EXPERTS, jnp.int32)])
    nxt = _lookup(jnp.where(nxt_e == N_EXPERTS, -1, nxt_e), blk_exp)
    meta = tuple(v.astype(jnp.int32) for v in (blk_exp, par, first, nxt, base, nval))
    row_sorted = a_ids + _lookup(blk_start * MOE_BLOCK - starts, e_s)
    _, pos = lax.sort_key_val(order, row_sorted)
    tok_sorted = jnp.concatenate([lax.shift_right_logical(order, 1), jnp.zeros((ROW_GROUP,), jnp.int32)])
    return meta, tok_sorted, pos, n_used.astype(jnp.int32).reshape(1)


def kernel(x, mem, norm1_w, w_in, q_norm_w, k_norm_w, lambda_q1, lambda_k1, lambda_q2, lambda_k2, subln_w, ssm_lambda_re, ssm_lambda_im, ssm_log_dt, ssm_b_re, ssm_b_im, ssm_c_re, ssm_c_im, ssm_d, ssm_glu_w, ssm_glu_b, ssm_out_norm_w, w_out, norm2_w, mem_norm_w, xq_w, xkv_w, xq_norm_w, xk_norm_w, xo_w, norm3_w, router_coarse_w, router_coarse_b, router_fine_w, router_fine_b, expert_w_gate, expert_w_up, expert_w_down):
    batch, seq, d = x.shape
    mem_len = mem.shape[1]
    t = batch * seq
    depth = norm1_w.shape[0]
    d_attn = DA_HEADS * DA_V_DIM
    d_ssm = d - d_attn
    qk_cols = DA_HEADS * 2 * DA_QK_DIM
    x_hd = d // X_HEADS
    h = x.reshape(t, d)
    mem2 = mem.reshape(batch * mem_len, d)

    for l in range(depth):
        lam_init = 0.8 - 0.6 * math.exp(-0.3 * l)
        lam = (jnp.exp(jnp.sum(lambda_q1[l].astype(F32) * lambda_k1[l].astype(F32)))
               - jnp.exp(jnp.sum(lambda_q2[l].astype(F32) * lambda_k2[l].astype(F32)))
               + lam_init).reshape(1)

        n_rep = qk_cols // DA_QK_DIM
        in_gain = jnp.concatenate([
            jnp.tile(q_norm_w[l].astype(F32) * (DA_QK_DIM ** -0.5 * LOG2E), n_rep),
            jnp.tile(k_norm_w[l].astype(F32), n_rep),
            jnp.ones((d_attn + d_ssm,), F32)])
        proj = _norm_matmul(h, norm1_w[l], w_in[l].astype(BF16), in_gain,
                            n_norm_cols=2 * qk_cols, chunk=DA_QK_DIM, tm=1024, tn=1024, name="in_proj")
        sub_gain = (subln_w[l].astype(F32) * (1.0 - lam_init)).reshape(1, DA_V_DIM)
        a = _diff_attn(proj, lam, sub_gain, batch=batch, seq=seq, tq=512)

        seg_len = seq // SCAN_SEGS
        u = proj[:, 2 * qk_cols + d_attn:]
        u_perm = u.reshape(batch, SCAN_SEGS, seg_len, d_ssm).transpose(0, 2, 1, 3).reshape(batch, seq, d_ssm)
        bd, a_re, a_im, cd, dd = _s5_params(ssm_lambda_re[l], ssm_lambda_im[l], ssm_log_dt[l],
                                            ssm_b_re[l], ssm_b_im[l], ssm_c_re[l], ssm_c_im[l], ssm_d[l])
        y_perm = _s5(u_perm, bd, a_re, a_im, cd, dd)
        y = y_perm.reshape(batch, seg_len, SCAN_SEGS, d_ssm).transpose(0, 2, 1, 3).reshape(t, d_ssm)
        h = _mix_out(a, y, h, ssm_glu_w[l].astype(BF16), ssm_glu_b[l], ssm_out_norm_w[l],
                     w_out[l].astype(BF16), tm=256)

        kv_gain = jnp.concatenate([jnp.tile(xk_norm_w[l].astype(F32), X_HEADS), jnp.ones((d,), F32)])
        kv = _norm_matmul(mem2, mem_norm_w[l], xkv_w[l].astype(BF16), kv_gain,
                          n_norm_cols=d, chunk=x_hd, tm=512, tn=512, name="kv_proj")
        q_gain = jnp.tile(xq_norm_w[l].astype(F32) * (x_hd ** -0.5), X_HEADS)
        q = _norm_matmul(h, norm2_w[l], xq_w[l].astype(BF16), q_gain,
                         n_norm_cols=d, chunk=x_hd, tm=1024, tn=1024, name="xq_proj")
        r_w = jnp.concatenate([router_coarse_w[l].astype(F32), router_fine_w[l].astype(F32)], axis=1)
        r_w = jnp.pad(r_w, ((0, 0), (0, ROUTE_LANES - r_w.shape[1])))
        r_hi = r_w.astype(BF16)
        r_lo = (r_w - r_hi.astype(F32)).astype(BF16)
        r_b = jnp.concatenate([router_coarse_b[l].astype(F32), router_fine_b[l].astype(F32)])
        r_b = jnp.pad(r_b, (0, ROUTE_LANES - r_b.shape[0])).reshape(1, ROUTE_LANES)
        h2, hn3, eid, wts = _xattn_route(q, kv, h, xo_w[l].astype(BF16), norm3_w[l], r_hi, r_lo, r_b,
                                         batch=batch, seq=seq, mem_len=mem_len, tm=256)

        meta, tok_sorted, pos, n_used = _dispatch(eid[:, :TOP_K_FINE], t)
        y = _moe_experts(hn3, expert_w_gate[l], expert_w_up[l], expert_w_down[l], meta, tok_sorted, n_used)
        h = _combine(y, h2, wts, pos, tm=256)

    return h.reshape(batch, seq, d)
```

```python
import functools
import math

import jax
import jax.numpy as jnp
from jax import lax
from jax.experimental import pallas as pl
from jax.experimental.pallas import tpu as pltpu

F32 = jnp.float32
BF16 = jnp.bfloat16

EPS = 1e-6
DA_HEADS = 4
DA_QK_DIM = 128
DA_V_DIM = 256
SSM_GROUP = 16
SSM_STATE = 64
X_HEADS = 4
MOE_GROUPS = 8
EXP_PER_GROUP = 8
N_EXPERTS = MOE_GROUPS * EXP_PER_GROUP
TOP_K_FINE = 2

LANES = 128
SUBLANES = 8
MXU_TILE = 256
VMEM_LIMIT = 56 * 1024 * 1024
NEG = -1e30
LOG2E = math.log2(math.e)

SSM_CHUNK_GROUPS = LANES // SSM_GROUP
SSM_CHUNK_STATE = SSM_CHUNK_GROUPS * SSM_STATE
SCAN_SEGS = SUBLANES
MOE_BLOCK = 256
ROW_GROUP = SUBLANES
WEIGHT_DMA_CHUNKS = 4
ROUTE_LANES = LANES


def _rms(x, eps=EPS):
    return x * lax.rsqrt(jnp.mean(x * x, axis=-1, keepdims=True) + eps)


def _dot(a, b):
    return jnp.dot(a, b, preferred_element_type=F32)


def _dot_nt(a, b):
    return lax.dot_general(a, b, (((1,), (1,)), ((), ())), preferred_element_type=F32)


def _resident(shape, index_map):
    return pl.BlockSpec(shape, index_map, pipeline_mode=pl.Buffered(1))


def _norm_matmul_kernel(x_ref, nw_ref, w_ref, g_ref, o_ref, xn_ref, *, n_norm_tiles, chunk):
    j = pl.program_id(1)

    @pl.when(j == 0)
    def _():
        x = x_ref[...].astype(F32)
        xn_ref[...] = (_rms(x) * nw_ref[...]).astype(BF16)

    normed = j < n_norm_tiles
    tn = w_ref.shape[1]
    sub = max(chunk, MXU_TILE)
    for s in range(tn // sub):
        acc = _dot(xn_ref[...], w_ref[:, s * sub:(s + 1) * sub])
        for c in range(sub // chunk):
            lo = s * sub + c * chunk
            a = acc[:, c * chunk:(c + 1) * chunk]
            inv = lax.rsqrt(jnp.mean(a * a, axis=-1, keepdims=True) + EPS)
            scale = jnp.where(normed, inv, 1.0)
            o_ref[:, lo:lo + chunk] = (a * scale * g_ref[:, lo:lo + chunk]).astype(o_ref.dtype)


def _norm_matmul(x, norm_w, w_bf16, gain, *, n_norm_cols, chunk, tm, tn, name):
    m, k = x.shape
    n = w_bf16.shape[1]
    assert m % tm == 0 and n % tn == 0 and tn % max(chunk, MXU_TILE) == 0 and n_norm_cols % tn == 0
    kern = functools.partial(_norm_matmul_kernel, n_norm_tiles=n_norm_cols // tn, chunk=chunk)
    return pl.pallas_call(
        kern,
        grid=(m // tm, n // tn),
        in_specs=[
            pl.BlockSpec((tm, k), lambda i, j: (i, 0)),
            pl.BlockSpec((1, k), lambda i, j: (0, 0)),
            pl.BlockSpec((k, tn), lambda i, j: (0, j)),
            pl.BlockSpec((1, tn), lambda i, j: (0, j)),
        ],
        out_specs=pl.BlockSpec((tm, tn), lambda i, j: (i, j)),
        out_shape=jax.ShapeDtypeStruct((m, n), BF16),
        scratch_shapes=[pltpu.VMEM((tm, k), BF16)],
        compiler_params=pltpu.CompilerParams(
            dimension_semantics=("parallel", "arbitrary"), vmem_limit_bytes=VMEM_LIMIT),
        name=name,
    )(x, norm_w.reshape(1, k).astype(F32), w_bf16, gain.reshape(1, n).astype(F32))


def _diff_attn_kernel(lam_ref, q_ref, k_ref, v_ref, g_ref, o_ref,
                      m1_ref, l1_ref, acc1_ref, m2_ref, l2_ref, acc2_ref, *, tq):
    qi = pl.program_id(2)
    stats = ((m1_ref, l1_ref, acc1_ref), (m2_ref, l2_ref, acc2_ref))
    for m_ref, l_ref, acc_ref in stats:
        m_ref[...] = jnp.full(m_ref.shape, NEG, F32)
        l_ref[...] = jnp.zeros(l_ref.shape, F32)
        acc_ref[...] = jnp.zeros(acc_ref.shape, F32)

    def block(j, masked):
        start = pl.multiple_of(j * tq, tq)
        for c, (m_ref, l_ref, acc_ref) in enumerate(stats):
            lo, hi = c * DA_QK_DIM, (c + 1) * DA_QK_DIM
            s = _dot_nt(q_ref[:, lo:hi], k_ref[pl.ds(start, tq), lo:hi])
            if masked:
                row = lax.broadcasted_iota(jnp.int32, s.shape, 0)
                col = lax.broadcasted_iota(jnp.int32, s.shape, 1)
                s = jnp.where(col <= row, s, NEG)
            m_old = m_ref[...]
            m_new = jnp.maximum(m_old, jnp.max(s, axis=-1, keepdims=True))
            p = jnp.exp2(s - m_new)
            alpha = jnp.exp2(m_old - m_new)
            l_ref[...] = alpha * l_ref[...] + jnp.sum(p, axis=-1, keepdims=True)
            acc_ref[...] = alpha * acc_ref[...] + _dot(p.astype(BF16), v_ref[pl.ds(start, tq), :])
            m_ref[...] = m_new

    def off_diag(j, carry):
        block(j, False)
        return carry

    lax.fori_loop(0, qi, off_diag, 0)
    block(qi, True)

    lam = lam_ref[0]
    o = acc1_ref[...] / l1_ref[...] - lam * (acc2_ref[...] / l2_ref[...])
    o_ref[...] = (_rms(o) * g_ref[...]).astype(o_ref.dtype)


def _diff_attn(proj, lam, gain, *, batch, seq, tq):
    t = batch * seq
    nq = seq // tq
    width = 2 * DA_QK_DIM
    k_blk0 = DA_HEADS
    v_blk0 = 2 * DA_HEADS
    kern = functools.partial(_diff_attn_kernel, tq=tq)
    return pl.pallas_call(
        kern,
        grid_spec=pltpu.PrefetchScalarGridSpec(
            num_scalar_prefetch=1,
            grid=(batch, DA_HEADS, nq),
            in_specs=[
                pl.BlockSpec((tq, width), lambda b, h, i, lam: (b * nq + i, h)),
                pl.BlockSpec((seq, width), lambda b, h, i, lam: (b, k_blk0 + h)),
                pl.BlockSpec((seq, width), lambda b, h, i, lam: (b, v_blk0 + h)),
                pl.BlockSpec((1, DA_V_DIM), lambda b, h, i, lam: (0, 0)),
            ],
            out_specs=pl.BlockSpec((tq, DA_V_DIM), lambda b, h, i, lam: (b * nq + i, h)),
            scratch_shapes=[
                pltpu.VMEM((tq, 1), F32), pltpu.VMEM((tq, 1), F32), pltpu.VMEM((tq, DA_V_DIM), F32),
                pltpu.VMEM((tq, 1), F32), pltpu.VMEM((tq, 1), F32), pltpu.VMEM((tq, DA_V_DIM), F32),
            ],
        ),
        out_shape=jax.ShapeDtypeStruct((t, DA_HEADS * DA_V_DIM), BF16),
        compiler_params=pltpu.CompilerParams(
            dimension_semantics=("parallel", "parallel", "arbitrary"), vmem_limit_bytes=VMEM_LIMIT),
        name="diff_attn",
    )(lam, proj, proj, proj, gain)


def _s5_kernel(u_ref, bd_ref, ar_ref, ai_ref, cd_ref, d_ref, o_ref, xs_ref, *, seq, rows):
    ns = SSM_CHUNK_STATE
    seg_len = seq // SCAN_SEGS
    n_row_blk = seq // rows

    def in_map(r, carry):
        r0 = pl.multiple_of(r * rows, rows)
        xs_ref[pl.ds(r0, rows), :] = _dot(u_ref[pl.ds(r0, rows), :], bd_ref[...])
        return carry

    lax.fori_loop(0, n_row_blk, in_map, 0)

    ar = jnp.broadcast_to(ar_ref[...], (SCAN_SEGS, ns))
    ai = jnp.broadcast_to(ai_ref[...], (SCAN_SEGS, ns))

    def advance(t, sr, si):
        t0 = pl.multiple_of(t * SCAN_SEGS, SCAN_SEGS)
        br = xs_ref[pl.ds(t0, SCAN_SEGS), 0:ns]
        bi = xs_ref[pl.ds(t0, SCAN_SEGS), ns:2 * ns]
        return t0, ar * sr - ai * si + br, ar * si + ai * sr + bi

    def local_step(t, carry):
        _, nr, ni = advance(t, *carry)
        return nr, ni

    zero = jnp.zeros((SCAN_SEGS, ns), F32)
    fr, fi = lax.fori_loop(0, seg_len, local_step, (zero, zero))

    pr, pi = ar, ai
    for _ in range(int(math.log2(seg_len))):
        pr, pi = pr * pr - pi * pi, 2.0 * pr * pi
    seg = lax.broadcasted_iota(jnp.int32, (SCAN_SEGS, ns), 0)

    def shifted(x, k):
        return jnp.where(seg >= k, pltpu.roll(x, k, 0), 0.0)

    k = 1
    while k < SCAN_SEGS:
        gr, gi = shifted(fr, k), shifted(fi, k)
        fr, fi = fr + pr * gr - pi * gi, fi + pr * gi + pi * gr
        pr, pi = pr * pr - pi * pi, 2.0 * pr * pi
        k *= 2
    sr0, si0 = shifted(fr, 1), shifted(fi, 1)

    def global_step(t, carry):
        t0, nr, ni = advance(t, *carry)
        xs_ref[pl.ds(t0, SCAN_SEGS), 0:ns] = nr
        xs_ref[pl.ds(t0, SCAN_SEGS), ns:2 * ns] = ni
        return nr, ni

    lax.fori_loop(0, seg_len, global_step, (sr0, si0))

    def out_map(r, carry):
        r0 = pl.multiple_of(r * rows, rows)
        x = xs_ref[pl.ds(r0, rows), :].astype(BF16)
        y = _dot(x, cd_ref[...]) + d_ref[...] * u_ref[pl.ds(r0, rows), :].astype(F32)
        o_ref[pl.ds(r0, rows), :] = jax.nn.gelu(y).astype(o_ref.dtype)
        return carry

    lax.fori_loop(0, n_row_blk, out_map, 0)


def _s5(u_perm, bd, a_re, a_im, cd, d_skip, *, rows=256):
    batch, seq, d_ssm = u_perm.shape
    n_chunks = d_ssm // LANES
    kern = functools.partial(_s5_kernel, seq=seq, rows=rows)
    return pl.pallas_call(
        kern,
        grid=(batch, n_chunks),
        in_specs=[
            pl.BlockSpec((None, seq, LANES), lambda b, c: (b, 0, c)),
            pl.BlockSpec((None, LANES, 2 * SSM_CHUNK_STATE), lambda b, c: (c, 0, 0)),
            pl.BlockSpec((None, 1, SSM_CHUNK_STATE), lambda b, c: (c, 0, 0)),
            pl.BlockSpec((None, 1, SSM_CHUNK_STATE), lambda b, c: (c, 0, 0)),
            pl.BlockSpec((None, 2 * SSM_CHUNK_STATE, LANES), lambda b, c: (c, 0, 0)),
            pl.BlockSpec((None, 1, LANES), lambda b, c: (c, 0, 0)),
        ],
        out_specs=pl.BlockSpec((None, seq, LANES), lambda b, c: (b, 0, c)),
        out_shape=jax.ShapeDtypeStruct((batch, seq, d_ssm), BF16),
        scratch_shapes=[pltpu.VMEM((seq, 2 * SSM_CHUNK_STATE), F32)],
        compiler_params=pltpu.CompilerParams(
            dimension_semantics=("parallel", "parallel"), vmem_limit_bytes=VMEM_LIMIT),
        name="s5_scan",
    )(u_perm, bd, a_re, a_im, cd, d_skip)


def _s5_params(lam_re, lam_im, log_dt, b_re, b_im, c_re, c_im, d_skip):
    g = lam_re.shape[0]
    nc = g // SSM_CHUNK_GROUPS
    lr = jnp.minimum(lam_re.astype(F32), -1e-4)
    li = lam_im.astype(F32)
    dt = jnp.exp(log_dt.astype(F32))[:, None]
    mag = jnp.exp(lr * dt)
    lb_re, lb_im = mag * jnp.cos(li * dt), mag * jnp.sin(li * dt)
    den = lr * lr + li * li
    coef_re = ((lb_re - 1.0) * lr + lb_im * li) / den
    coef_im = (lb_im * lr - (lb_re - 1.0) * li) / den
    br, bi = b_re.astype(F32), b_im.astype(F32)
    bb_re = coef_re[..., None] * br - coef_im[..., None] * bi
    bb_im = coef_re[..., None] * bi + coef_im[..., None] * br
    eye = jnp.eye(SSM_CHUNK_GROUPS, dtype=F32)

    def pack_in(bb):
        bb = bb.reshape(nc, SSM_CHUNK_GROUPS, SSM_STATE, SSM_GROUP)
        return jnp.einsum('cgph,gk->cghkp', bb, eye).reshape(nc, LANES, SSM_CHUNK_STATE)

    def pack_out(cc):
        cc = cc.astype(F32).reshape(nc, SSM_CHUNK_GROUPS, SSM_GROUP, SSM_STATE)
        return jnp.einsum('cghp,gk->ckpgh', cc, eye).reshape(nc, SSM_CHUNK_STATE, LANES)

    bd = jnp.concatenate([pack_in(bb_re), pack_in(bb_im)], axis=-1).astype(BF16)
    cd = jnp.concatenate([pack_out(c_re), -pack_out(c_im)], axis=1).astype(BF16)
    a_re = lb_re.reshape(nc, 1, SSM_CHUNK_STATE)
    a_im = lb_im.reshape(nc, 1, SSM_CHUNK_STATE)
    dd = d_skip.astype(F32).reshape(nc, 1, LANES)
    return bd, a_re, a_im, cd, dd


def _mix_out_kernel(a_ref, y_ref, x_ref, gw_ref, gb_ref, nw_ref, wo_ref, o_ref):
    d_attn = a_ref.shape[1]
    y = y_ref[...]
    gate = _dot(y, gw_ref[...]) + gb_ref[...]
    s = y.astype(F32) * jax.nn.sigmoid(gate)
    sn = (_rms(s) * nw_ref[...]).astype(BF16)
    acc = _dot(a_ref[...], wo_ref[0:d_attn, :]) + _dot(sn, wo_ref[d_attn:, :])
    o_ref[...] = x_ref[...] + acc


def _mix_out(a, y, x, glu_w, glu_b, norm_w, w_out, *, tm):
    t, d = x.shape
    d_attn, d_ssm = a.shape[1], y.shape[1]
    const = lambda i: (0, 0)
    return pl.pallas_call(
        _mix_out_kernel,
        grid=(t // tm,),
        in_specs=[
            pl.BlockSpec((tm, d_attn), lambda i: (i, 0)),
            pl.BlockSpec((tm, d_ssm), lambda i: (i, 0)),
            pl.BlockSpec((tm, d), lambda i: (i, 0)),
            _resident((d_ssm, d_ssm), const),
            _resident((1, d_ssm), const),
            _resident((1, d_ssm), const),
            _resident((d, d), const),
        ],
        out_specs=pl.BlockSpec((tm, d), lambda i: (i, 0)),
        out_shape=jax.ShapeDtypeStruct((t, d), F32),
        compiler_params=pltpu.CompilerParams(
            dimension_semantics=("parallel",), vmem_limit_bytes=VMEM_LIMIT),
        name="mix_out",
    )(a, y, x, glu_w, glu_b.reshape(1, d_ssm).astype(F32), norm_w.reshape(1, d_ssm).astype(F32), w_out)


def _xattn_route_kernel(q_ref, k_ref, v_ref, h_ref, xo_ref, nw_ref, rhi_ref, rlo_ref, rb_ref,
                        h2_ref, hn_ref, eid_ref, wts_ref):
    d = h_ref.shape[1]
    hd = d // X_HEADS
    h2 = h_ref[...]
    for h in range(X_HEADS):
        sl = slice(h * hd, (h + 1) * hd)
        s = _dot_nt(q_ref[:, sl], k_ref[:, sl])
        p = jnp.exp(s - jnp.max(s, axis=-1, keepdims=True))
        p = p * (1.0 / jnp.sum(p, axis=-1, keepdims=True))
        o = _dot(p.astype(BF16), v_ref[:, sl]).astype(BF16)
        h2 = h2 + _dot(o, xo_ref[sl, :])
    h2_ref[...] = h2
    hn = _rms(h2) * nw_ref[...]
    hn_ref[...] = hn

    hi = hn.astype(BF16)
    lo = (hn - hi.astype(F32)).astype(BF16)
    logits = (_dot(hi, rhi_ref[...]) + _dot(hi, rlo_ref[...]) + _dot(lo, rhi_ref[...])) + rb_ref[...]

    lane = lax.broadcasted_iota(jnp.int32, logits.shape, 1)
    big = jnp.int32(ROUTE_LANES)

    def first_lane(cond):
        return jnp.min(jnp.where(cond, lane, big), axis=-1, keepdims=True)

    c_mask = lane < MOE_GROUPS
    lc = jnp.where(c_mask, logits, NEG)
    mc = jnp.max(lc, axis=-1, keepdims=True)
    ec = jnp.exp(lc - mc)
    p_c = ec / jnp.sum(ec, axis=-1, keepdims=True)
    p_grp = jnp.max(p_c, axis=-1, keepdims=True)
    grp = first_lane(c_mask & (p_c == p_grp))
    f_lo = MOE_GROUPS + grp * EXP_PER_GROUP
    f_mask = (lane >= f_lo) & (lane < f_lo + EXP_PER_GROUP)
    lf = jnp.where(f_mask, logits, NEG)
    mf = jnp.max(lf, axis=-1, keepdims=True)
    ef = jnp.exp(lf - mf)
    pf = ef / jnp.sum(ef, axis=-1, keepdims=True)
    v1 = jnp.max(jnp.where(f_mask, pf, -1.0), axis=-1, keepdims=True)
    i1 = first_lane(f_mask & (pf == v1))
    rest = f_mask & (lane != i1)
    v2 = jnp.max(jnp.where(rest, pf, -1.0), axis=-1, keepdims=True)
    i2 = first_lane(rest & (pf == v2))
    tot = v1 + v2
    w1 = v1 / tot * p_grp
    w2 = v2 / tot * p_grp
    eid_ref[...] = jnp.where(lane == 0, i1 - MOE_GROUPS, jnp.where(lane == 1, i2 - MOE_GROUPS, 0))
    wts_ref[...] = jnp.where(lane == 0, w1, jnp.where(lane == 1, w2, 0.0))


def _xattn_route(q, kv, h1, xo_w, norm_w, r_hi, r_lo, r_b, *, batch, seq, mem_len, tm):
    t, d = h1.shape
    n = seq // tm
    const = lambda b, i: (0, 0)
    row = lambda b, i: (b * n + i, 0)
    return pl.pallas_call(
        _xattn_route_kernel,
        grid=(batch, n),
        in_specs=[
            pl.BlockSpec((tm, d), row),
            pl.BlockSpec((mem_len, d), lambda b, i: (b, 0)),
            pl.BlockSpec((mem_len, d), lambda b, i: (b, 1)),
            pl.BlockSpec((tm, d), row),
            _resident((d, d), const),
            _resident((1, d), const),
            _resident((d, ROUTE_LANES), const),
            _resident((d, ROUTE_LANES), const),
            _resident((1, ROUTE_LANES), const),
        ],
        out_specs=[
            pl.BlockSpec((tm, d), row),
            pl.BlockSpec((tm, d), row),
            pl.BlockSpec((tm, ROUTE_LANES), row),
            pl.BlockSpec((tm, ROUTE_LANES), row),
        ],
        out_shape=[
            jax.ShapeDtypeStruct((t, d), F32),
            jax.ShapeDtypeStruct((t, d), F32),
            jax.ShapeDtypeStruct((t, ROUTE_LANES), jnp.int32),
            jax.ShapeDtypeStruct((t, ROUTE_LANES), F32),
        ],
        compiler_params=pltpu.CompilerParams(
            dimension_semantics=("parallel", "parallel"), vmem_limit_bytes=VMEM_LIMIT),
        name="xattn_route",
    )(q, kv, kv, h1, xo_w, norm_w.reshape(1, d).astype(F32), r_hi, r_lo, r_b)


def _moe_kernel(be_ref, par_ref, first_ref, nxt_ref, base_ref, nval_ref, tok_ref, nu_ref,
                hn_hbm, wg_hbm, wu_hbm, wd_hbm, o_ref,
                xbuf, wgb, wub, wdb, gsem, wsem):
    b = pl.program_id(0)
    n_used = nu_ref[0]

    def weight_copies(e, slot):
        copies = []
        for hbm, buf in ((wg_hbm, wgb), (wu_hbm, wub), (wd_hbm, wdb)):
            rows = hbm.shape[1] // WEIGHT_DMA_CHUNKS
            for c in range(WEIGHT_DMA_CHUNKS):
                sl = pl.ds(c * rows, rows)
                copies.append(pltpu.make_async_copy(hbm.at[e, sl], buf.at[slot, sl], wsem.at[slot]))
        return copies

    def groups(blk):
        return (nval_ref[blk] + ROW_GROUP - 1) // ROW_GROUP

    def start_gather(blk, slot):
        base = base_ref[blk]

        def body(g, carry):
            for r in range(ROW_GROUP):
                tok = tok_ref[base + g * ROW_GROUP + r]
                pltpu.make_async_copy(hn_hbm.at[pl.ds(tok, 1)], xbuf.at[slot, g, pl.ds(r, 1)],
                                      gsem.at[slot]).start()
            return carry
        lax.fori_loop(0, groups(blk), body, 0)

    def wait_gather(blk, slot):
        filled = xbuf.at[slot, pl.ds(0, groups(blk))]
        pltpu.make_async_copy(filled, filled, gsem.at[slot]).wait()

    @pl.when(b == 0)
    def _():
        xbuf[...] = jnp.zeros(xbuf.shape, xbuf.dtype)
        for c in weight_copies(be_ref[0], par_ref[0]):
            c.start()
        start_gather(0, 0)

    @pl.when(b < n_used)
    def _():
        slot = b % 2
        wslot = par_ref[b]
        is_first = first_ref[b] == 1

        @pl.when(is_first & (nxt_ref[b] >= 0))
        def _():
            for c in weight_copies(nxt_ref[b], 1 - wslot):
                c.start()

        @pl.when(b + 1 < n_used)
        def _():
            start_gather(b + 1, 1 - slot)

        @pl.when(is_first)
        def _():
            for c in weight_copies(0, wslot):
                c.wait()

        wait_gather(b, slot)
        x = xbuf[slot].reshape(MOE_BLOCK, xbuf.shape[-1]).astype(BF16)
        g = _dot(x, wgb[wslot].astype(BF16))
        u = _dot(x, wub[wslot].astype(BF16))
        mid = (jax.nn.silu(g) * u).astype(BF16)
        o_ref[...] = _dot(mid, wdb[wslot].astype(BF16))

    @pl.when(b >= n_used)
    def _():
        o_ref[...] = jnp.zeros(o_ref.shape, o_ref.dtype)


def _moe_experts(hn, w_gate, w_up, w_down, meta, tok_sorted, n_used):
    t, d = hn.shape
    d_ff = w_gate.shape[2]
    blk_exp, par, first, nxt, base, nval = meta
    n_blocks = blk_exp.shape[0]
    any_spec = pl.BlockSpec(memory_space=pl.ANY)
    return pl.pallas_call(
        _moe_kernel,
        grid_spec=pltpu.PrefetchScalarGridSpec(
            num_scalar_prefetch=8,
            grid=(n_blocks,),
            in_specs=[any_spec, any_spec, any_spec, any_spec],
            out_specs=pl.BlockSpec((MOE_BLOCK, d), lambda b, *_: (b, 0)),
            scratch_shapes=[
                pltpu.VMEM((2, MOE_BLOCK // ROW_GROUP, ROW_GROUP, d), F32),
                pltpu.VMEM((2, d, d_ff), F32),
                pltpu.VMEM((2, d, d_ff), F32),
                pltpu.VMEM((2, d_ff, d), F32),
                pltpu.SemaphoreType.DMA((2,)),
                pltpu.SemaphoreType.DMA((2,)),
            ],
        ),
        out_shape=jax.ShapeDtypeStruct((n_blocks * MOE_BLOCK, d), F32),
        compiler_params=pltpu.CompilerParams(
            dimension_semantics=("arbitrary",), vmem_limit_bytes=VMEM_LIMIT),
        name="moe_experts",
    )(blk_exp, par, first, nxt, base, nval, tok_sorted, n_used, hn, w_gate, w_up, w_down)


def _combine_kernel(pos_ref, y_hbm, h_ref, w_ref, o_ref, ybuf, sem, *, tm):
    i = pl.program_id(0)
    n_groups = tm // ROW_GROUP

    def start_gather(tile, slot):
        base = tile * (tm * TOP_K_FINE)

        def body(g, carry):
            for r in range(ROW_GROUP):
                for k in range(TOP_K_FINE):
                    row = pos_ref[base + (g * ROW_GROUP + r) * TOP_K_FINE + k]
                    pltpu.make_async_copy(y_hbm.at[pl.ds(row, 1)], ybuf.at[slot, k, g, pl.ds(r, 1)],
                                          sem.at[slot]).start()
            return carry
        lax.fori_loop(0, n_groups, body, 0)

    @pl.when(i == 0)
    def _():
        start_gather(0, 0)

    @pl.when(i + 1 < pl.num_programs(0))
    def _():
        start_gather(i + 1, (i + 1) % 2)

    slot = i % 2
    pltpu.make_async_copy(ybuf.at[slot], ybuf.at[slot], sem.at[slot]).wait()
    w = w_ref[...]
    d = o_ref.shape[1]
    y0 = ybuf[slot, 0].reshape(tm, d)
    y1 = ybuf[slot, 1].reshape(tm, d)
    o_ref[...] = h_ref[...] + (w[:, 0:1] * y0 + w[:, 1:2] * y1)


def _combine(y, h2, wts, pos, *, tm):
    t, d = h2.shape
    kern = functools.partial(_combine_kernel, tm=tm)
    return pl.pallas_call(
        kern,
        grid_spec=pltpu.PrefetchScalarGridSpec(
            num_scalar_prefetch=1,
            grid=(t // tm,),
            in_specs=[
                pl.BlockSpec(memory_space=pl.ANY),
                pl.BlockSpec((tm, d), lambda i, pos: (i, 0)),
                pl.BlockSpec((tm, ROUTE_LANES), lambda i, pos: (i, 0)),
            ],
            out_specs=pl.BlockSpec((tm, d), lambda i, pos: (i, 0)),
            scratch_shapes=[
                pltpu.VMEM((2, TOP_K_FINE, tm // ROW_GROUP, ROW_GROUP, d), F32),
                pltpu.SemaphoreType.DMA((2,)),
            ],
        ),
        out_shape=jax.ShapeDtypeStruct((t, d), F32),
        compiler_params=pltpu.CompilerParams(
            dimension_semantics=("arbitrary",), vmem_limit_bytes=VMEM_LIMIT),
        name="moe_combine",
    )(pos, y, h2, wts)


def _lookup(table, idx):
    sel = idx[:, None] == jnp.arange(table.shape[0], dtype=jnp.int32)[None, :]
    return jnp.sum(jnp.where(sel, table[None, :], 0), axis=1).astype(jnp.int32)


def _dispatch(eid, n_tokens):
    n_assign = n_tokens * TOP_K_FINE
    experts = jnp.arange(N_EXPERTS, dtype=jnp.int32)
    e_flat = eid.reshape(n_assign)
    a_ids = jnp.arange(n_assign, dtype=jnp.int32)
    e_s, order = lax.sort_key_val(e_flat, a_ids)
    counts = jnp.sum((e_flat[:, None] == experts[None, :]).astype(jnp.int32), axis=0)
    starts = jnp.cumsum(counts) - counts
    nb = (counts + MOE_BLOCK - 1) // MOE_BLOCK
    blk_end = jnp.cumsum(nb)
    blk_start = blk_end - nb
    n_used = blk_end[-1]
    n_blocks = (n_assign + N_EXPERTS * (MOE_BLOCK - 1)) // MOE_BLOCK
    b_ids = jnp.arange(n_blocks, dtype=jnp.int32)
    used = b_ids < n_used
    blk_exp = jnp.minimum(jnp.sum((blk_end[None, :] <= b_ids[:, None]).astype(jnp.int32), axis=1),
                          N_EXPERTS - 1)
    j = b_ids - _lookup(blk_start, blk_exp)
    base = jnp.where(used, _lookup(starts, blk_exp) + j * MOE_BLOCK, 0)
    nval = jnp.where(used, jnp.clip(_lookup(counts, blk_exp) - j * MOE_BLOCK, 0, MOE_BLOCK), 0)
    first = (used & (j == 0)).astype(jnp.int32)
    active = counts > 0
    par = _lookup(jnp.cumsum(active.astype(jnp.int32)) - 1, blk_exp) & 1
    later = lax.cummin(jnp.where(active, experts, N_EXPERTS), reverse=True)
    nxt_e = jnp.concatenate([later[1:], jnp.full((1,), N_EXPERTS, jnp.int32)])
    nxt = _lookup(jnp.where(nxt_e == N_EXPERTS, -1, nxt_e), blk_exp)
    meta = tuple(v.astype(jnp.int32) for v in (blk_exp, par, first, nxt, base, nval))
    row_sorted = a_ids + _lookup(blk_start * MOE_BLOCK - starts, e_s)
    _, pos = lax.sort_key_val(order, row_sorted)
    tok_sorted = jnp.concatenate([lax.shift_right_logical(order, 1), jnp.zeros((ROW_GROUP,), jnp.int32)])
    return meta, tok_sorted, pos, n_used.astype(jnp.int32).reshape(1)


def kernel(x, mem, norm1_w, w_in, q_norm_w, k_norm_w, lambda_q1, lambda_k1, lambda_q2, lambda_k2, subln_w, ssm_lambda_re, ssm_lambda_im, ssm_log_dt, ssm_b_re, ssm_b_im, ssm_c_re, ssm_c_im, ssm_d, ssm_glu_w, ssm_glu_b, ssm_out_norm_w, w_out, norm2_w, mem_norm_w, xq_w, xkv_w, xq_norm_w, xk_norm_w, xo_w, norm3_w, router_coarse_w, router_coarse_b, router_fine_w, router_fine_b, expert_w_gate, expert_w_up, expert_w_down):
    batch, seq, d = x.shape
    mem_len = mem.shape[1]
    t = batch * seq
    depth = norm1_w.shape[0]
    d_attn = DA_HEADS * DA_V_DIM
    d_ssm = d - d_attn
    qk_cols = DA_HEADS * 2 * DA_QK_DIM
    x_hd = d // X_HEADS
    h = x.reshape(t, d)
    mem2 = mem.reshape(batch * mem_len, d)

    for l in range(depth):
        lam_init = 0.8 - 0.6 * math.exp(-0.3 * l)
        lam = (jnp.exp(jnp.sum(lambda_q1[l].astype(F32) * lambda_k1[l].astype(F32)))
               - jnp.exp(jnp.sum(lambda_q2[l].astype(F32) * lambda_k2[l].astype(F32)))
               + lam_init).reshape(1)

        n_rep = qk_cols // DA_QK_DIM
        in_gain = jnp.concatenate([
            jnp.tile(q_norm_w[l].astype(F32) * (DA_QK_DIM ** -0.5 * LOG2E), n_rep),
            jnp.tile(k_norm_w[l].astype(F32), n_rep),
            jnp.ones((d_attn + d_ssm,), F32)])
        proj = _norm_matmul(h, norm1_w[l], w_in[l].astype(BF16), in_gain,
                            n_norm_cols=2 * qk_cols, chunk=DA_QK_DIM, tm=1024, tn=1024, name="in_proj")
        sub_gain = (subln_w[l].astype(F32) * (1.0 - lam_init)).reshape(1, DA_V_DIM)
        a = _diff_attn(proj, lam, sub_gain, batch=batch, seq=seq, tq=512)

        seg_len = seq // SCAN_SEGS
        u = proj[:, 2 * qk_cols + d_attn:]
        u_perm = u.reshape(batch, SCAN_SEGS, seg_len, d_ssm).transpose(0, 2, 1, 3).reshape(batch, seq, d_ssm)
        bd, a_re, a_im, cd, dd = _s5_params(ssm_lambda_re[l], ssm_lambda_im[l], ssm_log_dt[l],
                                            ssm_b_re[l], ssm_b_im[l], ssm_c_re[l], ssm_c_im[l], ssm_d[l])
        y_perm = _s5(u_perm, bd, a_re, a_im, cd, dd)
        y = y_perm.reshape(batch, seg_len, SCAN_SEGS, d_ssm).transpose(0, 2, 1, 3).reshape(t, d_ssm)
        h = _mix_out(a, y, h, ssm_glu_w[l].astype(BF16), ssm_glu_b[l], ssm_out_norm_w[l],
                     w_out[l].astype(BF16), tm=512)

        kv_gain = jnp.concatenate([jnp.tile(xk_norm_w[l].astype(F32), X_HEADS), jnp.ones((d,), F32)])
        kv = _norm_matmul(mem2, mem_norm_w[l], xkv_w[l].astype(BF16), kv_gain,
                          n_norm_cols=d, chunk=x_hd, tm=512, tn=512, name="kv_proj")
        q_gain = jnp.tile(xq_norm_w[l].astype(F32) * (x_hd ** -0.5), X_HEADS)
        q = _norm_matmul(h, norm2_w[l], xq_w[l].astype(BF16), q_gain,
                         n_norm_cols=d, chunk=x_hd, tm=1024, tn=1024, name="xq_proj")
        r_w = jnp.concatenate([router_coarse_w[l].astype(F32), router_fine_w[l].astype(F32)], axis=1)
        r_w = jnp.pad(r_w, ((0, 0), (0, ROUTE_LANES - r_w.shape[1])))
        r_hi = r_w.astype(BF16)
        r_lo = (r_w - r_hi.astype(F32)).astype(BF16)
        r_b = jnp.concatenate([router_coarse_b[l].astype(F32), router_fine_b[l].astype(F32)])
        r_b = jnp.pad(r_b, (0, ROUTE_LANES - r_b.shape[0])).reshape(1, ROUTE_LANES)
        h2, hn3, eid, wts = _xattn_route(q, kv, h, xo_w[l].astype(BF16), norm3_w[l], r_hi, r_lo, r_b,
                                         batch=batch, seq=seq, mem_len=mem_len, tm=512)

        meta, tok_sorted, pos, n_used = _dispatch(eid[:, :TOP_K_FINE], t)
        y = _moe_experts(hn3, expert_w_gate[l], expert_w_up[l], expert_w_down[l], meta, tok_sorted, n_used)
        h = _combine(y, h2, wts, pos, tm=256)

    return h.reshape(batch, seq, d)
```

```python
import functools
import math

import jax
import jax.numpy as jnp
from jax import lax
from jax.experimental import pallas as pl
from jax.experimental.pallas import tpu as pltpu

F32 = jnp.float32
BF16 = jnp.bfloat16

EPS = 1e-6
DA_HEADS = 4
DA_QK_DIM = 128
DA_V_DIM = 256
SSM_GROUP = 16
SSM_STATE = 64
X_HEADS = 4
MOE_GROUPS = 8
EXP_PER_GROUP = 8
N_EXPERTS = MOE_GROUPS * EXP_PER_GROUP
TOP_K_FINE = 2

LANES = 128
SUBLANES = 8
MXU_TILE = 256
VMEM_LIMIT = 56 * 1024 * 1024
NEG = -1e30
LOG2E = math.log2(math.e)

SSM_CHUNK_GROUPS = LANES // SSM_GROUP
SSM_CHUNK_STATE = SSM_CHUNK_GROUPS * SSM_STATE
SCAN_SEGS = SUBLANES
MOE_BLOCK = 256
ROW_GROUP = SUBLANES
WEIGHT_DMA_CHUNKS = 4
ROUTE_LANES = LANES


def _rms(x, eps=EPS):
    return x * lax.rsqrt(jnp.mean(x * x, axis=-1, keepdims=True) + eps)


def _dot(a, b):
    return jnp.dot(a, b, preferred_element_type=F32)


def _dot_nt(a, b):
    return lax.dot_general(a, b, (((1,), (1,)), ((), ())), preferred_element_type=F32)


def _resident(shape, index_map):
    return pl.BlockSpec(shape, index_map, pipeline_mode=pl.Buffered(1))


def _norm_matmul_kernel(x_ref, nw_ref, w_ref, g_ref, o_ref, xn_ref, *, n_norm_tiles, chunk):
    j = pl.program_id(1)

    @pl.when(j == 0)
    def _():
        x = x_ref[...].astype(F32)
        xn_ref[...] = (_rms(x) * nw_ref[...]).astype(BF16)

    normed = j < n_norm_tiles
    tn = w_ref.shape[1]
    sub = max(chunk, MXU_TILE)
    for s in range(tn // sub):
        acc = _dot(xn_ref[...], w_ref[:, s * sub:(s + 1) * sub])
        for c in range(sub // chunk):
            lo = s * sub + c * chunk
            a = acc[:, c * chunk:(c + 1) * chunk]
            inv = lax.rsqrt(jnp.mean(a * a, axis=-1, keepdims=True) + EPS)
            scale = jnp.where(normed, inv, 1.0)
            o_ref[:, lo:lo + chunk] = (a * scale * g_ref[:, lo:lo + chunk]).astype(o_ref.dtype)


def _norm_matmul(x, norm_w, w_bf16, gain, *, n_norm_cols, chunk, tm, tn, name):
    m, k = x.shape
    n = w_bf16.shape[1]
    assert m % tm == 0 and n % tn == 0 and tn % max(chunk, MXU_TILE) == 0 and n_norm_cols % tn == 0
    kern = functools.partial(_norm_matmul_kernel, n_norm_tiles=n_norm_cols // tn, chunk=chunk)
    return pl.pallas_call(
        kern,
        grid=(m // tm, n // tn),
        in_specs=[
            pl.BlockSpec((tm, k), lambda i, j: (i, 0)),
            pl.BlockSpec((1, k), lambda i, j: (0, 0)),
            pl.BlockSpec((k, tn), lambda i, j: (0, j)),
            pl.BlockSpec((1, tn), lambda i, j: (0, j)),
        ],
        out_specs=pl.BlockSpec((tm, tn), lambda i, j: (i, j)),
        out_shape=jax.ShapeDtypeStruct((m, n), BF16),
        scratch_shapes=[pltpu.VMEM((tm, k), BF16)],
        compiler_params=pltpu.CompilerParams(
            dimension_semantics=("parallel", "arbitrary"), vmem_limit_bytes=VMEM_LIMIT),
        name=name,
    )(x, norm_w.reshape(1, k).astype(F32), w_bf16, gain.reshape(1, n).astype(F32))


def _diff_attn_kernel(lam_ref, q_ref, k_ref, v_ref, g_ref, o_ref,
                      m1_ref, l1_ref, acc1_ref, m2_ref, l2_ref, acc2_ref, *, tq):
    qi = pl.program_id(2)
    stats = ((m1_ref, l1_ref, acc1_ref), (m2_ref, l2_ref, acc2_ref))
    for m_ref, l_ref, acc_ref in stats:
        m_ref[...] = jnp.full(m_ref.shape, NEG, F32)
        l_ref[...] = jnp.zeros(l_ref.shape, F32)
        acc_ref[...] = jnp.zeros(acc_ref.shape, F32)

    def scores(j):
        start = pl.multiple_of(j * tq, tq)
        return tuple(_dot_nt(q_ref[:, c * DA_QK_DIM:(c + 1) * DA_QK_DIM],
                             k_ref[pl.ds(start, tq), c * DA_QK_DIM:(c + 1) * DA_QK_DIM])
                     for c in range(2))

    def accumulate(j, s_pair, masked):
        start = pl.multiple_of(j * tq, tq)
        for s, (m_ref, l_ref, acc_ref) in zip(s_pair, stats):
            if masked:
                row = lax.broadcasted_iota(jnp.int32, s.shape, 0)
                col = lax.broadcasted_iota(jnp.int32, s.shape, 1)
                s = jnp.where(col <= row, s, NEG)
            m_old = m_ref[...]
            m_new = jnp.maximum(m_old, jnp.max(s, axis=-1, keepdims=True))
            p = jnp.exp2(s - m_new)
            alpha = jnp.exp2(m_old - m_new)
            l_ref[...] = alpha * l_ref[...] + jnp.sum(p, axis=-1, keepdims=True)
            acc_ref[...] = alpha * acc_ref[...] + _dot(p.astype(BF16), v_ref[pl.ds(start, tq), :])
            m_ref[...] = m_new

    def off_diag(j, s_pair):
        s_next = scores(j + 1)
        accumulate(j, s_pair, False)
        return s_next

    s_diag = lax.fori_loop(0, qi, off_diag, scores(0))
    accumulate(qi, s_diag, True)

    lam = lam_ref[0]
    o = acc1_ref[...] / l1_ref[...] - lam * (acc2_ref[...] / l2_ref[...])
    o_ref[...] = (_rms(o) * g_ref[...]).astype(o_ref.dtype)


def _diff_attn(proj, lam, gain, *, batch, seq, tq):
    t = batch * seq
    nq = seq // tq
    width = 2 * DA_QK_DIM
    k_blk0 = DA_HEADS
    v_blk0 = 2 * DA_HEADS
    kern = functools.partial(_diff_attn_kernel, tq=tq)
    return pl.pallas_call(
        kern,
        grid_spec=pltpu.PrefetchScalarGridSpec(
            num_scalar_prefetch=1,
            grid=(batch, DA_HEADS, nq),
            in_specs=[
                pl.BlockSpec((tq, width), lambda b, h, i, lam: (b * nq + i, h)),
                pl.BlockSpec((seq, width), lambda b, h, i, lam: (b, k_blk0 + h)),
                pl.BlockSpec((seq, width), lambda b, h, i, lam: (b, v_blk0 + h)),
                pl.BlockSpec((1, DA_V_DIM), lambda b, h, i, lam: (0, 0)),
            ],
            out_specs=pl.BlockSpec((tq, DA_V_DIM), lambda b, h, i, lam: (b * nq + i, h)),
            scratch_shapes=[
                pltpu.VMEM((tq, 1), F32), pltpu.VMEM((tq, 1), F32), pltpu.VMEM((tq, DA_V_DIM), F32),
                pltpu.VMEM((tq, 1), F32), pltpu.VMEM((tq, 1), F32), pltpu.VMEM((tq, DA_V_DIM), F32),
            ],
        ),
        out_shape=jax.ShapeDtypeStruct((t, DA_HEADS * DA_V_DIM), BF16),
        compiler_params=pltpu.CompilerParams(
            dimension_semantics=("parallel", "parallel", "arbitrary"), vmem_limit_bytes=VMEM_LIMIT),
        name="diff_attn",
    )(lam, proj, proj, proj, gain)


def _s5_kernel(u_ref, bd_ref, ar_ref, ai_ref, cd_ref, d_ref, o_ref, xs_ref, *, seq, rows):
    ns = SSM_CHUNK_STATE
    seg_len = seq // SCAN_SEGS
    n_row_blk = seq // rows

    def in_map(r, carry):
        r0 = pl.multiple_of(r * rows, rows)
        xs_ref[pl.ds(r0, rows), :] = _dot(u_ref[pl.ds(r0, rows), :], bd_ref[...])
        return carry

    lax.fori_loop(0, n_row_blk, in_map, 0)

    ar = jnp.broadcast_to(ar_ref[...], (SCAN_SEGS, ns))
    ai = jnp.broadcast_to(ai_ref[...], (SCAN_SEGS, ns))

    def advance(t, sr, si):
        t0 = pl.multiple_of(t * SCAN_SEGS, SCAN_SEGS)
        br = xs_ref[pl.ds(t0, SCAN_SEGS), 0:ns]
        bi = xs_ref[pl.ds(t0, SCAN_SEGS), ns:2 * ns]
        return t0, ar * sr - ai * si + br, ar * si + ai * sr + bi

    def local_step(t, carry):
        _, nr, ni = advance(t, *carry)
        return nr, ni

    zero = jnp.zeros((SCAN_SEGS, ns), F32)
    fr, fi = lax.fori_loop(0, seg_len, local_step, (zero, zero))

    pr, pi = ar, ai
    for _ in range(int(math.log2(seg_len))):
        pr, pi = pr * pr - pi * pi, 2.0 * pr * pi
    seg = lax.broadcasted_iota(jnp.int32, (SCAN_SEGS, ns), 0)

    def shifted(x, k):
        return jnp.where(seg >= k, pltpu.roll(x, k, 0), 0.0)

    k = 1
    while k < SCAN_SEGS:
        gr, gi = shifted(fr, k), shifted(fi, k)
        fr, fi = fr + pr * gr - pi * gi, fi + pr * gi + pi * gr
        pr, pi = pr * pr - pi * pi, 2.0 * pr * pi
        k *= 2
    sr0, si0 = shifted(fr, 1), shifted(fi, 1)

    def global_step(t, carry):
        t0, nr, ni = advance(t, *carry)
        xs_ref[pl.ds(t0, SCAN_SEGS), 0:ns] = nr
        xs_ref[pl.ds(t0, SCAN_SEGS), ns:2 * ns] = ni
        return nr, ni

    lax.fori_loop(0, seg_len, global_step, (sr0, si0))

    def out_map(r, carry):
        r0 = pl.multiple_of(r * rows, rows)
        x = xs_ref[pl.ds(r0, rows), :].astype(BF16)
        y = _dot(x, cd_ref[...]) + d_ref[...] * u_ref[pl.ds(r0, rows), :].astype(F32)
        o_ref[pl.ds(r0, rows), :] = jax.nn.gelu(y).astype(o_ref.dtype)
        return carry

    lax.fori_loop(0, n_row_blk, out_map, 0)


def _s5(u_perm, bd, a_re, a_im, cd, d_skip, *, rows=256):
    batch, seq, d_ssm = u_perm.shape
    n_chunks = d_ssm // LANES
    kern = functools.partial(_s5_kernel, seq=seq, rows=rows)
    return pl.pallas_call(
        kern,
        grid=(batch, n_chunks),
        in_specs=[
            pl.BlockSpec((None, seq, LANES), lambda b, c: (b, 0, c)),
            pl.BlockSpec((None, LANES, 2 * SSM_CHUNK_STATE), lambda b, c: (c, 0, 0)),
            pl.BlockSpec((None, 1, SSM_CHUNK_STATE), lambda b, c: (c, 0, 0)),
            pl.BlockSpec((None, 1, SSM_CHUNK_STATE), lambda b, c: (c, 0, 0)),
            pl.BlockSpec((None, 2 * SSM_CHUNK_STATE, LANES), lambda b, c: (c, 0, 0)),
            pl.BlockSpec((None, 1, LANES), lambda b, c: (c, 0, 0)),
        ],
        out_specs=pl.BlockSpec((None, seq, LANES), lambda b, c: (b, 0, c)),
        out_shape=jax.ShapeDtypeStruct((batch, seq, d_ssm), BF16),
        scratch_shapes=[pltpu.VMEM((seq, 2 * SSM_CHUNK_STATE), F32)],
        compiler_params=pltpu.CompilerParams(
            dimension_semantics=("parallel", "parallel"), vmem_limit_bytes=VMEM_LIMIT),
        name="s5_scan",
    )(u_perm, bd, a_re, a_im, cd, d_skip)


def _s5_params(lam_re, lam_im, log_dt, b_re, b_im, c_re, c_im, d_skip):
    g = lam_re.shape[0]
    nc = g // SSM_CHUNK_GROUPS
    lr = jnp.minimum(lam_re.astype(F32), -1e-4)
    li = lam_im.astype(F32)
    dt = jnp.exp(log_dt.astype(F32))[:, None]
    mag = jnp.exp(lr * dt)
    lb_re, lb_im = mag * jnp.cos(li * dt), mag * jnp.sin(li * dt)
    den = lr * lr + li * li
    coef_re = ((lb_re - 1.0) * lr + lb_im * li) / den
    coef_im = (lb_im * lr - (lb_re - 1.0) * li) / den
    br, bi = b_re.astype(F32), b_im.astype(F32)
    bb_re = coef_re[..., None] * br - coef_im[..., None] * bi
    bb_im = coef_re[..., None] * bi + coef_im[..., None] * br
    eye = jnp.eye(SSM_CHUNK_GROUPS, dtype=F32)

    def pack_in(bb):
        bb = bb.reshape(nc, SSM_CHUNK_GROUPS, SSM_STATE, SSM_GROUP)
        return jnp.einsum('cgph,gk->cghkp', bb, eye).reshape(nc, LANES, SSM_CHUNK_STATE)

    def pack_out(cc):
        cc = cc.astype(F32).reshape(nc, SSM_CHUNK_GROUPS, SSM_GROUP, SSM_STATE)
        return jnp.einsum('cghp,gk->ckpgh', cc, eye).reshape(nc, SSM_CHUNK_STATE, LANES)

    bd = jnp.concatenate([pack_in(bb_re), pack_in(bb_im)], axis=-1).astype(BF16)
    cd = jnp.concatenate([pack_out(c_re), -pack_out(c_im)], axis=1).astype(BF16)
    a_re = lb_re.reshape(nc, 1, SSM_CHUNK_STATE)
    a_im = lb_im.reshape(nc, 1, SSM_CHUNK_STATE)
    dd = d_skip.astype(F32).reshape(nc, 1, LANES)
    return bd, a_re, a_im, cd, dd


def _mix_out_kernel(a_ref, y_ref, x_ref, gw_ref, gb_ref, nw_ref, wo_ref, o_ref):
    d_attn = a_ref.shape[1]
    y = y_ref[...]
    gate = _dot(y, gw_ref[...]) + gb_ref[...]
    s = y.astype(F32) * jax.nn.sigmoid(gate)
    sn = (_rms(s) * nw_ref[...]).astype(BF16)
    acc = _dot(a_ref[...], wo_ref[0:d_attn, :]) + _dot(sn, wo_ref[d_attn:, :])
    o_ref[...] = x_ref[...] + acc


def _mix_out(a, y, x, glu_w, glu_b, norm_w, w_out, *, tm):
    t, d = x.shape
    d_attn, d_ssm = a.shape[1], y.shape[1]
    const = lambda i: (0, 0)
    return pl.pallas_call(
        _mix_out_kernel,
        grid=(t // tm,),
        in_specs=[
            pl.BlockSpec((tm, d_attn), lambda i: (i, 0)),
            pl.BlockSpec((tm, d_ssm), lambda i: (i, 0)),
            pl.BlockSpec((tm, d), lambda i: (i, 0)),
            _resident((d_ssm, d_ssm), const),
            _resident((1, d_ssm), const),
            _resident((1, d_ssm), const),
            _resident((d, d), const),
        ],
        out_specs=pl.BlockSpec((tm, d), lambda i: (i, 0)),
        out_shape=jax.ShapeDtypeStruct((t, d), F32),
        compiler_params=pltpu.CompilerParams(
            dimension_semantics=("parallel",), vmem_limit_bytes=VMEM_LIMIT),
        name="mix_out",
    )(a, y, x, glu_w, glu_b.reshape(1, d_ssm).astype(F32), norm_w.reshape(1, d_ssm).astype(F32), w_out)


def _xattn_route_kernel(q_ref, k_ref, v_ref, h_ref, xo_ref, nw_ref, rhi_ref, rlo_ref, rb_ref,
                        h2_ref, hn_ref, eid_ref, wts_ref):
    d = h_ref.shape[1]
    hd = d // X_HEADS
    h2 = h_ref[...]
    for h in range(X_HEADS):
        sl = slice(h * hd, (h + 1) * hd)
        s = _dot_nt(q_ref[:, sl], k_ref[:, sl])
        p = jnp.exp(s - jnp.max(s, axis=-1, keepdims=True))
        p = p * (1.0 / jnp.sum(p, axis=-1, keepdims=True))
        o = _dot(p.astype(BF16), v_ref[:, sl]).astype(BF16)
        h2 = h2 + _dot(o, xo_ref[sl, :])
    h2_ref[...] = h2
    hn = _rms(h2) * nw_ref[...]
    hn_ref[...] = hn

    hi = hn.astype(BF16)
    lo = (hn - hi.astype(F32)).astype(BF16)
    logits = (_dot(hi, rhi_ref[...]) + _dot(hi, rlo_ref[...]) + _dot(lo, rhi_ref[...])) + rb_ref[...]

    lane = lax.broadcasted_iota(jnp.int32, logits.shape, 1)
    big = jnp.int32(ROUTE_LANES)

    def first_lane(cond):
        return jnp.min(jnp.where(cond, lane, big), axis=-1, keepdims=True)

    c_mask = lane < MOE_GROUPS
    lc = jnp.where(c_mask, logits, NEG)
    mc = jnp.max(lc, axis=-1, keepdims=True)
    ec = jnp.exp(lc - mc)
    p_c = ec / jnp.sum(ec, axis=-1, keepdims=True)
    p_grp = jnp.max(p_c, axis=-1, keepdims=True)
    grp = first_lane(c_mask & (p_c == p_grp))
    f_lo = MOE_GROUPS + grp * EXP_PER_GROUP
    f_mask = (lane >= f_lo) & (lane < f_lo + EXP_PER_GROUP)
    lf = jnp.where(f_mask, logits, NEG)
    mf = jnp.max(lf, axis=-1, keepdims=True)
    ef = jnp.exp(lf - mf)
    pf = ef / jnp.sum(ef, axis=-1, keepdims=True)
    v1 = jnp.max(jnp.where(f_mask, pf, -1.0), axis=-1, keepdims=True)
    i1 = first_lane(f_mask & (pf == v1))
    rest = f_mask & (lane != i1)
    v2 = jnp.max(jnp.where(rest, pf, -1.0), axis=-1, keepdims=True)
    i2 = first_lane(rest & (pf == v2))
    tot = v1 + v2
    w1 = v1 / tot * p_grp
    w2 = v2 / tot * p_grp
    eid_ref[...] = jnp.where(lane == 0, i1 - MOE_GROUPS, jnp.where(lane == 1, i2 - MOE_GROUPS, 0))
    wts_ref[...] = jnp.where(lane == 0, w1, jnp.where(lane == 1, w2, 0.0))


def _xattn_route(q, kv, h1, xo_w, norm_w, r_hi, r_lo, r_b, *, batch, seq, mem_len, tm):
    t, d = h1.shape
    n = seq // tm
    const = lambda b, i: (0, 0)
    row = lambda b, i: (b * n + i, 0)
    return pl.pallas_call(
        _xattn_route_kernel,
        grid=(batch, n),
        in_specs=[
            pl.BlockSpec((tm, d), row),
            pl.BlockSpec((mem_len, d), lambda b, i: (b, 0)),
            pl.BlockSpec((mem_len, d), lambda b, i: (b, 1)),
            pl.BlockSpec((tm, d), row),
            _resident((d, d), const),
            _resident((1, d), const),
            _resident((d, ROUTE_LANES), const),
            _resident((d, ROUTE_LANES), const),
            _resident((1, ROUTE_LANES), const),
        ],
        out_specs=[
            pl.BlockSpec((tm, d), row),
            pl.BlockSpec((tm, d), row),
            pl.BlockSpec((tm, ROUTE_LANES), row),
            pl.BlockSpec((tm, ROUTE_LANES), row),
        ],
        out_shape=[
            jax.ShapeDtypeStruct((t, d), F32),
            jax.ShapeDtypeStruct((t, d), F32),
            jax.ShapeDtypeStruct((t, ROUTE_LANES), jnp.int32),
            jax.ShapeDtypeStruct((t, ROUTE_LANES), F32),
        ],
        compiler_params=pltpu.CompilerParams(
            dimension_semantics=("parallel", "parallel"), vmem_limit_bytes=VMEM_LIMIT),
        name="xattn_route",
    )(q, kv, kv, h1, xo_w, norm_w.reshape(1, d).astype(F32), r_hi, r_lo, r_b)


def _moe_kernel(be_ref, par_ref, first_ref, nxt_ref, base_ref, nval_ref, tok_ref, nu_ref,
                hn_hbm, wg_hbm, wu_hbm, wd_hbm, o_ref,
                xbuf, wgb, wub, wdb, gsem, wsem):
    b = pl.program_id(0)
    n_used = nu_ref[0]

    def weight_copies(e, slot):
        copies = []
        for hbm, buf in ((wg_hbm, wgb), (wu_hbm, wub), (wd_hbm, wdb)):
            rows = hbm.shape[1] // WEIGHT_DMA_CHUNKS
            for c in range(WEIGHT_DMA_CHUNKS):
                sl = pl.ds(c * rows, rows)
                copies.append(pltpu.make_async_copy(hbm.at[e, sl], buf.at[slot, sl], wsem.at[slot]))
        return copies

    def groups(blk):
        return (nval_ref[blk] + ROW_GROUP - 1) // ROW_GROUP

    def start_gather(blk, slot):
        base = base_ref[blk]

        def body(g, carry):
            for r in range(ROW_GROUP):
                tok = tok_ref[base + g * ROW_GROUP + r]
                pltpu.make_async_copy(hn_hbm.at[pl.ds(tok, 1)], xbuf.at[slot, g, pl.ds(r, 1)],
                                      gsem.at[slot]).start()
            return carry
        lax.fori_loop(0, groups(blk), body, 0)

    def wait_gather(blk, slot):
        filled = xbuf.at[slot, pl.ds(0, groups(blk))]
        pltpu.make_async_copy(filled, filled, gsem.at[slot]).wait()

    @pl.when(b == 0)
    def _():
        xbuf[...] = jnp.zeros(xbuf.shape, xbuf.dtype)
        for c in weight_copies(be_ref[0], par_ref[0]):
            c.start()
        start_gather(0, 0)

    @pl.when(b < n_used)
    def _():
        slot = b % 2
        wslot = par_ref[b]
        is_first = first_ref[b] == 1

        @pl.when(is_first & (nxt_ref[b] >= 0))
        def _():
            for c in weight_copies(nxt_ref[b], 1 - wslot):
                c.start()

        @pl.when(b + 1 < n_used)
        def _():
            start_gather(b + 1, 1 - slot)

        @pl.when(is_first)
        def _():
            for c in weight_copies(0, wslot):
                c.wait()

        wait_gather(b, slot)
        x = xbuf[slot].reshape(MOE_BLOCK, xbuf.shape[-1]).astype(BF16)
        g = _dot(x, wgb[wslot].astype(BF16))
        u = _dot(x, wub[wslot].astype(BF16))
        mid = (jax.nn.silu(g) * u).astype(BF16)
        o_ref[...] = _dot(mid, wdb[wslot].astype(BF16))

    @pl.when(b >= n_used)
    def _():
        o_ref[...] = jnp.zeros(o_ref.shape, o_ref.dtype)


def _moe_experts(hn, w_gate, w_up, w_down, meta, tok_sorted, n_used):
    t, d = hn.shape
    d_ff = w_gate.shape[2]
    blk_exp, par, first, nxt, base, nval = meta
    n_blocks = blk_exp.shape[0]
    any_spec = pl.BlockSpec(memory_space=pl.ANY)
    return pl.pallas_call(
        _moe_kernel,
        grid_spec=pltpu.PrefetchScalarGridSpec(
            num_scalar_prefetch=8,
            grid=(n_blocks,),
            in_specs=[any_spec, any_spec, any_spec, any_spec],
            out_specs=pl.BlockSpec((MOE_BLOCK, d), lambda b, *_: (b, 0)),
            scratch_shapes=[
                pltpu.VMEM((2, MOE_BLOCK // ROW_GROUP, ROW_GROUP, d), F32),
                pltpu.VMEM((2, d, d_ff), F32),
                pltpu.VMEM((2, d, d_ff), F32),
                pltpu.VMEM((2, d_ff, d), F32),
                pltpu.SemaphoreType.DMA((2,)),
                pltpu.SemaphoreType.DMA((2,)),
            ],
        ),
        out_shape=jax.ShapeDtypeStruct((n_blocks * MOE_BLOCK, d), F32),
        compiler_params=pltpu.CompilerParams(
            dimension_semantics=("arbitrary",), vmem_limit_bytes=VMEM_LIMIT),
        name="moe_experts",
    )(blk_exp, par, first, nxt, base, nval, tok_sorted, n_used, hn, w_gate, w_up, w_down)


def _combine_kernel(pos_ref, y_hbm, h_ref, w_ref, o_ref, ybuf, sem, *, tm):
    i = pl.program_id(0)
    n_groups = tm // ROW_GROUP

    def start_gather(tile, slot):
        base = tile * (tm * TOP_K_FINE)

        def body(g, carry):
            for r in range(ROW_GROUP):
                for k in range(TOP_K_FINE):
                    row = pos_ref[base + (g * ROW_GROUP + r) * TOP_K_FINE + k]
                    pltpu.make_async_copy(y_hbm.at[pl.ds(row, 1)], ybuf.at[slot, k, g, pl.ds(r, 1)],
                                          sem.at[slot]).start()
            return carry
        lax.fori_loop(0, n_groups, body, 0)

    @pl.when(i == 0)
    def _():
        start_gather(0, 0)

    @pl.when(i + 1 < pl.num_programs(0))
    def _():
        start_gather(i + 1, (i + 1) % 2)

    slot = i % 2
    pltpu.make_async_copy(ybuf.at[slot], ybuf.at[slot], sem.at[slot]).wait()
    w = w_ref[...]
    d = o_ref.shape[1]
    y0 = ybuf[slot, 0].reshape(tm, d)
    y1 = ybuf[slot, 1].reshape(tm, d)
    o_ref[...] = h_ref[...] + (w[:, 0:1] * y0 + w[:, 1:2] * y1)


def _combine(y, h2, wts, pos, *, tm):
    t, d = h2.shape
    kern = functools.partial(_combine_kernel, tm=tm)
    return pl.pallas_call(
        kern,
        grid_spec=pltpu.PrefetchScalarGridSpec(
            num_scalar_prefetch=1,
            grid=(t // tm,),
            in_specs=[
                pl.BlockSpec(memory_space=pl.ANY),
                pl.BlockSpec((tm, d), lambda i, pos: (i, 0)),
                pl.BlockSpec((tm, ROUTE_LANES), lambda i, pos: (i, 0)),
            ],
            out_specs=pl.BlockSpec((tm, d), lambda i, pos: (i, 0)),
            scratch_shapes=[
                pltpu.VMEM((2, TOP_K_FINE, tm // ROW_GROUP, ROW_GROUP, d), F32),
                pltpu.SemaphoreType.DMA((2,)),
            ],
        ),
        out_shape=jax.ShapeDtypeStruct((t, d), F32),
        compiler_params=pltpu.CompilerParams(
            dimension_semantics=("arbitrary",), vmem_limit_bytes=VMEM_LIMIT),
        name="moe_combine",
    )(pos, y, h2, wts)


def _lookup(table, idx):
    sel = idx[:, None] == jnp.arange(table.shape[0], dtype=jnp.int32)[None, :]
    return jnp.sum(jnp.where(sel, table[None, :], 0), axis=1).astype(jnp.int32)


def _dispatch(eid, n_tokens):
    n_assign = n_tokens * TOP_K_FINE
    experts = jnp.arange(N_EXPERTS, dtype=jnp.int32)
    e_flat = eid.reshape(n_assign)
    a_ids = jnp.arange(n_assign, dtype=jnp.int32)
    e_s, order = lax.sort_key_val(e_flat, a_ids)
    counts = jnp.sum((e_flat[:, None] == experts[None, :]).astype(jnp.int32), axis=0)
    starts = jnp.cumsum(counts) - counts
    nb = (counts + MOE_BLOCK - 1) // MOE_BLOCK
    blk_end = jnp.cumsum(nb)
    blk_start = blk_end - nb
    n_used = blk_end[-1]
    n_blocks = (n_assign + N_EXPERTS * (MOE_BLOCK - 1)) // MOE_BLOCK
    b_ids = jnp.arange(n_blocks, dtype=jnp.int32)
    used = b_ids < n_used
    blk_exp = jnp.minimum(jnp.sum((blk_end[None, :] <= b_ids[:, None]).astype(jnp.int32), axis=1),
                          N_EXPERTS - 1)
    j = b_ids - _lookup(blk_start, blk_exp)
    base = jnp.where(used, _lookup(starts, blk_exp) + j * MOE_BLOCK, 0)
    nval = jnp.where(used, jnp.clip(_lookup(counts, blk_exp) - j * MOE_BLOCK, 0, MOE_BLOCK), 0)
    first = (used & (j == 0)).astype(jnp.int32)
    active = counts > 0
    par = _lookup(jnp.cumsum(active.astype(jnp.int32)) - 1, blk_exp) & 1
    later = lax.cummin(jnp.where(active, experts, N_EXPERTS), reverse=True)
    nxt_e = jnp.concatenate([later[1:], jnp.full((1,), N_EXPERTS, jnp.int32)])
    nxt = _lookup(jnp.where(nxt_e == N_EXPERTS, -1, nxt_e), blk_exp)
    meta = tuple(v.astype(jnp.int32) for v in (blk_exp, par, first, nxt, base, nval))
    row_sorted = a_ids + _lookup(blk_start * MOE_BLOCK - starts, e_s)
    _, pos = lax.sort_key_val(order, row_sorted)
    tok_sorted = jnp.concatenate([lax.shift_right_logical(order, 1), jnp.zeros((ROW_GROUP,), jnp.int32)])
    return meta, tok_sorted, pos, n_used.astype(jnp.int32).reshape(1)


def kernel(x, mem, norm1_w, w_in, q_norm_w, k_norm_w, lambda_q1, lambda_k1, lambda_q2, lambda_k2, subln_w, ssm_lambda_re, ssm_lambda_im, ssm_log_dt, ssm_b_re, ssm_b_im, ssm_c_re, ssm_c_im, ssm_d, ssm_glu_w, ssm_glu_b, ssm_out_norm_w, w_out, norm2_w, mem_norm_w, xq_w, xkv_w, xq_norm_w, xk_norm_w, xo_w, norm3_w, router_coarse_w, router_coarse_b, router_fine_w, router_fine_b, expert_w_gate, expert_w_up, expert_w_down):
    batch, seq, d = x.shape
    mem_len = mem.shape[1]
    t = batch * seq
    depth = norm1_w.shape[0]
    d_attn = DA_HEADS * DA_V_DIM
    d_ssm = d - d_attn
    qk_cols = DA_HEADS * 2 * DA_QK_DIM
    x_hd = d // X_HEADS
    h = x.reshape(t, d)
    mem2 = mem.reshape(batch * mem_len, d)

    for l in range(depth):
        lam_init = 0.8 - 0.6 * math.exp(-0.3 * l)
        lam = (jnp.exp(jnp.sum(lambda_q1[l].astype(F32) * lambda_k1[l].astype(F32)))
               - jnp.exp(jnp.sum(lambda_q2[l].astype(F32) * lambda_k2[l].astype(F32)))
               + lam_init).reshape(1)

        n_rep = qk_cols // DA_QK_DIM
        in_gain = jnp.concatenate([
            jnp.tile(q_norm_w[l].astype(F32) * (DA_QK_DIM ** -0.5 * LOG2E), n_rep),
            jnp.tile(k_norm_w[l].astype(F32), n_rep),
            jnp.ones((d_attn + d_ssm,), F32)])
        proj = _norm_matmul(h, norm1_w[l], w_in[l].astype(BF16), in_gain,
                            n_norm_cols=2 * qk_cols, chunk=DA_QK_DIM, tm=1024, tn=1024, name="in_proj")
        sub_gain = (subln_w[l].astype(F32) * (1.0 - lam_init)).reshape(1, DA_V_DIM)
        a = _diff_attn(proj, lam, sub_gain, batch=batch, seq=seq, tq=512)

        seg_len = seq // SCAN_SEGS
        u = proj[:, 2 * qk_cols + d_attn:]
        u_perm = u.reshape(batch, SCAN_SEGS, seg_len, d_ssm).transpose(0, 2, 1, 3).reshape(batch, seq, d_ssm)
        bd, a_re, a_im, cd, dd = _s5_params(ssm_lambda_re[l], ssm_lambda_im[l], ssm_log_dt[l],
                                            ssm_b_re[l], ssm_b_im[l], ssm_c_re[l], ssm_c_im[l], ssm_d[l])
        y_perm = _s5(u_perm, bd, a_re, a_im, cd, dd)
        y = y_perm.reshape(batch, seg_len, SCAN_SEGS, d_ssm).transpose(0, 2, 1, 3).reshape(t, d_ssm)
        h = _mix_out(a, y, h, ssm_glu_w[l].astype(BF16), ssm_glu_b[l], ssm_out_norm_w[l],
                     w_out[l].astype(BF16), tm=512)

        kv_gain = jnp.concatenate([jnp.tile(xk_norm_w[l].astype(F32), X_HEADS), jnp.ones((d,), F32)])
        kv = _norm_matmul(mem2, mem_norm_w[l], xkv_w[l].astype(BF16), kv_gain,
                          n_norm_cols=d, chunk=x_hd, tm=512, tn=512, name="kv_proj")
        q_gain = jnp.tile(xq_norm_w[l].astype(F32) * (x_hd ** -0.5), X_HEADS)
        q = _norm_matmul(h, norm2_w[l], xq_w[l].astype(BF16), q_gain,
                         n_norm_cols=d, chunk=x_hd, tm=1024, tn=1024, name="xq_proj")
        r_w = jnp.concatenate([router_coarse_w[l].astype(F32), router_fine_w[l].astype(F32)], axis=1)
        r_w = jnp.pad(r_w, ((0, 0), (0, ROUTE_LANES - r_w.shape[1])))
        r_hi = r_w.astype(BF16)
        r_lo = (r_w - r_hi.astype(F32)).astype(BF16)
        r_b = jnp.concatenate([router_coarse_b[l].astype(F32), router_fine_b[l].astype(F32)])
        r_b = jnp.pad(r_b, (0, ROUTE_LANES - r_b.shape[0])).reshape(1, ROUTE_LANES)
        h2, hn3, eid, wts = _xattn_route(q, kv, h, xo_w[l].astype(BF16), norm3_w[l], r_hi, r_lo, r_b,
                                         batch=batch, seq=seq, mem_len=mem_len, tm=512)

        meta, tok_sorted, pos, n_used = _dispatch(eid[:, :TOP_K_FINE], t)
        y = _moe_experts(hn3, expert_w_gate[l], expert_w_up[l], expert_w_down[l], meta, tok_sorted, n_used)
        h = _combine(y, h2, wts, pos, tm=256)

    return h.reshape(batch, seq, d)
```

```python
import functools
import math

import jax
import jax.numpy as jnp
from jax import lax
from jax.experimental import pallas as pl
from jax.experimental.pallas import tpu as pltpu

F32 = jnp.float32
BF16 = jnp.bfloat16

EPS = 1e-6
DA_HEADS = 4
DA_QK_DIM = 128
DA_V_DIM = 256
SSM_GROUP = 16
SSM_STATE = 64
X_HEADS = 4
MOE_GROUPS = 8
EXP_PER_GROUP = 8
N_EXPERTS = MOE_GROUPS * EXP_PER_GROUP
TOP_K_FINE = 2

LANES = 128
SUBLANES = 8
MXU_TILE = 256
VMEM_LIMIT = 56 * 1024 * 1024
NEG = -1e30
LOG2E = math.log2(math.e)

SSM_CHUNK_GROUPS = LANES // SSM_GROUP
SSM_CHUNK_STATE = SSM_CHUNK_GROUPS * SSM_STATE
SCAN_SEGS = SUBLANES
MOE_BLOCK = 256
ROW_GROUP = SUBLANES
WEIGHT_DMA_CHUNKS = 4
ROUTE_LANES = LANES


def _rms(x, eps=EPS):
    return x * lax.rsqrt(jnp.mean(x * x, axis=-1, keepdims=True) + eps)


def _dot(a, b):
    return jnp.dot(a, b, preferred_element_type=F32)


def _dot_nt(a, b):
    return lax.dot_general(a, b, (((1,), (1,)), ((), ())), preferred_element_type=F32)


def _resident(shape, index_map):
    return pl.BlockSpec(shape, index_map, pipeline_mode=pl.Buffered(1))


def _norm_matmul_kernel(x_ref, nw_ref, w_ref, g_ref, o_ref, xn_ref, *, n_norm_tiles, chunk):
    j = pl.program_id(1)

    @pl.when(j == 0)
    def _():
        x = x_ref[...].astype(F32)
        xn_ref[...] = (_rms(x) * nw_ref[...]).astype(BF16)

    normed = j < n_norm_tiles
    tn = w_ref.shape[1]
    sub = max(chunk, MXU_TILE)
    for s in range(tn // sub):
        acc = _dot(xn_ref[...], w_ref[:, s * sub:(s + 1) * sub])
        for c in range(sub // chunk):
            lo = s * sub + c * chunk
            a = acc[:, c * chunk:(c + 1) * chunk]
            inv = lax.rsqrt(jnp.mean(a * a, axis=-1, keepdims=True) + EPS)
            scale = jnp.where(normed, inv, 1.0)
            o_ref[:, lo:lo + chunk] = (a * scale * g_ref[:, lo:lo + chunk]).astype(o_ref.dtype)


def _norm_matmul(x, norm_w, w_bf16, gain, *, n_norm_cols, chunk, tm, tn, name):
    m, k = x.shape
    n = w_bf16.shape[1]
    assert m % tm == 0 and n % tn == 0 and tn % max(chunk, MXU_TILE) == 0 and n_norm_cols % tn == 0
    kern = functools.partial(_norm_matmul_kernel, n_norm_tiles=n_norm_cols // tn, chunk=chunk)
    return pl.pallas_call(
        kern,
        grid=(m // tm, n // tn),
        in_specs=[
            pl.BlockSpec((tm, k), lambda i, j: (i, 0)),
            pl.BlockSpec((1, k), lambda i, j: (0, 0)),
            pl.BlockSpec((k, tn), lambda i, j: (0, j)),
            pl.BlockSpec((1, tn), lambda i, j: (0, j)),
        ],
        out_specs=pl.BlockSpec((tm, tn), lambda i, j: (i, j)),
        out_shape=jax.ShapeDtypeStruct((m, n), BF16),
        scratch_shapes=[pltpu.VMEM((tm, k), BF16)],
        compiler_params=pltpu.CompilerParams(
            dimension_semantics=("parallel", "arbitrary"), vmem_limit_bytes=VMEM_LIMIT),
        name=name,
    )(x, norm_w.reshape(1, k).astype(F32), w_bf16, gain.reshape(1, n).astype(F32))


def _diff_attn_kernel(lam_ref, q_ref, k_ref, v_ref, g_ref, o_ref,
                      m1_ref, l1_ref, acc1_ref, m2_ref, l2_ref, acc2_ref, *, tq):
    qi = pl.program_id(2)
    stats = ((m1_ref, l1_ref, acc1_ref), (m2_ref, l2_ref, acc2_ref))
    for m_ref, l_ref, acc_ref in stats:
        m_ref[...] = jnp.full(m_ref.shape, NEG, F32)
        l_ref[...] = jnp.zeros(l_ref.shape, F32)
        acc_ref[...] = jnp.zeros(acc_ref.shape, F32)

    def scores(j):
        start = pl.multiple_of(j * tq, tq)
        return tuple(_dot_nt(q_ref[:, c * DA_QK_DIM:(c + 1) * DA_QK_DIM],
                             k_ref[pl.ds(start, tq), c * DA_QK_DIM:(c + 1) * DA_QK_DIM])
                     for c in range(2))

    def accumulate(j, s_pair, masked):
        start = pl.multiple_of(j * tq, tq)
        for s, (m_ref, l_ref, acc_ref) in zip(s_pair, stats):
            if masked:
                row = lax.broadcasted_iota(jnp.int32, s.shape, 0)
                col = lax.broadcasted_iota(jnp.int32, s.shape, 1)
                s = jnp.where(col <= row, s, NEG)
            m_old = m_ref[...]
            m_new = jnp.maximum(m_old, jnp.max(s, axis=-1, keepdims=True))
            p = jnp.exp2(s - m_new)
            alpha = jnp.exp2(m_old - m_new)
            l_ref[...] = alpha * l_ref[...] + jnp.sum(p, axis=-1, keepdims=True)
            acc_ref[...] = alpha * acc_ref[...] + _dot(p.astype(BF16), v_ref[pl.ds(start, tq), :])
            m_ref[...] = m_new

    def off_diag(j, s_pair):
        s_next = scores(j + 1)
        accumulate(j, s_pair, False)
        return s_next

    s_diag = lax.fori_loop(0, qi, off_diag, scores(0))
    accumulate(qi, s_diag, True)

    lam = lam_ref[0]
    o = acc1_ref[...] / l1_ref[...] - lam * (acc2_ref[...] / l2_ref[...])
    o_ref[...] = (_rms(o) * g_ref[...]).astype(o_ref.dtype)


def _diff_attn(proj, lam, gain, *, batch, seq, tq):
    t = batch * seq
    nq = seq // tq
    width = 2 * DA_QK_DIM
    k_blk0 = DA_HEADS
    v_blk0 = 2 * DA_HEADS
    kern = functools.partial(_diff_attn_kernel, tq=tq)
    return pl.pallas_call(
        kern,
        grid_spec=pltpu.PrefetchScalarGridSpec(
            num_scalar_prefetch=1,
            grid=(batch, DA_HEADS, nq),
            in_specs=[
                pl.BlockSpec((tq, width), lambda b, h, i, lam: (b * nq + i, h)),
                pl.BlockSpec((seq, width), lambda b, h, i, lam: (b, k_blk0 + h)),
                pl.BlockSpec((seq, width), lambda b, h, i, lam: (b, v_blk0 + h)),
                pl.BlockSpec((1, DA_V_DIM), lambda b, h, i, lam: (0, 0)),
            ],
            out_specs=pl.BlockSpec((tq, DA_V_DIM), lambda b, h, i, lam: (b * nq + i, h)),
            scratch_shapes=[
                pltpu.VMEM((tq, 1), F32), pltpu.VMEM((tq, 1), F32), pltpu.VMEM((tq, DA_V_DIM), F32),
                pltpu.VMEM((tq, 1), F32), pltpu.VMEM((tq, 1), F32), pltpu.VMEM((tq, DA_V_DIM), F32),
            ],
        ),
        out_shape=jax.ShapeDtypeStruct((t, DA_HEADS * DA_V_DIM), BF16),
        compiler_params=pltpu.CompilerParams(
            dimension_semantics=("parallel", "parallel", "arbitrary"), vmem_limit_bytes=VMEM_LIMIT),
        name="diff_attn",
    )(lam, proj, proj, proj, gain)


def _s5_kernel(u_ref, bd_ref, ar_ref, ai_ref, cd_ref, d_ref, o_ref, xs_ref, *, seq, rows):
    ns = SSM_CHUNK_STATE
    seg_len = seq // SCAN_SEGS
    n_row_blk = seq // rows

    def in_map(r, carry):
        r0 = pl.multiple_of(r * rows, rows)
        xs_ref[pl.ds(r0, rows), :] = _dot(u_ref[pl.ds(r0, rows), :], bd_ref[...])
        return carry

    lax.fori_loop(0, n_row_blk, in_map, 0)

    ar = jnp.broadcast_to(ar_ref[...], (SCAN_SEGS, ns))
    ai = jnp.broadcast_to(ai_ref[...], (SCAN_SEGS, ns))

    def advance(t, sr, si):
        t0 = pl.multiple_of(t * SCAN_SEGS, SCAN_SEGS)
        br = xs_ref[pl.ds(t0, SCAN_SEGS), 0:ns]
        bi = xs_ref[pl.ds(t0, SCAN_SEGS), ns:2 * ns]
        return t0, ar * sr - ai * si + br, ar * si + ai * sr + bi

    def local_step(t, carry):
        _, nr, ni = advance(t, *carry)
        return nr, ni

    zero = jnp.zeros((SCAN_SEGS, ns), F32)
    fr, fi = lax.fori_loop(0, seg_len, local_step, (zero, zero))

    pr, pi = ar, ai
    for _ in range(int(math.log2(seg_len))):
        pr, pi = pr * pr - pi * pi, 2.0 * pr * pi
    seg = lax.broadcasted_iota(jnp.int32, (SCAN_SEGS, ns), 0)

    def shifted(x, k):
        return jnp.where(seg >= k, pltpu.roll(x, k, 0), 0.0)

    k = 1
    while k < SCAN_SEGS:
        gr, gi = shifted(fr, k), shifted(fi, k)
        fr, fi = fr + pr * gr - pi * gi, fi + pr * gi + pi * gr
        pr, pi = pr * pr - pi * pi, 2.0 * pr * pi
        k *= 2
    sr0, si0 = shifted(fr, 1), shifted(fi, 1)

    def global_step(t, carry):
        t0, nr, ni = advance(t, *carry)
        xs_ref[pl.ds(t0, SCAN_SEGS), 0:ns] = nr
        xs_ref[pl.ds(t0, SCAN_SEGS), ns:2 * ns] = ni
        return nr, ni

    lax.fori_loop(0, seg_len, global_step, (sr0, si0))

    def out_map(r, carry):
        r0 = pl.multiple_of(r * rows, rows)
        x = xs_ref[pl.ds(r0, rows), :].astype(BF16)
        y = _dot(x, cd_ref[...]) + d_ref[...] * u_ref[pl.ds(r0, rows), :].astype(F32)
        o_ref[pl.ds(r0, rows), :] = jax.nn.gelu(y).astype(o_ref.dtype)
        return carry

    lax.fori_loop(0, n_row_blk, out_map, 0)


def _s5(u_perm, bd, a_re, a_im, cd, d_skip, *, rows=256):
    batch, seq, d_ssm = u_perm.shape
    n_chunks = d_ssm // LANES
    kern = functools.partial(_s5_kernel, seq=seq, rows=rows)
    return pl.pallas_call(
        kern,
        grid=(batch, n_chunks),
        in_specs=[
            pl.BlockSpec((None, seq, LANES), lambda b, c: (b, 0, c)),
            pl.BlockSpec((None, LANES, 2 * SSM_CHUNK_STATE), lambda b, c: (c, 0, 0)),
            pl.BlockSpec((None, 1, SSM_CHUNK_STATE), lambda b, c: (c, 0, 0)),
            pl.BlockSpec((None, 1, SSM_CHUNK_STATE), lambda b, c: (c, 0, 0)),
            pl.BlockSpec((None, 2 * SSM_CHUNK_STATE, LANES), lambda b, c: (c, 0, 0)),
            pl.BlockSpec((None, 1, LANES), lambda b, c: (c, 0, 0)),
        ],
        out_specs=pl.BlockSpec((None, seq, LANES), lambda b, c: (b, 0, c)),
        out_shape=jax.ShapeDtypeStruct((batch, seq, d_ssm), BF16),
        scratch_shapes=[pltpu.VMEM((seq, 2 * SSM_CHUNK_STATE), F32)],
        compiler_params=pltpu.CompilerParams(
            dimension_semantics=("parallel", "parallel"), vmem_limit_bytes=VMEM_LIMIT),
        name="s5_scan",
    )(u_perm, bd, a_re, a_im, cd, d_skip)


def _s5_params(lam_re, lam_im, log_dt, b_re, b_im, c_re, c_im, d_skip):
    g = lam_re.shape[0]
    nc = g // SSM_CHUNK_GROUPS
    lr = jnp.minimum(lam_re.astype(F32), -1e-4)
    li = lam_im.astype(F32)
    dt = jnp.exp(log_dt.astype(F32))[:, None]
    mag = jnp.exp(lr * dt)
    lb_re, lb_im = mag * jnp.cos(li * dt), mag * jnp.sin(li * dt)
    den = lr * lr + li * li
    coef_re = ((lb_re - 1.0) * lr + lb_im * li) / den
    coef_im = (lb_im * lr - (lb_re - 1.0) * li) / den
    br, bi = b_re.astype(F32), b_im.astype(F32)
    bb_re = coef_re[..., None] * br - coef_im[..., None] * bi
    bb_im = coef_re[..., None] * bi + coef_im[..., None] * br
    eye = jnp.eye(SSM_CHUNK_GROUPS, dtype=F32)

    def pack_in(bb):
        bb = bb.reshape(nc, SSM_CHUNK_GROUPS, SSM_STATE, SSM_GROUP)
        return jnp.einsum('cgph,gk->cghkp', bb, eye).reshape(nc, LANES, SSM_CHUNK_STATE)

    def pack_out(cc):
        cc = cc.astype(F32).reshape(nc, SSM_CHUNK_GROUPS, SSM_GROUP, SSM_STATE)
        return jnp.einsum('cghp,gk->ckpgh', cc, eye).reshape(nc, SSM_CHUNK_STATE, LANES)

    bd = jnp.concatenate([pack_in(bb_re), pack_in(bb_im)], axis=-1).astype(BF16)
    cd = jnp.concatenate([pack_out(c_re), -pack_out(c_im)], axis=1).astype(BF16)
    a_re = lb_re.reshape(nc, 1, SSM_CHUNK_STATE)
    a_im = lb_im.reshape(nc, 1, SSM_CHUNK_STATE)
    dd = d_skip.astype(F32).reshape(nc, 1, LANES)
    return bd, a_re, a_im, cd, dd


def _mix_out_kernel(a_ref, y_ref, x_ref, gw_ref, gb_ref, nw_ref, wo_ref, o_ref):
    d_attn = a_ref.shape[1]
    y = y_ref[...]
    gate = _dot(y, gw_ref[...]) + gb_ref[...]
    s = y.astype(F32) * jax.nn.sigmoid(gate)
    sn = (_rms(s) * nw_ref[...]).astype(BF16)
    acc = _dot(a_ref[...], wo_ref[0:d_attn, :]) + _dot(sn, wo_ref[d_attn:, :])
    o_ref[...] = x_ref[...] + acc


def _mix_out(a, y, x, glu_w, glu_b, norm_w, w_out, *, tm):
    t, d = x.shape
    d_attn, d_ssm = a.shape[1], y.shape[1]
    const = lambda i: (0, 0)
    return pl.pallas_call(
        _mix_out_kernel,
        grid=(t // tm,),
        in_specs=[
            pl.BlockSpec((tm, d_attn), lambda i: (i, 0)),
            pl.BlockSpec((tm, d_ssm), lambda i: (i, 0)),
            pl.BlockSpec((tm, d), lambda i: (i, 0)),
            _resident((d_ssm, d_ssm), const),
            _resident((1, d_ssm), const),
            _resident((1, d_ssm), const),
            _resident((d, d), const),
        ],
        out_specs=pl.BlockSpec((tm, d), lambda i: (i, 0)),
        out_shape=jax.ShapeDtypeStruct((t, d), F32),
        compiler_params=pltpu.CompilerParams(
            dimension_semantics=("parallel",), vmem_limit_bytes=VMEM_LIMIT),
        name="mix_out",
    )(a, y, x, glu_w, glu_b.reshape(1, d_ssm).astype(F32), norm_w.reshape(1, d_ssm).astype(F32), w_out)


def _xattn_route_kernel(q_ref, k_ref, v_ref, h_ref, xo_ref, nw_ref, rhi_ref, rlo_ref, rb_ref,
                        h2_ref, hn_ref, eid_ref, wts_ref):
    d = h_ref.shape[1]
    hd = d // X_HEADS
    h2 = h_ref[...]
    for h in range(X_HEADS):
        sl = slice(h * hd, (h + 1) * hd)
        s = _dot_nt(q_ref[:, sl], k_ref[:, sl])
        p = jnp.exp(s - jnp.max(s, axis=-1, keepdims=True))
        p = p * (1.0 / jnp.sum(p, axis=-1, keepdims=True))
        o = _dot(p.astype(BF16), v_ref[:, sl]).astype(BF16)
        h2 = h2 + _dot(o, xo_ref[sl, :])
    h2_ref[...] = h2
    hn = _rms(h2) * nw_ref[...]
    hn_ref[...] = hn

    hi = hn.astype(BF16)
    lo = (hn - hi.astype(F32)).astype(BF16)
    logits = (_dot(hi, rhi_ref[...]) + _dot(hi, rlo_ref[...]) + _dot(lo, rhi_ref[...])) + rb_ref[...]

    lane = lax.broadcasted_iota(jnp.int32, logits.shape, 1)
    big = jnp.int32(ROUTE_LANES)

    def first_lane(cond):
        return jnp.min(jnp.where(cond, lane, big), axis=-1, keepdims=True)

    c_mask = lane < MOE_GROUPS
    lc = jnp.where(c_mask, logits, NEG)
    mc = jnp.max(lc, axis=-1, keepdims=True)
    ec = jnp.exp(lc - mc)
    p_c = ec / jnp.sum(ec, axis=-1, keepdims=True)
    p_grp = jnp.max(p_c, axis=-1, keepdims=True)
    grp = first_lane(c_mask & (p_c == p_grp))
    f_lo = MOE_GROUPS + grp * EXP_PER_GROUP
    f_mask = (lane >= f_lo) & (lane < f_lo + EXP_PER_GROUP)
    lf = jnp.where(f_mask, logits, NEG)
    mf = jnp.max(lf, axis=-1, keepdims=True)
    ef = jnp.exp(lf - mf)
    pf = ef / jnp.sum(ef, axis=-1, keepdims=True)
    v1 = jnp.max(jnp.where(f_mask, pf, -1.0), axis=-1, keepdims=True)
    i1 = first_lane(f_mask & (pf == v1))
    rest = f_mask & (lane != i1)
    v2 = jnp.max(jnp.where(rest, pf, -1.0), axis=-1, keepdims=True)
    i2 = first_lane(rest & (pf == v2))
    tot = v1 + v2
    w1 = v1 / tot * p_grp
    w2 = v2 / tot * p_grp
    eid_ref[...] = jnp.where(lane == 0, i1 - MOE_GROUPS, jnp.where(lane == 1, i2 - MOE_GROUPS, 0))
    wts_ref[...] = jnp.where(lane == 0, w1, jnp.where(lane == 1, w2, 0.0))


def _xattn_route(q, kv, h1, xo_w, norm_w, r_hi, r_lo, r_b, *, batch, seq, mem_len, tm):
    t, d = h1.shape
    n = seq // tm
    const = lambda b, i: (0, 0)
    row = lambda b, i: (b * n + i, 0)
    return pl.pallas_call(
        _xattn_route_kernel,
        grid=(batch, n),
        in_specs=[
            pl.BlockSpec((tm, d), row),
            pl.BlockSpec((mem_len, d), lambda b, i: (b, 0)),
            pl.BlockSpec((mem_len, d), lambda b, i: (b, 1)),
            pl.BlockSpec((tm, d), row),
            _resident((d, d), const),
            _resident((1, d), const),
            _resident((d, ROUTE_LANES), const),
            _resident((d, ROUTE_LANES), const),
            _resident((1, ROUTE_LANES), const),
        ],
        out_specs=[
            pl.BlockSpec((tm, d), row),
            pl.BlockSpec((tm, d), row),
            pl.BlockSpec((tm, ROUTE_LANES), row),
            pl.BlockSpec((tm, ROUTE_LANES), row),
        ],
        out_shape=[
            jax.ShapeDtypeStruct((t, d), F32),
            jax.ShapeDtypeStruct((t, d), F32),
            jax.ShapeDtypeStruct((t, ROUTE_LANES), jnp.int32),
            jax.ShapeDtypeStruct((t, ROUTE_LANES), F32),
        ],
        compiler_params=pltpu.CompilerParams(
            dimension_semantics=("parallel", "parallel"), vmem_limit_bytes=VMEM_LIMIT),
        name="xattn_route",
    )(q, kv, kv, h1, xo_w, norm_w.reshape(1, d).astype(F32), r_hi, r_lo, r_b)


def _moe_kernel(be_ref, par_ref, first_ref, nxt_ref, base_ref, nval_ref, tok_ref, nu_ref,
                hn_hbm, wg_hbm, wu_hbm, wd_hbm, o_ref,
                xbuf0, xbuf1, wgb, wub, wdb, gsem, wsem):
    b = pl.program_id(0)
    n_used = nu_ref[0]
    xbufs = (xbuf0, xbuf1)

    def weight_copies(e, slot):
        copies = []
        for hbm, buf in ((wg_hbm, wgb), (wu_hbm, wub), (wd_hbm, wdb)):
            rows = hbm.shape[1] // WEIGHT_DMA_CHUNKS
            for c in range(WEIGHT_DMA_CHUNKS):
                sl = pl.ds(c * rows, rows)
                copies.append(pltpu.make_async_copy(hbm.at[e, sl], buf.at[slot, sl], wsem.at[slot]))
        return copies

    def start_gather(blk, slot):
        base = base_ref[blk]
        for g in range(MOE_BLOCK // ROW_GROUP):
            for r in range(ROW_GROUP):
                tok = tok_ref[base + g * ROW_GROUP + r]
                pltpu.make_async_copy(hn_hbm.at[pl.ds(tok, 1)], xbufs[slot].at[g, pl.ds(r, 1)],
                                      gsem.at[slot]).start()

    def wait_gather(slot):
        pltpu.make_async_copy(xbufs[slot], xbufs[slot], gsem.at[slot]).wait()

    @pl.when(b == 0)
    def _():
        for c in weight_copies(be_ref[0], par_ref[0]):
            c.start()
        start_gather(0, 0)

    @pl.when(b < n_used)
    def _():
        wslot = par_ref[b]
        is_first = first_ref[b] == 1

        @pl.when(is_first & (nxt_ref[b] >= 0))
        def _():
            for c in weight_copies(nxt_ref[b], 1 - wslot):
                c.start()

        @pl.when(is_first)
        def _():
            for c in weight_copies(0, wslot):
                c.wait()

        def run(slot):
            wait_gather(slot)
            x = xbufs[slot][...].reshape(MOE_BLOCK, o_ref.shape[1]).astype(BF16)
            start_gather(b + 1, 1 - slot)
            g = _dot(x, wgb[wslot].astype(BF16))
            u = _dot(x, wub[wslot].astype(BF16))
            mid = (jax.nn.silu(g) * u).astype(BF16)
            o_ref[...] = _dot(mid, wdb[wslot].astype(BF16))

            @pl.when(b == n_used - 1)
            def _():
                wait_gather(1 - slot)

        for slot in range(2):
            pl.when(b % 2 == slot)(functools.partial(run, slot))

    @pl.when(b >= n_used)
    def _():
        o_ref[...] = jnp.zeros(o_ref.shape, o_ref.dtype)


def _moe_experts(hn, w_gate, w_up, w_down, meta, tok_sorted, n_used):
    t, d = hn.shape
    d_ff = w_gate.shape[2]
    blk_exp, par, first, nxt, base, nval = meta
    n_blocks = blk_exp.shape[0] - 1
    any_spec = pl.BlockSpec(memory_space=pl.ANY)
    return pl.pallas_call(
        _moe_kernel,
        grid_spec=pltpu.PrefetchScalarGridSpec(
            num_scalar_prefetch=8,
            grid=(n_blocks,),
            in_specs=[any_spec, any_spec, any_spec, any_spec],
            out_specs=pl.BlockSpec((MOE_BLOCK, d), lambda b, *_: (b, 0)),
            scratch_shapes=[
                pltpu.VMEM((MOE_BLOCK // ROW_GROUP, ROW_GROUP, d), F32),
                pltpu.VMEM((MOE_BLOCK // ROW_GROUP, ROW_GROUP, d), F32),
                pltpu.VMEM((2, d, d_ff), F32),
                pltpu.VMEM((2, d, d_ff), F32),
                pltpu.VMEM((2, d_ff, d), F32),
                pltpu.SemaphoreType.DMA((2,)),
                pltpu.SemaphoreType.DMA((2,)),
            ],
        ),
        out_shape=jax.ShapeDtypeStruct((n_blocks * MOE_BLOCK, d), F32),
        compiler_params=pltpu.CompilerParams(
            dimension_semantics=("arbitrary",), vmem_limit_bytes=VMEM_LIMIT),
        name="moe_experts",
    )(blk_exp, par, first, nxt, base, nval, tok_sorted, n_used, hn, w_gate, w_up, w_down)


def _combine_kernel(pos_ref, y_hbm, h_ref, w_ref, o_ref, ybuf, sem, *, tm):
    i = pl.program_id(0)
    n_groups = tm // ROW_GROUP

    def start_gather(tile, slot):
        base = tile * (tm * TOP_K_FINE)

        def body(g, carry):
            for r in range(ROW_GROUP):
                for k in range(TOP_K_FINE):
                    row = pos_ref[base + (g * ROW_GROUP + r) * TOP_K_FINE + k]
                    pltpu.make_async_copy(y_hbm.at[pl.ds(row, 1)], ybuf.at[slot, k, g, pl.ds(r, 1)],
                                          sem.at[slot]).start()
            return carry
        lax.fori_loop(0, n_groups, body, 0)

    @pl.when(i == 0)
    def _():
        start_gather(0, 0)

    @pl.when(i + 1 < pl.num_programs(0))
    def _():
        start_gather(i + 1, (i + 1) % 2)

    slot = i % 2
    pltpu.make_async_copy(ybuf.at[slot], ybuf.at[slot], sem.at[slot]).wait()
    w = w_ref[...]
    d = o_ref.shape[1]
    y0 = ybuf[slot, 0].reshape(tm, d)
    y1 = ybuf[slot, 1].reshape(tm, d)
    o_ref[...] = h_ref[...] + (w[:, 0:1] * y0 + w[:, 1:2] * y1)


def _combine(y, h2, wts, pos, *, tm):
    t, d = h2.shape
    kern = functools.partial(_combine_kernel, tm=tm)
    return pl.pallas_call(
        kern,
        grid_spec=pltpu.PrefetchScalarGridSpec(
            num_scalar_prefetch=1,
            grid=(t // tm,),
            in_specs=[
                pl.BlockSpec(memory_space=pl.ANY),
                pl.BlockSpec((tm, d), lambda i, pos: (i, 0)),
                pl.BlockSpec((tm, ROUTE_LANES), lambda i, pos: (i, 0)),
            ],
            out_specs=pl.BlockSpec((tm, d), lambda i, pos: (i, 0)),
            scratch_shapes=[
                pltpu.VMEM((2, TOP_K_FINE, tm // ROW_GROUP, ROW_GROUP, d), F32),
                pltpu.SemaphoreType.DMA((2,)),
            ],
        ),
        out_shape=jax.ShapeDtypeStruct((t, d), F32),
        compiler_params=pltpu.CompilerParams(
            dimension_semantics=("arbitrary",), vmem_limit_bytes=VMEM_LIMIT),
        name="moe_combine",
    )(pos, y, h2, wts)


def _lookup(table, idx):
    sel = idx[:, None] == jnp.arange(table.shape[0], dtype=jnp.int32)[None, :]
    return jnp.sum(jnp.where(sel, table[None, :], 0), axis=1).astype(jnp.int32)


def _dispatch(eid, n_tokens):
    n_assign = n_tokens * TOP_K_FINE
    experts = jnp.arange(N_EXPERTS, dtype=jnp.int32)
    e_flat = eid.reshape(n_assign)
    a_ids = jnp.arange(n_assign, dtype=jnp.int32)
    e_s, order = lax.sort_key_val(e_flat, a_ids)
    counts = jnp.sum((e_flat[:, None] == experts[None, :]).astype(jnp.int32), axis=0)
    starts = jnp.cumsum(counts) - counts
    nb = (counts + MOE_BLOCK - 1) // MOE_BLOCK
    blk_end = jnp.cumsum(nb)
    blk_start = blk_end - nb
    n_used = blk_end[-1]
    n_blocks = (n_assign + N_EXPERTS * (MOE_BLOCK - 1)) // MOE_BLOCK
    b_ids = jnp.arange(n_blocks, dtype=jnp.int32)
    used = b_ids < n_used
    blk_exp = jnp.minimum(jnp.sum((blk_end[None, :] <= b_ids[:, None]).astype(jnp.int32), axis=1),
                          N_EXPERTS - 1)
    j = b_ids - _lookup(blk_start, blk_exp)
    base = jnp.where(used, _lookup(starts, blk_exp) + j * MOE_BLOCK, 0)
    nval = jnp.where(used, jnp.clip(_lookup(counts, blk_exp) - j * MOE_BLOCK, 0, MOE_BLOCK), 0)
    first = (used & (j == 0)).astype(jnp.int32)
    active = counts > 0
    par = _lookup(jnp.cumsum(active.astype(jnp.int32)) - 1, blk_exp) & 1
    later = lax.cummin(jnp.where(active, experts, N_EXPERTS), reverse=True)
    nxt_e = jnp.concatenate([later[1:], jnp.full((1,), N_EXPERTS, jnp.int32)])
    nxt = _lookup(jnp.where(nxt_e == N_EXPERTS, -1, nxt_e), blk_exp)
    meta = tuple(jnp.pad(v.astype(jnp.int32), (0, 1)) for v in (blk_exp, par, first, nxt, base, nval))
    row_sorted = a_ids + _lookup(blk_start * MOE_BLOCK - starts, e_s)
    _, pos = lax.sort_key_val(order, row_sorted)
    tok_sorted = jnp.concatenate([lax.shift_right_logical(order, 1), jnp.zeros((MOE_BLOCK,), jnp.int32)])
    return meta, tok_sorted, pos, n_used.astype(jnp.int32).reshape(1)


def kernel(x, mem, norm1_w, w_in, q_norm_w, k_norm_w, lambda_q1, lambda_k1, lambda_q2, lambda_k2, subln_w, ssm_lambda_re, ssm_lambda_im, ssm_log_dt, ssm_b_re, ssm_b_im, ssm_c_re, ssm_c_im, ssm_d, ssm_glu_w, ssm_glu_b, ssm_out_norm_w, w_out, norm2_w, mem_norm_w, xq_w, xkv_w, xq_norm_w, xk_norm_w, xo_w, norm3_w, router_coarse_w, router_coarse_b, router_fine_w, router_fine_b, expert_w_gate, expert_w_up, expert_w_down):
    batch, seq, d = x.shape
    mem_len = mem.shape[1]
    t = batch * seq
    depth = norm1_w.shape[0]
    d_attn = DA_HEADS * DA_V_DIM
    d_ssm = d - d_attn
    qk_cols = DA_HEADS * 2 * DA_QK_DIM
    x_hd = d // X_HEADS
    h = x.reshape(t, d)
    mem2 = mem.reshape(batch * mem_len, d)

    for l in range(depth):
        lam_init = 0.8 - 0.6 * math.exp(-0.3 * l)
        lam = (jnp.exp(jnp.sum(lambda_q1[l].astype(F32) * lambda_k1[l].astype(F32)))
               - jnp.exp(jnp.sum(lambda_q2[l].astype(F32) * lambda_k2[l].astype(F32)))
               + lam_init).reshape(1)

        n_rep = qk_cols // DA_QK_DIM
        in_gain = jnp.concatenate([
            jnp.tile(q_norm_w[l].astype(F32) * (DA_QK_DIM ** -0.5 * LOG2E), n_rep),
            jnp.tile(k_norm_w[l].astype(F32), n_rep),
            jnp.ones((d_attn + d_ssm,), F32)])
        proj = _norm_matmul(h, norm1_w[l], w_in[l].astype(BF16), in_gain,
                            n_norm_cols=2 * qk_cols, chunk=DA_QK_DIM, tm=1024, tn=1024, name="in_proj")
        sub_gain = (subln_w[l].astype(F32) * (1.0 - lam_init)).reshape(1, DA_V_DIM)
        a = _diff_attn(proj, lam, sub_gain, batch=batch, seq=seq, tq=512)

        seg_len = seq // SCAN_SEGS
        u = proj[:, 2 * qk_cols + d_attn:]
        u_perm = u.reshape(batch, SCAN_SEGS, seg_len, d_ssm).transpose(0, 2, 1, 3).reshape(batch, seq, d_ssm)
        bd, a_re, a_im, cd, dd = _s5_params(ssm_lambda_re[l], ssm_lambda_im[l], ssm_log_dt[l],
                                            ssm_b_re[l], ssm_b_im[l], ssm_c_re[l], ssm_c_im[l], ssm_d[l])
        y_perm = _s5(u_perm, bd, a_re, a_im, cd, dd)
        y = y_perm.reshape(batch, seg_len, SCAN_SEGS, d_ssm).transpose(0, 2, 1, 3).reshape(t, d_ssm)
        h = _mix_out(a, y, h, ssm_glu_w[l].astype(BF16), ssm_glu_b[l], ssm_out_norm_w[l],
                     w_out[l].astype(BF16), tm=512)

        kv_gain = jnp.concatenate([jnp.tile(xk_norm_w[l].astype(F32), X_HEADS), jnp.ones((d,), F32)])
        kv = _norm_matmul(mem2, mem_norm_w[l], xkv_w[l].astype(BF16), kv_gain,
                          n_norm_cols=d, chunk=x_hd, tm=512, tn=512, name="kv_proj")
        q_gain = jnp.tile(xq_norm_w[l].astype(F32) * (x_hd ** -0.5), X_HEADS)
        q = _norm_matmul(h, norm2_w[l], xq_w[l].astype(BF16), q_gain,
                         n_norm_cols=d, chunk=x_hd, tm=1024, tn=1024, name="xq_proj")
        r_w = jnp.concatenate([router_coarse_w[l].astype(F32), router_fine_w[l].astype(F32)], axis=1)
        r_w = jnp.pad(r_w, ((0, 0), (0, ROUTE_LANES - r_w.shape[1])))
        r_hi = r_w.astype(BF16)
        r_lo = (r_w - r_hi.astype(F32)).astype(BF16)
        r_b = jnp.concatenate([router_coarse_b[l].astype(F32), router_fine_b[l].astype(F32)])
        r_b = jnp.pad(r_b, (0, ROUTE_LANES - r_b.shape[0])).reshape(1, ROUTE_LANES)
        h2, hn3, eid, wts = _xattn_route(q, kv, h, xo_w[l].astype(BF16), norm3_w[l], r_hi, r_lo, r_b,
                                         batch=batch, seq=seq, mem_len=mem_len, tm=512)

        meta, tok_sorted, pos, n_used = _dispatch(eid[:, :TOP_K_FINE], t)
        y = _moe_experts(hn3, expert_w_gate[l], expert_w_up[l], expert_w_down[l], meta, tok_sorted, n_used)
        h = _combine(y, h2, wts, pos, tm=256)

    return h.reshape(batch, seq, d)
```

```python
import functools
import math

import jax
import jax.numpy as jnp
from jax import lax
from jax.experimental import pallas as pl
from jax.experimental.pallas import tpu as pltpu

F32 = jnp.float32
BF16 = jnp.bfloat16

EPS = 1e-6
DA_HEADS = 4
DA_QK_DIM = 128
DA_V_DIM = 256
SSM_GROUP = 16
SSM_STATE = 64
X_HEADS = 4
MOE_GROUPS = 8
EXP_PER_GROUP = 8
N_EXPERTS = MOE_GROUPS * EXP_PER_GROUP
TOP_K_FINE = 2

LANES = 128
SUBLANES = 8
MXU_TILE = 256
VMEM_LIMIT = 56 * 1024 * 1024
NEG = -1e30
LOG2E = math.log2(math.e)

SSM_CHUNK_GROUPS = LANES // SSM_GROUP
SSM_CHUNK_STATE = SSM_CHUNK_GROUPS * SSM_STATE
SCAN_SEGS = SUBLANES
MOE_BLOCK = 256
ROW_GROUP = SUBLANES
WEIGHT_DMA_CHUNKS = 4
ROUTE_LANES = LANES


def _rms(x, eps=EPS):
    return x * lax.rsqrt(jnp.mean(x * x, axis=-1, keepdims=True) + eps)


def _dot(a, b):
    return jnp.dot(a, b, preferred_element_type=F32)


def _dot_nt(a, b):
    return lax.dot_general(a, b, (((1,), (1,)), ((), ())), preferred_element_type=F32)


def _pack_halves(x):
    n = x.shape[1] // 2
    hi = lax.bitcast_convert_type(x[:, :n].astype(BF16).astype(F32), jnp.uint32)
    lo = lax.bitcast_convert_type(x[:, n:].astype(BF16).astype(F32), jnp.uint32)
    return hi | lax.shift_right_logical(lo, jnp.uint32(16))


def _unpack_halves(p):
    hi = lax.bitcast_convert_type(p & jnp.uint32(0xFFFF0000), F32)
    lo = lax.bitcast_convert_type(lax.shift_left(p, jnp.uint32(16)), F32)
    return hi, lo


def _resident(shape, index_map):
    return pl.BlockSpec(shape, index_map, pipeline_mode=pl.Buffered(1))


def _norm_matmul_kernel(x_ref, nw_ref, w_ref, g_ref, o_ref, xn_ref, *, n_norm_tiles, chunk):
    j = pl.program_id(1)

    @pl.when(j == 0)
    def _():
        x = x_ref[...].astype(F32)
        xn_ref[...] = (_rms(x) * nw_ref[...]).astype(BF16)

    normed = j < n_norm_tiles
    tn = w_ref.shape[1]
    sub = max(chunk, MXU_TILE)
    for s in range(tn // sub):
        acc = _dot(xn_ref[...], w_ref[:, s * sub:(s + 1) * sub])
        for c in range(sub // chunk):
            lo = s * sub + c * chunk
            a = acc[:, c * chunk:(c + 1) * chunk]
            inv = lax.rsqrt(jnp.mean(a * a, axis=-1, keepdims=True) + EPS)
            scale = jnp.where(normed, inv, 1.0)
            o_ref[:, lo:lo + chunk] = (a * scale * g_ref[:, lo:lo + chunk]).astype(o_ref.dtype)


def _norm_matmul(x, norm_w, w_bf16, gain, *, n_norm_cols, chunk, tm, tn, name):
    m, k = x.shape
    n = w_bf16.shape[1]
    assert m % tm == 0 and n % tn == 0 and tn % max(chunk, MXU_TILE) == 0 and n_norm_cols % tn == 0
    kern = functools.partial(_norm_matmul_kernel, n_norm_tiles=n_norm_cols // tn, chunk=chunk)
    return pl.pallas_call(
        kern,
        grid=(m // tm, n // tn),
        in_specs=[
            pl.BlockSpec((tm, k), lambda i, j: (i, 0)),
            pl.BlockSpec((1, k), lambda i, j: (0, 0)),
            pl.BlockSpec((k, tn), lambda i, j: (0, j)),
            pl.BlockSpec((1, tn), lambda i, j: (0, j)),
        ],
        out_specs=pl.BlockSpec((tm, tn), lambda i, j: (i, j)),
        out_shape=jax.ShapeDtypeStruct((m, n), BF16),
        scratch_shapes=[pltpu.VMEM((tm, k), BF16)],
        compiler_params=pltpu.CompilerParams(
            dimension_semantics=("parallel", "arbitrary"), vmem_limit_bytes=VMEM_LIMIT),
        name=name,
    )(x, norm_w.reshape(1, k).astype(F32), w_bf16, gain.reshape(1, n).astype(F32))


def _diff_attn_kernel(lam_ref, q_ref, k_ref, v_ref, g_ref, o_ref,
                      m1_ref, l1_ref, acc1_ref, m2_ref, l2_ref, acc2_ref, *, tq):
    qi = pl.program_id(2)
    stats = ((m1_ref, l1_ref, acc1_ref), (m2_ref, l2_ref, acc2_ref))
    for m_ref, l_ref, acc_ref in stats:
        m_ref[...] = jnp.full(m_ref.shape, NEG, F32)
        l_ref[...] = jnp.zeros(l_ref.shape, F32)
        acc_ref[...] = jnp.zeros(acc_ref.shape, F32)

    def scores(j):
        start = pl.multiple_of(j * tq, tq)
        return tuple(_dot_nt(q_ref[:, c * DA_QK_DIM:(c + 1) * DA_QK_DIM],
                             k_ref[pl.ds(start, tq), c * DA_QK_DIM:(c + 1) * DA_QK_DIM])
                     for c in range(2))

    def accumulate(j, s_pair, masked):
        start = pl.multiple_of(j * tq, tq)
        for s, (m_ref, l_ref, acc_ref) in zip(s_pair, stats):
            if masked:
                row = lax.broadcasted_iota(jnp.int32, s.shape, 0)
                col = lax.broadcasted_iota(jnp.int32, s.shape, 1)
                s = jnp.where(col <= row, s, NEG)
            m_old = m_ref[...]
            m_new = jnp.maximum(m_old, jnp.max(s, axis=-1, keepdims=True))
            p = jnp.exp2(s - m_new)
            alpha = jnp.exp2(m_old - m_new)
            l_ref[...] = alpha * l_ref[...] + jnp.sum(p, axis=-1, keepdims=True)
            acc_ref[...] = alpha * acc_ref[...] + _dot(p.astype(BF16), v_ref[pl.ds(start, tq), :])
            m_ref[...] = m_new

    def off_diag(j, s_pair):
        s_next = scores(j + 1)
        accumulate(j, s_pair, False)
        return s_next

    s_diag = lax.fori_loop(0, qi, off_diag, scores(0))
    accumulate(qi, s_diag, True)

    lam = lam_ref[0]
    o = acc1_ref[...] / l1_ref[...] - lam * (acc2_ref[...] / l2_ref[...])
    o_ref[...] = (_rms(o) * g_ref[...]).astype(o_ref.dtype)


def _diff_attn(proj, lam, gain, *, batch, seq, tq):
    t = batch * seq
    nq = seq // tq
    width = 2 * DA_QK_DIM
    k_blk0 = DA_HEADS
    v_blk0 = 2 * DA_HEADS
    kern = functools.partial(_diff_attn_kernel, tq=tq)
    return pl.pallas_call(
        kern,
        grid_spec=pltpu.PrefetchScalarGridSpec(
            num_scalar_prefetch=1,
            grid=(batch, DA_HEADS, nq),
            in_specs=[
                pl.BlockSpec((tq, width), lambda b, h, i, lam: (b * nq + i, h)),
                pl.BlockSpec((seq, width), lambda b, h, i, lam: (b, k_blk0 + h)),
                pl.BlockSpec((seq, width), lambda b, h, i, lam: (b, v_blk0 + h)),
                pl.BlockSpec((1, DA_V_DIM), lambda b, h, i, lam: (0, 0)),
            ],
            out_specs=pl.BlockSpec((tq, DA_V_DIM), lambda b, h, i, lam: (b * nq + i, h)),
            scratch_shapes=[
                pltpu.VMEM((tq, 1), F32), pltpu.VMEM((tq, 1), F32), pltpu.VMEM((tq, DA_V_DIM), F32),
                pltpu.VMEM((tq, 1), F32), pltpu.VMEM((tq, 1), F32), pltpu.VMEM((tq, DA_V_DIM), F32),
            ],
        ),
        out_shape=jax.ShapeDtypeStruct((t, DA_HEADS * DA_V_DIM), BF16),
        compiler_params=pltpu.CompilerParams(
            dimension_semantics=("parallel", "parallel", "arbitrary"), vmem_limit_bytes=VMEM_LIMIT),
        name="diff_attn",
    )(lam, proj, proj, proj, gain)


def _s5_kernel(u_ref, bd_ref, ar_ref, ai_ref, cd_ref, d_ref, o_ref, xs_ref, *, seq, rows):
    ns = SSM_CHUNK_STATE
    seg_len = seq // SCAN_SEGS
    n_row_blk = seq // rows

    def in_map(r, carry):
        r0 = pl.multiple_of(r * rows, rows)
        xs_ref[pl.ds(r0, rows), :] = _dot(u_ref[pl.ds(r0, rows), :], bd_ref[...])
        return carry

    lax.fori_loop(0, n_row_blk, in_map, 0)

    ar = jnp.broadcast_to(ar_ref[...], (SCAN_SEGS, ns))
    ai = jnp.broadcast_to(ai_ref[...], (SCAN_SEGS, ns))

    def advance(t, sr, si):
        t0 = pl.multiple_of(t * SCAN_SEGS, SCAN_SEGS)
        br = xs_ref[pl.ds(t0, SCAN_SEGS), 0:ns]
        bi = xs_ref[pl.ds(t0, SCAN_SEGS), ns:2 * ns]
        return t0, ar * sr - ai * si + br, ar * si + ai * sr + bi

    def local_step(t, carry):
        _, nr, ni = advance(t, *carry)
        return nr, ni

    zero = jnp.zeros((SCAN_SEGS, ns), F32)
    fr, fi = lax.fori_loop(0, seg_len, local_step, (zero, zero))

    pr, pi = ar, ai
    for _ in range(int(math.log2(seg_len))):
        pr, pi = pr * pr - pi * pi, 2.0 * pr * pi
    seg = lax.broadcasted_iota(jnp.int32, (SCAN_SEGS, ns), 0)

    def shifted(x, k):
        return jnp.where(seg >= k, pltpu.roll(x, k, 0), 0.0)

    k = 1
    while k < SCAN_SEGS:
        gr, gi = shifted(fr, k), shifted(fi, k)
        fr, fi = fr + pr * gr - pi * gi, fi + pr * gi + pi * gr
        pr, pi = pr * pr - pi * pi, 2.0 * pr * pi
        k *= 2
    sr0, si0 = shifted(fr, 1), shifted(fi, 1)

    def global_step(t, carry):
        t0, nr, ni = advance(t, *carry)
        xs_ref[pl.ds(t0, SCAN_SEGS), 0:ns] = nr
        xs_ref[pl.ds(t0, SCAN_SEGS), ns:2 * ns] = ni
        return nr, ni

    lax.fori_loop(0, seg_len, global_step, (sr0, si0))

    def out_map(r, carry):
        r0 = pl.multiple_of(r * rows, rows)
        x = xs_ref[pl.ds(r0, rows), :].astype(BF16)
        y = _dot(x, cd_ref[...]) + d_ref[...] * u_ref[pl.ds(r0, rows), :].astype(F32)
        o_ref[pl.ds(r0, rows), :] = jax.nn.gelu(y).astype(o_ref.dtype)
        return carry

    lax.fori_loop(0, n_row_blk, out_map, 0)


def _s5(u_perm, bd, a_re, a_im, cd, d_skip, *, rows=256):
    batch, seq, d_ssm = u_perm.shape
    n_chunks = d_ssm // LANES
    kern = functools.partial(_s5_kernel, seq=seq, rows=rows)
    return pl.pallas_call(
        kern,
        grid=(batch, n_chunks),
        in_specs=[
            pl.BlockSpec((None, seq, LANES), lambda b, c: (b, 0, c)),
            pl.BlockSpec((None, LANES, 2 * SSM_CHUNK_STATE), lambda b, c: (c, 0, 0)),
            pl.BlockSpec((None, 1, SSM_CHUNK_STATE), lambda b, c: (c, 0, 0)),
            pl.BlockSpec((None, 1, SSM_CHUNK_STATE), lambda b, c: (c, 0, 0)),
            pl.BlockSpec((None, 2 * SSM_CHUNK_STATE, LANES), lambda b, c: (c, 0, 0)),
            pl.BlockSpec((None, 1, LANES), lambda b, c: (c, 0, 0)),
        ],
        out_specs=pl.BlockSpec((None, seq, LANES), lambda b, c: (b, 0, c)),
        out_shape=jax.ShapeDtypeStruct((batch, seq, d_ssm), BF16),
        scratch_shapes=[pltpu.VMEM((seq, 2 * SSM_CHUNK_STATE), F32)],
        compiler_params=pltpu.CompilerParams(
            dimension_semantics=("parallel", "parallel"), vmem_limit_bytes=VMEM_LIMIT),
        name="s5_scan",
    )(u_perm, bd, a_re, a_im, cd, d_skip)


def _s5_params(lam_re, lam_im, log_dt, b_re, b_im, c_re, c_im, d_skip):
    g = lam_re.shape[0]
    nc = g // SSM_CHUNK_GROUPS
    lr = jnp.minimum(lam_re.astype(F32), -1e-4)
    li = lam_im.astype(F32)
    dt = jnp.exp(log_dt.astype(F32))[:, None]
    mag = jnp.exp(lr * dt)
    lb_re, lb_im = mag * jnp.cos(li * dt), mag * jnp.sin(li * dt)
    den = lr * lr + li * li
    coef_re = ((lb_re - 1.0) * lr + lb_im * li) / den
    coef_im = (lb_im * lr - (lb_re - 1.0) * li) / den
    br, bi = b_re.astype(F32), b_im.astype(F32)
    bb_re = coef_re[..., None] * br - coef_im[..., None] * bi
    bb_im = coef_re[..., None] * bi + coef_im[..., None] * br
    eye = jnp.eye(SSM_CHUNK_GROUPS, dtype=F32)

    def pack_in(bb):
        bb = bb.reshape(nc, SSM_CHUNK_GROUPS, SSM_STATE, SSM_GROUP)
        return jnp.einsum('cgph,gk->cghkp', bb, eye).reshape(nc, LANES, SSM_CHUNK_STATE)

    def pack_out(cc):
        cc = cc.astype(F32).reshape(nc, SSM_CHUNK_GROUPS, SSM_GROUP, SSM_STATE)
        return jnp.einsum('cghp,gk->ckpgh', cc, eye).reshape(nc, SSM_CHUNK_STATE, LANES)

    bd = jnp.concatenate([pack_in(bb_re), pack_in(bb_im)], axis=-1).astype(BF16)
    cd = jnp.concatenate([pack_out(c_re), -pack_out(c_im)], axis=1).astype(BF16)
    a_re = lb_re.reshape(nc, 1, SSM_CHUNK_STATE)
    a_im = lb_im.reshape(nc, 1, SSM_CHUNK_STATE)
    dd = d_skip.astype(F32).reshape(nc, 1, LANES)
    return bd, a_re, a_im, cd, dd


def _mix_out_kernel(a_ref, y_ref, x_ref, gw_ref, gb_ref, nw_ref, wo_ref, o_ref):
    d_attn = a_ref.shape[1]
    y = y_ref[...]
    gate = _dot(y, gw_ref[...]) + gb_ref[...]
    s = y.astype(F32) * jax.nn.sigmoid(gate)
    sn = (_rms(s) * nw_ref[...]).astype(BF16)
    acc = _dot(a_ref[...], wo_ref[0:d_attn, :]) + _dot(sn, wo_ref[d_attn:, :])
    o_ref[...] = x_ref[...] + acc


def _mix_out(a, y, x, glu_w, glu_b, norm_w, w_out, *, tm):
    t, d = x.shape
    d_attn, d_ssm = a.shape[1], y.shape[1]
    const = lambda i: (0, 0)
    return pl.pallas_call(
        _mix_out_kernel,
        grid=(t // tm,),
        in_specs=[
            pl.BlockSpec((tm, d_attn), lambda i: (i, 0)),
            pl.BlockSpec((tm, d_ssm), lambda i: (i, 0)),
            pl.BlockSpec((tm, d), lambda i: (i, 0)),
            _resident((d_ssm, d_ssm), const),
            _resident((1, d_ssm), const),
            _resident((1, d_ssm), const),
            _resident((d, d), const),
        ],
        out_specs=pl.BlockSpec((tm, d), lambda i: (i, 0)),
        out_shape=jax.ShapeDtypeStruct((t, d), F32),
        compiler_params=pltpu.CompilerParams(
            dimension_semantics=("parallel",), vmem_limit_bytes=VMEM_LIMIT),
        name="mix_out",
    )(a, y, x, glu_w, glu_b.reshape(1, d_ssm).astype(F32), norm_w.reshape(1, d_ssm).astype(F32), w_out)


def _xattn_route_kernel(q_ref, k_ref, v_ref, h_ref, xo_ref, nw_ref, rhi_ref, rlo_ref, rb_ref,
                        h2_ref, hn_ref, eid_ref, wts_ref):
    d = h_ref.shape[1]
    hd = d // X_HEADS
    h2 = h_ref[...]
    for h in range(X_HEADS):
        sl = slice(h * hd, (h + 1) * hd)
        s = _dot_nt(q_ref[:, sl], k_ref[:, sl])
        p = jnp.exp(s - jnp.max(s, axis=-1, keepdims=True))
        p = p * (1.0 / jnp.sum(p, axis=-1, keepdims=True))
        o = _dot(p.astype(BF16), v_ref[:, sl]).astype(BF16)
        h2 = h2 + _dot(o, xo_ref[sl, :])
    h2_ref[...] = h2
    hn = _rms(h2) * nw_ref[...]
    hn_ref[...] = _pack_halves(hn)

    hi = hn.astype(BF16)
    lo = (hn - hi.astype(F32)).astype(BF16)
    logits = (_dot(hi, rhi_ref[...]) + _dot(hi, rlo_ref[...]) + _dot(lo, rhi_ref[...])) + rb_ref[...]

    lane = lax.broadcasted_iota(jnp.int32, logits.shape, 1)
    big = jnp.int32(ROUTE_LANES)

    def first_lane(cond):
        return jnp.min(jnp.where(cond, lane, big), axis=-1, keepdims=True)

    c_mask = lane < MOE_GROUPS
    lc = jnp.where(c_mask, logits, NEG)
    mc = jnp.max(lc, axis=-1, keepdims=True)
    ec = jnp.exp(lc - mc)
    p_c = ec / jnp.sum(ec, axis=-1, keepdims=True)
    p_grp = jnp.max(p_c, axis=-1, keepdims=True)
    grp = first_lane(c_mask & (p_c == p_grp))
    f_lo = MOE_GROUPS + grp * EXP_PER_GROUP
    f_mask = (lane >= f_lo) & (lane < f_lo + EXP_PER_GROUP)
    lf = jnp.where(f_mask, logits, NEG)
    mf = jnp.max(lf, axis=-1, keepdims=True)
    ef = jnp.exp(lf - mf)
    pf = ef / jnp.sum(ef, axis=-1, keepdims=True)
    v1 = jnp.max(jnp.where(f_mask, pf, -1.0), axis=-1, keepdims=True)
    i1 = first_lane(f_mask & (pf == v1))
    rest = f_mask & (lane != i1)
    v2 = jnp.max(jnp.where(rest, pf, -1.0), axis=-1, keepdims=True)
    i2 = first_lane(rest & (pf == v2))
    tot = v1 + v2
    w1 = v1 / tot * p_grp
    w2 = v2 / tot * p_grp
    eid_ref[...] = jnp.where(lane == 0, i1 - MOE_GROUPS, jnp.where(lane == 1, i2 - MOE_GROUPS, 0))
    wts_ref[...] = jnp.where(lane == 0, w1, jnp.where(lane == 1, w2, 0.0))


def _xattn_route(q, kv, h1, xo_w, norm_w, r_hi, r_lo, r_b, *, batch, seq, mem_len, tm):
    t, d = h1.shape
    n = seq // tm
    const = lambda b, i: (0, 0)
    row = lambda b, i: (b * n + i, 0)
    return pl.pallas_call(
        _xattn_route_kernel,
        grid=(batch, n),
        in_specs=[
            pl.BlockSpec((tm, d), row),
            pl.BlockSpec((mem_len, d), lambda b, i: (b, 0)),
            pl.BlockSpec((mem_len, d), lambda b, i: (b, 1)),
            pl.BlockSpec((tm, d), row),
            _resident((d, d), const),
            _resident((1, d), const),
            _resident((d, ROUTE_LANES), const),
            _resident((d, ROUTE_LANES), const),
            _resident((1, ROUTE_LANES), const),
        ],
        out_specs=[
            pl.BlockSpec((tm, d), row),
            pl.BlockSpec((tm, d // 2), row),
            pl.BlockSpec((tm, ROUTE_LANES), row),
            pl.BlockSpec((tm, ROUTE_LANES), row),
        ],
        out_shape=[
            jax.ShapeDtypeStruct((t, d), F32),
            jax.ShapeDtypeStruct((t, d // 2), jnp.uint32),
            jax.ShapeDtypeStruct((t, ROUTE_LANES), jnp.int32),
            jax.ShapeDtypeStruct((t, ROUTE_LANES), F32),
        ],
        compiler_params=pltpu.CompilerParams(
            dimension_semantics=("parallel", "parallel"), vmem_limit_bytes=VMEM_LIMIT),
        name="xattn_route",
    )(q, kv, kv, h1, xo_w, norm_w.reshape(1, d).astype(F32), r_hi, r_lo, r_b)


def _moe_kernel(be_ref, par_ref, first_ref, nxt_ref, base_ref, nval_ref, tok_ref, nu_ref,
                hn_hbm, wg_hbm, wu_hbm, wd_hbm, o_ref,
                xbuf, wgb, wub, wdb, gsem, wsem):
    b = pl.program_id(0)
    n_used = nu_ref[0]

    def weight_copies(e, slot):
        copies = []
        for hbm, buf in ((wg_hbm, wgb), (wu_hbm, wub), (wd_hbm, wdb)):
            rows = hbm.shape[1] // WEIGHT_DMA_CHUNKS
            for c in range(WEIGHT_DMA_CHUNKS):
                sl = pl.ds(c * rows, rows)
                copies.append(pltpu.make_async_copy(hbm.at[e, sl], buf.at[slot, sl], wsem.at[slot]))
        return copies

    def groups(blk):
        return (nval_ref[blk] + ROW_GROUP - 1) // ROW_GROUP

    def start_gather(blk, slot):
        base = base_ref[blk]

        def body(g, carry):
            for r in range(ROW_GROUP):
                tok = tok_ref[base + g * ROW_GROUP + r]
                pltpu.make_async_copy(hn_hbm.at[pl.ds(tok, 1)], xbuf.at[slot, g, pl.ds(r, 1)],
                                      gsem.at[slot]).start()
            return carry
        lax.fori_loop(0, groups(blk), body, 0)

    def wait_gather(blk, slot):
        filled = xbuf.at[slot, pl.ds(0, groups(blk))]
        pltpu.make_async_copy(filled, filled, gsem.at[slot]).wait()

    @pl.when(b == 0)
    def _():
        xbuf[...] = jnp.zeros(xbuf.shape, xbuf.dtype)
        for c in weight_copies(be_ref[0], par_ref[0]):
            c.start()
        start_gather(0, 0)

    @pl.when(b < n_used)
    def _():
        slot = b % 2
        wslot = par_ref[b]
        is_first = first_ref[b] == 1

        @pl.when(is_first & (nxt_ref[b] >= 0))
        def _():
            for c in weight_copies(nxt_ref[b], 1 - wslot):
                c.start()

        @pl.when(b + 1 < n_used)
        def _():
            start_gather(b + 1, 1 - slot)

        @pl.when(is_first)
        def _():
            for c in weight_copies(0, wslot):
                c.wait()

        wait_gather(b, slot)
        half = xbuf.shape[-1]
        x_hi, x_lo = (v.astype(BF16) for v in _unpack_halves(xbuf[slot].reshape(MOE_BLOCK, half)))

        def up(w):
            return _dot(x_hi, w[wslot, 0:half, :].astype(BF16)) + _dot(x_lo, w[wslot, half:, :].astype(BF16))

        mid = (jax.nn.silu(up(wgb)) * up(wub)).astype(BF16)
        o_ref[...] = _pack_halves(_dot(mid, wdb[wslot].astype(BF16)))

    @pl.when(b >= n_used)
    def _():
        o_ref[...] = jnp.zeros(o_ref.shape, o_ref.dtype)


def _moe_experts(hn_packed, w_gate, w_up, w_down, meta, tok_sorted, n_used):
    d, d_ff = w_gate.shape[1:]
    half = hn_packed.shape[1]
    blk_exp, par, first, nxt, base, nval = meta
    n_blocks = blk_exp.shape[0]
    any_spec = pl.BlockSpec(memory_space=pl.ANY)
    return pl.pallas_call(
        _moe_kernel,
        grid_spec=pltpu.PrefetchScalarGridSpec(
            num_scalar_prefetch=8,
            grid=(n_blocks,),
            in_specs=[any_spec, any_spec, any_spec, any_spec],
            out_specs=pl.BlockSpec((MOE_BLOCK, half), lambda b, *_: (b, 0)),
            scratch_shapes=[
                pltpu.VMEM((2, MOE_BLOCK // ROW_GROUP, ROW_GROUP, half), jnp.uint32),
                pltpu.VMEM((2, d, d_ff), F32),
                pltpu.VMEM((2, d, d_ff), F32),
                pltpu.VMEM((2, d_ff, d), F32),
                pltpu.SemaphoreType.DMA((2,)),
                pltpu.SemaphoreType.DMA((2,)),
            ],
        ),
        out_shape=jax.ShapeDtypeStruct((n_blocks * MOE_BLOCK, half), jnp.uint32),
        compiler_params=pltpu.CompilerParams(
            dimension_semantics=("arbitrary",), vmem_limit_bytes=VMEM_LIMIT),
        name="moe_experts",
    )(blk_exp, par, first, nxt, base, nval, tok_sorted, n_used, hn_packed, w_gate, w_up, w_down)


def _combine_kernel(pos_ref, y_hbm, h_ref, w_ref, o_ref, ybuf, sem, *, tm):
    i = pl.program_id(0)
    n_groups = tm // ROW_GROUP

    def start_gather(tile, slot):
        base = tile * (tm * TOP_K_FINE)

        def body(g, carry):
            for r in range(ROW_GROUP):
                for k in range(TOP_K_FINE):
                    row = pos_ref[base + (g * ROW_GROUP + r) * TOP_K_FINE + k]
                    pltpu.make_async_copy(y_hbm.at[pl.ds(row, 1)], ybuf.at[slot, k, g, pl.ds(r, 1)],
                                          sem.at[slot]).start()
            return carry
        lax.fori_loop(0, n_groups, body, 0)

    @pl.when(i == 0)
    def _():
        start_gather(0, 0)

    @pl.when(i + 1 < pl.num_programs(0))
    def _():
        start_gather(i + 1, (i + 1) % 2)

    slot = i % 2
    pltpu.make_async_copy(ybuf.at[slot], ybuf.at[slot], sem.at[slot]).wait()
    w = w_ref[...]
    half = ybuf.shape[-1]
    y0 = _unpack_halves(ybuf[slot, 0].reshape(tm, half))
    y1 = _unpack_halves(ybuf[slot, 1].reshape(tm, half))
    for c in range(2):
        cols = slice(c * half, (c + 1) * half)
        o_ref[:, cols] = h_ref[:, cols] + (w[:, 0:1] * y0[c] + w[:, 1:2] * y1[c])


def _combine(y, h2, wts, pos, *, tm):
    t, d = h2.shape
    kern = functools.partial(_combine_kernel, tm=tm)
    return pl.pallas_call(
        kern,
        grid_spec=pltpu.PrefetchScalarGridSpec(
            num_scalar_prefetch=1,
            grid=(t // tm,),
            in_specs=[
                pl.BlockSpec(memory_space=pl.ANY),
                pl.BlockSpec((tm, d), lambda i, pos: (i, 0)),
                pl.BlockSpec((tm, ROUTE_LANES), lambda i, pos: (i, 0)),
            ],
            out_specs=pl.BlockSpec((tm, d), lambda i, pos: (i, 0)),
            scratch_shapes=[
                pltpu.VMEM((2, TOP_K_FINE, tm // ROW_GROUP, ROW_GROUP, y.shape[1]), jnp.uint32),
                pltpu.SemaphoreType.DMA((2,)),
            ],
        ),
        out_shape=jax.ShapeDtypeStruct((t, d), F32),
        compiler_params=pltpu.CompilerParams(
            dimension_semantics=("arbitrary",), vmem_limit_bytes=VMEM_LIMIT),
        name="moe_combine",
    )(pos, y, h2, wts)


def _lookup(table, idx):
    sel = idx[:, None] == jnp.arange(table.shape[0], dtype=jnp.int32)[None, :]
    return jnp.sum(jnp.where(sel, table[None, :], 0), axis=1).astype(jnp.int32)


def _dispatch(eid, n_tokens):
    n_assign = n_tokens * TOP_K_FINE
    experts = jnp.arange(N_EXPERTS, dtype=jnp.int32)
    e_flat = eid.reshape(n_assign)
    a_ids = jnp.arange(n_assign, dtype=jnp.int32)
    e_s, order = lax.sort_key_val(e_flat, a_ids)
    counts = jnp.sum((e_flat[:, None] == experts[None, :]).astype(jnp.int32), axis=0)
    starts = jnp.cumsum(counts) - counts
    nb = (counts + MOE_BLOCK - 1) // MOE_BLOCK
    blk_end = jnp.cumsum(nb)
    blk_start = blk_end - nb
    n_used = blk_end[-1]
    n_blocks = (n_assign + N_EXPERTS * (MOE_BLOCK - 1)) // MOE_BLOCK
    b_ids = jnp.arange(n_blocks, dtype=jnp.int32)
    used = b_ids < n_used
    blk_exp = jnp.minimum(jnp.sum((blk_end[None, :] <= b_ids[:, None]).astype(jnp.int32), axis=1),
                          N_EXPERTS - 1)
    j = b_ids - _lookup(blk_start, blk_exp)
    base = jnp.where(used, _lookup(starts, blk_exp) + j * MOE_BLOCK, 0)
    nval = jnp.where(used, jnp.clip(_lookup(counts, blk_exp) - j * MOE_BLOCK, 0, MOE_BLOCK), 0)
    first = (used & (j == 0)).astype(jnp.int32)
    active = counts > 0
    par = _lookup(jnp.cumsum(active.astype(jnp.int32)) - 1, blk_exp) & 1
    later = lax.cummin(jnp.where(active, experts, N_EXPERTS), reverse=True)
    nxt_e = jnp.concatenate([later[1:], jnp.full((1,), N_EXPERTS, jnp.int32)])
    nxt = _lookup(jnp.where(nxt_e == N_EXPERTS, -1, nxt_e), blk_exp)
    meta = tuple(v.astype(jnp.int32) for v in (blk_exp, par, first, nxt, base, nval))
    row_sorted = a_ids + _lookup(blk_start * MOE_BLOCK - starts, e_s)
    _, pos = lax.sort_key_val(order, row_sorted)
    tok_sorted = jnp.concatenate([lax.shift_right_logical(order, 1), jnp.zeros((ROW_GROUP,), jnp.int32)])
    return meta, tok_sorted, pos, n_used.astype(jnp.int32).reshape(1)


def kernel(x, mem, norm1_w, w_in, q_norm_w, k_norm_w, lambda_q1, lambda_k1, lambda_q2, lambda_k2, subln_w, ssm_lambda_re, ssm_lambda_im, ssm_log_dt, ssm_b_re, ssm_b_im, ssm_c_re, ssm_c_im, ssm_d, ssm_glu_w, ssm_glu_b, ssm_out_norm_w, w_out, norm2_w, mem_norm_w, xq_w, xkv_w, xq_norm_w, xk_norm_w, xo_w, norm3_w, router_coarse_w, router_coarse_b, router_fine_w, router_fine_b, expert_w_gate, expert_w_up, expert_w_down):
    batch, seq, d = x.shape
    mem_len = mem.shape[1]
    t = batch * seq
    depth = norm1_w.shape[0]
    d_attn = DA_HEADS * DA_V_DIM
    d_ssm = d - d_attn
    qk_cols = DA_HEADS * 2 * DA_QK_DIM
    x_hd = d // X_HEADS
    h = x.reshape(t, d)
    mem2 = mem.reshape(batch * mem_len, d)

    for l in range(depth):
        lam_init = 0.8 - 0.6 * math.exp(-0.3 * l)
        lam = (jnp.exp(jnp.sum(lambda_q1[l].astype(F32) * lambda_k1[l].astype(F32)))
               - jnp.exp(jnp.sum(lambda_q2[l].astype(F32) * lambda_k2[l].astype(F32)))
               + lam_init).reshape(1)

        n_rep = qk_cols // DA_QK_DIM
        in_gain = jnp.concatenate([
            jnp.tile(q_norm_w[l].astype(F32) * (DA_QK_DIM ** -0.5 * LOG2E), n_rep),
            jnp.tile(k_norm_w[l].astype(F32), n_rep),
            jnp.ones((d_attn + d_ssm,), F32)])
        proj = _norm_matmul(h, norm1_w[l], w_in[l].astype(BF16), in_gain,
                            n_norm_cols=2 * qk_cols, chunk=DA_QK_DIM, tm=1024, tn=1024, name="in_proj")
        sub_gain = (subln_w[l].astype(F32) * (1.0 - lam_init)).reshape(1, DA_V_DIM)
        a = _diff_attn(proj, lam, sub_gain, batch=batch, seq=seq, tq=512)

        seg_len = seq // SCAN_SEGS
        u = proj[:, 2 * qk_cols + d_attn:]
        u_perm = u.reshape(batch, SCAN_SEGS, seg_len, d_ssm).transpose(0, 2, 1, 3).reshape(batch, seq, d_ssm)
        bd, a_re, a_im, cd, dd = _s5_params(ssm_lambda_re[l], ssm_lambda_im[l], ssm_log_dt[l],
                                            ssm_b_re[l], ssm_b_im[l], ssm_c_re[l], ssm_c_im[l], ssm_d[l])
        y_perm = _s5(u_perm, bd, a_re, a_im, cd, dd)
        y = y_perm.reshape(batch, seg_len, SCAN_SEGS, d_ssm).transpose(0, 2, 1, 3).reshape(t, d_ssm)
        h = _mix_out(a, y, h, ssm_glu_w[l].astype(BF16), ssm_glu_b[l], ssm_out_norm_w[l],
                     w_out[l].astype(BF16), tm=512)

        kv_gain = jnp.concatenate([jnp.tile(xk_norm_w[l].astype(F32), X_HEADS), jnp.ones((d,), F32)])
        kv = _norm_matmul(mem2, mem_norm_w[l], xkv_w[l].astype(BF16), kv_gain,
                          n_norm_cols=d, chunk=x_hd, tm=512, tn=512, name="kv_proj")
        q_gain = jnp.tile(xq_norm_w[l].astype(F32) * (x_hd ** -0.5), X_HEADS)
        q = _norm_matmul(h, norm2_w[l], xq_w[l].astype(BF16), q_gain,
                         n_norm_cols=d, chunk=x_hd, tm=1024, tn=1024, name="xq_proj")
        r_w = jnp.concatenate([router_coarse_w[l].astype(F32), router_fine_w[l].astype(F32)], axis=1)
        r_w = jnp.pad(r_w, ((0, 0), (0, ROUTE_LANES - r_w.shape[1])))
        r_hi = r_w.astype(BF16)
        r_lo = (r_w - r_hi.astype(F32)).astype(BF16)
        r_b = jnp.concatenate([router_coarse_b[l].astype(F32), router_fine_b[l].astype(F32)])
        r_b = jnp.pad(r_b, (0, ROUTE_LANES - r_b.shape[0])).reshape(1, ROUTE_LANES)
        h2, hn3, eid, wts = _xattn_route(q, kv, h, xo_w[l].astype(BF16), norm3_w[l], r_hi, r_lo, r_b,
                                         batch=batch, seq=seq, mem_len=mem_len, tm=512)

        meta, tok_sorted, pos, n_used = _dispatch(eid[:, :TOP_K_FINE], t)
        y = _moe_experts(hn3, expert_w_gate[l], expert_w_up[l], expert_w_down[l], meta, tok_sorted, n_used)
        h = _combine(y, h2, wts, pos, tm=256)

    return h.reshape(batch, seq, d)
```

```python
import functools
import math

import jax
import jax.numpy as jnp
from jax import lax
from jax.experimental import pallas as pl
from jax.experimental.pallas import tpu as pltpu

F32 = jnp.float32
BF16 = jnp.bfloat16

EPS = 1e-6
DA_HEADS = 4
DA_QK_DIM = 128
DA_V_DIM = 256
SSM_GROUP = 16
SSM_STATE = 64
X_HEADS = 4
MOE_GROUPS = 8
EXP_PER_GROUP = 8
N_EXPERTS = MOE_GROUPS * EXP_PER_GROUP
TOP_K_FINE = 2

LANES = 128
SUBLANES = 8
MXU_TILE = 256
VMEM_LIMIT = 56 * 1024 * 1024
NEG = -1e30
LOG2E = math.log2(math.e)

SSM_CHUNK_GROUPS = LANES // SSM_GROUP
SSM_CHUNK_STATE = SSM_CHUNK_GROUPS * SSM_STATE
SCAN_SEGS = SUBLANES
MOE_BLOCK = 256
ROW_GROUP = SUBLANES
WEIGHT_DMA_CHUNKS = 4
ROUTE_LANES = LANES


def _rms(x, eps=EPS):
    return x * lax.rsqrt(jnp.mean(x * x, axis=-1, keepdims=True) + eps)


def _dot(a, b):
    return jnp.dot(a, b, preferred_element_type=F32)


def _dot_nt(a, b):
    return lax.dot_general(a, b, (((1,), (1,)), ((), ())), preferred_element_type=F32)


def _pack_halves(x):
    n = x.shape[1] // 2
    hi = lax.bitcast_convert_type(x[:, :n].astype(BF16).astype(F32), jnp.uint32)
    lo = lax.bitcast_convert_type(x[:, n:].astype(BF16).astype(F32), jnp.uint32)
    return hi | lax.shift_right_logical(lo, jnp.uint32(16))


def _unpack_halves(p):
    hi = lax.bitcast_convert_type(p & jnp.uint32(0xFFFF0000), F32)
    lo = lax.bitcast_convert_type(lax.shift_left(p, jnp.uint32(16)), F32)
    return hi, lo


def _resident(shape, index_map):
    return pl.BlockSpec(shape, index_map, pipeline_mode=pl.Buffered(1))


def _norm_matmul_kernel(x_ref, nw_ref, w_ref, g_ref, o_ref, xn_ref, *, n_norm_tiles, chunk):
    j = pl.program_id(1)

    @pl.when(j == 0)
    def _():
        x = x_ref[...].astype(F32)
        xn_ref[...] = (_rms(x) * nw_ref[...]).astype(BF16)

    normed = j < n_norm_tiles
    tn = w_ref.shape[1]
    sub = max(chunk, MXU_TILE)
    for s in range(tn // sub):
        acc = _dot(xn_ref[...], w_ref[:, s * sub:(s + 1) * sub])
        for c in range(sub // chunk):
            lo = s * sub + c * chunk
            a = acc[:, c * chunk:(c + 1) * chunk]
            inv = lax.rsqrt(jnp.mean(a * a, axis=-1, keepdims=True) + EPS)
            scale = jnp.where(normed, inv, 1.0)
            o_ref[:, lo:lo + chunk] = (a * scale * g_ref[:, lo:lo + chunk]).astype(o_ref.dtype)


def _norm_matmul(x, norm_w, w_bf16, gain, *, n_norm_cols, chunk, tm, tn, name):
    m, k = x.shape
    n = w_bf16.shape[1]
    assert m % tm == 0 and n % tn == 0 and tn % max(chunk, MXU_TILE) == 0 and n_norm_cols % tn == 0
    kern = functools.partial(_norm_matmul_kernel, n_norm_tiles=n_norm_cols // tn, chunk=chunk)
    return pl.pallas_call(
        kern,
        grid=(m // tm, n // tn),
        in_specs=[
            pl.BlockSpec((tm, k), lambda i, j: (i, 0)),
            pl.BlockSpec((1, k), lambda i, j: (0, 0)),
            pl.BlockSpec((k, tn), lambda i, j: (0, j)),
            pl.BlockSpec((1, tn), lambda i, j: (0, j)),
        ],
        out_specs=pl.BlockSpec((tm, tn), lambda i, j: (i, j)),
        out_shape=jax.ShapeDtypeStruct((m, n), BF16),
        scratch_shapes=[pltpu.VMEM((tm, k), BF16)],
        compiler_params=pltpu.CompilerParams(
            dimension_semantics=("parallel", "arbitrary"), vmem_limit_bytes=VMEM_LIMIT),
        name=name,
    )(x, norm_w.reshape(1, k).astype(F32), w_bf16, gain.reshape(1, n).astype(F32))


def _diff_attn_kernel(lam_ref, q_ref, k_ref, v_ref, g_ref, o_ref,
                      m1_ref, l1_ref, acc1_ref, m2_ref, l2_ref, acc2_ref, *, tq):
    qi = pl.program_id(2)
    stats = ((m1_ref, l1_ref, acc1_ref), (m2_ref, l2_ref, acc2_ref))
    for m_ref, l_ref, acc_ref in stats:
        m_ref[...] = jnp.full(m_ref.shape, NEG, F32)
        l_ref[...] = jnp.zeros(l_ref.shape, F32)
        acc_ref[...] = jnp.zeros(acc_ref.shape, F32)

    def scores(j):
        start = pl.multiple_of(j * tq, tq)
        return tuple(_dot_nt(q_ref[:, c * DA_QK_DIM:(c + 1) * DA_QK_DIM],
                             k_ref[pl.ds(start, tq), c * DA_QK_DIM:(c + 1) * DA_QK_DIM])
                     for c in range(2))

    def accumulate(j, s_pair, masked):
        start = pl.multiple_of(j * tq, tq)
        for s, (m_ref, l_ref, acc_ref) in zip(s_pair, stats):
            if masked:
                row = lax.broadcasted_iota(jnp.int32, s.shape, 0)
                col = lax.broadcasted_iota(jnp.int32, s.shape, 1)
                s = jnp.where(col <= row, s, NEG)
            m_old = m_ref[...]
            m_new = jnp.maximum(m_old, jnp.max(s, axis=-1, keepdims=True))
            p = jnp.exp2(s - m_new)
            alpha = jnp.exp2(m_old - m_new)
            l_ref[...] = alpha * l_ref[...] + jnp.sum(p, axis=-1, keepdims=True)
            acc_ref[...] = alpha * acc_ref[...] + _dot(p.astype(BF16), v_ref[pl.ds(start, tq), :])
            m_ref[...] = m_new

    def off_diag(j, s_pair):
        s_next = scores(j + 1)
        accumulate(j, s_pair, False)
        return s_next

    s_diag = lax.fori_loop(0, qi, off_diag, scores(0))
    accumulate(qi, s_diag, True)

    lam = lam_ref[0]
    o = acc1_ref[...] / l1_ref[...] - lam * (acc2_ref[...] / l2_ref[...])
    o_ref[...] = (_rms(o) * g_ref[...]).astype(o_ref.dtype)


def _diff_attn(proj, lam, gain, *, batch, seq, tq):
    t = batch * seq
    nq = seq // tq
    width = 2 * DA_QK_DIM
    k_blk0 = DA_HEADS
    v_blk0 = 2 * DA_HEADS
    kern = functools.partial(_diff_attn_kernel, tq=tq)
    return pl.pallas_call(
        kern,
        grid_spec=pltpu.PrefetchScalarGridSpec(
            num_scalar_prefetch=1,
            grid=(batch, DA_HEADS, nq),
            in_specs=[
                pl.BlockSpec((tq, width), lambda b, h, i, lam: (b * nq + i, h)),
                pl.BlockSpec((seq, width), lambda b, h, i, lam: (b, k_blk0 + h)),
                pl.BlockSpec((seq, width), lambda b, h, i, lam: (b, v_blk0 + h)),
                pl.BlockSpec((1, DA_V_DIM), lambda b, h, i, lam: (0, 0)),
            ],
            out_specs=pl.BlockSpec((tq, DA_V_DIM), lambda b, h, i, lam: (b * nq + i, h)),
            scratch_shapes=[
                pltpu.VMEM((tq, 1), F32), pltpu.VMEM((tq, 1), F32), pltpu.VMEM((tq, DA_V_DIM), F32),
                pltpu.VMEM((tq, 1), F32), pltpu.VMEM((tq, 1), F32), pltpu.VMEM((tq, DA_V_DIM), F32),
            ],
        ),
        out_shape=jax.ShapeDtypeStruct((t, DA_HEADS * DA_V_DIM), BF16),
        compiler_params=pltpu.CompilerParams(
            dimension_semantics=("parallel", "parallel", "arbitrary"), vmem_limit_bytes=VMEM_LIMIT),
        name="diff_attn",
    )(lam, proj, proj, proj, gain)


def _s5_kernel(u_ref, bd_ref, ar_ref, ai_ref, cd_ref, d_ref, o_ref, xs_ref, *, seq, rows):
    ns = SSM_CHUNK_STATE
    seg_len = seq // SCAN_SEGS
    n_row_blk = seq // rows
    steps = rows // SCAN_SEGS

    def in_map(r):
        rs = slice(r * rows, (r + 1) * rows)
        xs_ref[rs, :] = _dot(u_ref[rs, :], bd_ref[...])

    def out_map(r):
        rs = slice(r * rows, (r + 1) * rows)
        y = _dot(xs_ref[rs, :].astype(BF16), cd_ref[...]) + d_ref[...] * u_ref[rs, :].astype(F32)
        o_ref[rs, :] = jax.nn.gelu(y).astype(o_ref.dtype)

    ar = jnp.broadcast_to(ar_ref[...], (SCAN_SEGS, ns))
    ai = jnp.broadcast_to(ai_ref[...], (SCAN_SEGS, ns))

    def advance(t, sr, si):
        ts = slice(t * SCAN_SEGS, (t + 1) * SCAN_SEGS)
        return ar * sr - ai * si + xs_ref[ts, 0:ns], ar * si + ai * sr + xs_ref[ts, ns:2 * ns]

    in_map(0)
    fr = fi = jnp.zeros((SCAN_SEGS, ns), F32)
    for r in range(n_row_blk):
        if r + 1 < n_row_blk:
            in_map(r + 1)
        for t in range(r * steps, (r + 1) * steps):
            fr, fi = advance(t, fr, fi)

    pr, pi = ar, ai
    for _ in range(int(math.log2(seg_len))):
        pr, pi = pr * pr - pi * pi, 2.0 * pr * pi
    seg = lax.broadcasted_iota(jnp.int32, (SCAN_SEGS, ns), 0)

    def shifted(x, k):
        return jnp.where(seg >= k, pltpu.roll(x, k, 0), 0.0)

    k = 1
    while k < SCAN_SEGS:
        gr, gi = shifted(fr, k), shifted(fi, k)
        fr, fi = fr + pr * gr - pi * gi, fi + pr * gi + pi * gr
        pr, pi = pr * pr - pi * pi, 2.0 * pr * pi
        k *= 2
    sr, si = shifted(fr, 1), shifted(fi, 1)

    for r in range(n_row_blk):
        for t in range(r * steps, (r + 1) * steps):
            sr, si = advance(t, sr, si)
            ts = slice(t * SCAN_SEGS, (t + 1) * SCAN_SEGS)
            xs_ref[ts, 0:ns] = sr
            xs_ref[ts, ns:2 * ns] = si
        if r >= 1:
            out_map(r - 1)
    out_map(n_row_blk - 1)


def _s5(u_perm, bd, a_re, a_im, cd, d_skip, *, rows=256):
    batch, seq, d_ssm = u_perm.shape
    n_chunks = d_ssm // LANES
    kern = functools.partial(_s5_kernel, seq=seq, rows=rows)
    return pl.pallas_call(
        kern,
        grid=(batch, n_chunks),
        in_specs=[
            pl.BlockSpec((None, seq, LANES), lambda b, c: (b, 0, c)),
            pl.BlockSpec((None, LANES, 2 * SSM_CHUNK_STATE), lambda b, c: (c, 0, 0)),
            pl.BlockSpec((None, 1, SSM_CHUNK_STATE), lambda b, c: (c, 0, 0)),
            pl.BlockSpec((None, 1, SSM_CHUNK_STATE), lambda b, c: (c, 0, 0)),
            pl.BlockSpec((None, 2 * SSM_CHUNK_STATE, LANES), lambda b, c: (c, 0, 0)),
            pl.BlockSpec((None, 1, LANES), lambda b, c: (c, 0, 0)),
        ],
        out_specs=pl.BlockSpec((None, seq, LANES), lambda b, c: (b, 0, c)),
        out_shape=jax.ShapeDtypeStruct((batch, seq, d_ssm), BF16),
        scratch_shapes=[pltpu.VMEM((seq, 2 * SSM_CHUNK_STATE), F32)],
        compiler_params=pltpu.CompilerParams(
            dimension_semantics=("parallel", "parallel"), vmem_limit_bytes=VMEM_LIMIT),
        name="s5_scan",
    )(u_perm, bd, a_re, a_im, cd, d_skip)


def _s5_params(lam_re, lam_im, log_dt, b_re, b_im, c_re, c_im, d_skip):
    g = lam_re.shape[0]
    nc = g // SSM_CHUNK_GROUPS
    lr = jnp.minimum(lam_re.astype(F32), -1e-4)
    li = lam_im.astype(F32)
    dt = jnp.exp(log_dt.astype(F32))[:, None]
    mag = jnp.exp(lr * dt)
    lb_re, lb_im = mag * jnp.cos(li * dt), mag * jnp.sin(li * dt)
    den = lr * lr + li * li
    coef_re = ((lb_re - 1.0) * lr + lb_im * li) / den
    coef_im = (lb_im * lr - (lb_re - 1.0) * li) / den
    br, bi = b_re.astype(F32), b_im.astype(F32)
    bb_re = coef_re[..., None] * br - coef_im[..., None] * bi
    bb_im = coef_re[..., None] * bi + coef_im[..., None] * br
    eye = jnp.eye(SSM_CHUNK_GROUPS, dtype=F32)

    def pack_in(bb):
        bb = bb.reshape(nc, SSM_CHUNK_GROUPS, SSM_STATE, SSM_GROUP)
        return jnp.einsum('cgph,gk->cghkp', bb, eye).reshape(nc, LANES, SSM_CHUNK_STATE)

    def pack_out(cc):
        cc = cc.astype(F32).reshape(nc, SSM_CHUNK_GROUPS, SSM_GROUP, SSM_STATE)
        return jnp.einsum('cghp,gk->ckpgh', cc, eye).reshape(nc, SSM_CHUNK_STATE, LANES)

    bd = jnp.concatenate([pack_in(bb_re), pack_in(bb_im)], axis=-1).astype(BF16)
    cd = jnp.concatenate([pack_out(c_re), -pack_out(c_im)], axis=1).astype(BF16)
    a_re = lb_re.reshape(nc, 1, SSM_CHUNK_STATE)
    a_im = lb_im.reshape(nc, 1, SSM_CHUNK_STATE)
    dd = d_skip.astype(F32).reshape(nc, 1, LANES)
    return bd, a_re, a_im, cd, dd


def _mix_out_kernel(a_ref, y_ref, x_ref, gw_ref, gb_ref, nw_ref, wo_ref, o_ref):
    d_attn = a_ref.shape[1]
    y = y_ref[...]
    gate = _dot(y, gw_ref[...]) + gb_ref[...]
    s = y.astype(F32) * jax.nn.sigmoid(gate)
    sn = (_rms(s) * nw_ref[...]).astype(BF16)
    acc = _dot(a_ref[...], wo_ref[0:d_attn, :]) + _dot(sn, wo_ref[d_attn:, :])
    o_ref[...] = x_ref[...] + acc


def _mix_out(a, y, x, glu_w, glu_b, norm_w, w_out, *, tm):
    t, d = x.shape
    d_attn, d_ssm = a.shape[1], y.shape[1]
    const = lambda i: (0, 0)
    return pl.pallas_call(
        _mix_out_kernel,
        grid=(t // tm,),
        in_specs=[
            pl.BlockSpec((tm, d_attn), lambda i: (i, 0)),
            pl.BlockSpec((tm, d_ssm), lambda i: (i, 0)),
            pl.BlockSpec((tm, d), lambda i: (i, 0)),
            _resident((d_ssm, d_ssm), const),
            _resident((1, d_ssm), const),
            _resident((1, d_ssm), const),
            _resident((d, d), const),
        ],
        out_specs=pl.BlockSpec((tm, d), lambda i: (i, 0)),
        out_shape=jax.ShapeDtypeStruct((t, d), F32),
        compiler_params=pltpu.CompilerParams(
            dimension_semantics=("parallel",), vmem_limit_bytes=VMEM_LIMIT),
        name="mix_out",
    )(a, y, x, glu_w, glu_b.reshape(1, d_ssm).astype(F32), norm_w.reshape(1, d_ssm).astype(F32), w_out)


def _xattn_route_kernel(q_ref, k_ref, v_ref, h_ref, xo_ref, nw_ref, rhi_ref, rlo_ref, rb_ref,
                        h2_ref, hn_ref, eid_ref, wts_ref):
    d = h_ref.shape[1]
    hd = d // X_HEADS
    h2 = h_ref[...]
    for h in range(X_HEADS):
        sl = slice(h * hd, (h + 1) * hd)
        s = _dot_nt(q_ref[:, sl], k_ref[:, sl])
        p = jnp.exp(s - jnp.max(s, axis=-1, keepdims=True))
        p = p * (1.0 / jnp.sum(p, axis=-1, keepdims=True))
        o = _dot(p.astype(BF16), v_ref[:, sl]).astype(BF16)
        h2 = h2 + _dot(o, xo_ref[sl, :])
    h2_ref[...] = h2
    hn = _rms(h2) * nw_ref[...]
    hn_ref[...] = _pack_halves(hn)

    hi = hn.astype(BF16)
    lo = (hn - hi.astype(F32)).astype(BF16)
    logits = (_dot(hi, rhi_ref[...]) + _dot(hi, rlo_ref[...]) + _dot(lo, rhi_ref[...])) + rb_ref[...]

    lane = lax.broadcasted_iota(jnp.int32, logits.shape, 1)
    big = jnp.int32(ROUTE_LANES)

    def first_lane(cond):
        return jnp.min(jnp.where(cond, lane, big), axis=-1, keepdims=True)

    c_mask = lane < MOE_GROUPS
    lc = jnp.where(c_mask, logits, NEG)
    mc = jnp.max(lc, axis=-1, keepdims=True)
    ec = jnp.exp(lc - mc)
    p_c = ec / jnp.sum(ec, axis=-1, keepdims=True)
    p_grp = jnp.max(p_c, axis=-1, keepdims=True)
    grp = first_lane(c_mask & (p_c == p_grp))
    f_lo = MOE_GROUPS + grp * EXP_PER_GROUP
    f_mask = (lane >= f_lo) & (lane < f_lo + EXP_PER_GROUP)
    lf = jnp.where(f_mask, logits, NEG)
    mf = jnp.max(lf, axis=-1, keepdims=True)
    ef = jnp.exp(lf - mf)
    pf = ef / jnp.sum(ef, axis=-1, keepdims=True)
    v1 = jnp.max(jnp.where(f_mask, pf, -1.0), axis=-1, keepdims=True)
    i1 = first_lane(f_mask & (pf == v1))
    rest = f_mask & (lane != i1)
    v2 = jnp.max(jnp.where(rest, pf, -1.0), axis=-1, keepdims=True)
    i2 = first_lane(rest & (pf == v2))
    tot = v1 + v2
    w1 = v1 / tot * p_grp
    w2 = v2 / tot * p_grp
    eid_ref[...] = jnp.where(lane == 0, i1 - MOE_GROUPS, jnp.where(lane == 1, i2 - MOE_GROUPS, 0))
    wts_ref[...] = jnp.where(lane == 0, w1, jnp.where(lane == 1, w2, 0.0))


def _xattn_route(q, kv, h1, xo_w, norm_w, r_hi, r_lo, r_b, *, batch, seq, mem_len, tm):
    t, d = h1.shape
    n = seq // tm
    const = lambda b, i: (0, 0)
    row = lambda b, i: (b * n + i, 0)
    return pl.pallas_call(
        _xattn_route_kernel,
        grid=(batch, n),
        in_specs=[
            pl.BlockSpec((tm, d), row),
            pl.BlockSpec((mem_len, d), lambda b, i: (b, 0)),
            pl.BlockSpec((mem_len, d), lambda b, i: (b, 1)),
            pl.BlockSpec((tm, d), row),
            _resident((d, d), const),
            _resident((1, d), const),
            _resident((d, ROUTE_LANES), const),
            _resident((d, ROUTE_LANES), const),
            _resident((1, ROUTE_LANES), const),
        ],
        out_specs=[
            pl.BlockSpec((tm, d), row),
            pl.BlockSpec((tm, d // 2), row),
            pl.BlockSpec((tm, ROUTE_LANES), row),
            pl.BlockSpec((tm, ROUTE_LANES), row),
        ],
        out_shape=[
            jax.ShapeDtypeStruct((t, d), F32),
            jax.ShapeDtypeStruct((t, d // 2), jnp.uint32),
            jax.ShapeDtypeStruct((t, ROUTE_LANES), jnp.int32),
            jax.ShapeDtypeStruct((t, ROUTE_LANES), F32),
        ],
        compiler_params=pltpu.CompilerParams(
            dimension_semantics=("parallel", "parallel"), vmem_limit_bytes=VMEM_LIMIT),
        name="xattn_route",
    )(q, kv, kv, h1, xo_w, norm_w.reshape(1, d).astype(F32), r_hi, r_lo, r_b)


def _moe_kernel(be_ref, par_ref, first_ref, nxt_ref, base_ref, nval_ref, tok_ref, nu_ref,
                hn_hbm, wg_hbm, wu_hbm, wd_hbm, o_ref,
                xbuf, wgb, wub, wdb, gsem, wsem):
    b = pl.program_id(0)
    n_used = nu_ref[0]

    def weight_copies(e, slot):
        copies = []
        for hbm, buf in ((wg_hbm, wgb), (wu_hbm, wub), (wd_hbm, wdb)):
            rows = hbm.shape[1] // WEIGHT_DMA_CHUNKS
            for c in range(WEIGHT_DMA_CHUNKS):
                sl = pl.ds(c * rows, rows)
                copies.append(pltpu.make_async_copy(hbm.at[e, sl], buf.at[slot, sl], wsem.at[slot]))
        return copies

    def groups(blk):
        return (nval_ref[blk] + ROW_GROUP - 1) // ROW_GROUP

    def start_gather(blk, slot):
        base = base_ref[blk]

        def body(g, carry):
            for r in range(ROW_GROUP):
                tok = tok_ref[base + g * ROW_GROUP + r]
                pltpu.make_async_copy(hn_hbm.at[pl.ds(tok, 1)], xbuf.at[slot, g, pl.ds(r, 1)],
                                      gsem.at[slot]).start()
            return carry
        lax.fori_loop(0, groups(blk), body, 0)

    def wait_gather(blk, slot):
        filled = xbuf.at[slot, pl.ds(0, groups(blk))]
        pltpu.make_async_copy(filled, filled, gsem.at[slot]).wait()

    @pl.when(b == 0)
    def _():
        xbuf[...] = jnp.zeros(xbuf.shape, xbuf.dtype)
        for c in weight_copies(be_ref[0], par_ref[0]):
            c.start()
        start_gather(0, 0)

    @pl.when(b < n_used)
    def _():
        slot = b % 2
        wslot = par_ref[b]
        is_first = first_ref[b] == 1

        @pl.when(is_first & (nxt_ref[b] >= 0))
        def _():
            for c in weight_copies(nxt_ref[b], 1 - wslot):
                c.start()

        @pl.when(b + 1 < n_used)
        def _():
            start_gather(b + 1, 1 - slot)

        @pl.when(is_first)
        def _():
            for c in weight_copies(0, wslot):
                c.wait()

        wait_gather(b, slot)
        half = xbuf.shape[-1]
        x_hi, x_lo = (v.astype(BF16) for v in _unpack_halves(xbuf[slot].reshape(MOE_BLOCK, half)))

        def up(w):
            return _dot(x_hi, w[wslot, 0:half, :].astype(BF16)) + _dot(x_lo, w[wslot, half:, :].astype(BF16))

        mid = (jax.nn.silu(up(wgb)) * up(wub)).astype(BF16)
        o_ref[...] = _pack_halves(_dot(mid, wdb[wslot].astype(BF16)))

    @pl.when(b >= n_used)
    def _():
        o_ref[...] = jnp.zeros(o_ref.shape, o_ref.dtype)


def _moe_experts(hn_packed, w_gate, w_up, w_down, meta, tok_sorted, n_used):
    d, d_ff = w_gate.shape[1:]
    half = hn_packed.shape[1]
    blk_exp, par, first, nxt, base, nval = meta
    n_blocks = blk_exp.shape[0]
    any_spec = pl.BlockSpec(memory_space=pl.ANY)
    return pl.pallas_call(
        _moe_kernel,
        grid_spec=pltpu.PrefetchScalarGridSpec(
            num_scalar_prefetch=8,
            grid=(n_blocks,),
            in_specs=[any_spec, any_spec, any_spec, any_spec],
            out_specs=pl.BlockSpec((MOE_BLOCK, half), lambda b, *_: (b, 0)),
            scratch_shapes=[
                pltpu.VMEM((2, MOE_BLOCK // ROW_GROUP, ROW_GROUP, half), jnp.uint32),
                pltpu.VMEM((2, d, d_ff), F32),
                pltpu.VMEM((2, d, d_ff), F32),
                pltpu.VMEM((2, d_ff, d), F32),
                pltpu.SemaphoreType.DMA((2,)),
                pltpu.SemaphoreType.DMA((2,)),
            ],
        ),
        out_shape=jax.ShapeDtypeStruct((n_blocks * MOE_BLOCK, half), jnp.uint32),
        compiler_params=pltpu.CompilerParams(
            dimension_semantics=("arbitrary",), vmem_limit_bytes=VMEM_LIMIT),
        name="moe_experts",
    )(blk_exp, par, first, nxt, base, nval, tok_sorted, n_used, hn_packed, w_gate, w_up, w_down)


def _combine_kernel(pos_ref, y_hbm, h_ref, w_ref, o_ref, ybuf, sem, *, tm):
    i = pl.program_id(0)
    n_groups = tm // ROW_GROUP

    def start_gather(tile, slot):
        base = tile * (tm * TOP_K_FINE)

        def body(g, carry):
            for r in range(ROW_GROUP):
                for k in range(TOP_K_FINE):
                    row = pos_ref[base + (g * ROW_GROUP + r) * TOP_K_FINE + k]
                    pltpu.make_async_copy(y_hbm.at[pl.ds(row, 1)], ybuf.at[slot, k, g, pl.ds(r, 1)],
                                          sem.at[slot]).start()
            return carry
        lax.fori_loop(0, n_groups, body, 0)

    @pl.when(i == 0)
    def _():
        start_gather(0, 0)

    @pl.when(i + 1 < pl.num_programs(0))
    def _():
        start_gather(i + 1, (i + 1) % 2)

    slot = i % 2
    pltpu.make_async_copy(ybuf.at[slot], ybuf.at[slot], sem.at[slot]).wait()
    w = w_ref[...]
    half = ybuf.shape[-1]
    y0 = _unpack_halves(ybuf[slot, 0].reshape(tm, half))
    y1 = _unpack_halves(ybuf[slot, 1].reshape(tm, half))
    for c in range(2):
        cols = slice(c * half, (c + 1) * half)
        o_ref[:, cols] = h_ref[:, cols] + (w[:, 0:1] * y0[c] + w[:, 1:2] * y1[c])


def _combine(y, h2, wts, pos, *, tm):
    t, d = h2.shape
    kern = functools.partial(_combine_kernel, tm=tm)
    return pl.pallas_call(
        kern,
        grid_spec=pltpu.PrefetchScalarGridSpec(
            num_scalar_prefetch=1,
            grid=(t // tm,),
            in_specs=[
                pl.BlockSpec(memory_space=pl.ANY),
                pl.BlockSpec((tm, d), lambda i, pos: (i, 0)),
                pl.BlockSpec((tm, ROUTE_LANES), lambda i, pos: (i, 0)),
            ],
            out_specs=pl.BlockSpec((tm, d), lambda i, pos: (i, 0)),
            scratch_shapes=[
                pltpu.VMEM((2, TOP_K_FINE, tm // ROW_GROUP, ROW_GROUP, y.shape[1]), jnp.uint32),
                pltpu.SemaphoreType.DMA((2,)),
            ],
        ),
        out_shape=jax.ShapeDtypeStruct((t, d), F32),
        compiler_params=pltpu.CompilerParams(
            dimension_semantics=("arbitrary",), vmem_limit_bytes=VMEM_LIMIT),
        name="moe_combine",
    )(pos, y, h2, wts)


def _lookup(table, idx):
    sel = idx[:, None] == jnp.arange(table.shape[0], dtype=jnp.int32)[None, :]
    return jnp.sum(jnp.where(sel, table[None, :], 0), axis=1).astype(jnp.int32)


def _dispatch(eid, n_tokens):
    n_assign = n_tokens * TOP_K_FINE
    experts = jnp.arange(N_EXPERTS, dtype=jnp.int32)
    e_flat = eid.reshape(n_assign)
    a_ids = jnp.arange(n_assign, dtype=jnp.int32)
    e_s, order = lax.sort_key_val(e_flat, a_ids)
    counts = jnp.sum((e_flat[:, None] == experts[None, :]).astype(jnp.int32), axis=0)
    starts = jnp.cumsum(counts) - counts
    nb = (counts + MOE_BLOCK - 1) // MOE_BLOCK
    blk_end = jnp.cumsum(nb)
    blk_start = blk_end - nb
    n_used = blk_end[-1]
    n_blocks = (n_assign + N_EXPERTS * (MOE_BLOCK - 1)) // MOE_BLOCK
    b_ids = jnp.arange(n_blocks, dtype=jnp.int32)
    used = b_ids < n_used
    blk_exp = jnp.minimum(jnp.sum((blk_end[None, :] <= b_ids[:, None]).astype(jnp.int32), axis=1),
                          N_EXPERTS - 1)
    j = b_ids - _lookup(blk_start, blk_exp)
    base = jnp.where(used, _lookup(starts, blk_exp) + j * MOE_BLOCK, 0)
    nval = jnp.where(used, jnp.clip(_lookup(counts, blk_exp) - j * MOE_BLOCK, 0, MOE_BLOCK), 0)
    first = (used & (j == 0)).astype(jnp.int32)
    active = counts > 0
    par = _lookup(jnp.cumsum(active.astype(jnp.int32)) - 1, blk_exp) & 1
    later = lax.cummin(jnp.where(active, experts, N_EXPERTS), reverse=True)
    nxt_e = jnp.concatenate([later[1:], jnp.full((1,), N_EXPERTS, jnp.int32)])
    nxt = _lookup(jnp.where(nxt_e == N_EXPERTS, -1, nxt_e), blk_exp)
    meta = tuple(v.astype(jnp.int32) for v in (blk_exp, par, first, nxt, base, nval))
    row_sorted = a_ids + _lookup(blk_start * MOE_BLOCK - starts, e_s)
    _, pos = lax.sort_key_val(order, row_sorted)
    tok_sorted = jnp.concatenate([lax.shift_right_logical(order, 1), jnp.zeros((ROW_GROUP,), jnp.int32)])
    return meta, tok_sorted, pos, n_used.astype(jnp.int32).reshape(1)


def kernel(x, mem, norm1_w, w_in, q_norm_w, k_norm_w, lambda_q1, lambda_k1, lambda_q2, lambda_k2, subln_w, ssm_lambda_re, ssm_lambda_im, ssm_log_dt, ssm_b_re, ssm_b_im, ssm_c_re, ssm_c_im, ssm_d, ssm_glu_w, ssm_glu_b, ssm_out_norm_w, w_out, norm2_w, mem_norm_w, xq_w, xkv_w, xq_norm_w, xk_norm_w, xo_w, norm3_w, router_coarse_w, router_coarse_b, router_fine_w, router_fine_b, expert_w_gate, expert_w_up, expert_w_down):
    batch, seq, d = x.shape
    mem_len = mem.shape[1]
    t = batch * seq
    depth = norm1_w.shape[0]
    d_attn = DA_HEADS * DA_V_DIM
    d_ssm = d - d_attn
    qk_cols = DA_HEADS * 2 * DA_QK_DIM
    x_hd = d // X_HEADS
    h = x.reshape(t, d)
    mem2 = mem.reshape(batch * mem_len, d)

    for l in range(depth):
        lam_init = 0.8 - 0.6 * math.exp(-0.3 * l)
        lam = (jnp.exp(jnp.sum(lambda_q1[l].astype(F32) * lambda_k1[l].astype(F32)))
               - jnp.exp(jnp.sum(lambda_q2[l].astype(F32) * lambda_k2[l].astype(F32)))
               + lam_init).reshape(1)

        n_rep = qk_cols // DA_QK_DIM
        in_gain = jnp.concatenate([
            jnp.tile(q_norm_w[l].astype(F32) * (DA_QK_DIM ** -0.5 * LOG2E), n_rep),
            jnp.tile(k_norm_w[l].astype(F32), n_rep),
            jnp.ones((d_attn + d_ssm,), F32)])
        proj = _norm_matmul(h, norm1_w[l], w_in[l].astype(BF16), in_gain,
                            n_norm_cols=2 * qk_cols, chunk=DA_QK_DIM, tm=1024, tn=1024, name="in_proj")
        sub_gain = (subln_w[l].astype(F32) * (1.0 - lam_init)).reshape(1, DA_V_DIM)
        a = _diff_attn(proj, lam, sub_gain, batch=batch, seq=seq, tq=512)

        seg_len = seq // SCAN_SEGS
        u = proj[:, 2 * qk_cols + d_attn:]
        u_perm = u.reshape(batch, SCAN_SEGS, seg_len, d_ssm).transpose(0, 2, 1, 3).reshape(batch, seq, d_ssm)
        bd, a_re, a_im, cd, dd = _s5_params(ssm_lambda_re[l], ssm_lambda_im[l], ssm_log_dt[l],
                                            ssm_b_re[l], ssm_b_im[l], ssm_c_re[l], ssm_c_im[l], ssm_d[l])
        y_perm = _s5(u_perm, bd, a_re, a_im, cd, dd)
        y = y_perm.reshape(batch, seg_len, SCAN_SEGS, d_ssm).transpose(0, 2, 1, 3).reshape(t, d_ssm)
        h = _mix_out(a, y, h, ssm_glu_w[l].astype(BF16), ssm_glu_b[l], ssm_out_norm_w[l],
                     w_out[l].astype(BF16), tm=512)

        kv_gain = jnp.concatenate([jnp.tile(xk_norm_w[l].astype(F32), X_HEADS), jnp.ones((d,), F32)])
        kv = _norm_matmul(mem2, mem_norm_w[l], xkv_w[l].astype(BF16), kv_gain,
                          n_norm_cols=d, chunk=x_hd, tm=512, tn=512, name="kv_proj")
        q_gain = jnp.tile(xq_norm_w[l].astype(F32) * (x_hd ** -0.5), X_HEADS)
        q = _norm_matmul(h, norm2_w[l], xq_w[l].astype(BF16), q_gain,
                         n_norm_cols=d, chunk=x_hd, tm=1024, tn=1024, name="xq_proj")
        r_w = jnp.concatenate([router_coarse_w[l].astype(F32), router_fine_w[l].astype(F32)], axis=1)
        r_w = jnp.pad(r_w, ((0, 0), (0, ROUTE_LANES - r_w.shape[1])))
        r_hi = r_w.astype(BF16)
        r_lo = (r_w - r_hi.astype(F32)).astype(BF16)
        r_b = jnp.concatenate([router_coarse_b[l].astype(F32), router_fine_b[l].astype(F32)])
        r_b = jnp.pad(r_b, (0, ROUTE_LANES - r_b.shape[0])).reshape(1, ROUTE_LANES)
        h2, hn3, eid, wts = _xattn_route(q, kv, h, xo_w[l].astype(BF16), norm3_w[l], r_hi, r_lo, r_b,
                                         batch=batch, seq=seq, mem_len=mem_len, tm=512)

        meta, tok_sorted, pos, n_used = _dispatch(eid[:, :TOP_K_FINE], t)
        y = _moe_experts(hn3, expert_w_gate[l], expert_w_up[l], expert_w_down[l], meta, tok_sorted, n_used)
        h = _combine(y, h2, wts, pos, tm=256)

    return h.reshape(batch, seq, d)
```

```python
import functools
import math

import jax
import jax.numpy as jnp
from jax import lax
from jax.experimental import pallas as pl
from jax.experimental.pallas import tpu as pltpu

F32 = jnp.float32
BF16 = jnp.bfloat16

EPS = 1e-6
DA_HEADS = 4
DA_QK_DIM = 128
DA_V_DIM = 256
SSM_GROUP = 16
SSM_STATE = 64
X_HEADS = 4
MOE_GROUPS = 8
EXP_PER_GROUP = 8
N_EXPERTS = MOE_GROUPS * EXP_PER_GROUP
TOP_K_FINE = 2

LANES = 128
SUBLANES = 8
MXU_TILE = 256
VMEM_LIMIT = 56 * 1024 * 1024
NEG = -1e30
LOG2E = math.log2(math.e)

SSM_CHUNK_GROUPS = LANES // SSM_GROUP
SSM_CHUNK_STATE = SSM_CHUNK_GROUPS * SSM_STATE
SCAN_SEGS = SUBLANES
ATTN_TILE = 512
MOE_BLOCK = 256
ROW_GROUP = SUBLANES
WEIGHT_DMA_CHUNKS = 4
ROUTE_LANES = LANES


def _rms(x, eps=EPS):
    return x * lax.rsqrt(jnp.mean(x * x, axis=-1, keepdims=True) + eps)


def _dot(a, b):
    return jnp.dot(a, b, preferred_element_type=F32)


def _dot_nt(a, b):
    return lax.dot_general(a, b, (((1,), (1,)), ((), ())), preferred_element_type=F32)


def _pack_halves(x):
    n = x.shape[1] // 2
    hi = lax.bitcast_convert_type(x[:, :n].astype(BF16).astype(F32), jnp.uint32)
    lo = lax.bitcast_convert_type(x[:, n:].astype(BF16).astype(F32), jnp.uint32)
    return hi | lax.shift_right_logical(lo, jnp.uint32(16))


def _unpack_halves(p):
    hi = lax.bitcast_convert_type(p & jnp.uint32(0xFFFF0000), F32)
    lo = lax.bitcast_convert_type(lax.shift_left(p, jnp.uint32(16)), F32)
    return hi, lo


def _resident(shape, index_map):
    return pl.BlockSpec(shape, index_map, pipeline_mode=pl.Buffered(1))


def _norm_matmul_kernel(x_ref, nw_ref, w_ref, g_ref, o_ref, xn_ref, *, n_norm_tiles, chunk):
    j = pl.program_id(1)

    @pl.when(j == 0)
    def _():
        x = x_ref[...].astype(F32)
        xn_ref[...] = (_rms(x) * nw_ref[...]).astype(BF16)

    normed = j < n_norm_tiles
    tn = w_ref.shape[1]
    sub = max(chunk, MXU_TILE)
    for s in range(tn // sub):
        acc = _dot(xn_ref[...], w_ref[:, s * sub:(s + 1) * sub])
        for c in range(sub // chunk):
            lo = s * sub + c * chunk
            a = acc[:, c * chunk:(c + 1) * chunk]
            inv = lax.rsqrt(jnp.mean(a * a, axis=-1, keepdims=True) + EPS)
            scale = jnp.where(normed, inv, 1.0)
            o_ref[:, lo:lo + chunk] = (a * scale * g_ref[:, lo:lo + chunk]).astype(o_ref.dtype)


def _norm_matmul(x, norm_w, w_bf16, gain, *, n_norm_cols, chunk, tm, tn, name):
    m, k = x.shape
    n = w_bf16.shape[1]
    assert m % tm == 0 and n % tn == 0 and tn % max(chunk, MXU_TILE) == 0 and n_norm_cols % tn == 0
    kern = functools.partial(_norm_matmul_kernel, n_norm_tiles=n_norm_cols // tn, chunk=chunk)
    return pl.pallas_call(
        kern,
        grid=(m // tm, n // tn),
        in_specs=[
            pl.BlockSpec((tm, k), lambda i, j: (i, 0)),
            pl.BlockSpec((1, k), lambda i, j: (0, 0)),
            pl.BlockSpec((k, tn), lambda i, j: (0, j)),
            pl.BlockSpec((1, tn), lambda i, j: (0, j)),
        ],
        out_specs=pl.BlockSpec((tm, tn), lambda i, j: (i, j)),
        out_shape=jax.ShapeDtypeStruct((m, n), BF16),
        scratch_shapes=[pltpu.VMEM((tm, k), BF16)],
        compiler_params=pltpu.CompilerParams(
            dimension_semantics=("parallel", "arbitrary"), vmem_limit_bytes=VMEM_LIMIT),
        name=name,
    )(x, norm_w.reshape(1, k).astype(F32), w_bf16, gain.reshape(1, n).astype(F32))


def _diff_attn_kernel(lam_ref, qa_ref, qb_ref, k_ref, v_ref, g_ref, o_ref, *stat_refs, tq, n_q):
    pair = pl.program_id(2)
    stats_a = (stat_refs[0:3], stat_refs[3:6])
    stats_b = (stat_refs[6:9], stat_refs[9:12])

    def scores(q_ref, j):
        return tuple(_dot_nt(q_ref[:, c * DA_QK_DIM:(c + 1) * DA_QK_DIM],
                             k_ref[j * tq:(j + 1) * tq, c * DA_QK_DIM:(c + 1) * DA_QK_DIM])
                     for c in range(2))

    def accumulate(stats, j, s_pair, masked):
        for s, (m_ref, l_ref, acc_ref) in zip(s_pair, stats):
            if masked:
                row = lax.broadcasted_iota(jnp.int32, s.shape, 0)
                col = lax.broadcasted_iota(jnp.int32, s.shape, 1)
                s = jnp.where(col <= row, s, NEG)
            m_old = m_ref[...]
            m_new = jnp.maximum(m_old, jnp.max(s, axis=-1, keepdims=True))
            p = jnp.exp2(s - m_new)
            alpha = jnp.exp2(m_old - m_new)
            l_ref[...] = alpha * l_ref[...] + jnp.sum(p, axis=-1, keepdims=True)
            acc_ref[...] = alpha * acc_ref[...] + _dot(p.astype(BF16), v_ref[j * tq:(j + 1) * tq, :])
            m_ref[...] = m_new

    def finish(stats, rows):
        (_, l1, acc1), (_, l2, acc2) = stats
        o = acc1[...] / l1[...] - lam_ref[0] * (acc2[...] / l2[...])
        o_ref[rows, :] = (_rms(o) * g_ref[...]).astype(o_ref.dtype)

    def run(p):
        tiles = ((qa_ref, stats_a, p), (qb_ref, stats_b, n_q - 1 - p))
        for _, stats, _ in tiles:
            for m_ref, l_ref, acc_ref in stats:
                m_ref[...] = jnp.full(m_ref.shape, NEG, F32)
                l_ref[...] = jnp.zeros(l_ref.shape, F32)
                acc_ref[...] = jnp.zeros(acc_ref.shape, F32)
        pending = [scores(q_ref, 0) for q_ref, _, _ in tiles]
        for j in range(n_q - p):
            for idx, (q_ref, stats, diag) in enumerate(tiles):
                if j > diag:
                    continue
                s_pair = pending[idx]
                if j < diag:
                    pending[idx] = scores(q_ref, j + 1)
                accumulate(stats, j, s_pair, masked=(j == diag))
        finish(stats_a, slice(0, tq))
        finish(stats_b, slice(tq, 2 * tq))

    for p in range(n_q // 2):
        pl.when(pair == p)(functools.partial(run, p))


def _attn_tile_pos(tile, nq):
    b, qt = tile // nq, tile % nq
    return b * nq + jnp.where(qt < nq // 2, 2 * qt, 2 * (nq - 1 - qt) + 1)


def _diff_attn(proj, lam, gain, *, batch, seq, tq):
    t = batch * seq
    nq = seq // tq
    assert nq % 2 == 0
    width = 2 * DA_QK_DIM
    k_blk0 = DA_HEADS
    v_blk0 = 2 * DA_HEADS
    kern = functools.partial(_diff_attn_kernel, tq=tq, n_q=nq)
    stat = [pltpu.VMEM((tq, 1), F32), pltpu.VMEM((tq, 1), F32), pltpu.VMEM((tq, DA_V_DIM), F32)]
    return pl.pallas_call(
        kern,
        grid_spec=pltpu.PrefetchScalarGridSpec(
            num_scalar_prefetch=1,
            grid=(batch, DA_HEADS, nq // 2),
            in_specs=[
                pl.BlockSpec((tq, width), lambda b, h, p, lam: (b * nq + p, h)),
                pl.BlockSpec((tq, width), lambda b, h, p, lam: (b * nq + nq - 1 - p, h)),
                pl.BlockSpec((seq, width), lambda b, h, p, lam: (b, k_blk0 + h)),
                pl.BlockSpec((seq, width), lambda b, h, p, lam: (b, v_blk0 + h)),
                pl.BlockSpec((1, DA_V_DIM), lambda b, h, p, lam: (0, 0)),
            ],
            out_specs=pl.BlockSpec((2 * tq, DA_V_DIM), lambda b, h, p, lam: (b * (nq // 2) + p, h)),
            scratch_shapes=stat * 4,
        ),
        out_shape=jax.ShapeDtypeStruct((t, DA_HEADS * DA_V_DIM), BF16),
        compiler_params=pltpu.CompilerParams(
            dimension_semantics=("parallel", "parallel", "arbitrary"), vmem_limit_bytes=VMEM_LIMIT),
        name="diff_attn",
    )(lam, proj, proj, proj, proj, gain)


def _s5_kernel(u_ref, bd_ref, ar_ref, ai_ref, cd_ref, d_ref, o_ref, xs_ref, *, seq, rows):
    ns = SSM_CHUNK_STATE
    seg_len = seq // SCAN_SEGS
    n_row_blk = seq // rows
    steps = rows // SCAN_SEGS

    def in_map(r):
        rs = slice(r * rows, (r + 1) * rows)
        xs_ref[rs, :] = _dot(u_ref[rs, :], bd_ref[...])

    def out_map(r):
        rs = slice(r * rows, (r + 1) * rows)
        y = _dot(xs_ref[rs, :].astype(BF16), cd_ref[...]) + d_ref[...] * u_ref[rs, :].astype(F32)
        o_ref[rs, :] = jax.nn.gelu(y).astype(o_ref.dtype)

    ar = jnp.broadcast_to(ar_ref[...], (SCAN_SEGS, ns))
    ai = jnp.broadcast_to(ai_ref[...], (SCAN_SEGS, ns))

    def advance(t, sr, si):
        ts = slice(t * SCAN_SEGS, (t + 1) * SCAN_SEGS)
        return ar * sr - ai * si + xs_ref[ts, 0:ns], ar * si + ai * sr + xs_ref[ts, ns:2 * ns]

    in_map(0)
    fr = fi = jnp.zeros((SCAN_SEGS, ns), F32)
    for r in range(n_row_blk):
        if r + 1 < n_row_blk:
            in_map(r + 1)
        for t in range(r * steps, (r + 1) * steps):
            fr, fi = advance(t, fr, fi)

    pr, pi = ar, ai
    for _ in range(int(math.log2(seg_len))):
        pr, pi = pr * pr - pi * pi, 2.0 * pr * pi
    seg = lax.broadcasted_iota(jnp.int32, (SCAN_SEGS, ns), 0)

    def shifted(x, k):
        return jnp.where(seg >= k, pltpu.roll(x, k, 0), 0.0)

    k = 1
    while k < SCAN_SEGS:
        gr, gi = shifted(fr, k), shifted(fi, k)
        fr, fi = fr + pr * gr - pi * gi, fi + pr * gi + pi * gr
        pr, pi = pr * pr - pi * pi, 2.0 * pr * pi
        k *= 2
    sr, si = shifted(fr, 1), shifted(fi, 1)

    for r in range(n_row_blk):
        for t in range(r * steps, (r + 1) * steps):
            sr, si = advance(t, sr, si)
            ts = slice(t * SCAN_SEGS, (t + 1) * SCAN_SEGS)
            xs_ref[ts, 0:ns] = sr
            xs_ref[ts, ns:2 * ns] = si
        if r >= 1:
            out_map(r - 1)
    out_map(n_row_blk - 1)


def _s5(u_perm, bd, a_re, a_im, cd, d_skip, *, rows=256):
    batch, seq, d_ssm = u_perm.shape
    n_chunks = d_ssm // LANES
    kern = functools.partial(_s5_kernel, seq=seq, rows=rows)
    return pl.pallas_call(
        kern,
        grid=(batch, n_chunks),
        in_specs=[
            pl.BlockSpec((None, seq, LANES), lambda b, c: (b, 0, c)),
            pl.BlockSpec((None, LANES, 2 * SSM_CHUNK_STATE), lambda b, c: (c, 0, 0)),
            pl.BlockSpec((None, 1, SSM_CHUNK_STATE), lambda b, c: (c, 0, 0)),
            pl.BlockSpec((None, 1, SSM_CHUNK_STATE), lambda b, c: (c, 0, 0)),
            pl.BlockSpec((None, 2 * SSM_CHUNK_STATE, LANES), lambda b, c: (c, 0, 0)),
            pl.BlockSpec((None, 1, LANES), lambda b, c: (c, 0, 0)),
        ],
        out_specs=pl.BlockSpec((None, seq, LANES), lambda b, c: (b, 0, c)),
        out_shape=jax.ShapeDtypeStruct((batch, seq, d_ssm), BF16),
        scratch_shapes=[pltpu.VMEM((seq, 2 * SSM_CHUNK_STATE), F32)],
        compiler_params=pltpu.CompilerParams(
            dimension_semantics=("parallel", "parallel"), vmem_limit_bytes=VMEM_LIMIT),
        name="s5_scan",
    )(u_perm, bd, a_re, a_im, cd, d_skip)


def _s5_params(lam_re, lam_im, log_dt, b_re, b_im, c_re, c_im, d_skip):
    g = lam_re.shape[0]
    nc = g // SSM_CHUNK_GROUPS
    lr = jnp.minimum(lam_re.astype(F32), -1e-4)
    li = lam_im.astype(F32)
    dt = jnp.exp(log_dt.astype(F32))[:, None]
    mag = jnp.exp(lr * dt)
    lb_re, lb_im = mag * jnp.cos(li * dt), mag * jnp.sin(li * dt)
    den = lr * lr + li * li
    coef_re = ((lb_re - 1.0) * lr + lb_im * li) / den
    coef_im = (lb_im * lr - (lb_re - 1.0) * li) / den
    br, bi = b_re.astype(F32), b_im.astype(F32)
    bb_re = coef_re[..., None] * br - coef_im[..., None] * bi
    bb_im = coef_re[..., None] * bi + coef_im[..., None] * br
    eye = jnp.eye(SSM_CHUNK_GROUPS, dtype=F32)

    def pack_in(bb):
        bb = bb.reshape(nc, SSM_CHUNK_GROUPS, SSM_STATE, SSM_GROUP)
        return jnp.einsum('cgph,gk->cghkp', bb, eye).reshape(nc, LANES, SSM_CHUNK_STATE)

    def pack_out(cc):
        cc = cc.astype(F32).reshape(nc, SSM_CHUNK_GROUPS, SSM_GROUP, SSM_STATE)
        return jnp.einsum('cghp,gk->ckpgh', cc, eye).reshape(nc, SSM_CHUNK_STATE, LANES)

    bd = jnp.concatenate([pack_in(bb_re), pack_in(bb_im)], axis=-1).astype(BF16)
    cd = jnp.concatenate([pack_out(c_re), -pack_out(c_im)], axis=1).astype(BF16)
    a_re = lb_re.reshape(nc, 1, SSM_CHUNK_STATE)
    a_im = lb_im.reshape(nc, 1, SSM_CHUNK_STATE)
    dd = d_skip.astype(F32).reshape(nc, 1, LANES)
    return bd, a_re, a_im, cd, dd


def _mix_out_kernel(a_ref, y_ref, x_ref, gw_ref, gb_ref, nw_ref, wo_ref, o_ref):
    d_attn = a_ref.shape[1]
    y = y_ref[...]
    gate = _dot(y, gw_ref[...]) + gb_ref[...]
    s = y.astype(F32) * jax.nn.sigmoid(gate)
    sn = (_rms(s) * nw_ref[...]).astype(BF16)
    acc = _dot(a_ref[...], wo_ref[0:d_attn, :]) + _dot(sn, wo_ref[d_attn:, :])
    o_ref[...] = x_ref[...] + acc


def _mix_out(a, y, x, glu_w, glu_b, norm_w, w_out, *, tm, attn_tiles):
    t, d = x.shape
    d_attn, d_ssm = a.shape[1], y.shape[1]
    const = lambda i: (0, 0)
    return pl.pallas_call(
        _mix_out_kernel,
        grid=(t // tm,),
        in_specs=[
            pl.BlockSpec((tm, d_attn), lambda i: (_attn_tile_pos(i, attn_tiles), 0)),
            pl.BlockSpec((tm, d_ssm), lambda i: (i, 0)),
            pl.BlockSpec((tm, d), lambda i: (i, 0)),
            _resident((d_ssm, d_ssm), const),
            _resident((1, d_ssm), const),
            _resident((1, d_ssm), const),
            _resident((d, d), const),
        ],
        out_specs=pl.BlockSpec((tm, d), lambda i: (i, 0)),
        out_shape=jax.ShapeDtypeStruct((t, d), F32),
        compiler_params=pltpu.CompilerParams(
            dimension_semantics=("parallel",), vmem_limit_bytes=VMEM_LIMIT),
        name="mix_out",
    )(a, y, x, glu_w, glu_b.reshape(1, d_ssm).astype(F32), norm_w.reshape(1, d_ssm).astype(F32), w_out)


def _xattn_route_kernel(q_ref, k_ref, v_ref, h_ref, xo_ref, nw_ref, rhi_ref, rlo_ref, rb_ref,
                        h2_ref, hn_ref, eid_ref, wts_ref):
    d = h_ref.shape[1]
    hd = d // X_HEADS
    h2 = h_ref[...]
    for h in range(X_HEADS):
        sl = slice(h * hd, (h + 1) * hd)
        s = _dot_nt(q_ref[:, sl], k_ref[:, sl])
        p = jnp.exp(s - jnp.max(s, axis=-1, keepdims=True))
        p = p * (1.0 / jnp.sum(p, axis=-1, keepdims=True))
        o = _dot(p.astype(BF16), v_ref[:, sl]).astype(BF16)
        h2 = h2 + _dot(o, xo_ref[sl, :])
    h2_ref[...] = h2
    hn = _rms(h2) * nw_ref[...]
    hn_ref[...] = _pack_halves(hn)

    hi = hn.astype(BF16)
    lo = (hn - hi.astype(F32)).astype(BF16)
    logits = (_dot(hi, rhi_ref[...]) + _dot(hi, rlo_ref[...]) + _dot(lo, rhi_ref[...])) + rb_ref[...]

    lane = lax.broadcasted_iota(jnp.int32, logits.shape, 1)
    big = jnp.int32(ROUTE_LANES)

    def first_lane(cond):
        return jnp.min(jnp.where(cond, lane, big), axis=-1, keepdims=True)

    c_mask = lane < MOE_GROUPS
    lc = jnp.where(c_mask, logits, NEG)
    mc = jnp.max(lc, axis=-1, keepdims=True)
    ec = jnp.exp(lc - mc)
    p_c = ec / jnp.sum(ec, axis=-1, keepdims=True)
    p_grp = jnp.max(p_c, axis=-1, keepdims=True)
    grp = first_lane(c_mask & (p_c == p_grp))
    f_lo = MOE_GROUPS + grp * EXP_PER_GROUP
    f_mask = (lane >= f_lo) & (lane < f_lo + EXP_PER_GROUP)
    lf = jnp.where(f_mask, logits, NEG)
    mf = jnp.max(lf, axis=-1, keepdims=True)
    ef = jnp.exp(lf - mf)
    pf = ef / jnp.sum(ef, axis=-1, keepdims=True)
    v1 = jnp.max(jnp.where(f_mask, pf, -1.0), axis=-1, keepdims=True)
    i1 = first_lane(f_mask & (pf == v1))
    rest = f_mask & (lane != i1)
    v2 = jnp.max(jnp.where(rest, pf, -1.0), axis=-1, keepdims=True)
    i2 = first_lane(rest & (pf == v2))
    tot = v1 + v2
    w1 = v1 / tot * p_grp
    w2 = v2 / tot * p_grp
    eid_ref[...] = jnp.where(lane == 0, i1 - MOE_GROUPS, jnp.where(lane == 1, i2 - MOE_GROUPS, 0))
    wts_ref[...] = jnp.where(lane == 0, w1, jnp.where(lane == 1, w2, 0.0))


def _xattn_route(q, kv, h1, xo_w, norm_w, r_hi, r_lo, r_b, *, batch, seq, mem_len, tm):
    t, d = h1.shape
    n = seq // tm
    const = lambda b, i: (0, 0)
    row = lambda b, i: (b * n + i, 0)
    return pl.pallas_call(
        _xattn_route_kernel,
        grid=(batch, n),
        in_specs=[
            pl.BlockSpec((tm, d), row),
            pl.BlockSpec((mem_len, d), lambda b, i: (b, 0)),
            pl.BlockSpec((mem_len, d), lambda b, i: (b, 1)),
            pl.BlockSpec((tm, d), row),
            _resident((d, d), const),
            _resident((1, d), const),
            _resident((d, ROUTE_LANES), const),
            _resident((d, ROUTE_LANES), const),
            _resident((1, ROUTE_LANES), const),
        ],
        out_specs=[
            pl.BlockSpec((tm, d), row),
            pl.BlockSpec((tm, d // 2), row),
            pl.BlockSpec((tm, ROUTE_LANES), row),
            pl.BlockSpec((tm, ROUTE_LANES), row),
        ],
        out_shape=[
            jax.ShapeDtypeStruct((t, d), F32),
            jax.ShapeDtypeStruct((t, d // 2), jnp.uint32),
            jax.ShapeDtypeStruct((t, ROUTE_LANES), jnp.int32),
            jax.ShapeDtypeStruct((t, ROUTE_LANES), F32),
        ],
        compiler_params=pltpu.CompilerParams(
            dimension_semantics=("parallel", "parallel"), vmem_limit_bytes=VMEM_LIMIT),
        name="xattn_route",
    )(q, kv, kv, h1, xo_w, norm_w.reshape(1, d).astype(F32), r_hi, r_lo, r_b)


def _moe_kernel(be_ref, par_ref, first_ref, nxt_ref, base_ref, nval_ref, tok_ref, nu_ref,
                hn_hbm, wg_hbm, wu_hbm, wd_hbm, o_ref,
                xbuf, wgb, wub, wdb, gsem, wsem):
    b = pl.program_id(0)
    n_used = nu_ref[0]

    def weight_copies(e, slot):
        copies = []
        for hbm, buf in ((wg_hbm, wgb), (wu_hbm, wub), (wd_hbm, wdb)):
            rows = hbm.shape[1] // WEIGHT_DMA_CHUNKS
            for c in range(WEIGHT_DMA_CHUNKS):
                sl = pl.ds(c * rows, rows)
                copies.append(pltpu.make_async_copy(hbm.at[e, sl], buf.at[slot, sl], wsem.at[slot]))
        return copies

    def groups(blk):
        return (nval_ref[blk] + ROW_GROUP - 1) // ROW_GROUP

    def start_gather(blk, slot):
        base = base_ref[blk]

        def body(g, carry):
            for r in range(ROW_GROUP):
                tok = tok_ref[base + g * ROW_GROUP + r]
                pltpu.make_async_copy(hn_hbm.at[pl.ds(tok, 1)], xbuf.at[slot, g, pl.ds(r, 1)],
                                      gsem.at[slot]).start()
            return carry
        lax.fori_loop(0, groups(blk), body, 0)

    def wait_gather(blk, slot):
        filled = xbuf.at[slot, pl.ds(0, groups(blk))]
        pltpu.make_async_copy(filled, filled, gsem.at[slot]).wait()

    @pl.when(b == 0)
    def _():
        xbuf[...] = jnp.zeros(xbuf.shape, xbuf.dtype)
        for c in weight_copies(be_ref[0], par_ref[0]):
            c.start()
        start_gather(0, 0)

    @pl.when(b < n_used)
    def _():
        slot = b % 2
        wslot = par_ref[b]
        is_first = first_ref[b] == 1

        @pl.when(is_first & (nxt_ref[b] >= 0))
        def _():
            for c in weight_copies(nxt_ref[b], 1 - wslot):
                c.start()

        @pl.when(b + 1 < n_used)
        def _():
            start_gather(b + 1, 1 - slot)

        @pl.when(is_first)
        def _():
            for c in weight_copies(0, wslot):
                c.wait()

        wait_gather(b, slot)
        half = xbuf.shape[-1]
        x_hi, x_lo = (v.astype(BF16) for v in _unpack_halves(xbuf[slot].reshape(MOE_BLOCK, half)))

        def up(w):
            return _dot(x_hi, w[wslot, 0:half, :].astype(BF16)) + _dot(x_lo, w[wslot, half:, :].astype(BF16))

        mid = (jax.nn.silu(up(wgb)) * up(wub)).astype(BF16)
        o_ref[...] = _pack_halves(_dot(mid, wdb[wslot].astype(BF16)))

    @pl.when(b >= n_used)
    def _():
        o_ref[...] = jnp.zeros(o_ref.shape, o_ref.dtype)


def _moe_experts(hn_packed, w_gate, w_up, w_down, meta, tok_sorted, n_used):
    d, d_ff = w_gate.shape[1:]
    half = hn_packed.shape[1]
    blk_exp, par, first, nxt, base, nval = meta
    n_blocks = blk_exp.shape[0]
    any_spec = pl.BlockSpec(memory_space=pl.ANY)
    return pl.pallas_call(
        _moe_kernel,
        grid_spec=pltpu.PrefetchScalarGridSpec(
            num_scalar_prefetch=8,
            grid=(n_blocks,),
            in_specs=[any_spec, any_spec, any_spec, any_spec],
            out_specs=pl.BlockSpec((MOE_BLOCK, half), lambda b, *_: (b, 0)),
            scratch_shapes=[
                pltpu.VMEM((2, MOE_BLOCK // ROW_GROUP, ROW_GROUP, half), jnp.uint32),
                pltpu.VMEM((2, d, d_ff), F32),
                pltpu.VMEM((2, d, d_ff), F32),
                pltpu.VMEM((2, d_ff, d), F32),
                pltpu.SemaphoreType.DMA((2,)),
                pltpu.SemaphoreType.DMA((2,)),
            ],
        ),
        out_shape=jax.ShapeDtypeStruct((n_blocks * MOE_BLOCK, half), jnp.uint32),
        compiler_params=pltpu.CompilerParams(
            dimension_semantics=("arbitrary",), vmem_limit_bytes=VMEM_LIMIT),
        name="moe_experts",
    )(blk_exp, par, first, nxt, base, nval, tok_sorted, n_used, hn_packed, w_gate, w_up, w_down)


def _combine_kernel(pos_ref, y_hbm, h_ref, w_ref, o_ref, ybuf, sem, *, tm):
    i = pl.program_id(0)
    n_groups = tm // ROW_GROUP

    def start_gather(tile, slot):
        base = tile * (tm * TOP_K_FINE)

        def body(g, carry):
            for r in range(ROW_GROUP):
                for k in range(TOP_K_FINE):
                    row = pos_ref[base + (g * ROW_GROUP + r) * TOP_K_FINE + k]
                    pltpu.make_async_copy(y_hbm.at[pl.ds(row, 1)], ybuf.at[slot, k, g, pl.ds(r, 1)],
                                          sem.at[slot]).start()
            return carry
        lax.fori_loop(0, n_groups, body, 0)

    @pl.when(i == 0)
    def _():
        start_gather(0, 0)

    @pl.when(i + 1 < pl.num_programs(0))
    def _():
        start_gather(i + 1, (i + 1) % 2)

    slot = i % 2
    pltpu.make_async_copy(ybuf.at[slot], ybuf.at[slot], sem.at[slot]).wait()
    w = w_ref[...]
    half = ybuf.shape[-1]
    y0 = _unpack_halves(ybuf[slot, 0].reshape(tm, half))
    y1 = _unpack_halves(ybuf[slot, 1].reshape(tm, half))
    for c in range(2):
        cols = slice(c * half, (c + 1) * half)
        o_ref[:, cols] = h_ref[:, cols] + (w[:, 0:1] * y0[c] + w[:, 1:2] * y1[c])


def _combine(y, h2, wts, pos, *, tm):
    t, d = h2.shape
    kern = functools.partial(_combine_kernel, tm=tm)
    return pl.pallas_call(
        kern,
        grid_spec=pltpu.PrefetchScalarGridSpec(
            num_scalar_prefetch=1,
            grid=(t // tm,),
            in_specs=[
                pl.BlockSpec(memory_space=pl.ANY),
                pl.BlockSpec((tm, d), lambda i, pos: (i, 0)),
                pl.BlockSpec((tm, ROUTE_LANES), lambda i, pos: (i, 0)),
            ],
            out_specs=pl.BlockSpec((tm, d), lambda i, pos: (i, 0)),
            scratch_shapes=[
                pltpu.VMEM((2, TOP_K_FINE, tm // ROW_GROUP, ROW_GROUP, y.shape[1]), jnp.uint32),
                pltpu.SemaphoreType.DMA((2,)),
            ],
        ),
        out_shape=jax.ShapeDtypeStruct((t, d), F32),
        compiler_params=pltpu.CompilerParams(
            dimension_semantics=("arbitrary",), vmem_limit_bytes=VMEM_LIMIT),
        name="moe_combine",
    )(pos, y, h2, wts)


def _lookup(table, idx):
    sel = idx[:, None] == jnp.arange(table.shape[0], dtype=jnp.int32)[None, :]
    return jnp.sum(jnp.where(sel, table[None, :], 0), axis=1).astype(jnp.int32)


def _dispatch(eid, n_tokens):
    n_assign = n_tokens * TOP_K_FINE
    experts = jnp.arange(N_EXPERTS, dtype=jnp.int32)
    e_flat = eid.reshape(n_assign)
    a_ids = jnp.arange(n_assign, dtype=jnp.int32)
    e_s, order = lax.sort_key_val(e_flat, a_ids)
    counts = jnp.sum((e_flat[:, None] == experts[None, :]).astype(jnp.int32), axis=0)
    starts = jnp.cumsum(counts) - counts
    nb = (counts + MOE_BLOCK - 1) // MOE_BLOCK
    blk_end = jnp.cumsum(nb)
    blk_start = blk_end - nb
    n_used = blk_end[-1]
    n_blocks = (n_assign + N_EXPERTS * (MOE_BLOCK - 1)) // MOE_BLOCK
    b_ids = jnp.arange(n_blocks, dtype=jnp.int32)
    used = b_ids < n_used
    blk_exp = jnp.minimum(jnp.sum((blk_end[None, :] <= b_ids[:, None]).astype(jnp.int32), axis=1),
                          N_EXPERTS - 1)
    j = b_ids - _lookup(blk_start, blk_exp)
    base = jnp.where(used, _lookup(starts, blk_exp) + j * MOE_BLOCK, 0)
    nval = jnp.where(used, jnp.clip(_lookup(counts, blk_exp) - j * MOE_BLOCK, 0, MOE_BLOCK), 0)
    first = (used & (j == 0)).astype(jnp.int32)
    active = counts > 0
    par = _lookup(jnp.cumsum(active.astype(jnp.int32)) - 1, blk_exp) & 1
    later = lax.cummin(jnp.where(active, experts, N_EXPERTS), reverse=True)
    nxt_e = jnp.concatenate([later[1:], jnp.full((1,), N_EXPERTS, jnp.int32)])
    nxt = _lookup(jnp.where(nxt_e == N_EXPERTS, -1, nxt_e), blk_exp)
    meta = tuple(v.astype(jnp.int32) for v in (blk_exp, par, first, nxt, base, nval))
    row_sorted = a_ids + _lookup(blk_start * MOE_BLOCK - starts, e_s)
    _, pos = lax.sort_key_val(order, row_sorted)
    tok_sorted = jnp.concatenate([lax.shift_right_logical(order, 1), jnp.zeros((ROW_GROUP,), jnp.int32)])
    return meta, tok_sorted, pos, n_used.astype(jnp.int32).reshape(1)


def kernel(x, mem, norm1_w, w_in, q_norm_w, k_norm_w, lambda_q1, lambda_k1, lambda_q2, lambda_k2, subln_w, ssm_lambda_re, ssm_lambda_im, ssm_log_dt, ssm_b_re, ssm_b_im, ssm_c_re, ssm_c_im, ssm_d, ssm_glu_w, ssm_glu_b, ssm_out_norm_w, w_out, norm2_w, mem_norm_w, xq_w, xkv_w, xq_norm_w, xk_norm_w, xo_w, norm3_w, router_coarse_w, router_coarse_b, router_fine_w, router_fine_b, expert_w_gate, expert_w_up, expert_w_down):
    batch, seq, d = x.shape
    mem_len = mem.shape[1]
    t = batch * seq
    depth = norm1_w.shape[0]
    d_attn = DA_HEADS * DA_V_DIM
    d_ssm = d - d_attn
    qk_cols = DA_HEADS * 2 * DA_QK_DIM
    x_hd = d // X_HEADS
    h = x.reshape(t, d)
    mem2 = mem.reshape(batch * mem_len, d)

    for l in range(depth):
        lam_init = 0.8 - 0.6 * math.exp(-0.3 * l)
        lam = (jnp.exp(jnp.sum(lambda_q1[l].astype(F32) * lambda_k1[l].astype(F32)))
               - jnp.exp(jnp.sum(lambda_q2[l].astype(F32) * lambda_k2[l].astype(F32)))
               + lam_init).reshape(1)

        n_rep = qk_cols // DA_QK_DIM
        in_gain = jnp.concatenate([
            jnp.tile(q_norm_w[l].astype(F32) * (DA_QK_DIM ** -0.5 * LOG2E), n_rep),
            jnp.tile(k_norm_w[l].astype(F32), n_rep),
            jnp.ones((d_attn + d_ssm,), F32)])
        proj = _norm_matmul(h, norm1_w[l], w_in[l].astype(BF16), in_gain,
                            n_norm_cols=2 * qk_cols, chunk=DA_QK_DIM, tm=1024, tn=1024, name="in_proj")
        sub_gain = (subln_w[l].astype(F32) * (1.0 - lam_init)).reshape(1, DA_V_DIM)
        a = _diff_attn(proj, lam, sub_gain, batch=batch, seq=seq, tq=ATTN_TILE)

        seg_len = seq // SCAN_SEGS
        u = proj[:, 2 * qk_cols + d_attn:]
        u_perm = u.reshape(batch, SCAN_SEGS, seg_len, d_ssm).transpose(0, 2, 1, 3).reshape(batch, seq, d_ssm)
        bd, a_re, a_im, cd, dd = _s5_params(ssm_lambda_re[l], ssm_lambda_im[l], ssm_log_dt[l],
                                            ssm_b_re[l], ssm_b_im[l], ssm_c_re[l], ssm_c_im[l], ssm_d[l])
        y_perm = _s5(u_perm, bd, a_re, a_im, cd, dd)
        y = y_perm.reshape(batch, seg_len, SCAN_SEGS, d_ssm).transpose(0, 2, 1, 3).reshape(t, d_ssm)
        h = _mix_out(a, y, h, ssm_glu_w[l].astype(BF16), ssm_glu_b[l], ssm_out_norm_w[l],
                     w_out[l].astype(BF16), tm=ATTN_TILE, attn_tiles=seq // ATTN_TILE)

        kv_gain = jnp.concatenate([jnp.tile(xk_norm_w[l].astype(F32), X_HEADS), jnp.ones((d,), F32)])
        kv = _norm_matmul(mem2, mem_norm_w[l], xkv_w[l].astype(BF16), kv_gain,
                          n_norm_cols=d, chunk=x_hd, tm=512, tn=512, name="kv_proj")
        q_gain = jnp.tile(xq_norm_w[l].astype(F32) * (x_hd ** -0.5), X_HEADS)
        q = _norm_matmul(h, norm2_w[l], xq_w[l].astype(BF16), q_gain,
                         n_norm_cols=d, chunk=x_hd, tm=1024, tn=1024, name="xq_proj")
        r_w = jnp.concatenate([router_coarse_w[l].astype(F32), router_fine_w[l].astype(F32)], axis=1)
        r_w = jnp.pad(r_w, ((0, 0), (0, ROUTE_LANES - r_w.shape[1])))
        r_hi = r_w.astype(BF16)
        r_lo = (r_w - r_hi.astype(F32)).astype(BF16)
        r_b = jnp.concatenate([router_coarse_b[l].astype(F32), router_fine_b[l].astype(F32)])
        r_b = jnp.pad(r_b, (0, ROUTE_LANES - r_b.shape[0])).reshape(1, ROUTE_LANES)
        h2, hn3, eid, wts = _xattn_route(q, kv, h, xo_w[l].astype(BF16), norm3_w[l], r_hi, r_lo, r_b,
                                         batch=batch, seq=seq, mem_len=mem_len, tm=512)

        meta, tok_sorted, pos, n_used = _dispatch(eid[:, :TOP_K_FINE], t)
        y = _moe_experts(hn3, expert_w_gate[l], expert_w_up[l], expert_w_down[l], meta, tok_sorted, n_used)
        h = _combine(y, h2, wts, pos, tm=256)

    return h.reshape(batch, seq, d)
```

```python
import functools
import math

import jax
import jax.numpy as jnp
from jax import lax
from jax.experimental import pallas as pl
from jax.experimental.pallas import tpu as pltpu

F32 = jnp.float32
BF16 = jnp.bfloat16

EPS = 1e-6
DA_HEADS = 4
DA_QK_DIM = 128
DA_V_DIM = 256
SSM_GROUP = 16
SSM_STATE = 64
X_HEADS = 4
MOE_GROUPS = 8
EXP_PER_GROUP = 8
N_EXPERTS = MOE_GROUPS * EXP_PER_GROUP
TOP_K_FINE = 2

LANES = 128
SUBLANES = 8
MXU_TILE = 256
VMEM_LIMIT = 56 * 1024 * 1024
NEG = -1e30
LOG2E = math.log2(math.e)

SSM_CHUNK_GROUPS = LANES // SSM_GROUP
SSM_CHUNK_STATE = SSM_CHUNK_GROUPS * SSM_STATE
SCAN_SEGS = SUBLANES
ATTN_TILE = 512
MOE_BLOCK = 256
ROW_GROUP = SUBLANES
WEIGHT_DMA_CHUNKS = 4
ROUTE_LANES = LANES


def _rms(x, eps=EPS):
    return x * lax.rsqrt(jnp.mean(x * x, axis=-1, keepdims=True) + eps)


def _dot(a, b):
    return jnp.dot(a, b, preferred_element_type=F32)


def _dot_nt(a, b):
    return lax.dot_general(a, b, (((1,), (1,)), ((), ())), preferred_element_type=F32)


def _pack_halves(x):
    n = x.shape[1] // 2
    hi = lax.bitcast_convert_type(x[:, :n].astype(BF16).astype(F32), jnp.uint32)
    lo = lax.bitcast_convert_type(x[:, n:].astype(BF16).astype(F32), jnp.uint32)
    return hi | lax.shift_right_logical(lo, jnp.uint32(16))


def _unpack_halves(p):
    hi = lax.bitcast_convert_type(p & jnp.uint32(0xFFFF0000), F32)
    lo = lax.bitcast_convert_type(lax.shift_left(p, jnp.uint32(16)), F32)
    return hi, lo


def _resident(shape, index_map):
    return pl.BlockSpec(shape, index_map, pipeline_mode=pl.Buffered(1))


def _norm_matmul_kernel(x_ref, nw_ref, w_ref, g_ref, o_ref, xn_ref, *, n_norm_tiles, chunk):
    j = pl.program_id(1)

    @pl.when(j == 0)
    def _():
        x = x_ref[...].astype(F32)
        xn_ref[...] = (_rms(x) * nw_ref[...]).astype(BF16)

    normed = j < n_norm_tiles
    tn = w_ref.shape[1]
    sub = max(chunk, MXU_TILE)
    for s in range(tn // sub):
        acc = _dot(xn_ref[...], w_ref[:, s * sub:(s + 1) * sub])
        for c in range(sub // chunk):
            lo = s * sub + c * chunk
            a = acc[:, c * chunk:(c + 1) * chunk]
            inv = lax.rsqrt(jnp.mean(a * a, axis=-1, keepdims=True) + EPS)
            scale = jnp.where(normed, inv, 1.0)
            o_ref[:, lo:lo + chunk] = (a * scale * g_ref[:, lo:lo + chunk]).astype(o_ref.dtype)


def _norm_matmul(x, norm_w, w_bf16, gain, *, n_norm_cols, chunk, tm, tn, name):
    m, k = x.shape
    n = w_bf16.shape[1]
    assert m % tm == 0 and n % tn == 0 and tn % max(chunk, MXU_TILE) == 0 and n_norm_cols % tn == 0
    kern = functools.partial(_norm_matmul_kernel, n_norm_tiles=n_norm_cols // tn, chunk=chunk)
    return pl.pallas_call(
        kern,
        grid=(m // tm, n // tn),
        in_specs=[
            pl.BlockSpec((tm, k), lambda i, j: (i, 0)),
            pl.BlockSpec((1, k), lambda i, j: (0, 0)),
            pl.BlockSpec((k, tn), lambda i, j: (0, j)),
            pl.BlockSpec((1, tn), lambda i, j: (0, j)),
        ],
        out_specs=pl.BlockSpec((tm, tn), lambda i, j: (i, j)),
        out_shape=jax.ShapeDtypeStruct((m, n), BF16),
        scratch_shapes=[pltpu.VMEM((tm, k), BF16)],
        compiler_params=pltpu.CompilerParams(
            dimension_semantics=("parallel", "arbitrary"), vmem_limit_bytes=VMEM_LIMIT),
        name=name,
    )(x, norm_w.reshape(1, k).astype(F32), w_bf16, gain.reshape(1, n).astype(F32))


def _diff_attn_kernel(lam_ref, qa_ref, qb_ref, k_ref, v_ref, g_ref, o_ref, *stat_refs, tq, n_q):
    pair = pl.program_id(2)
    stats_a = (stat_refs[0:3], stat_refs[3:6])
    stats_b = (stat_refs[6:9], stat_refs[9:12])

    def scores(q_ref, j):
        return tuple(_dot_nt(q_ref[:, c * DA_QK_DIM:(c + 1) * DA_QK_DIM],
                             k_ref[j * tq:(j + 1) * tq, c * DA_QK_DIM:(c + 1) * DA_QK_DIM])
                     for c in range(2))

    def accumulate(stats, j, s_pair, masked):
        for s, (m_ref, l_ref, acc_ref) in zip(s_pair, stats):
            if masked:
                row = lax.broadcasted_iota(jnp.int32, s.shape, 0)
                col = lax.broadcasted_iota(jnp.int32, s.shape, 1)
                s = jnp.where(col <= row, s, NEG)
            m_old = m_ref[...]
            m_new = jnp.maximum(m_old, jnp.max(s, axis=-1, keepdims=True))
            p = jnp.exp2(s - m_new)
            alpha = jnp.exp2(m_old - m_new)
            l_ref[...] = alpha * l_ref[...] + jnp.sum(p, axis=-1, keepdims=True)
            acc_ref[...] = alpha * acc_ref[...] + _dot(p.astype(BF16), v_ref[j * tq:(j + 1) * tq, :])
            m_ref[...] = m_new

    def finish(stats, rows):
        (_, l1, acc1), (_, l2, acc2) = stats
        o = acc1[...] / l1[...] - lam_ref[0] * (acc2[...] / l2[...])
        o_ref[rows, :] = (_rms(o) * g_ref[...]).astype(o_ref.dtype)

    def run(p):
        tiles = ((qa_ref, stats_a, p), (qb_ref, stats_b, n_q - 1 - p))
        for _, stats, _ in tiles:
            for m_ref, l_ref, acc_ref in stats:
                m_ref[...] = jnp.full(m_ref.shape, NEG, F32)
                l_ref[...] = jnp.zeros(l_ref.shape, F32)
                acc_ref[...] = jnp.zeros(acc_ref.shape, F32)
        pending = [scores(q_ref, 0) for q_ref, _, _ in tiles]
        for j in range(n_q - p):
            for idx, (q_ref, stats, diag) in enumerate(tiles):
                if j > diag:
                    continue
                s_pair = pending[idx]
                if j < diag:
                    pending[idx] = scores(q_ref, j + 1)
                accumulate(stats, j, s_pair, masked=(j == diag))
        finish(stats_a, slice(0, tq))
        finish(stats_b, slice(tq, 2 * tq))

    for p in range(n_q // 2):
        pl.when(pair == p)(functools.partial(run, p))


def _attn_tile_pos(tile, nq):
    b, qt = tile // nq, tile % nq
    return b * nq + jnp.where(qt < nq // 2, 2 * qt, 2 * (nq - 1 - qt) + 1)


def _diff_attn(proj, lam, gain, *, batch, seq, tq):
    t = batch * seq
    nq = seq // tq
    assert nq % 2 == 0
    width = 2 * DA_QK_DIM
    k_blk0 = DA_HEADS
    v_blk0 = 2 * DA_HEADS
    kern = functools.partial(_diff_attn_kernel, tq=tq, n_q=nq)
    stat = [pltpu.VMEM((tq, 1), F32), pltpu.VMEM((tq, 1), F32), pltpu.VMEM((tq, DA_V_DIM), F32)]
    return pl.pallas_call(
        kern,
        grid_spec=pltpu.PrefetchScalarGridSpec(
            num_scalar_prefetch=1,
            grid=(batch, DA_HEADS, nq // 2),
            in_specs=[
                pl.BlockSpec((tq, width), lambda b, h, p, lam: (b * nq + p, h)),
                pl.BlockSpec((tq, width), lambda b, h, p, lam: (b * nq + nq - 1 - p, h)),
                pl.BlockSpec((seq, width), lambda b, h, p, lam: (b, k_blk0 + h)),
                pl.BlockSpec((seq, width), lambda b, h, p, lam: (b, v_blk0 + h)),
                pl.BlockSpec((1, DA_V_DIM), lambda b, h, p, lam: (0, 0)),
            ],
            out_specs=pl.BlockSpec((2 * tq, DA_V_DIM), lambda b, h, p, lam: (b * (nq // 2) + p, h)),
            scratch_shapes=stat * 4,
        ),
        out_shape=jax.ShapeDtypeStruct((t, DA_HEADS * DA_V_DIM), BF16),
        compiler_params=pltpu.CompilerParams(
            dimension_semantics=("parallel", "parallel", "arbitrary"), vmem_limit_bytes=VMEM_LIMIT),
        name="diff_attn",
    )(lam, proj, proj, proj, proj, gain)


def _s5_kernel(u_ref, bd_ref, ar_ref, ai_ref, cd_ref, d_ref, o_ref, xs_ref, us_ref, ys_ref, *, seq, rows):
    ns = SSM_CHUNK_STATE
    seg_len = seq // SCAN_SEGS
    n_row_blk = seq // rows
    steps = rows // SCAN_SEGS

    for seg in range(SCAN_SEGS):
        us_ref[pl.ds(seg, seg_len, stride=SCAN_SEGS), :] = (
            u_ref[seg * seg_len:(seg + 1) * seg_len, :].astype(F32))

    def in_map(r):
        rs = slice(r * rows, (r + 1) * rows)
        xs_ref[rs, :] = _dot(us_ref[rs, :].astype(BF16), bd_ref[...])

    def out_map(r):
        rs = slice(r * rows, (r + 1) * rows)
        y = _dot(xs_ref[rs, :].astype(BF16), cd_ref[...]) + d_ref[...] * us_ref[rs, :]
        ys_ref[rs, :] = jax.nn.gelu(y)
        for seg in range(SCAN_SEGS):
            t0 = seg * seg_len + r * steps
            o_ref[t0:t0 + steps, :] = (
                ys_ref[pl.ds(r * rows + seg, steps, stride=SCAN_SEGS), :].astype(o_ref.dtype))

    ar = jnp.broadcast_to(ar_ref[...], (SCAN_SEGS, ns))
    ai = jnp.broadcast_to(ai_ref[...], (SCAN_SEGS, ns))

    def advance(t, sr, si):
        ts = slice(t * SCAN_SEGS, (t + 1) * SCAN_SEGS)
        return ar * sr - ai * si + xs_ref[ts, 0:ns], ar * si + ai * sr + xs_ref[ts, ns:2 * ns]

    in_map(0)
    fr = fi = jnp.zeros((SCAN_SEGS, ns), F32)
    for r in range(n_row_blk):
        if r + 1 < n_row_blk:
            in_map(r + 1)
        for t in range(r * steps, (r + 1) * steps):
            fr, fi = advance(t, fr, fi)

    pr, pi = ar, ai
    for _ in range(int(math.log2(seg_len))):
        pr, pi = pr * pr - pi * pi, 2.0 * pr * pi
    seg = lax.broadcasted_iota(jnp.int32, (SCAN_SEGS, ns), 0)

    def shifted(x, k):
        return jnp.where(seg >= k, pltpu.roll(x, k, 0), 0.0)

    k = 1
    while k < SCAN_SEGS:
        gr, gi = shifted(fr, k), shifted(fi, k)
        fr, fi = fr + pr * gr - pi * gi, fi + pr * gi + pi * gr
        pr, pi = pr * pr - pi * pi, 2.0 * pr * pi
        k *= 2
    sr, si = shifted(fr, 1), shifted(fi, 1)

    for r in range(n_row_blk):
        for t in range(r * steps, (r + 1) * steps):
            sr, si = advance(t, sr, si)
            ts = slice(t * SCAN_SEGS, (t + 1) * SCAN_SEGS)
            xs_ref[ts, 0:ns] = sr
            xs_ref[ts, ns:2 * ns] = si
        if r >= 1:
            out_map(r - 1)
    out_map(n_row_blk - 1)


def _s5(proj, u_col0, d_ssm, bd, a_re, a_im, cd, d_skip, *, batch, seq, rows=256):
    n_chunks = d_ssm // LANES
    u_blk0 = u_col0 // LANES
    kern = functools.partial(_s5_kernel, seq=seq, rows=rows)
    return pl.pallas_call(
        kern,
        grid=(batch, n_chunks),
        in_specs=[
            pl.BlockSpec((seq, LANES), lambda b, c: (b, u_blk0 + c)),
            pl.BlockSpec((None, LANES, 2 * SSM_CHUNK_STATE), lambda b, c: (c, 0, 0)),
            pl.BlockSpec((None, 1, SSM_CHUNK_STATE), lambda b, c: (c, 0, 0)),
            pl.BlockSpec((None, 1, SSM_CHUNK_STATE), lambda b, c: (c, 0, 0)),
            pl.BlockSpec((None, 2 * SSM_CHUNK_STATE, LANES), lambda b, c: (c, 0, 0)),
            pl.BlockSpec((None, 1, LANES), lambda b, c: (c, 0, 0)),
        ],
        out_specs=pl.BlockSpec((seq, LANES), lambda b, c: (b, c)),
        out_shape=jax.ShapeDtypeStruct((batch * seq, d_ssm), BF16),
        scratch_shapes=[pltpu.VMEM((seq, 2 * SSM_CHUNK_STATE), F32),
                        pltpu.VMEM((seq, LANES), F32), pltpu.VMEM((seq, LANES), F32)],
        compiler_params=pltpu.CompilerParams(
            dimension_semantics=("parallel", "parallel"), vmem_limit_bytes=VMEM_LIMIT),
        name="s5_scan",
    )(proj, bd, a_re, a_im, cd, d_skip)


def _s5_params(lam_re, lam_im, log_dt, b_re, b_im, c_re, c_im, d_skip):
    g = lam_re.shape[0]
    nc = g // SSM_CHUNK_GROUPS
    lr = jnp.minimum(lam_re.astype(F32), -1e-4)
    li = lam_im.astype(F32)
    dt = jnp.exp(log_dt.astype(F32))[:, None]
    mag = jnp.exp(lr * dt)
    lb_re, lb_im = mag * jnp.cos(li * dt), mag * jnp.sin(li * dt)
    den = lr * lr + li * li
    coef_re = ((lb_re - 1.0) * lr + lb_im * li) / den
    coef_im = (lb_im * lr - (lb_re - 1.0) * li) / den
    br, bi = b_re.astype(F32), b_im.astype(F32)
    bb_re = coef_re[..., None] * br - coef_im[..., None] * bi
    bb_im = coef_re[..., None] * bi + coef_im[..., None] * br
    eye = jnp.eye(SSM_CHUNK_GROUPS, dtype=F32)

    def pack_in(bb):
        bb = bb.reshape(nc, SSM_CHUNK_GROUPS, SSM_STATE, SSM_GROUP)
        return jnp.einsum('cgph,gk->cghkp', bb, eye).reshape(nc, LANES, SSM_CHUNK_STATE)

    def pack_out(cc):
        cc = cc.astype(F32).reshape(nc, SSM_CHUNK_GROUPS, SSM_GROUP, SSM_STATE)
        return jnp.einsum('cghp,gk->ckpgh', cc, eye).reshape(nc, SSM_CHUNK_STATE, LANES)

    bd = jnp.concatenate([pack_in(bb_re), pack_in(bb_im)], axis=-1).astype(BF16)
    cd = jnp.concatenate([pack_out(c_re), -pack_out(c_im)], axis=1).astype(BF16)
    a_re = lb_re.reshape(nc, 1, SSM_CHUNK_STATE)
    a_im = lb_im.reshape(nc, 1, SSM_CHUNK_STATE)
    dd = d_skip.astype(F32).reshape(nc, 1, LANES)
    return bd, a_re, a_im, cd, dd


def _mix_out_kernel(a_ref, y_ref, x_ref, gw_ref, gb_ref, nw_ref, wo_ref, o_ref):
    d_attn = a_ref.shape[1]
    y = y_ref[...]
    gate = _dot(y, gw_ref[...]) + gb_ref[...]
    s = y.astype(F32) * jax.nn.sigmoid(gate)
    sn = (_rms(s) * nw_ref[...]).astype(BF16)
    acc = _dot(a_ref[...], wo_ref[0:d_attn, :]) + _dot(sn, wo_ref[d_attn:, :])
    o_ref[...] = x_ref[...] + acc


def _mix_out(a, y, x, glu_w, glu_b, norm_w, w_out, *, tm, attn_tiles):
    t, d = x.shape
    d_attn, d_ssm = a.shape[1], y.shape[1]
    const = lambda i: (0, 0)
    return pl.pallas_call(
        _mix_out_kernel,
        grid=(t // tm,),
        in_specs=[
            pl.BlockSpec((tm, d_attn), lambda i: (_attn_tile_pos(i, attn_tiles), 0)),
            pl.BlockSpec((tm, d_ssm), lambda i: (i, 0)),
            pl.BlockSpec((tm, d), lambda i: (i, 0)),
            _resident((d_ssm, d_ssm), const),
            _resident((1, d_ssm), const),
            _resident((1, d_ssm), const),
            _resident((d, d), const),
        ],
        out_specs=pl.BlockSpec((tm, d), lambda i: (i, 0)),
        out_shape=jax.ShapeDtypeStruct((t, d), F32),
        compiler_params=pltpu.CompilerParams(
            dimension_semantics=("parallel",), vmem_limit_bytes=VMEM_LIMIT),
        name="mix_out",
    )(a, y, x, glu_w, glu_b.reshape(1, d_ssm).astype(F32), norm_w.reshape(1, d_ssm).astype(F32), w_out)


def _xattn_route_kernel(q_ref, k_ref, v_ref, h_ref, xo_ref, nw_ref, rhi_ref, rlo_ref, rb_ref,
                        h2_ref, hn_ref, eid_ref, wts_ref):
    d = h_ref.shape[1]
    hd = d // X_HEADS
    h2 = h_ref[...]
    for h in range(X_HEADS):
        sl = slice(h * hd, (h + 1) * hd)
        s = _dot_nt(q_ref[:, sl], k_ref[:, sl])
        p = jnp.exp(s - jnp.max(s, axis=-1, keepdims=True))
        p = p * (1.0 / jnp.sum(p, axis=-1, keepdims=True))
        o = _dot(p.astype(BF16), v_ref[:, sl]).astype(BF16)
        h2 = h2 + _dot(o, xo_ref[sl, :])
    h2_ref[...] = h2
    hn = _rms(h2) * nw_ref[...]
    hn_ref[...] = _pack_halves(hn)

    hi = hn.astype(BF16)
    lo = (hn - hi.astype(F32)).astype(BF16)
    hi_both = _dot(hi, jnp.concatenate([rhi_ref[...], rlo_ref[...]], axis=1))
    logits = (hi_both[:, :ROUTE_LANES] + hi_both[:, ROUTE_LANES:] + _dot(lo, rhi_ref[...])) + rb_ref[...]

    lane = lax.broadcasted_iota(jnp.int32, logits.shape, 1)
    big = jnp.int32(ROUTE_LANES)

    def first_lane(cond):
        return jnp.min(jnp.where(cond, lane, big), axis=-1, keepdims=True)

    c_mask = lane < MOE_GROUPS
    lc = jnp.where(c_mask, logits, NEG)
    mc = jnp.max(lc, axis=-1, keepdims=True)
    ec = jnp.exp(lc - mc)
    p_c = ec / jnp.sum(ec, axis=-1, keepdims=True)
    p_grp = jnp.max(p_c, axis=-1, keepdims=True)
    grp = first_lane(c_mask & (p_c == p_grp))
    f_lo = MOE_GROUPS + grp * EXP_PER_GROUP
    f_mask = (lane >= f_lo) & (lane < f_lo + EXP_PER_GROUP)
    lf = jnp.where(f_mask, logits, NEG)
    mf = jnp.max(lf, axis=-1, keepdims=True)
    ef = jnp.exp(lf - mf)
    pf = ef / jnp.sum(ef, axis=-1, keepdims=True)
    v1 = jnp.max(jnp.where(f_mask, pf, -1.0), axis=-1, keepdims=True)
    i1 = first_lane(f_mask & (pf == v1))
    rest = f_mask & (lane != i1)
    v2 = jnp.max(jnp.where(rest, pf, -1.0), axis=-1, keepdims=True)
    i2 = first_lane(rest & (pf == v2))
    tot = v1 + v2
    w1 = v1 / tot * p_grp
    w2 = v2 / tot * p_grp
    eid_ref[...] = jnp.where(lane == 0, i1 - MOE_GROUPS, jnp.where(lane == 1, i2 - MOE_GROUPS, 0))
    wts_ref[...] = jnp.where(lane == 0, w1, jnp.where(lane == 1, w2, 0.0))


def _xattn_route(q, kv, h1, xo_w, norm_w, r_hi, r_lo, r_b, *, batch, seq, mem_len, tm):
    t, d = h1.shape
    n = seq // tm
    const = lambda b, i: (0, 0)
    row = lambda b, i: (b * n + i, 0)
    return pl.pallas_call(
        _xattn_route_kernel,
        grid=(batch, n),
        in_specs=[
            pl.BlockSpec((tm, d), row),
            pl.BlockSpec((mem_len, d), lambda b, i: (b, 0)),
            pl.BlockSpec((mem_len, d), lambda b, i: (b, 1)),
            pl.BlockSpec((tm, d), row),
            _resident((d, d), const),
            _resident((1, d), const),
            _resident((d, ROUTE_LANES), const),
            _resident((d, ROUTE_LANES), const),
            _resident((1, ROUTE_LANES), const),
        ],
        out_specs=[
            pl.BlockSpec((tm, d), row),
            pl.BlockSpec((tm, d // 2), row),
            pl.BlockSpec((tm, ROUTE_LANES), row),
            pl.BlockSpec((tm, ROUTE_LANES), row),
        ],
        out_shape=[
            jax.ShapeDtypeStruct((t, d), F32),
            jax.ShapeDtypeStruct((t, d // 2), jnp.uint32),
            jax.ShapeDtypeStruct((t, ROUTE_LANES), jnp.int32),
            jax.ShapeDtypeStruct((t, ROUTE_LANES), F32),
        ],
        compiler_params=pltpu.CompilerParams(
            dimension_semantics=("parallel", "parallel"), vmem_limit_bytes=VMEM_LIMIT),
        name="xattn_route",
    )(q, kv, kv, h1, xo_w, norm_w.reshape(1, d).astype(F32), r_hi, r_lo, r_b)


def _moe_kernel(be_ref, par_ref, first_ref, nxt_ref, base_ref, nval_ref, tok_ref, nu_ref,
                hn_hbm, wg_hbm, wu_hbm, wd_hbm, o_ref,
                xbuf, wgb, wub, wdb, gsem, wsem):
    b = pl.program_id(0)
    n_used = nu_ref[0]

    def weight_copies(e, slot):
        copies = []
        for hbm, buf in ((wg_hbm, wgb), (wu_hbm, wub), (wd_hbm, wdb)):
            rows = hbm.shape[1] // WEIGHT_DMA_CHUNKS
            for c in range(WEIGHT_DMA_CHUNKS):
                sl = pl.ds(c * rows, rows)
                copies.append(pltpu.make_async_copy(hbm.at[e, sl], buf.at[slot, sl], wsem.at[slot]))
        return copies

    def groups(blk):
        return (nval_ref[blk] + ROW_GROUP - 1) // ROW_GROUP

    def start_gather(blk, slot):
        base = base_ref[blk]

        def body(g, carry):
            for r in range(ROW_GROUP):
                tok = tok_ref[base + g * ROW_GROUP + r]
                pltpu.make_async_copy(hn_hbm.at[pl.ds(tok, 1)], xbuf.at[slot, g, pl.ds(r, 1)],
                                      gsem.at[slot]).start()
            return carry
        lax.fori_loop(0, groups(blk), body, 0)

    def wait_gather(blk, slot):
        filled = xbuf.at[slot, pl.ds(0, groups(blk))]
        pltpu.make_async_copy(filled, filled, gsem.at[slot]).wait()

    @pl.when(b == 0)
    def _():
        xbuf[...] = jnp.zeros(xbuf.shape, xbuf.dtype)
        for c in weight_copies(be_ref[0], par_ref[0]):
            c.start()
        start_gather(0, 0)

    @pl.when(b < n_used)
    def _():
        slot = b % 2
        wslot = par_ref[b]
        is_first = first_ref[b] == 1

        @pl.when(is_first & (nxt_ref[b] >= 0))
        def _():
            for c in weight_copies(nxt_ref[b], 1 - wslot):
                c.start()

        @pl.when(b + 1 < n_used)
        def _():
            start_gather(b + 1, 1 - slot)

        @pl.when(is_first)
        def _():
            for c in weight_copies(0, wslot):
                c.wait()

        wait_gather(b, slot)
        half = xbuf.shape[-1]
        x_hi, x_lo = (v.astype(BF16) for v in _unpack_halves(xbuf[slot].reshape(MOE_BLOCK, half)))

        def up(w):
            return _dot(x_hi, w[wslot, 0:half, :].astype(BF16)) + _dot(x_lo, w[wslot, half:, :].astype(BF16))

        mid = (jax.nn.silu(up(wgb)) * up(wub)).astype(BF16)
        o_ref[...] = _pack_halves(_dot(mid, wdb[wslot].astype(BF16)))

    @pl.when(b >= n_used)
    def _():
        o_ref[...] = jnp.zeros(o_ref.shape, o_ref.dtype)


def _moe_experts(hn_packed, w_gate, w_up, w_down, meta, tok_sorted, n_used):
    d, d_ff = w_gate.shape[1:]
    half = hn_packed.shape[1]
    blk_exp, par, first, nxt, base, nval = meta
    n_blocks = blk_exp.shape[0]
    any_spec = pl.BlockSpec(memory_space=pl.ANY)
    return pl.pallas_call(
        _moe_kernel,
        grid_spec=pltpu.PrefetchScalarGridSpec(
            num_scalar_prefetch=8,
            grid=(n_blocks,),
            in_specs=[any_spec, any_spec, any_spec, any_spec],
            out_specs=pl.BlockSpec((MOE_BLOCK, half), lambda b, *_: (b, 0)),
            scratch_shapes=[
                pltpu.VMEM((2, MOE_BLOCK // ROW_GROUP, ROW_GROUP, half), jnp.uint32),
                pltpu.VMEM((2, d, d_ff), F32),
                pltpu.VMEM((2, d, d_ff), F32),
                pltpu.VMEM((2, d_ff, d), F32),
                pltpu.SemaphoreType.DMA((2,)),
                pltpu.SemaphoreType.DMA((2,)),
            ],
        ),
        out_shape=jax.ShapeDtypeStruct((n_blocks * MOE_BLOCK, half), jnp.uint32),
        compiler_params=pltpu.CompilerParams(
            dimension_semantics=("arbitrary",), vmem_limit_bytes=VMEM_LIMIT),
        name="moe_experts",
    )(blk_exp, par, first, nxt, base, nval, tok_sorted, n_used, hn_packed, w_gate, w_up, w_down)


def _combine_kernel(pos_ref, y_hbm, h_ref, w_ref, o_ref, ybuf, sem, *, tm):
    i = pl.program_id(0)
    n_groups = tm // ROW_GROUP

    def start_gather(tile, slot):
        base = tile * (tm * TOP_K_FINE)

        def body(g, carry):
            for r in range(ROW_GROUP):
                for k in range(TOP_K_FINE):
                    row = pos_ref[base + (g * ROW_GROUP + r) * TOP_K_FINE + k]
                    pltpu.make_async_copy(y_hbm.at[pl.ds(row, 1)], ybuf.at[slot, k, g, pl.ds(r, 1)],
                                          sem.at[slot]).start()
            return carry
        lax.fori_loop(0, n_groups, body, 0)

    @pl.when(i == 0)
    def _():
        start_gather(0, 0)

    @pl.when(i + 1 < pl.num_programs(0))
    def _():
        start_gather(i + 1, (i + 1) % 2)

    slot = i % 2
    pltpu.make_async_copy(ybuf.at[slot], ybuf.at[slot], sem.at[slot]).wait()
    w = w_ref[...]
    half = ybuf.shape[-1]
    y0 = _unpack_halves(ybuf[slot, 0].reshape(tm, half))
    y1 = _unpack_halves(ybuf[slot, 1].reshape(tm, half))
    for c in range(2):
        cols = slice(c * half, (c + 1) * half)
        o_ref[:, cols] = h_ref[:, cols] + (w[:, 0:1] * y0[c] + w[:, 1:2] * y1[c])


def _combine(y, h2, wts, pos, *, tm):
    t, d = h2.shape
    kern = functools.partial(_combine_kernel, tm=tm)
    return pl.pallas_call(
        kern,
        grid_spec=pltpu.PrefetchScalarGridSpec(
            num_scalar_prefetch=1,
            grid=(t // tm,),
            in_specs=[
                pl.BlockSpec(memory_space=pl.ANY),
                pl.BlockSpec((tm, d), lambda i, pos: (i, 0)),
                pl.BlockSpec((tm, ROUTE_LANES), lambda i, pos: (i, 0)),
            ],
            out_specs=pl.BlockSpec((tm, d), lambda i, pos: (i, 0)),
            scratch_shapes=[
                pltpu.VMEM((2, TOP_K_FINE, tm // ROW_GROUP, ROW_GROUP, y.shape[1]), jnp.uint32),
                pltpu.SemaphoreType.DMA((2,)),
            ],
        ),
        out_shape=jax.ShapeDtypeStruct((t, d), F32),
        compiler_params=pltpu.CompilerParams(
            dimension_semantics=("arbitrary",), vmem_limit_bytes=VMEM_LIMIT),
        name="moe_combine",
    )(pos, y, h2, wts)


def _lookup(table, idx):
    sel = idx[:, None] == jnp.arange(table.shape[0], dtype=jnp.int32)[None, :]
    return jnp.sum(jnp.where(sel, table[None, :], 0), axis=1).astype(jnp.int32)


def _dispatch(eid, n_tokens):
    n_assign = n_tokens * TOP_K_FINE
    experts = jnp.arange(N_EXPERTS, dtype=jnp.int32)
    e_flat = eid.reshape(n_assign)
    a_ids = jnp.arange(n_assign, dtype=jnp.int32)
    e_s, order = lax.sort_key_val(e_flat, a_ids)
    counts = jnp.sum((e_flat[:, None] == experts[None, :]).astype(jnp.int32), axis=0)
    starts = jnp.cumsum(counts) - counts
    nb = (counts + MOE_BLOCK - 1) // MOE_BLOCK
    blk_end = jnp.cumsum(nb)
    blk_start = blk_end - nb
    n_used = blk_end[-1]
    n_blocks = (n_assign + N_EXPERTS * (MOE_BLOCK - 1)) // MOE_BLOCK
    b_ids = jnp.arange(n_blocks, dtype=jnp.int32)
    used = b_ids < n_used
    blk_exp = jnp.minimum(jnp.sum((blk_end[None, :] <= b_ids[:, None]).astype(jnp.int32), axis=1),
                          N_EXPERTS - 1)
    j = b_ids - _lookup(blk_start, blk_exp)
    base = jnp.where(used, _lookup(starts, blk_exp) + j * MOE_BLOCK, 0)
    nval = jnp.where(used, jnp.clip(_lookup(counts, blk_exp) - j * MOE_BLOCK, 0, MOE_BLOCK), 0)
    first = (used & (j == 0)).astype(jnp.int32)
    active = counts > 0
    par = _lookup(jnp.cumsum(active.astype(jnp.int32)) - 1, blk_exp) & 1
    later = lax.cummin(jnp.where(active, experts, N_EXPERTS), reverse=True)
    nxt_e = jnp.concatenate([later[1:], jnp.full((1,), N_EXPERTS, jnp.int32)])
    nxt = _lookup(jnp.where(nxt_e == N_EXPERTS, -1, nxt_e), blk_exp)
    meta = tuple(v.astype(jnp.int32) for v in (blk_exp, par, first, nxt, base, nval))
    row_sorted = a_ids + _lookup(blk_start * MOE_BLOCK - starts, e_s)
    _, pos = lax.sort_key_val(order, row_sorted)
    tok_sorted = jnp.concatenate([lax.shift_right_logical(order, 1), jnp.zeros((ROW_GROUP,), jnp.int32)])
    return meta, tok_sorted, pos, n_used.astype(jnp.int32).reshape(1)


def kernel(x, mem, norm1_w, w_in, q_norm_w, k_norm_w, lambda_q1, lambda_k1, lambda_q2, lambda_k2, subln_w, ssm_lambda_re, ssm_lambda_im, ssm_log_dt, ssm_b_re, ssm_b_im, ssm_c_re, ssm_c_im, ssm_d, ssm_glu_w, ssm_glu_b, ssm_out_norm_w, w_out, norm2_w, mem_norm_w, xq_w, xkv_w, xq_norm_w, xk_norm_w, xo_w, norm3_w, router_coarse_w, router_coarse_b, router_fine_w, router_fine_b, expert_w_gate, expert_w_up, expert_w_down):
    batch, seq, d = x.shape
    mem_len = mem.shape[1]
    t = batch * seq
    depth = norm1_w.shape[0]
    d_attn = DA_HEADS * DA_V_DIM
    d_ssm = d - d_attn
    qk_cols = DA_HEADS * 2 * DA_QK_DIM
    x_hd = d // X_HEADS
    h = x.reshape(t, d)
    mem2 = mem.reshape(batch * mem_len, d)

    for l in range(depth):
        lam_init = 0.8 - 0.6 * math.exp(-0.3 * l)
        lam = (jnp.exp(jnp.sum(lambda_q1[l].astype(F32) * lambda_k1[l].astype(F32)))
               - jnp.exp(jnp.sum(lambda_q2[l].astype(F32) * lambda_k2[l].astype(F32)))
               + lam_init).reshape(1)

        n_rep = qk_cols // DA_QK_DIM
        in_gain = jnp.concatenate([
            jnp.tile(q_norm_w[l].astype(F32) * (DA_QK_DIM ** -0.5 * LOG2E), n_rep),
            jnp.tile(k_norm_w[l].astype(F32), n_rep),
            jnp.ones((d_attn + d_ssm,), F32)])
        proj = _norm_matmul(h, norm1_w[l], w_in[l].astype(BF16), in_gain,
                            n_norm_cols=2 * qk_cols, chunk=DA_QK_DIM, tm=1024, tn=1024, name="in_proj")
        sub_gain = (subln_w[l].astype(F32) * (1.0 - lam_init)).reshape(1, DA_V_DIM)
        a = _diff_attn(proj, lam, sub_gain, batch=batch, seq=seq, tq=ATTN_TILE)

        bd, a_re, a_im, cd, dd = _s5_params(ssm_lambda_re[l], ssm_lambda_im[l], ssm_log_dt[l],
                                            ssm_b_re[l], ssm_b_im[l], ssm_c_re[l], ssm_c_im[l], ssm_d[l])
        y = _s5(proj, 2 * qk_cols + d_attn, d_ssm, bd, a_re, a_im, cd, dd, batch=batch, seq=seq)
        h = _mix_out(a, y, h, ssm_glu_w[l].astype(BF16), ssm_glu_b[l], ssm_out_norm_w[l],
                     w_out[l].astype(BF16), tm=ATTN_TILE, attn_tiles=seq // ATTN_TILE)

        kv_gain = jnp.concatenate([jnp.tile(xk_norm_w[l].astype(F32), X_HEADS), jnp.ones((d,), F32)])
        kv = _norm_matmul(mem2, mem_norm_w[l], xkv_w[l].astype(BF16), kv_gain,
                          n_norm_cols=d, chunk=x_hd, tm=512, tn=512, name="kv_proj")
        q_gain = jnp.tile(xq_norm_w[l].astype(F32) * (x_hd ** -0.5), X_HEADS)
        q = _norm_matmul(h, norm2_w[l], xq_w[l].astype(BF16), q_gain,
                         n_norm_cols=d, chunk=x_hd, tm=1024, tn=1024, name="xq_proj")
        r_w = jnp.concatenate([router_coarse_w[l].astype(F32), router_fine_w[l].astype(F32)], axis=1)
        r_w = jnp.pad(r_w, ((0, 0), (0, ROUTE_LANES - r_w.shape[1])))
        r_hi = r_w.astype(BF16)
        r_lo = (r_w - r_hi.astype(F32)).astype(BF16)
        r_b = jnp.concatenate([router_coarse_b[l].astype(F32), router_fine_b[l].astype(F32)])
        r_b = jnp.pad(r_b, (0, ROUTE_LANES - r_b.shape[0])).reshape(1, ROUTE_LANES)
        h2, hn3, eid, wts = _xattn_route(q, kv, h, xo_w[l].astype(BF16), norm3_w[l], r_hi, r_lo, r_b,
                                         batch=batch, seq=seq, mem_len=mem_len, tm=512)

        meta, tok_sorted, pos, n_used = _dispatch(eid[:, :TOP_K_FINE], t)
        y = _moe_experts(hn3, expert_w_gate[l], expert_w_up[l], expert_w_down[l], meta, tok_sorted, n_used)
        h = _combine(y, h2, wts, pos, tm=256)

    return h.reshape(batch, seq, d)
```

```python
import functools
import math

import jax
import jax.numpy as jnp
from jax import lax
from jax.experimental import pallas as pl
from jax.experimental.pallas import tpu as pltpu

F32 = jnp.float32
BF16 = jnp.bfloat16

EPS = 1e-6
DA_HEADS = 4
DA_QK_DIM = 128
DA_V_DIM = 256
SSM_GROUP = 16
SSM_STATE = 64
X_HEADS = 4
MOE_GROUPS = 8
EXP_PER_GROUP = 8
N_EXPERTS = MOE_GROUPS * EXP_PER_GROUP
TOP_K_FINE = 2

LANES = 128
SUBLANES = 8
MXU_TILE = 256
VMEM_LIMIT = 56 * 1024 * 1024
NEG = -1e30
LOG2E = math.log2(math.e)

SSM_CHUNK_GROUPS = LANES // SSM_GROUP
SSM_CHUNK_STATE = SSM_CHUNK_GROUPS * SSM_STATE
SCAN_SEGS = SUBLANES
ATTN_TILE = 512
MOE_BLOCK = 256
ROW_GROUP = SUBLANES
WEIGHT_DMA_CHUNKS = 4
WEIGHT_SLOTS = 3
ROUTE_LANES = LANES


def _rms(x, eps=EPS):
    return x * lax.rsqrt(jnp.mean(x * x, axis=-1, keepdims=True) + eps)


def _dot(a, b):
    return jnp.dot(a, b, preferred_element_type=F32)


def _dot_nt(a, b):
    return lax.dot_general(a, b, (((1,), (1,)), ((), ())), preferred_element_type=F32)


def _pack_halves(x):
    n = x.shape[1] // 2
    hi = lax.bitcast_convert_type(x[:, :n].astype(BF16).astype(F32), jnp.uint32)
    lo = lax.bitcast_convert_type(x[:, n:].astype(BF16).astype(F32), jnp.uint32)
    return hi | lax.shift_right_logical(lo, jnp.uint32(16))


def _unpack_halves(p):
    hi = lax.bitcast_convert_type(p & jnp.uint32(0xFFFF0000), F32)
    lo = lax.bitcast_convert_type(lax.shift_left(p, jnp.uint32(16)), F32)
    return hi, lo


def _resident(shape, index_map):
    return pl.BlockSpec(shape, index_map, pipeline_mode=pl.Buffered(1))


def _norm_matmul_kernel(x_ref, nw_ref, w_ref, g_ref, o_ref, xn_ref, *, n_norm_tiles, chunk):
    j = pl.program_id(1)

    @pl.when(j == 0)
    def _():
        x = x_ref[...].astype(F32)
        xn_ref[...] = (_rms(x) * nw_ref[...]).astype(BF16)

    normed = j < n_norm_tiles
    tn = w_ref.shape[1]
    sub = max(chunk, MXU_TILE)
    for s in range(tn // sub):
        acc = _dot(xn_ref[...], w_ref[:, s * sub:(s + 1) * sub])
        for c in range(sub // chunk):
            lo = s * sub + c * chunk
            a = acc[:, c * chunk:(c + 1) * chunk]
            inv = lax.rsqrt(jnp.mean(a * a, axis=-1, keepdims=True) + EPS)
            scale = jnp.where(normed, inv, 1.0)
            o_ref[:, lo:lo + chunk] = (a * scale * g_ref[:, lo:lo + chunk]).astype(o_ref.dtype)


def _norm_matmul(x, norm_w, w_bf16, gain, *, n_norm_cols, chunk, tm, tn, name):
    m, k = x.shape
    n = w_bf16.shape[1]
    assert m % tm == 0 and n % tn == 0 and tn % max(chunk, MXU_TILE) == 0 and n_norm_cols % tn == 0
    kern = functools.partial(_norm_matmul_kernel, n_norm_tiles=n_norm_cols // tn, chunk=chunk)
    return pl.pallas_call(
        kern,
        grid=(m // tm, n // tn),
        in_specs=[
            pl.BlockSpec((tm, k), lambda i, j: (i, 0)),
            pl.BlockSpec((1, k), lambda i, j: (0, 0)),
            pl.BlockSpec((k, tn), lambda i, j: (0, j)),
            pl.BlockSpec((1, tn), lambda i, j: (0, j)),
        ],
        out_specs=pl.BlockSpec((tm, tn), lambda i, j: (i, j)),
        out_shape=jax.ShapeDtypeStruct((m, n), BF16),
        scratch_shapes=[pltpu.VMEM((tm, k), BF16)],
        compiler_params=pltpu.CompilerParams(
            dimension_semantics=("parallel", "arbitrary"), vmem_limit_bytes=VMEM_LIMIT),
        name=name,
    )(x, norm_w.reshape(1, k).astype(F32), w_bf16, gain.reshape(1, n).astype(F32))


def _diff_attn_kernel(lam_ref, qa_ref, qb_ref, k_ref, v_ref, g_ref, o_ref, *stat_refs, tq, n_q):
    pair = pl.program_id(2)
    stats_a = (stat_refs[0:3], stat_refs[3:6])
    stats_b = (stat_refs[6:9], stat_refs[9:12])

    def scores(q_ref, j):
        return tuple(_dot_nt(q_ref[:, c * DA_QK_DIM:(c + 1) * DA_QK_DIM],
                             k_ref[j * tq:(j + 1) * tq, c * DA_QK_DIM:(c + 1) * DA_QK_DIM])
                     for c in range(2))

    def accumulate(stats, j, s_pair, masked):
        for s, (m_ref, l_ref, acc_ref) in zip(s_pair, stats):
            if masked:
                row = lax.broadcasted_iota(jnp.int32, s.shape, 0)
                col = lax.broadcasted_iota(jnp.int32, s.shape, 1)
                s = jnp.where(col <= row, s, NEG)
            m_old = m_ref[...]
            m_new = jnp.maximum(m_old, jnp.max(s, axis=-1, keepdims=True))
            p = jnp.exp2(s - m_new)
            alpha = jnp.exp2(m_old - m_new)
            l_ref[...] = alpha * l_ref[...] + jnp.sum(p, axis=-1, keepdims=True)
            acc_ref[...] = alpha * acc_ref[...] + _dot(p.astype(BF16), v_ref[j * tq:(j + 1) * tq, :])
            m_ref[...] = m_new

    def finish(stats, rows):
        (_, l1, acc1), (_, l2, acc2) = stats
        o = acc1[...] / l1[...] - lam_ref[0] * (acc2[...] / l2[...])
        o_ref[rows, :] = (_rms(o) * g_ref[...]).astype(o_ref.dtype)

    def run(p):
        tiles = ((qa_ref, stats_a, p), (qb_ref, stats_b, n_q - 1 - p))
        for _, stats, _ in tiles:
            for m_ref, l_ref, acc_ref in stats:
                m_ref[...] = jnp.full(m_ref.shape, NEG, F32)
                l_ref[...] = jnp.zeros(l_ref.shape, F32)
                acc_ref[...] = jnp.zeros(acc_ref.shape, F32)
        pending = [scores(q_ref, 0) for q_ref, _, _ in tiles]
        for j in range(n_q - p):
            for idx, (q_ref, stats, diag) in enumerate(tiles):
                if j > diag:
                    continue
                s_pair = pending[idx]
                if j < diag:
                    pending[idx] = scores(q_ref, j + 1)
                accumulate(stats, j, s_pair, masked=(j == diag))
        finish(stats_a, slice(0, tq))
        finish(stats_b, slice(tq, 2 * tq))

    for p in range(n_q // 2):
        pl.when(pair == p)(functools.partial(run, p))


def _attn_tile_pos(tile, nq):
    b, qt = tile // nq, tile % nq
    return b * nq + jnp.where(qt < nq // 2, 2 * qt, 2 * (nq - 1 - qt) + 1)


def _diff_attn(proj, lam, gain, *, batch, seq, tq):
    t = batch * seq
    nq = seq // tq
    assert nq % 2 == 0
    width = 2 * DA_QK_DIM
    k_blk0 = DA_HEADS
    v_blk0 = 2 * DA_HEADS
    kern = functools.partial(_diff_attn_kernel, tq=tq, n_q=nq)
    stat = [pltpu.VMEM((tq, 1), F32), pltpu.VMEM((tq, 1), F32), pltpu.VMEM((tq, DA_V_DIM), F32)]
    return pl.pallas_call(
        kern,
        grid_spec=pltpu.PrefetchScalarGridSpec(
            num_scalar_prefetch=1,
            grid=(batch, DA_HEADS, nq // 2),
            in_specs=[
                pl.BlockSpec((tq, width), lambda b, h, p, lam: (b * nq + p, h)),
                pl.BlockSpec((tq, width), lambda b, h, p, lam: (b * nq + nq - 1 - p, h)),
                pl.BlockSpec((seq, width), lambda b, h, p, lam: (b, k_blk0 + h)),
                pl.BlockSpec((seq, width), lambda b, h, p, lam: (b, v_blk0 + h)),
                pl.BlockSpec((1, DA_V_DIM), lambda b, h, p, lam: (0, 0)),
            ],
            out_specs=pl.BlockSpec((2 * tq, DA_V_DIM), lambda b, h, p, lam: (b * (nq // 2) + p, h)),
            scratch_shapes=stat * 4,
        ),
        out_shape=jax.ShapeDtypeStruct((t, DA_HEADS * DA_V_DIM), BF16),
        compiler_params=pltpu.CompilerParams(
            dimension_semantics=("parallel", "parallel", "arbitrary"), vmem_limit_bytes=VMEM_LIMIT),
        name="diff_attn",
    )(lam, proj, proj, proj, proj, gain)


def _s5_kernel(u_ref, bd_ref, ar_ref, ai_ref, cd_ref, d_ref, o_ref, xs_ref, us_ref, ys_ref, *, seq, rows):
    ns = SSM_CHUNK_STATE
    seg_len = seq // SCAN_SEGS
    n_row_blk = seq // rows
    steps = rows // SCAN_SEGS

    for seg in range(SCAN_SEGS):
        us_ref[pl.ds(seg, seg_len, stride=SCAN_SEGS), :] = (
            u_ref[seg * seg_len:(seg + 1) * seg_len, :].astype(F32))

    def in_map(r):
        rs = slice(r * rows, (r + 1) * rows)
        xs_ref[rs, :] = _dot(us_ref[rs, :].astype(BF16), bd_ref[...])

    def out_map(r):
        rs = slice(r * rows, (r + 1) * rows)
        y = _dot(xs_ref[rs, :].astype(BF16), cd_ref[...]) + d_ref[...] * us_ref[rs, :]
        ys_ref[rs, :] = jax.nn.gelu(y)
        for seg in range(SCAN_SEGS):
            t0 = seg * seg_len + r * steps
            o_ref[t0:t0 + steps, :] = (
                ys_ref[pl.ds(r * rows + seg, steps, stride=SCAN_SEGS), :].astype(o_ref.dtype))

    ar = jnp.broadcast_to(ar_ref[...], (SCAN_SEGS, ns))
    ai = jnp.broadcast_to(ai_ref[...], (SCAN_SEGS, ns))

    def advance(t, sr, si):
        ts = slice(t * SCAN_SEGS, (t + 1) * SCAN_SEGS)
        return ar * sr - ai * si + xs_ref[ts, 0:ns], ar * si + ai * sr + xs_ref[ts, ns:2 * ns]

    in_map(0)
    fr = fi = jnp.zeros((SCAN_SEGS, ns), F32)
    for r in range(n_row_blk):
        if r + 1 < n_row_blk:
            in_map(r + 1)
        for t in range(r * steps, (r + 1) * steps):
            fr, fi = advance(t, fr, fi)

    pr, pi = ar, ai
    for _ in range(int(math.log2(seg_len))):
        pr, pi = pr * pr - pi * pi, 2.0 * pr * pi
    seg = lax.broadcasted_iota(jnp.int32, (SCAN_SEGS, ns), 0)

    def shifted(x, k):
        return jnp.where(seg >= k, pltpu.roll(x, k, 0), 0.0)

    k = 1
    while k < SCAN_SEGS:
        gr, gi = shifted(fr, k), shifted(fi, k)
        fr, fi = fr + pr * gr - pi * gi, fi + pr * gi + pi * gr
        pr, pi = pr * pr - pi * pi, 2.0 * pr * pi
        k *= 2
    sr, si = shifted(fr, 1), shifted(fi, 1)

    for r in range(n_row_blk):
        for t in range(r * steps, (r + 1) * steps):
            sr, si = advance(t, sr, si)
            ts = slice(t * SCAN_SEGS, (t + 1) * SCAN_SEGS)
            xs_ref[ts, 0:ns] = sr
            xs_ref[ts, ns:2 * ns] = si
        if r >= 1:
            out_map(r - 1)
    out_map(n_row_blk - 1)


def _s5(proj, u_col0, d_ssm, bd, a_re, a_im, cd, d_skip, *, batch, seq, rows=256):
    n_chunks = d_ssm // LANES
    u_blk0 = u_col0 // LANES
    kern = functools.partial(_s5_kernel, seq=seq, rows=rows)
    return pl.pallas_call(
        kern,
        grid=(batch, n_chunks),
        in_specs=[
            pl.BlockSpec((seq, LANES), lambda b, c: (b, u_blk0 + c)),
            pl.BlockSpec((None, LANES, 2 * SSM_CHUNK_STATE), lambda b, c: (c, 0, 0)),
            pl.BlockSpec((None, 1, SSM_CHUNK_STATE), lambda b, c: (c, 0, 0)),
            pl.BlockSpec((None, 1, SSM_CHUNK_STATE), lambda b, c: (c, 0, 0)),
            pl.BlockSpec((None, 2 * SSM_CHUNK_STATE, LANES), lambda b, c: (c, 0, 0)),
            pl.BlockSpec((None, 1, LANES), lambda b, c: (c, 0, 0)),
        ],
        out_specs=pl.BlockSpec((seq, LANES), lambda b, c: (b, c)),
        out_shape=jax.ShapeDtypeStruct((batch * seq, d_ssm), BF16),
        scratch_shapes=[pltpu.VMEM((seq, 2 * SSM_CHUNK_STATE), F32),
                        pltpu.VMEM((seq, LANES), F32), pltpu.VMEM((seq, LANES), F32)],
        compiler_params=pltpu.CompilerParams(
            dimension_semantics=("parallel", "parallel"), vmem_limit_bytes=VMEM_LIMIT),
        name="s5_scan",
    )(proj, bd, a_re, a_im, cd, d_skip)


def _s5_params(lam_re, lam_im, log_dt, b_re, b_im, c_re, c_im, d_skip):
    g = lam_re.shape[0]
    nc = g // SSM_CHUNK_GROUPS
    lr = jnp.minimum(lam_re.astype(F32), -1e-4)
    li = lam_im.astype(F32)
    dt = jnp.exp(log_dt.astype(F32))[:, None]
    mag = jnp.exp(lr * dt)
    lb_re, lb_im = mag * jnp.cos(li * dt), mag * jnp.sin(li * dt)
    den = lr * lr + li * li
    coef_re = ((lb_re - 1.0) * lr + lb_im * li) / den
    coef_im = (lb_im * lr - (lb_re - 1.0) * li) / den
    br, bi = b_re.astype(F32), b_im.astype(F32)
    bb_re = coef_re[..., None] * br - coef_im[..., None] * bi
    bb_im = coef_re[..., None] * bi + coef_im[..., None] * br
    eye = jnp.eye(SSM_CHUNK_GROUPS, dtype=F32)

    def pack_in(bb):
        bb = bb.reshape(nc, SSM_CHUNK_GROUPS, SSM_STATE, SSM_GROUP)
        return jnp.einsum('cgph,gk->cghkp', bb, eye).reshape(nc, LANES, SSM_CHUNK_STATE)

    def pack_out(cc):
        cc = cc.astype(F32).reshape(nc, SSM_CHUNK_GROUPS, SSM_GROUP, SSM_STATE)
        return jnp.einsum('cghp,gk->ckpgh', cc, eye).reshape(nc, SSM_CHUNK_STATE, LANES)

    bd = jnp.concatenate([pack_in(bb_re), pack_in(bb_im)], axis=-1).astype(BF16)
    cd = jnp.concatenate([pack_out(c_re), -pack_out(c_im)], axis=1).astype(BF16)
    a_re = lb_re.reshape(nc, 1, SSM_CHUNK_STATE)
    a_im = lb_im.reshape(nc, 1, SSM_CHUNK_STATE)
    dd = d_skip.astype(F32).reshape(nc, 1, LANES)
    return bd, a_re, a_im, cd, dd


def _mix_out_kernel(a_ref, y_ref, x_ref, gw_ref, gb_ref, nw_ref, wo_ref, o_ref):
    d_attn = a_ref.shape[1]
    y = y_ref[...]
    gate = _dot(y, gw_ref[...]) + gb_ref[...]
    s = y.astype(F32) * jax.nn.sigmoid(gate)
    sn = (_rms(s) * nw_ref[...]).astype(BF16)
    acc = _dot(a_ref[...], wo_ref[0:d_attn, :]) + _dot(sn, wo_ref[d_attn:, :])
    o_ref[...] = x_ref[...] + acc


def _mix_out(a, y, x, glu_w, glu_b, norm_w, w_out, *, tm, attn_tiles):
    t, d = x.shape
    d_attn, d_ssm = a.shape[1], y.shape[1]
    const = lambda i: (0, 0)
    return pl.pallas_call(
        _mix_out_kernel,
        grid=(t // tm,),
        in_specs=[
            pl.BlockSpec((tm, d_attn), lambda i: (_attn_tile_pos(i, attn_tiles), 0)),
            pl.BlockSpec((tm, d_ssm), lambda i: (i, 0)),
            pl.BlockSpec((tm, d), lambda i: (i, 0)),
            _resident((d_ssm, d_ssm), const),
            _resident((1, d_ssm), const),
            _resident((1, d_ssm), const),
            _resident((d, d), const),
        ],
        out_specs=pl.BlockSpec((tm, d), lambda i: (i, 0)),
        out_shape=jax.ShapeDtypeStruct((t, d), F32),
        compiler_params=pltpu.CompilerParams(
            dimension_semantics=("parallel",), vmem_limit_bytes=VMEM_LIMIT),
        name="mix_out",
    )(a, y, x, glu_w, glu_b.reshape(1, d_ssm).astype(F32), norm_w.reshape(1, d_ssm).astype(F32), w_out)


def _xattn_route_kernel(q_ref, k_ref, v_ref, h_ref, xo_ref, nw_ref, rhi_ref, rlo_ref, rb_ref,
                        h2_ref, hn_ref, eid_ref, wts_ref):
    d = h_ref.shape[1]
    hd = d // X_HEADS
    h2 = h_ref[...]
    for h in range(X_HEADS):
        sl = slice(h * hd, (h + 1) * hd)
        s = _dot_nt(q_ref[:, sl], k_ref[:, sl])
        p = jnp.exp(s - jnp.max(s, axis=-1, keepdims=True))
        p = p * (1.0 / jnp.sum(p, axis=-1, keepdims=True))
        o = _dot(p.astype(BF16), v_ref[:, sl]).astype(BF16)
        h2 = h2 + _dot(o, xo_ref[sl, :])
    h2_ref[...] = h2
    hn = _rms(h2) * nw_ref[...]
    hn_ref[...] = _pack_halves(hn)

    hi = hn.astype(BF16)
    lo = (hn - hi.astype(F32)).astype(BF16)
    hi_both = _dot(hi, jnp.concatenate([rhi_ref[...], rlo_ref[...]], axis=1))
    logits = (hi_both[:, :ROUTE_LANES] + hi_both[:, ROUTE_LANES:] + _dot(lo, rhi_ref[...])) + rb_ref[...]

    lane = lax.broadcasted_iota(jnp.int32, logits.shape, 1)
    big = jnp.int32(ROUTE_LANES)

    def first_lane(cond):
        return jnp.min(jnp.where(cond, lane, big), axis=-1, keepdims=True)

    c_mask = lane < MOE_GROUPS
    lc = jnp.where(c_mask, logits, NEG)
    mc = jnp.max(lc, axis=-1, keepdims=True)
    ec = jnp.exp(lc - mc)
    p_c = ec / jnp.sum(ec, axis=-1, keepdims=True)
    p_grp = jnp.max(p_c, axis=-1, keepdims=True)
    grp = first_lane(c_mask & (p_c == p_grp))
    f_lo = MOE_GROUPS + grp * EXP_PER_GROUP
    f_mask = (lane >= f_lo) & (lane < f_lo + EXP_PER_GROUP)
    lf = jnp.where(f_mask, logits, NEG)
    mf = jnp.max(lf, axis=-1, keepdims=True)
    ef = jnp.exp(lf - mf)
    pf = ef / jnp.sum(ef, axis=-1, keepdims=True)
    v1 = jnp.max(jnp.where(f_mask, pf, -1.0), axis=-1, keepdims=True)
    i1 = first_lane(f_mask & (pf == v1))
    rest = f_mask & (lane != i1)
    v2 = jnp.max(jnp.where(rest, pf, -1.0), axis=-1, keepdims=True)
    i2 = first_lane(rest & (pf == v2))
    tot = v1 + v2
    w1 = v1 / tot * p_grp
    w2 = v2 / tot * p_grp
    eid_ref[...] = jnp.where(lane == 0, i1 - MOE_GROUPS, jnp.where(lane == 1, i2 - MOE_GROUPS, 0))
    wts_ref[...] = jnp.where(lane == 0, w1, jnp.where(lane == 1, w2, 0.0))


def _xattn_route(q, kv, h1, xo_w, norm_w, r_hi, r_lo, r_b, *, batch, seq, mem_len, tm):
    t, d = h1.shape
    n = seq // tm
    const = lambda b, i: (0, 0)
    row = lambda b, i: (b * n + i, 0)
    return pl.pallas_call(
        _xattn_route_kernel,
        grid=(batch, n),
        in_specs=[
            pl.BlockSpec((tm, d), row),
            pl.BlockSpec((mem_len, d), lambda b, i: (b, 0)),
            pl.BlockSpec((mem_len, d), lambda b, i: (b, 1)),
            pl.BlockSpec((tm, d), row),
            _resident((d, d), const),
            _resident((1, d), const),
            _resident((d, ROUTE_LANES), const),
            _resident((d, ROUTE_LANES), const),
            _resident((1, ROUTE_LANES), const),
        ],
        out_specs=[
            pl.BlockSpec((tm, d), row),
            pl.BlockSpec((tm, d // 2), row),
            pl.BlockSpec((tm, ROUTE_LANES), row),
            pl.BlockSpec((tm, ROUTE_LANES), row),
        ],
        out_shape=[
            jax.ShapeDtypeStruct((t, d), F32),
            jax.ShapeDtypeStruct((t, d // 2), jnp.uint32),
            jax.ShapeDtypeStruct((t, ROUTE_LANES), jnp.int32),
            jax.ShapeDtypeStruct((t, ROUTE_LANES), F32),
        ],
        compiler_params=pltpu.CompilerParams(
            dimension_semantics=("parallel", "parallel"), vmem_limit_bytes=VMEM_LIMIT),
        name="xattn_route",
    )(q, kv, kv, h1, xo_w, norm_w.reshape(1, d).astype(F32), r_hi, r_lo, r_b)


def _moe_kernel(be_ref, par_ref, first_ref, nxt_ref, base_ref, nval_ref, tok_ref, nu_ref,
                hn_hbm, wg_hbm, wu_hbm, wd_hbm, o_ref,
                xbuf, wgb, wub, wdb, gsem, wsem):
    b = pl.program_id(0)
    n_used = nu_ref[0]

    def weight_copies(e, slot):
        copies = []
        for hbm, buf in ((wg_hbm, wgb), (wu_hbm, wub), (wd_hbm, wdb)):
            rows = hbm.shape[1] // WEIGHT_DMA_CHUNKS
            for c in range(WEIGHT_DMA_CHUNKS):
                sl = pl.ds(c * rows, rows)
                copies.append(pltpu.make_async_copy(hbm.at[e, sl], buf.at[slot, sl], wsem.at[slot]))
        return copies

    def groups(blk):
        return (nval_ref[blk] + ROW_GROUP - 1) // ROW_GROUP

    def start_gather(blk, slot):
        base = base_ref[blk]

        def body(g, carry):
            for r in range(ROW_GROUP):
                tok = tok_ref[base + g * ROW_GROUP + r]
                pltpu.make_async_copy(hn_hbm.at[pl.ds(tok, 1)], xbuf.at[slot, g, pl.ds(r, 1)],
                                      gsem.at[slot]).start()
            return carry
        lax.fori_loop(0, groups(blk), body, 0)

    def wait_gather(blk, slot):
        filled = xbuf.at[slot, pl.ds(0, groups(blk))]
        pltpu.make_async_copy(filled, filled, gsem.at[slot]).wait()

    @pl.when(b == 0)
    def _():
        xbuf[...] = jnp.zeros(xbuf.shape, xbuf.dtype)
        for c in weight_copies(be_ref[0], par_ref[0]):
            c.start()

        @pl.when(nu_ref[1] >= 0)
        def _():
            for c in weight_copies(nu_ref[1], 1):
                c.start()
        start_gather(0, 0)

    @pl.when(b < n_used)
    def _():
        slot = b % 2
        wslot = par_ref[b]
        is_first = first_ref[b] == 1

        @pl.when(is_first & (nxt_ref[b] >= 0))
        def _():
            for c in weight_copies(nxt_ref[b], (wslot + WEIGHT_SLOTS - 1) % WEIGHT_SLOTS):
                c.start()

        @pl.when(b + 1 < n_used)
        def _():
            start_gather(b + 1, 1 - slot)

        @pl.when(is_first)
        def _():
            for c in weight_copies(0, wslot):
                c.wait()

        wait_gather(b, slot)
        half = xbuf.shape[-1]
        x_hi, x_lo = (v.astype(BF16) for v in _unpack_halves(xbuf[slot].reshape(MOE_BLOCK, half)))

        def up(w):
            return _dot(x_hi, w[wslot, 0:half, :].astype(BF16)) + _dot(x_lo, w[wslot, half:, :].astype(BF16))

        mid = (jax.nn.silu(up(wgb)) * up(wub)).astype(BF16)
        o_ref[...] = _pack_halves(_dot(mid, wdb[wslot].astype(BF16)))

    @pl.when(b >= n_used)
    def _():
        o_ref[...] = jnp.zeros(o_ref.shape, o_ref.dtype)


def _moe_experts(hn_packed, w_gate, w_up, w_down, meta, tok_sorted, n_used):
    d, d_ff = w_gate.shape[1:]
    half = hn_packed.shape[1]
    blk_exp, par, first, nxt, base, nval = meta
    n_blocks = blk_exp.shape[0]
    any_spec = pl.BlockSpec(memory_space=pl.ANY)
    return pl.pallas_call(
        _moe_kernel,
        grid_spec=pltpu.PrefetchScalarGridSpec(
            num_scalar_prefetch=8,
            grid=(n_blocks,),
            in_specs=[any_spec, any_spec, any_spec, any_spec],
            out_specs=pl.BlockSpec((MOE_BLOCK, half), lambda b, *_: (b, 0)),
            scratch_shapes=[
                pltpu.VMEM((2, MOE_BLOCK // ROW_GROUP, ROW_GROUP, half), jnp.uint32),
                pltpu.VMEM((WEIGHT_SLOTS, d, d_ff), F32),
                pltpu.VMEM((WEIGHT_SLOTS, d, d_ff), F32),
                pltpu.VMEM((WEIGHT_SLOTS, d_ff, d), F32),
                pltpu.SemaphoreType.DMA((2,)),
                pltpu.SemaphoreType.DMA((WEIGHT_SLOTS,)),
            ],
        ),
        out_shape=jax.ShapeDtypeStruct((n_blocks * MOE_BLOCK, half), jnp.uint32),
        compiler_params=pltpu.CompilerParams(
            dimension_semantics=("arbitrary",), vmem_limit_bytes=VMEM_LIMIT),
        name="moe_experts",
    )(blk_exp, par, first, nxt, base, nval, tok_sorted, n_used, hn_packed, w_gate, w_up, w_down)


def _combine_kernel(pos_ref, y_hbm, h_ref, w_ref, o_ref, ybuf, sem, *, tm):
    i = pl.program_id(0)
    n_groups = tm // ROW_GROUP

    def start_gather(tile, slot):
        base = tile * (tm * TOP_K_FINE)

        def body(g, carry):
            for r in range(ROW_GROUP):
                for k in range(TOP_K_FINE):
                    row = pos_ref[base + (g * ROW_GROUP + r) * TOP_K_FINE + k]
                    pltpu.make_async_copy(y_hbm.at[pl.ds(row, 1)], ybuf.at[slot, k, g, pl.ds(r, 1)],
                                          sem.at[slot]).start()
            return carry
        lax.fori_loop(0, n_groups, body, 0)

    @pl.when(i == 0)
    def _():
        start_gather(0, 0)

    @pl.when(i + 1 < pl.num_programs(0))
    def _():
        start_gather(i + 1, (i + 1) % 2)

    slot = i % 2
    pltpu.make_async_copy(ybuf.at[slot], ybuf.at[slot], sem.at[slot]).wait()
    w = w_ref[...]
    half = ybuf.shape[-1]
    y0 = _unpack_halves(ybuf[slot, 0].reshape(tm, half))
    y1 = _unpack_halves(ybuf[slot, 1].reshape(tm, half))
    for c in range(2):
        cols = slice(c * half, (c + 1) * half)
        o_ref[:, cols] = h_ref[:, cols] + (w[:, 0:1] * y0[c] + w[:, 1:2] * y1[c])


def _combine(y, h2, wts, pos, *, tm):
    t, d = h2.shape
    kern = functools.partial(_combine_kernel, tm=tm)
    return pl.pallas_call(
        kern,
        grid_spec=pltpu.PrefetchScalarGridSpec(
            num_scalar_prefetch=1,
            grid=(t // tm,),
            in_specs=[
                pl.BlockSpec(memory_space=pl.ANY),
                pl.BlockSpec((tm, d), lambda i, pos: (i, 0)),
                pl.BlockSpec((tm, ROUTE_LANES), lambda i, pos: (i, 0)),
            ],
            out_specs=pl.BlockSpec((tm, d), lambda i, pos: (i, 0)),
            scratch_shapes=[
                pltpu.VMEM((2, TOP_K_FINE, tm // ROW_GROUP, ROW_GROUP, y.shape[1]), jnp.uint32),
                pltpu.SemaphoreType.DMA((2,)),
            ],
        ),
        out_shape=jax.ShapeDtypeStruct((t, d), F32),
        compiler_params=pltpu.CompilerParams(
            dimension_semantics=("arbitrary",), vmem_limit_bytes=VMEM_LIMIT),
        name="moe_combine",
    )(pos, y, h2, wts)


def _lookup(table, idx):
    sel = idx[:, None] == jnp.arange(table.shape[0], dtype=jnp.int32)[None, :]
    return jnp.sum(jnp.where(sel, table[None, :], 0), axis=1).astype(jnp.int32)


def _dispatch(eid, n_tokens):
    n_assign = n_tokens * TOP_K_FINE
    experts = jnp.arange(N_EXPERTS, dtype=jnp.int32)
    e_flat = eid.reshape(n_assign)
    a_ids = jnp.arange(n_assign, dtype=jnp.int32)
    e_s, order = lax.sort_key_val(e_flat, a_ids)
    counts = jnp.sum((e_flat[:, None] == experts[None, :]).astype(jnp.int32), axis=0)
    starts = jnp.cumsum(counts) - counts
    nb = (counts + MOE_BLOCK - 1) // MOE_BLOCK
    blk_end = jnp.cumsum(nb)
    blk_start = blk_end - nb
    n_used = blk_end[-1]
    n_blocks = (n_assign + N_EXPERTS * (MOE_BLOCK - 1)) // MOE_BLOCK
    b_ids = jnp.arange(n_blocks, dtype=jnp.int32)
    used = b_ids < n_used
    blk_exp = jnp.minimum(jnp.sum((blk_end[None, :] <= b_ids[:, None]).astype(jnp.int32), axis=1),
                          N_EXPERTS - 1)
    j = b_ids - _lookup(blk_start, blk_exp)
    base = jnp.where(used, _lookup(starts, blk_exp) + j * MOE_BLOCK, 0)
    nval = jnp.where(used, jnp.clip(_lookup(counts, blk_exp) - j * MOE_BLOCK, 0, MOE_BLOCK), 0)
    first = (used & (j == 0)).astype(jnp.int32)
    active = counts > 0
    arank = jnp.cumsum(active.astype(jnp.int32)) - 1
    n_act = arank[-1] + 1
    by_rank = jnp.sum(jnp.where(active[None, :] & (arank[None, :] == experts[:, None]), experts[None, :], 0), axis=1)
    rank = _lookup(arank, blk_exp)
    par = rank % WEIGHT_SLOTS
    ahead = rank + (WEIGHT_SLOTS - 1)
    nxt = jnp.where(ahead < n_act, _lookup(by_rank, jnp.minimum(ahead, N_EXPERTS - 1)), -1)
    second = jnp.where(n_act > 1, by_rank[1], -1)
    meta = tuple(v.astype(jnp.int32) for v in (blk_exp, par, first, nxt, base, nval))
    row_sorted = a_ids + _lookup(blk_start * MOE_BLOCK - starts, e_s)
    _, pos = lax.sort_key_val(order, row_sorted)
    tok_sorted = jnp.concatenate([lax.shift_right_logical(order, 1), jnp.zeros((ROW_GROUP,), jnp.int32)])
    return meta, tok_sorted, pos, jnp.stack([n_used, second]).astype(jnp.int32)


def kernel(x, mem, norm1_w, w_in, q_norm_w, k_norm_w, lambda_q1, lambda_k1, lambda_q2, lambda_k2, subln_w, ssm_lambda_re, ssm_lambda_im, ssm_log_dt, ssm_b_re, ssm_b_im, ssm_c_re, ssm_c_im, ssm_d, ssm_glu_w, ssm_glu_b, ssm_out_norm_w, w_out, norm2_w, mem_norm_w, xq_w, xkv_w, xq_norm_w, xk_norm_w, xo_w, norm3_w, router_coarse_w, router_coarse_b, router_fine_w, router_fine_b, expert_w_gate, expert_w_up, expert_w_down):
    batch, seq, d = x.shape
    mem_len = mem.shape[1]
    t = batch * seq
    depth = norm1_w.shape[0]
    d_attn = DA_HEADS * DA_V_DIM
    d_ssm = d - d_attn
    qk_cols = DA_HEADS * 2 * DA_QK_DIM
    x_hd = d // X_HEADS
    h = x.reshape(t, d)
    mem2 = mem.reshape(batch * mem_len, d)

    for l in range(depth):
        lam_init = 0.8 - 0.6 * math.exp(-0.3 * l)
        lam = (jnp.exp(jnp.sum(lambda_q1[l].astype(F32) * lambda_k1[l].astype(F32)))
               - jnp.exp(jnp.sum(lambda_q2[l].astype(F32) * lambda_k2[l].astype(F32)))
               + lam_init).reshape(1)

        n_rep = qk_cols // DA_QK_DIM
        in_gain = jnp.concatenate([
            jnp.tile(q_norm_w[l].astype(F32) * (DA_QK_DIM ** -0.5 * LOG2E), n_rep),
            jnp.tile(k_norm_w[l].astype(F32), n_rep),
            jnp.ones((d_attn + d_ssm,), F32)])
        proj = _norm_matmul(h, norm1_w[l], w_in[l].astype(BF16), in_gain,
                            n_norm_cols=2 * qk_cols, chunk=DA_QK_DIM, tm=1024, tn=1024, name="in_proj")
        sub_gain = (subln_w[l].astype(F32) * (1.0 - lam_init)).reshape(1, DA_V_DIM)
        a = _diff_attn(proj, lam, sub_gain, batch=batch, seq=seq, tq=ATTN_TILE)

        bd, a_re, a_im, cd, dd = _s5_params(ssm_lambda_re[l], ssm_lambda_im[l], ssm_log_dt[l],
                                            ssm_b_re[l], ssm_b_im[l], ssm_c_re[l], ssm_c_im[l], ssm_d[l])
        y = _s5(proj, 2 * qk_cols + d_attn, d_ssm, bd, a_re, a_im, cd, dd, batch=batch, seq=seq)
        h = _mix_out(a, y, h, ssm_glu_w[l].astype(BF16), ssm_glu_b[l], ssm_out_norm_w[l],
                     w_out[l].astype(BF16), tm=ATTN_TILE, attn_tiles=seq // ATTN_TILE)

        kv_gain = jnp.concatenate([jnp.tile(xk_norm_w[l].astype(F32), X_HEADS), jnp.ones((d,), F32)])
        kv = _norm_matmul(mem2, mem_norm_w[l], xkv_w[l].astype(BF16), kv_gain,
                          n_norm_cols=d, chunk=x_hd, tm=512, tn=512, name="kv_proj")
        q_gain = jnp.tile(xq_norm_w[l].astype(F32) * (x_hd ** -0.5), X_HEADS)
        q = _norm_matmul(h, norm2_w[l], xq_w[l].astype(BF16), q_gain,
                         n_norm_cols=d, chunk=x_hd, tm=1024, tn=1024, name="xq_proj")
        r_w = jnp.concatenate([router_coarse_w[l].astype(F32), router_fine_w[l].astype(F32)], axis=1)
        r_w = jnp.pad(r_w, ((0, 0), (0, ROUTE_LANES - r_w.shape[1])))
        r_hi = r_w.astype(BF16)
        r_lo = (r_w - r_hi.astype(F32)).astype(BF16)
        r_b = jnp.concatenate([router_coarse_b[l].astype(F32), router_fine_b[l].astype(F32)])
        r_b = jnp.pad(r_b, (0, ROUTE_LANES - r_b.shape[0])).reshape(1, ROUTE_LANES)
        h2, hn3, eid, wts = _xattn_route(q, kv, h, xo_w[l].astype(BF16), norm3_w[l], r_hi, r_lo, r_b,
                                         batch=batch, seq=seq, mem_len=mem_len, tm=512)

        meta, tok_sorted, pos, n_used = _dispatch(eid[:, :TOP_K_FINE], t)
        y = _moe_experts(hn3, expert_w_gate[l], expert_w_up[l], expert_w_down[l], meta, tok_sorted, n_used)
        h = _combine(y, h2, wts, pos, tm=256)

    return h.reshape(batch, seq, d)
```

```python
import functools
import math

import jax
import jax.numpy as jnp
from jax import lax
from jax.experimental import pallas as pl
from jax.experimental.pallas import tpu as pltpu

F32 = jnp.float32
BF16 = jnp.bfloat16

EPS = 1e-6
DA_HEADS = 4
DA_QK_DIM = 128
DA_V_DIM = 256
SSM_GROUP = 16
SSM_STATE = 64
X_HEADS = 4
MOE_GROUPS = 8
EXP_PER_GROUP = 8
N_EXPERTS = MOE_GROUPS * EXP_PER_GROUP
TOP_K_FINE = 2

LANES = 128
SUBLANES = 8
MXU_TILE = 256
VMEM_LIMIT = 56 * 1024 * 1024
NEG = -1e30
LOG2E = math.log2(math.e)

SSM_CHUNK_GROUPS = LANES // SSM_GROUP
SSM_CHUNK_STATE = SSM_CHUNK_GROUPS * SSM_STATE
SCAN_SEGS = SUBLANES
ATTN_TILE = 512
MOE_BLOCK = 256
ROW_GROUP = SUBLANES
WEIGHT_DMA_CHUNKS = 4
ROUTE_LANES = LANES


def _rms(x, eps=EPS):
    return x * lax.rsqrt(jnp.mean(x * x, axis=-1, keepdims=True) + eps)


def _dot(a, b):
    return jnp.dot(a, b, preferred_element_type=F32)


def _dot_nt(a, b):
    return lax.dot_general(a, b, (((1,), (1,)), ((), ())), preferred_element_type=F32)


def _pack_halves(x):
    n = x.shape[1] // 2
    hi = lax.bitcast_convert_type(x[:, :n].astype(BF16).astype(F32), jnp.uint32)
    lo = lax.bitcast_convert_type(x[:, n:].astype(BF16).astype(F32), jnp.uint32)
    return hi | lax.shift_right_logical(lo, jnp.uint32(16))


def _unpack_halves(p):
    hi = lax.bitcast_convert_type(p & jnp.uint32(0xFFFF0000), F32)
    lo = lax.bitcast_convert_type(lax.shift_left(p, jnp.uint32(16)), F32)
    return hi, lo


def _resident(shape, index_map):
    return pl.BlockSpec(shape, index_map, pipeline_mode=pl.Buffered(1))


def _norm_matmul_kernel(x_ref, nw_ref, w_ref, g_ref, o_ref, xn_ref, *, n_norm_tiles, chunk):
    j = pl.program_id(1)

    @pl.when(j == 0)
    def _():
        x = x_ref[...].astype(F32)
        xn_ref[...] = (_rms(x) * nw_ref[...]).astype(BF16)

    normed = j < n_norm_tiles
    tn = w_ref.shape[1]
    sub = max(chunk, MXU_TILE)
    for s in range(tn // sub):
        w = w_ref[:, s * sub:(s + 1) * sub]
        if w.dtype != BF16:
            w = w.astype(BF16)
        acc = _dot(xn_ref[...], w)
        for c in range(sub // chunk):
            lo = s * sub + c * chunk
            a = acc[:, c * chunk:(c + 1) * chunk]
            inv = lax.rsqrt(jnp.mean(a * a, axis=-1, keepdims=True) + EPS)
            scale = jnp.where(normed, inv, 1.0)
            o_ref[:, lo:lo + chunk] = (a * scale * g_ref[:, lo:lo + chunk]).astype(o_ref.dtype)


def _norm_matmul(x, norm_w, w, gain, *, n_norm_cols, chunk, tm, tn, name):
    m, k = x.shape
    n = w.shape[1]
    assert m % tm == 0 and n % tn == 0 and tn % max(chunk, MXU_TILE) == 0 and n_norm_cols % tn == 0
    kern = functools.partial(_norm_matmul_kernel, n_norm_tiles=n_norm_cols // tn, chunk=chunk)
    return pl.pallas_call(
        kern,
        grid=(m // tm, n // tn),
        in_specs=[
            pl.BlockSpec((tm, k), lambda i, j: (i, 0)),
            pl.BlockSpec((1, k), lambda i, j: (0, 0)),
            pl.BlockSpec((k, tn), lambda i, j: (0, j)),
            pl.BlockSpec((1, tn), lambda i, j: (0, j)),
        ],
        out_specs=pl.BlockSpec((tm, tn), lambda i, j: (i, j)),
        out_shape=jax.ShapeDtypeStruct((m, n), BF16),
        scratch_shapes=[pltpu.VMEM((tm, k), BF16)],
        compiler_params=pltpu.CompilerParams(
            dimension_semantics=("parallel", "arbitrary"), vmem_limit_bytes=VMEM_LIMIT),
        name=name,
    )(x, norm_w.reshape(1, k).astype(F32), w, gain.reshape(1, n).astype(F32))


def _diff_attn_kernel(lam_ref, qa_ref, qb_ref, k_ref, v_ref, g_ref, o_ref, *stat_refs, tq, n_q):
    pair = pl.program_id(2)
    stats_a = (stat_refs[0:3], stat_refs[3:6])
    stats_b = (stat_refs[6:9], stat_refs[9:12])

    def scores(q_ref, j):
        return tuple(_dot_nt(q_ref[:, c * DA_QK_DIM:(c + 1) * DA_QK_DIM],
                             k_ref[j * tq:(j + 1) * tq, c * DA_QK_DIM:(c + 1) * DA_QK_DIM])
                     for c in range(2))

    def accumulate(stats, j, s_pair, masked):
        for s, (m_ref, l_ref, acc_ref) in zip(s_pair, stats):
            if masked:
                row = lax.broadcasted_iota(jnp.int32, s.shape, 0)
                col = lax.broadcasted_iota(jnp.int32, s.shape, 1)
                s = jnp.where(col <= row, s, NEG)
            m_old = m_ref[...]
            m_new = jnp.maximum(m_old, jnp.max(s, axis=-1, keepdims=True))
            p = jnp.exp2(s - m_new)
            alpha = jnp.exp2(m_old - m_new)
            l_ref[...] = alpha * l_ref[...] + jnp.sum(p, axis=-1, keepdims=True)
            acc_ref[...] = alpha * acc_ref[...] + _dot(p.astype(BF16), v_ref[j * tq:(j + 1) * tq, :])
            m_ref[...] = m_new

    def finish(stats, rows):
        (_, l1, acc1), (_, l2, acc2) = stats
        o = acc1[...] / l1[...] - lam_ref[0] * (acc2[...] / l2[...])
        o_ref[rows, :] = (_rms(o) * g_ref[...]).astype(o_ref.dtype)

    def run(p):
        tiles = ((qa_ref, stats_a, p), (qb_ref, stats_b, n_q - 1 - p))
        for _, stats, _ in tiles:
            for m_ref, l_ref, acc_ref in stats:
                m_ref[...] = jnp.full(m_ref.shape, NEG, F32)
                l_ref[...] = jnp.zeros(l_ref.shape, F32)
                acc_ref[...] = jnp.zeros(acc_ref.shape, F32)
        pending = [scores(q_ref, 0) for q_ref, _, _ in tiles]
        for j in range(n_q - p):
            for idx, (q_ref, stats, diag) in enumerate(tiles):
                if j > diag:
                    continue
                s_pair = pending[idx]
                if j < diag:
                    pending[idx] = scores(q_ref, j + 1)
                accumulate(stats, j, s_pair, masked=(j == diag))
        finish(stats_a, slice(0, tq))
        finish(stats_b, slice(tq, 2 * tq))

    for p in range(n_q // 2):
        pl.when(pair == p)(functools.partial(run, p))


def _attn_tile_pos(tile, nq):
    b, qt = tile // nq, tile % nq
    return b * nq + jnp.where(qt < nq // 2, 2 * qt, 2 * (nq - 1 - qt) + 1)


def _diff_attn(proj, lam, gain, *, batch, seq, tq):
    t = batch * seq
    nq = seq // tq
    assert nq % 2 == 0
    width = 2 * DA_QK_DIM
    k_blk0 = DA_HEADS
    v_blk0 = 2 * DA_HEADS
    kern = functools.partial(_diff_attn_kernel, tq=tq, n_q=nq)
    stat = [pltpu.VMEM((tq, 1), F32), pltpu.VMEM((tq, 1), F32), pltpu.VMEM((tq, DA_V_DIM), F32)]
    return pl.pallas_call(
        kern,
        grid_spec=pltpu.PrefetchScalarGridSpec(
            num_scalar_prefetch=1,
            grid=(batch, DA_HEADS, nq // 2),
            in_specs=[
                pl.BlockSpec((tq, width), lambda b, h, p, lam: (b * nq + p, h)),
                pl.BlockSpec((tq, width), lambda b, h, p, lam: (b * nq + nq - 1 - p, h)),
                pl.BlockSpec((seq, width), lambda b, h, p, lam: (b, k_blk0 + h)),
                pl.BlockSpec((seq, width), lambda b, h, p, lam: (b, v_blk0 + h)),
                pl.BlockSpec((1, DA_V_DIM), lambda b, h, p, lam: (0, 0)),
            ],
            out_specs=pl.BlockSpec((2 * tq, DA_V_DIM), lambda b, h, p, lam: (b * (nq // 2) + p, h)),
            scratch_shapes=stat * 4,
        ),
        out_shape=jax.ShapeDtypeStruct((t, DA_HEADS * DA_V_DIM), BF16),
        compiler_params=pltpu.CompilerParams(
            dimension_semantics=("parallel", "parallel", "arbitrary"), vmem_limit_bytes=VMEM_LIMIT),
        name="diff_attn",
    )(lam, proj, proj, proj, proj, gain)


def _s5_kernel(u_ref, bd_ref, ar_ref, ai_ref, cd_ref, d_ref, o_ref, xs_ref, us_ref, ys_ref, *, seq, rows):
    ns = SSM_CHUNK_STATE
    seg_len = seq // SCAN_SEGS
    n_row_blk = seq // rows
    steps = rows // SCAN_SEGS

    for seg in range(SCAN_SEGS):
        us_ref[pl.ds(seg, seg_len, stride=SCAN_SEGS), :] = (
            u_ref[seg * seg_len:(seg + 1) * seg_len, :].astype(F32))

    def in_map(r):
        rs = slice(r * rows, (r + 1) * rows)
        xs_ref[rs, :] = _dot(us_ref[rs, :].astype(BF16), bd_ref[...])

    def out_map(r):
        rs = slice(r * rows, (r + 1) * rows)
        y = _dot(xs_ref[rs, :].astype(BF16), cd_ref[...]) + d_ref[...] * us_ref[rs, :]
        ys_ref[rs, :] = jax.nn.gelu(y)
        for seg in range(SCAN_SEGS):
            t0 = seg * seg_len + r * steps
            o_ref[t0:t0 + steps, :] = (
                ys_ref[pl.ds(r * rows + seg, steps, stride=SCAN_SEGS), :].astype(o_ref.dtype))

    ar = jnp.broadcast_to(ar_ref[...], (SCAN_SEGS, ns))
    ai = jnp.broadcast_to(ai_ref[...], (SCAN_SEGS, ns))

    def advance(t, sr, si):
        ts = slice(t * SCAN_SEGS, (t + 1) * SCAN_SEGS)
        return ar * sr - ai * si + xs_ref[ts, 0:ns], ar * si + ai * sr + xs_ref[ts, ns:2 * ns]

    in_map(0)
    fr = fi = jnp.zeros((SCAN_SEGS, ns), F32)
    for r in range(n_row_blk):
        if r + 1 < n_row_blk:
            in_map(r + 1)
        for t in range(r * steps, (r + 1) * steps):
            fr, fi = advance(t, fr, fi)

    pr, pi = ar, ai
    for _ in range(int(math.log2(seg_len))):
        pr, pi = pr * pr - pi * pi, 2.0 * pr * pi
    seg = lax.broadcasted_iota(jnp.int32, (SCAN_SEGS, ns), 0)

    def shifted(x, k):
        return jnp.where(seg >= k, pltpu.roll(x, k, 0), 0.0)

    k = 1
    while k < SCAN_SEGS:
        gr, gi = shifted(fr, k), shifted(fi, k)
        fr, fi = fr + pr * gr - pi * gi, fi + pr * gi + pi * gr
        pr, pi = pr * pr - pi * pi, 2.0 * pr * pi
        k *= 2
    sr, si = shifted(fr, 1), shifted(fi, 1)

    for r in range(n_row_blk):
        for t in range(r * steps, (r + 1) * steps):
            sr, si = advance(t, sr, si)
            ts = slice(t * SCAN_SEGS, (t + 1) * SCAN_SEGS)
            xs_ref[ts, 0:ns] = sr
            xs_ref[ts, ns:2 * ns] = si
        if r >= 1:
            out_map(r - 1)
    out_map(n_row_blk - 1)


def _s5(proj, u_col0, d_ssm, bd, a_re, a_im, cd, d_skip, *, batch, seq, rows=256):
    n_chunks = d_ssm // LANES
    u_blk0 = u_col0 // LANES
    kern = functools.partial(_s5_kernel, seq=seq, rows=rows)
    return pl.pallas_call(
        kern,
        grid=(batch, n_chunks),
        in_specs=[
            pl.BlockSpec((seq, LANES), lambda b, c: (b, u_blk0 + c)),
            pl.BlockSpec((None, LANES, 2 * SSM_CHUNK_STATE), lambda b, c: (c, 0, 0)),
            pl.BlockSpec((None, 1, SSM_CHUNK_STATE), lambda b, c: (c, 0, 0)),
            pl.BlockSpec((None, 1, SSM_CHUNK_STATE), lambda b, c: (c, 0, 0)),
            pl.BlockSpec((None, 2 * SSM_CHUNK_STATE, LANES), lambda b, c: (c, 0, 0)),
            pl.BlockSpec((None, 1, LANES), lambda b, c: (c, 0, 0)),
        ],
        out_specs=pl.BlockSpec((seq, LANES), lambda b, c: (b, c)),
        out_shape=jax.ShapeDtypeStruct((batch * seq, d_ssm), BF16),
        scratch_shapes=[pltpu.VMEM((seq, 2 * SSM_CHUNK_STATE), F32),
                        pltpu.VMEM((seq, LANES), F32), pltpu.VMEM((seq, LANES), F32)],
        compiler_params=pltpu.CompilerParams(
            dimension_semantics=("parallel", "parallel"), vmem_limit_bytes=VMEM_LIMIT),
        name="s5_scan",
    )(proj, bd, a_re, a_im, cd, d_skip)


def _s5_params(lam_re, lam_im, log_dt, b_re, b_im, c_re, c_im, d_skip):
    g = lam_re.shape[0]
    nc = g // SSM_CHUNK_GROUPS
    lr = jnp.minimum(lam_re.astype(F32), -1e-4)
    li = lam_im.astype(F32)
    dt = jnp.exp(log_dt.astype(F32))[:, None]
    mag = jnp.exp(lr * dt)
    lb_re, lb_im = mag * jnp.cos(li * dt), mag * jnp.sin(li * dt)
    den = lr * lr + li * li
    coef_re = ((lb_re - 1.0) * lr + lb_im * li) / den
    coef_im = (lb_im * lr - (lb_re - 1.0) * li) / den
    br, bi = b_re.astype(F32), b_im.astype(F32)
    bb_re = coef_re[..., None] * br - coef_im[..., None] * bi
    bb_im = coef_re[..., None] * bi + coef_im[..., None] * br
    eye = jnp.eye(SSM_CHUNK_GROUPS, dtype=F32)

    def pack_in(bb):
        bb = bb.reshape(nc, SSM_CHUNK_GROUPS, SSM_STATE, SSM_GROUP)
        return jnp.einsum('cgph,gk->cghkp', bb, eye).reshape(nc, LANES, SSM_CHUNK_STATE)

    def pack_out(cc):
        cc = cc.astype(F32).reshape(nc, SSM_CHUNK_GROUPS, SSM_GROUP, SSM_STATE)
        return jnp.einsum('cghp,gk->ckpgh', cc, eye).reshape(nc, SSM_CHUNK_STATE, LANES)

    bd = jnp.concatenate([pack_in(bb_re), pack_in(bb_im)], axis=-1).astype(BF16)
    cd = jnp.concatenate([pack_out(c_re), -pack_out(c_im)], axis=1).astype(BF16)
    a_re = lb_re.reshape(nc, 1, SSM_CHUNK_STATE)
    a_im = lb_im.reshape(nc, 1, SSM_CHUNK_STATE)
    dd = d_skip.astype(F32).reshape(nc, 1, LANES)
    return bd, a_re, a_im, cd, dd


def _mix_out_kernel(a_ref, y_ref, x_ref, gw_ref, gb_ref, nw_ref, wo_ref, o_ref):
    d_attn = a_ref.shape[1]
    y = y_ref[...]
    gate = _dot(y, gw_ref[...]) + gb_ref[...]
    s = y.astype(F32) * jax.nn.sigmoid(gate)
    sn = (_rms(s) * nw_ref[...]).astype(BF16)
    acc = _dot(a_ref[...], wo_ref[0:d_attn, :]) + _dot(sn, wo_ref[d_attn:, :])
    o_ref[...] = x_ref[...] + acc


def _mix_out(a, y, x, glu_w, glu_b, norm_w, w_out, *, tm, attn_tiles):
    t, d = x.shape
    d_attn, d_ssm = a.shape[1], y.shape[1]
    const = lambda i: (0, 0)
    return pl.pallas_call(
        _mix_out_kernel,
        grid=(t // tm,),
        in_specs=[
            pl.BlockSpec((tm, d_attn), lambda i: (_attn_tile_pos(i, attn_tiles), 0)),
            pl.BlockSpec((tm, d_ssm), lambda i: (i, 0)),
            pl.BlockSpec((tm, d), lambda i: (i, 0)),
            _resident((d_ssm, d_ssm), const),
            _resident((1, d_ssm), const),
            _resident((1, d_ssm), const),
            _resident((d, d), const),
        ],
        out_specs=pl.BlockSpec((tm, d), lambda i: (i, 0)),
        out_shape=jax.ShapeDtypeStruct((t, d), F32),
        compiler_params=pltpu.CompilerParams(
            dimension_semantics=("parallel",), vmem_limit_bytes=VMEM_LIMIT),
        name="mix_out",
    )(a, y, x, glu_w, glu_b.reshape(1, d_ssm).astype(F32), norm_w.reshape(1, d_ssm).astype(F32), w_out)


def _xattn_route_kernel(q_ref, k_ref, v_ref, h_ref, xo_ref, nw_ref, rhi_ref, rlo_ref, rb_ref,
                        h2_ref, hn_ref, eid_ref, wts_ref):
    d = h_ref.shape[1]
    hd = d // X_HEADS
    h2 = h_ref[...]
    for h in range(X_HEADS):
        sl = slice(h * hd, (h + 1) * hd)
        s = _dot_nt(q_ref[:, sl], k_ref[:, sl])
        p = jnp.exp(s - jnp.max(s, axis=-1, keepdims=True))
        p = p * (1.0 / jnp.sum(p, axis=-1, keepdims=True))
        o = _dot(p.astype(BF16), v_ref[:, sl]).astype(BF16)
        h2 = h2 + _dot(o, xo_ref[sl, :])
    h2_ref[...] = h2
    hn = _rms(h2) * nw_ref[...]
    hn_ref[...] = _pack_halves(hn)

    hi = hn.astype(BF16)
    lo = (hn - hi.astype(F32)).astype(BF16)
    hi_both = _dot(hi, jnp.concatenate([rhi_ref[...], rlo_ref[...]], axis=1))
    logits = (hi_both[:, :ROUTE_LANES] + hi_both[:, ROUTE_LANES:] + _dot(lo, rhi_ref[...])) + rb_ref[...]

    lane = lax.broadcasted_iota(jnp.int32, logits.shape, 1)
    big = jnp.int32(ROUTE_LANES)

    def first_lane(cond):
        return jnp.min(jnp.where(cond, lane, big), axis=-1, keepdims=True)

    c_mask = lane < MOE_GROUPS
    lc = jnp.where(c_mask, logits, NEG)
    mc = jnp.max(lc, axis=-1, keepdims=True)
    ec = jnp.exp(lc - mc)
    p_c = ec / jnp.sum(ec, axis=-1, keepdims=True)
    p_grp = jnp.max(p_c, axis=-1, keepdims=True)
    grp = first_lane(c_mask & (p_c == p_grp))
    f_lo = MOE_GROUPS + grp * EXP_PER_GROUP
    f_mask = (lane >= f_lo) & (lane < f_lo + EXP_PER_GROUP)
    lf = jnp.where(f_mask, logits, NEG)
    mf = jnp.max(lf, axis=-1, keepdims=True)
    ef = jnp.exp(lf - mf)
    pf = ef / jnp.sum(ef, axis=-1, keepdims=True)
    v1 = jnp.max(jnp.where(f_mask, pf, -1.0), axis=-1, keepdims=True)
    i1 = first_lane(f_mask & (pf == v1))
    rest = f_mask & (lane != i1)
    v2 = jnp.max(jnp.where(rest, pf, -1.0), axis=-1, keepdims=True)
    i2 = first_lane(rest & (pf == v2))
    tot = v1 + v2
    w1 = v1 / tot * p_grp
    w2 = v2 / tot * p_grp
    eid_ref[...] = jnp.where(lane == 0, i1 - MOE_GROUPS, jnp.where(lane == 1, i2 - MOE_GROUPS, 0))
    wts_ref[...] = jnp.where(lane == 0, w1, jnp.where(lane == 1, w2, 0.0))


def _xattn_route(q, kv, h1, xo_w, norm_w, r_hi, r_lo, r_b, *, batch, seq, mem_len, tm):
    t, d = h1.shape
    n = seq // tm
    const = lambda b, i: (0, 0)
    row = lambda b, i: (b * n + i, 0)
    return pl.pallas_call(
        _xattn_route_kernel,
        grid=(batch, n),
        in_specs=[
            pl.BlockSpec((tm, d), row),
            pl.BlockSpec((mem_len, d), lambda b, i: (b, 0)),
            pl.BlockSpec((mem_len, d), lambda b, i: (b, 1)),
            pl.BlockSpec((tm, d), row),
            _resident((d, d), const),
            _resident((1, d), const),
            _resident((d, ROUTE_LANES), const),
            _resident((d, ROUTE_LANES), const),
            _resident((1, ROUTE_LANES), const),
        ],
        out_specs=[
            pl.BlockSpec((tm, d), row),
            pl.BlockSpec((tm, d // 2), row),
            pl.BlockSpec((tm, ROUTE_LANES), row),
            pl.BlockSpec((tm, ROUTE_LANES), row),
        ],
        out_shape=[
            jax.ShapeDtypeStruct((t, d), F32),
            jax.ShapeDtypeStruct((t, d // 2), jnp.uint32),
            jax.ShapeDtypeStruct((t, ROUTE_LANES), jnp.int32),
            jax.ShapeDtypeStruct((t, ROUTE_LANES), F32),
        ],
        compiler_params=pltpu.CompilerParams(
            dimension_semantics=("parallel", "parallel"), vmem_limit_bytes=VMEM_LIMIT),
        name="xattn_route",
    )(q, kv, kv, h1, xo_w, norm_w.reshape(1, d).astype(F32), r_hi, r_lo, r_b)


def _moe_kernel(be_ref, par_ref, first_ref, nxt_ref, base_ref, nval_ref, tok_ref, nu_ref,
                hn_hbm, wg_hbm, wu_hbm, wd_hbm, o_ref,
                xbuf, wgb, wub, wdb, gsem, wsem):
    b = pl.program_id(0)
    n_used = nu_ref[0]

    def weight_copies(e, slot):
        copies = []
        for hbm, buf in ((wg_hbm, wgb), (wu_hbm, wub), (wd_hbm, wdb)):
            rows = hbm.shape[1] // WEIGHT_DMA_CHUNKS
            for c in range(WEIGHT_DMA_CHUNKS):
                sl = pl.ds(c * rows, rows)
                copies.append(pltpu.make_async_copy(hbm.at[e, sl], buf.at[slot, sl], wsem.at[slot]))
        return copies

    def groups(blk):
        return (nval_ref[blk] + ROW_GROUP - 1) // ROW_GROUP

    def start_gather(blk, slot):
        base = base_ref[blk]

        def body(g, carry):
            for r in range(ROW_GROUP):
                tok = tok_ref[base + g * ROW_GROUP + r]
                pltpu.make_async_copy(hn_hbm.at[pl.ds(tok, 1)], xbuf.at[slot, g, pl.ds(r, 1)],
                                      gsem.at[slot]).start()
            return carry
        lax.fori_loop(0, groups(blk), body, 0)

    def wait_gather(blk, slot):
        filled = xbuf.at[slot, pl.ds(0, groups(blk))]
        pltpu.make_async_copy(filled, filled, gsem.at[slot]).wait()

    @pl.when(b == 0)
    def _():
        xbuf[...] = jnp.zeros(xbuf.shape, xbuf.dtype)
        for c in weight_copies(be_ref[0], par_ref[0]):
            c.start()
        start_gather(0, 0)

    @pl.when(b < n_used)
    def _():
        slot = b % 2
        wslot = par_ref[b]
        is_first = first_ref[b] == 1

        @pl.when(is_first & (nxt_ref[b] >= 0))
        def _():
            for c in weight_copies(nxt_ref[b], 1 - wslot):
                c.start()

        @pl.when(b + 1 < n_used)
        def _():
            start_gather(b + 1, 1 - slot)

        @pl.when(is_first)
        def _():
            for c in weight_copies(0, wslot):
                c.wait()

        wait_gather(b, slot)
        half = xbuf.shape[-1]
        x_hi, x_lo = (v.astype(BF16) for v in _unpack_halves(xbuf[slot].reshape(MOE_BLOCK, half)))

        def up(w):
            return _dot(x_hi, w[wslot, 0:half, :].astype(BF16)) + _dot(x_lo, w[wslot, half:, :].astype(BF16))

        mid = (jax.nn.silu(up(wgb)) * up(wub)).astype(BF16)
        o_ref[...] = _pack_halves(_dot(mid, wdb[wslot].astype(BF16)))

    @pl.when(b >= n_used)
    def _():
        o_ref[...] = jnp.zeros(o_ref.shape, o_ref.dtype)


def _moe_experts(hn_packed, w_gate, w_up, w_down, meta, tok_sorted, n_used):
    d, d_ff = w_gate.shape[1:]
    half = hn_packed.shape[1]
    blk_exp, par, first, nxt, base, nval = meta
    n_blocks = blk_exp.shape[0]
    any_spec = pl.BlockSpec(memory_space=pl.ANY)
    return pl.pallas_call(
        _moe_kernel,
        grid_spec=pltpu.PrefetchScalarGridSpec(
            num_scalar_prefetch=8,
            grid=(n_blocks,),
            in_specs=[any_spec, any_spec, any_spec, any_spec],
            out_specs=pl.BlockSpec((MOE_BLOCK, half), lambda b, *_: (b, 0)),
            scratch_shapes=[
                pltpu.VMEM((2, MOE_BLOCK // ROW_GROUP, ROW_GROUP, half), jnp.uint32),
                pltpu.VMEM((2, d, d_ff), F32),
                pltpu.VMEM((2, d, d_ff), F32),
                pltpu.VMEM((2, d_ff, d), F32),
                pltpu.SemaphoreType.DMA((2,)),
                pltpu.SemaphoreType.DMA((2,)),
            ],
        ),
        out_shape=jax.ShapeDtypeStruct((n_blocks * MOE_BLOCK, half), jnp.uint32),
        compiler_params=pltpu.CompilerParams(
            dimension_semantics=("arbitrary",), vmem_limit_bytes=VMEM_LIMIT),
        name="moe_experts",
    )(blk_exp, par, first, nxt, base, nval, tok_sorted, n_used, hn_packed, w_gate, w_up, w_down)


def _combine_kernel(pos_ref, y_hbm, h_ref, w_ref, o_ref, ybuf0, ybuf1, sem, *, tm):
    i = pl.program_id(0)
    last = pl.num_programs(0) - 1
    ybufs = (ybuf0, ybuf1)

    def start_gather(tile, slot):
        base = tile * (tm * TOP_K_FINE)
        for g in range(tm // ROW_GROUP):
            for r in range(ROW_GROUP):
                for k in range(TOP_K_FINE):
                    row = pos_ref[base + (g * ROW_GROUP + r) * TOP_K_FINE + k]
                    pltpu.make_async_copy(y_hbm.at[pl.ds(row, 1)], ybufs[slot].at[k, g, pl.ds(r, 1)],
                                          sem.at[slot]).start()

    def wait_gather(slot):
        pltpu.make_async_copy(ybufs[slot], ybufs[slot], sem.at[slot]).wait()

    @pl.when(i == 0)
    def _():
        start_gather(0, 0)

    def run(slot):
        wait_gather(slot)
        start_gather(jnp.minimum(i + 1, last), 1 - slot)
        w = w_ref[...]
        half = ybufs[slot].shape[-1]
        y0 = _unpack_halves(ybufs[slot][0].reshape(tm, half))
        y1 = _unpack_halves(ybufs[slot][1].reshape(tm, half))
        for c in range(2):
            cols = slice(c * half, (c + 1) * half)
            o_ref[:, cols] = h_ref[:, cols] + (w[:, 0:1] * y0[c] + w[:, 1:2] * y1[c])

        @pl.when(i == last)
        def _():
            wait_gather(1 - slot)

    for slot in range(2):
        pl.when(i % 2 == slot)(functools.partial(run, slot))


def _combine(y, h2, wts, pos, *, tm):
    t, d = h2.shape
    kern = functools.partial(_combine_kernel, tm=tm)
    return pl.pallas_call(
        kern,
        grid_spec=pltpu.PrefetchScalarGridSpec(
            num_scalar_prefetch=1,
            grid=(t // tm,),
            in_specs=[
                pl.BlockSpec(memory_space=pl.ANY),
                pl.BlockSpec((tm, d), lambda i, pos: (i, 0)),
                pl.BlockSpec((tm, ROUTE_LANES), lambda i, pos: (i, 0)),
            ],
            out_specs=pl.BlockSpec((tm, d), lambda i, pos: (i, 0)),
            scratch_shapes=[
                pltpu.VMEM((TOP_K_FINE, tm // ROW_GROUP, ROW_GROUP, y.shape[1]), jnp.uint32),
                pltpu.VMEM((TOP_K_FINE, tm // ROW_GROUP, ROW_GROUP, y.shape[1]), jnp.uint32),
                pltpu.SemaphoreType.DMA((2,)),
            ],
        ),
        out_shape=jax.ShapeDtypeStruct((t, d), F32),
        compiler_params=pltpu.CompilerParams(
            dimension_semantics=("arbitrary",), vmem_limit_bytes=VMEM_LIMIT),
        name="moe_combine",
    )(pos, y, h2, wts)


def _lookup(table, idx):
    sel = idx[:, None] == jnp.arange(table.shape[0], dtype=jnp.int32)[None, :]
    return jnp.sum(jnp.where(sel, table[None, :], 0), axis=1).astype(jnp.int32)


def _dispatch(eid, n_tokens):
    n_assign = n_tokens * TOP_K_FINE
    experts = jnp.arange(N_EXPERTS, dtype=jnp.int32)
    e_flat = eid.reshape(n_assign)
    a_ids = jnp.arange(n_assign, dtype=jnp.int32)
    e_s, order = lax.sort_key_val(e_flat, a_ids)
    counts = jnp.sum((e_flat[:, None] == experts[None, :]).astype(jnp.int32), axis=0)
    starts = jnp.cumsum(counts) - counts
    nb = (counts + MOE_BLOCK - 1) // MOE_BLOCK
    blk_end = jnp.cumsum(nb)
    blk_start = blk_end - nb
    n_used = blk_end[-1]
    n_blocks = (n_assign + N_EXPERTS * (MOE_BLOCK - 1)) // MOE_BLOCK
    b_ids = jnp.arange(n_blocks, dtype=jnp.int32)
    used = b_ids < n_used
    blk_exp = jnp.minimum(jnp.sum((blk_end[None, :] <= b_ids[:, None]).astype(jnp.int32), axis=1),
                          N_EXPERTS - 1)
    j = b_ids - _lookup(blk_start, blk_exp)
    base = jnp.where(used, _lookup(starts, blk_exp) + j * MOE_BLOCK, 0)
    nval = jnp.where(used, jnp.clip(_lookup(counts, blk_exp) - j * MOE_BLOCK, 0, MOE_BLOCK), 0)
    first = (used & (j == 0)).astype(jnp.int32)
    active = counts > 0
    par = _lookup(jnp.cumsum(active.astype(jnp.int32)) - 1, blk_exp) & 1
    later = lax.cummin(jnp.where(active, experts, N_EXPERTS), reverse=True)
    nxt_e = jnp.concatenate([later[1:], jnp.full((1,), N_EXPERTS, jnp.int32)])
    nxt = _lookup(jnp.where(nxt_e == N_EXPERTS, -1, nxt_e), blk_exp)
    meta = tuple(v.astype(jnp.int32) for v in (blk_exp, par, first, nxt, base, nval))
    row_sorted = a_ids + _lookup(blk_start * MOE_BLOCK - starts, e_s)
    _, pos = lax.sort_key_val(order, row_sorted)
    tok_sorted = jnp.concatenate([lax.shift_right_logical(order, 1), jnp.zeros((ROW_GROUP,), jnp.int32)])
    return meta, tok_sorted, pos, n_used.astype(jnp.int32).reshape(1)


def kernel(x, mem, norm1_w, w_in, q_norm_w, k_norm_w, lambda_q1, lambda_k1, lambda_q2, lambda_k2, subln_w, ssm_lambda_re, ssm_lambda_im, ssm_log_dt, ssm_b_re, ssm_b_im, ssm_c_re, ssm_c_im, ssm_d, ssm_glu_w, ssm_glu_b, ssm_out_norm_w, w_out, norm2_w, mem_norm_w, xq_w, xkv_w, xq_norm_w, xk_norm_w, xo_w, norm3_w, router_coarse_w, router_coarse_b, router_fine_w, router_fine_b, expert_w_gate, expert_w_up, expert_w_down):
    batch, seq, d = x.shape
    mem_len = mem.shape[1]
    t = batch * seq
    depth = norm1_w.shape[0]
    d_attn = DA_HEADS * DA_V_DIM
    d_ssm = d - d_attn
    qk_cols = DA_HEADS * 2 * DA_QK_DIM
    x_hd = d // X_HEADS
    h = x.reshape(t, d)
    mem2 = mem.reshape(batch * mem_len, d)

    for l in range(depth):
        lam_init = 0.8 - 0.6 * math.exp(-0.3 * l)
        lam = (jnp.exp(jnp.sum(lambda_q1[l].astype(F32) * lambda_k1[l].astype(F32)))
               - jnp.exp(jnp.sum(lambda_q2[l].astype(F32) * lambda_k2[l].astype(F32)))
               + lam_init).reshape(1)

        n_rep = qk_cols // DA_QK_DIM
        in_gain = jnp.concatenate([
            jnp.tile(q_norm_w[l].astype(F32) * (DA_QK_DIM ** -0.5 * LOG2E), n_rep),
            jnp.tile(k_norm_w[l].astype(F32), n_rep),
            jnp.ones((d_attn + d_ssm,), F32)])
        proj = _norm_matmul(h, norm1_w[l], w_in[l].astype(BF16), in_gain,
                            n_norm_cols=2 * qk_cols, chunk=DA_QK_DIM, tm=1024, tn=1024, name="in_proj")
        sub_gain = (subln_w[l].astype(F32) * (1.0 - lam_init)).reshape(1, DA_V_DIM)
        a = _diff_attn(proj, lam, sub_gain, batch=batch, seq=seq, tq=ATTN_TILE)

        bd, a_re, a_im, cd, dd = _s5_params(ssm_lambda_re[l], ssm_lambda_im[l], ssm_log_dt[l],
                                            ssm_b_re[l], ssm_b_im[l], ssm_c_re[l], ssm_c_im[l], ssm_d[l])
        y = _s5(proj, 2 * qk_cols + d_attn, d_ssm, bd, a_re, a_im, cd, dd, batch=batch, seq=seq)
        h = _mix_out(a, y, h, ssm_glu_w[l].astype(BF16), ssm_glu_b[l], ssm_out_norm_w[l],
                     w_out[l].astype(BF16), tm=ATTN_TILE, attn_tiles=seq // ATTN_TILE)

        kv_gain = jnp.concatenate([jnp.tile(xk_norm_w[l].astype(F32), X_HEADS), jnp.ones((d,), F32)])
        kv = _norm_matmul(mem2, mem_norm_w[l], xkv_w[l], kv_gain,
                          n_norm_cols=d, chunk=x_hd, tm=batch * mem_len, tn=512, name="kv_proj")
        q_gain = jnp.tile(xq_norm_w[l].astype(F32) * (x_hd ** -0.5), X_HEADS)
        q = _norm_matmul(h, norm2_w[l], xq_w[l].astype(BF16), q_gain,
                         n_norm_cols=d, chunk=x_hd, tm=1024, tn=1024, name="xq_proj")
        r_w = jnp.concatenate([router_coarse_w[l].astype(F32), router_fine_w[l].astype(F32)], axis=1)
        r_w = jnp.pad(r_w, ((0, 0), (0, ROUTE_LANES - r_w.shape[1])))
        r_hi = r_w.astype(BF16)
        r_lo = (r_w - r_hi.astype(F32)).astype(BF16)
        r_b = jnp.concatenate([router_coarse_b[l].astype(F32), router_fine_b[l].astype(F32)])
        r_b = jnp.pad(r_b, (0, ROUTE_LANES - r_b.shape[0])).reshape(1, ROUTE_LANES)
        h2, hn3, eid, wts = _xattn_route(q, kv, h, xo_w[l].astype(BF16), norm3_w[l], r_hi, r_lo, r_b,
                                         batch=batch, seq=seq, mem_len=mem_len, tm=512)

        meta, tok_sorted, pos, n_used = _dispatch(eid[:, :TOP_K_FINE], t)
        y = _moe_experts(hn3, expert_w_gate[l], expert_w_up[l], expert_w_down[l], meta, tok_sorted, n_used)
        h = _combine(y, h2, wts, pos, tm=256)

    return h.reshape(batch, seq, d)
```

```python
import functools
import math

import jax
import jax.numpy as jnp
from jax import lax
from jax.experimental import pallas as pl
from jax.experimental.pallas import tpu as pltpu

F32 = jnp.float32
BF16 = jnp.bfloat16

EPS = 1e-6
DA_HEADS = 4
DA_QK_DIM = 128
DA_V_DIM = 256
SSM_GROUP = 16
SSM_STATE = 64
X_HEADS = 4
MOE_GROUPS = 8
EXP_PER_GROUP = 8
N_EXPERTS = MOE_GROUPS * EXP_PER_GROUP
TOP_K_FINE = 2

LANES = 128
SUBLANES = 8
MXU_TILE = 256
VMEM_LIMIT = 56 * 1024 * 1024
NEG = -1e30
LOG2E = math.log2(math.e)

SSM_CHUNK_GROUPS = LANES // SSM_GROUP
SSM_CHUNK_STATE = SSM_CHUNK_GROUPS * SSM_STATE
SCAN_SEGS = SUBLANES
ATTN_TILE = 512
MOE_BLOCK = 256
ROW_GROUP = SUBLANES
WEIGHT_DMA_CHUNKS = 4
ROUTE_LANES = LANES


def _rms(x, eps=EPS):
    return x * lax.rsqrt(jnp.mean(x * x, axis=-1, keepdims=True) + eps)


def _dot(a, b):
    return jnp.dot(a, b, preferred_element_type=F32)


def _dot_nt(a, b):
    return lax.dot_general(a, b, (((1,), (1,)), ((), ())), preferred_element_type=F32)


def _pack_halves(x):
    n = x.shape[1] // 2
    hi = lax.bitcast_convert_type(x[:, :n].astype(BF16).astype(F32), jnp.uint32)
    lo = lax.bitcast_convert_type(x[:, n:].astype(BF16).astype(F32), jnp.uint32)
    return hi | lax.shift_right_logical(lo, jnp.uint32(16))


def _unpack_halves(p):
    hi = lax.bitcast_convert_type(p & jnp.uint32(0xFFFF0000), F32)
    lo = lax.bitcast_convert_type(lax.shift_left(p, jnp.uint32(16)), F32)
    return hi, lo


def _resident(shape, index_map):
    return pl.BlockSpec(shape, index_map, pipeline_mode=pl.Buffered(1))


def _norm_matmul_kernel(x_ref, nw_ref, w_ref, g_ref, o_ref, xn_ref, *, n_norm_tiles, chunk):
    j = pl.program_id(1)

    @pl.when(j == 0)
    def _():
        x = x_ref[...].astype(F32)
        xn_ref[...] = (_rms(x) * nw_ref[...]).astype(BF16)

    normed = j < n_norm_tiles
    tn = w_ref.shape[1]
    sub = max(chunk, MXU_TILE)
    for s in range(tn // sub):
        w = w_ref[:, s * sub:(s + 1) * sub]
        if w.dtype != BF16:
            w = w.astype(BF16)
        acc = _dot(xn_ref[...], w)
        for c in range(sub // chunk):
            lo = s * sub + c * chunk
            a = acc[:, c * chunk:(c + 1) * chunk]
            inv = lax.rsqrt(jnp.mean(a * a, axis=-1, keepdims=True) + EPS)
            scale = jnp.where(normed, inv, 1.0)
            o_ref[:, lo:lo + chunk] = (a * scale * g_ref[:, lo:lo + chunk]).astype(o_ref.dtype)


def _norm_matmul(x, norm_w, w, gain, *, n_norm_cols, chunk, tm, tn, name):
    m, k = x.shape
    n = w.shape[1]
    assert m % tm == 0 and n % tn == 0 and tn % max(chunk, MXU_TILE) == 0 and n_norm_cols % tn == 0
    kern = functools.partial(_norm_matmul_kernel, n_norm_tiles=n_norm_cols // tn, chunk=chunk)
    return pl.pallas_call(
        kern,
        grid=(m // tm, n // tn),
        in_specs=[
            pl.BlockSpec((tm, k), lambda i, j: (i, 0)),
            pl.BlockSpec((1, k), lambda i, j: (0, 0)),
            pl.BlockSpec((k, tn), lambda i, j: (0, j)),
            pl.BlockSpec((1, tn), lambda i, j: (0, j)),
        ],
        out_specs=pl.BlockSpec((tm, tn), lambda i, j: (i, j)),
        out_shape=jax.ShapeDtypeStruct((m, n), BF16),
        scratch_shapes=[pltpu.VMEM((tm, k), BF16)],
        compiler_params=pltpu.CompilerParams(
            dimension_semantics=("parallel", "arbitrary"), vmem_limit_bytes=VMEM_LIMIT),
        name=name,
    )(x, norm_w.reshape(1, k).astype(F32), w, gain.reshape(1, n).astype(F32))


def _diff_attn_kernel(lam_ref, qa_ref, qb_ref, k_ref, v_ref, g_ref, o_ref, *stat_refs, tq, n_q):
    pair = pl.program_id(2)
    stats_a = (stat_refs[0:3], stat_refs[3:6])
    stats_b = (stat_refs[6:9], stat_refs[9:12])

    def scores(q_ref, j):
        return tuple(_dot_nt(q_ref[:, c * DA_QK_DIM:(c + 1) * DA_QK_DIM],
                             k_ref[j * tq:(j + 1) * tq, c * DA_QK_DIM:(c + 1) * DA_QK_DIM])
                     for c in range(2))

    def accumulate(stats, j, s_pair, masked):
        for s, (m_ref, l_ref, acc_ref) in zip(s_pair, stats):
            if masked:
                row = lax.broadcasted_iota(jnp.int32, s.shape, 0)
                col = lax.broadcasted_iota(jnp.int32, s.shape, 1)
                s = jnp.where(col <= row, s, NEG)
            m_old = m_ref[...]
            m_new = jnp.maximum(m_old, jnp.max(s, axis=-1, keepdims=True))
            p = jnp.exp2(s - m_new)
            alpha = jnp.exp2(m_old - m_new)
            l_ref[...] = alpha * l_ref[...] + jnp.sum(p, axis=-1, keepdims=True)
            acc_ref[...] = alpha * acc_ref[...] + _dot(p.astype(BF16), v_ref[j * tq:(j + 1) * tq, :])
            m_ref[...] = m_new

    def finish(stats, rows):
        (_, l1, acc1), (_, l2, acc2) = stats
        o = acc1[...] / l1[...] - lam_ref[0] * (acc2[...] / l2[...])
        o_ref[rows, :] = (_rms(o) * g_ref[...]).astype(o_ref.dtype)

    def run(p):
        tiles = ((qa_ref, stats_a, p), (qb_ref, stats_b, n_q - 1 - p))
        for _, stats, _ in tiles:
            for m_ref, l_ref, acc_ref in stats:
                m_ref[...] = jnp.full(m_ref.shape, NEG, F32)
                l_ref[...] = jnp.zeros(l_ref.shape, F32)
                acc_ref[...] = jnp.zeros(acc_ref.shape, F32)
        pending = [scores(q_ref, 0) for q_ref, _, _ in tiles]
        for j in range(n_q - p):
            for idx, (q_ref, stats, diag) in enumerate(tiles):
                if j > diag:
                    continue
                s_pair = pending[idx]
                if j < diag:
                    pending[idx] = scores(q_ref, j + 1)
                accumulate(stats, j, s_pair, masked=(j == diag))
        finish(stats_a, slice(0, tq))
        finish(stats_b, slice(tq, 2 * tq))

    for p in range(n_q // 2):
        pl.when(pair == p)(functools.partial(run, p))


def _attn_tile_pos(tile, nq):
    b, qt = tile // nq, tile % nq
    return b * nq + jnp.where(qt < nq // 2, 2 * qt, 2 * (nq - 1 - qt) + 1)


def _diff_attn(proj, lam, gain, *, batch, seq, tq):
    t = batch * seq
    nq = seq // tq
    assert nq % 2 == 0
    width = 2 * DA_QK_DIM
    k_blk0 = DA_HEADS
    v_blk0 = 2 * DA_HEADS
    kern = functools.partial(_diff_attn_kernel, tq=tq, n_q=nq)
    stat = [pltpu.VMEM((tq, 1), F32), pltpu.VMEM((tq, 1), F32), pltpu.VMEM((tq, DA_V_DIM), F32)]
    return pl.pallas_call(
        kern,
        grid_spec=pltpu.PrefetchScalarGridSpec(
            num_scalar_prefetch=1,
            grid=(batch, DA_HEADS, nq // 2),
            in_specs=[
                pl.BlockSpec((tq, width), lambda b, h, p, lam: (b * nq + p, h)),
                pl.BlockSpec((tq, width), lambda b, h, p, lam: (b * nq + nq - 1 - p, h)),
                pl.BlockSpec((seq, width), lambda b, h, p, lam: (b, k_blk0 + h)),
                pl.BlockSpec((seq, width), lambda b, h, p, lam: (b, v_blk0 + h)),
                pl.BlockSpec((1, DA_V_DIM), lambda b, h, p, lam: (0, 0)),
            ],
            out_specs=pl.BlockSpec((2 * tq, DA_V_DIM), lambda b, h, p, lam: (b * (nq // 2) + p, h)),
            scratch_shapes=stat * 4,
        ),
        out_shape=jax.ShapeDtypeStruct((t, DA_HEADS * DA_V_DIM), BF16),
        compiler_params=pltpu.CompilerParams(
            dimension_semantics=("parallel", "parallel", "arbitrary"), vmem_limit_bytes=VMEM_LIMIT),
        name="diff_attn",
    )(lam, proj, proj, proj, proj, gain)


def _s5_kernel(u_ref, bd_ref, ar_ref, ai_ref, cd_ref, d_ref, o_ref, xs_ref, us_ref, ys_ref, *, seq, rows):
    ns = SSM_CHUNK_STATE
    seg_len = seq // SCAN_SEGS
    n_row_blk = seq // rows
    steps = rows // SCAN_SEGS

    for seg in range(SCAN_SEGS):
        us_ref[pl.ds(seg, seg_len, stride=SCAN_SEGS), :] = (
            u_ref[seg * seg_len:(seg + 1) * seg_len, :].astype(F32))

    def in_map(r):
        rs = slice(r * rows, (r + 1) * rows)
        xs_ref[rs, :] = _dot(us_ref[rs, :].astype(BF16), bd_ref[...])

    def out_map(r):
        rs = slice(r * rows, (r + 1) * rows)
        y = _dot(xs_ref[rs, :].astype(BF16), cd_ref[...]) + d_ref[...] * us_ref[rs, :]
        ys_ref[rs, :] = jax.nn.gelu(y)
        for seg in range(SCAN_SEGS):
            t0 = seg * seg_len + r * steps
            o_ref[t0:t0 + steps, :] = (
                ys_ref[pl.ds(r * rows + seg, steps, stride=SCAN_SEGS), :].astype(o_ref.dtype))

    ar = jnp.broadcast_to(ar_ref[...], (SCAN_SEGS, ns))
    ai = jnp.broadcast_to(ai_ref[...], (SCAN_SEGS, ns))

    def advance(t, sr, si):
        ts = slice(t * SCAN_SEGS, (t + 1) * SCAN_SEGS)
        return ar * sr - ai * si + xs_ref[ts, 0:ns], ar * si + ai * sr + xs_ref[ts, ns:2 * ns]

    in_map(0)
    fr = fi = jnp.zeros((SCAN_SEGS, ns), F32)
    for r in range(n_row_blk):
        if r + 1 < n_row_blk:
            in_map(r + 1)
        for t in range(r * steps, (r + 1) * steps):
            fr, fi = advance(t, fr, fi)

    pr, pi = ar, ai
    for _ in range(int(math.log2(seg_len))):
        pr, pi = pr * pr - pi * pi, 2.0 * pr * pi
    seg = lax.broadcasted_iota(jnp.int32, (SCAN_SEGS, ns), 0)

    def shifted(x, k):
        return jnp.where(seg >= k, pltpu.roll(x, k, 0), 0.0)

    k = 1
    while k < SCAN_SEGS:
        gr, gi = shifted(fr, k), shifted(fi, k)
        fr, fi = fr + pr * gr - pi * gi, fi + pr * gi + pi * gr
        pr, pi = pr * pr - pi * pi, 2.0 * pr * pi
        k *= 2
    sr, si = shifted(fr, 1), shifted(fi, 1)

    for r in range(n_row_blk):
        for t in range(r * steps, (r + 1) * steps):
            sr, si = advance(t, sr, si)
            ts = slice(t * SCAN_SEGS, (t + 1) * SCAN_SEGS)
            xs_ref[ts, 0:ns] = sr
            xs_ref[ts, ns:2 * ns] = si
        if r >= 1:
            out_map(r - 1)
    out_map(n_row_blk - 1)


def _s5(proj, u_col0, d_ssm, bd, a_re, a_im, cd, d_skip, *, batch, seq, rows=256):
    n_chunks = d_ssm // LANES
    u_blk0 = u_col0 // LANES
    kern = functools.partial(_s5_kernel, seq=seq, rows=rows)
    return pl.pallas_call(
        kern,
        grid=(batch, n_chunks),
        in_specs=[
            pl.BlockSpec((seq, LANES), lambda b, c: (b, u_blk0 + c)),
            pl.BlockSpec((None, LANES, 2 * SSM_CHUNK_STATE), lambda b, c: (c, 0, 0)),
            pl.BlockSpec((None, 1, SSM_CHUNK_STATE), lambda b, c: (c, 0, 0)),
            pl.BlockSpec((None, 1, SSM_CHUNK_STATE), lambda b, c: (c, 0, 0)),
            pl.BlockSpec((None, 2 * SSM_CHUNK_STATE, LANES), lambda b, c: (c, 0, 0)),
            pl.BlockSpec((None, 1, LANES), lambda b, c: (c, 0, 0)),
        ],
        out_specs=pl.BlockSpec((seq, LANES), lambda b, c: (b, c)),
        out_shape=jax.ShapeDtypeStruct((batch * seq, d_ssm), BF16),
        scratch_shapes=[pltpu.VMEM((seq, 2 * SSM_CHUNK_STATE), F32),
                        pltpu.VMEM((seq, LANES), F32), pltpu.VMEM((seq, LANES), F32)],
        compiler_params=pltpu.CompilerParams(
            dimension_semantics=("parallel", "parallel"), vmem_limit_bytes=VMEM_LIMIT),
        name="s5_scan",
    )(proj, bd, a_re, a_im, cd, d_skip)


def _s5_params(lam_re, lam_im, log_dt, b_re, b_im, c_re, c_im, d_skip):
    g = lam_re.shape[0]
    nc = g // SSM_CHUNK_GROUPS
    lr = jnp.minimum(lam_re.astype(F32), -1e-4)
    li = lam_im.astype(F32)
    dt = jnp.exp(log_dt.astype(F32))[:, None]
    mag = jnp.exp(lr * dt)
    lb_re, lb_im = mag * jnp.cos(li * dt), mag * jnp.sin(li * dt)
    den = lr * lr + li * li
    coef_re = ((lb_re - 1.0) * lr + lb_im * li) / den
    coef_im = (lb_im * lr - (lb_re - 1.0) * li) / den
    br, bi = b_re.astype(F32), b_im.astype(F32)
    bb_re = coef_re[..., None] * br - coef_im[..., None] * bi
    bb_im = coef_re[..., None] * bi + coef_im[..., None] * br
    eye = jnp.eye(SSM_CHUNK_GROUPS, dtype=F32)

    def pack_in(bb):
        bb = bb.reshape(nc, SSM_CHUNK_GROUPS, SSM_STATE, SSM_GROUP)
        return jnp.einsum('cgph,gk->cghkp', bb, eye).reshape(nc, LANES, SSM_CHUNK_STATE)

    def pack_out(cc):
        cc = cc.astype(F32).reshape(nc, SSM_CHUNK_GROUPS, SSM_GROUP, SSM_STATE)
        return jnp.einsum('cghp,gk->ckpgh', cc, eye).reshape(nc, SSM_CHUNK_STATE, LANES)

    bd = jnp.concatenate([pack_in(bb_re), pack_in(bb_im)], axis=-1).astype(BF16)
    cd = jnp.concatenate([pack_out(c_re), -pack_out(c_im)], axis=1).astype(BF16)
    a_re = lb_re.reshape(nc, 1, SSM_CHUNK_STATE)
    a_im = lb_im.reshape(nc, 1, SSM_CHUNK_STATE)
    dd = d_skip.astype(F32).reshape(nc, 1, LANES)
    return bd, a_re, a_im, cd, dd


def _mix_out_kernel(a_ref, y_ref, x_ref, gw_ref, gb_ref, nw_ref, wo_ref, o_ref):
    d_attn = a_ref.shape[1]
    y = y_ref[...]
    gate = _dot(y, gw_ref[...]) + gb_ref[...]
    s = y.astype(F32) * jax.nn.sigmoid(gate)
    sn = (_rms(s) * nw_ref[...]).astype(BF16)
    acc = _dot(a_ref[...], wo_ref[0:d_attn, :]) + _dot(sn, wo_ref[d_attn:, :])
    o_ref[...] = x_ref[...] + acc


def _mix_out(a, y, x, glu_w, glu_b, norm_w, w_out, *, tm, attn_tiles):
    t, d = x.shape
    d_attn, d_ssm = a.shape[1], y.shape[1]
    const = lambda i: (0, 0)
    return pl.pallas_call(
        _mix_out_kernel,
        grid=(t // tm,),
        in_specs=[
            pl.BlockSpec((tm, d_attn), lambda i: (_attn_tile_pos(i, attn_tiles), 0)),
            pl.BlockSpec((tm, d_ssm), lambda i: (i, 0)),
            pl.BlockSpec((tm, d), lambda i: (i, 0)),
            _resident((d_ssm, d_ssm), const),
            _resident((1, d_ssm), const),
            _resident((1, d_ssm), const),
            _resident((d, d), const),
        ],
        out_specs=pl.BlockSpec((tm, d), lambda i: (i, 0)),
        out_shape=jax.ShapeDtypeStruct((t, d), F32),
        compiler_params=pltpu.CompilerParams(
            dimension_semantics=("parallel",), vmem_limit_bytes=VMEM_LIMIT),
        name="mix_out",
    )(a, y, x, glu_w, glu_b.reshape(1, d_ssm).astype(F32), norm_w.reshape(1, d_ssm).astype(F32), w_out)


def _xattn_route_kernel(q_ref, k_ref, v_ref, h_ref, xo_ref, nw_ref, rhi_ref, rlo_ref, rb_ref,
                        h2_ref, hn_ref, eid_ref, wts_ref):
    d = h_ref.shape[1]
    hd = d // X_HEADS
    h2 = h_ref[...]
    for h in range(X_HEADS):
        sl = slice(h * hd, (h + 1) * hd)
        s = _dot_nt(q_ref[:, sl], k_ref[:, sl])
        p = jnp.exp(s - jnp.max(s, axis=-1, keepdims=True))
        p = p * (1.0 / jnp.sum(p, axis=-1, keepdims=True))
        o = _dot(p.astype(BF16), v_ref[:, sl]).astype(BF16)
        h2 = h2 + _dot(o, xo_ref[sl, :])
    h2_ref[...] = h2
    hn = _rms(h2) * nw_ref[...]
    hn_ref[...] = _pack_halves(hn)

    hi = hn.astype(BF16)
    lo = (hn - hi.astype(F32)).astype(BF16)
    hi_both = _dot(hi, jnp.concatenate([rhi_ref[...], rlo_ref[...]], axis=1))
    logits = (hi_both[:, :ROUTE_LANES] + hi_both[:, ROUTE_LANES:] + _dot(lo, rhi_ref[...])) + rb_ref[...]

    lt = logits.T
    idx = lax.broadcasted_iota(jnp.int32, (SUBLANES, lt.shape[1]), 0)

    def first_row(cond):
        return jnp.min(jnp.where(cond, idx, SUBLANES), axis=0, keepdims=True)

    def softmax_rows(x):
        e = jnp.exp(x - jnp.max(x, axis=0, keepdims=True))
        return e / jnp.sum(e, axis=0, keepdims=True)

    p_c = softmax_rows(lt[0:MOE_GROUPS, :])
    p_grp = jnp.max(p_c, axis=0, keepdims=True)
    grp = first_row(p_c == p_grp)
    lf = lt[MOE_GROUPS:MOE_GROUPS + EXP_PER_GROUP, :]
    for g in range(1, MOE_GROUPS):
        lo_row = MOE_GROUPS + g * EXP_PER_GROUP
        lf = jnp.where(grp == g, lt[lo_row:lo_row + EXP_PER_GROUP, :], lf)
    pf = softmax_rows(lf)
    v1 = jnp.max(pf, axis=0, keepdims=True)
    i1 = first_row(pf == v1)
    rest = idx != i1
    v2 = jnp.max(jnp.where(rest, pf, -1.0), axis=0, keepdims=True)
    i2 = first_row(rest & (pf == v2))
    tot = v1 + v2
    e1 = grp * EXP_PER_GROUP + i1
    e2 = grp * EXP_PER_GROUP + i2
    eid_ref[...] = jnp.where(idx == 0, e1, jnp.where(idx == 1, e2, 0))
    wts_ref[...] = jnp.where(idx == 0, v1 / tot * p_grp, jnp.where(idx == 1, v2 / tot * p_grp, 0.0))


def _xattn_route(q, kv, h1, xo_w, norm_w, r_hi, r_lo, r_b, *, batch, seq, mem_len, tm):
    t, d = h1.shape
    n = seq // tm
    const = lambda b, i: (0, 0)
    row = lambda b, i: (b * n + i, 0)
    return pl.pallas_call(
        _xattn_route_kernel,
        grid=(batch, n),
        in_specs=[
            pl.BlockSpec((tm, d), row),
            pl.BlockSpec((mem_len, d), lambda b, i: (b, 0)),
            pl.BlockSpec((mem_len, d), lambda b, i: (b, 1)),
            pl.BlockSpec((tm, d), row),
            _resident((d, d), const),
            _resident((1, d), const),
            _resident((d, ROUTE_LANES), const),
            _resident((d, ROUTE_LANES), const),
            _resident((1, ROUTE_LANES), const),
        ],
        out_specs=[
            pl.BlockSpec((tm, d), row),
            pl.BlockSpec((tm, d // 2), row),
            pl.BlockSpec((SUBLANES, tm), row),
            pl.BlockSpec((SUBLANES, tm), row),
        ],
        out_shape=[
            jax.ShapeDtypeStruct((t, d), F32),
            jax.ShapeDtypeStruct((t, d // 2), jnp.uint32),
            jax.ShapeDtypeStruct((t // tm * SUBLANES, tm), jnp.int32),
            jax.ShapeDtypeStruct((t // tm * SUBLANES, tm), F32),
        ],
        compiler_params=pltpu.CompilerParams(
            dimension_semantics=("parallel", "parallel"), vmem_limit_bytes=VMEM_LIMIT),
        name="xattn_route",
    )(q, kv, kv, h1, xo_w, norm_w.reshape(1, d).astype(F32), r_hi, r_lo, r_b)


def _moe_kernel(be_ref, par_ref, first_ref, nxt_ref, base_ref, nval_ref, tok_ref, nu_ref,
                hn_hbm, wg_hbm, wu_hbm, wd_hbm, o_ref,
                xbuf, wgb, wub, wdb, gsem, wsem):
    b = pl.program_id(0)
    n_used = nu_ref[0]

    def weight_copies(e, slot):
        copies = []
        for hbm, buf in ((wg_hbm, wgb), (wu_hbm, wub), (wd_hbm, wdb)):
            rows = hbm.shape[1] // WEIGHT_DMA_CHUNKS
            for c in range(WEIGHT_DMA_CHUNKS):
                sl = pl.ds(c * rows, rows)
                copies.append(pltpu.make_async_copy(hbm.at[e, sl], buf.at[slot, sl], wsem.at[slot]))
        return copies

    def groups(blk):
        return (nval_ref[blk] + ROW_GROUP - 1) // ROW_GROUP

    def start_gather(blk, slot):
        base = base_ref[blk]

        def body(g, carry):
            for r in range(ROW_GROUP):
                tok = tok_ref[base + g * ROW_GROUP + r]
                pltpu.make_async_copy(hn_hbm.at[pl.ds(tok, 1)], xbuf.at[slot, g, pl.ds(r, 1)],
                                      gsem.at[slot]).start()
            return carry
        lax.fori_loop(0, groups(blk), body, 0)

    def wait_gather(blk, slot):
        filled = xbuf.at[slot, pl.ds(0, groups(blk))]
        pltpu.make_async_copy(filled, filled, gsem.at[slot]).wait()

    @pl.when(b == 0)
    def _():
        xbuf[...] = jnp.zeros(xbuf.shape, xbuf.dtype)
        for c in weight_copies(be_ref[0], par_ref[0]):
            c.start()
        start_gather(0, 0)

    @pl.when(b < n_used)
    def _():
        slot = b % 2
        wslot = par_ref[b]
        is_first = first_ref[b] == 1

        @pl.when(is_first & (nxt_ref[b] >= 0))
        def _():
            for c in weight_copies(nxt_ref[b], 1 - wslot):
                c.start()

        @pl.when(b + 1 < n_used)
        def _():
            start_gather(b + 1, 1 - slot)

        @pl.when(is_first)
        def _():
            for c in weight_copies(0, wslot):
                c.wait()

        wait_gather(b, slot)
        half = xbuf.shape[-1]
        x_hi, x_lo = (v.astype(BF16) for v in _unpack_halves(xbuf[slot].reshape(MOE_BLOCK, half)))

        def up(w):
            return _dot(x_hi, w[wslot, 0:half, :].astype(BF16)) + _dot(x_lo, w[wslot, half:, :].astype(BF16))

        mid = (jax.nn.silu(up(wgb)) * up(wub)).astype(BF16)
        o_ref[...] = _pack_halves(_dot(mid, wdb[wslot].astype(BF16)))

    @pl.when(b >= n_used)
    def _():
        o_ref[...] = jnp.zeros(o_ref.shape, o_ref.dtype)


def _moe_experts(hn_packed, w_gate, w_up, w_down, meta, tok_sorted, n_used):
    d, d_ff = w_gate.shape[1:]
    half = hn_packed.shape[1]
    blk_exp, par, first, nxt, base, nval = meta
    n_blocks = blk_exp.shape[0]
    any_spec = pl.BlockSpec(memory_space=pl.ANY)
    return pl.pallas_call(
        _moe_kernel,
        grid_spec=pltpu.PrefetchScalarGridSpec(
            num_scalar_prefetch=8,
            grid=(n_blocks,),
            in_specs=[any_spec, any_spec, any_spec, any_spec],
            out_specs=pl.BlockSpec((MOE_BLOCK, half), lambda b, *_: (b, 0)),
            scratch_shapes=[
                pltpu.VMEM((2, MOE_BLOCK // ROW_GROUP, ROW_GROUP, half), jnp.uint32),
                pltpu.VMEM((2, d, d_ff), F32),
                pltpu.VMEM((2, d, d_ff), F32),
                pltpu.VMEM((2, d_ff, d), F32),
                pltpu.SemaphoreType.DMA((2,)),
                pltpu.SemaphoreType.DMA((2,)),
            ],
        ),
        out_shape=jax.ShapeDtypeStruct((n_blocks * MOE_BLOCK, half), jnp.uint32),
        compiler_params=pltpu.CompilerParams(
            dimension_semantics=("arbitrary",), vmem_limit_bytes=VMEM_LIMIT),
        name="moe_experts",
    )(blk_exp, par, first, nxt, base, nval, tok_sorted, n_used, hn_packed, w_gate, w_up, w_down)


def _combine_kernel(pos_ref, y_hbm, h_ref, w_ref, o_ref, ybuf, sem, *, tm):
    i = pl.program_id(0)
    n_groups = tm // ROW_GROUP

    def start_gather(tile, slot):
        base = tile * (tm * TOP_K_FINE)

        def body(g, carry):
            for r in range(ROW_GROUP):
                for k in range(TOP_K_FINE):
                    row = pos_ref[base + (g * ROW_GROUP + r) * TOP_K_FINE + k]
                    pltpu.make_async_copy(y_hbm.at[pl.ds(row, 1)], ybuf.at[slot, k, g, pl.ds(r, 1)],
                                          sem.at[slot]).start()
            return carry
        lax.fori_loop(0, n_groups, body, 0)

    @pl.when(i == 0)
    def _():
        start_gather(0, 0)

    @pl.when(i + 1 < pl.num_programs(0))
    def _():
        start_gather(i + 1, (i + 1) % 2)

    slot = i % 2
    pltpu.make_async_copy(ybuf.at[slot], ybuf.at[slot], sem.at[slot]).wait()
    w = w_ref[...]
    half = ybuf.shape[-1]
    y0 = _unpack_halves(ybuf[slot, 0].reshape(tm, half))
    y1 = _unpack_halves(ybuf[slot, 1].reshape(tm, half))
    for c in range(2):
        cols = slice(c * half, (c + 1) * half)
        o_ref[:, cols] = h_ref[:, cols] + (w[:, 0:1] * y0[c] + w[:, 1:2] * y1[c])


def _combine(y, h2, wts, pos, *, tm):
    t, d = h2.shape
    kern = functools.partial(_combine_kernel, tm=tm)
    return pl.pallas_call(
        kern,
        grid_spec=pltpu.PrefetchScalarGridSpec(
            num_scalar_prefetch=1,
            grid=(t // tm,),
            in_specs=[
                pl.BlockSpec(memory_space=pl.ANY),
                pl.BlockSpec((tm, d), lambda i, pos: (i, 0)),
                pl.BlockSpec((tm, ROUTE_LANES), lambda i, pos: (i, 0)),
            ],
            out_specs=pl.BlockSpec((tm, d), lambda i, pos: (i, 0)),
            scratch_shapes=[
                pltpu.VMEM((2, TOP_K_FINE, tm // ROW_GROUP, ROW_GROUP, y.shape[1]), jnp.uint32),
                pltpu.SemaphoreType.DMA((2,)),
            ],
        ),
        out_shape=jax.ShapeDtypeStruct((t, d), F32),
        compiler_params=pltpu.CompilerParams(
            dimension_semantics=("arbitrary",), vmem_limit_bytes=VMEM_LIMIT),
        name="moe_combine",
    )(pos, y, h2, wts)


def _lookup(table, idx):
    sel = idx[:, None] == jnp.arange(table.shape[0], dtype=jnp.int32)[None, :]
    return jnp.sum(jnp.where(sel, table[None, :], 0), axis=1).astype(jnp.int32)


def _dispatch(eid, n_tokens):
    n_assign = n_tokens * TOP_K_FINE
    experts = jnp.arange(N_EXPERTS, dtype=jnp.int32)
    e_flat = eid.reshape(n_assign)
    a_ids = jnp.arange(n_assign, dtype=jnp.int32)
    e_s, order = lax.sort_key_val(e_flat, a_ids)
    counts = jnp.sum((e_flat[:, None] == experts[None, :]).astype(jnp.int32), axis=0)
    starts = jnp.cumsum(counts) - counts
    nb = (counts + MOE_BLOCK - 1) // MOE_BLOCK
    blk_end = jnp.cumsum(nb)
    blk_start = blk_end - nb
    n_used = blk_end[-1]
    n_blocks = (n_assign + N_EXPERTS * (MOE_BLOCK - 1)) // MOE_BLOCK
    b_ids = jnp.arange(n_blocks, dtype=jnp.int32)
    used = b_ids < n_used
    blk_exp = jnp.minimum(jnp.sum((blk_end[None, :] <= b_ids[:, None]).astype(jnp.int32), axis=1),
                          N_EXPERTS - 1)
    j = b_ids - _lookup(blk_start, blk_exp)
    base = jnp.where(used, _lookup(starts, blk_exp) + j * MOE_BLOCK, 0)
    nval = jnp.where(used, jnp.clip(_lookup(counts, blk_exp) - j * MOE_BLOCK, 0, MOE_BLOCK), 0)
    first = (used & (j == 0)).astype(jnp.int32)
    active = counts > 0
    par = _lookup(jnp.cumsum(active.astype(jnp.int32)) - 1, blk_exp) & 1
    later = lax.cummin(jnp.where(active, experts, N_EXPERTS), reverse=True)
    nxt_e = jnp.concatenate([later[1:], jnp.full((1,), N_EXPERTS, jnp.int32)])
    nxt = _lookup(jnp.where(nxt_e == N_EXPERTS, -1, nxt_e), blk_exp)
    meta = tuple(v.astype(jnp.int32) for v in (blk_exp, par, first, nxt, base, nval))
    row_sorted = a_ids + _lookup(blk_start * MOE_BLOCK - starts, e_s)
    _, pos = lax.sort_key_val(order, row_sorted)
    tok_sorted = jnp.concatenate([lax.shift_right_logical(order, 1), jnp.zeros((ROW_GROUP,), jnp.int32)])
    return meta, tok_sorted, pos, n_used.astype(jnp.int32).reshape(1)


def kernel(x, mem, norm1_w, w_in, q_norm_w, k_norm_w, lambda_q1, lambda_k1, lambda_q2, lambda_k2, subln_w, ssm_lambda_re, ssm_lambda_im, ssm_log_dt, ssm_b_re, ssm_b_im, ssm_c_re, ssm_c_im, ssm_d, ssm_glu_w, ssm_glu_b, ssm_out_norm_w, w_out, norm2_w, mem_norm_w, xq_w, xkv_w, xq_norm_w, xk_norm_w, xo_w, norm3_w, router_coarse_w, router_coarse_b, router_fine_w, router_fine_b, expert_w_gate, expert_w_up, expert_w_down):
    batch, seq, d = x.shape
    mem_len = mem.shape[1]
    t = batch * seq
    depth = norm1_w.shape[0]
    d_attn = DA_HEADS * DA_V_DIM
    d_ssm = d - d_attn
    qk_cols = DA_HEADS * 2 * DA_QK_DIM
    x_hd = d // X_HEADS
    h = x.reshape(t, d)
    mem2 = mem.reshape(batch * mem_len, d)

    for l in range(depth):
        lam_init = 0.8 - 0.6 * math.exp(-0.3 * l)
        lam = (jnp.exp(jnp.sum(lambda_q1[l].astype(F32) * lambda_k1[l].astype(F32)))
               - jnp.exp(jnp.sum(lambda_q2[l].astype(F32) * lambda_k2[l].astype(F32)))
               + lam_init).reshape(1)

        n_rep = qk_cols // DA_QK_DIM
        in_gain = jnp.concatenate([
            jnp.tile(q_norm_w[l].astype(F32) * (DA_QK_DIM ** -0.5 * LOG2E), n_rep),
            jnp.tile(k_norm_w[l].astype(F32), n_rep),
            jnp.ones((d_attn + d_ssm,), F32)])
        proj = _norm_matmul(h, norm1_w[l], w_in[l].astype(BF16), in_gain,
                            n_norm_cols=2 * qk_cols, chunk=DA_QK_DIM, tm=1024, tn=1024, name="in_proj")
        sub_gain = (subln_w[l].astype(F32) * (1.0 - lam_init)).reshape(1, DA_V_DIM)
        a = _diff_attn(proj, lam, sub_gain, batch=batch, seq=seq, tq=ATTN_TILE)

        bd, a_re, a_im, cd, dd = _s5_params(ssm_lambda_re[l], ssm_lambda_im[l], ssm_log_dt[l],
                                            ssm_b_re[l], ssm_b_im[l], ssm_c_re[l], ssm_c_im[l], ssm_d[l])
        y = _s5(proj, 2 * qk_cols + d_attn, d_ssm, bd, a_re, a_im, cd, dd, batch=batch, seq=seq)
        h = _mix_out(a, y, h, ssm_glu_w[l].astype(BF16), ssm_glu_b[l], ssm_out_norm_w[l],
                     w_out[l].astype(BF16), tm=ATTN_TILE, attn_tiles=seq // ATTN_TILE)

        kv_gain = jnp.concatenate([jnp.tile(xk_norm_w[l].astype(F32), X_HEADS), jnp.ones((d,), F32)])
        kv = _norm_matmul(mem2, mem_norm_w[l], xkv_w[l], kv_gain,
                          n_norm_cols=d, chunk=x_hd, tm=batch * mem_len, tn=512, name="kv_proj")
        q_gain = jnp.tile(xq_norm_w[l].astype(F32) * (x_hd ** -0.5), X_HEADS)
        q = _norm_matmul(h, norm2_w[l], xq_w[l].astype(BF16), q_gain,
                         n_norm_cols=d, chunk=x_hd, tm=1024, tn=1024, name="xq_proj")
        r_w = jnp.concatenate([router_coarse_w[l].astype(F32), router_fine_w[l].astype(F32)], axis=1)
        r_w = jnp.pad(r_w, ((0, 0), (0, ROUTE_LANES - r_w.shape[1])))
        r_hi = r_w.astype(BF16)
        r_lo = (r_w - r_hi.astype(F32)).astype(BF16)
        r_b = jnp.concatenate([router_coarse_b[l].astype(F32), router_fine_b[l].astype(F32)])
        r_b = jnp.pad(r_b, (0, ROUTE_LANES - r_b.shape[0])).reshape(1, ROUTE_LANES)
        h2, hn3, eid_t, wts_t = _xattn_route(q, kv, h, xo_w[l].astype(BF16), norm3_w[l], r_hi, r_lo, r_b,
                                             batch=batch, seq=seq, mem_len=mem_len, tm=512)

        def per_token(x_t):
            return x_t.reshape(-1, SUBLANES, x_t.shape[1])[:, :TOP_K_FINE, :].transpose(0, 2, 1).reshape(t, TOP_K_FINE)

        eid = per_token(eid_t)
        wts = jnp.pad(per_token(wts_t), ((0, 0), (0, ROUTE_LANES - TOP_K_FINE)))

        meta, tok_sorted, pos, n_used = _dispatch(eid, t)
        y = _moe_experts(hn3, expert_w_gate[l], expert_w_up[l], expert_w_down[l], meta, tok_sorted, n_used)
        h = _combine(y, h2, wts, pos, tm=256)

    return h.reshape(batch, seq, d)
```

```python
import functools
import math

import jax
import jax.numpy as jnp
from jax import lax
from jax.experimental import pallas as pl
from jax.experimental.pallas import tpu as pltpu

F32 = jnp.float32
BF16 = jnp.bfloat16

EPS = 1e-6
DA_HEADS = 4
DA_QK_DIM = 128
DA_V_DIM = 256
SSM_GROUP = 16
SSM_STATE = 64
X_HEADS = 4
MOE_GROUPS = 8
EXP_PER_GROUP = 8
N_EXPERTS = MOE_GROUPS * EXP_PER_GROUP
TOP_K_FINE = 2

LANES = 128
SUBLANES = 8
MXU_TILE = 256
VMEM_LIMIT = 56 * 1024 * 1024
NEG = -1e30
LOG2E = math.log2(math.e)

SSM_CHUNK_GROUPS = LANES // SSM_GROUP
SSM_CHUNK_STATE = SSM_CHUNK_GROUPS * SSM_STATE
SCAN_SEGS = SUBLANES
ATTN_TILE = 512
MOE_BLOCK = 256
ROW_GROUP = SUBLANES
WEIGHT_DMA_CHUNKS = 4
ROUTE_LANES = LANES


def _rms(x, eps=EPS):
    return x * lax.rsqrt(jnp.mean(x * x, axis=-1, keepdims=True) + eps)


def _dot(a, b):
    return jnp.dot(a, b, preferred_element_type=F32)


def _dot_nt(a, b):
    return lax.dot_general(a, b, (((1,), (1,)), ((), ())), preferred_element_type=F32)


def _pack_halves(x):
    n = x.shape[1] // 2
    hi = lax.bitcast_convert_type(x[:, :n].astype(BF16).astype(F32), jnp.uint32)
    lo = lax.bitcast_convert_type(x[:, n:].astype(BF16).astype(F32), jnp.uint32)
    return hi | lax.shift_right_logical(lo, jnp.uint32(16))


def _unpack_halves(p):
    hi = lax.bitcast_convert_type(p & jnp.uint32(0xFFFF0000), F32)
    lo = lax.bitcast_convert_type(lax.shift_left(p, jnp.uint32(16)), F32)
    return hi, lo


def _resident(shape, index_map):
    return pl.BlockSpec(shape, index_map, pipeline_mode=pl.Buffered(1))


def _norm_matmul_kernel(x_ref, nw_ref, w_ref, g_ref, o_ref, xn_ref, *, n_norm_tiles, chunk):
    j = pl.program_id(1)

    @pl.when(j == 0)
    def _():
        x = x_ref[...].astype(F32)
        xn_ref[...] = (_rms(x) * nw_ref[...]).astype(BF16)

    normed = j < n_norm_tiles
    tn = w_ref.shape[1]
    sub = max(chunk, MXU_TILE)
    for s in range(tn // sub):
        w = w_ref[:, s * sub:(s + 1) * sub]
        if w.dtype != BF16:
            w = w.astype(BF16)
        acc = _dot(xn_ref[...], w)
        for c in range(sub // chunk):
            lo = s * sub + c * chunk
            a = acc[:, c * chunk:(c + 1) * chunk]
            inv = lax.rsqrt(jnp.mean(a * a, axis=-1, keepdims=True) + EPS)
            scale = jnp.where(normed, inv, 1.0)
            o_ref[:, lo:lo + chunk] = (a * scale * g_ref[:, lo:lo + chunk]).astype(o_ref.dtype)


def _norm_matmul(x, norm_w, w, gain, *, n_norm_cols, chunk, tm, tn, name):
    m, k = x.shape
    n = w.shape[1]
    assert m % tm == 0 and n % tn == 0 and tn % max(chunk, MXU_TILE) == 0 and n_norm_cols % tn == 0
    kern = functools.partial(_norm_matmul_kernel, n_norm_tiles=n_norm_cols // tn, chunk=chunk)
    return pl.pallas_call(
        kern,
        grid=(m // tm, n // tn),
        in_specs=[
            pl.BlockSpec((tm, k), lambda i, j: (i, 0)),
            pl.BlockSpec((1, k), lambda i, j: (0, 0)),
            pl.BlockSpec((k, tn), lambda i, j: (0, j)),
            pl.BlockSpec((1, tn), lambda i, j: (0, j)),
        ],
        out_specs=pl.BlockSpec((tm, tn), lambda i, j: (i, j)),
        out_shape=jax.ShapeDtypeStruct((m, n), BF16),
        scratch_shapes=[pltpu.VMEM((tm, k), BF16)],
        compiler_params=pltpu.CompilerParams(
            dimension_semantics=("parallel", "arbitrary"), vmem_limit_bytes=VMEM_LIMIT),
        name=name,
    )(x, norm_w.reshape(1, k).astype(F32), w, gain.reshape(1, n).astype(F32))


def _diff_attn_kernel(lam_ref, qa_ref, qb_ref, k_ref, v_ref, g_ref, o_ref, *stat_refs, tq, n_q):
    pair = pl.program_id(2)
    stats_a = (stat_refs[0:3], stat_refs[3:6])
    stats_b = (stat_refs[6:9], stat_refs[9:12])

    def scores(q_ref, j):
        return tuple(_dot_nt(q_ref[:, c * DA_QK_DIM:(c + 1) * DA_QK_DIM],
                             k_ref[j * tq:(j + 1) * tq, c * DA_QK_DIM:(c + 1) * DA_QK_DIM])
                     for c in range(2))

    def accumulate(stats, j, s_pair, masked):
        for s, (m_ref, l_ref, acc_ref) in zip(s_pair, stats):
            if masked:
                row = lax.broadcasted_iota(jnp.int32, s.shape, 0)
                col = lax.broadcasted_iota(jnp.int32, s.shape, 1)
                s = jnp.where(col <= row, s, NEG)
            m_old = m_ref[...]
            m_new = jnp.maximum(m_old, jnp.max(s, axis=-1, keepdims=True))
            p = jnp.exp2(s - m_new)
            alpha = jnp.exp2(m_old - m_new)
            l_ref[...] = alpha * l_ref[...] + jnp.sum(p, axis=-1, keepdims=True)
            acc_ref[...] = alpha * acc_ref[...] + _dot(p.astype(BF16), v_ref[j * tq:(j + 1) * tq, :])
            m_ref[...] = m_new

    def finish(stats, rows):
        (_, l1, acc1), (_, l2, acc2) = stats
        o = acc1[...] / l1[...] - lam_ref[0] * (acc2[...] / l2[...])
        o_ref[rows, :] = (_rms(o) * g_ref[...]).astype(o_ref.dtype)

    def run(p):
        tiles = ((qa_ref, stats_a, p), (qb_ref, stats_b, n_q - 1 - p))
        for _, stats, _ in tiles:
            for m_ref, l_ref, acc_ref in stats:
                m_ref[...] = jnp.full(m_ref.shape, NEG, F32)
                l_ref[...] = jnp.zeros(l_ref.shape, F32)
                acc_ref[...] = jnp.zeros(acc_ref.shape, F32)
        pending = [scores(q_ref, 0) for q_ref, _, _ in tiles]
        for j in range(n_q - p):
            for idx, (q_ref, stats, diag) in enumerate(tiles):
                if j > diag:
                    continue
                s_pair = pending[idx]
                if j < diag:
                    pending[idx] = scores(q_ref, j + 1)
                accumulate(stats, j, s_pair, masked=(j == diag))
        finish(stats_a, slice(0, tq))
        finish(stats_b, slice(tq, 2 * tq))

    for p in range(n_q // 2):
        pl.when(pair == p)(functools.partial(run, p))


def _attn_tile_pos(tile, nq):
    b, qt = tile // nq, tile % nq
    return b * nq + jnp.where(qt < nq // 2, 2 * qt, 2 * (nq - 1 - qt) + 1)


def _diff_attn(proj, lam, gain, *, batch, seq, tq):
    t = batch * seq
    nq = seq // tq
    assert nq % 2 == 0
    width = 2 * DA_QK_DIM
    k_blk0 = DA_HEADS
    v_blk0 = 2 * DA_HEADS
    kern = functools.partial(_diff_attn_kernel, tq=tq, n_q=nq)
    stat = [pltpu.VMEM((tq, 1), F32), pltpu.VMEM((tq, 1), F32), pltpu.VMEM((tq, DA_V_DIM), F32)]
    return pl.pallas_call(
        kern,
        grid_spec=pltpu.PrefetchScalarGridSpec(
            num_scalar_prefetch=1,
            grid=(batch, DA_HEADS, nq // 2),
            in_specs=[
                pl.BlockSpec((tq, width), lambda b, h, p, lam: (b * nq + p, h)),
                pl.BlockSpec((tq, width), lambda b, h, p, lam: (b * nq + nq - 1 - p, h)),
                pl.BlockSpec((seq, width), lambda b, h, p, lam: (b, k_blk0 + h)),
                pl.BlockSpec((seq, width), lambda b, h, p, lam: (b, v_blk0 + h)),
                pl.BlockSpec((1, DA_V_DIM), lambda b, h, p, lam: (0, 0)),
            ],
            out_specs=pl.BlockSpec((2 * tq, DA_V_DIM), lambda b, h, p, lam: (b * (nq // 2) + p, h)),
            scratch_shapes=stat * 4,
        ),
        out_shape=jax.ShapeDtypeStruct((t, DA_HEADS * DA_V_DIM), BF16),
        compiler_params=pltpu.CompilerParams(
            dimension_semantics=("parallel", "parallel", "arbitrary"), vmem_limit_bytes=VMEM_LIMIT),
        name="diff_attn",
    )(lam, proj, proj, proj, proj, gain)


def _s5_kernel(u_ref, bd_ref, ar_ref, ai_ref, cd_ref, d_ref, o_ref, xs_ref, us_ref, ys_ref, *, seq, rows):
    ns = SSM_CHUNK_STATE
    seg_len = seq // SCAN_SEGS
    n_row_blk = seq // rows
    steps = rows // SCAN_SEGS

    for seg in range(SCAN_SEGS):
        us_ref[pl.ds(seg, seg_len, stride=SCAN_SEGS), :] = (
            u_ref[seg * seg_len:(seg + 1) * seg_len, :].astype(F32))

    def in_map(r):
        rs = slice(r * rows, (r + 1) * rows)
        xs_ref[rs, :] = _dot(us_ref[rs, :].astype(BF16), bd_ref[...])

    def out_map(r):
        rs = slice(r * rows, (r + 1) * rows)
        y = _dot(xs_ref[rs, :].astype(BF16), cd_ref[...]) + d_ref[...] * us_ref[rs, :]
        ys_ref[rs, :] = jax.nn.gelu(y)
        for seg in range(SCAN_SEGS):
            t0 = seg * seg_len + r * steps
            o_ref[t0:t0 + steps, :] = (
                ys_ref[pl.ds(r * rows + seg, steps, stride=SCAN_SEGS), :].astype(o_ref.dtype))

    ar = jnp.broadcast_to(ar_ref[...], (SCAN_SEGS, ns))
    ai = jnp.broadcast_to(ai_ref[...], (SCAN_SEGS, ns))

    def advance(t, sr, si):
        ts = slice(t * SCAN_SEGS, (t + 1) * SCAN_SEGS)
        return ar * sr - ai * si + xs_ref[ts, 0:ns], ar * si + ai * sr + xs_ref[ts, ns:2 * ns]

    in_map(0)
    fr = fi = jnp.zeros((SCAN_SEGS, ns), F32)
    for r in range(n_row_blk):
        if r + 1 < n_row_blk:
            in_map(r + 1)
        for t in range(r * steps, (r + 1) * steps):
            fr, fi = advance(t, fr, fi)

    pr, pi = ar, ai
    for _ in range(int(math.log2(seg_len))):
        pr, pi = pr * pr - pi * pi, 2.0 * pr * pi
    seg = lax.broadcasted_iota(jnp.int32, (SCAN_SEGS, ns), 0)

    def shifted(x, k):
        return jnp.where(seg >= k, pltpu.roll(x, k, 0), 0.0)

    k = 1
    while k < SCAN_SEGS:
        gr, gi = shifted(fr, k), shifted(fi, k)
        fr, fi = fr + pr * gr - pi * gi, fi + pr * gi + pi * gr
        pr, pi = pr * pr - pi * pi, 2.0 * pr * pi
        k *= 2
    sr, si = shifted(fr, 1), shifted(fi, 1)

    for r in range(n_row_blk):
        for t in range(r * steps, (r + 1) * steps):
            sr, si = advance(t, sr, si)
            ts = slice(t * SCAN_SEGS, (t + 1) * SCAN_SEGS)
            xs_ref[ts, 0:ns] = sr
            xs_ref[ts, ns:2 * ns] = si
        if r >= 1:
            out_map(r - 1)
    out_map(n_row_blk - 1)


def _s5(proj, u_col0, d_ssm, bd, a_re, a_im, cd, d_skip, *, batch, seq, rows=256):
    n_chunks = d_ssm // LANES
    u_blk0 = u_col0 // LANES
    kern = functools.partial(_s5_kernel, seq=seq, rows=rows)
    return pl.pallas_call(
        kern,
        grid=(batch, n_chunks),
        in_specs=[
            pl.BlockSpec((seq, LANES), lambda b, c: (b, u_blk0 + c)),
            pl.BlockSpec((None, LANES, 2 * SSM_CHUNK_STATE), lambda b, c: (c, 0, 0)),
            pl.BlockSpec((None, 1, SSM_CHUNK_STATE), lambda b, c: (c, 0, 0)),
            pl.BlockSpec((None, 1, SSM_CHUNK_STATE), lambda b, c: (c, 0, 0)),
            pl.BlockSpec((None, 2 * SSM_CHUNK_STATE, LANES), lambda b, c: (c, 0, 0)),
            pl.BlockSpec((None, 1, LANES), lambda b, c: (c, 0, 0)),
        ],
        out_specs=pl.BlockSpec((seq, LANES), lambda b, c: (b, c)),
        out_shape=jax.ShapeDtypeStruct((batch * seq, d_ssm), BF16),
        scratch_shapes=[pltpu.VMEM((seq, 2 * SSM_CHUNK_STATE), F32),
                        pltpu.VMEM((seq, LANES), F32), pltpu.VMEM((seq, LANES), F32)],
        compiler_params=pltpu.CompilerParams(
            dimension_semantics=("parallel", "parallel"), vmem_limit_bytes=VMEM_LIMIT),
        name="s5_scan",
    )(proj, bd, a_re, a_im, cd, d_skip)


def _s5_params(lam_re, lam_im, log_dt, b_re, b_im, c_re, c_im, d_skip):
    g = lam_re.shape[0]
    nc = g // SSM_CHUNK_GROUPS
    lr = jnp.minimum(lam_re.astype(F32), -1e-4)
    li = lam_im.astype(F32)
    dt = jnp.exp(log_dt.astype(F32))[:, None]
    mag = jnp.exp(lr * dt)
    lb_re, lb_im = mag * jnp.cos(li * dt), mag * jnp.sin(li * dt)
    den = lr * lr + li * li
    coef_re = ((lb_re - 1.0) * lr + lb_im * li) / den
    coef_im = (lb_im * lr - (lb_re - 1.0) * li) / den
    br, bi = b_re.astype(F32), b_im.astype(F32)
    bb_re = coef_re[..., None] * br - coef_im[..., None] * bi
    bb_im = coef_re[..., None] * bi + coef_im[..., None] * br
    eye = jnp.eye(SSM_CHUNK_GROUPS, dtype=F32)

    def pack_in(bb):
        bb = bb.reshape(nc, SSM_CHUNK_GROUPS, SSM_STATE, SSM_GROUP)
        return jnp.einsum('cgph,gk->cghkp', bb, eye).reshape(nc, LANES, SSM_CHUNK_STATE)

    def pack_out(cc):
        cc = cc.astype(F32).reshape(nc, SSM_CHUNK_GROUPS, SSM_GROUP, SSM_STATE)
        return jnp.einsum('cghp,gk->ckpgh', cc, eye).reshape(nc, SSM_CHUNK_STATE, LANES)

    bd = jnp.concatenate([pack_in(bb_re), pack_in(bb_im)], axis=-1).astype(BF16)
    cd = jnp.concatenate([pack_out(c_re), -pack_out(c_im)], axis=1).astype(BF16)
    a_re = lb_re.reshape(nc, 1, SSM_CHUNK_STATE)
    a_im = lb_im.reshape(nc, 1, SSM_CHUNK_STATE)
    dd = d_skip.astype(F32).reshape(nc, 1, LANES)
    return bd, a_re, a_im, cd, dd


def _mix_out_kernel(a_ref, y_ref, x_ref, gw_ref, gb_ref, nw_ref, wo_ref, o_ref):
    d_attn = a_ref.shape[1]
    y = y_ref[...]
    gate = _dot(y, gw_ref[...]) + gb_ref[...]
    s = y.astype(F32) * jax.nn.sigmoid(gate)
    sn = (_rms(s) * nw_ref[...]).astype(BF16)
    acc = _dot(a_ref[...], wo_ref[0:d_attn, :]) + _dot(sn, wo_ref[d_attn:, :])
    o_ref[...] = x_ref[...] + acc


def _mix_out(a, y, x, glu_w, glu_b, norm_w, w_out, *, tm, attn_tiles):
    t, d = x.shape
    d_attn, d_ssm = a.shape[1], y.shape[1]
    const = lambda i: (0, 0)
    return pl.pallas_call(
        _mix_out_kernel,
        grid=(t // tm,),
        in_specs=[
            pl.BlockSpec((tm, d_attn), lambda i: (_attn_tile_pos(i, attn_tiles), 0)),
            pl.BlockSpec((tm, d_ssm), lambda i: (i, 0)),
            pl.BlockSpec((tm, d), lambda i: (i, 0)),
            _resident((d_ssm, d_ssm), const),
            _resident((1, d_ssm), const),
            _resident((1, d_ssm), const),
            _resident((d, d), const),
        ],
        out_specs=pl.BlockSpec((tm, d), lambda i: (i, 0)),
        out_shape=jax.ShapeDtypeStruct((t, d), F32),
        compiler_params=pltpu.CompilerParams(
            dimension_semantics=("parallel",), vmem_limit_bytes=VMEM_LIMIT),
        name="mix_out",
    )(a, y, x, glu_w, glu_b.reshape(1, d_ssm).astype(F32), norm_w.reshape(1, d_ssm).astype(F32), w_out)


def _xattn_route_kernel(q_ref, k_ref, v_ref, h_ref, xo_ref, nw_ref, rhi_ref, rlo_ref, rb_ref,
                        h2_ref, hn_ref, eid_ref, wts_ref):
    d = h_ref.shape[1]
    hd = d // X_HEADS
    h2 = h_ref[...]
    for h in range(X_HEADS):
        sl = slice(h * hd, (h + 1) * hd)
        s = _dot_nt(q_ref[:, sl], k_ref[:, sl])
        p = jnp.exp(s - jnp.max(s, axis=-1, keepdims=True))
        p = p * (1.0 / jnp.sum(p, axis=-1, keepdims=True))
        o = _dot(p.astype(BF16), v_ref[:, sl]).astype(BF16)
        h2 = h2 + _dot(o, xo_ref[sl, :])
    h2_ref[...] = h2
    hn = _rms(h2) * nw_ref[...]
    hn_ref[...] = _pack_halves(hn)

    hi = hn.astype(BF16)
    lo = (hn - hi.astype(F32)).astype(BF16)
    hi_both = _dot(hi, jnp.concatenate([rhi_ref[...], rlo_ref[...]], axis=1))
    logits = (hi_both[:, :ROUTE_LANES] + hi_both[:, ROUTE_LANES:] + _dot(lo, rhi_ref[...])) + rb_ref[...]

    lt = logits.T
    idx = lax.broadcasted_iota(jnp.int32, (SUBLANES, lt.shape[1]), 0)

    def first_row(cond):
        return jnp.min(jnp.where(cond, idx, SUBLANES), axis=0, keepdims=True)

    def softmax_rows(x):
        e = jnp.exp(x - jnp.max(x, axis=0, keepdims=True))
        return e / jnp.sum(e, axis=0, keepdims=True)

    p_c = softmax_rows(lt[0:MOE_GROUPS, :])
    p_grp = jnp.max(p_c, axis=0, keepdims=True)
    grp = first_row(p_c == p_grp)
    lf = lt[MOE_GROUPS:MOE_GROUPS + EXP_PER_GROUP, :]
    for g in range(1, MOE_GROUPS):
        lo_row = MOE_GROUPS + g * EXP_PER_GROUP
        lf = jnp.where(grp == g, lt[lo_row:lo_row + EXP_PER_GROUP, :], lf)
    pf = softmax_rows(lf)
    v1 = jnp.max(pf, axis=0, keepdims=True)
    i1 = first_row(pf == v1)
    rest = idx != i1
    v2 = jnp.max(jnp.where(rest, pf, -1.0), axis=0, keepdims=True)
    i2 = first_row(rest & (pf == v2))
    tot = v1 + v2
    e1 = grp * EXP_PER_GROUP + i1
    e2 = grp * EXP_PER_GROUP + i2
    eid_ref[...] = jnp.where(idx == 0, e1, jnp.where(idx == 1, e2, 0))
    wts_ref[...] = jnp.where(idx == 0, v1 / tot * p_grp, jnp.where(idx == 1, v2 / tot * p_grp, 0.0))


def _xattn_route(q, kv, h1, xo_w, norm_w, r_hi, r_lo, r_b, *, batch, seq, mem_len, tm):
    t, d = h1.shape
    n = seq // tm
    const = lambda b, i: (0, 0)
    row = lambda b, i: (b * n + i, 0)
    return pl.pallas_call(
        _xattn_route_kernel,
        grid=(batch, n),
        in_specs=[
            pl.BlockSpec((tm, d), row),
            pl.BlockSpec((mem_len, d), lambda b, i: (b, 0)),
            pl.BlockSpec((mem_len, d), lambda b, i: (b, 1)),
            pl.BlockSpec((tm, d), row),
            _resident((d, d), const),
            _resident((1, d), const),
            _resident((d, ROUTE_LANES), const),
            _resident((d, ROUTE_LANES), const),
            _resident((1, ROUTE_LANES), const),
        ],
        out_specs=[
            pl.BlockSpec((tm, d), row),
            pl.BlockSpec((tm, d // 2), row),
            pl.BlockSpec((SUBLANES, tm), row),
            pl.BlockSpec((SUBLANES, tm), row),
        ],
        out_shape=[
            jax.ShapeDtypeStruct((t, d), F32),
            jax.ShapeDtypeStruct((t, d // 2), jnp.uint32),
            jax.ShapeDtypeStruct((t // tm * SUBLANES, tm), jnp.int32),
            jax.ShapeDtypeStruct((t // tm * SUBLANES, tm), F32),
        ],
        compiler_params=pltpu.CompilerParams(
            dimension_semantics=("parallel", "parallel"), vmem_limit_bytes=VMEM_LIMIT),
        name="xattn_route",
    )(q, kv, kv, h1, xo_w, norm_w.reshape(1, d).astype(F32), r_hi, r_lo, r_b)


def _moe_kernel(be_ref, par_ref, first_ref, nxt_ref, base_ref, nval_ref, tok_ref, nu_ref,
                hn_hbm, wg_hbm, wu_hbm, wd_hbm, o_ref,
                xbuf, wgb, wub, wdb, gsem, wsem):
    b = pl.program_id(0)
    n_used = nu_ref[0]

    def weight_copies(e, slot):
        copies = []
        for hbm, buf in ((wg_hbm, wgb), (wu_hbm, wub), (wd_hbm, wdb)):
            rows = hbm.shape[1] // WEIGHT_DMA_CHUNKS
            for c in range(WEIGHT_DMA_CHUNKS):
                sl = pl.ds(c * rows, rows)
                copies.append(pltpu.make_async_copy(hbm.at[e, sl], buf.at[slot, sl], wsem.at[slot]))
        return copies

    def groups(blk):
        return (nval_ref[blk] + ROW_GROUP - 1) // ROW_GROUP

    def start_gather(blk, slot):
        base = base_ref[blk]

        def body(g, carry):
            for r in range(ROW_GROUP):
                tok = tok_ref[base + g * ROW_GROUP + r]
                pltpu.make_async_copy(hn_hbm.at[pl.ds(tok, 1)], xbuf.at[slot, g, pl.ds(r, 1)],
                                      gsem.at[slot]).start()
            return carry
        lax.fori_loop(0, groups(blk), body, 0)

    def wait_gather(blk, slot):
        filled = xbuf.at[slot, pl.ds(0, groups(blk))]
        pltpu.make_async_copy(filled, filled, gsem.at[slot]).wait()

    @pl.when(b == 0)
    def _():
        xbuf[...] = jnp.zeros(xbuf.shape, xbuf.dtype)
        for c in weight_copies(be_ref[0], par_ref[0]):
            c.start()
        start_gather(0, 0)

    @pl.when(b < n_used)
    def _():
        slot = b % 2
        wslot = par_ref[b]
        is_first = first_ref[b] == 1

        @pl.when(is_first & (nxt_ref[b] >= 0))
        def _():
            for c in weight_copies(nxt_ref[b], 1 - wslot):
                c.start()

        @pl.when(b + 1 < n_used)
        def _():
            start_gather(b + 1, 1 - slot)

        @pl.when(is_first)
        def _():
            for c in weight_copies(0, wslot):
                c.wait()

        wait_gather(b, slot)
        half = xbuf.shape[-1]
        x_hi, x_lo = (v.astype(BF16) for v in _unpack_halves(xbuf[slot].reshape(MOE_BLOCK, half)))

        def up(w):
            return _dot(x_hi, w[wslot, 0:half, :].astype(BF16)) + _dot(x_lo, w[wslot, half:, :].astype(BF16))

        mid = (jax.nn.silu(up(wgb)) * up(wub)).astype(BF16)
        o_ref[...] = _pack_halves(_dot(mid, wdb[wslot].astype(BF16)))

    @pl.when(b >= n_used)
    def _():
        o_ref[...] = jnp.zeros(o_ref.shape, o_ref.dtype)


def _moe_experts(hn_packed, w_gate, w_up, w_down, meta, tok_sorted, n_used):
    d, d_ff = w_gate.shape[1:]
    half = hn_packed.shape[1]
    blk_exp, par, first, nxt, base, nval = meta
    n_blocks = blk_exp.shape[0]
    any_spec = pl.BlockSpec(memory_space=pl.ANY)
    return pl.pallas_call(
        _moe_kernel,
        grid_spec=pltpu.PrefetchScalarGridSpec(
            num_scalar_prefetch=8,
            grid=(n_blocks,),
            in_specs=[any_spec, any_spec, any_spec, any_spec],
            out_specs=pl.BlockSpec((MOE_BLOCK, half), lambda b, *_: (b, 0)),
            scratch_shapes=[
                pltpu.VMEM((2, MOE_BLOCK // ROW_GROUP, ROW_GROUP, half), jnp.uint32),
                pltpu.VMEM((2, d, d_ff), F32),
                pltpu.VMEM((2, d, d_ff), F32),
                pltpu.VMEM((2, d_ff, d), F32),
                pltpu.SemaphoreType.DMA((2,)),
                pltpu.SemaphoreType.DMA((2,)),
            ],
        ),
        out_shape=jax.ShapeDtypeStruct((n_blocks * MOE_BLOCK, half), jnp.uint32),
        compiler_params=pltpu.CompilerParams(
            dimension_semantics=("arbitrary",), vmem_limit_bytes=VMEM_LIMIT),
        name="moe_experts",
    )(blk_exp, par, first, nxt, base, nval, tok_sorted, n_used, hn_packed, w_gate, w_up, w_down)


def _combine_kernel(pos_ref, y_hbm, h_ref, w_ref, o_ref, ybuf, sem, *, tm):
    i = pl.program_id(0)
    n_groups = tm // ROW_GROUP

    def start_gather(tile, slot):
        base = tile * (tm * TOP_K_FINE)

        def body(g, carry):
            for r in range(ROW_GROUP):
                for k in range(TOP_K_FINE):
                    row = pos_ref[base + (g * ROW_GROUP + r) * TOP_K_FINE + k]
                    pltpu.make_async_copy(y_hbm.at[pl.ds(row, 1)], ybuf.at[slot, k, g, pl.ds(r, 1)],
                                          sem.at[slot]).start()
            return carry
        lax.fori_loop(0, n_groups, body, 0)

    @pl.when(i == 0)
    def _():
        start_gather(0, 0)

    @pl.when(i + 1 < pl.num_programs(0))
    def _():
        start_gather(i + 1, (i + 1) % 2)

    slot = i % 2
    pltpu.make_async_copy(ybuf.at[slot], ybuf.at[slot], sem.at[slot]).wait()
    w = w_ref[...]
    half = ybuf.shape[-1]
    y0 = _unpack_halves(ybuf[slot, 0].reshape(tm, half))
    y1 = _unpack_halves(ybuf[slot, 1].reshape(tm, half))
    for c in range(2):
        cols = slice(c * half, (c + 1) * half)
        o_ref[:, cols] = h_ref[:, cols] + (w[:, 0:1] * y0[c] + w[:, 1:2] * y1[c])


def _combine(y, h2, wts, pos, *, tm):
    t, d = h2.shape
    kern = functools.partial(_combine_kernel, tm=tm)
    return pl.pallas_call(
        kern,
        grid_spec=pltpu.PrefetchScalarGridSpec(
            num_scalar_prefetch=1,
            grid=(t // tm,),
            in_specs=[
                pl.BlockSpec(memory_space=pl.ANY),
                pl.BlockSpec((tm, d), lambda i, pos: (i, 0)),
                pl.BlockSpec((tm, ROUTE_LANES), lambda i, pos: (i, 0)),
            ],
            out_specs=pl.BlockSpec((tm, d), lambda i, pos: (i, 0)),
            scratch_shapes=[
                pltpu.VMEM((2, TOP_K_FINE, tm // ROW_GROUP, ROW_GROUP, y.shape[1]), jnp.uint32),
                pltpu.SemaphoreType.DMA((2,)),
            ],
        ),
        out_shape=jax.ShapeDtypeStruct((t, d), F32),
        compiler_params=pltpu.CompilerParams(
            dimension_semantics=("arbitrary",), vmem_limit_bytes=VMEM_LIMIT),
        name="moe_combine",
    )(pos, y, h2, wts)


def _lookup(table, idx):
    sel = idx[:, None] == jnp.arange(table.shape[0], dtype=jnp.int32)[None, :]
    return jnp.sum(jnp.where(sel, table[None, :], 0), axis=1).astype(jnp.int32)


def _dispatch(eid, n_tokens):
    n_assign = n_tokens * TOP_K_FINE
    experts = jnp.arange(N_EXPERTS, dtype=jnp.int32)
    e_flat = eid.reshape(n_assign)
    a_ids = jnp.arange(n_assign, dtype=jnp.int32)
    e_s, order = lax.sort_key_val(e_flat, a_ids)
    counts = jnp.sum((e_flat[:, None] == experts[None, :]).astype(jnp.int32), axis=0)
    starts = jnp.cumsum(counts) - counts
    nb = (counts + MOE_BLOCK - 1) // MOE_BLOCK
    blk_end = jnp.cumsum(nb)
    blk_start = blk_end - nb
    n_used = blk_end[-1]
    n_blocks = (n_assign + N_EXPERTS * (MOE_BLOCK - 1)) // MOE_BLOCK
    b_ids = jnp.arange(n_blocks, dtype=jnp.int32)
    used = b_ids < n_used
    blk_exp = jnp.minimum(jnp.sum((blk_end[None, :] <= b_ids[:, None]).astype(jnp.int32), axis=1),
                          N_EXPERTS - 1)
    j = b_ids - _lookup(blk_start, blk_exp)
    base = jnp.where(used, _lookup(starts, blk_exp) + j * MOE_BLOCK, 0)
    nval = jnp.where(used, jnp.clip(_lookup(counts, blk_exp) - j * MOE_BLOCK, 0, MOE_BLOCK), 0)
    first = (used & (j == 0)).astype(jnp.int32)
    active = counts > 0
    par = _lookup(jnp.cumsum(active.astype(jnp.int32)) - 1, blk_exp) & 1
    later = lax.cummin(jnp.where(active, experts, N_EXPERTS), reverse=True)
    nxt_e = jnp.concatenate([later[1:], jnp.full((1,), N_EXPERTS, jnp.int32)])
    nxt = _lookup(jnp.where(nxt_e == N_EXPERTS, -1, nxt_e), blk_exp)
    meta = tuple(v.astype(jnp.int32) for v in (blk_exp, par, first, nxt, base, nval))
    row_sorted = a_ids + _lookup(blk_start * MOE_BLOCK - starts, e_s)
    _, pos = lax.sort_key_val(order, row_sorted)
    tok_sorted = jnp.concatenate([lax.shift_right_logical(order, 1), jnp.zeros((ROW_GROUP,), jnp.int32)])
    return meta, tok_sorted, pos, n_used.astype(jnp.int32).reshape(1)


def kernel(x, mem, norm1_w, w_in, q_norm_w, k_norm_w, lambda_q1, lambda_k1, lambda_q2, lambda_k2, subln_w, ssm_lambda_re, ssm_lambda_im, ssm_log_dt, ssm_b_re, ssm_b_im, ssm_c_re, ssm_c_im, ssm_d, ssm_glu_w, ssm_glu_b, ssm_out_norm_w, w_out, norm2_w, mem_norm_w, xq_w, xkv_w, xq_norm_w, xk_norm_w, xo_w, norm3_w, router_coarse_w, router_coarse_b, router_fine_w, router_fine_b, expert_w_gate, expert_w_up, expert_w_down):
    batch, seq, d = x.shape
    mem_len = mem.shape[1]
    t = batch * seq
    depth = norm1_w.shape[0]
    d_attn = DA_HEADS * DA_V_DIM
    d_ssm = d - d_attn
    qk_cols = DA_HEADS * 2 * DA_QK_DIM
    x_hd = d // X_HEADS
    h = x.reshape(t, d)
    mem2 = mem.reshape(batch * mem_len, d)

    for l in range(depth):
        lam_init = 0.8 - 0.6 * math.exp(-0.3 * l)
        lam = (jnp.exp(jnp.sum(lambda_q1[l].astype(F32) * lambda_k1[l].astype(F32)))
               - jnp.exp(jnp.sum(lambda_q2[l].astype(F32) * lambda_k2[l].astype(F32)))
               + lam_init).reshape(1)

        n_rep = qk_cols // DA_QK_DIM
        in_gain = jnp.concatenate([
            jnp.tile(q_norm_w[l].astype(F32) * (DA_QK_DIM ** -0.5 * LOG2E), n_rep),
            jnp.tile(k_norm_w[l].astype(F32), n_rep),
            jnp.ones((d_attn + d_ssm,), F32)])
        proj = _norm_matmul(h, norm1_w[l], w_in[l].astype(BF16), in_gain,
                            n_norm_cols=2 * qk_cols, chunk=DA_QK_DIM, tm=1024, tn=2048, name="in_proj")
        sub_gain = (subln_w[l].astype(F32) * (1.0 - lam_init)).reshape(1, DA_V_DIM)
        a = _diff_attn(proj, lam, sub_gain, batch=batch, seq=seq, tq=ATTN_TILE)

        bd, a_re, a_im, cd, dd = _s5_params(ssm_lambda_re[l], ssm_lambda_im[l], ssm_log_dt[l],
                                            ssm_b_re[l], ssm_b_im[l], ssm_c_re[l], ssm_c_im[l], ssm_d[l])
        y = _s5(proj, 2 * qk_cols + d_attn, d_ssm, bd, a_re, a_im, cd, dd, batch=batch, seq=seq)
        h = _mix_out(a, y, h, ssm_glu_w[l].astype(BF16), ssm_glu_b[l], ssm_out_norm_w[l],
                     w_out[l].astype(BF16), tm=ATTN_TILE, attn_tiles=seq // ATTN_TILE)

        kv_gain = jnp.concatenate([jnp.tile(xk_norm_w[l].astype(F32), X_HEADS), jnp.ones((d,), F32)])
        kv = _norm_matmul(mem2, mem_norm_w[l], xkv_w[l], kv_gain,
                          n_norm_cols=d, chunk=x_hd, tm=batch * mem_len, tn=512, name="kv_proj")
        q_gain = jnp.tile(xq_norm_w[l].astype(F32) * (x_hd ** -0.5), X_HEADS)
        q = _norm_matmul(h, norm2_w[l], xq_w[l].astype(BF16), q_gain,
                         n_norm_cols=d, chunk=x_hd, tm=1024, tn=2048, name="xq_proj")
        r_w = jnp.concatenate([router_coarse_w[l].astype(F32), router_fine_w[l].astype(F32)], axis=1)
        r_w = jnp.pad(r_w, ((0, 0), (0, ROUTE_LANES - r_w.shape[1])))
        r_hi = r_w.astype(BF16)
        r_lo = (r_w - r_hi.astype(F32)).astype(BF16)
        r_b = jnp.concatenate([router_coarse_b[l].astype(F32), router_fine_b[l].astype(F32)])
        r_b = jnp.pad(r_b, (0, ROUTE_LANES - r_b.shape[0])).reshape(1, ROUTE_LANES)
        h2, hn3, eid_t, wts_t = _xattn_route(q, kv, h, xo_w[l].astype(BF16), norm3_w[l], r_hi, r_lo, r_b,
                                             batch=batch, seq=seq, mem_len=mem_len, tm=512)

        def per_token(x_t):
            return x_t.reshape(-1, SUBLANES, x_t.shape[1])[:, :TOP_K_FINE, :].transpose(0, 2, 1).reshape(t, TOP_K_FINE)

        eid = per_token(eid_t)
        wts = jnp.pad(per_token(wts_t), ((0, 0), (0, ROUTE_LANES - TOP_K_FINE)))

        meta, tok_sorted, pos, n_used = _dispatch(eid, t)
        y = _moe_experts(hn3, expert_w_gate[l], expert_w_up[l], expert_w_down[l], meta, tok_sorted, n_used)
        h = _combine(y, h2, wts, pos, tm=256)

    return h.reshape(batch, seq, d)
```

```python
import functools
import math

import jax
import jax.numpy as jnp
from jax import lax
from jax.experimental import pallas as pl
from jax.experimental.pallas import tpu as pltpu

F32 = jnp.float32
BF16 = jnp.bfloat16

EPS = 1e-6
DA_HEADS = 4
DA_QK_DIM = 128
DA_V_DIM = 256
SSM_GROUP = 16
SSM_STATE = 64
X_HEADS = 4
MOE_GROUPS = 8
EXP_PER_GROUP = 8
N_EXPERTS = MOE_GROUPS * EXP_PER_GROUP
TOP_K_FINE = 2

LANES = 128
SUBLANES = 8
MXU_TILE = 256
VMEM_LIMIT = 56 * 1024 * 1024
NEG = -1e30
LOG2E = math.log2(math.e)

SSM_CHUNK_GROUPS = LANES // SSM_GROUP
SSM_CHUNK_STATE = SSM_CHUNK_GROUPS * SSM_STATE
SCAN_SEGS = SUBLANES
ATTN_TILE = 512
PROJ_ROW_TILE = 1024
PROJ_COL_TILE = 2048
KV_COL_TILE = 1024
XATTN_ROW_TILE = 512
COMBINE_ROW_TILE = 512
S5_ROW_BLOCK = 256
MOE_BLOCK = 256
ROW_GROUP = SUBLANES
WEIGHT_DMA_CHUNKS = 4
ROUTE_LANES = LANES


def _rms(x, eps=EPS):
    return x * lax.rsqrt(jnp.mean(x * x, axis=-1, keepdims=True) + eps)


def _dot(a, b):
    return jnp.dot(a, b, preferred_element_type=F32)


def _dot_nt(a, b):
    return lax.dot_general(a, b, (((1,), (1,)), ((), ())), preferred_element_type=F32)


def _pack_halves(x):
    n = x.shape[1] // 2
    hi = lax.bitcast_convert_type(x[:, :n].astype(BF16).astype(F32), jnp.uint32)
    lo = lax.bitcast_convert_type(x[:, n:].astype(BF16).astype(F32), jnp.uint32)
    return hi | lax.shift_right_logical(lo, jnp.uint32(16))


def _unpack_halves(p):
    hi = lax.bitcast_convert_type(p & jnp.uint32(0xFFFF0000), F32)
    lo = lax.bitcast_convert_type(lax.shift_left(p, jnp.uint32(16)), F32)
    return hi, lo


def _resident(shape, index_map):
    return pl.BlockSpec(shape, index_map, pipeline_mode=pl.Buffered(1))


def _norm_matmul_kernel(x_ref, nw_ref, w_ref, g_ref, o_ref, xn_ref, *, n_norm_tiles, chunk):
    j = pl.program_id(1)

    @pl.when(j == 0)
    def _():
        x = x_ref[...].astype(F32)
        xn_ref[...] = (_rms(x) * nw_ref[...]).astype(BF16)

    normed = j < n_norm_tiles
    tn = w_ref.shape[1]
    sub = max(chunk, MXU_TILE)
    for s in range(tn // sub):
        w = w_ref[:, s * sub:(s + 1) * sub]
        if w.dtype != BF16:
            w = w.astype(BF16)
        acc = _dot(xn_ref[...], w)
        for c in range(sub // chunk):
            lo = s * sub + c * chunk
            a = acc[:, c * chunk:(c + 1) * chunk]
            inv = lax.rsqrt(jnp.mean(a * a, axis=-1, keepdims=True) + EPS)
            scale = jnp.where(normed, inv, 1.0)
            o_ref[:, lo:lo + chunk] = (a * scale * g_ref[:, lo:lo + chunk]).astype(o_ref.dtype)


def _norm_matmul(x, norm_w, w, gain, *, n_norm_cols, chunk, tm, tn, name):
    m, k = x.shape
    n = w.shape[1]
    assert m % tm == 0 and n % tn == 0 and tn % max(chunk, MXU_TILE) == 0 and n_norm_cols % tn == 0
    kern = functools.partial(_norm_matmul_kernel, n_norm_tiles=n_norm_cols // tn, chunk=chunk)
    return pl.pallas_call(
        kern,
        grid=(m // tm, n // tn),
        in_specs=[
            pl.BlockSpec((tm, k), lambda i, j: (i, 0)),
            pl.BlockSpec((1, k), lambda i, j: (0, 0)),
            pl.BlockSpec((k, tn), lambda i, j: (0, j)),
            pl.BlockSpec((1, tn), lambda i, j: (0, j)),
        ],
        out_specs=pl.BlockSpec((tm, tn), lambda i, j: (i, j)),
        out_shape=jax.ShapeDtypeStruct((m, n), BF16),
        scratch_shapes=[pltpu.VMEM((tm, k), BF16)],
        compiler_params=pltpu.CompilerParams(
            dimension_semantics=("parallel", "arbitrary"), vmem_limit_bytes=VMEM_LIMIT),
        name=name,
    )(x, norm_w.reshape(1, k).astype(F32), w, gain.reshape(1, n).astype(F32))


def _diff_attn_kernel(lam_ref, qa_ref, qb_ref, k_ref, v_ref, g_ref, o_ref, *stat_refs, tq, n_q):
    pair = pl.program_id(2)
    stats_a = (stat_refs[0:3], stat_refs[3:6])
    stats_b = (stat_refs[6:9], stat_refs[9:12])

    def scores(q_ref, j):
        return tuple(_dot_nt(q_ref[:, c * DA_QK_DIM:(c + 1) * DA_QK_DIM],
                             k_ref[j * tq:(j + 1) * tq, c * DA_QK_DIM:(c + 1) * DA_QK_DIM])
                     for c in range(2))

    def accumulate(stats, j, s_pair, masked):
        for s, (m_ref, l_ref, acc_ref) in zip(s_pair, stats):
            if masked:
                row = lax.broadcasted_iota(jnp.int32, s.shape, 0)
                col = lax.broadcasted_iota(jnp.int32, s.shape, 1)
                s = jnp.where(col <= row, s, NEG)
            m_old = m_ref[...]
            m_new = jnp.maximum(m_old, jnp.max(s, axis=-1, keepdims=True))
            p = jnp.exp2(s - m_new)
            alpha = jnp.exp2(m_old - m_new)
            l_ref[...] = alpha * l_ref[...] + jnp.sum(p, axis=-1, keepdims=True)
            acc_ref[...] = alpha * acc_ref[...] + _dot(p.astype(BF16), v_ref[j * tq:(j + 1) * tq, :])
            m_ref[...] = m_new

    def finish(stats, rows):
        (_, l1, acc1), (_, l2, acc2) = stats
        o = acc1[...] / l1[...] - lam_ref[0] * (acc2[...] / l2[...])
        o_ref[rows, :] = (_rms(o) * g_ref[...]).astype(o_ref.dtype)

    def run(p):
        tiles = ((qa_ref, stats_a, p), (qb_ref, stats_b, n_q - 1 - p))
        for _, stats, _ in tiles:
            for m_ref, l_ref, acc_ref in stats:
                m_ref[...] = jnp.full(m_ref.shape, NEG, F32)
                l_ref[...] = jnp.zeros(l_ref.shape, F32)
                acc_ref[...] = jnp.zeros(acc_ref.shape, F32)
        pending = [scores(q_ref, 0) for q_ref, _, _ in tiles]
        for j in range(n_q - p):
            for idx, (q_ref, stats, diag) in enumerate(tiles):
                if j > diag:
                    continue
                s_pair = pending[idx]
                if j < diag:
                    pending[idx] = scores(q_ref, j + 1)
                accumulate(stats, j, s_pair, masked=(j == diag))
        finish(stats_a, slice(0, tq))
        finish(stats_b, slice(tq, 2 * tq))

    for p in range(n_q // 2):
        pl.when(pair == p)(functools.partial(run, p))


def _attn_tile_pos(tile, nq):
    b, qt = tile // nq, tile % nq
    return b * nq + jnp.where(qt < nq // 2, 2 * qt, 2 * (nq - 1 - qt) + 1)


def _diff_attn(proj, lam, gain, *, batch, seq, tq):
    t = batch * seq
    nq = seq // tq
    assert nq % 2 == 0
    width = 2 * DA_QK_DIM
    k_blk0 = DA_HEADS
    v_blk0 = 2 * DA_HEADS
    kern = functools.partial(_diff_attn_kernel, tq=tq, n_q=nq)
    stat = [pltpu.VMEM((tq, 1), F32), pltpu.VMEM((tq, 1), F32), pltpu.VMEM((tq, DA_V_DIM), F32)]
    return pl.pallas_call(
        kern,
        grid_spec=pltpu.PrefetchScalarGridSpec(
            num_scalar_prefetch=1,
            grid=(batch, DA_HEADS, nq // 2),
            in_specs=[
                pl.BlockSpec((tq, width), lambda b, h, p, lam: (b * nq + p, h)),
                pl.BlockSpec((tq, width), lambda b, h, p, lam: (b * nq + nq - 1 - p, h)),
                pl.BlockSpec((seq, width), lambda b, h, p, lam: (b, k_blk0 + h)),
                pl.BlockSpec((seq, width), lambda b, h, p, lam: (b, v_blk0 + h)),
                pl.BlockSpec((1, DA_V_DIM), lambda b, h, p, lam: (0, 0)),
            ],
            out_specs=pl.BlockSpec((2 * tq, DA_V_DIM), lambda b, h, p, lam: (b * (nq // 2) + p, h)),
            scratch_shapes=stat * 4,
        ),
        out_shape=jax.ShapeDtypeStruct((t, DA_HEADS * DA_V_DIM), BF16),
        compiler_params=pltpu.CompilerParams(
            dimension_semantics=("parallel", "parallel", "arbitrary"), vmem_limit_bytes=VMEM_LIMIT),
        name="diff_attn",
    )(lam, proj, proj, proj, proj, gain)


def _s5_kernel(u_ref, bd_ref, ar_ref, ai_ref, cd_ref, d_ref, o_ref, xs_ref, us_ref, ys_ref, *, seq, rows):
    ns = SSM_CHUNK_STATE
    seg_len = seq // SCAN_SEGS
    n_row_blk = seq // rows
    steps = rows // SCAN_SEGS

    for seg in range(SCAN_SEGS):
        us_ref[pl.ds(seg, seg_len, stride=SCAN_SEGS), :] = (
            u_ref[seg * seg_len:(seg + 1) * seg_len, :].astype(F32))

    def in_map(r):
        rs = slice(r * rows, (r + 1) * rows)
        xs_ref[rs, :] = _dot(us_ref[rs, :].astype(BF16), bd_ref[...])

    def out_map(r):
        rs = slice(r * rows, (r + 1) * rows)
        y = _dot(xs_ref[rs, :].astype(BF16), cd_ref[...]) + d_ref[...] * us_ref[rs, :]
        ys_ref[rs, :] = jax.nn.gelu(y)
        for seg in range(SCAN_SEGS):
            t0 = seg * seg_len + r * steps
            o_ref[t0:t0 + steps, :] = (
                ys_ref[pl.ds(r * rows + seg, steps, stride=SCAN_SEGS), :].astype(o_ref.dtype))

    ar = jnp.broadcast_to(ar_ref[...], (SCAN_SEGS, ns))
    ai = jnp.broadcast_to(ai_ref[...], (SCAN_SEGS, ns))

    def advance(t, sr, si):
        ts = slice(t * SCAN_SEGS, (t + 1) * SCAN_SEGS)
        return ar * sr - ai * si + xs_ref[ts, 0:ns], ar * si + ai * sr + xs_ref[ts, ns:2 * ns]

    in_map(0)
    fr = fi = jnp.zeros((SCAN_SEGS, ns), F32)
    for r in range(n_row_blk):
        if r + 1 < n_row_blk:
            in_map(r + 1)
        for t in range(r * steps, (r + 1) * steps):
            fr, fi = advance(t, fr, fi)

    pr, pi = ar, ai
    for _ in range(int(math.log2(seg_len))):
        pr, pi = pr * pr - pi * pi, 2.0 * pr * pi
    seg = lax.broadcasted_iota(jnp.int32, (SCAN_SEGS, ns), 0)

    def shifted(x, k):
        return jnp.where(seg >= k, pltpu.roll(x, k, 0), 0.0)

    k = 1
    while k < SCAN_SEGS:
        gr, gi = shifted(fr, k), shifted(fi, k)
        fr, fi = fr + pr * gr - pi * gi, fi + pr * gi + pi * gr
        pr, pi = pr * pr - pi * pi, 2.0 * pr * pi
        k *= 2
    sr, si = shifted(fr, 1), shifted(fi, 1)

    for r in range(n_row_blk):
        for t in range(r * steps, (r + 1) * steps):
            sr, si = advance(t, sr, si)
            ts = slice(t * SCAN_SEGS, (t + 1) * SCAN_SEGS)
            xs_ref[ts, 0:ns] = sr
            xs_ref[ts, ns:2 * ns] = si
        if r >= 1:
            out_map(r - 1)
    out_map(n_row_blk - 1)


def _s5(proj, u_col0, d_ssm, bd, a_re, a_im, cd, d_skip, *, batch, seq, rows=S5_ROW_BLOCK):
    n_chunks = d_ssm // LANES
    u_blk0 = u_col0 // LANES
    kern = functools.partial(_s5_kernel, seq=seq, rows=rows)
    return pl.pallas_call(
        kern,
        grid=(batch, n_chunks),
        in_specs=[
            pl.BlockSpec((seq, LANES), lambda b, c: (b, u_blk0 + c)),
            pl.BlockSpec((None, LANES, 2 * SSM_CHUNK_STATE), lambda b, c: (c, 0, 0)),
            pl.BlockSpec((None, 1, SSM_CHUNK_STATE), lambda b, c: (c, 0, 0)),
            pl.BlockSpec((None, 1, SSM_CHUNK_STATE), lambda b, c: (c, 0, 0)),
            pl.BlockSpec((None, 2 * SSM_CHUNK_STATE, LANES), lambda b, c: (c, 0, 0)),
            pl.BlockSpec((None, 1, LANES), lambda b, c: (c, 0, 0)),
        ],
        out_specs=pl.BlockSpec((seq, LANES), lambda b, c: (b, c)),
        out_shape=jax.ShapeDtypeStruct((batch * seq, d_ssm), BF16),
        scratch_shapes=[pltpu.VMEM((seq, 2 * SSM_CHUNK_STATE), F32),
                        pltpu.VMEM((seq, LANES), F32), pltpu.VMEM((seq, LANES), F32)],
        compiler_params=pltpu.CompilerParams(
            dimension_semantics=("parallel", "parallel"), vmem_limit_bytes=VMEM_LIMIT),
        name="s5_scan",
    )(proj, bd, a_re, a_im, cd, d_skip)


def _s5_params(lam_re, lam_im, log_dt, b_re, b_im, c_re, c_im, d_skip):
    g = lam_re.shape[0]
    nc = g // SSM_CHUNK_GROUPS
    lr = jnp.minimum(lam_re.astype(F32), -1e-4)
    li = lam_im.astype(F32)
    dt = jnp.exp(log_dt.astype(F32))[:, None]
    mag = jnp.exp(lr * dt)
    lb_re, lb_im = mag * jnp.cos(li * dt), mag * jnp.sin(li * dt)
    den = lr * lr + li * li
    coef_re = ((lb_re - 1.0) * lr + lb_im * li) / den
    coef_im = (lb_im * lr - (lb_re - 1.0) * li) / den
    br, bi = b_re.astype(F32), b_im.astype(F32)
    bb_re = coef_re[..., None] * br - coef_im[..., None] * bi
    bb_im = coef_re[..., None] * bi + coef_im[..., None] * br
    eye = jnp.eye(SSM_CHUNK_GROUPS, dtype=F32)

    def pack_in(bb):
        bb = bb.reshape(nc, SSM_CHUNK_GROUPS, SSM_STATE, SSM_GROUP)
        return jnp.einsum('cgph,gk->cghkp', bb, eye).reshape(nc, LANES, SSM_CHUNK_STATE)

    def pack_out(cc):
        cc = cc.astype(F32).reshape(nc, SSM_CHUNK_GROUPS, SSM_GROUP, SSM_STATE)
        return jnp.einsum('cghp,gk->ckpgh', cc, eye).reshape(nc, SSM_CHUNK_STATE, LANES)

    bd = jnp.concatenate([pack_in(bb_re), pack_in(bb_im)], axis=-1).astype(BF16)
    cd = jnp.concatenate([pack_out(c_re), -pack_out(c_im)], axis=1).astype(BF16)
    a_re = lb_re.reshape(nc, 1, SSM_CHUNK_STATE)
    a_im = lb_im.reshape(nc, 1, SSM_CHUNK_STATE)
    dd = d_skip.astype(F32).reshape(nc, 1, LANES)
    return bd, a_re, a_im, cd, dd


def _mix_out_kernel(a_ref, y_ref, x_ref, gw_ref, gb_ref, nw_ref, wo_ref, o_ref):
    d_attn = a_ref.shape[1]
    y = y_ref[...]
    gate = _dot(y, gw_ref[...]) + gb_ref[...]
    s = y.astype(F32) * jax.nn.sigmoid(gate)
    sn = (_rms(s) * nw_ref[...]).astype(BF16)
    acc = _dot(a_ref[...], wo_ref[0:d_attn, :]) + _dot(sn, wo_ref[d_attn:, :])
    o_ref[...] = x_ref[...] + acc


def _mix_out(a, y, x, glu_w, glu_b, norm_w, w_out, *, tm, attn_tiles):
    t, d = x.shape
    d_attn, d_ssm = a.shape[1], y.shape[1]
    const = lambda i: (0, 0)
    return pl.pallas_call(
        _mix_out_kernel,
        grid=(t // tm,),
        in_specs=[
            pl.BlockSpec((tm, d_attn), lambda i: (_attn_tile_pos(i, attn_tiles), 0)),
            pl.BlockSpec((tm, d_ssm), lambda i: (i, 0)),
            pl.BlockSpec((tm, d), lambda i: (i, 0)),
            _resident((d_ssm, d_ssm), const),
            _resident((1, d_ssm), const),
            _resident((1, d_ssm), const),
            _resident((d, d), const),
        ],
        out_specs=pl.BlockSpec((tm, d), lambda i: (i, 0)),
        out_shape=jax.ShapeDtypeStruct((t, d), F32),
        compiler_params=pltpu.CompilerParams(
            dimension_semantics=("parallel",), vmem_limit_bytes=VMEM_LIMIT),
        name="mix_out",
    )(a, y, x, glu_w, glu_b.reshape(1, d_ssm).astype(F32), norm_w.reshape(1, d_ssm).astype(F32), w_out)


def _xattn_route_kernel(q_ref, k_ref, v_ref, h_ref, xo_ref, nw_ref, rhi_ref, rlo_ref, rb_ref,
                        h2_ref, hn_ref, eid_ref, wts_ref):
    d = h_ref.shape[1]
    hd = d // X_HEADS
    h2 = h_ref[...]
    for h in range(X_HEADS):
        sl = slice(h * hd, (h + 1) * hd)
        s = _dot_nt(q_ref[:, sl], k_ref[:, sl])
        p = jnp.exp(s - jnp.max(s, axis=-1, keepdims=True))
        p = p * (1.0 / jnp.sum(p, axis=-1, keepdims=True))
        o = _dot(p.astype(BF16), v_ref[:, sl]).astype(BF16)
        h2 = h2 + _dot(o, xo_ref[sl, :])
    h2_ref[...] = h2
    hn = _rms(h2) * nw_ref[...]
    hn_ref[...] = _pack_halves(hn)

    hi = hn.astype(BF16)
    lo = (hn - hi.astype(F32)).astype(BF16)
    hi_both = _dot(hi, jnp.concatenate([rhi_ref[...], rlo_ref[...]], axis=1))
    logits = (hi_both[:, :ROUTE_LANES] + hi_both[:, ROUTE_LANES:] + _dot(lo, rhi_ref[...])) + rb_ref[...]

    lt = logits.T
    idx = lax.broadcasted_iota(jnp.int32, (SUBLANES, lt.shape[1]), 0)

    def first_row(cond):
        return jnp.min(jnp.where(cond, idx, SUBLANES), axis=0, keepdims=True)

    def softmax_rows(x):
        e = jnp.exp(x - jnp.max(x, axis=0, keepdims=True))
        return e / jnp.sum(e, axis=0, keepdims=True)

    p_c = softmax_rows(lt[0:MOE_GROUPS, :])
    p_grp = jnp.max(p_c, axis=0, keepdims=True)
    grp = first_row(p_c == p_grp)
    lf = lt[MOE_GROUPS:MOE_GROUPS + EXP_PER_GROUP, :]
    for g in range(1, MOE_GROUPS):
        lo_row = MOE_GROUPS + g * EXP_PER_GROUP
        lf = jnp.where(grp == g, lt[lo_row:lo_row + EXP_PER_GROUP, :], lf)
    pf = softmax_rows(lf)
    v1 = jnp.max(pf, axis=0, keepdims=True)
    i1 = first_row(pf == v1)
    rest = idx != i1
    v2 = jnp.max(jnp.where(rest, pf, -1.0), axis=0, keepdims=True)
    i2 = first_row(rest & (pf == v2))
    tot = v1 + v2
    e1 = grp * EXP_PER_GROUP + i1
    e2 = grp * EXP_PER_GROUP + i2
    eid_ref[...] = jnp.where(idx == 0, e1, jnp.where(idx == 1, e2, 0))
    wts_ref[...] = jnp.where(idx == 0, v1 / tot * p_grp, jnp.where(idx == 1, v2 / tot * p_grp, 0.0))


def _xattn_route(q, kv, h1, xo_w, norm_w, r_hi, r_lo, r_b, *, batch, seq, mem_len, tm):
    t, d = h1.shape
    n = seq // tm
    const = lambda b, i: (0, 0)
    row = lambda b, i: (b * n + i, 0)
    return pl.pallas_call(
        _xattn_route_kernel,
        grid=(batch, n),
        in_specs=[
            pl.BlockSpec((tm, d), row),
            pl.BlockSpec((mem_len, d), lambda b, i: (b, 0)),
            pl.BlockSpec((mem_len, d), lambda b, i: (b, 1)),
            pl.BlockSpec((tm, d), row),
            _resident((d, d), const),
            _resident((1, d), const),
            _resident((d, ROUTE_LANES), const),
            _resident((d, ROUTE_LANES), const),
            _resident((1, ROUTE_LANES), const),
        ],
        out_specs=[
            pl.BlockSpec((tm, d), row),
            pl.BlockSpec((tm, d // 2), row),
            pl.BlockSpec((SUBLANES, tm), row),
            pl.BlockSpec((SUBLANES, tm), row),
        ],
        out_shape=[
            jax.ShapeDtypeStruct((t, d), F32),
            jax.ShapeDtypeStruct((t, d // 2), jnp.uint32),
            jax.ShapeDtypeStruct((t // tm * SUBLANES, tm), jnp.int32),
            jax.ShapeDtypeStruct((t // tm * SUBLANES, tm), F32),
        ],
        compiler_params=pltpu.CompilerParams(
            dimension_semantics=("parallel", "parallel"), vmem_limit_bytes=VMEM_LIMIT),
        name="xattn_route",
    )(q, kv, kv, h1, xo_w, norm_w.reshape(1, d).astype(F32), r_hi, r_lo, r_b)


def _moe_kernel(be_ref, par_ref, first_ref, nxt_ref, base_ref, nval_ref, tok_ref, nu_ref,
                hn_hbm, wg_hbm, wu_hbm, wd_hbm, o_ref,
                xbuf, wgb, wub, wdb, gsem, wsem):
    b = pl.program_id(0)
    n_used = nu_ref[0]

    def weight_copies(e, slot):
        copies = []
        for hbm, buf in ((wg_hbm, wgb), (wu_hbm, wub), (wd_hbm, wdb)):
            rows = hbm.shape[1] // WEIGHT_DMA_CHUNKS
            for c in range(WEIGHT_DMA_CHUNKS):
                sl = pl.ds(c * rows, rows)
                copies.append(pltpu.make_async_copy(hbm.at[e, sl], buf.at[slot, sl], wsem.at[slot]))
        return copies

    def groups(blk):
        return (nval_ref[blk] + ROW_GROUP - 1) // ROW_GROUP

    def start_gather(blk, slot):
        base = base_ref[blk]

        def body(g, carry):
            for r in range(ROW_GROUP):
                tok = tok_ref[base + g * ROW_GROUP + r]
                pltpu.make_async_copy(hn_hbm.at[pl.ds(tok, 1)], xbuf.at[slot, g, pl.ds(r, 1)],
                                      gsem.at[slot]).start()
            return carry
        lax.fori_loop(0, groups(blk), body, 0)

    def wait_gather(blk, slot):
        filled = xbuf.at[slot, pl.ds(0, groups(blk))]
        pltpu.make_async_copy(filled, filled, gsem.at[slot]).wait()

    @pl.when(b == 0)
    def _():
        xbuf[...] = jnp.zeros(xbuf.shape, xbuf.dtype)
        for c in weight_copies(be_ref[0], par_ref[0]):
            c.start()
        start_gather(0, 0)

    @pl.when(b < n_used)
    def _():
        slot = b % 2
        wslot = par_ref[b]
        is_first = first_ref[b] == 1

        @pl.when(is_first & (nxt_ref[b] >= 0))
        def _():
            for c in weight_copies(nxt_ref[b], 1 - wslot):
                c.start()

        @pl.when(b + 1 < n_used)
        def _():
            start_gather(b + 1, 1 - slot)

        @pl.when(is_first)
        def _():
            for c in weight_copies(0, wslot):
                c.wait()

        wait_gather(b, slot)
        half = xbuf.shape[-1]
        x_hi, x_lo = (v.astype(BF16) for v in _unpack_halves(xbuf[slot].reshape(MOE_BLOCK, half)))

        def up(w):
            return _dot(x_hi, w[wslot, 0:half, :].astype(BF16)) + _dot(x_lo, w[wslot, half:, :].astype(BF16))

        mid = (jax.nn.silu(up(wgb)) * up(wub)).astype(BF16)
        o_ref[...] = _pack_halves(_dot(mid, wdb[wslot].astype(BF16)))

    @pl.when(b >= n_used)
    def _():
        o_ref[...] = jnp.zeros(o_ref.shape, o_ref.dtype)


def _moe_experts(hn_packed, w_gate, w_up, w_down, meta, tok_sorted, n_used):
    d, d_ff = w_gate.shape[1:]
    half = hn_packed.shape[1]
    blk_exp, par, first, nxt, base, nval = meta
    n_blocks = blk_exp.shape[0]
    any_spec = pl.BlockSpec(memory_space=pl.ANY)
    return pl.pallas_call(
        _moe_kernel,
        grid_spec=pltpu.PrefetchScalarGridSpec(
            num_scalar_prefetch=8,
            grid=(n_blocks,),
            in_specs=[any_spec, any_spec, any_spec, any_spec],
            out_specs=pl.BlockSpec((MOE_BLOCK, half), lambda b, *_: (b, 0)),
            scratch_shapes=[
                pltpu.VMEM((2, MOE_BLOCK // ROW_GROUP, ROW_GROUP, half), jnp.uint32),
                pltpu.VMEM((2, d, d_ff), F32),
                pltpu.VMEM((2, d, d_ff), F32),
                pltpu.VMEM((2, d_ff, d), F32),
                pltpu.SemaphoreType.DMA((2,)),
                pltpu.SemaphoreType.DMA((2,)),
            ],
        ),
        out_shape=jax.ShapeDtypeStruct((n_blocks * MOE_BLOCK, half), jnp.uint32),
        compiler_params=pltpu.CompilerParams(
            dimension_semantics=("arbitrary",), vmem_limit_bytes=VMEM_LIMIT),
        name="moe_experts",
    )(blk_exp, par, first, nxt, base, nval, tok_sorted, n_used, hn_packed, w_gate, w_up, w_down)


def _combine_kernel(pos_ref, y_hbm, h_ref, w_ref, o_ref, ybuf, sem, *, tm):
    i = pl.program_id(0)
    n_groups = tm // ROW_GROUP

    def start_gather(tile, slot):
        base = tile * (tm * TOP_K_FINE)

        def body(g, carry):
            for r in range(ROW_GROUP):
                for k in range(TOP_K_FINE):
                    row = pos_ref[base + (g * ROW_GROUP + r) * TOP_K_FINE + k]
                    pltpu.make_async_copy(y_hbm.at[pl.ds(row, 1)], ybuf.at[slot, k, g, pl.ds(r, 1)],
                                          sem.at[slot]).start()
            return carry
        lax.fori_loop(0, n_groups, body, 0)

    @pl.when(i == 0)
    def _():
        start_gather(0, 0)

    @pl.when(i + 1 < pl.num_programs(0))
    def _():
        start_gather(i + 1, (i + 1) % 2)

    slot = i % 2
    pltpu.make_async_copy(ybuf.at[slot], ybuf.at[slot], sem.at[slot]).wait()
    w = w_ref[...]
    half = ybuf.shape[-1]
    y0 = _unpack_halves(ybuf[slot, 0].reshape(tm, half))
    y1 = _unpack_halves(ybuf[slot, 1].reshape(tm, half))
    for c in range(2):
        cols = slice(c * half, (c + 1) * half)
        o_ref[:, cols] = h_ref[:, cols] + (w[:, 0:1] * y0[c] + w[:, 1:2] * y1[c])


def _combine(y, h2, wts, pos, *, tm):
    t, d = h2.shape
    kern = functools.partial(_combine_kernel, tm=tm)
    return pl.pallas_call(
        kern,
        grid_spec=pltpu.PrefetchScalarGridSpec(
            num_scalar_prefetch=1,
            grid=(t // tm,),
            in_specs=[
                pl.BlockSpec(memory_space=pl.ANY),
                pl.BlockSpec((tm, d), lambda i, pos: (i, 0)),
                pl.BlockSpec((tm, ROUTE_LANES), lambda i, pos: (i, 0)),
            ],
            out_specs=pl.BlockSpec((tm, d), lambda i, pos: (i, 0)),
            scratch_shapes=[
                pltpu.VMEM((2, TOP_K_FINE, tm // ROW_GROUP, ROW_GROUP, y.shape[1]), jnp.uint32),
                pltpu.SemaphoreType.DMA((2,)),
            ],
        ),
        out_shape=jax.ShapeDtypeStruct((t, d), F32),
        compiler_params=pltpu.CompilerParams(
            dimension_semantics=("arbitrary",), vmem_limit_bytes=VMEM_LIMIT),
        name="moe_combine",
    )(pos, y, h2, wts)


def _lookup(table, idx):
    sel = idx[:, None] == jnp.arange(table.shape[0], dtype=jnp.int32)[None, :]
    return jnp.sum(jnp.where(sel, table[None, :], 0), axis=1).astype(jnp.int32)


def _dispatch(eid, n_tokens):
    n_assign = n_tokens * TOP_K_FINE
    experts = jnp.arange(N_EXPERTS, dtype=jnp.int32)
    e_flat = eid.reshape(n_assign)
    a_ids = jnp.arange(n_assign, dtype=jnp.int32)
    e_s, order = lax.sort_key_val(e_flat, a_ids)
    counts = jnp.sum((e_flat[:, None] == experts[None, :]).astype(jnp.int32), axis=0)
    starts = jnp.cumsum(counts) - counts
    nb = (counts + MOE_BLOCK - 1) // MOE_BLOCK
    blk_end = jnp.cumsum(nb)
    blk_start = blk_end - nb
    n_used = blk_end[-1]
    n_blocks = (n_assign + N_EXPERTS * (MOE_BLOCK - 1)) // MOE_BLOCK
    b_ids = jnp.arange(n_blocks, dtype=jnp.int32)
    used = b_ids < n_used
    blk_exp = jnp.minimum(jnp.sum((blk_end[None, :] <= b_ids[:, None]).astype(jnp.int32), axis=1),
                          N_EXPERTS - 1)
    j = b_ids - _lookup(blk_start, blk_exp)
    base = jnp.where(used, _lookup(starts, blk_exp) + j * MOE_BLOCK, 0)
    nval = jnp.where(used, jnp.clip(_lookup(counts, blk_exp) - j * MOE_BLOCK, 0, MOE_BLOCK), 0)
    first = (used & (j == 0)).astype(jnp.int32)
    active = counts > 0
    par = _lookup(jnp.cumsum(active.astype(jnp.int32)) - 1, blk_exp) & 1
    later = lax.cummin(jnp.where(active, experts, N_EXPERTS), reverse=True)
    nxt_e = jnp.concatenate([later[1:], jnp.full((1,), N_EXPERTS, jnp.int32)])
    nxt = _lookup(jnp.where(nxt_e == N_EXPERTS, -1, nxt_e), blk_exp)
    meta = tuple(v.astype(jnp.int32) for v in (blk_exp, par, first, nxt, base, nval))
    row_sorted = a_ids + _lookup(blk_start * MOE_BLOCK - starts, e_s)
    _, pos = lax.sort_key_val(order, row_sorted)
    tok_sorted = jnp.concatenate([lax.shift_right_logical(order, 1), jnp.zeros((ROW_GROUP,), jnp.int32)])
    return meta, tok_sorted, pos, n_used.astype(jnp.int32).reshape(1)


def kernel(x, mem, norm1_w, w_in, q_norm_w, k_norm_w, lambda_q1, lambda_k1, lambda_q2, lambda_k2, subln_w, ssm_lambda_re, ssm_lambda_im, ssm_log_dt, ssm_b_re, ssm_b_im, ssm_c_re, ssm_c_im, ssm_d, ssm_glu_w, ssm_glu_b, ssm_out_norm_w, w_out, norm2_w, mem_norm_w, xq_w, xkv_w, xq_norm_w, xk_norm_w, xo_w, norm3_w, router_coarse_w, router_coarse_b, router_fine_w, router_fine_b, expert_w_gate, expert_w_up, expert_w_down):
    batch, seq, d = x.shape
    mem_len = mem.shape[1]
    t = batch * seq
    depth = norm1_w.shape[0]
    d_attn = DA_HEADS * DA_V_DIM
    d_ssm = d - d_attn
    qk_cols = DA_HEADS * 2 * DA_QK_DIM
    x_hd = d // X_HEADS
    h = x.reshape(t, d)
    mem2 = mem.reshape(batch * mem_len, d)

    for l in range(depth):
        lam_init = 0.8 - 0.6 * math.exp(-0.3 * l)
        lam = (jnp.exp(jnp.sum(lambda_q1[l].astype(F32) * lambda_k1[l].astype(F32)))
               - jnp.exp(jnp.sum(lambda_q2[l].astype(F32) * lambda_k2[l].astype(F32)))
               + lam_init).reshape(1)

        n_rep = qk_cols // DA_QK_DIM
        in_gain = jnp.concatenate([
            jnp.tile(q_norm_w[l].astype(F32) * (DA_QK_DIM ** -0.5 * LOG2E), n_rep),
            jnp.tile(k_norm_w[l].astype(F32), n_rep),
            jnp.ones((d_attn + d_ssm,), F32)])
        proj = _norm_matmul(h, norm1_w[l], w_in[l].astype(BF16), in_gain,
                            n_norm_cols=2 * qk_cols, chunk=DA_QK_DIM, tm=PROJ_ROW_TILE, tn=PROJ_COL_TILE, name="in_proj")
        sub_gain = (subln_w[l].astype(F32) * (1.0 - lam_init)).reshape(1, DA_V_DIM)
        a = _diff_attn(proj, lam, sub_gain, batch=batch, seq=seq, tq=ATTN_TILE)

        bd, a_re, a_im, cd, dd = _s5_params(ssm_lambda_re[l], ssm_lambda_im[l], ssm_log_dt[l],
                                            ssm_b_re[l], ssm_b_im[l], ssm_c_re[l], ssm_c_im[l], ssm_d[l])
        y = _s5(proj, 2 * qk_cols + d_attn, d_ssm, bd, a_re, a_im, cd, dd, batch=batch, seq=seq)
        h = _mix_out(a, y, h, ssm_glu_w[l].astype(BF16), ssm_glu_b[l], ssm_out_norm_w[l],
                     w_out[l].astype(BF16), tm=ATTN_TILE, attn_tiles=seq // ATTN_TILE)

        kv_gain = jnp.concatenate([jnp.tile(xk_norm_w[l].astype(F32), X_HEADS), jnp.ones((d,), F32)])
        kv = _norm_matmul(mem2, mem_norm_w[l], xkv_w[l], kv_gain,
                          n_norm_cols=d, chunk=x_hd, tm=batch * mem_len, tn=KV_COL_TILE, name="kv_proj")
        q_gain = jnp.tile(xq_norm_w[l].astype(F32) * (x_hd ** -0.5), X_HEADS)
        q = _norm_matmul(h, norm2_w[l], xq_w[l].astype(BF16), q_gain,
                         n_norm_cols=d, chunk=x_hd, tm=PROJ_ROW_TILE, tn=PROJ_COL_TILE, name="xq_proj")
        r_w = jnp.concatenate([router_coarse_w[l].astype(F32), router_fine_w[l].astype(F32)], axis=1)
        r_w = jnp.pad(r_w, ((0, 0), (0, ROUTE_LANES - r_w.shape[1])))
        r_hi = r_w.astype(BF16)
        r_lo = (r_w - r_hi.astype(F32)).astype(BF16)
        r_b = jnp.concatenate([router_coarse_b[l].astype(F32), router_fine_b[l].astype(F32)])
        r_b = jnp.pad(r_b, (0, ROUTE_LANES - r_b.shape[0])).reshape(1, ROUTE_LANES)
        h2, hn3, eid_t, wts_t = _xattn_route(q, kv, h, xo_w[l].astype(BF16), norm3_w[l], r_hi, r_lo, r_b,
                                             batch=batch, seq=seq, mem_len=mem_len, tm=XATTN_ROW_TILE)

        def per_token(x_t):
            return x_t.reshape(-1, SUBLANES, x_t.shape[1])[:, :TOP_K_FINE, :].transpose(0, 2, 1).reshape(t, TOP_K_FINE)

        eid = per_token(eid_t)
        wts = jnp.pad(per_token(wts_t), ((0, 0), (0, ROUTE_LANES - TOP_K_FINE)))

        meta, tok_sorted, pos, n_used = _dispatch(eid, t)
        y = _moe_experts(hn3, expert_w_gate[l], expert_w_up[l], expert_w_down[l], meta, tok_sorted, n_used)
        h = _combine(y, h2, wts, pos, tm=COMBINE_ROW_TILE)

    return h.reshape(batch, seq, d)
```

```python
import functools
import math

import jax
import jax.numpy as jnp
from jax import lax
from jax.experimental import pallas as pl
from jax.experimental.pallas import tpu as pltpu

F32 = jnp.float32
BF16 = jnp.bfloat16

EPS = 1e-6
DA_HEADS = 4
DA_QK_DIM = 128
DA_V_DIM = 256
SSM_GROUP = 16
SSM_STATE = 64
X_HEADS = 4
MOE_GROUPS = 8
EXP_PER_GROUP = 8
N_EXPERTS = MOE_GROUPS * EXP_PER_GROUP
TOP_K_FINE = 2

LANES = 128
SUBLANES = 8
MXU_TILE = 256
VMEM_LIMIT = 56 * 1024 * 1024
NEG = -1e30
LOG2E = math.log2(math.e)

SSM_CHUNK_GROUPS = LANES // SSM_GROUP
SSM_CHUNK_STATE = SSM_CHUNK_GROUPS * SSM_STATE
SCAN_SEGS = SUBLANES
ATTN_TILE = 512
PROJ_ROW_TILE = 1024
PROJ_COL_TILE = 2048
KV_COL_TILE = 512
XATTN_ROW_TILE = 512
COMBINE_ROW_TILE = 256
S5_ROW_BLOCK = 256
MOE_BLOCK = 256
ROW_GROUP = SUBLANES
WEIGHT_DMA_CHUNKS = 4
ROUTE_LANES = LANES


def _rms(x, eps=EPS):
    return x * lax.rsqrt(jnp.mean(x * x, axis=-1, keepdims=True) + eps)


def _dot(a, b):
    return jnp.dot(a, b, preferred_element_type=F32)


def _dot_nt(a, b):
    return lax.dot_general(a, b, (((1,), (1,)), ((), ())), preferred_element_type=F32)


def _pack_halves(x):
    n = x.shape[1] // 2
    hi = lax.bitcast_convert_type(x[:, :n].astype(BF16).astype(F32), jnp.uint32)
    lo = lax.bitcast_convert_type(x[:, n:].astype(BF16).astype(F32), jnp.uint32)
    return hi | lax.shift_right_logical(lo, jnp.uint32(16))


def _unpack_halves(p):
    hi = lax.bitcast_convert_type(p & jnp.uint32(0xFFFF0000), F32)
    lo = lax.bitcast_convert_type(lax.shift_left(p, jnp.uint32(16)), F32)
    return hi, lo


def _resident(shape, index_map):
    return pl.BlockSpec(shape, index_map, pipeline_mode=pl.Buffered(1))


def _norm_matmul_kernel(x_ref, nw_ref, w_ref, g_ref, o_ref, xn_ref, *, n_norm_tiles, chunk):
    j = pl.program_id(1)

    @pl.when(j == 0)
    def _():
        x = x_ref[...].astype(F32)
        xn_ref[...] = (_rms(x) * nw_ref[...]).astype(BF16)

    normed = j < n_norm_tiles
    tn = w_ref.shape[1]
    sub = max(chunk, MXU_TILE)
    for s in range(tn // sub):
        w = w_ref[:, s * sub:(s + 1) * sub]
        if w.dtype != BF16:
            w = w.astype(BF16)
        acc = _dot(xn_ref[...], w)
        for c in range(sub // chunk):
            lo = s * sub + c * chunk
            a = acc[:, c * chunk:(c + 1) * chunk]
            inv = lax.rsqrt(jnp.mean(a * a, axis=-1, keepdims=True) + EPS)
            scale = jnp.where(normed, inv, 1.0)
            o_ref[:, lo:lo + chunk] = (a * scale * g_ref[:, lo:lo + chunk]).astype(o_ref.dtype)


def _norm_matmul(x, norm_w, w, gain, *, n_norm_cols, chunk, tm, tn, name):
    m, k = x.shape
    n = w.shape[1]
    assert m % tm == 0 and n % tn == 0 and tn % max(chunk, MXU_TILE) == 0 and n_norm_cols % tn == 0
    kern = functools.partial(_norm_matmul_kernel, n_norm_tiles=n_norm_cols // tn, chunk=chunk)
    return pl.pallas_call(
        kern,
        grid=(m // tm, n // tn),
        in_specs=[
            pl.BlockSpec((tm, k), lambda i, j: (i, 0)),
            pl.BlockSpec((1, k), lambda i, j: (0, 0)),
            pl.BlockSpec((k, tn), lambda i, j: (0, j)),
            pl.BlockSpec((1, tn), lambda i, j: (0, j)),
        ],
        out_specs=pl.BlockSpec((tm, tn), lambda i, j: (i, j)),
        out_shape=jax.ShapeDtypeStruct((m, n), BF16),
        scratch_shapes=[pltpu.VMEM((tm, k), BF16)],
        compiler_params=pltpu.CompilerParams(
            dimension_semantics=("parallel", "arbitrary"), vmem_limit_bytes=VMEM_LIMIT),
        name=name,
    )(x, norm_w.reshape(1, k).astype(F32), w, gain.reshape(1, n).astype(F32))


def _diff_attn_kernel(lam_ref, qa_ref, qb_ref, k_ref, v_ref, g_ref, o_ref, *stat_refs, tq, n_q):
    pair = pl.program_id(2)
    stats_a = (stat_refs[0:3], stat_refs[3:6])
    stats_b = (stat_refs[6:9], stat_refs[9:12])

    def scores(q_ref, j):
        return tuple(_dot_nt(q_ref[:, c * DA_QK_DIM:(c + 1) * DA_QK_DIM],
                             k_ref[j * tq:(j + 1) * tq, c * DA_QK_DIM:(c + 1) * DA_QK_DIM])
                     for c in range(2))

    def accumulate(stats, j, s_pair, masked):
        n_t = tq // LANES
        for s, (m_ref, l_ref, acc_ref) in zip(s_pair, stats):
            if masked:
                row = lax.broadcasted_iota(jnp.int32, s.shape, 0)
                col = lax.broadcasted_iota(jnp.int32, s.shape, 1)
                s = jnp.where(col <= row, s, NEG)
            tiles = [s[:, c * LANES:(c + 1) * LANES] for c in range(n_t)]
            fold = tiles[0]
            for t_ in tiles[1:]:
                fold = jnp.maximum(fold, t_)
            m_old = m_ref[...]
            m_new = jnp.maximum(m_old, jnp.max(fold, axis=-1, keepdims=True))
            p_tiles = [jnp.exp2(t_ - m_new) for t_ in tiles]
            psum = p_tiles[0]
            for t_ in p_tiles[1:]:
                psum = psum + t_
            alpha = jnp.exp2(m_old - m_new)
            l_ref[...] = alpha * l_ref[...] + jnp.sum(psum, axis=-1, keepdims=True)
            p = jnp.concatenate([t_.astype(BF16) for t_ in p_tiles], axis=1)
            pv = _dot(p, v_ref[j * tq:(j + 1) * tq, :])
            for c in range(DA_V_DIM // LANES):
                cols = slice(c * LANES, (c + 1) * LANES)
                acc_ref[:, cols] = alpha * acc_ref[:, cols] + pv[:, cols]
            m_ref[...] = m_new

    def finish(stats, rows):
        (_, l1, acc1), (_, l2, acc2) = stats
        o = acc1[...] / l1[:, 0:1] - lam_ref[0] * (acc2[...] / l2[:, 0:1])
        o_ref[rows, :] = (_rms(o) * g_ref[...]).astype(o_ref.dtype)

    def run(p):
        tiles = ((qa_ref, stats_a, p), (qb_ref, stats_b, n_q - 1 - p))
        for _, stats, _ in tiles:
            for m_ref, l_ref, acc_ref in stats:
                m_ref[...] = jnp.full(m_ref.shape, NEG, F32)
                l_ref[...] = jnp.zeros(l_ref.shape, F32)
                acc_ref[...] = jnp.zeros(acc_ref.shape, F32)
        pending = [scores(q_ref, 0) for q_ref, _, _ in tiles]
        for j in range(n_q - p):
            for idx, (q_ref, stats, diag) in enumerate(tiles):
                if j > diag:
                    continue
                s_pair = pending[idx]
                if j < diag:
                    pending[idx] = scores(q_ref, j + 1)
                accumulate(stats, j, s_pair, masked=(j == diag))
        finish(stats_a, slice(0, tq))
        finish(stats_b, slice(tq, 2 * tq))

    for p in range(n_q // 2):
        pl.when(pair == p)(functools.partial(run, p))


def _attn_tile_pos(tile, nq):
    b, qt = tile // nq, tile % nq
    return b * nq + jnp.where(qt < nq // 2, 2 * qt, 2 * (nq - 1 - qt) + 1)


def _diff_attn(proj, lam, gain, *, batch, seq, tq):
    t = batch * seq
    nq = seq // tq
    assert nq % 2 == 0
    width = 2 * DA_QK_DIM
    k_blk0 = DA_HEADS
    v_blk0 = 2 * DA_HEADS
    kern = functools.partial(_diff_attn_kernel, tq=tq, n_q=nq)
    stat = [pltpu.VMEM((tq, LANES), F32), pltpu.VMEM((tq, LANES), F32), pltpu.VMEM((tq, DA_V_DIM), F32)]
    return pl.pallas_call(
        kern,
        grid_spec=pltpu.PrefetchScalarGridSpec(
            num_scalar_prefetch=1,
            grid=(batch, DA_HEADS, nq // 2),
            in_specs=[
                pl.BlockSpec((tq, width), lambda b, h, p, lam: (b * nq + p, h)),
                pl.BlockSpec((tq, width), lambda b, h, p, lam: (b * nq + nq - 1 - p, h)),
                pl.BlockSpec((seq, width), lambda b, h, p, lam: (b, k_blk0 + h)),
                pl.BlockSpec((seq, width), lambda b, h, p, lam: (b, v_blk0 + h)),
                pl.BlockSpec((1, DA_V_DIM), lambda b, h, p, lam: (0, 0)),
            ],
            out_specs=pl.BlockSpec((2 * tq, DA_V_DIM), lambda b, h, p, lam: (b * (nq // 2) + p, h)),
            scratch_shapes=stat * 4,
        ),
        out_shape=jax.ShapeDtypeStruct((t, DA_HEADS * DA_V_DIM), BF16),
        compiler_params=pltpu.CompilerParams(
            dimension_semantics=("parallel", "parallel", "arbitrary"), vmem_limit_bytes=VMEM_LIMIT),
        name="diff_attn",
    )(lam, proj, proj, proj, proj, gain)


def _s5_kernel(u_ref, bd_ref, ar_ref, ai_ref, cd_ref, d_ref, o_ref, xs_ref, us_ref, ys_ref, *, seq, rows):
    ns = SSM_CHUNK_STATE
    seg_len = seq // SCAN_SEGS
    n_row_blk = seq // rows
    steps = rows // SCAN_SEGS

    for seg in range(SCAN_SEGS):
        us_ref[pl.ds(seg, seg_len, stride=SCAN_SEGS), :] = (
            u_ref[seg * seg_len:(seg + 1) * seg_len, :].astype(F32))

    def in_map(r):
        rs = slice(r * rows, (r + 1) * rows)
        xs_ref[rs, :] = _dot(us_ref[rs, :].astype(BF16), bd_ref[...])

    def out_map(r):
        rs = slice(r * rows, (r + 1) * rows)
        y = _dot(xs_ref[rs, :].astype(BF16), cd_ref[...]) + d_ref[...] * us_ref[rs, :]
        ys_ref[rs, :] = jax.nn.gelu(y)
        for seg in range(SCAN_SEGS):
            t0 = seg * seg_len + r * steps
            o_ref[t0:t0 + steps, :] = (
                ys_ref[pl.ds(r * rows + seg, steps, stride=SCAN_SEGS), :].astype(o_ref.dtype))

    ar = jnp.broadcast_to(ar_ref[...], (SCAN_SEGS, ns))
    ai = jnp.broadcast_to(ai_ref[...], (SCAN_SEGS, ns))

    def advance(t, sr, si):
        ts = slice(t * SCAN_SEGS, (t + 1) * SCAN_SEGS)
        return ar * sr - ai * si + xs_ref[ts, 0:ns], ar * si + ai * sr + xs_ref[ts, ns:2 * ns]

    in_map(0)
    fr = fi = jnp.zeros((SCAN_SEGS, ns), F32)
    for r in range(n_row_blk):
        if r + 1 < n_row_blk:
            in_map(r + 1)
        for t in range(r * steps, (r + 1) * steps):
            fr, fi = advance(t, fr, fi)

    pr, pi = ar, ai
    for _ in range(int(math.log2(seg_len))):
        pr, pi = pr * pr - pi * pi, 2.0 * pr * pi
    seg = lax.broadcasted_iota(jnp.int32, (SCAN_SEGS, ns), 0)

    def shifted(x, k):
        return jnp.where(seg >= k, pltpu.roll(x, k, 0), 0.0)

    k = 1
    while k < SCAN_SEGS:
        gr, gi = shifted(fr, k), shifted(fi, k)
        fr, fi = fr + pr * gr - pi * gi, fi + pr * gi + pi * gr
        pr, pi = pr * pr - pi * pi, 2.0 * pr * pi
        k *= 2
    sr, si = shifted(fr, 1), shifted(fi, 1)

    for r in range(n_row_blk):
        for t in range(r * steps, (r + 1) * steps):
            sr, si = advance(t, sr, si)
            ts = slice(t * SCAN_SEGS, (t + 1) * SCAN_SEGS)
            xs_ref[ts, 0:ns] = sr
            xs_ref[ts, ns:2 * ns] = si
        if r >= 1:
            out_map(r - 1)
    out_map(n_row_blk - 1)


def _s5(proj, u_col0, d_ssm, bd, a_re, a_im, cd, d_skip, *, batch, seq, rows=S5_ROW_BLOCK):
    n_chunks = d_ssm // LANES
    u_blk0 = u_col0 // LANES
    kern = functools.partial(_s5_kernel, seq=seq, rows=rows)
    return pl.pallas_call(
        kern,
        grid=(batch, n_chunks),
        in_specs=[
            pl.BlockSpec((seq, LANES), lambda b, c: (b, u_blk0 + c)),
            pl.BlockSpec((None, LANES, 2 * SSM_CHUNK_STATE), lambda b, c: (c, 0, 0)),
            pl.BlockSpec((None, 1, SSM_CHUNK_STATE), lambda b, c: (c, 0, 0)),
            pl.BlockSpec((None, 1, SSM_CHUNK_STATE), lambda b, c: (c, 0, 0)),
            pl.BlockSpec((None, 2 * SSM_CHUNK_STATE, LANES), lambda b, c: (c, 0, 0)),
            pl.BlockSpec((None, 1, LANES), lambda b, c: (c, 0, 0)),
        ],
        out_specs=pl.BlockSpec((seq, LANES), lambda b, c: (b, c)),
        out_shape=jax.ShapeDtypeStruct((batch * seq, d_ssm), BF16),
        scratch_shapes=[pltpu.VMEM((seq, 2 * SSM_CHUNK_STATE), F32),
                        pltpu.VMEM((seq, LANES), F32), pltpu.VMEM((seq, LANES), F32)],
        compiler_params=pltpu.CompilerParams(
            dimension_semantics=("parallel", "parallel"), vmem_limit_bytes=VMEM_LIMIT),
        name="s5_scan",
    )(proj, bd, a_re, a_im, cd, d_skip)


def _s5_params(lam_re, lam_im, log_dt, b_re, b_im, c_re, c_im, d_skip):
    g = lam_re.shape[0]
    nc = g // SSM_CHUNK_GROUPS
    lr = jnp.minimum(lam_re.astype(F32), -1e-4)
    li = lam_im.astype(F32)
    dt = jnp.exp(log_dt.astype(F32))[:, None]
    mag = jnp.exp(lr * dt)
    lb_re, lb_im = mag * jnp.cos(li * dt), mag * jnp.sin(li * dt)
    den = lr * lr + li * li
    coef_re = ((lb_re - 1.0) * lr + lb_im * li) / den
    coef_im = (lb_im * lr - (lb_re - 1.0) * li) / den
    br, bi = b_re.astype(F32), b_im.astype(F32)
    bb_re = coef_re[..., None] * br - coef_im[..., None] * bi
    bb_im = coef_re[..., None] * bi + coef_im[..., None] * br
    eye = jnp.eye(SSM_CHUNK_GROUPS, dtype=F32)

    def pack_in(bb):
        bb = bb.reshape(nc, SSM_CHUNK_GROUPS, SSM_STATE, SSM_GROUP)
        return jnp.einsum('cgph,gk->cghkp', bb, eye).reshape(nc, LANES, SSM_CHUNK_STATE)

    def pack_out(cc):
        cc = cc.astype(F32).reshape(nc, SSM_CHUNK_GROUPS, SSM_GROUP, SSM_STATE)
        return jnp.einsum('cghp,gk->ckpgh', cc, eye).reshape(nc, SSM_CHUNK_STATE, LANES)

    bd = jnp.concatenate([pack_in(bb_re), pack_in(bb_im)], axis=-1).astype(BF16)
    cd = jnp.concatenate([pack_out(c_re), -pack_out(c_im)], axis=1).astype(BF16)
    a_re = lb_re.reshape(nc, 1, SSM_CHUNK_STATE)
    a_im = lb_im.reshape(nc, 1, SSM_CHUNK_STATE)
    dd = d_skip.astype(F32).reshape(nc, 1, LANES)
    return bd, a_re, a_im, cd, dd


def _mix_out_kernel(a_ref, y_ref, x_ref, gw_ref, gb_ref, nw_ref, wo_ref, o_ref):
    d_attn = a_ref.shape[1]
    y = y_ref[...]
    gate = _dot(y, gw_ref[...]) + gb_ref[...]
    s = y.astype(F32) * jax.nn.sigmoid(gate)
    sn = (_rms(s) * nw_ref[...]).astype(BF16)
    acc = _dot(a_ref[...], wo_ref[0:d_attn, :]) + _dot(sn, wo_ref[d_attn:, :])
    o_ref[...] = x_ref[...] + acc


def _mix_out(a, y, x, glu_w, glu_b, norm_w, w_out, *, tm, attn_tiles):
    t, d = x.shape
    d_attn, d_ssm = a.shape[1], y.shape[1]
    const = lambda i: (0, 0)
    return pl.pallas_call(
        _mix_out_kernel,
        grid=(t // tm,),
        in_specs=[
            pl.BlockSpec((tm, d_attn), lambda i: (_attn_tile_pos(i, attn_tiles), 0)),
            pl.BlockSpec((tm, d_ssm), lambda i: (i, 0)),
            pl.BlockSpec((tm, d), lambda i: (i, 0)),
            _resident((d_ssm, d_ssm), const),
            _resident((1, d_ssm), const),
            _resident((1, d_ssm), const),
            _resident((d, d), const),
        ],
        out_specs=pl.BlockSpec((tm, d), lambda i: (i, 0)),
        out_shape=jax.ShapeDtypeStruct((t, d), F32),
        compiler_params=pltpu.CompilerParams(
            dimension_semantics=("parallel",), vmem_limit_bytes=VMEM_LIMIT),
        name="mix_out",
    )(a, y, x, glu_w, glu_b.reshape(1, d_ssm).astype(F32), norm_w.reshape(1, d_ssm).astype(F32), w_out)


def _xattn_route_kernel(q_ref, k_ref, v_ref, h_ref, xo_ref, nw_ref, rhi_ref, rlo_ref, rb_ref,
                        h2_ref, hn_ref, eid_ref, wts_ref):
    d = h_ref.shape[1]
    hd = d // X_HEADS
    h2 = h_ref[...]
    for h in range(X_HEADS):
        sl = slice(h * hd, (h + 1) * hd)
        s = _dot_nt(q_ref[:, sl], k_ref[:, sl])
        p = jnp.exp(s - jnp.max(s, axis=-1, keepdims=True))
        p = p * (1.0 / jnp.sum(p, axis=-1, keepdims=True))
        o = _dot(p.astype(BF16), v_ref[:, sl]).astype(BF16)
        h2 = h2 + _dot(o, xo_ref[sl, :])
    h2_ref[...] = h2
    hn = _rms(h2) * nw_ref[...]
    hn_ref[...] = _pack_halves(hn)

    hi = hn.astype(BF16)
    lo = (hn - hi.astype(F32)).astype(BF16)
    hi_both = _dot(hi, jnp.concatenate([rhi_ref[...], rlo_ref[...]], axis=1))
    logits = (hi_both[:, :ROUTE_LANES] + hi_both[:, ROUTE_LANES:] + _dot(lo, rhi_ref[...])) + rb_ref[...]

    lt = logits.T
    idx = lax.broadcasted_iota(jnp.int32, (SUBLANES, lt.shape[1]), 0)

    def first_row(cond):
        return jnp.min(jnp.where(cond, idx, SUBLANES), axis=0, keepdims=True)

    def softmax_rows(x):
        e = jnp.exp(x - jnp.max(x, axis=0, keepdims=True))
        return e / jnp.sum(e, axis=0, keepdims=True)

    p_c = softmax_rows(lt[0:MOE_GROUPS, :])
    p_grp = jnp.max(p_c, axis=0, keepdims=True)
    grp = first_row(p_c == p_grp)
    lf = lt[MOE_GROUPS:MOE_GROUPS + EXP_PER_GROUP, :]
    for g in range(1, MOE_GROUPS):
        lo_row = MOE_GROUPS + g * EXP_PER_GROUP
        lf = jnp.where(grp == g, lt[lo_row:lo_row + EXP_PER_GROUP, :], lf)
    pf = softmax_rows(lf)
    v1 = jnp.max(pf, axis=0, keepdims=True)
    i1 = first_row(pf == v1)
    rest = idx != i1
    v2 = jnp.max(jnp.where(rest, pf, -1.0), axis=0, keepdims=True)
    i2 = first_row(rest & (pf == v2))
    tot = v1 + v2
    e1 = grp * EXP_PER_GROUP + i1
    e2 = grp * EXP_PER_GROUP + i2
    eid_ref[...] = jnp.where(idx == 0, e1, jnp.where(idx == 1, e2, 0))
    wts_ref[...] = jnp.where(idx == 0, v1 / tot * p_grp, jnp.where(idx == 1, v2 / tot * p_grp, 0.0))


def _xattn_route(q, kv, h1, xo_w, norm_w, r_hi, r_lo, r_b, *, batch, seq, mem_len, tm):
    t, d = h1.shape
    n = seq // tm
    const = lambda b, i: (0, 0)
    row = lambda b, i: (b * n + i, 0)
    return pl.pallas_call(
        _xattn_route_kernel,
        grid=(batch, n),
        in_specs=[
            pl.BlockSpec((tm, d), row),
            pl.BlockSpec((mem_len, d), lambda b, i: (b, 0)),
            pl.BlockSpec((mem_len, d), lambda b, i: (b, 1)),
            pl.BlockSpec((tm, d), row),
            _resident((d, d), const),
            _resident((1, d), const),
            _resident((d, ROUTE_LANES), const),
            _resident((d, ROUTE_LANES), const),
            _resident((1, ROUTE_LANES), const),
        ],
        out_specs=[
            pl.BlockSpec((tm, d), row),
            pl.BlockSpec((tm, d // 2), row),
            pl.BlockSpec((SUBLANES, tm), row),
            pl.BlockSpec((SUBLANES, tm), row),
        ],
        out_shape=[
            jax.ShapeDtypeStruct((t, d), F32),
            jax.ShapeDtypeStruct((t, d // 2), jnp.uint32),
            jax.ShapeDtypeStruct((t // tm * SUBLANES, tm), jnp.int32),
            jax.ShapeDtypeStruct((t // tm * SUBLANES, tm), F32),
        ],
        compiler_params=pltpu.CompilerParams(
            dimension_semantics=("parallel", "parallel"), vmem_limit_bytes=VMEM_LIMIT),
        name="xattn_route",
    )(q, kv, kv, h1, xo_w, norm_w.reshape(1, d).astype(F32), r_hi, r_lo, r_b)


def _moe_kernel(be_ref, par_ref, first_ref, nxt_ref, base_ref, nval_ref, tok_ref, nu_ref,
                hn_hbm, wg_hbm, wu_hbm, wd_hbm, o_ref,
                xbuf, wgb, wub, wdb, gsem, wsem):
    b = pl.program_id(0)
    n_used = nu_ref[0]

    def weight_copies(e, slot):
        copies = []
        for hbm, buf in ((wg_hbm, wgb), (wu_hbm, wub), (wd_hbm, wdb)):
            rows = hbm.shape[1] // WEIGHT_DMA_CHUNKS
            for c in range(WEIGHT_DMA_CHUNKS):
                sl = pl.ds(c * rows, rows)
                copies.append(pltpu.make_async_copy(hbm.at[e, sl], buf.at[slot, sl], wsem.at[slot]))
        return copies

    def groups(blk):
        return (nval_ref[blk] + ROW_GROUP - 1) // ROW_GROUP

    def start_gather(blk, slot):
        base = base_ref[blk]

        def body(g, carry):
            for r in range(ROW_GROUP):
                tok = tok_ref[base + g * ROW_GROUP + r]
                pltpu.make_async_copy(hn_hbm.at[pl.ds(tok, 1)], xbuf.at[slot, g, pl.ds(r, 1)],
                                      gsem.at[slot]).start()
            return carry
        lax.fori_loop(0, groups(blk), body, 0)

    def wait_gather(blk, slot):
        filled = xbuf.at[slot, pl.ds(0, groups(blk))]
        pltpu.make_async_copy(filled, filled, gsem.at[slot]).wait()

    @pl.when(b == 0)
    def _():
        xbuf[...] = jnp.zeros(xbuf.shape, xbuf.dtype)
        for c in weight_copies(be_ref[0], par_ref[0]):
            c.start()
        start_gather(0, 0)

    @pl.when(b < n_used)
    def _():
        slot = b % 2
        wslot = par_ref[b]
        is_first = first_ref[b] == 1

        @pl.when(is_first & (nxt_ref[b] >= 0))
        def _():
            for c in weight_copies(nxt_ref[b], 1 - wslot):
                c.start()

        @pl.when(b + 1 < n_used)
        def _():
            start_gather(b + 1, 1 - slot)

        @pl.when(is_first)
        def _():
            for c in weight_copies(0, wslot):
                c.wait()

        wait_gather(b, slot)
        half = xbuf.shape[-1]
        x_hi, x_lo = (v.astype(BF16) for v in _unpack_halves(xbuf[slot].reshape(MOE_BLOCK, half)))

        def up(w):
            return _dot(x_hi, w[wslot, 0:half, :].astype(BF16)) + _dot(x_lo, w[wslot, half:, :].astype(BF16))

        mid = (jax.nn.silu(up(wgb)) * up(wub)).astype(BF16)
        o_ref[...] = _pack_halves(_dot(mid, wdb[wslot].astype(BF16)))

    @pl.when(b >= n_used)
    def _():
        o_ref[...] = jnp.zeros(o_ref.shape, o_ref.dtype)


def _moe_experts(hn_packed, w_gate, w_up, w_down, meta, tok_sorted, n_used):
    d, d_ff = w_gate.shape[1:]
    half = hn_packed.shape[1]
    blk_exp, par, first, nxt, base, nval = meta
    n_blocks = blk_exp.shape[0]
    any_spec = pl.BlockSpec(memory_space=pl.ANY)
    return pl.pallas_call(
        _moe_kernel,
        grid_spec=pltpu.PrefetchScalarGridSpec(
            num_scalar_prefetch=8,
            grid=(n_blocks,),
            in_specs=[any_spec, any_spec, any_spec, any_spec],
            out_specs=pl.BlockSpec((MOE_BLOCK, half), lambda b, *_: (b, 0)),
            scratch_shapes=[
                pltpu.VMEM((2, MOE_BLOCK // ROW_GROUP, ROW_GROUP, half), jnp.uint32),
                pltpu.VMEM((2, d, d_ff), F32),
                pltpu.VMEM((2, d, d_ff), F32),
                pltpu.VMEM((2, d_ff, d), F32),
                pltpu.SemaphoreType.DMA((2,)),
                pltpu.SemaphoreType.DMA((2,)),
            ],
        ),
        out_shape=jax.ShapeDtypeStruct((n_blocks * MOE_BLOCK, half), jnp.uint32),
        compiler_params=pltpu.CompilerParams(
            dimension_semantics=("arbitrary",), vmem_limit_bytes=VMEM_LIMIT),
        name="moe_experts",
    )(blk_exp, par, first, nxt, base, nval, tok_sorted, n_used, hn_packed, w_gate, w_up, w_down)


def _combine_kernel(pos_ref, y_hbm, h_ref, w_ref, o_ref, ybuf, sem, *, tm):
    i = pl.program_id(0)
    n_groups = tm // ROW_GROUP

    def start_gather(tile, slot):
        base = tile * (tm * TOP_K_FINE)

        def body(g, carry):
            for r in range(ROW_GROUP):
                for k in range(TOP_K_FINE):
                    row = pos_ref[base + (g * ROW_GROUP + r) * TOP_K_FINE + k]
                    pltpu.make_async_copy(y_hbm.at[pl.ds(row, 1)], ybuf.at[slot, k, g, pl.ds(r, 1)],
                                          sem.at[slot]).start()
            return carry
        lax.fori_loop(0, n_groups, body, 0)

    @pl.when(i == 0)
    def _():
        start_gather(0, 0)

    @pl.when(i + 1 < pl.num_programs(0))
    def _():
        start_gather(i + 1, (i + 1) % 2)

    slot = i % 2
    pltpu.make_async_copy(ybuf.at[slot], ybuf.at[slot], sem.at[slot]).wait()
    w = w_ref[...]
    half = ybuf.shape[-1]
    y0 = _unpack_halves(ybuf[slot, 0].reshape(tm, half))
    y1 = _unpack_halves(ybuf[slot, 1].reshape(tm, half))
    for c in range(2):
        cols = slice(c * half, (c + 1) * half)
        o_ref[:, cols] = h_ref[:, cols] + (w[:, 0:1] * y0[c] + w[:, 1:2] * y1[c])


def _combine(y, h2, wts, pos, *, tm):
    t, d = h2.shape
    kern = functools.partial(_combine_kernel, tm=tm)
    return pl.pallas_call(
        kern,
        grid_spec=pltpu.PrefetchScalarGridSpec(
            num_scalar_prefetch=1,
            grid=(t // tm,),
            in_specs=[
                pl.BlockSpec(memory_space=pl.ANY),
                pl.BlockSpec((tm, d), lambda i, pos: (i, 0)),
                pl.BlockSpec((tm, ROUTE_LANES), lambda i, pos: (i, 0)),
            ],
            out_specs=pl.BlockSpec((tm, d), lambda i, pos: (i, 0)),
            scratch_shapes=[
                pltpu.VMEM((2, TOP_K_FINE, tm // ROW_GROUP, ROW_GROUP, y.shape[1]), jnp.uint32),
                pltpu.SemaphoreType.DMA((2,)),
            ],
        ),
        out_shape=jax.ShapeDtypeStruct((t, d), F32),
        compiler_params=pltpu.CompilerParams(
            dimension_semantics=("arbitrary",), vmem_limit_bytes=VMEM_LIMIT),
        name="moe_combine",
    )(pos, y, h2, wts)


def _lookup(table, idx):
    sel = idx[:, None] == jnp.arange(table.shape[0], dtype=jnp.int32)[None, :]
    return jnp.sum(jnp.where(sel, table[None, :], 0), axis=1).astype(jnp.int32)


def _dispatch(eid, n_tokens):
    n_assign = n_tokens * TOP_K_FINE
    experts = jnp.arange(N_EXPERTS, dtype=jnp.int32)
    e_flat = eid.reshape(n_assign)
    a_ids = jnp.arange(n_assign, dtype=jnp.int32)
    e_s, order = lax.sort_key_val(e_flat, a_ids)
    counts = jnp.sum((e_flat[:, None] == experts[None, :]).astype(jnp.int32), axis=0)
    starts = jnp.cumsum(counts) - counts
    nb = (counts + MOE_BLOCK - 1) // MOE_BLOCK
    blk_end = jnp.cumsum(nb)
    blk_start = blk_end - nb
    n_used = blk_end[-1]
    n_blocks = (n_assign + N_EXPERTS * (MOE_BLOCK - 1)) // MOE_BLOCK
    b_ids = jnp.arange(n_blocks, dtype=jnp.int32)
    used = b_ids < n_used
    blk_exp = jnp.minimum(jnp.sum((blk_end[None, :] <= b_ids[:, None]).astype(jnp.int32), axis=1),
                          N_EXPERTS - 1)
    j = b_ids - _lookup(blk_start, blk_exp)
    base = jnp.where(used, _lookup(starts, blk_exp) + j * MOE_BLOCK, 0)
    nval = jnp.where(used, jnp.clip(_lookup(counts, blk_exp) - j * MOE_BLOCK, 0, MOE_BLOCK), 0)
    first = (used & (j == 0)).astype(jnp.int32)
    active = counts > 0
    par = _lookup(jnp.cumsum(active.astype(jnp.int32)) - 1, blk_exp) & 1
    later = lax.cummin(jnp.where(active, experts, N_EXPERTS), reverse=True)
    nxt_e = jnp.concatenate([later[1:], jnp.full((1,), N_EXPERTS, jnp.int32)])
    nxt = _lookup(jnp.where(nxt_e == N_EXPERTS, -1, nxt_e), blk_exp)
    meta = tuple(v.astype(jnp.int32) for v in (blk_exp, par, first, nxt, base, nval))
    row_sorted = a_ids + _lookup(blk_start * MOE_BLOCK - starts, e_s)
    _, pos = lax.sort_key_val(order, row_sorted)
    tok_sorted = jnp.concatenate([lax.shift_right_logical(order, 1), jnp.zeros((ROW_GROUP,), jnp.int32)])
    return meta, tok_sorted, pos, n_used.astype(jnp.int32).reshape(1)


def kernel(x, mem, norm1_w, w_in, q_norm_w, k_norm_w, lambda_q1, lambda_k1, lambda_q2, lambda_k2, subln_w, ssm_lambda_re, ssm_lambda_im, ssm_log_dt, ssm_b_re, ssm_b_im, ssm_c_re, ssm_c_im, ssm_d, ssm_glu_w, ssm_glu_b, ssm_out_norm_w, w_out, norm2_w, mem_norm_w, xq_w, xkv_w, xq_norm_w, xk_norm_w, xo_w, norm3_w, router_coarse_w, router_coarse_b, router_fine_w, router_fine_b, expert_w_gate, expert_w_up, expert_w_down):
    batch, seq, d = x.shape
    mem_len = mem.shape[1]
    t = batch * seq
    depth = norm1_w.shape[0]
    d_attn = DA_HEADS * DA_V_DIM
    d_ssm = d - d_attn
    qk_cols = DA_HEADS * 2 * DA_QK_DIM
    x_hd = d // X_HEADS
    h = x.reshape(t, d)
    mem2 = mem.reshape(batch * mem_len, d)

    for l in range(depth):
        lam_init = 0.8 - 0.6 * math.exp(-0.3 * l)
        lam = (jnp.exp(jnp.sum(lambda_q1[l].astype(F32) * lambda_k1[l].astype(F32)))
               - jnp.exp(jnp.sum(lambda_q2[l].astype(F32) * lambda_k2[l].astype(F32)))
               + lam_init).reshape(1)

        n_rep = qk_cols // DA_QK_DIM
        in_gain = jnp.concatenate([
            jnp.tile(q_norm_w[l].astype(F32) * (DA_QK_DIM ** -0.5 * LOG2E), n_rep),
            jnp.tile(k_norm_w[l].astype(F32), n_rep),
            jnp.ones((d_attn + d_ssm,), F32)])
        proj = _norm_matmul(h, norm1_w[l], w_in[l].astype(BF16), in_gain,
                            n_norm_cols=2 * qk_cols, chunk=DA_QK_DIM, tm=PROJ_ROW_TILE, tn=PROJ_COL_TILE, name="in_proj")
        sub_gain = (subln_w[l].astype(F32) * (1.0 - lam_init)).reshape(1, DA_V_DIM)
        a = _diff_attn(proj, lam, sub_gain, batch=batch, seq=seq, tq=ATTN_TILE)

        bd, a_re, a_im, cd, dd = _s5_params(ssm_lambda_re[l], ssm_lambda_im[l], ssm_log_dt[l],
                                            ssm_b_re[l], ssm_b_im[l], ssm_c_re[l], ssm_c_im[l], ssm_d[l])
        y = _s5(proj, 2 * qk_cols + d_attn, d_ssm, bd, a_re, a_im, cd, dd, batch=batch, seq=seq)
        h = _mix_out(a, y, h, ssm_glu_w[l].astype(BF16), ssm_glu_b[l], ssm_out_norm_w[l],
                     w_out[l].astype(BF16), tm=ATTN_TILE, attn_tiles=seq // ATTN_TILE)

        kv_gain = jnp.concatenate([jnp.tile(xk_norm_w[l].astype(F32), X_HEADS), jnp.ones((d,), F32)])
        kv = _norm_matmul(mem2, mem_norm_w[l], xkv_w[l], kv_gain,
                          n_norm_cols=d, chunk=x_hd, tm=batch * mem_len, tn=KV_COL_TILE, name="kv_proj")
        q_gain = jnp.tile(xq_norm_w[l].astype(F32) * (x_hd ** -0.5), X_HEADS)
        q = _norm_matmul(h, norm2_w[l], xq_w[l].astype(BF16), q_gain,
                         n_norm_cols=d, chunk=x_hd, tm=PROJ_ROW_TILE, tn=PROJ_COL_TILE, name="xq_proj")
        r_w = jnp.concatenate([router_coarse_w[l].astype(F32), router_fine_w[l].astype(F32)], axis=1)
        r_w = jnp.pad(r_w, ((0, 0), (0, ROUTE_LANES - r_w.shape[1])))
        r_hi = r_w.astype(BF16)
        r_lo = (r_w - r_hi.astype(F32)).astype(BF16)
        r_b = jnp.concatenate([router_coarse_b[l].astype(F32), router_fine_b[l].astype(F32)])
        r_b = jnp.pad(r_b, (0, ROUTE_LANES - r_b.shape[0])).reshape(1, ROUTE_LANES)
        h2, hn3, eid_t, wts_t = _xattn_route(q, kv, h, xo_w[l].astype(BF16), norm3_w[l], r_hi, r_lo, r_b,
                                             batch=batch, seq=seq, mem_len=mem_len, tm=XATTN_ROW_TILE)

        def per_token(x_t):
            return x_t.reshape(-1, SUBLANES, x_t.shape[1])[:, :TOP_K_FINE, :].transpose(0, 2, 1).reshape(t, TOP_K_FINE)

        eid = per_token(eid_t)
        wts = jnp.pad(per_token(wts_t), ((0, 0), (0, ROUTE_LANES - TOP_K_FINE)))

        meta, tok_sorted, pos, n_used = _dispatch(eid, t)
        y = _moe_experts(hn3, expert_w_gate[l], expert_w_up[l], expert_w_down[l], meta, tok_sorted, n_used)
        h = _combine(y, h2, wts, pos, tm=COMBINE_ROW_TILE)

    return h.reshape(batch, seq, d)
```

```python
import functools
import math

import jax
import jax.numpy as jnp
from jax import lax
from jax.experimental import pallas as pl
from jax.experimental.pallas import tpu as pltpu

F32 = jnp.float32
BF16 = jnp.bfloat16

EPS = 1e-6
DA_HEADS = 4
DA_QK_DIM = 128
DA_V_DIM = 256
SSM_GROUP = 16
SSM_STATE = 64
X_HEADS = 4
MOE_GROUPS = 8
EXP_PER_GROUP = 8
N_EXPERTS = MOE_GROUPS * EXP_PER_GROUP
TOP_K_FINE = 2

LANES = 128
SUBLANES = 8
MXU_TILE = 256
VMEM_LIMIT = 56 * 1024 * 1024
NEG = -1e30
LOG2E = math.log2(math.e)

SSM_CHUNK_GROUPS = LANES // SSM_GROUP
SSM_CHUNK_STATE = SSM_CHUNK_GROUPS * SSM_STATE
SCAN_SEGS = SUBLANES
ATTN_TILE = 512
PROJ_ROW_TILE = 1024
PROJ_COL_TILE = 2048
KV_COL_TILE = 512
XATTN_ROW_TILE = 512
COMBINE_ROW_TILE = 256
S5_ROW_BLOCK = 256
MOE_BLOCK = 256
ROW_GROUP = SUBLANES
WEIGHT_DMA_CHUNKS = 4
ROUTE_LANES = LANES


def _rms(x, eps=EPS):
    return x * lax.rsqrt(jnp.mean(x * x, axis=-1, keepdims=True) + eps)


def _dot(a, b):
    return jnp.dot(a, b, preferred_element_type=F32)


def _dot_nt(a, b):
    return lax.dot_general(a, b, (((1,), (1,)), ((), ())), preferred_element_type=F32)


def _pack_halves(x):
    n = x.shape[1] // 2
    hi = lax.bitcast_convert_type(x[:, :n].astype(BF16).astype(F32), jnp.uint32)
    lo = lax.bitcast_convert_type(x[:, n:].astype(BF16).astype(F32), jnp.uint32)
    return hi | lax.shift_right_logical(lo, jnp.uint32(16))


def _unpack_halves(p):
    hi = lax.bitcast_convert_type(p & jnp.uint32(0xFFFF0000), F32)
    lo = lax.bitcast_convert_type(lax.shift_left(p, jnp.uint32(16)), F32)
    return hi, lo


def _resident(shape, index_map):
    return pl.BlockSpec(shape, index_map, pipeline_mode=pl.Buffered(1))


def _norm_matmul_kernel(x_ref, nw_ref, w_ref, g_ref, o_ref, xn_ref, *, n_norm_tiles, chunk):
    j = pl.program_id(1)

    @pl.when(j == 0)
    def _():
        x = x_ref[...].astype(F32)
        xn_ref[...] = (_rms(x) * nw_ref[...]).astype(BF16)

    normed = j < n_norm_tiles
    tn = w_ref.shape[1]
    sub = max(chunk, MXU_TILE)
    for s in range(tn // sub):
        w = w_ref[:, s * sub:(s + 1) * sub]
        if w.dtype != BF16:
            w = w.astype(BF16)
        acc = _dot(xn_ref[...], w)
        for c in range(sub // chunk):
            lo = s * sub + c * chunk
            a = acc[:, c * chunk:(c + 1) * chunk]
            inv = lax.rsqrt(jnp.mean(a * a, axis=-1, keepdims=True) + EPS)
            scale = jnp.where(normed, inv, 1.0)
            o_ref[:, lo:lo + chunk] = (a * scale * g_ref[:, lo:lo + chunk]).astype(o_ref.dtype)


def _norm_matmul(x, norm_w, w, gain, *, n_norm_cols, chunk, tm, tn, name):
    m, k = x.shape
    n = w.shape[1]
    assert m % tm == 0 and n % tn == 0 and tn % max(chunk, MXU_TILE) == 0 and n_norm_cols % tn == 0
    kern = functools.partial(_norm_matmul_kernel, n_norm_tiles=n_norm_cols // tn, chunk=chunk)
    return pl.pallas_call(
        kern,
        grid=(m // tm, n // tn),
        in_specs=[
            pl.BlockSpec((tm, k), lambda i, j: (i, 0)),
            pl.BlockSpec((1, k), lambda i, j: (0, 0)),
            pl.BlockSpec((k, tn), lambda i, j: (0, j)),
            pl.BlockSpec((1, tn), lambda i, j: (0, j)),
        ],
        out_specs=pl.BlockSpec((tm, tn), lambda i, j: (i, j)),
        out_shape=jax.ShapeDtypeStruct((m, n), BF16),
        scratch_shapes=[pltpu.VMEM((tm, k), BF16)],
        compiler_params=pltpu.CompilerParams(
            dimension_semantics=("parallel", "arbitrary"), vmem_limit_bytes=VMEM_LIMIT),
        name=name,
    )(x, norm_w.reshape(1, k).astype(F32), w, gain.reshape(1, n).astype(F32))


def _diff_attn_kernel(lam_ref, qa_ref, qb_ref, k_ref, v_ref, g_ref, o_ref, *stat_refs, tq, n_q):
    pair = pl.program_id(2)
    stats_a = (stat_refs[0:3], stat_refs[3:6])
    stats_b = (stat_refs[6:9], stat_refs[9:12])

    def scores(q_ref, j):
        return tuple(_dot_nt(q_ref[:, c * DA_QK_DIM:(c + 1) * DA_QK_DIM],
                             k_ref[j * tq:(j + 1) * tq, c * DA_QK_DIM:(c + 1) * DA_QK_DIM])
                     for c in range(2))

    def accumulate(stats, j, s_pair, masked):
        n_t = tq // LANES
        for s, (m_ref, l_ref, acc_ref) in zip(s_pair, stats):
            if masked:
                row = lax.broadcasted_iota(jnp.int32, s.shape, 0)
                col = lax.broadcasted_iota(jnp.int32, s.shape, 1)
                s = jnp.where(col <= row, s, NEG)
            tiles = [s[:, c * LANES:(c + 1) * LANES] for c in range(n_t)]
            fold = tiles[0]
            for t_ in tiles[1:]:
                fold = jnp.maximum(fold, t_)
            m_old = m_ref[...]
            m_new = jnp.maximum(m_old, jnp.max(fold, axis=-1, keepdims=True))
            p_tiles = [jnp.exp2(t_ - m_new) for t_ in tiles]
            psum = p_tiles[0]
            for t_ in p_tiles[1:]:
                psum = psum + t_
            alpha = jnp.exp2(m_old - m_new)
            l_ref[...] = alpha * l_ref[...] + psum
            p = jnp.concatenate([t_.astype(BF16) for t_ in p_tiles], axis=1)
            pv = _dot(p, v_ref[j * tq:(j + 1) * tq, :])
            for c in range(DA_V_DIM // LANES):
                cols = slice(c * LANES, (c + 1) * LANES)
                acc_ref[:, cols] = alpha * acc_ref[:, cols] + pv[:, cols]
            m_ref[...] = m_new

    def finish(stats, rows):
        (_, l1, acc1), (_, l2, acc2) = stats
        l1, l2 = (jnp.sum(l[...], axis=-1, keepdims=True) for l in (l1, l2))
        o = acc1[...] / l1 - lam_ref[0] * (acc2[...] / l2)
        o_ref[rows, :] = (_rms(o) * g_ref[...]).astype(o_ref.dtype)

    def run(p):
        tiles = ((qa_ref, stats_a, p), (qb_ref, stats_b, n_q - 1 - p))
        for _, stats, _ in tiles:
            for m_ref, l_ref, acc_ref in stats:
                m_ref[...] = jnp.full(m_ref.shape, NEG, F32)
                l_ref[...] = jnp.zeros(l_ref.shape, F32)
                acc_ref[...] = jnp.zeros(acc_ref.shape, F32)
        pending = [scores(q_ref, 0) for q_ref, _, _ in tiles]
        for j in range(n_q - p):
            for idx, (q_ref, stats, diag) in enumerate(tiles):
                if j > diag:
                    continue
                s_pair = pending[idx]
                if j < diag:
                    pending[idx] = scores(q_ref, j + 1)
                accumulate(stats, j, s_pair, masked=(j == diag))
        finish(stats_a, slice(0, tq))
        finish(stats_b, slice(tq, 2 * tq))

    for p in range(n_q // 2):
        pl.when(pair == p)(functools.partial(run, p))


def _attn_tile_pos(tile, nq):
    b, qt = tile // nq, tile % nq
    return b * nq + jnp.where(qt < nq // 2, 2 * qt, 2 * (nq - 1 - qt) + 1)


def _diff_attn(proj, lam, gain, *, batch, seq, tq):
    t = batch * seq
    nq = seq // tq
    assert nq % 2 == 0
    width = 2 * DA_QK_DIM
    k_blk0 = DA_HEADS
    v_blk0 = 2 * DA_HEADS
    kern = functools.partial(_diff_attn_kernel, tq=tq, n_q=nq)
    stat = [pltpu.VMEM((tq, LANES), F32), pltpu.VMEM((tq, LANES), F32), pltpu.VMEM((tq, DA_V_DIM), F32)]
    return pl.pallas_call(
        kern,
        grid_spec=pltpu.PrefetchScalarGridSpec(
            num_scalar_prefetch=1,
            grid=(batch, DA_HEADS, nq // 2),
            in_specs=[
                pl.BlockSpec((tq, width), lambda b, h, p, lam: (b * nq + p, h)),
                pl.BlockSpec((tq, width), lambda b, h, p, lam: (b * nq + nq - 1 - p, h)),
                pl.BlockSpec((seq, width), lambda b, h, p, lam: (b, k_blk0 + h)),
                pl.BlockSpec((seq, width), lambda b, h, p, lam: (b, v_blk0 + h)),
                pl.BlockSpec((1, DA_V_DIM), lambda b, h, p, lam: (0, 0)),
            ],
            out_specs=pl.BlockSpec((2 * tq, DA_V_DIM), lambda b, h, p, lam: (b * (nq // 2) + p, h)),
            scratch_shapes=stat * 4,
        ),
        out_shape=jax.ShapeDtypeStruct((t, DA_HEADS * DA_V_DIM), BF16),
        compiler_params=pltpu.CompilerParams(
            dimension_semantics=("parallel", "parallel", "arbitrary"), vmem_limit_bytes=VMEM_LIMIT),
        name="diff_attn",
    )(lam, proj, proj, proj, proj, gain)


def _s5_kernel(u_ref, bd_ref, ar_ref, ai_ref, cd_ref, d_ref, o_ref, xs_ref, us_ref, ys_ref, *, seq, rows):
    ns = SSM_CHUNK_STATE
    seg_len = seq // SCAN_SEGS
    n_row_blk = seq // rows
    steps = rows // SCAN_SEGS

    for seg in range(SCAN_SEGS):
        us_ref[pl.ds(seg, seg_len, stride=SCAN_SEGS), :] = (
            u_ref[seg * seg_len:(seg + 1) * seg_len, :].astype(F32))

    def in_map(r):
        rs = slice(r * rows, (r + 1) * rows)
        xs_ref[rs, :] = _dot(us_ref[rs, :].astype(BF16), bd_ref[...])

    def out_map(r):
        rs = slice(r * rows, (r + 1) * rows)
        y = _dot(xs_ref[rs, :].astype(BF16), cd_ref[...]) + d_ref[...] * us_ref[rs, :]
        ys_ref[rs, :] = jax.nn.gelu(y)
        for seg in range(SCAN_SEGS):
            t0 = seg * seg_len + r * steps
            o_ref[t0:t0 + steps, :] = (
                ys_ref[pl.ds(r * rows + seg, steps, stride=SCAN_SEGS), :].astype(o_ref.dtype))

    ar = jnp.broadcast_to(ar_ref[...], (SCAN_SEGS, ns))
    ai = jnp.broadcast_to(ai_ref[...], (SCAN_SEGS, ns))

    def advance(t, sr, si):
        ts = slice(t * SCAN_SEGS, (t + 1) * SCAN_SEGS)
        return ar * sr - ai * si + xs_ref[ts, 0:ns], ar * si + ai * sr + xs_ref[ts, ns:2 * ns]

    in_map(0)
    fr = fi = jnp.zeros((SCAN_SEGS, ns), F32)
    for r in range(n_row_blk):
        if r + 1 < n_row_blk:
            in_map(r + 1)
        for t in range(r * steps, (r + 1) * steps):
            fr, fi = advance(t, fr, fi)

    pr, pi = ar, ai
    for _ in range(int(math.log2(seg_len))):
        pr, pi = pr * pr - pi * pi, 2.0 * pr * pi
    seg = lax.broadcasted_iota(jnp.int32, (SCAN_SEGS, ns), 0)

    def shifted(x, k):
        return jnp.where(seg >= k, pltpu.roll(x, k, 0), 0.0)

    k = 1
    while k < SCAN_SEGS:
        gr, gi = shifted(fr, k), shifted(fi, k)
        fr, fi = fr + pr * gr - pi * gi, fi + pr * gi + pi * gr
        pr, pi = pr * pr - pi * pi, 2.0 * pr * pi
        k *= 2
    sr, si = shifted(fr, 1), shifted(fi, 1)

    for r in range(n_row_blk):
        for t in range(r * steps, (r + 1) * steps):
            sr, si = advance(t, sr, si)
            ts = slice(t * SCAN_SEGS, (t + 1) * SCAN_SEGS)
            xs_ref[ts, 0:ns] = sr
            xs_ref[ts, ns:2 * ns] = si
        if r >= 1:
            out_map(r - 1)
    out_map(n_row_blk - 1)


def _s5(proj, u_col0, d_ssm, bd, a_re, a_im, cd, d_skip, *, batch, seq, rows=S5_ROW_BLOCK):
    n_chunks = d_ssm // LANES
    u_blk0 = u_col0 // LANES
    kern = functools.partial(_s5_kernel, seq=seq, rows=rows)
    return pl.pallas_call(
        kern,
        grid=(batch, n_chunks),
        in_specs=[
            pl.BlockSpec((seq, LANES), lambda b, c: (b, u_blk0 + c)),
            pl.BlockSpec((None, LANES, 2 * SSM_CHUNK_STATE), lambda b, c: (c, 0, 0)),
            pl.BlockSpec((None, 1, SSM_CHUNK_STATE), lambda b, c: (c, 0, 0)),
            pl.BlockSpec((None, 1, SSM_CHUNK_STATE), lambda b, c: (c, 0, 0)),
            pl.BlockSpec((None, 2 * SSM_CHUNK_STATE, LANES), lambda b, c: (c, 0, 0)),
            pl.BlockSpec((None, 1, LANES), lambda b, c: (c, 0, 0)),
        ],
        out_specs=pl.BlockSpec((seq, LANES), lambda b, c: (b, c)),
        out_shape=jax.ShapeDtypeStruct((batch * seq, d_ssm), BF16),
        scratch_shapes=[pltpu.VMEM((seq, 2 * SSM_CHUNK_STATE), F32),
                        pltpu.VMEM((seq, LANES), F32), pltpu.VMEM((seq, LANES), F32)],
        compiler_params=pltpu.CompilerParams(
            dimension_semantics=("parallel", "parallel"), vmem_limit_bytes=VMEM_LIMIT),
        name="s5_scan",
    )(proj, bd, a_re, a_im, cd, d_skip)


def _s5_params(lam_re, lam_im, log_dt, b_re, b_im, c_re, c_im, d_skip):
    g = lam_re.shape[0]
    nc = g // SSM_CHUNK_GROUPS
    lr = jnp.minimum(lam_re.astype(F32), -1e-4)
    li = lam_im.astype(F32)
    dt = jnp.exp(log_dt.astype(F32))[:, None]
    mag = jnp.exp(lr * dt)
    lb_re, lb_im = mag * jnp.cos(li * dt), mag * jnp.sin(li * dt)
    den = lr * lr + li * li
    coef_re = ((lb_re - 1.0) * lr + lb_im * li) / den
    coef_im = (lb_im * lr - (lb_re - 1.0) * li) / den
    br, bi = b_re.astype(F32), b_im.astype(F32)
    bb_re = coef_re[..., None] * br - coef_im[..., None] * bi
    bb_im = coef_re[..., None] * bi + coef_im[..., None] * br
    eye = jnp.eye(SSM_CHUNK_GROUPS, dtype=F32)

    def pack_in(bb):
        bb = bb.reshape(nc, SSM_CHUNK_GROUPS, SSM_STATE, SSM_GROUP)
        return jnp.einsum('cgph,gk->cghkp', bb, eye).reshape(nc, LANES, SSM_CHUNK_STATE)

    def pack_out(cc):
        cc = cc.astype(F32).reshape(nc, SSM_CHUNK_GROUPS, SSM_GROUP, SSM_STATE)
        return jnp.einsum('cghp,gk->ckpgh', cc, eye).reshape(nc, SSM_CHUNK_STATE, LANES)

    bd = jnp.concatenate([pack_in(bb_re), pack_in(bb_im)], axis=-1).astype(BF16)
    cd = jnp.concatenate([pack_out(c_re), -pack_out(c_im)], axis=1).astype(BF16)
    a_re = lb_re.reshape(nc, 1, SSM_CHUNK_STATE)
    a_im = lb_im.reshape(nc, 1, SSM_CHUNK_STATE)
    dd = d_skip.astype(F32).reshape(nc, 1, LANES)
    return bd, a_re, a_im, cd, dd


def _mix_out_kernel(a_ref, y_ref, x_ref, gw_ref, gb_ref, nw_ref, wo_ref, o_ref):
    d_attn = a_ref.shape[1]
    y = y_ref[...]
    gate = _dot(y, gw_ref[...]) + gb_ref[...]
    s = y.astype(F32) * jax.nn.sigmoid(gate)
    sn = (_rms(s) * nw_ref[...]).astype(BF16)
    acc = _dot(a_ref[...], wo_ref[0:d_attn, :]) + _dot(sn, wo_ref[d_attn:, :])
    o_ref[...] = x_ref[...] + acc


def _mix_out(a, y, x, glu_w, glu_b, norm_w, w_out, *, tm, attn_tiles):
    t, d = x.shape
    d_attn, d_ssm = a.shape[1], y.shape[1]
    const = lambda i: (0, 0)
    return pl.pallas_call(
        _mix_out_kernel,
        grid=(t // tm,),
        in_specs=[
            pl.BlockSpec((tm, d_attn), lambda i: (_attn_tile_pos(i, attn_tiles), 0)),
            pl.BlockSpec((tm, d_ssm), lambda i: (i, 0)),
            pl.BlockSpec((tm, d), lambda i: (i, 0)),
            _resident((d_ssm, d_ssm), const),
            _resident((1, d_ssm), const),
            _resident((1, d_ssm), const),
            _resident((d, d), const),
        ],
        out_specs=pl.BlockSpec((tm, d), lambda i: (i, 0)),
        out_shape=jax.ShapeDtypeStruct((t, d), F32),
        compiler_params=pltpu.CompilerParams(
            dimension_semantics=("parallel",), vmem_limit_bytes=VMEM_LIMIT),
        name="mix_out",
    )(a, y, x, glu_w, glu_b.reshape(1, d_ssm).astype(F32), norm_w.reshape(1, d_ssm).astype(F32), w_out)


def _xattn_route_kernel(q_ref, k_ref, v_ref, h_ref, xo_ref, nw_ref, rhi_ref, rlo_ref, rb_ref,
                        h2_ref, hn_ref, eid_ref, wts_ref):
    d = h_ref.shape[1]
    hd = d // X_HEADS
    h2 = h_ref[...]
    for h in range(X_HEADS):
        sl = slice(h * hd, (h + 1) * hd)
        s = _dot_nt(q_ref[:, sl], k_ref[:, sl])
        p = jnp.exp(s - jnp.max(s, axis=-1, keepdims=True))
        p = p * (1.0 / jnp.sum(p, axis=-1, keepdims=True))
        o = _dot(p.astype(BF16), v_ref[:, sl]).astype(BF16)
        h2 = h2 + _dot(o, xo_ref[sl, :])
    h2_ref[...] = h2
    hn = _rms(h2) * nw_ref[...]
    hn_ref[...] = _pack_halves(hn)

    hi = hn.astype(BF16)
    lo = (hn - hi.astype(F32)).astype(BF16)
    hi_both = _dot(hi, jnp.concatenate([rhi_ref[...], rlo_ref[...]], axis=1))
    logits = (hi_both[:, :ROUTE_LANES] + hi_both[:, ROUTE_LANES:] + _dot(lo, rhi_ref[...])) + rb_ref[...]

    lt = logits.T
    idx = lax.broadcasted_iota(jnp.int32, (SUBLANES, lt.shape[1]), 0)

    def first_row(cond):
        return jnp.min(jnp.where(cond, idx, SUBLANES), axis=0, keepdims=True)

    def softmax_rows(x):
        e = jnp.exp(x - jnp.max(x, axis=0, keepdims=True))
        return e / jnp.sum(e, axis=0, keepdims=True)

    p_c = softmax_rows(lt[0:MOE_GROUPS, :])
    p_grp = jnp.max(p_c, axis=0, keepdims=True)
    grp = first_row(p_c == p_grp)
    lf = lt[MOE_GROUPS:MOE_GROUPS + EXP_PER_GROUP, :]
    for g in range(1, MOE_GROUPS):
        lo_row = MOE_GROUPS + g * EXP_PER_GROUP
        lf = jnp.where(grp == g, lt[lo_row:lo_row + EXP_PER_GROUP, :], lf)
    pf = softmax_rows(lf)
    v1 = jnp.max(pf, axis=0, keepdims=True)
    i1 = first_row(pf == v1)
    rest = idx != i1
    v2 = jnp.max(jnp.where(rest, pf, -1.0), axis=0, keepdims=True)
    i2 = first_row(rest & (pf == v2))
    tot = v1 + v2
    e1 = grp * EXP_PER_GROUP + i1
    e2 = grp * EXP_PER_GROUP + i2
    eid_ref[...] = jnp.where(idx == 0, e1, jnp.where(idx == 1, e2, 0))
    wts_ref[...] = jnp.where(idx == 0, v1 / tot * p_grp, jnp.where(idx == 1, v2 / tot * p_grp, 0.0))


def _xattn_route(q, kv, h1, xo_w, norm_w, r_hi, r_lo, r_b, *, batch, seq, mem_len, tm):
    t, d = h1.shape
    n = seq // tm
    const = lambda b, i: (0, 0)
    row = lambda b, i: (b * n + i, 0)
    return pl.pallas_call(
        _xattn_route_kernel,
        grid=(batch, n),
        in_specs=[
            pl.BlockSpec((tm, d), row),
            pl.BlockSpec((mem_len, d), lambda b, i: (b, 0)),
            pl.BlockSpec((mem_len, d), lambda b, i: (b, 1)),
            pl.BlockSpec((tm, d), row),
            _resident((d, d), const),
            _resident((1, d), const),
            _resident((d, ROUTE_LANES), const),
            _resident((d, ROUTE_LANES), const),
            _resident((1, ROUTE_LANES), const),
        ],
        out_specs=[
            pl.BlockSpec((tm, d), row),
            pl.BlockSpec((tm, d // 2), row),
            pl.BlockSpec((SUBLANES, tm), row),
            pl.BlockSpec((SUBLANES, tm), row),
        ],
        out_shape=[
            jax.ShapeDtypeStruct((t, d), F32),
            jax.ShapeDtypeStruct((t, d // 2), jnp.uint32),
            jax.ShapeDtypeStruct((t // tm * SUBLANES, tm), jnp.int32),
            jax.ShapeDtypeStruct((t // tm * SUBLANES, tm), F32),
        ],
        compiler_params=pltpu.CompilerParams(
            dimension_semantics=("parallel", "parallel"), vmem_limit_bytes=VMEM_LIMIT),
        name="xattn_route",
    )(q, kv, kv, h1, xo_w, norm_w.reshape(1, d).astype(F32), r_hi, r_lo, r_b)


def _moe_kernel(be_ref, par_ref, first_ref, nxt_ref, base_ref, nval_ref, tok_ref, nu_ref,
                hn_hbm, wg_hbm, wu_hbm, wd_hbm, o_ref,
                xbuf, wgb, wub, wdb, gsem, wsem):
    b = pl.program_id(0)
    n_used = nu_ref[0]

    def weight_copies(e, slot):
        copies = []
        for hbm, buf in ((wg_hbm, wgb), (wu_hbm, wub), (wd_hbm, wdb)):
            rows = hbm.shape[1] // WEIGHT_DMA_CHUNKS
            for c in range(WEIGHT_DMA_CHUNKS):
                sl = pl.ds(c * rows, rows)
                copies.append(pltpu.make_async_copy(hbm.at[e, sl], buf.at[slot, sl], wsem.at[slot]))
        return copies

    def groups(blk):
        return (nval_ref[blk] + ROW_GROUP - 1) // ROW_GROUP

    def start_gather(blk, slot):
        base = base_ref[blk]

        def body(g, carry):
            for r in range(ROW_GROUP):
                tok = tok_ref[base + g * ROW_GROUP + r]
                pltpu.make_async_copy(hn_hbm.at[pl.ds(tok, 1)], xbuf.at[slot, g, pl.ds(r, 1)],
                                      gsem.at[slot]).start()
            return carry
        lax.fori_loop(0, groups(blk), body, 0)

    def wait_gather(blk, slot):
        filled = xbuf.at[slot, pl.ds(0, groups(blk))]
        pltpu.make_async_copy(filled, filled, gsem.at[slot]).wait()

    @pl.when(b == 0)
    def _():
        xbuf[...] = jnp.zeros(xbuf.shape, xbuf.dtype)
        for c in weight_copies(be_ref[0], par_ref[0]):
            c.start()
        start_gather(0, 0)

    @pl.when(b < n_used)
    def _():
        slot = b % 2
        wslot = par_ref[b]
        is_first = first_ref[b] == 1

        @pl.when(is_first & (nxt_ref[b] >= 0))
        def _():
            for c in weight_copies(nxt_ref[b], 1 - wslot):
                c.start()

        @pl.when(b + 1 < n_used)
        def _():
            start_gather(b + 1, 1 - slot)

        @pl.when(is_first)
        def _():
            for c in weight_copies(0, wslot):
                c.wait()

        wait_gather(b, slot)
        half = xbuf.shape[-1]

        def experts(rows):
            groups_ = rows // ROW_GROUP
            x_hi, x_lo = (v.astype(BF16)
                          for v in _unpack_halves(xbuf[slot, 0:groups_].reshape(rows, half)))

            def up(w):
                return (_dot(x_hi, w[wslot, 0:half, :].astype(BF16))
                        + _dot(x_lo, w[wslot, half:, :].astype(BF16)))

            mid = (jax.nn.silu(up(wgb)) * up(wub)).astype(BF16)
            o_ref[0:rows, :] = _pack_halves(_dot(mid, wdb[wslot].astype(BF16)))
            if rows < MOE_BLOCK:
                o_ref[rows:, :] = jnp.zeros((MOE_BLOCK - rows, half), o_ref.dtype)

        small = nval_ref[b] <= MOE_BLOCK // 2
        pl.when(small)(functools.partial(experts, MOE_BLOCK // 2))
        pl.when(jnp.logical_not(small))(functools.partial(experts, MOE_BLOCK))

    @pl.when(b >= n_used)
    def _():
        o_ref[...] = jnp.zeros(o_ref.shape, o_ref.dtype)


def _moe_experts(hn_packed, w_gate, w_up, w_down, meta, tok_sorted, n_used):
    d, d_ff = w_gate.shape[1:]
    half = hn_packed.shape[1]
    blk_exp, par, first, nxt, base, nval = meta
    n_blocks = blk_exp.shape[0]
    any_spec = pl.BlockSpec(memory_space=pl.ANY)
    return pl.pallas_call(
        _moe_kernel,
        grid_spec=pltpu.PrefetchScalarGridSpec(
            num_scalar_prefetch=8,
            grid=(n_blocks,),
            in_specs=[any_spec, any_spec, any_spec, any_spec],
            out_specs=pl.BlockSpec((MOE_BLOCK, half), lambda b, *_: (b, 0)),
            scratch_shapes=[
                pltpu.VMEM((2, MOE_BLOCK // ROW_GROUP, ROW_GROUP, half), jnp.uint32),
                pltpu.VMEM((2, d, d_ff), F32),
                pltpu.VMEM((2, d, d_ff), F32),
                pltpu.VMEM((2, d_ff, d), F32),
                pltpu.SemaphoreType.DMA((2,)),
                pltpu.SemaphoreType.DMA((2,)),
            ],
        ),
        out_shape=jax.ShapeDtypeStruct((n_blocks * MOE_BLOCK, half), jnp.uint32),
        compiler_params=pltpu.CompilerParams(
            dimension_semantics=("arbitrary",), vmem_limit_bytes=VMEM_LIMIT),
        name="moe_experts",
    )(blk_exp, par, first, nxt, base, nval, tok_sorted, n_used, hn_packed, w_gate, w_up, w_down)


def _combine_kernel(pos_ref, y_hbm, h_ref, w_ref, o_ref, ybuf, sem, *, tm):
    i = pl.program_id(0)
    n_groups = tm // ROW_GROUP

    def start_gather(tile, slot):
        base = tile * (tm * TOP_K_FINE)

        def body(g, carry):
            for r in range(ROW_GROUP):
                for k in range(TOP_K_FINE):
                    row = pos_ref[base + (g * ROW_GROUP + r) * TOP_K_FINE + k]
                    pltpu.make_async_copy(y_hbm.at[pl.ds(row, 1)], ybuf.at[slot, k, g, pl.ds(r, 1)],
                                          sem.at[slot]).start()
            return carry
        lax.fori_loop(0, n_groups, body, 0)

    @pl.when(i == 0)
    def _():
        start_gather(0, 0)

    @pl.when(i + 1 < pl.num_programs(0))
    def _():
        start_gather(i + 1, (i + 1) % 2)

    slot = i % 2
    pltpu.make_async_copy(ybuf.at[slot], ybuf.at[slot], sem.at[slot]).wait()
    w = w_ref[...]
    half = ybuf.shape[-1]
    y0 = _unpack_halves(ybuf[slot, 0].reshape(tm, half))
    y1 = _unpack_halves(ybuf[slot, 1].reshape(tm, half))
    for c in range(2):
        cols = slice(c * half, (c + 1) * half)
        o_ref[:, cols] = h_ref[:, cols] + (w[:, 0:1] * y0[c] + w[:, 1:2] * y1[c])


def _combine(y, h2, wts, pos, *, tm):
    t, d = h2.shape
    kern = functools.partial(_combine_kernel, tm=tm)
    return pl.pallas_call(
        kern,
        grid_spec=pltpu.PrefetchScalarGridSpec(
            num_scalar_prefetch=1,
            grid=(t // tm,),
            in_specs=[
                pl.BlockSpec(memory_space=pl.ANY),
                pl.BlockSpec((tm, d), lambda i, pos: (i, 0)),
                pl.BlockSpec((tm, ROUTE_LANES), lambda i, pos: (i, 0)),
            ],
            out_specs=pl.BlockSpec((tm, d), lambda i, pos: (i, 0)),
            scratch_shapes=[
                pltpu.VMEM((2, TOP_K_FINE, tm // ROW_GROUP, ROW_GROUP, y.shape[1]), jnp.uint32),
                pltpu.SemaphoreType.DMA((2,)),
            ],
        ),
        out_shape=jax.ShapeDtypeStruct((t, d), F32),
        compiler_params=pltpu.CompilerParams(
            dimension_semantics=("arbitrary",), vmem_limit_bytes=VMEM_LIMIT),
        name="moe_combine",
    )(pos, y, h2, wts)


def _lookup(table, idx):
    sel = idx[:, None] == jnp.arange(table.shape[0], dtype=jnp.int32)[None, :]
    return jnp.sum(jnp.where(sel, table[None, :], 0), axis=1).astype(jnp.int32)


def _dispatch(eid, n_tokens):
    n_assign = n_tokens * TOP_K_FINE
    experts = jnp.arange(N_EXPERTS, dtype=jnp.int32)
    e_flat = eid.reshape(n_assign)
    a_ids = jnp.arange(n_assign, dtype=jnp.int32)
    e_s, order = lax.sort_key_val(e_flat, a_ids)
    counts = jnp.sum((e_flat[:, None] == experts[None, :]).astype(jnp.int32), axis=0)
    starts = jnp.cumsum(counts) - counts
    nb = (counts + MOE_BLOCK - 1) // MOE_BLOCK
    blk_end = jnp.cumsum(nb)
    blk_start = blk_end - nb
    n_used = blk_end[-1]
    n_blocks = (n_assign + N_EXPERTS * (MOE_BLOCK - 1)) // MOE_BLOCK
    b_ids = jnp.arange(n_blocks, dtype=jnp.int32)
    used = b_ids < n_used
    blk_exp = jnp.minimum(jnp.sum((blk_end[None, :] <= b_ids[:, None]).astype(jnp.int32), axis=1),
                          N_EXPERTS - 1)
    j = b_ids - _lookup(blk_start, blk_exp)
    base = jnp.where(used, _lookup(starts, blk_exp) + j * MOE_BLOCK, 0)
    nval = jnp.where(used, jnp.clip(_lookup(counts, blk_exp) - j * MOE_BLOCK, 0, MOE_BLOCK), 0)
    first = (used & (j == 0)).astype(jnp.int32)
    active = counts > 0
    par = _lookup(jnp.cumsum(active.astype(jnp.int32)) - 1, blk_exp) & 1
    later = lax.cummin(jnp.where(active, experts, N_EXPERTS), reverse=True)
    nxt_e = jnp.concatenate([later[1:], jnp.full((1,), N_EXPERTS, jnp.int32)])
    nxt = _lookup(jnp.where(nxt_e == N_EXPERTS, -1, nxt_e), blk_exp)
    meta = tuple(v.astype(jnp.int32) for v in (blk_exp, par, first, nxt, base, nval))
    row_sorted = a_ids + _lookup(blk_start * MOE_BLOCK - starts, e_s)
    _, pos = lax.sort_key_val(order, row_sorted)
    tok_sorted = jnp.concatenate([lax.shift_right_logical(order, 1), jnp.zeros((ROW_GROUP,), jnp.int32)])
    return meta, tok_sorted, pos, n_used.astype(jnp.int32).reshape(1)


def kernel(x, mem, norm1_w, w_in, q_norm_w, k_norm_w, lambda_q1, lambda_k1, lambda_q2, lambda_k2, subln_w, ssm_lambda_re, ssm_lambda_im, ssm_log_dt, ssm_b_re, ssm_b_im, ssm_c_re, ssm_c_im, ssm_d, ssm_glu_w, ssm_glu_b, ssm_out_norm_w, w_out, norm2_w, mem_norm_w, xq_w, xkv_w, xq_norm_w, xk_norm_w, xo_w, norm3_w, router_coarse_w, router_coarse_b, router_fine_w, router_fine_b, expert_w_gate, expert_w_up, expert_w_down):
    batch, seq, d = x.shape
    mem_len = mem.shape[1]
    t = batch * seq
    depth = norm1_w.shape[0]
    d_attn = DA_HEADS * DA_V_DIM
    d_ssm = d - d_attn
    qk_cols = DA_HEADS * 2 * DA_QK_DIM
    x_hd = d // X_HEADS
    h = x.reshape(t, d)
    mem2 = mem.reshape(batch * mem_len, d)

    for l in range(depth):
        lam_init = 0.8 - 0.6 * math.exp(-0.3 * l)
        lam = (jnp.exp(jnp.sum(lambda_q1[l].astype(F32) * lambda_k1[l].astype(F32)))
               - jnp.exp(jnp.sum(lambda_q2[l].astype(F32) * lambda_k2[l].astype(F32)))
               + lam_init).reshape(1)

        n_rep = qk_cols // DA_QK_DIM
        in_gain = jnp.concatenate([
            jnp.tile(q_norm_w[l].astype(F32) * (DA_QK_DIM ** -0.5 * LOG2E), n_rep),
            jnp.tile(k_norm_w[l].astype(F32), n_rep),
            jnp.ones((d_attn + d_ssm,), F32)])
        proj = _norm_matmul(h, norm1_w[l], w_in[l].astype(BF16), in_gain,
                            n_norm_cols=2 * qk_cols, chunk=DA_QK_DIM, tm=PROJ_ROW_TILE, tn=PROJ_COL_TILE, name="in_proj")
        sub_gain = (subln_w[l].astype(F32) * (1.0 - lam_init)).reshape(1, DA_V_DIM)
        a = _diff_attn(proj, lam, sub_gain, batch=batch, seq=seq, tq=ATTN_TILE)

        bd, a_re, a_im, cd, dd = _s5_params(ssm_lambda_re[l], ssm_lambda_im[l], ssm_log_dt[l],
                                            ssm_b_re[l], ssm_b_im[l], ssm_c_re[l], ssm_c_im[l], ssm_d[l])
        y = _s5(proj, 2 * qk_cols + d_attn, d_ssm, bd, a_re, a_im, cd, dd, batch=batch, seq=seq)
        h = _mix_out(a, y, h, ssm_glu_w[l].astype(BF16), ssm_glu_b[l], ssm_out_norm_w[l],
                     w_out[l].astype(BF16), tm=ATTN_TILE, attn_tiles=seq // ATTN_TILE)

        kv_gain = jnp.concatenate([jnp.tile(xk_norm_w[l].astype(F32), X_HEADS), jnp.ones((d,), F32)])
        kv = _norm_matmul(mem2, mem_norm_w[l], xkv_w[l], kv_gain,
                          n_norm_cols=d, chunk=x_hd, tm=batch * mem_len, tn=KV_COL_TILE, name="kv_proj")
        q_gain = jnp.tile(xq_norm_w[l].astype(F32) * (x_hd ** -0.5), X_HEADS)
        q = _norm_matmul(h, norm2_w[l], xq_w[l].astype(BF16), q_gain,
                         n_norm_cols=d, chunk=x_hd, tm=PROJ_ROW_TILE, tn=PROJ_COL_TILE, name="xq_proj")
        r_w = jnp.concatenate([router_coarse_w[l].astype(F32), router_fine_w[l].astype(F32)], axis=1)
        r_w = jnp.pad(r_w, ((0, 0), (0, ROUTE_LANES - r_w.shape[1])))
        r_hi = r_w.astype(BF16)
        r_lo = (r_w - r_hi.astype(F32)).astype(BF16)
        r_b = jnp.concatenate([router_coarse_b[l].astype(F32), router_fine_b[l].astype(F32)])
        r_b = jnp.pad(r_b, (0, ROUTE_LANES - r_b.shape[0])).reshape(1, ROUTE_LANES)
        h2, hn3, eid_t, wts_t = _xattn_route(q, kv, h, xo_w[l].astype(BF16), norm3_w[l], r_hi, r_lo, r_b,
                                             batch=batch, seq=seq, mem_len=mem_len, tm=XATTN_ROW_TILE)

        def per_token(x_t):
            return x_t.reshape(-1, SUBLANES, x_t.shape[1])[:, :TOP_K_FINE, :].transpose(0, 2, 1).reshape(t, TOP_K_FINE)

        eid = per_token(eid_t)
        wts = jnp.pad(per_token(wts_t), ((0, 0), (0, ROUTE_LANES - TOP_K_FINE)))

        meta, tok_sorted, pos, n_used = _dispatch(eid, t)
        y = _moe_experts(hn3, expert_w_gate[l], expert_w_up[l], expert_w_down[l], meta, tok_sorted, n_used)
        h = _combine(y, h2, wts, pos, tm=COMBINE_ROW_TILE)

    return h.reshape(batch, seq, d)
```

```python
import functools
import math

import jax
import jax.numpy as jnp
from jax import lax
from jax.experimental import pallas as pl
from jax.experimental.pallas import tpu as pltpu

F32 = jnp.float32
BF16 = jnp.bfloat16

EPS = 1e-6
DA_HEADS = 4
DA_QK_DIM = 128
DA_V_DIM = 256
SSM_GROUP = 16
SSM_STATE = 64
X_HEADS = 4
MOE_GROUPS = 8
EXP_PER_GROUP = 8
N_EXPERTS = MOE_GROUPS * EXP_PER_GROUP
TOP_K_FINE = 2

LANES = 128
SUBLANES = 8
MXU_TILE = 256
VMEM_LIMIT = 56 * 1024 * 1024
NEG = -1e30
LOG2E = math.log2(math.e)

SSM_CHUNK_GROUPS = LANES // SSM_GROUP
SSM_CHUNK_STATE = SSM_CHUNK_GROUPS * SSM_STATE
SCAN_SEGS = SUBLANES
ATTN_TILE = 512
PROJ_ROW_TILE = 1024
PROJ_COL_TILE = 2048
KV_COL_TILE = 512
XATTN_ROW_TILE = 512
COMBINE_ROW_TILE = 256
S5_ROW_BLOCK = 256
MOE_BLOCK = 256
MOE_HEIGHT_STEP = 64
ROW_GROUP = SUBLANES
WEIGHT_DMA_CHUNKS = 4
ROUTE_LANES = LANES


def _rms(x, eps=EPS):
    return x * lax.rsqrt(jnp.mean(x * x, axis=-1, keepdims=True) + eps)


def _dot(a, b):
    return jnp.dot(a, b, preferred_element_type=F32)


def _dot_nt(a, b):
    return lax.dot_general(a, b, (((1,), (1,)), ((), ())), preferred_element_type=F32)


def _pack_halves(x):
    n = x.shape[1] // 2
    hi = lax.bitcast_convert_type(x[:, :n].astype(BF16).astype(F32), jnp.uint32)
    lo = lax.bitcast_convert_type(x[:, n:].astype(BF16).astype(F32), jnp.uint32)
    return hi | lax.shift_right_logical(lo, jnp.uint32(16))


def _unpack_halves(p):
    hi = lax.bitcast_convert_type(p & jnp.uint32(0xFFFF0000), F32)
    lo = lax.bitcast_convert_type(lax.shift_left(p, jnp.uint32(16)), F32)
    return hi, lo


def _resident(shape, index_map):
    return pl.BlockSpec(shape, index_map, pipeline_mode=pl.Buffered(1))


def _norm_matmul_kernel(x_ref, nw_ref, w_ref, g_ref, o_ref, xn_ref, *, n_norm_tiles, chunk):
    j = pl.program_id(1)

    @pl.when(j == 0)
    def _():
        x = x_ref[...].astype(F32)
        xn_ref[...] = (_rms(x) * nw_ref[...]).astype(BF16)

    normed = j < n_norm_tiles
    tn = w_ref.shape[1]
    sub = max(chunk, MXU_TILE)
    for s in range(tn // sub):
        w = w_ref[:, s * sub:(s + 1) * sub]
        if w.dtype != BF16:
            w = w.astype(BF16)
        acc = _dot(xn_ref[...], w)
        for c in range(sub // chunk):
            lo = s * sub + c * chunk
            a = acc[:, c * chunk:(c + 1) * chunk]
            inv = lax.rsqrt(jnp.mean(a * a, axis=-1, keepdims=True) + EPS)
            scale = jnp.where(normed, inv, 1.0)
            o_ref[:, lo:lo + chunk] = (a * scale * g_ref[:, lo:lo + chunk]).astype(o_ref.dtype)


def _norm_matmul(x, norm_w, w, gain, *, n_norm_cols, chunk, tm, tn, name):
    m, k = x.shape
    n = w.shape[1]
    assert m % tm == 0 and n % tn == 0 and tn % max(chunk, MXU_TILE) == 0 and n_norm_cols % tn == 0
    kern = functools.partial(_norm_matmul_kernel, n_norm_tiles=n_norm_cols // tn, chunk=chunk)
    return pl.pallas_call(
        kern,
        grid=(m // tm, n // tn),
        in_specs=[
            pl.BlockSpec((tm, k), lambda i, j: (i, 0)),
            pl.BlockSpec((1, k), lambda i, j: (0, 0)),
            pl.BlockSpec((k, tn), lambda i, j: (0, j)),
            pl.BlockSpec((1, tn), lambda i, j: (0, j)),
        ],
        out_specs=pl.BlockSpec((tm, tn), lambda i, j: (i, j)),
        out_shape=jax.ShapeDtypeStruct((m, n), BF16),
        scratch_shapes=[pltpu.VMEM((tm, k), BF16)],
        compiler_params=pltpu.CompilerParams(
            dimension_semantics=("parallel", "arbitrary"), vmem_limit_bytes=VMEM_LIMIT),
        name=name,
    )(x, norm_w.reshape(1, k).astype(F32), w, gain.reshape(1, n).astype(F32))


def _diff_attn_kernel(lam_ref, qa_ref, qb_ref, k_ref, v_ref, g_ref, o_ref, *stat_refs, tq, n_q):
    pair = pl.program_id(2)
    stats_a = (stat_refs[0:3], stat_refs[3:6])
    stats_b = (stat_refs[6:9], stat_refs[9:12])

    def scores(q_ref, j):
        return tuple(_dot_nt(q_ref[:, c * DA_QK_DIM:(c + 1) * DA_QK_DIM],
                             k_ref[j * tq:(j + 1) * tq, c * DA_QK_DIM:(c + 1) * DA_QK_DIM])
                     for c in range(2))

    def accumulate(stats, j, s_pair, masked):
        n_t = tq // LANES
        for s, (m_ref, l_ref, acc_ref) in zip(s_pair, stats):
            if masked:
                row = lax.broadcasted_iota(jnp.int32, s.shape, 0)
                col = lax.broadcasted_iota(jnp.int32, s.shape, 1)
                s = jnp.where(col <= row, s, NEG)
            tiles = [s[:, c * LANES:(c + 1) * LANES] for c in range(n_t)]
            fold = tiles[0]
            for t_ in tiles[1:]:
                fold = jnp.maximum(fold, t_)
            m_old = m_ref[...]
            m_new = jnp.maximum(m_old, jnp.max(fold, axis=-1, keepdims=True))
            p_tiles = [jnp.exp2(t_ - m_new) for t_ in tiles]
            psum = p_tiles[0]
            for t_ in p_tiles[1:]:
                psum = psum + t_
            alpha = jnp.exp2(m_old - m_new)
            l_ref[...] = alpha * l_ref[...] + psum
            p = jnp.concatenate([t_.astype(BF16) for t_ in p_tiles], axis=1)
            pv = _dot(p, v_ref[j * tq:(j + 1) * tq, :])
            for c in range(DA_V_DIM // LANES):
                cols = slice(c * LANES, (c + 1) * LANES)
                acc_ref[:, cols] = alpha * acc_ref[:, cols] + pv[:, cols]
            m_ref[...] = m_new

    def finish(stats, rows):
        (_, l1, acc1), (_, l2, acc2) = stats
        l1, l2 = (jnp.sum(l[...], axis=-1, keepdims=True) for l in (l1, l2))
        o = acc1[...] / l1 - lam_ref[0] * (acc2[...] / l2)
        o_ref[rows, :] = (_rms(o) * g_ref[...]).astype(o_ref.dtype)

    def run(p):
        tiles = ((qa_ref, stats_a, p), (qb_ref, stats_b, n_q - 1 - p))
        for _, stats, _ in tiles:
            for m_ref, l_ref, acc_ref in stats:
                m_ref[...] = jnp.full(m_ref.shape, NEG, F32)
                l_ref[...] = jnp.zeros(l_ref.shape, F32)
                acc_ref[...] = jnp.zeros(acc_ref.shape, F32)
        pending = [scores(q_ref, 0) for q_ref, _, _ in tiles]
        for j in range(n_q - p):
            for idx, (q_ref, stats, diag) in enumerate(tiles):
                if j > diag:
                    continue
                s_pair = pending[idx]
                if j < diag:
                    pending[idx] = scores(q_ref, j + 1)
                accumulate(stats, j, s_pair, masked=(j == diag))
        finish(stats_a, slice(0, tq))
        finish(stats_b, slice(tq, 2 * tq))

    for p in range(n_q // 2):
        pl.when(pair == p)(functools.partial(run, p))


def _attn_tile_pos(tile, nq):
    b, qt = tile // nq, tile % nq
    return b * nq + jnp.where(qt < nq // 2, 2 * qt, 2 * (nq - 1 - qt) + 1)


def _diff_attn(proj, lam, gain, *, batch, seq, tq):
    t = batch * seq
    nq = seq // tq
    assert nq % 2 == 0
    width = 2 * DA_QK_DIM
    k_blk0 = DA_HEADS
    v_blk0 = 2 * DA_HEADS
    kern = functools.partial(_diff_attn_kernel, tq=tq, n_q=nq)
    stat = [pltpu.VMEM((tq, LANES), F32), pltpu.VMEM((tq, LANES), F32), pltpu.VMEM((tq, DA_V_DIM), F32)]
    return pl.pallas_call(
        kern,
        grid_spec=pltpu.PrefetchScalarGridSpec(
            num_scalar_prefetch=1,
            grid=(batch, DA_HEADS, nq // 2),
            in_specs=[
                pl.BlockSpec((tq, width), lambda b, h, p, lam: (b * nq + p, h)),
                pl.BlockSpec((tq, width), lambda b, h, p, lam: (b * nq + nq - 1 - p, h)),
                pl.BlockSpec((seq, width), lambda b, h, p, lam: (b, k_blk0 + h)),
                pl.BlockSpec((seq, width), lambda b, h, p, lam: (b, v_blk0 + h)),
                pl.BlockSpec((1, DA_V_DIM), lambda b, h, p, lam: (0, 0)),
            ],
            out_specs=pl.BlockSpec((2 * tq, DA_V_DIM), lambda b, h, p, lam: (b * (nq // 2) + p, h)),
            scratch_shapes=stat * 4,
        ),
        out_shape=jax.ShapeDtypeStruct((t, DA_HEADS * DA_V_DIM), BF16),
        compiler_params=pltpu.CompilerParams(
            dimension_semantics=("parallel", "parallel", "arbitrary"), vmem_limit_bytes=VMEM_LIMIT),
        name="diff_attn",
    )(lam, proj, proj, proj, proj, gain)


def _s5_kernel(u_ref, bd_ref, ar_ref, ai_ref, cd_ref, d_ref, o_ref, xs_ref, us_ref, ys_ref, *, seq, rows):
    ns = SSM_CHUNK_STATE
    seg_len = seq // SCAN_SEGS
    n_row_blk = seq // rows
    steps = rows // SCAN_SEGS

    for seg in range(SCAN_SEGS):
        us_ref[pl.ds(seg, seg_len, stride=SCAN_SEGS), :] = (
            u_ref[seg * seg_len:(seg + 1) * seg_len, :].astype(F32))

    def in_map(r):
        rs = slice(r * rows, (r + 1) * rows)
        xs_ref[rs, :] = _dot(us_ref[rs, :].astype(BF16), bd_ref[...])

    def out_map(r):
        rs = slice(r * rows, (r + 1) * rows)
        y = _dot(xs_ref[rs, :].astype(BF16), cd_ref[...]) + d_ref[...] * us_ref[rs, :]
        ys_ref[rs, :] = jax.nn.gelu(y)
        for seg in range(SCAN_SEGS):
            t0 = seg * seg_len + r * steps
            o_ref[t0:t0 + steps, :] = (
                ys_ref[pl.ds(r * rows + seg, steps, stride=SCAN_SEGS), :].astype(o_ref.dtype))

    ar = jnp.broadcast_to(ar_ref[...], (SCAN_SEGS, ns))
    ai = jnp.broadcast_to(ai_ref[...], (SCAN_SEGS, ns))

    def advance(t, sr, si):
        ts = slice(t * SCAN_SEGS, (t + 1) * SCAN_SEGS)
        return ar * sr - ai * si + xs_ref[ts, 0:ns], ar * si + ai * sr + xs_ref[ts, ns:2 * ns]

    in_map(0)
    fr = fi = jnp.zeros((SCAN_SEGS, ns), F32)
    for r in range(n_row_blk):
        if r + 1 < n_row_blk:
            in_map(r + 1)
        for t in range(r * steps, (r + 1) * steps):
            fr, fi = advance(t, fr, fi)

    pr, pi = ar, ai
    for _ in range(int(math.log2(seg_len))):
        pr, pi = pr * pr - pi * pi, 2.0 * pr * pi
    seg = lax.broadcasted_iota(jnp.int32, (SCAN_SEGS, ns), 0)

    def shifted(x, k):
        return jnp.where(seg >= k, pltpu.roll(x, k, 0), 0.0)

    k = 1
    while k < SCAN_SEGS:
        gr, gi = shifted(fr, k), shifted(fi, k)
        fr, fi = fr + pr * gr - pi * gi, fi + pr * gi + pi * gr
        pr, pi = pr * pr - pi * pi, 2.0 * pr * pi
        k *= 2
    sr, si = shifted(fr, 1), shifted(fi, 1)

    for r in range(n_row_blk):
        for t in range(r * steps, (r + 1) * steps):
            sr, si = advance(t, sr, si)
            ts = slice(t * SCAN_SEGS, (t + 1) * SCAN_SEGS)
            xs_ref[ts, 0:ns] = sr
            xs_ref[ts, ns:2 * ns] = si
        if r >= 1:
            out_map(r - 1)
    out_map(n_row_blk - 1)


def _s5(proj, u_col0, d_ssm, bd, a_re, a_im, cd, d_skip, *, batch, seq, rows=S5_ROW_BLOCK):
    n_chunks = d_ssm // LANES
    u_blk0 = u_col0 // LANES
    kern = functools.partial(_s5_kernel, seq=seq, rows=rows)
    return pl.pallas_call(
        kern,
        grid=(batch, n_chunks),
        in_specs=[
            pl.BlockSpec((seq, LANES), lambda b, c: (b, u_blk0 + c)),
            pl.BlockSpec((None, LANES, 2 * SSM_CHUNK_STATE), lambda b, c: (c, 0, 0)),
            pl.BlockSpec((None, 1, SSM_CHUNK_STATE), lambda b, c: (c, 0, 0)),
            pl.BlockSpec((None, 1, SSM_CHUNK_STATE), lambda b, c: (c, 0, 0)),
            pl.BlockSpec((None, 2 * SSM_CHUNK_STATE, LANES), lambda b, c: (c, 0, 0)),
            pl.BlockSpec((None, 1, LANES), lambda b, c: (c, 0, 0)),
        ],
        out_specs=pl.BlockSpec((seq, LANES), lambda b, c: (b, c)),
        out_shape=jax.ShapeDtypeStruct((batch * seq, d_ssm), BF16),
        scratch_shapes=[pltpu.VMEM((seq, 2 * SSM_CHUNK_STATE), F32),
                        pltpu.VMEM((seq, LANES), F32), pltpu.VMEM((seq, LANES), F32)],
        compiler_params=pltpu.CompilerParams(
            dimension_semantics=("parallel", "parallel"), vmem_limit_bytes=VMEM_LIMIT),
        name="s5_scan",
    )(proj, bd, a_re, a_im, cd, d_skip)


def _s5_params(lam_re, lam_im, log_dt, b_re, b_im, c_re, c_im, d_skip):
    g = lam_re.shape[0]
    nc = g // SSM_CHUNK_GROUPS
    lr = jnp.minimum(lam_re.astype(F32), -1e-4)
    li = lam_im.astype(F32)
    dt = jnp.exp(log_dt.astype(F32))[:, None]
    mag = jnp.exp(lr * dt)
    lb_re, lb_im = mag * jnp.cos(li * dt), mag * jnp.sin(li * dt)
    den = lr * lr + li * li
    coef_re = ((lb_re - 1.0) * lr + lb_im * li) / den
    coef_im = (lb_im * lr - (lb_re - 1.0) * li) / den
    br, bi = b_re.astype(F32), b_im.astype(F32)
    bb_re = coef_re[..., None] * br - coef_im[..., None] * bi
    bb_im = coef_re[..., None] * bi + coef_im[..., None] * br
    eye = jnp.eye(SSM_CHUNK_GROUPS, dtype=F32)

    def pack_in(bb):
        bb = bb.reshape(nc, SSM_CHUNK_GROUPS, SSM_STATE, SSM_GROUP)
        return jnp.einsum('cgph,gk->cghkp', bb, eye).reshape(nc, LANES, SSM_CHUNK_STATE)

    def pack_out(cc):
        cc = cc.astype(F32).reshape(nc, SSM_CHUNK_GROUPS, SSM_GROUP, SSM_STATE)
        return jnp.einsum('cghp,gk->ckpgh', cc, eye).reshape(nc, SSM_CHUNK_STATE, LANES)

    bd = jnp.concatenate([pack_in(bb_re), pack_in(bb_im)], axis=-1).astype(BF16)
    cd = jnp.concatenate([pack_out(c_re), -pack_out(c_im)], axis=1).astype(BF16)
    a_re = lb_re.reshape(nc, 1, SSM_CHUNK_STATE)
    a_im = lb_im.reshape(nc, 1, SSM_CHUNK_STATE)
    dd = d_skip.astype(F32).reshape(nc, 1, LANES)
    return bd, a_re, a_im, cd, dd


def _mix_out_kernel(a_ref, y_ref, x_ref, gw_ref, gb_ref, nw_ref, wo_ref, o_ref):
    d_attn = a_ref.shape[1]
    y = y_ref[...]
    gate = _dot(y, gw_ref[...]) + gb_ref[...]
    s = y.astype(F32) * jax.nn.sigmoid(gate)
    sn = (_rms(s) * nw_ref[...]).astype(BF16)
    acc = _dot(a_ref[...], wo_ref[0:d_attn, :]) + _dot(sn, wo_ref[d_attn:, :])
    o_ref[...] = x_ref[...] + acc


def _mix_out(a, y, x, glu_w, glu_b, norm_w, w_out, *, tm, attn_tiles):
    t, d = x.shape
    d_attn, d_ssm = a.shape[1], y.shape[1]
    const = lambda i: (0, 0)
    return pl.pallas_call(
        _mix_out_kernel,
        grid=(t // tm,),
        in_specs=[
            pl.BlockSpec((tm, d_attn), lambda i: (_attn_tile_pos(i, attn_tiles), 0)),
            pl.BlockSpec((tm, d_ssm), lambda i: (i, 0)),
            pl.BlockSpec((tm, d), lambda i: (i, 0)),
            _resident((d_ssm, d_ssm), const),
            _resident((1, d_ssm), const),
            _resident((1, d_ssm), const),
            _resident((d, d), const),
        ],
        out_specs=pl.BlockSpec((tm, d), lambda i: (i, 0)),
        out_shape=jax.ShapeDtypeStruct((t, d), F32),
        compiler_params=pltpu.CompilerParams(
            dimension_semantics=("parallel",), vmem_limit_bytes=VMEM_LIMIT),
        name="mix_out",
    )(a, y, x, glu_w, glu_b.reshape(1, d_ssm).astype(F32), norm_w.reshape(1, d_ssm).astype(F32), w_out)


def _xattn_route_kernel(q_ref, k_ref, v_ref, h_ref, xo_ref, nw_ref, rhi_ref, rlo_ref, rb_ref,
                        h2_ref, hn_ref, eid_ref, wts_ref):
    d = h_ref.shape[1]
    hd = d // X_HEADS
    h2 = h_ref[...]
    for h in range(X_HEADS):
        sl = slice(h * hd, (h + 1) * hd)
        s = _dot_nt(q_ref[:, sl], k_ref[:, sl])
        p = jnp.exp(s - jnp.max(s, axis=-1, keepdims=True))
        p = p * (1.0 / jnp.sum(p, axis=-1, keepdims=True))
        o = _dot(p.astype(BF16), v_ref[:, sl]).astype(BF16)
        h2 = h2 + _dot(o, xo_ref[sl, :])
    h2_ref[...] = h2
    hn = _rms(h2) * nw_ref[...]
    hn_ref[...] = _pack_halves(hn)

    hi = hn.astype(BF16)
    lo = (hn - hi.astype(F32)).astype(BF16)
    hi_both = _dot(hi, jnp.concatenate([rhi_ref[...], rlo_ref[...]], axis=1))
    logits = (hi_both[:, :ROUTE_LANES] + hi_both[:, ROUTE_LANES:] + _dot(lo, rhi_ref[...])) + rb_ref[...]

    lt = logits.T
    idx = lax.broadcasted_iota(jnp.int32, (SUBLANES, lt.shape[1]), 0)

    def first_row(cond):
        return jnp.min(jnp.where(cond, idx, SUBLANES), axis=0, keepdims=True)

    def softmax_rows(x):
        e = jnp.exp(x - jnp.max(x, axis=0, keepdims=True))
        return e / jnp.sum(e, axis=0, keepdims=True)

    p_c = softmax_rows(lt[0:MOE_GROUPS, :])
    p_grp = jnp.max(p_c, axis=0, keepdims=True)
    grp = first_row(p_c == p_grp)
    lf = lt[MOE_GROUPS:MOE_GROUPS + EXP_PER_GROUP, :]
    for g in range(1, MOE_GROUPS):
        lo_row = MOE_GROUPS + g * EXP_PER_GROUP
        lf = jnp.where(grp == g, lt[lo_row:lo_row + EXP_PER_GROUP, :], lf)
    pf = softmax_rows(lf)
    v1 = jnp.max(pf, axis=0, keepdims=True)
    i1 = first_row(pf == v1)
    rest = idx != i1
    v2 = jnp.max(jnp.where(rest, pf, -1.0), axis=0, keepdims=True)
    i2 = first_row(rest & (pf == v2))
    tot = v1 + v2
    e1 = grp * EXP_PER_GROUP + i1
    e2 = grp * EXP_PER_GROUP + i2
    eid_ref[...] = jnp.where(idx == 0, e1, jnp.where(idx == 1, e2, 0))
    wts_ref[...] = jnp.where(idx == 0, v1 / tot * p_grp, jnp.where(idx == 1, v2 / tot * p_grp, 0.0))


def _xattn_route(q, kv, h1, xo_w, norm_w, r_hi, r_lo, r_b, *, batch, seq, mem_len, tm):
    t, d = h1.shape
    n = seq // tm
    const = lambda b, i: (0, 0)
    row = lambda b, i: (b * n + i, 0)
    return pl.pallas_call(
        _xattn_route_kernel,
        grid=(batch, n),
        in_specs=[
            pl.BlockSpec((tm, d), row),
            pl.BlockSpec((mem_len, d), lambda b, i: (b, 0)),
            pl.BlockSpec((mem_len, d), lambda b, i: (b, 1)),
            pl.BlockSpec((tm, d), row),
            _resident((d, d), const),
            _resident((1, d), const),
            _resident((d, ROUTE_LANES), const),
            _resident((d, ROUTE_LANES), const),
            _resident((1, ROUTE_LANES), const),
        ],
        out_specs=[
            pl.BlockSpec((tm, d), row),
            pl.BlockSpec((tm, d // 2), row),
            pl.BlockSpec((SUBLANES, tm), row),
            pl.BlockSpec((SUBLANES, tm), row),
        ],
        out_shape=[
            jax.ShapeDtypeStruct((t, d), F32),
            jax.ShapeDtypeStruct((t, d // 2), jnp.uint32),
            jax.ShapeDtypeStruct((t // tm * SUBLANES, tm), jnp.int32),
            jax.ShapeDtypeStruct((t // tm * SUBLANES, tm), F32),
        ],
        compiler_params=pltpu.CompilerParams(
            dimension_semantics=("parallel", "parallel"), vmem_limit_bytes=VMEM_LIMIT),
        name="xattn_route",
    )(q, kv, kv, h1, xo_w, norm_w.reshape(1, d).astype(F32), r_hi, r_lo, r_b)


def _moe_kernel(be_ref, par_ref, first_ref, nxt_ref, base_ref, nval_ref, tok_ref, nu_ref,
                hn_hbm, wg_hbm, wu_hbm, wd_hbm, o_ref,
                xbuf, wgb, wub, wdb, gsem, wsem):
    b = pl.program_id(0)
    n_used = nu_ref[0]

    def weight_copies(e, slot):
        copies = []
        for hbm, buf in ((wg_hbm, wgb), (wu_hbm, wub), (wd_hbm, wdb)):
            rows = hbm.shape[1] // WEIGHT_DMA_CHUNKS
            for c in range(WEIGHT_DMA_CHUNKS):
                sl = pl.ds(c * rows, rows)
                copies.append(pltpu.make_async_copy(hbm.at[e, sl], buf.at[slot, sl], wsem.at[slot]))
        return copies

    def groups(blk):
        return (nval_ref[blk] + ROW_GROUP - 1) // ROW_GROUP

    def start_gather(blk, slot):
        base = base_ref[blk]

        def body(g, carry):
            for r in range(ROW_GROUP):
                tok = tok_ref[base + g * ROW_GROUP + r]
                pltpu.make_async_copy(hn_hbm.at[pl.ds(tok, 1)], xbuf.at[slot, g, pl.ds(r, 1)],
                                      gsem.at[slot]).start()
            return carry
        lax.fori_loop(0, groups(blk), body, 0)

    def wait_gather(blk, slot):
        filled = xbuf.at[slot, pl.ds(0, groups(blk))]
        pltpu.make_async_copy(filled, filled, gsem.at[slot]).wait()

    @pl.when(b == 0)
    def _():
        xbuf[...] = jnp.zeros(xbuf.shape, xbuf.dtype)
        for c in weight_copies(be_ref[0], par_ref[0]):
            c.start()
        start_gather(0, 0)

    @pl.when(b < n_used)
    def _():
        slot = b % 2
        wslot = par_ref[b]
        is_first = first_ref[b] == 1

        @pl.when(is_first & (nxt_ref[b] >= 0))
        def _():
            for c in weight_copies(nxt_ref[b], 1 - wslot):
                c.start()

        @pl.when(b + 1 < n_used)
        def _():
            start_gather(b + 1, 1 - slot)

        @pl.when(is_first)
        def _():
            for c in weight_copies(0, wslot):
                c.wait()

        wait_gather(b, slot)
        half = xbuf.shape[-1]

        def experts(rows):
            groups_ = rows // ROW_GROUP
            x_hi, x_lo = (v.astype(BF16)
                          for v in _unpack_halves(xbuf[slot, 0:groups_].reshape(rows, half)))

            def up(w):
                return (_dot(x_hi, w[wslot, 0:half, :].astype(BF16))
                        + _dot(x_lo, w[wslot, half:, :].astype(BF16)))

            mid = (jax.nn.silu(up(wgb)) * up(wub)).astype(BF16)
            o_ref[0:rows, :] = _pack_halves(_dot(mid, wdb[wslot].astype(BF16)))
            if rows < MOE_BLOCK:
                o_ref[rows:, :] = jnp.zeros((MOE_BLOCK - rows, half), o_ref.dtype)

        for rows in range(MOE_HEIGHT_STEP, MOE_BLOCK + 1, MOE_HEIGHT_STEP):
            fits = (nval_ref[b] > rows - MOE_HEIGHT_STEP) & (nval_ref[b] <= rows)
            pl.when(fits)(functools.partial(experts, rows))

    @pl.when(b >= n_used)
    def _():
        o_ref[...] = jnp.zeros(o_ref.shape, o_ref.dtype)


def _moe_experts(hn_packed, w_gate, w_up, w_down, meta, tok_sorted, n_used):
    d, d_ff = w_gate.shape[1:]
    half = hn_packed.shape[1]
    blk_exp, par, first, nxt, base, nval = meta
    n_blocks = blk_exp.shape[0]
    any_spec = pl.BlockSpec(memory_space=pl.ANY)
    return pl.pallas_call(
        _moe_kernel,
        grid_spec=pltpu.PrefetchScalarGridSpec(
            num_scalar_prefetch=8,
            grid=(n_blocks,),
            in_specs=[any_spec, any_spec, any_spec, any_spec],
            out_specs=pl.BlockSpec((MOE_BLOCK, half), lambda b, *_: (b, 0)),
            scratch_shapes=[
                pltpu.VMEM((2, MOE_BLOCK // ROW_GROUP, ROW_GROUP, half), jnp.uint32),
                pltpu.VMEM((2, d, d_ff), F32),
                pltpu.VMEM((2, d, d_ff), F32),
                pltpu.VMEM((2, d_ff, d), F32),
                pltpu.SemaphoreType.DMA((2,)),
                pltpu.SemaphoreType.DMA((2,)),
            ],
        ),
        out_shape=jax.ShapeDtypeStruct((n_blocks * MOE_BLOCK, half), jnp.uint32),
        compiler_params=pltpu.CompilerParams(
            dimension_semantics=("arbitrary",), vmem_limit_bytes=VMEM_LIMIT),
        name="moe_experts",
    )(blk_exp, par, first, nxt, base, nval, tok_sorted, n_used, hn_packed, w_gate, w_up, w_down)


def _combine_kernel(pos_ref, y_hbm, h_ref, w_ref, o_ref, ybuf, sem, *, tm):
    i = pl.program_id(0)
    n_groups = tm // ROW_GROUP

    def start_gather(tile, slot):
        base = tile * (tm * TOP_K_FINE)

        def body(g, carry):
            for r in range(ROW_GROUP):
                for k in range(TOP_K_FINE):
                    row = pos_ref[base + (g * ROW_GROUP + r) * TOP_K_FINE + k]
                    pltpu.make_async_copy(y_hbm.at[pl.ds(row, 1)], ybuf.at[slot, k, g, pl.ds(r, 1)],
                                          sem.at[slot]).start()
            return carry
        lax.fori_loop(0, n_groups, body, 0)

    @pl.when(i == 0)
    def _():
        start_gather(0, 0)

    @pl.when(i + 1 < pl.num_programs(0))
    def _():
        start_gather(i + 1, (i + 1) % 2)

    slot = i % 2
    pltpu.make_async_copy(ybuf.at[slot], ybuf.at[slot], sem.at[slot]).wait()
    w = w_ref[...]
    half = ybuf.shape[-1]
    y0 = _unpack_halves(ybuf[slot, 0].reshape(tm, half))
    y1 = _unpack_halves(ybuf[slot, 1].reshape(tm, half))
    for c in range(2):
        cols = slice(c * half, (c + 1) * half)
        o_ref[:, cols] = h_ref[:, cols] + (w[:, 0:1] * y0[c] + w[:, 1:2] * y1[c])


def _combine(y, h2, wts, pos, *, tm):
    t, d = h2.shape
    kern = functools.partial(_combine_kernel, tm=tm)
    return pl.pallas_call(
        kern,
        grid_spec=pltpu.PrefetchScalarGridSpec(
            num_scalar_prefetch=1,
            grid=(t // tm,),
            in_specs=[
                pl.BlockSpec(memory_space=pl.ANY),
                pl.BlockSpec((tm, d), lambda i, pos: (i, 0)),
                pl.BlockSpec((tm, ROUTE_LANES), lambda i, pos: (i, 0)),
            ],
            out_specs=pl.BlockSpec((tm, d), lambda i, pos: (i, 0)),
            scratch_shapes=[
                pltpu.VMEM((2, TOP_K_FINE, tm // ROW_GROUP, ROW_GROUP, y.shape[1]), jnp.uint32),
                pltpu.SemaphoreType.DMA((2,)),
            ],
        ),
        out_shape=jax.ShapeDtypeStruct((t, d), F32),
        compiler_params=pltpu.CompilerParams(
            dimension_semantics=("arbitrary",), vmem_limit_bytes=VMEM_LIMIT),
        name="moe_combine",
    )(pos, y, h2, wts)


def _lookup(table, idx):
    sel = idx[:, None] == jnp.arange(table.shape[0], dtype=jnp.int32)[None, :]
    return jnp.sum(jnp.where(sel, table[None, :], 0), axis=1).astype(jnp.int32)


def _dispatch(eid, n_tokens):
    n_assign = n_tokens * TOP_K_FINE
    experts = jnp.arange(N_EXPERTS, dtype=jnp.int32)
    e_flat = eid.reshape(n_assign)
    a_ids = jnp.arange(n_assign, dtype=jnp.int32)
    e_s, order = lax.sort_key_val(e_flat, a_ids)
    counts = jnp.sum((e_flat[:, None] == experts[None, :]).astype(jnp.int32), axis=0)
    starts = jnp.cumsum(counts) - counts
    nb = (counts + MOE_BLOCK - 1) // MOE_BLOCK
    blk_end = jnp.cumsum(nb)
    blk_start = blk_end - nb
    n_used = blk_end[-1]
    n_blocks = (n_assign + N_EXPERTS * (MOE_BLOCK - 1)) // MOE_BLOCK
    b_ids = jnp.arange(n_blocks, dtype=jnp.int32)
    used = b_ids < n_used
    blk_exp = jnp.minimum(jnp.sum((blk_end[None, :] <= b_ids[:, None]).astype(jnp.int32), axis=1),
                          N_EXPERTS - 1)
    j = b_ids - _lookup(blk_start, blk_exp)
    base = jnp.where(used, _lookup(starts, blk_exp) + j * MOE_BLOCK, 0)
    nval = jnp.where(used, jnp.clip(_lookup(counts, blk_exp) - j * MOE_BLOCK, 0, MOE_BLOCK), 0)
    first = (used & (j == 0)).astype(jnp.int32)
    active = counts > 0
    par = _lookup(jnp.cumsum(active.astype(jnp.int32)) - 1, blk_exp) & 1
    later = lax.cummin(jnp.where(active, experts, N_EXPERTS), reverse=True)
    nxt_e = jnp.concatenate([later[1:], jnp.full((1,), N_EXPERTS, jnp.int32)])
    nxt = _lookup(jnp.where(nxt_e == N_EXPERTS, -1, nxt_e), blk_exp)
    meta = tuple(v.astype(jnp.int32) for v in (blk_exp, par, first, nxt, base, nval))
    row_sorted = a_ids + _lookup(blk_start * MOE_BLOCK - starts, e_s)
    _, pos = lax.sort_key_val(order, row_sorted)
    tok_sorted = jnp.concatenate([lax.shift_right_logical(order, 1), jnp.zeros((ROW_GROUP,), jnp.int32)])
    return meta, tok_sorted, pos, n_used.astype(jnp.int32).reshape(1)


def kernel(x, mem, norm1_w, w_in, q_norm_w, k_norm_w, lambda_q1, lambda_k1, lambda_q2, lambda_k2, subln_w, ssm_lambda_re, ssm_lambda_im, ssm_log_dt, ssm_b_re, ssm_b_im, ssm_c_re, ssm_c_im, ssm_d, ssm_glu_w, ssm_glu_b, ssm_out_norm_w, w_out, norm2_w, mem_norm_w, xq_w, xkv_w, xq_norm_w, xk_norm_w, xo_w, norm3_w, router_coarse_w, router_coarse_b, router_fine_w, router_fine_b, expert_w_gate, expert_w_up, expert_w_down):
    batch, seq, d = x.shape
    mem_len = mem.shape[1]
    t = batch * seq
    depth = norm1_w.shape[0]
    d_attn = DA_HEADS * DA_V_DIM
    d_ssm = d - d_attn
    qk_cols = DA_HEADS * 2 * DA_QK_DIM
    x_hd = d // X_HEADS
    h = x.reshape(t, d)
    mem2 = mem.reshape(batch * mem_len, d)

    for l in range(depth):
        lam_init = 0.8 - 0.6 * math.exp(-0.3 * l)
        lam = (jnp.exp(jnp.sum(lambda_q1[l].astype(F32) * lambda_k1[l].astype(F32)))
               - jnp.exp(jnp.sum(lambda_q2[l].astype(F32) * lambda_k2[l].astype(F32)))
               + lam_init).reshape(1)

        n_rep = qk_cols // DA_QK_DIM
        in_gain = jnp.concatenate([
            jnp.tile(q_norm_w[l].astype(F32) * (DA_QK_DIM ** -0.5 * LOG2E), n_rep),
            jnp.tile(k_norm_w[l].astype(F32), n_rep),
            jnp.ones((d_attn + d_ssm,), F32)])
        proj = _norm_matmul(h, norm1_w[l], w_in[l].astype(BF16), in_gain,
                            n_norm_cols=2 * qk_cols, chunk=DA_QK_DIM, tm=PROJ_ROW_TILE, tn=PROJ_COL_TILE, name="in_proj")
        sub_gain = (subln_w[l].astype(F32) * (1.0 - lam_init)).reshape(1, DA_V_DIM)
        a = _diff_attn(proj, lam, sub_gain, batch=batch, seq=seq, tq=ATTN_TILE)

        bd, a_re, a_im, cd, dd = _s5_params(ssm_lambda_re[l], ssm_lambda_im[l], ssm_log_dt[l],
                                            ssm_b_re[l], ssm_b_im[l], ssm_c_re[l], ssm_c_im[l], ssm_d[l])
        y = _s5(proj, 2 * qk_cols + d_attn, d_ssm, bd, a_re, a_im, cd, dd, batch=batch, seq=seq)
        h = _mix_out(a, y, h, ssm_glu_w[l].astype(BF16), ssm_glu_b[l], ssm_out_norm_w[l],
                     w_out[l].astype(BF16), tm=ATTN_TILE, attn_tiles=seq // ATTN_TILE)

        kv_gain = jnp.concatenate([jnp.tile(xk_norm_w[l].astype(F32), X_HEADS), jnp.ones((d,), F32)])
        kv = _norm_matmul(mem2, mem_norm_w[l], xkv_w[l], kv_gain,
                          n_norm_cols=d, chunk=x_hd, tm=batch * mem_len, tn=KV_COL_TILE, name="kv_proj")
        q_gain = jnp.tile(xq_norm_w[l].astype(F32) * (x_hd ** -0.5), X_HEADS)
        q = _norm_matmul(h, norm2_w[l], xq_w[l].astype(BF16), q_gain,
                         n_norm_cols=d, chunk=x_hd, tm=PROJ_ROW_TILE, tn=PROJ_COL_TILE, name="xq_proj")
        r_w = jnp.concatenate([router_coarse_w[l].astype(F32), router_fine_w[l].astype(F32)], axis=1)
        r_w = jnp.pad(r_w, ((0, 0), (0, ROUTE_LANES - r_w.shape[1])))
        r_hi = r_w.astype(BF16)
        r_lo = (r_w - r_hi.astype(F32)).astype(BF16)
        r_b = jnp.concatenate([router_coarse_b[l].astype(F32), router_fine_b[l].astype(F32)])
        r_b = jnp.pad(r_b, (0, ROUTE_LANES - r_b.shape[0])).reshape(1, ROUTE_LANES)
        h2, hn3, eid_t, wts_t = _xattn_route(q, kv, h, xo_w[l].astype(BF16), norm3_w[l], r_hi, r_lo, r_b,
                                             batch=batch, seq=seq, mem_len=mem_len, tm=XATTN_ROW_TILE)

        def per_token(x_t):
            return x_t.reshape(-1, SUBLANES, x_t.shape[1])[:, :TOP_K_FINE, :].transpose(0, 2, 1).reshape(t, TOP_K_FINE)

        eid = per_token(eid_t)
        wts = jnp.pad(per_token(wts_t), ((0, 0), (0, ROUTE_LANES - TOP_K_FINE)))

        meta, tok_sorted, pos, n_used = _dispatch(eid, t)
        y = _moe_experts(hn3, expert_w_gate[l], expert_w_up[l], expert_w_down[l], meta, tok_sorted, n_used)
        h = _combine(y, h2, wts, pos, tm=COMBINE_ROW_TILE)

    return h.reshape(batch, seq, d)
```

```python
import functools
import math

import jax
import jax.numpy as jnp
from jax import lax
from jax.experimental import pallas as pl
from jax.experimental.pallas import tpu as pltpu

F32 = jnp.float32
BF16 = jnp.bfloat16

EPS = 1e-6
DA_HEADS = 4
DA_QK_DIM = 128
DA_V_DIM = 256
SSM_GROUP = 16
SSM_STATE = 64
X_HEADS = 4
MOE_GROUPS = 8
EXP_PER_GROUP = 8
N_EXPERTS = MOE_GROUPS * EXP_PER_GROUP
TOP_K_FINE = 2

LANES = 128
SUBLANES = 8
MXU_TILE = 256
VMEM_LIMIT = 56 * 1024 * 1024
NEG = -1e30
LOG2E = math.log2(math.e)

SSM_CHUNK_GROUPS = LANES // SSM_GROUP
SSM_CHUNK_STATE = SSM_CHUNK_GROUPS * SSM_STATE
SCAN_SEGS = SUBLANES
ATTN_TILE = 512
PROJ_ROW_TILE = 1024
PROJ_COL_TILE = 2048
KV_COL_TILE = 512
XATTN_ROW_TILE = 512
COMBINE_ROW_TILE = 256
S5_ROW_BLOCK = 256
MOE_BLOCK = 256
ROW_GROUP = SUBLANES
WEIGHT_DMA_CHUNKS = 4
ROUTE_LANES = LANES


def _rms(x, eps=EPS):
    return x * lax.rsqrt(jnp.mean(x * x, axis=-1, keepdims=True) + eps)


def _dot(a, b):
    return jnp.dot(a, b, preferred_element_type=F32)


def _dot_nt(a, b):
    return lax.dot_general(a, b, (((1,), (1,)), ((), ())), preferred_element_type=F32)


def _pack_halves(x):
    n = x.shape[1] // 2
    hi = lax.bitcast_convert_type(x[:, :n].astype(BF16).astype(F32), jnp.uint32)
    lo = lax.bitcast_convert_type(x[:, n:].astype(BF16).astype(F32), jnp.uint32)
    return hi | lax.shift_right_logical(lo, jnp.uint32(16))


def _unpack_halves(p):
    hi = lax.bitcast_convert_type(p & jnp.uint32(0xFFFF0000), F32)
    lo = lax.bitcast_convert_type(lax.shift_left(p, jnp.uint32(16)), F32)
    return hi, lo


def _resident(shape, index_map):
    return pl.BlockSpec(shape, index_map, pipeline_mode=pl.Buffered(1))


def _norm_matmul_kernel(x_ref, nw_ref, w_ref, g_ref, o_ref, xn_ref, *, n_norm_tiles, chunk):
    j = pl.program_id(1)

    @pl.when(j == 0)
    def _():
        x = x_ref[...].astype(F32)
        xn_ref[...] = (_rms(x) * nw_ref[...]).astype(BF16)

    normed = j < n_norm_tiles
    tn = w_ref.shape[1]
    sub = max(chunk, MXU_TILE)
    for s in range(tn // sub):
        w = w_ref[:, s * sub:(s + 1) * sub]
        if w.dtype != BF16:
            w = w.astype(BF16)
        acc = _dot(xn_ref[...], w)
        for c in range(sub // chunk):
            lo = s * sub + c * chunk
            a = acc[:, c * chunk:(c + 1) * chunk]
            inv = lax.rsqrt(jnp.mean(a * a, axis=-1, keepdims=True) + EPS)
            scale = jnp.where(normed, inv, 1.0)
            o_ref[:, lo:lo + chunk] = (a * scale * g_ref[:, lo:lo + chunk]).astype(o_ref.dtype)


def _norm_matmul(x, norm_w, w, gain, *, n_norm_cols, chunk, tm, tn, name):
    m, k = x.shape
    n = w.shape[1]
    assert m % tm == 0 and n % tn == 0 and tn % max(chunk, MXU_TILE) == 0 and n_norm_cols % tn == 0
    kern = functools.partial(_norm_matmul_kernel, n_norm_tiles=n_norm_cols // tn, chunk=chunk)
    return pl.pallas_call(
        kern,
        grid=(m // tm, n // tn),
        in_specs=[
            pl.BlockSpec((tm, k), lambda i, j: (i, 0)),
            pl.BlockSpec((1, k), lambda i, j: (0, 0)),
            pl.BlockSpec((k, tn), lambda i, j: (0, j)),
            pl.BlockSpec((1, tn), lambda i, j: (0, j)),
        ],
        out_specs=pl.BlockSpec((tm, tn), lambda i, j: (i, j)),
        out_shape=jax.ShapeDtypeStruct((m, n), BF16),
        scratch_shapes=[pltpu.VMEM((tm, k), BF16)],
        compiler_params=pltpu.CompilerParams(
            dimension_semantics=("parallel", "arbitrary"), vmem_limit_bytes=VMEM_LIMIT),
        name=name,
    )(x, norm_w.reshape(1, k).astype(F32), w, gain.reshape(1, n).astype(F32))


def _diff_attn_kernel(lam_ref, qa_ref, qb_ref, k_ref, v_ref, g_ref, o_ref, *stat_refs, tq, n_q):
    pair = pl.program_id(2)
    stats_a = (stat_refs[0:3], stat_refs[3:6])
    stats_b = (stat_refs[6:9], stat_refs[9:12])

    def scores(q_ref, j):
        return tuple(_dot_nt(q_ref[:, c * DA_QK_DIM:(c + 1) * DA_QK_DIM],
                             k_ref[j * tq:(j + 1) * tq, c * DA_QK_DIM:(c + 1) * DA_QK_DIM])
                     for c in range(2))

    def accumulate(stats, j, s_pair, masked):
        n_t = tq // LANES
        for s, (m_ref, l_ref, acc_ref) in zip(s_pair, stats):
            if masked:
                row = lax.broadcasted_iota(jnp.int32, s.shape, 0)
                col = lax.broadcasted_iota(jnp.int32, s.shape, 1)
                s = jnp.where(col <= row, s, NEG)
            tiles = [s[:, c * LANES:(c + 1) * LANES] for c in range(n_t)]
            fold = tiles[0]
            for t_ in tiles[1:]:
                fold = jnp.maximum(fold, t_)
            m_old = m_ref[...]
            m_new = jnp.maximum(m_old, jnp.max(fold, axis=-1, keepdims=True))
            p_tiles = [jnp.exp2(t_ - m_new) for t_ in tiles]
            psum = p_tiles[0]
            for t_ in p_tiles[1:]:
                psum = psum + t_
            alpha = jnp.exp2(m_old - m_new)
            l_ref[...] = alpha * l_ref[...] + psum
            p = jnp.concatenate([t_.astype(BF16) for t_ in p_tiles], axis=1)
            pv = _dot(p, v_ref[j * tq:(j + 1) * tq, :])
            for c in range(DA_V_DIM // LANES):
                cols = slice(c * LANES, (c + 1) * LANES)
                acc_ref[:, cols] = alpha * acc_ref[:, cols] + pv[:, cols]
            m_ref[...] = m_new

    def finish(stats, rows):
        (_, l1, acc1), (_, l2, acc2) = stats
        l1, l2 = (jnp.sum(l[...], axis=-1, keepdims=True) for l in (l1, l2))
        o = acc1[...] / l1 - lam_ref[0] * (acc2[...] / l2)
        o_ref[rows, :] = (_rms(o) * g_ref[...]).astype(o_ref.dtype)

    def run(p):
        tiles = ((qa_ref, stats_a, p), (qb_ref, stats_b, n_q - 1 - p))
        for _, stats, _ in tiles:
            for m_ref, l_ref, acc_ref in stats:
                m_ref[...] = jnp.full(m_ref.shape, NEG, F32)
                l_ref[...] = jnp.zeros(l_ref.shape, F32)
                acc_ref[...] = jnp.zeros(acc_ref.shape, F32)
        pending = [scores(q_ref, 0) for q_ref, _, _ in tiles]
        for j in range(n_q - p):
            for idx, (q_ref, stats, diag) in enumerate(tiles):
                if j > diag:
                    continue
                s_pair = pending[idx]
                if j < diag:
                    pending[idx] = scores(q_ref, j + 1)
                accumulate(stats, j, s_pair, masked=(j == diag))
        finish(stats_a, slice(0, tq))
        finish(stats_b, slice(tq, 2 * tq))

    for p in range(n_q // 2):
        pl.when(pair == p)(functools.partial(run, p))


def _attn_tile_pos(tile, nq):
    b, qt = tile // nq, tile % nq
    return b * nq + jnp.where(qt < nq // 2, 2 * qt, 2 * (nq - 1 - qt) + 1)


def _diff_attn(proj, lam, gain, *, batch, seq, tq):
    t = batch * seq
    nq = seq // tq
    assert nq % 2 == 0
    width = 2 * DA_QK_DIM
    k_blk0 = DA_HEADS
    v_blk0 = 2 * DA_HEADS
    kern = functools.partial(_diff_attn_kernel, tq=tq, n_q=nq)
    stat = [pltpu.VMEM((tq, LANES), F32), pltpu.VMEM((tq, LANES), F32), pltpu.VMEM((tq, DA_V_DIM), F32)]
    return pl.pallas_call(
        kern,
        grid_spec=pltpu.PrefetchScalarGridSpec(
            num_scalar_prefetch=1,
            grid=(batch, DA_HEADS, nq // 2),
            in_specs=[
                pl.BlockSpec((tq, width), lambda b, h, p, lam: (b * nq + p, h)),
                pl.BlockSpec((tq, width), lambda b, h, p, lam: (b * nq + nq - 1 - p, h)),
                pl.BlockSpec((seq, width), lambda b, h, p, lam: (b, k_blk0 + h)),
                pl.BlockSpec((seq, width), lambda b, h, p, lam: (b, v_blk0 + h)),
                pl.BlockSpec((1, DA_V_DIM), lambda b, h, p, lam: (0, 0)),
            ],
            out_specs=pl.BlockSpec((2 * tq, DA_V_DIM), lambda b, h, p, lam: (b * (nq // 2) + p, h)),
            scratch_shapes=stat * 4,
        ),
        out_shape=jax.ShapeDtypeStruct((t, DA_HEADS * DA_V_DIM), BF16),
        compiler_params=pltpu.CompilerParams(
            dimension_semantics=("parallel", "parallel", "arbitrary"), vmem_limit_bytes=VMEM_LIMIT),
        name="diff_attn",
    )(lam, proj, proj, proj, proj, gain)


def _s5_kernel(u_ref, bd_ref, ar_ref, ai_ref, cd_ref, d_ref, o_ref, xs_ref, us_ref, ys_ref, *, seq, rows):
    ns = SSM_CHUNK_STATE
    seg_len = seq // SCAN_SEGS
    n_row_blk = seq // rows
    steps = rows // SCAN_SEGS

    for seg in range(SCAN_SEGS):
        us_ref[pl.ds(seg, seg_len, stride=SCAN_SEGS), :] = (
            u_ref[seg * seg_len:(seg + 1) * seg_len, :].astype(F32))

    def in_map(r):
        rs = slice(r * rows, (r + 1) * rows)
        xs_ref[rs, :] = _dot(us_ref[rs, :].astype(BF16), bd_ref[...])

    def out_map(r):
        rs = slice(r * rows, (r + 1) * rows)
        y = _dot(xs_ref[rs, :].astype(BF16), cd_ref[...]) + d_ref[...] * us_ref[rs, :]
        ys_ref[rs, :] = jax.nn.gelu(y)
        for seg in range(SCAN_SEGS):
            t0 = seg * seg_len + r * steps
            o_ref[t0:t0 + steps, :] = (
                ys_ref[pl.ds(r * rows + seg, steps, stride=SCAN_SEGS), :].astype(o_ref.dtype))

    ar = jnp.broadcast_to(ar_ref[...], (SCAN_SEGS, ns))
    ai = jnp.broadcast_to(ai_ref[...], (SCAN_SEGS, ns))

    def advance(t, sr, si):
        ts = slice(t * SCAN_SEGS, (t + 1) * SCAN_SEGS)
        return ar * sr - ai * si + xs_ref[ts, 0:ns], ar * si + ai * sr + xs_ref[ts, ns:2 * ns]

    in_map(0)
    fr = fi = jnp.zeros((SCAN_SEGS, ns), F32)
    for r in range(n_row_blk):
        if r + 1 < n_row_blk:
            in_map(r + 1)
        for t in range(r * steps, (r + 1) * steps):
            fr, fi = advance(t, fr, fi)

    pr, pi = ar, ai
    for _ in range(int(math.log2(seg_len))):
        pr, pi = pr * pr - pi * pi, 2.0 * pr * pi
    seg = lax.broadcasted_iota(jnp.int32, (SCAN_SEGS, ns), 0)

    def shifted(x, k):
        return jnp.where(seg >= k, pltpu.roll(x, k, 0), 0.0)

    k = 1
    while k < SCAN_SEGS:
        gr, gi = shifted(fr, k), shifted(fi, k)
        fr, fi = fr + pr * gr - pi * gi, fi + pr * gi + pi * gr
        pr, pi = pr * pr - pi * pi, 2.0 * pr * pi
        k *= 2
    sr, si = shifted(fr, 1), shifted(fi, 1)

    for r in range(n_row_blk):
        for t in range(r * steps, (r + 1) * steps):
            sr, si = advance(t, sr, si)
            ts = slice(t * SCAN_SEGS, (t + 1) * SCAN_SEGS)
            xs_ref[ts, 0:ns] = sr
            xs_ref[ts, ns:2 * ns] = si
        if r >= 1:
            out_map(r - 1)
    out_map(n_row_blk - 1)


def _s5(proj, u_col0, d_ssm, bd, a_re, a_im, cd, d_skip, *, batch, seq, rows=S5_ROW_BLOCK):
    n_chunks = d_ssm // LANES
    u_blk0 = u_col0 // LANES
    kern = functools.partial(_s5_kernel, seq=seq, rows=rows)
    return pl.pallas_call(
        kern,
        grid=(batch, n_chunks),
        in_specs=[
            pl.BlockSpec((seq, LANES), lambda b, c: (b, u_blk0 + c)),
            pl.BlockSpec((None, LANES, 2 * SSM_CHUNK_STATE), lambda b, c: (c, 0, 0)),
            pl.BlockSpec((None, 1, SSM_CHUNK_STATE), lambda b, c: (c, 0, 0)),
            pl.BlockSpec((None, 1, SSM_CHUNK_STATE), lambda b, c: (c, 0, 0)),
            pl.BlockSpec((None, 2 * SSM_CHUNK_STATE, LANES), lambda b, c: (c, 0, 0)),
            pl.BlockSpec((None, 1, LANES), lambda b, c: (c, 0, 0)),
        ],
        out_specs=pl.BlockSpec((seq, LANES), lambda b, c: (b, c)),
        out_shape=jax.ShapeDtypeStruct((batch * seq, d_ssm), BF16),
        scratch_shapes=[pltpu.VMEM((seq, 2 * SSM_CHUNK_STATE), F32),
                        pltpu.VMEM((seq, LANES), F32), pltpu.VMEM((seq, LANES), F32)],
        compiler_params=pltpu.CompilerParams(
            dimension_semantics=("parallel", "parallel"), vmem_limit_bytes=VMEM_LIMIT),
        name="s5_scan",
    )(proj, bd, a_re, a_im, cd, d_skip)


def _s5_params(lam_re, lam_im, log_dt, b_re, b_im, c_re, c_im, d_skip):
    g = lam_re.shape[0]
    nc = g // SSM_CHUNK_GROUPS
    lr = jnp.minimum(lam_re.astype(F32), -1e-4)
    li = lam_im.astype(F32)
    dt = jnp.exp(log_dt.astype(F32))[:, None]
    mag = jnp.exp(lr * dt)
    lb_re, lb_im = mag * jnp.cos(li * dt), mag * jnp.sin(li * dt)
    den = lr * lr + li * li
    coef_re = ((lb_re - 1.0) * lr + lb_im * li) / den
    coef_im = (lb_im * lr - (lb_re - 1.0) * li) / den
    br, bi = b_re.astype(F32), b_im.astype(F32)
    bb_re = coef_re[..., None] * br - coef_im[..., None] * bi
    bb_im = coef_re[..., None] * bi + coef_im[..., None] * br
    eye = jnp.eye(SSM_CHUNK_GROUPS, dtype=F32)

    def pack_in(bb):
        bb = bb.reshape(nc, SSM_CHUNK_GROUPS, SSM_STATE, SSM_GROUP)
        return jnp.einsum('cgph,gk->cghkp', bb, eye).reshape(nc, LANES, SSM_CHUNK_STATE)

    def pack_out(cc):
        cc = cc.astype(F32).reshape(nc, SSM_CHUNK_GROUPS, SSM_GROUP, SSM_STATE)
        return jnp.einsum('cghp,gk->ckpgh', cc, eye).reshape(nc, SSM_CHUNK_STATE, LANES)

    bd = jnp.concatenate([pack_in(bb_re), pack_in(bb_im)], axis=-1).astype(BF16)
    cd = jnp.concatenate([pack_out(c_re), -pack_out(c_im)], axis=1).astype(BF16)
    a_re = lb_re.reshape(nc, 1, SSM_CHUNK_STATE)
    a_im = lb_im.reshape(nc, 1, SSM_CHUNK_STATE)
    dd = d_skip.astype(F32).reshape(nc, 1, LANES)
    return bd, a_re, a_im, cd, dd


def _mix_out_kernel(a_ref, y_ref, x_ref, gw_ref, gb_ref, nw_ref, wo_ref, o_ref):
    d_attn = a_ref.shape[1]
    y = y_ref[...]
    gate = _dot(y, gw_ref[...]) + gb_ref[...]
    s = y.astype(F32) * jax.nn.sigmoid(gate)
    sn = (_rms(s) * nw_ref[...]).astype(BF16)
    acc = _dot(a_ref[...], wo_ref[0:d_attn, :]) + _dot(sn, wo_ref[d_attn:, :])
    o_ref[...] = x_ref[...] + acc


def _mix_out(a, y, x, glu_w, glu_b, norm_w, w_out, *, tm, attn_tiles):
    t, d = x.shape
    d_attn, d_ssm = a.shape[1], y.shape[1]
    const = lambda i: (0, 0)
    return pl.pallas_call(
        _mix_out_kernel,
        grid=(t // tm,),
        in_specs=[
            pl.BlockSpec((tm, d_attn), lambda i: (_attn_tile_pos(i, attn_tiles), 0)),
            pl.BlockSpec((tm, d_ssm), lambda i: (i, 0)),
            pl.BlockSpec((tm, d), lambda i: (i, 0)),
            _resident((d_ssm, d_ssm), const),
            _resident((1, d_ssm), const),
            _resident((1, d_ssm), const),
            _resident((d, d), const),
        ],
        out_specs=pl.BlockSpec((tm, d), lambda i: (i, 0)),
        out_shape=jax.ShapeDtypeStruct((t, d), F32),
        compiler_params=pltpu.CompilerParams(
            dimension_semantics=("parallel",), vmem_limit_bytes=VMEM_LIMIT),
        name="mix_out",
    )(a, y, x, glu_w, glu_b.reshape(1, d_ssm).astype(F32), norm_w.reshape(1, d_ssm).astype(F32), w_out)


def _xattn_route_kernel(q_ref, k_ref, v_ref, h_ref, xo_ref, nw_ref, rhi_ref, rlo_ref, rb_ref,
                        h2_ref, hn_ref, eid_ref, wts_ref):
    d = h_ref.shape[1]
    hd = d // X_HEADS
    heads = [slice(h * hd, (h + 1) * hd) for h in range(X_HEADS)]
    h2 = h_ref[...]
    s_next = _dot_nt(q_ref[:, heads[0]], k_ref[:, heads[0]])
    for h, sl in enumerate(heads):
        s = s_next
        if h + 1 < X_HEADS:
            s_next = _dot_nt(q_ref[:, heads[h + 1]], k_ref[:, heads[h + 1]])
        p = jnp.exp(s - jnp.max(s, axis=-1, keepdims=True))
        p = p * (1.0 / jnp.sum(p, axis=-1, keepdims=True))
        o = _dot(p.astype(BF16), v_ref[:, sl]).astype(BF16)
        h2 = h2 + _dot(o, xo_ref[sl, :])
    h2_ref[...] = h2
    hn = _rms(h2) * nw_ref[...]
    hn_ref[...] = _pack_halves(hn)

    hi = hn.astype(BF16)
    lo = (hn - hi.astype(F32)).astype(BF16)
    hi_both = _dot(hi, jnp.concatenate([rhi_ref[...], rlo_ref[...]], axis=1))
    logits = (hi_both[:, :ROUTE_LANES] + hi_both[:, ROUTE_LANES:] + _dot(lo, rhi_ref[...])) + rb_ref[...]

    lt = logits.T
    idx = lax.broadcasted_iota(jnp.int32, (SUBLANES, lt.shape[1]), 0)

    def first_row(cond):
        return jnp.min(jnp.where(cond, idx, SUBLANES), axis=0, keepdims=True)

    def softmax_rows(x):
        e = jnp.exp(x - jnp.max(x, axis=0, keepdims=True))
        return e / jnp.sum(e, axis=0, keepdims=True)

    p_c = softmax_rows(lt[0:MOE_GROUPS, :])
    p_grp = jnp.max(p_c, axis=0, keepdims=True)
    grp = first_row(p_c == p_grp)
    lf = lt[MOE_GROUPS:MOE_GROUPS + EXP_PER_GROUP, :]
    for g in range(1, MOE_GROUPS):
        lo_row = MOE_GROUPS + g * EXP_PER_GROUP
        lf = jnp.where(grp == g, lt[lo_row:lo_row + EXP_PER_GROUP, :], lf)
    pf = softmax_rows(lf)
    v1 = jnp.max(pf, axis=0, keepdims=True)
    i1 = first_row(pf == v1)
    rest = idx != i1
    v2 = jnp.max(jnp.where(rest, pf, -1.0), axis=0, keepdims=True)
    i2 = first_row(rest & (pf == v2))
    tot = v1 + v2
    e1 = grp * EXP_PER_GROUP + i1
    e2 = grp * EXP_PER_GROUP + i2
    eid_ref[...] = jnp.where(idx == 0, e1, jnp.where(idx == 1, e2, 0))
    wts_ref[...] = jnp.where(idx == 0, v1 / tot * p_grp, jnp.where(idx == 1, v2 / tot * p_grp, 0.0))


def _xattn_route(q, kv, h1, xo_w, norm_w, r_hi, r_lo, r_b, *, batch, seq, mem_len, tm):
    t, d = h1.shape
    n = seq // tm
    const = lambda b, i: (0, 0)
    row = lambda b, i: (b * n + i, 0)
    return pl.pallas_call(
        _xattn_route_kernel,
        grid=(batch, n),
        in_specs=[
            pl.BlockSpec((tm, d), row),
            pl.BlockSpec((mem_len, d), lambda b, i: (b, 0)),
            pl.BlockSpec((mem_len, d), lambda b, i: (b, 1)),
            pl.BlockSpec((tm, d), row),
            _resident((d, d), const),
            _resident((1, d), const),
            _resident((d, ROUTE_LANES), const),
            _resident((d, ROUTE_LANES), const),
            _resident((1, ROUTE_LANES), const),
        ],
        out_specs=[
            pl.BlockSpec((tm, d), row),
            pl.BlockSpec((tm, d // 2), row),
            pl.BlockSpec((SUBLANES, tm), row),
            pl.BlockSpec((SUBLANES, tm), row),
        ],
        out_shape=[
            jax.ShapeDtypeStruct((t, d), F32),
            jax.ShapeDtypeStruct((t, d // 2), jnp.uint32),
            jax.ShapeDtypeStruct((t // tm * SUBLANES, tm), jnp.int32),
            jax.ShapeDtypeStruct((t // tm * SUBLANES, tm), F32),
        ],
        compiler_params=pltpu.CompilerParams(
            dimension_semantics=("parallel", "parallel"), vmem_limit_bytes=VMEM_LIMIT),
        name="xattn_route",
    )(q, kv, kv, h1, xo_w, norm_w.reshape(1, d).astype(F32), r_hi, r_lo, r_b)


def _moe_kernel(be_ref, par_ref, first_ref, nxt_ref, base_ref, nval_ref, tok_ref, nu_ref,
                hn_hbm, wg_hbm, wu_hbm, wd_hbm, o_ref,
                xbuf, wgb, wub, wdb, gsem, wsem):
    b = pl.program_id(0)
    n_used = nu_ref[0]

    def weight_copies(e, slot):
        copies = []
        for hbm, buf in ((wg_hbm, wgb), (wu_hbm, wub), (wd_hbm, wdb)):
            rows = hbm.shape[1] // WEIGHT_DMA_CHUNKS
            for c in range(WEIGHT_DMA_CHUNKS):
                sl = pl.ds(c * rows, rows)
                copies.append(pltpu.make_async_copy(hbm.at[e, sl], buf.at[slot, sl], wsem.at[slot]))
        return copies

    def groups(blk):
        return (nval_ref[blk] + ROW_GROUP - 1) // ROW_GROUP

    def start_gather(blk, slot):
        base = base_ref[blk]

        def body(g, carry):
            for r in range(ROW_GROUP):
                tok = tok_ref[base + g * ROW_GROUP + r]
                pltpu.make_async_copy(hn_hbm.at[pl.ds(tok, 1)], xbuf.at[slot, g, pl.ds(r, 1)],
                                      gsem.at[slot]).start()
            return carry
        lax.fori_loop(0, groups(blk), body, 0)

    def wait_gather(blk, slot):
        filled = xbuf.at[slot, pl.ds(0, groups(blk))]
        pltpu.make_async_copy(filled, filled, gsem.at[slot]).wait()

    @pl.when(b == 0)
    def _():
        xbuf[...] = jnp.zeros(xbuf.shape, xbuf.dtype)
        for c in weight_copies(be_ref[0], par_ref[0]):
            c.start()
        start_gather(0, 0)

    @pl.when(b < n_used)
    def _():
        slot = b % 2
        wslot = par_ref[b]
        is_first = first_ref[b] == 1

        @pl.when(is_first & (nxt_ref[b] >= 0))
        def _():
            for c in weight_copies(nxt_ref[b], 1 - wslot):
                c.start()

        @pl.when(b + 1 < n_used)
        def _():
            start_gather(b + 1, 1 - slot)

        @pl.when(is_first)
        def _():
            for c in weight_copies(0, wslot):
                c.wait()

        wait_gather(b, slot)
        half = xbuf.shape[-1]

        def experts(rows):
            groups_ = rows // ROW_GROUP
            x_hi, x_lo = (v.astype(BF16)
                          for v in _unpack_halves(xbuf[slot, 0:groups_].reshape(rows, half)))

            def up(w):
                return (_dot(x_hi, w[wslot, 0:half, :].astype(BF16))
                        + _dot(x_lo, w[wslot, half:, :].astype(BF16)))

            mid = (jax.nn.silu(up(wgb)) * up(wub)).astype(BF16)
            o_ref[0:rows, :] = _pack_halves(_dot(mid, wdb[wslot].astype(BF16)))
            if rows < MOE_BLOCK:
                o_ref[rows:, :] = jnp.zeros((MOE_BLOCK - rows, half), o_ref.dtype)

        small = nval_ref[b] <= MOE_BLOCK // 2
        pl.when(small)(functools.partial(experts, MOE_BLOCK // 2))
        pl.when(jnp.logical_not(small))(functools.partial(experts, MOE_BLOCK))

    @pl.when(b >= n_used)
    def _():
        o_ref[...] = jnp.zeros(o_ref.shape, o_ref.dtype)


def _moe_experts(hn_packed, w_gate, w_up, w_down, meta, tok_sorted, n_used):
    d, d_ff = w_gate.shape[1:]
    half = hn_packed.shape[1]
    blk_exp, par, first, nxt, base, nval = meta
    n_blocks = blk_exp.shape[0]
    any_spec = pl.BlockSpec(memory_space=pl.ANY)
    return pl.pallas_call(
        _moe_kernel,
        grid_spec=pltpu.PrefetchScalarGridSpec(
            num_scalar_prefetch=8,
            grid=(n_blocks,),
            in_specs=[any_spec, any_spec, any_spec, any_spec],
            out_specs=pl.BlockSpec((MOE_BLOCK, half), lambda b, *_: (b, 0)),
            scratch_shapes=[
                pltpu.VMEM((2, MOE_BLOCK // ROW_GROUP, ROW_GROUP, half), jnp.uint32),
                pltpu.VMEM((2, d, d_ff), F32),
                pltpu.VMEM((2, d, d_ff), F32),
                pltpu.VMEM((2, d_ff, d), F32),
                pltpu.SemaphoreType.DMA((2,)),
                pltpu.SemaphoreType.DMA((2,)),
            ],
        ),
        out_shape=jax.ShapeDtypeStruct((n_blocks * MOE_BLOCK, half), jnp.uint32),
        compiler_params=pltpu.CompilerParams(
            dimension_semantics=("arbitrary",), vmem_limit_bytes=VMEM_LIMIT),
        name="moe_experts",
    )(blk_exp, par, first, nxt, base, nval, tok_sorted, n_used, hn_packed, w_gate, w_up, w_down)


def _combine_kernel(pos_ref, y_hbm, h_ref, w_ref, o_ref, ybuf, sem, *, tm):
    i = pl.program_id(0)
    n_groups = tm // ROW_GROUP

    def start_gather(tile, slot):
        base = tile * (tm * TOP_K_FINE)

        def body(g, carry):
            for r in range(ROW_GROUP):
                for k in range(TOP_K_FINE):
                    row = pos_ref[base + (g * ROW_GROUP + r) * TOP_K_FINE + k]
                    pltpu.make_async_copy(y_hbm.at[pl.ds(row, 1)], ybuf.at[slot, k, g, pl.ds(r, 1)],
                                          sem.at[slot]).start()
            return carry
        lax.fori_loop(0, n_groups, body, 0)

    @pl.when(i == 0)
    def _():
        start_gather(0, 0)

    @pl.when(i + 1 < pl.num_programs(0))
    def _():
        start_gather(i + 1, (i + 1) % 2)

    slot = i % 2
    pltpu.make_async_copy(ybuf.at[slot], ybuf.at[slot], sem.at[slot]).wait()
    w = w_ref[...]
    half = ybuf.shape[-1]
    y0 = _unpack_halves(ybuf[slot, 0].reshape(tm, half))
    y1 = _unpack_halves(ybuf[slot, 1].reshape(tm, half))
    for c in range(2):
        cols = slice(c * half, (c + 1) * half)
        o_ref[:, cols] = h_ref[:, cols] + (w[:, 0:1] * y0[c] + w[:, 1:2] * y1[c])


def _combine(y, h2, wts, pos, *, tm):
    t, d = h2.shape
    kern = functools.partial(_combine_kernel, tm=tm)
    return pl.pallas_call(
        kern,
        grid_spec=pltpu.PrefetchScalarGridSpec(
            num_scalar_prefetch=1,
            grid=(t // tm,),
            in_specs=[
                pl.BlockSpec(memory_space=pl.ANY),
                pl.BlockSpec((tm, d), lambda i, pos: (i, 0)),
                pl.BlockSpec((tm, ROUTE_LANES), lambda i, pos: (i, 0)),
            ],
            out_specs=pl.BlockSpec((tm, d), lambda i, pos: (i, 0)),
            scratch_shapes=[
                pltpu.VMEM((2, TOP_K_FINE, tm // ROW_GROUP, ROW_GROUP, y.shape[1]), jnp.uint32),
                pltpu.SemaphoreType.DMA((2,)),
            ],
        ),
        out_shape=jax.ShapeDtypeStruct((t, d), F32),
        compiler_params=pltpu.CompilerParams(
            dimension_semantics=("arbitrary",), vmem_limit_bytes=VMEM_LIMIT),
        name="moe_combine",
    )(pos, y, h2, wts)


def _lookup(table, idx):
    sel = idx[:, None] == jnp.arange(table.shape[0], dtype=jnp.int32)[None, :]
    return jnp.sum(jnp.where(sel, table[None, :], 0), axis=1).astype(jnp.int32)


def _dispatch(eid, n_tokens):
    n_assign = n_tokens * TOP_K_FINE
    experts = jnp.arange(N_EXPERTS, dtype=jnp.int32)
    e_flat = eid.reshape(n_assign)
    a_ids = jnp.arange(n_assign, dtype=jnp.int32)
    e_s, order = lax.sort_key_val(e_flat, a_ids)
    counts = jnp.sum((e_flat[:, None] == experts[None, :]).astype(jnp.int32), axis=0)
    starts = jnp.cumsum(counts) - counts
    nb = (counts + MOE_BLOCK - 1) // MOE_BLOCK
    blk_end = jnp.cumsum(nb)
    blk_start = blk_end - nb
    n_used = blk_end[-1]
    n_blocks = (n_assign + N_EXPERTS * (MOE_BLOCK - 1)) // MOE_BLOCK
    b_ids = jnp.arange(n_blocks, dtype=jnp.int32)
    used = b_ids < n_used
    blk_exp = jnp.minimum(jnp.sum((blk_end[None, :] <= b_ids[:, None]).astype(jnp.int32), axis=1),
                          N_EXPERTS - 1)
    j = b_ids - _lookup(blk_start, blk_exp)
    base = jnp.where(used, _lookup(starts, blk_exp) + j * MOE_BLOCK, 0)
    nval = jnp.where(used, jnp.clip(_lookup(counts, blk_exp) - j * MOE_BLOCK, 0, MOE_BLOCK), 0)
    first = (used & (j == 0)).astype(jnp.int32)
    active = counts > 0
    par = _lookup(jnp.cumsum(active.astype(jnp.int32)) - 1, blk_exp) & 1
    later = lax.cummin(jnp.where(active, experts, N_EXPERTS), reverse=True)
    nxt_e = jnp.concatenate([later[1:], jnp.full((1,), N_EXPERTS, jnp.int32)])
    nxt = _lookup(jnp.where(nxt_e == N_EXPERTS, -1, nxt_e), blk_exp)
    meta = tuple(v.astype(jnp.int32) for v in (blk_exp, par, first, nxt, base, nval))
    row_sorted = a_ids + _lookup(blk_start * MOE_BLOCK - starts, e_s)
    _, pos = lax.sort_key_val(order, row_sorted)
    tok_sorted = jnp.concatenate([lax.shift_right_logical(order, 1), jnp.zeros((ROW_GROUP,), jnp.int32)])
    return meta, tok_sorted, pos, n_used.astype(jnp.int32).reshape(1)


def kernel(x, mem, norm1_w, w_in, q_norm_w, k_norm_w, lambda_q1, lambda_k1, lambda_q2, lambda_k2, subln_w, ssm_lambda_re, ssm_lambda_im, ssm_log_dt, ssm_b_re, ssm_b_im, ssm_c_re, ssm_c_im, ssm_d, ssm_glu_w, ssm_glu_b, ssm_out_norm_w, w_out, norm2_w, mem_norm_w, xq_w, xkv_w, xq_norm_w, xk_norm_w, xo_w, norm3_w, router_coarse_w, router_coarse_b, router_fine_w, router_fine_b, expert_w_gate, expert_w_up, expert_w_down):
    batch, seq, d = x.shape
    mem_len = mem.shape[1]
    t = batch * seq
    depth = norm1_w.shape[0]
    d_attn = DA_HEADS * DA_V_DIM
    d_ssm = d - d_attn
    qk_cols = DA_HEADS * 2 * DA_QK_DIM
    x_hd = d // X_HEADS
    h = x.reshape(t, d)
    mem2 = mem.reshape(batch * mem_len, d)

    for l in range(depth):
        lam_init = 0.8 - 0.6 * math.exp(-0.3 * l)
        lam = (jnp.exp(jnp.sum(lambda_q1[l].astype(F32) * lambda_k1[l].astype(F32)))
               - jnp.exp(jnp.sum(lambda_q2[l].astype(F32) * lambda_k2[l].astype(F32)))
               + lam_init).reshape(1)

        n_rep = qk_cols // DA_QK_DIM
        in_gain = jnp.concatenate([
            jnp.tile(q_norm_w[l].astype(F32) * (DA_QK_DIM ** -0.5 * LOG2E), n_rep),
            jnp.tile(k_norm_w[l].astype(F32), n_rep),
            jnp.ones((d_attn + d_ssm,), F32)])
        proj = _norm_matmul(h, norm1_w[l], w_in[l].astype(BF16), in_gain,
                            n_norm_cols=2 * qk_cols, chunk=DA_QK_DIM, tm=PROJ_ROW_TILE, tn=PROJ_COL_TILE, name="in_proj")
        sub_gain = (subln_w[l].astype(F32) * (1.0 - lam_init)).reshape(1, DA_V_DIM)
        a = _diff_attn(proj, lam, sub_gain, batch=batch, seq=seq, tq=ATTN_TILE)

        bd, a_re, a_im, cd, dd = _s5_params(ssm_lambda_re[l], ssm_lambda_im[l], ssm_log_dt[l],
                                            ssm_b_re[l], ssm_b_im[l], ssm_c_re[l], ssm_c_im[l], ssm_d[l])
        y = _s5(proj, 2 * qk_cols + d_attn, d_ssm, bd, a_re, a_im, cd, dd, batch=batch, seq=seq)
        h = _mix_out(a, y, h, ssm_glu_w[l].astype(BF16), ssm_glu_b[l], ssm_out_norm_w[l],
                     w_out[l].astype(BF16), tm=ATTN_TILE, attn_tiles=seq // ATTN_TILE)

        kv_gain = jnp.concatenate([jnp.tile(xk_norm_w[l].astype(F32), X_HEADS), jnp.ones((d,), F32)])
        kv = _norm_matmul(mem2, mem_norm_w[l], xkv_w[l], kv_gain,
                          n_norm_cols=d, chunk=x_hd, tm=batch * mem_len, tn=KV_COL_TILE, name="kv_proj")
        q_gain = jnp.tile(xq_norm_w[l].astype(F32) * (x_hd ** -0.5), X_HEADS)
        q = _norm_matmul(h, norm2_w[l], xq_w[l].astype(BF16), q_gain,
                         n_norm_cols=d, chunk=x_hd, tm=PROJ_ROW_TILE, tn=PROJ_COL_TILE, name="xq_proj")
        r_w = jnp.concatenate([router_coarse_w[l].astype(F32), router_fine_w[l].astype(F32)], axis=1)
        r_w = jnp.pad(r_w, ((0, 0), (0, ROUTE_LANES - r_w.shape[1])))
        r_hi = r_w.astype(BF16)
        r_lo = (r_w - r_hi.astype(F32)).astype(BF16)
        r_b = jnp.concatenate([router_coarse_b[l].astype(F32), router_fine_b[l].astype(F32)])
        r_b = jnp.pad(r_b, (0, ROUTE_LANES - r_b.shape[0])).reshape(1, ROUTE_LANES)
        h2, hn3, eid_t, wts_t = _xattn_route(q, kv, h, xo_w[l].astype(BF16), norm3_w[l], r_hi, r_lo, r_b,
                                             batch=batch, seq=seq, mem_len=mem_len, tm=XATTN_ROW_TILE)

        def per_token(x_t):
            return x_t.reshape(-1, SUBLANES, x_t.shape[1])[:, :TOP_K_FINE, :].transpose(0, 2, 1).reshape(t, TOP_K_FINE)

        eid = per_token(eid_t)
        wts = jnp.pad(per_token(wts_t), ((0, 0), (0, ROUTE_LANES - TOP_K_FINE)))

        meta, tok_sorted, pos, n_used = _dispatch(eid, t)
        y = _moe_experts(hn3, expert_w_gate[l], expert_w_up[l], expert_w_down[l], meta, tok_sorted, n_used)
        h = _combine(y, h2, wts, pos, tm=COMBINE_ROW_TILE)

    return h.reshape(batch, seq, d)
```

```python
import functools
import math

import jax
import jax.numpy as jnp
from jax import lax
from jax.experimental import pallas as pl
from jax.experimental.pallas import tpu as pltpu

F32 = jnp.float32
BF16 = jnp.bfloat16

EPS = 1e-6
DA_HEADS = 4
DA_QK_DIM = 128
DA_V_DIM = 256
SSM_GROUP = 16
SSM_STATE = 64
X_HEADS = 4
MOE_GROUPS = 8
EXP_PER_GROUP = 8
N_EXPERTS = MOE_GROUPS * EXP_PER_GROUP
TOP_K_FINE = 2

LANES = 128
SUBLANES = 8
MXU_TILE = 256
VMEM_LIMIT = 56 * 1024 * 1024
NEG = -1e30
LOG2E = math.log2(math.e)

SSM_CHUNK_GROUPS = LANES // SSM_GROUP
SSM_CHUNK_STATE = SSM_CHUNK_GROUPS * SSM_STATE
SCAN_SEGS = SUBLANES
ATTN_TILE = 512
PROJ_ROW_TILE = 1024
PROJ_COL_TILE = 2048
KV_COL_TILE = 512
XATTN_ROW_TILE = 512
COMBINE_ROW_TILE = 256
S5_ROW_BLOCK = 256
MOE_BLOCK = 256
ROW_GROUP = SUBLANES
WEIGHT_DMA_CHUNKS = 4
ROUTE_LANES = LANES


def _rms(x, eps=EPS):
    return x * lax.rsqrt(jnp.mean(x * x, axis=-1, keepdims=True) + eps)


def _dot(a, b):
    return jnp.dot(a, b, preferred_element_type=F32)


def _dot_nt(a, b):
    return lax.dot_general(a, b, (((1,), (1,)), ((), ())), preferred_element_type=F32)


def _pack_halves(x):
    n = x.shape[1] // 2
    hi = lax.bitcast_convert_type(x[:, :n].astype(BF16).astype(F32), jnp.uint32)
    lo = lax.bitcast_convert_type(x[:, n:].astype(BF16).astype(F32), jnp.uint32)
    return hi | lax.shift_right_logical(lo, jnp.uint32(16))


def _unpack_halves(p):
    hi = lax.bitcast_convert_type(p & jnp.uint32(0xFFFF0000), F32)
    lo = lax.bitcast_convert_type(lax.shift_left(p, jnp.uint32(16)), F32)
    return hi, lo


def _resident(shape, index_map):
    return pl.BlockSpec(shape, index_map, pipeline_mode=pl.Buffered(1))


def _norm_matmul_kernel(x_ref, nw_ref, w_ref, g_ref, o_ref, *, n_norm_tiles, chunk):
    j = pl.program_id(1)

    xn = (_rms(x_ref[...].astype(F32)) * nw_ref[...]).astype(BF16)

    normed = j < n_norm_tiles
    tn = w_ref.shape[1]
    sub = max(chunk, MXU_TILE)
    for s in range(tn // sub):
        w = w_ref[:, s * sub:(s + 1) * sub]
        if w.dtype != BF16:
            w = w.astype(BF16)
        acc = _dot(xn, w)
        for c in range(sub // chunk):
            lo = s * sub + c * chunk
            a = acc[:, c * chunk:(c + 1) * chunk]
            inv = lax.rsqrt(jnp.mean(a * a, axis=-1, keepdims=True) + EPS)
            scale = jnp.where(normed, inv, 1.0)
            o_ref[:, lo:lo + chunk] = (a * scale * g_ref[:, lo:lo + chunk]).astype(o_ref.dtype)


def _norm_matmul(x, norm_w, w, gain, *, n_norm_cols, chunk, tm, tn, name):
    m, k = x.shape
    n = w.shape[1]
    assert m % tm == 0 and n % tn == 0 and tn % max(chunk, MXU_TILE) == 0 and n_norm_cols % tn == 0
    kern = functools.partial(_norm_matmul_kernel, n_norm_tiles=n_norm_cols // tn, chunk=chunk)
    return pl.pallas_call(
        kern,
        grid=(m // tm, n // tn),
        in_specs=[
            pl.BlockSpec((tm, k), lambda i, j: (i, 0)),
            pl.BlockSpec((1, k), lambda i, j: (0, 0)),
            pl.BlockSpec((k, tn), lambda i, j: (0, j)),
            pl.BlockSpec((1, tn), lambda i, j: (0, j)),
        ],
        out_specs=pl.BlockSpec((tm, tn), lambda i, j: (i, j)),
        out_shape=jax.ShapeDtypeStruct((m, n), BF16),
        compiler_params=pltpu.CompilerParams(
            dimension_semantics=("parallel", "parallel"), vmem_limit_bytes=VMEM_LIMIT),
        name=name,
    )(x, norm_w.reshape(1, k).astype(F32), w, gain.reshape(1, n).astype(F32))


def _diff_attn_kernel(lam_ref, qa_ref, qb_ref, k_ref, v_ref, g_ref, o_ref, *stat_refs, tq, n_q):
    pair = pl.program_id(2)
    stats_a = (stat_refs[0:3], stat_refs[3:6])
    stats_b = (stat_refs[6:9], stat_refs[9:12])

    def scores(q_ref, j):
        return tuple(_dot_nt(q_ref[:, c * DA_QK_DIM:(c + 1) * DA_QK_DIM],
                             k_ref[j * tq:(j + 1) * tq, c * DA_QK_DIM:(c + 1) * DA_QK_DIM])
                     for c in range(2))

    def accumulate(stats, j, s_pair, masked):
        n_t = tq // LANES
        for s, (m_ref, l_ref, acc_ref) in zip(s_pair, stats):
            if masked:
                row = lax.broadcasted_iota(jnp.int32, s.shape, 0)
                col = lax.broadcasted_iota(jnp.int32, s.shape, 1)
                s = jnp.where(col <= row, s, NEG)
            tiles = [s[:, c * LANES:(c + 1) * LANES] for c in range(n_t)]
            fold = tiles[0]
            for t_ in tiles[1:]:
                fold = jnp.maximum(fold, t_)
            m_old = m_ref[...]
            m_new = jnp.maximum(m_old, jnp.max(fold, axis=-1, keepdims=True))
            p_tiles = [jnp.exp2(t_ - m_new) for t_ in tiles]
            psum = p_tiles[0]
            for t_ in p_tiles[1:]:
                psum = psum + t_
            alpha = jnp.exp2(m_old - m_new)
            l_ref[...] = alpha * l_ref[...] + psum
            p = jnp.concatenate([t_.astype(BF16) for t_ in p_tiles], axis=1)
            pv = _dot(p, v_ref[j * tq:(j + 1) * tq, :])
            for c in range(DA_V_DIM // LANES):
                cols = slice(c * LANES, (c + 1) * LANES)
                acc_ref[:, cols] = alpha * acc_ref[:, cols] + pv[:, cols]
            m_ref[...] = m_new

    def finish(stats, rows):
        (_, l1, acc1), (_, l2, acc2) = stats
        l1, l2 = (jnp.sum(l[...], axis=-1, keepdims=True) for l in (l1, l2))
        o = acc1[...] / l1 - lam_ref[0] * (acc2[...] / l2)
        o_ref[rows, :] = (_rms(o) * g_ref[...]).astype(o_ref.dtype)

    def run(p):
        tiles = ((qa_ref, stats_a, p), (qb_ref, stats_b, n_q - 1 - p))
        for _, stats, _ in tiles:
            for m_ref, l_ref, acc_ref in stats:
                m_ref[...] = jnp.full(m_ref.shape, NEG, F32)
                l_ref[...] = jnp.zeros(l_ref.shape, F32)
                acc_ref[...] = jnp.zeros(acc_ref.shape, F32)
        pending = [scores(q_ref, 0) for q_ref, _, _ in tiles]
        for j in range(n_q - p):
            for idx, (q_ref, stats, diag) in enumerate(tiles):
                if j > diag:
                    continue
                s_pair = pending[idx]
                if j < diag:
                    pending[idx] = scores(q_ref, j + 1)
                accumulate(stats, j, s_pair, masked=(j == diag))
        finish(stats_a, slice(0, tq))
        finish(stats_b, slice(tq, 2 * tq))

    for p in range(n_q // 2):
        pl.when(pair == p)(functools.partial(run, p))


def _attn_tile_pos(tile, nq):
    b, qt = tile // nq, tile % nq
    return b * nq + jnp.where(qt < nq // 2, 2 * qt, 2 * (nq - 1 - qt) + 1)


def _diff_attn(proj, lam, gain, *, batch, seq, tq):
    t = batch * seq
    nq = seq // tq
    assert nq % 2 == 0
    width = 2 * DA_QK_DIM
    k_blk0 = DA_HEADS
    v_blk0 = 2 * DA_HEADS
    kern = functools.partial(_diff_attn_kernel, tq=tq, n_q=nq)
    stat = [pltpu.VMEM((tq, LANES), F32), pltpu.VMEM((tq, LANES), F32), pltpu.VMEM((tq, DA_V_DIM), F32)]
    return pl.pallas_call(
        kern,
        grid_spec=pltpu.PrefetchScalarGridSpec(
            num_scalar_prefetch=1,
            grid=(batch, DA_HEADS, nq // 2),
            in_specs=[
                pl.BlockSpec((tq, width), lambda b, h, p, lam: (b * nq + p, h)),
                pl.BlockSpec((tq, width), lambda b, h, p, lam: (b * nq + nq - 1 - p, h)),
                pl.BlockSpec((seq, width), lambda b, h, p, lam: (b, k_blk0 + h)),
                pl.BlockSpec((seq, width), lambda b, h, p, lam: (b, v_blk0 + h)),
                pl.BlockSpec((1, DA_V_DIM), lambda b, h, p, lam: (0, 0)),
            ],
            out_specs=pl.BlockSpec((2 * tq, DA_V_DIM), lambda b, h, p, lam: (b * (nq // 2) + p, h)),
            scratch_shapes=stat * 4,
        ),
        out_shape=jax.ShapeDtypeStruct((t, DA_HEADS * DA_V_DIM), BF16),
        compiler_params=pltpu.CompilerParams(
            dimension_semantics=("parallel", "parallel", "arbitrary"), vmem_limit_bytes=VMEM_LIMIT),
        name="diff_attn",
    )(lam, proj, proj, proj, proj, gain)


def _s5_kernel(u_ref, bd_ref, ar_ref, ai_ref, cd_ref, d_ref, o_ref, xs_ref, us_ref, ys_ref, *, seq, rows):
    ns = SSM_CHUNK_STATE
    seg_len = seq // SCAN_SEGS
    n_row_blk = seq // rows
    steps = rows // SCAN_SEGS

    for seg in range(SCAN_SEGS):
        us_ref[pl.ds(seg, seg_len, stride=SCAN_SEGS), :] = (
            u_ref[seg * seg_len:(seg + 1) * seg_len, :].astype(F32))

    def in_map(r):
        rs = slice(r * rows, (r + 1) * rows)
        xs_ref[rs, :] = _dot(us_ref[rs, :].astype(BF16), bd_ref[...])

    def out_map(r):
        rs = slice(r * rows, (r + 1) * rows)
        y = _dot(xs_ref[rs, :].astype(BF16), cd_ref[...]) + d_ref[...] * us_ref[rs, :]
        ys_ref[rs, :] = jax.nn.gelu(y)
        for seg in range(SCAN_SEGS):
            t0 = seg * seg_len + r * steps
            o_ref[t0:t0 + steps, :] = (
                ys_ref[pl.ds(r * rows + seg, steps, stride=SCAN_SEGS), :].astype(o_ref.dtype))

    ar = jnp.broadcast_to(ar_ref[...], (SCAN_SEGS, ns))
    ai = jnp.broadcast_to(ai_ref[...], (SCAN_SEGS, ns))

    def advance(t, sr, si):
        ts = slice(t * SCAN_SEGS, (t + 1) * SCAN_SEGS)
        return ar * sr - ai * si + xs_ref[ts, 0:ns], ar * si + ai * sr + xs_ref[ts, ns:2 * ns]

    in_map(0)
    fr = fi = jnp.zeros((SCAN_SEGS, ns), F32)
    for r in range(n_row_blk):
        if r + 1 < n_row_blk:
            in_map(r + 1)
        for t in range(r * steps, (r + 1) * steps):
            fr, fi = advance(t, fr, fi)

    pr, pi = ar, ai
    for _ in range(int(math.log2(seg_len))):
        pr, pi = pr * pr - pi * pi, 2.0 * pr * pi
    seg = lax.broadcasted_iota(jnp.int32, (SCAN_SEGS, ns), 0)

    def shifted(x, k):
        return jnp.where(seg >= k, pltpu.roll(x, k, 0), 0.0)

    k = 1
    while k < SCAN_SEGS:
        gr, gi = shifted(fr, k), shifted(fi, k)
        fr, fi = fr + pr * gr - pi * gi, fi + pr * gi + pi * gr
        pr, pi = pr * pr - pi * pi, 2.0 * pr * pi
        k *= 2
    sr, si = shifted(fr, 1), shifted(fi, 1)

    for r in range(n_row_blk):
        for t in range(r * steps, (r + 1) * steps):
            sr, si = advance(t, sr, si)
            ts = slice(t * SCAN_SEGS, (t + 1) * SCAN_SEGS)
            xs_ref[ts, 0:ns] = sr
            xs_ref[ts, ns:2 * ns] = si
        if r >= 1:
            out_map(r - 1)
    out_map(n_row_blk - 1)


def _s5(proj, u_col0, d_ssm, bd, a_re, a_im, cd, d_skip, *, batch, seq, rows=S5_ROW_BLOCK):
    n_chunks = d_ssm // LANES
    u_blk0 = u_col0 // LANES
    kern = functools.partial(_s5_kernel, seq=seq, rows=rows)
    return pl.pallas_call(
        kern,
        grid=(batch, n_chunks),
        in_specs=[
            pl.BlockSpec((seq, LANES), lambda b, c: (b, u_blk0 + c)),
            pl.BlockSpec((None, LANES, 2 * SSM_CHUNK_STATE), lambda b, c: (c, 0, 0)),
            pl.BlockSpec((None, 1, SSM_CHUNK_STATE), lambda b, c: (c, 0, 0)),
            pl.BlockSpec((None, 1, SSM_CHUNK_STATE), lambda b, c: (c, 0, 0)),
            pl.BlockSpec((None, 2 * SSM_CHUNK_STATE, LANES), lambda b, c: (c, 0, 0)),
            pl.BlockSpec((None, 1, LANES), lambda b, c: (c, 0, 0)),
        ],
        out_specs=pl.BlockSpec((seq, LANES), lambda b, c: (b, c)),
        out_shape=jax.ShapeDtypeStruct((batch * seq, d_ssm), BF16),
        scratch_shapes=[pltpu.VMEM((seq, 2 * SSM_CHUNK_STATE), F32),
                        pltpu.VMEM((seq, LANES), F32), pltpu.VMEM((seq, LANES), F32)],
        compiler_params=pltpu.CompilerParams(
            dimension_semantics=("parallel", "parallel"), vmem_limit_bytes=VMEM_LIMIT),
        name="s5_scan",
    )(proj, bd, a_re, a_im, cd, d_skip)


def _s5_params(lam_re, lam_im, log_dt, b_re, b_im, c_re, c_im, d_skip):
    g = lam_re.shape[0]
    nc = g // SSM_CHUNK_GROUPS
    lr = jnp.minimum(lam_re.astype(F32), -1e-4)
    li = lam_im.astype(F32)
    dt = jnp.exp(log_dt.astype(F32))[:, None]
    mag = jnp.exp(lr * dt)
    lb_re, lb_im = mag * jnp.cos(li * dt), mag * jnp.sin(li * dt)
    den = lr * lr + li * li
    coef_re = ((lb_re - 1.0) * lr + lb_im * li) / den
    coef_im = (lb_im * lr - (lb_re - 1.0) * li) / den
    br, bi = b_re.astype(F32), b_im.astype(F32)
    bb_re = coef_re[..., None] * br - coef_im[..., None] * bi
    bb_im = coef_re[..., None] * bi + coef_im[..., None] * br
    eye = jnp.eye(SSM_CHUNK_GROUPS, dtype=F32)

    def pack_in(bb):
        bb = bb.reshape(nc, SSM_CHUNK_GROUPS, SSM_STATE, SSM_GROUP)
        return jnp.einsum('cgph,gk->cghkp', bb, eye).reshape(nc, LANES, SSM_CHUNK_STATE)

    def pack_out(cc):
        cc = cc.astype(F32).reshape(nc, SSM_CHUNK_GROUPS, SSM_GROUP, SSM_STATE)
        return jnp.einsum('cghp,gk->ckpgh', cc, eye).reshape(nc, SSM_CHUNK_STATE, LANES)

    bd = jnp.concatenate([pack_in(bb_re), pack_in(bb_im)], axis=-1).astype(BF16)
    cd = jnp.concatenate([pack_out(c_re), -pack_out(c_im)], axis=1).astype(BF16)
    a_re = lb_re.reshape(nc, 1, SSM_CHUNK_STATE)
    a_im = lb_im.reshape(nc, 1, SSM_CHUNK_STATE)
    dd = d_skip.astype(F32).reshape(nc, 1, LANES)
    return bd, a_re, a_im, cd, dd


def _mix_out_kernel(a_ref, y_ref, x_ref, gw_ref, gb_ref, nw_ref, wo_ref, o_ref):
    d_attn = a_ref.shape[1]
    y = y_ref[...]
    gate = _dot(y, gw_ref[...]) + gb_ref[...]
    s = y.astype(F32) * jax.nn.sigmoid(gate)
    sn = (_rms(s) * nw_ref[...]).astype(BF16)
    acc = _dot(a_ref[...], wo_ref[0:d_attn, :]) + _dot(sn, wo_ref[d_attn:, :])
    o_ref[...] = x_ref[...] + acc


def _mix_out(a, y, x, glu_w, glu_b, norm_w, w_out, *, tm, attn_tiles):
    t, d = x.shape
    d_attn, d_ssm = a.shape[1], y.shape[1]
    const = lambda i: (0, 0)
    return pl.pallas_call(
        _mix_out_kernel,
        grid=(t // tm,),
        in_specs=[
            pl.BlockSpec((tm, d_attn), lambda i: (_attn_tile_pos(i, attn_tiles), 0)),
            pl.BlockSpec((tm, d_ssm), lambda i: (i, 0)),
            pl.BlockSpec((tm, d), lambda i: (i, 0)),
            _resident((d_ssm, d_ssm), const),
            _resident((1, d_ssm), const),
            _resident((1, d_ssm), const),
            _resident((d, d), const),
        ],
        out_specs=pl.BlockSpec((tm, d), lambda i: (i, 0)),
        out_shape=jax.ShapeDtypeStruct((t, d), F32),
        compiler_params=pltpu.CompilerParams(
            dimension_semantics=("parallel",), vmem_limit_bytes=VMEM_LIMIT),
        name="mix_out",
    )(a, y, x, glu_w, glu_b.reshape(1, d_ssm).astype(F32), norm_w.reshape(1, d_ssm).astype(F32), w_out)


def _xattn_route_kernel(q_ref, k_ref, v_ref, h_ref, xo_ref, nw_ref, rhi_ref, rlo_ref, rb_ref,
                        h2_ref, hn_ref, eid_ref, wts_ref):
    d = h_ref.shape[1]
    hd = d // X_HEADS
    heads = [slice(h * hd, (h + 1) * hd) for h in range(X_HEADS)]
    h2 = h_ref[...]
    s_next = _dot_nt(q_ref[:, heads[0]], k_ref[:, heads[0]])
    for h, sl in enumerate(heads):
        s = s_next
        if h + 1 < X_HEADS:
            s_next = _dot_nt(q_ref[:, heads[h + 1]], k_ref[:, heads[h + 1]])
        p = jnp.exp(s - jnp.max(s, axis=-1, keepdims=True))
        p = p * (1.0 / jnp.sum(p, axis=-1, keepdims=True))
        o = _dot(p.astype(BF16), v_ref[:, sl]).astype(BF16)
        h2 = h2 + _dot(o, xo_ref[sl, :])
    h2_ref[...] = h2
    hn = _rms(h2) * nw_ref[...]
    hn_ref[...] = _pack_halves(hn)

    hi = hn.astype(BF16)
    lo = (hn - hi.astype(F32)).astype(BF16)
    hi_both = _dot(hi, jnp.concatenate([rhi_ref[...], rlo_ref[...]], axis=1))
    logits = (hi_both[:, :ROUTE_LANES] + hi_both[:, ROUTE_LANES:] + _dot(lo, rhi_ref[...])) + rb_ref[...]

    lt = logits.T
    idx = lax.broadcasted_iota(jnp.int32, (SUBLANES, lt.shape[1]), 0)

    def first_row(cond):
        return jnp.min(jnp.where(cond, idx, SUBLANES), axis=0, keepdims=True)

    def softmax_rows(x):
        e = jnp.exp(x - jnp.max(x, axis=0, keepdims=True))
        return e / jnp.sum(e, axis=0, keepdims=True)

    p_c = softmax_rows(lt[0:MOE_GROUPS, :])
    p_grp = jnp.max(p_c, axis=0, keepdims=True)
    grp = first_row(p_c == p_grp)
    lf = lt[MOE_GROUPS:MOE_GROUPS + EXP_PER_GROUP, :]
    for g in range(1, MOE_GROUPS):
        lo_row = MOE_GROUPS + g * EXP_PER_GROUP
        lf = jnp.where(grp == g, lt[lo_row:lo_row + EXP_PER_GROUP, :], lf)
    pf = softmax_rows(lf)
    v1 = jnp.max(pf, axis=0, keepdims=True)
    i1 = first_row(pf == v1)
    rest = idx != i1
    v2 = jnp.max(jnp.where(rest, pf, -1.0), axis=0, keepdims=True)
    i2 = first_row(rest & (pf == v2))
    tot = v1 + v2
    e1 = grp * EXP_PER_GROUP + i1
    e2 = grp * EXP_PER_GROUP + i2
    eid_ref[...] = jnp.where(idx == 0, e1, jnp.where(idx == 1, e2, 0))
    wts_ref[...] = jnp.where(idx == 0, v1 / tot * p_grp, jnp.where(idx == 1, v2 / tot * p_grp, 0.0))


def _xattn_route(q, kv, h1, xo_w, norm_w, r_hi, r_lo, r_b, *, batch, seq, mem_len, tm):
    t, d = h1.shape
    n = seq // tm
    const = lambda b, i: (0, 0)
    row = lambda b, i: (b * n + i, 0)
    return pl.pallas_call(
        _xattn_route_kernel,
        grid=(batch, n),
        in_specs=[
            pl.BlockSpec((tm, d), row),
            pl.BlockSpec((mem_len, d), lambda b, i: (b, 0)),
            pl.BlockSpec((mem_len, d), lambda b, i: (b, 1)),
            pl.BlockSpec((tm, d), row),
            _resident((d, d), const),
            _resident((1, d), const),
            _resident((d, ROUTE_LANES), const),
            _resident((d, ROUTE_LANES), const),
            _resident((1, ROUTE_LANES), const),
        ],
        out_specs=[
            pl.BlockSpec((tm, d), row),
            pl.BlockSpec((tm, d // 2), row),
            pl.BlockSpec((SUBLANES, tm), row),
            pl.BlockSpec((SUBLANES, tm), row),
        ],
        out_shape=[
            jax.ShapeDtypeStruct((t, d), F32),
            jax.ShapeDtypeStruct((t, d // 2), jnp.uint32),
            jax.ShapeDtypeStruct((t // tm * SUBLANES, tm), jnp.int32),
            jax.ShapeDtypeStruct((t // tm * SUBLANES, tm), F32),
        ],
        compiler_params=pltpu.CompilerParams(
            dimension_semantics=("parallel", "parallel"), vmem_limit_bytes=VMEM_LIMIT),
        name="xattn_route",
    )(q, kv, kv, h1, xo_w, norm_w.reshape(1, d).astype(F32), r_hi, r_lo, r_b)


def _moe_kernel(be_ref, par_ref, first_ref, nxt_ref, base_ref, nval_ref, tok_ref, nu_ref,
                hn_hbm, wg_hbm, wu_hbm, wd_hbm, o_ref,
                xbuf, wgb, wub, wdb, gsem, wsem):
    b = pl.program_id(0)
    n_used = nu_ref[0]

    def weight_copies(e, slot):
        copies = []
        for hbm, buf in ((wg_hbm, wgb), (wu_hbm, wub), (wd_hbm, wdb)):
            rows = hbm.shape[1] // WEIGHT_DMA_CHUNKS
            for c in range(WEIGHT_DMA_CHUNKS):
                sl = pl.ds(c * rows, rows)
                copies.append(pltpu.make_async_copy(hbm.at[e, sl], buf.at[slot, sl], wsem.at[slot]))
        return copies

    def groups(blk):
        return (nval_ref[blk] + ROW_GROUP - 1) // ROW_GROUP

    def start_gather(blk, slot):
        base = base_ref[blk]

        def body(g, carry):
            for r in range(ROW_GROUP):
                tok = tok_ref[base + g * ROW_GROUP + r]
                pltpu.make_async_copy(hn_hbm.at[pl.ds(tok, 1)], xbuf.at[slot, g, pl.ds(r, 1)],
                                      gsem.at[slot]).start()
            return carry
        lax.fori_loop(0, groups(blk), body, 0)

    def wait_gather(blk, slot):
        filled = xbuf.at[slot, pl.ds(0, groups(blk))]
        pltpu.make_async_copy(filled, filled, gsem.at[slot]).wait()

    @pl.when(b == 0)
    def _():
        xbuf[...] = jnp.zeros(xbuf.shape, xbuf.dtype)
        for c in weight_copies(be_ref[0], par_ref[0]):
            c.start()
        start_gather(0, 0)

    @pl.when(b < n_used)
    def _():
        slot = b % 2
        wslot = par_ref[b]
        is_first = first_ref[b] == 1

        @pl.when(is_first & (nxt_ref[b] >= 0))
        def _():
            for c in weight_copies(nxt_ref[b], 1 - wslot):
                c.start()

        @pl.when(b + 1 < n_used)
        def _():
            start_gather(b + 1, 1 - slot)

        @pl.when(is_first)
        def _():
            for c in weight_copies(0, wslot):
                c.wait()

        wait_gather(b, slot)
        half = xbuf.shape[-1]

        def experts(rows):
            groups_ = rows // ROW_GROUP
            x_hi, x_lo = (v.astype(BF16)
                          for v in _unpack_halves(xbuf[slot, 0:groups_].reshape(rows, half)))

            def up(w):
                return (_dot(x_hi, w[wslot, 0:half, :].astype(BF16))
                        + _dot(x_lo, w[wslot, half:, :].astype(BF16)))

            mid = (jax.nn.silu(up(wgb)) * up(wub)).astype(BF16)
            o_ref[0:rows, :] = _pack_halves(_dot(mid, wdb[wslot].astype(BF16)))
            if rows < MOE_BLOCK:
                o_ref[rows:, :] = jnp.zeros((MOE_BLOCK - rows, half), o_ref.dtype)

        small = nval_ref[b] <= MOE_BLOCK // 2
        pl.when(small)(functools.partial(experts, MOE_BLOCK // 2))
        pl.when(jnp.logical_not(small))(functools.partial(experts, MOE_BLOCK))

    @pl.when(b >= n_used)
    def _():
        o_ref[...] = jnp.zeros(o_ref.shape, o_ref.dtype)


def _moe_experts(hn_packed, w_gate, w_up, w_down, meta, tok_sorted, n_used):
    d, d_ff = w_gate.shape[1:]
    half = hn_packed.shape[1]
    blk_exp, par, first, nxt, base, nval = meta
    n_blocks = blk_exp.shape[0]
    any_spec = pl.BlockSpec(memory_space=pl.ANY)
    return pl.pallas_call(
        _moe_kernel,
        grid_spec=pltpu.PrefetchScalarGridSpec(
            num_scalar_prefetch=8,
            grid=(n_blocks,),
            in_specs=[any_spec, any_spec, any_spec, any_spec],
            out_specs=pl.BlockSpec((MOE_BLOCK, half), lambda b, *_: (b, 0)),
            scratch_shapes=[
                pltpu.VMEM((2, MOE_BLOCK // ROW_GROUP, ROW_GROUP, half), jnp.uint32),
                pltpu.VMEM((2, d, d_ff), F32),
                pltpu.VMEM((2, d, d_ff), F32),
                pltpu.VMEM((2, d_ff, d), F32),
                pltpu.SemaphoreType.DMA((2,)),
                pltpu.SemaphoreType.DMA((2,)),
            ],
        ),
        out_shape=jax.ShapeDtypeStruct((n_blocks * MOE_BLOCK, half), jnp.uint32),
        compiler_params=pltpu.CompilerParams(
            dimension_semantics=("arbitrary",), vmem_limit_bytes=VMEM_LIMIT),
        name="moe_experts",
    )(blk_exp, par, first, nxt, base, nval, tok_sorted, n_used, hn_packed, w_gate, w_up, w_down)


def _combine_kernel(pos_ref, y_hbm, h_ref, w_ref, o_ref, ybuf, sem, *, tm):
    i = pl.program_id(0)
    n_groups = tm // ROW_GROUP

    def start_gather(tile, slot):
        base = tile * (tm * TOP_K_FINE)

        def body(g, carry):
            for r in range(ROW_GROUP):
                for k in range(TOP_K_FINE):
                    row = pos_ref[base + (g * ROW_GROUP + r) * TOP_K_FINE + k]
                    pltpu.make_async_copy(y_hbm.at[pl.ds(row, 1)], ybuf.at[slot, k, g, pl.ds(r, 1)],
                                          sem.at[slot]).start()
            return carry
        lax.fori_loop(0, n_groups, body, 0)

    @pl.when(i == 0)
    def _():
        start_gather(0, 0)

    @pl.when(i + 1 < pl.num_programs(0))
    def _():
        start_gather(i + 1, (i + 1) % 2)

    slot = i % 2
    pltpu.make_async_copy(ybuf.at[slot], ybuf.at[slot], sem.at[slot]).wait()
    w = w_ref[...]
    half = ybuf.shape[-1]
    y0 = _unpack_halves(ybuf[slot, 0].reshape(tm, half))
    y1 = _unpack_halves(ybuf[slot, 1].reshape(tm, half))
    for c in range(2):
        cols = slice(c * half, (c + 1) * half)
        o_ref[:, cols] = h_ref[:, cols] + (w[:, 0:1] * y0[c] + w[:, 1:2] * y1[c])


def _combine(y, h2, wts, pos, *, tm):
    t, d = h2.shape
    kern = functools.partial(_combine_kernel, tm=tm)
    return pl.pallas_call(
        kern,
        grid_spec=pltpu.PrefetchScalarGridSpec(
            num_scalar_prefetch=1,
            grid=(t // tm,),
            in_specs=[
                pl.BlockSpec(memory_space=pl.ANY),
                pl.BlockSpec((tm, d), lambda i, pos: (i, 0)),
                pl.BlockSpec((tm, ROUTE_LANES), lambda i, pos: (i, 0)),
            ],
            out_specs=pl.BlockSpec((tm, d), lambda i, pos: (i, 0)),
            scratch_shapes=[
                pltpu.VMEM((2, TOP_K_FINE, tm // ROW_GROUP, ROW_GROUP, y.shape[1]), jnp.uint32),
                pltpu.SemaphoreType.DMA((2,)),
            ],
        ),
        out_shape=jax.ShapeDtypeStruct((t, d), F32),
        compiler_params=pltpu.CompilerParams(
            dimension_semantics=("arbitrary",), vmem_limit_bytes=VMEM_LIMIT),
        name="moe_combine",
    )(pos, y, h2, wts)


def _lookup(table, idx):
    sel = idx[:, None] == jnp.arange(table.shape[0], dtype=jnp.int32)[None, :]
    return jnp.sum(jnp.where(sel, table[None, :], 0), axis=1).astype(jnp.int32)


def _dispatch(eid, n_tokens):
    n_assign = n_tokens * TOP_K_FINE
    experts = jnp.arange(N_EXPERTS, dtype=jnp.int32)
    e_flat = eid.reshape(n_assign)
    a_ids = jnp.arange(n_assign, dtype=jnp.int32)
    e_s, order = lax.sort_key_val(e_flat, a_ids)
    counts = jnp.sum((e_flat[:, None] == experts[None, :]).astype(jnp.int32), axis=0)
    starts = jnp.cumsum(counts) - counts
    nb = (counts + MOE_BLOCK - 1) // MOE_BLOCK
    blk_end = jnp.cumsum(nb)
    blk_start = blk_end - nb
    n_used = blk_end[-1]
    n_blocks = (n_assign + N_EXPERTS * (MOE_BLOCK - 1)) // MOE_BLOCK
    b_ids = jnp.arange(n_blocks, dtype=jnp.int32)
    used = b_ids < n_used
    blk_exp = jnp.minimum(jnp.sum((blk_end[None, :] <= b_ids[:, None]).astype(jnp.int32), axis=1),
                          N_EXPERTS - 1)
    j = b_ids - _lookup(blk_start, blk_exp)
    base = jnp.where(used, _lookup(starts, blk_exp) + j * MOE_BLOCK, 0)
    nval = jnp.where(used, jnp.clip(_lookup(counts, blk_exp) - j * MOE_BLOCK, 0, MOE_BLOCK), 0)
    first = (used & (j == 0)).astype(jnp.int32)
    active = counts > 0
    par = _lookup(jnp.cumsum(active.astype(jnp.int32)) - 1, blk_exp) & 1
    later = lax.cummin(jnp.where(active, experts, N_EXPERTS), reverse=True)
    nxt_e = jnp.concatenate([later[1:], jnp.full((1,), N_EXPERTS, jnp.int32)])
    nxt = _lookup(jnp.where(nxt_e == N_EXPERTS, -1, nxt_e), blk_exp)
    meta = tuple(v.astype(jnp.int32) for v in (blk_exp, par, first, nxt, base, nval))
    row_sorted = a_ids + _lookup(blk_start * MOE_BLOCK - starts, e_s)
    _, pos = lax.sort_key_val(order, row_sorted)
    tok_sorted = jnp.concatenate([lax.shift_right_logical(order, 1), jnp.zeros((ROW_GROUP,), jnp.int32)])
    return meta, tok_sorted, pos, n_used.astype(jnp.int32).reshape(1)


def kernel(x, mem, norm1_w, w_in, q_norm_w, k_norm_w, lambda_q1, lambda_k1, lambda_q2, lambda_k2, subln_w, ssm_lambda_re, ssm_lambda_im, ssm_log_dt, ssm_b_re, ssm_b_im, ssm_c_re, ssm_c_im, ssm_d, ssm_glu_w, ssm_glu_b, ssm_out_norm_w, w_out, norm2_w, mem_norm_w, xq_w, xkv_w, xq_norm_w, xk_norm_w, xo_w, norm3_w, router_coarse_w, router_coarse_b, router_fine_w, router_fine_b, expert_w_gate, expert_w_up, expert_w_down):
    batch, seq, d = x.shape
    mem_len = mem.shape[1]
    t = batch * seq
    depth = norm1_w.shape[0]
    d_attn = DA_HEADS * DA_V_DIM
    d_ssm = d - d_attn
    qk_cols = DA_HEADS * 2 * DA_QK_DIM
    x_hd = d // X_HEADS
    h = x.reshape(t, d)
    mem2 = mem.reshape(batch * mem_len, d)

    for l in range(depth):
        lam_init = 0.8 - 0.6 * math.exp(-0.3 * l)
        lam = (jnp.exp(jnp.sum(lambda_q1[l].astype(F32) * lambda_k1[l].astype(F32)))
               - jnp.exp(jnp.sum(lambda_q2[l].astype(F32) * lambda_k2[l].astype(F32)))
               + lam_init).reshape(1)

        n_rep = qk_cols // DA_QK_DIM
        in_gain = jnp.concatenate([
            jnp.tile(q_norm_w[l].astype(F32) * (DA_QK_DIM ** -0.5 * LOG2E), n_rep),
            jnp.tile(k_norm_w[l].astype(F32), n_rep),
            jnp.ones((d_attn + d_ssm,), F32)])
        proj = _norm_matmul(h, norm1_w[l], w_in[l].astype(BF16), in_gain,
                            n_norm_cols=2 * qk_cols, chunk=DA_QK_DIM, tm=PROJ_ROW_TILE, tn=PROJ_COL_TILE, name="in_proj")
        sub_gain = (subln_w[l].astype(F32) * (1.0 - lam_init)).reshape(1, DA_V_DIM)
        a = _diff_attn(proj, lam, sub_gain, batch=batch, seq=seq, tq=ATTN_TILE)

        bd, a_re, a_im, cd, dd = _s5_params(ssm_lambda_re[l], ssm_lambda_im[l], ssm_log_dt[l],
                                            ssm_b_re[l], ssm_b_im[l], ssm_c_re[l], ssm_c_im[l], ssm_d[l])
        y = _s5(proj, 2 * qk_cols + d_attn, d_ssm, bd, a_re, a_im, cd, dd, batch=batch, seq=seq)
        h = _mix_out(a, y, h, ssm_glu_w[l].astype(BF16), ssm_glu_b[l], ssm_out_norm_w[l],
                     w_out[l].astype(BF16), tm=ATTN_TILE, attn_tiles=seq // ATTN_TILE)

        kv_gain = jnp.concatenate([jnp.tile(xk_norm_w[l].astype(F32), X_HEADS), jnp.ones((d,), F32)])
        kv = _norm_matmul(mem2, mem_norm_w[l], xkv_w[l], kv_gain,
                          n_norm_cols=d, chunk=x_hd, tm=batch * mem_len, tn=KV_COL_TILE, name="kv_proj")
        q_gain = jnp.tile(xq_norm_w[l].astype(F32) * (x_hd ** -0.5), X_HEADS)
        q = _norm_matmul(h, norm2_w[l], xq_w[l].astype(BF16), q_gain,
                         n_norm_cols=d, chunk=x_hd, tm=PROJ_ROW_TILE, tn=PROJ_COL_TILE, name="xq_proj")
        r_w = jnp.concatenate([router_coarse_w[l].astype(F32), router_fine_w[l].astype(F32)], axis=1)
        r_w = jnp.pad(r_w, ((0, 0), (0, ROUTE_LANES - r_w.shape[1])))
        r_hi = r_w.astype(BF16)
        r_lo = (r_w - r_hi.astype(F32)).astype(BF16)
        r_b = jnp.concatenate([router_coarse_b[l].astype(F32), router_fine_b[l].astype(F32)])
        r_b = jnp.pad(r_b, (0, ROUTE_LANES - r_b.shape[0])).reshape(1, ROUTE_LANES)
        h2, hn3, eid_t, wts_t = _xattn_route(q, kv, h, xo_w[l].astype(BF16), norm3_w[l], r_hi, r_lo, r_b,
                                             batch=batch, seq=seq, mem_len=mem_len, tm=XATTN_ROW_TILE)

        def per_token(x_t):
            return x_t.reshape(-1, SUBLANES, x_t.shape[1])[:, :TOP_K_FINE, :].transpose(0, 2, 1).reshape(t, TOP_K_FINE)

        eid = per_token(eid_t)
        wts = jnp.pad(per_token(wts_t), ((0, 0), (0, ROUTE_LANES - TOP_K_FINE)))

        meta, tok_sorted, pos, n_used = _dispatch(eid, t)
        y = _moe_experts(hn3, expert_w_gate[l], expert_w_up[l], expert_w_down[l], meta, tok_sorted, n_used)
        h = _combine(y, h2, wts, pos, tm=COMBINE_ROW_TILE)

    return h.reshape(batch, seq, d)
```

```python
import functools
import math

import jax
import jax.numpy as jnp
from jax import lax
from jax.experimental import pallas as pl
from jax.experimental.pallas import tpu as pltpu

F32 = jnp.float32
BF16 = jnp.bfloat16

EPS = 1e-6
DA_HEADS = 4
DA_QK_DIM = 128
DA_V_DIM = 256
SSM_GROUP = 16
SSM_STATE = 64
X_HEADS = 4
MOE_GROUPS = 8
EXP_PER_GROUP = 8
N_EXPERTS = MOE_GROUPS * EXP_PER_GROUP
TOP_K_FINE = 2

LANES = 128
SUBLANES = 8
MXU_TILE = 256
VMEM_LIMIT = 56 * 1024 * 1024
NEG = -1e30
LOG2E = math.log2(math.e)

SSM_CHUNK_GROUPS = LANES // SSM_GROUP
SSM_CHUNK_STATE = SSM_CHUNK_GROUPS * SSM_STATE
SCAN_SEGS = SUBLANES
ATTN_TILE = 512
PROJ_ROW_TILE = 1024
PROJ_COL_TILE = 2048
KV_COL_TILE = 512
XATTN_ROW_TILE = 512
COMBINE_ROW_TILE = 256
S5_ROW_BLOCK = 256
MOE_BLOCK = 256
ROW_GROUP = SUBLANES
WEIGHT_DMA_CHUNKS = 4
WEIGHT_DMA_PRIORITY = 1
ROUTE_LANES = LANES


def _rms(x, eps=EPS):
    return x * lax.rsqrt(jnp.mean(x * x, axis=-1, keepdims=True) + eps)


def _dot(a, b):
    return jnp.dot(a, b, preferred_element_type=F32)


def _dot_nt(a, b):
    return lax.dot_general(a, b, (((1,), (1,)), ((), ())), preferred_element_type=F32)


def _pack_halves(x):
    n = x.shape[1] // 2
    hi = lax.bitcast_convert_type(x[:, :n].astype(BF16).astype(F32), jnp.uint32)
    lo = lax.bitcast_convert_type(x[:, n:].astype(BF16).astype(F32), jnp.uint32)
    return hi | lax.shift_right_logical(lo, jnp.uint32(16))


def _unpack_halves(p):
    hi = lax.bitcast_convert_type(p & jnp.uint32(0xFFFF0000), F32)
    lo = lax.bitcast_convert_type(lax.shift_left(p, jnp.uint32(16)), F32)
    return hi, lo


def _resident(shape, index_map):
    return pl.BlockSpec(shape, index_map, pipeline_mode=pl.Buffered(1))


def _norm_matmul_kernel(x_ref, nw_ref, w_ref, g_ref, o_ref, *, n_norm_tiles, chunk):
    j = pl.program_id(1)

    xn = (_rms(x_ref[...].astype(F32)) * nw_ref[...]).astype(BF16)

    normed = j < n_norm_tiles
    tn = w_ref.shape[1]
    sub = max(chunk, MXU_TILE)
    for s in range(tn // sub):
        w = w_ref[:, s * sub:(s + 1) * sub]
        if w.dtype != BF16:
            w = w.astype(BF16)
        acc = _dot(xn, w)
        for c in range(sub // chunk):
            lo = s * sub + c * chunk
            a = acc[:, c * chunk:(c + 1) * chunk]
            inv = lax.rsqrt(jnp.mean(a * a, axis=-1, keepdims=True) + EPS)
            scale = jnp.where(normed, inv, 1.0)
            o_ref[:, lo:lo + chunk] = (a * scale * g_ref[:, lo:lo + chunk]).astype(o_ref.dtype)


def _norm_matmul(x, norm_w, w, gain, *, n_norm_cols, chunk, tm, tn, name):
    m, k = x.shape
    n = w.shape[1]
    assert m % tm == 0 and n % tn == 0 and tn % max(chunk, MXU_TILE) == 0 and n_norm_cols % tn == 0
    kern = functools.partial(_norm_matmul_kernel, n_norm_tiles=n_norm_cols // tn, chunk=chunk)
    return pl.pallas_call(
        kern,
        grid=(m // tm, n // tn),
        in_specs=[
            pl.BlockSpec((tm, k), lambda i, j: (i, 0)),
            pl.BlockSpec((1, k), lambda i, j: (0, 0)),
            pl.BlockSpec((k, tn), lambda i, j: (0, j)),
            pl.BlockSpec((1, tn), lambda i, j: (0, j)),
        ],
        out_specs=pl.BlockSpec((tm, tn), lambda i, j: (i, j)),
        out_shape=jax.ShapeDtypeStruct((m, n), BF16),
        compiler_params=pltpu.CompilerParams(
            dimension_semantics=("parallel", "parallel"), vmem_limit_bytes=VMEM_LIMIT),
        name=name,
    )(x, norm_w.reshape(1, k).astype(F32), w, gain.reshape(1, n).astype(F32))


def _diff_attn_kernel(lam_ref, qa_ref, qb_ref, k_ref, v_ref, g_ref, o_ref, *stat_refs, tq, n_q):
    pair = pl.program_id(2)
    stats_a = (stat_refs[0:3], stat_refs[3:6])
    stats_b = (stat_refs[6:9], stat_refs[9:12])

    def scores(q_ref, j):
        return tuple(_dot_nt(q_ref[:, c * DA_QK_DIM:(c + 1) * DA_QK_DIM],
                             k_ref[j * tq:(j + 1) * tq, c * DA_QK_DIM:(c + 1) * DA_QK_DIM])
                     for c in range(2))

    def accumulate(stats, j, s_pair, masked):
        n_t = tq // LANES
        for s, (m_ref, l_ref, acc_ref) in zip(s_pair, stats):
            if masked:
                row = lax.broadcasted_iota(jnp.int32, s.shape, 0)
                col = lax.broadcasted_iota(jnp.int32, s.shape, 1)
                s = jnp.where(col <= row, s, NEG)
            tiles = [s[:, c * LANES:(c + 1) * LANES] for c in range(n_t)]
            fold = tiles[0]
            for t_ in tiles[1:]:
                fold = jnp.maximum(fold, t_)
            m_old = m_ref[...]
            m_new = jnp.maximum(m_old, jnp.max(fold, axis=-1, keepdims=True))
            p_tiles = [jnp.exp2(t_ - m_new) for t_ in tiles]
            psum = p_tiles[0]
            for t_ in p_tiles[1:]:
                psum = psum + t_
            alpha = jnp.exp2(m_old - m_new)
            l_ref[...] = alpha * l_ref[...] + psum
            p = jnp.concatenate([t_.astype(BF16) for t_ in p_tiles], axis=1)
            pv = _dot(p, v_ref[j * tq:(j + 1) * tq, :])
            for c in range(DA_V_DIM // LANES):
                cols = slice(c * LANES, (c + 1) * LANES)
                acc_ref[:, cols] = alpha * acc_ref[:, cols] + pv[:, cols]
            m_ref[...] = m_new

    def finish(stats, rows):
        (_, l1, acc1), (_, l2, acc2) = stats
        l1, l2 = (jnp.sum(l[...], axis=-1, keepdims=True) for l in (l1, l2))
        o = acc1[...] / l1 - lam_ref[0] * (acc2[...] / l2)
        o_ref[rows, :] = (_rms(o) * g_ref[...]).astype(o_ref.dtype)

    def run(p):
        tiles = ((qa_ref, stats_a, p), (qb_ref, stats_b, n_q - 1 - p))
        for _, stats, _ in tiles:
            for m_ref, l_ref, acc_ref in stats:
                m_ref[...] = jnp.full(m_ref.shape, NEG, F32)
                l_ref[...] = jnp.zeros(l_ref.shape, F32)
                acc_ref[...] = jnp.zeros(acc_ref.shape, F32)
        pending = [scores(q_ref, 0) for q_ref, _, _ in tiles]
        for j in range(n_q - p):
            for idx, (q_ref, stats, diag) in enumerate(tiles):
                if j > diag:
                    continue
                s_pair = pending[idx]
                if j < diag:
                    pending[idx] = scores(q_ref, j + 1)
                accumulate(stats, j, s_pair, masked=(j == diag))
        finish(stats_a, slice(0, tq))
        finish(stats_b, slice(tq, 2 * tq))

    for p in range(n_q // 2):
        pl.when(pair == p)(functools.partial(run, p))


def _attn_tile_pos(tile, nq):
    b, qt = tile // nq, tile % nq
    return b * nq + jnp.where(qt < nq // 2, 2 * qt, 2 * (nq - 1 - qt) + 1)


def _diff_attn(proj, lam, gain, *, batch, seq, tq):
    t = batch * seq
    nq = seq // tq
    assert nq % 2 == 0
    width = 2 * DA_QK_DIM
    k_blk0 = DA_HEADS
    v_blk0 = 2 * DA_HEADS
    kern = functools.partial(_diff_attn_kernel, tq=tq, n_q=nq)
    stat = [pltpu.VMEM((tq, LANES), F32), pltpu.VMEM((tq, LANES), F32), pltpu.VMEM((tq, DA_V_DIM), F32)]
    return pl.pallas_call(
        kern,
        grid_spec=pltpu.PrefetchScalarGridSpec(
            num_scalar_prefetch=1,
            grid=(batch, DA_HEADS, nq // 2),
            in_specs=[
                pl.BlockSpec((tq, width), lambda b, h, p, lam: (b * nq + p, h)),
                pl.BlockSpec((tq, width), lambda b, h, p, lam: (b * nq + nq - 1 - p, h)),
                pl.BlockSpec((seq, width), lambda b, h, p, lam: (b, k_blk0 + h)),
                pl.BlockSpec((seq, width), lambda b, h, p, lam: (b, v_blk0 + h)),
                pl.BlockSpec((1, DA_V_DIM), lambda b, h, p, lam: (0, 0)),
            ],
            out_specs=pl.BlockSpec((2 * tq, DA_V_DIM), lambda b, h, p, lam: (b * (nq // 2) + p, h)),
            scratch_shapes=stat * 4,
        ),
        out_shape=jax.ShapeDtypeStruct((t, DA_HEADS * DA_V_DIM), BF16),
        compiler_params=pltpu.CompilerParams(
            dimension_semantics=("parallel", "parallel", "arbitrary"), vmem_limit_bytes=VMEM_LIMIT),
        name="diff_attn",
    )(lam, proj, proj, proj, proj, gain)


def _s5_kernel(u_ref, bd_ref, ar_ref, ai_ref, cd_ref, d_ref, o_ref, xs_ref, us_ref, ys_ref, *, seq, rows):
    ns = SSM_CHUNK_STATE
    seg_len = seq // SCAN_SEGS
    n_row_blk = seq // rows
    steps = rows // SCAN_SEGS

    for seg in range(SCAN_SEGS):
        us_ref[pl.ds(seg, seg_len, stride=SCAN_SEGS), :] = (
            u_ref[seg * seg_len:(seg + 1) * seg_len, :].astype(F32))

    def in_map(r):
        rs = slice(r * rows, (r + 1) * rows)
        xs_ref[rs, :] = _dot(us_ref[rs, :].astype(BF16), bd_ref[...])

    def out_map(r):
        rs = slice(r * rows, (r + 1) * rows)
        y = _dot(xs_ref[rs, :].astype(BF16), cd_ref[...]) + d_ref[...] * us_ref[rs, :]
        ys_ref[rs, :] = jax.nn.gelu(y)
        for seg in range(SCAN_SEGS):
            t0 = seg * seg_len + r * steps
            o_ref[t0:t0 + steps, :] = (
                ys_ref[pl.ds(r * rows + seg, steps, stride=SCAN_SEGS), :].astype(o_ref.dtype))

    ar = jnp.broadcast_to(ar_ref[...], (SCAN_SEGS, ns))
    ai = jnp.broadcast_to(ai_ref[...], (SCAN_SEGS, ns))

    def advance(t, sr, si):
        ts = slice(t * SCAN_SEGS, (t + 1) * SCAN_SEGS)
        return ar * sr - ai * si + xs_ref[ts, 0:ns], ar * si + ai * sr + xs_ref[ts, ns:2 * ns]

    in_map(0)
    fr = fi = jnp.zeros((SCAN_SEGS, ns), F32)
    for r in range(n_row_blk):
        if r + 1 < n_row_blk:
            in_map(r + 1)
        for t in range(r * steps, (r + 1) * steps):
            fr, fi = advance(t, fr, fi)

    pr, pi = ar, ai
    for _ in range(int(math.log2(seg_len))):
        pr, pi = pr * pr - pi * pi, 2.0 * pr * pi
    seg = lax.broadcasted_iota(jnp.int32, (SCAN_SEGS, ns), 0)

    def shifted(x, k):
        return jnp.where(seg >= k, pltpu.roll(x, k, 0), 0.0)

    k = 1
    while k < SCAN_SEGS:
        gr, gi = shifted(fr, k), shifted(fi, k)
        fr, fi = fr + pr * gr - pi * gi, fi + pr * gi + pi * gr
        pr, pi = pr * pr - pi * pi, 2.0 * pr * pi
        k *= 2
    sr, si = shifted(fr, 1), shifted(fi, 1)

    for r in range(n_row_blk):
        for t in range(r * steps, (r + 1) * steps):
            sr, si = advance(t, sr, si)
            ts = slice(t * SCAN_SEGS, (t + 1) * SCAN_SEGS)
            xs_ref[ts, 0:ns] = sr
            xs_ref[ts, ns:2 * ns] = si
        if r >= 1:
            out_map(r - 1)
    out_map(n_row_blk - 1)


def _s5(proj, u_col0, d_ssm, bd, a_re, a_im, cd, d_skip, *, batch, seq, rows=S5_ROW_BLOCK):
    n_chunks = d_ssm // LANES
    u_blk0 = u_col0 // LANES
    kern = functools.partial(_s5_kernel, seq=seq, rows=rows)
    return pl.pallas_call(
        kern,
        grid=(batch, n_chunks),
        in_specs=[
            pl.BlockSpec((seq, LANES), lambda b, c: (b, u_blk0 + c)),
            pl.BlockSpec((None, LANES, 2 * SSM_CHUNK_STATE), lambda b, c: (c, 0, 0)),
            pl.BlockSpec((None, 1, SSM_CHUNK_STATE), lambda b, c: (c, 0, 0)),
            pl.BlockSpec((None, 1, SSM_CHUNK_STATE), lambda b, c: (c, 0, 0)),
            pl.BlockSpec((None, 2 * SSM_CHUNK_STATE, LANES), lambda b, c: (c, 0, 0)),
            pl.BlockSpec((None, 1, LANES), lambda b, c: (c, 0, 0)),
        ],
        out_specs=pl.BlockSpec((seq, LANES), lambda b, c: (b, c)),
        out_shape=jax.ShapeDtypeStruct((batch * seq, d_ssm), BF16),
        scratch_shapes=[pltpu.VMEM((seq, 2 * SSM_CHUNK_STATE), F32),
                        pltpu.VMEM((seq, LANES), F32), pltpu.VMEM((seq, LANES), F32)],
        compiler_params=pltpu.CompilerParams(
            dimension_semantics=("parallel", "parallel"), vmem_limit_bytes=VMEM_LIMIT),
        name="s5_scan",
    )(proj, bd, a_re, a_im, cd, d_skip)


def _s5_params(lam_re, lam_im, log_dt, b_re, b_im, c_re, c_im, d_skip):
    g = lam_re.shape[0]
    nc = g // SSM_CHUNK_GROUPS
    lr = jnp.minimum(lam_re.astype(F32), -1e-4)
    li = lam_im.astype(F32)
    dt = jnp.exp(log_dt.astype(F32))[:, None]
    mag = jnp.exp(lr * dt)
    lb_re, lb_im = mag * jnp.cos(li * dt), mag * jnp.sin(li * dt)
    den = lr * lr + li * li
    coef_re = ((lb_re - 1.0) * lr + lb_im * li) / den
    coef_im = (lb_im * lr - (lb_re - 1.0) * li) / den
    br, bi = b_re.astype(F32), b_im.astype(F32)
    bb_re = coef_re[..., None] * br - coef_im[..., None] * bi
    bb_im = coef_re[..., None] * bi + coef_im[..., None] * br
    eye = jnp.eye(SSM_CHUNK_GROUPS, dtype=F32)

    def pack_in(bb):
        bb = bb.reshape(nc, SSM_CHUNK_GROUPS, SSM_STATE, SSM_GROUP)
        return jnp.einsum('cgph,gk->cghkp', bb, eye).reshape(nc, LANES, SSM_CHUNK_STATE)

    def pack_out(cc):
        cc = cc.astype(F32).reshape(nc, SSM_CHUNK_GROUPS, SSM_GROUP, SSM_STATE)
        return jnp.einsum('cghp,gk->ckpgh', cc, eye).reshape(nc, SSM_CHUNK_STATE, LANES)

    bd = jnp.concatenate([pack_in(bb_re), pack_in(bb_im)], axis=-1).astype(BF16)
    cd = jnp.concatenate([pack_out(c_re), -pack_out(c_im)], axis=1).astype(BF16)
    a_re = lb_re.reshape(nc, 1, SSM_CHUNK_STATE)
    a_im = lb_im.reshape(nc, 1, SSM_CHUNK_STATE)
    dd = d_skip.astype(F32).reshape(nc, 1, LANES)
    return bd, a_re, a_im, cd, dd


def _mix_out_kernel(a_ref, y_ref, x_ref, gw_ref, gb_ref, nw_ref, wo_ref, o_ref):
    d_attn = a_ref.shape[1]
    y = y_ref[...]
    gate = _dot(y, gw_ref[...]) + gb_ref[...]
    s = y.astype(F32) * jax.nn.sigmoid(gate)
    sn = (_rms(s) * nw_ref[...]).astype(BF16)
    acc = _dot(a_ref[...], wo_ref[0:d_attn, :]) + _dot(sn, wo_ref[d_attn:, :])
    o_ref[...] = x_ref[...] + acc


def _mix_out(a, y, x, glu_w, glu_b, norm_w, w_out, *, tm, attn_tiles):
    t, d = x.shape
    d_attn, d_ssm = a.shape[1], y.shape[1]
    const = lambda i: (0, 0)
    return pl.pallas_call(
        _mix_out_kernel,
        grid=(t // tm,),
        in_specs=[
            pl.BlockSpec((tm, d_attn), lambda i: (_attn_tile_pos(i, attn_tiles), 0)),
            pl.BlockSpec((tm, d_ssm), lambda i: (i, 0)),
            pl.BlockSpec((tm, d), lambda i: (i, 0)),
            _resident((d_ssm, d_ssm), const),
            _resident((1, d_ssm), const),
            _resident((1, d_ssm), const),
            _resident((d, d), const),
        ],
        out_specs=pl.BlockSpec((tm, d), lambda i: (i, 0)),
        out_shape=jax.ShapeDtypeStruct((t, d), F32),
        compiler_params=pltpu.CompilerParams(
            dimension_semantics=("parallel",), vmem_limit_bytes=VMEM_LIMIT),
        name="mix_out",
    )(a, y, x, glu_w, glu_b.reshape(1, d_ssm).astype(F32), norm_w.reshape(1, d_ssm).astype(F32), w_out)


def _xattn_route_kernel(q_ref, k_ref, v_ref, h_ref, xo_ref, nw_ref, rhi_ref, rlo_ref, rb_ref,
                        h2_ref, hn_ref, eid_ref, wts_ref):
    d = h_ref.shape[1]
    hd = d // X_HEADS
    heads = [slice(h * hd, (h + 1) * hd) for h in range(X_HEADS)]
    h2 = h_ref[...]
    s_next = _dot_nt(q_ref[:, heads[0]], k_ref[:, heads[0]])
    for h, sl in enumerate(heads):
        s = s_next
        if h + 1 < X_HEADS:
            s_next = _dot_nt(q_ref[:, heads[h + 1]], k_ref[:, heads[h + 1]])
        p = jnp.exp(s - jnp.max(s, axis=-1, keepdims=True))
        p = p * (1.0 / jnp.sum(p, axis=-1, keepdims=True))
        o = _dot(p.astype(BF16), v_ref[:, sl]).astype(BF16)
        h2 = h2 + _dot(o, xo_ref[sl, :])
    h2_ref[...] = h2
    hn = _rms(h2) * nw_ref[...]
    hn_ref[...] = _pack_halves(hn)

    hi = hn.astype(BF16)
    lo = (hn - hi.astype(F32)).astype(BF16)
    hi_both = _dot(hi, jnp.concatenate([rhi_ref[...], rlo_ref[...]], axis=1))
    logits = (hi_both[:, :ROUTE_LANES] + hi_both[:, ROUTE_LANES:] + _dot(lo, rhi_ref[...])) + rb_ref[...]

    lt = logits.T
    idx = lax.broadcasted_iota(jnp.int32, (SUBLANES, lt.shape[1]), 0)

    def first_row(cond):
        return jnp.min(jnp.where(cond, idx, SUBLANES), axis=0, keepdims=True)

    def softmax_rows(x):
        e = jnp.exp(x - jnp.max(x, axis=0, keepdims=True))
        return e / jnp.sum(e, axis=0, keepdims=True)

    p_c = softmax_rows(lt[0:MOE_GROUPS, :])
    p_grp = jnp.max(p_c, axis=0, keepdims=True)
    grp = first_row(p_c == p_grp)
    lf = lt[MOE_GROUPS:MOE_GROUPS + EXP_PER_GROUP, :]
    for g in range(1, MOE_GROUPS):
        lo_row = MOE_GROUPS + g * EXP_PER_GROUP
        lf = jnp.where(grp == g, lt[lo_row:lo_row + EXP_PER_GROUP, :], lf)
    pf = softmax_rows(lf)
    v1 = jnp.max(pf, axis=0, keepdims=True)
    i1 = first_row(pf == v1)
    rest = idx != i1
    v2 = jnp.max(jnp.where(rest, pf, -1.0), axis=0, keepdims=True)
    i2 = first_row(rest & (pf == v2))
    tot = v1 + v2
    e1 = grp * EXP_PER_GROUP + i1
    e2 = grp * EXP_PER_GROUP + i2
    eid_ref[...] = jnp.where(idx == 0, e1, jnp.where(idx == 1, e2, 0))
    wts_ref[...] = jnp.where(idx == 0, v1 / tot * p_grp, jnp.where(idx == 1, v2 / tot * p_grp, 0.0))


def _xattn_route(q, kv, h1, xo_w, norm_w, r_hi, r_lo, r_b, *, batch, seq, mem_len, tm):
    t, d = h1.shape
    n = seq // tm
    const = lambda b, i: (0, 0)
    row = lambda b, i: (b * n + i, 0)
    return pl.pallas_call(
        _xattn_route_kernel,
        grid=(batch, n),
        in_specs=[
            pl.BlockSpec((tm, d), row),
            pl.BlockSpec((mem_len, d), lambda b, i: (b, 0)),
            pl.BlockSpec((mem_len, d), lambda b, i: (b, 1)),
            pl.BlockSpec((tm, d), row),
            _resident((d, d), const),
            _resident((1, d), const),
            _resident((d, ROUTE_LANES), const),
            _resident((d, ROUTE_LANES), const),
            _resident((1, ROUTE_LANES), const),
        ],
        out_specs=[
            pl.BlockSpec((tm, d), row),
            pl.BlockSpec((tm, d // 2), row),
            pl.BlockSpec((SUBLANES, tm), row),
            pl.BlockSpec((SUBLANES, tm), row),
        ],
        out_shape=[
            jax.ShapeDtypeStruct((t, d), F32),
            jax.ShapeDtypeStruct((t, d // 2), jnp.uint32),
            jax.ShapeDtypeStruct((t // tm * SUBLANES, tm), jnp.int32),
            jax.ShapeDtypeStruct((t // tm * SUBLANES, tm), F32),
        ],
        compiler_params=pltpu.CompilerParams(
            dimension_semantics=("parallel", "parallel"), vmem_limit_bytes=VMEM_LIMIT),
        name="xattn_route",
    )(q, kv, kv, h1, xo_w, norm_w.reshape(1, d).astype(F32), r_hi, r_lo, r_b)


def _moe_kernel(be_ref, par_ref, first_ref, nxt_ref, base_ref, nval_ref, tok_ref, nu_ref,
                hn_hbm, wg_hbm, wu_hbm, wd_hbm, o_ref,
                xbuf, wgb, wub, wdb, gsem, wsem):
    b = pl.program_id(0)
    n_used = nu_ref[0]

    def weight_copies(e, slot):
        copies = []
        for hbm, buf in ((wg_hbm, wgb), (wu_hbm, wub), (wd_hbm, wdb)):
            rows = hbm.shape[1] // WEIGHT_DMA_CHUNKS
            for c in range(WEIGHT_DMA_CHUNKS):
                sl = pl.ds(c * rows, rows)
                copies.append(pltpu.make_async_copy(hbm.at[e, sl], buf.at[slot, sl], wsem.at[slot]))
        return copies

    def groups(blk):
        return (nval_ref[blk] + ROW_GROUP - 1) // ROW_GROUP

    def start_gather(blk, slot):
        base = base_ref[blk]

        def body(g, carry):
            for r in range(ROW_GROUP):
                tok = tok_ref[base + g * ROW_GROUP + r]
                pltpu.make_async_copy(hn_hbm.at[pl.ds(tok, 1)], xbuf.at[slot, g, pl.ds(r, 1)],
                                      gsem.at[slot]).start()
            return carry
        lax.fori_loop(0, groups(blk), body, 0)

    def wait_gather(blk, slot):
        filled = xbuf.at[slot, pl.ds(0, groups(blk))]
        pltpu.make_async_copy(filled, filled, gsem.at[slot]).wait()

    @pl.when(b == 0)
    def _():
        xbuf[...] = jnp.zeros(xbuf.shape, xbuf.dtype)
        for c in weight_copies(be_ref[0], par_ref[0]):
            c.start(priority=WEIGHT_DMA_PRIORITY)
        start_gather(0, 0)

    @pl.when(b < n_used)
    def _():
        slot = b % 2
        wslot = par_ref[b]
        is_first = first_ref[b] == 1

        @pl.when(is_first & (nxt_ref[b] >= 0))
        def _():
            for c in weight_copies(nxt_ref[b], 1 - wslot):
                c.start(priority=WEIGHT_DMA_PRIORITY)

        @pl.when(b + 1 < n_used)
        def _():
            start_gather(b + 1, 1 - slot)

        @pl.when(is_first)
        def _():
            for c in weight_copies(0, wslot):
                c.wait()

        wait_gather(b, slot)
        half = xbuf.shape[-1]

        def experts(rows):
            groups_ = rows // ROW_GROUP
            x_hi, x_lo = (v.astype(BF16)
                          for v in _unpack_halves(xbuf[slot, 0:groups_].reshape(rows, half)))

            def up(w):
                return (_dot(x_hi, w[wslot, 0:half, :].astype(BF16))
                        + _dot(x_lo, w[wslot, half:, :].astype(BF16)))

            mid = (jax.nn.silu(up(wgb)) * up(wub)).astype(BF16)
            o_ref[0:rows, :] = _pack_halves(_dot(mid, wdb[wslot].astype(BF16)))
            if rows < MOE_BLOCK:
                o_ref[rows:, :] = jnp.zeros((MOE_BLOCK - rows, half), o_ref.dtype)

        small = nval_ref[b] <= MOE_BLOCK // 2
        pl.when(small)(functools.partial(experts, MOE_BLOCK // 2))
        pl.when(jnp.logical_not(small))(functools.partial(experts, MOE_BLOCK))

    @pl.when(b >= n_used)
    def _():
        o_ref[...] = jnp.zeros(o_ref.shape, o_ref.dtype)


def _moe_experts(hn_packed, w_gate, w_up, w_down, meta, tok_sorted, n_used):
    d, d_ff = w_gate.shape[1:]
    half = hn_packed.shape[1]
    blk_exp, par, first, nxt, base, nval = meta
    n_blocks = blk_exp.shape[0]
    any_spec = pl.BlockSpec(memory_space=pl.ANY)
    return pl.pallas_call(
        _moe_kernel,
        grid_spec=pltpu.PrefetchScalarGridSpec(
            num_scalar_prefetch=8,
            grid=(n_blocks,),
            in_specs=[any_spec, any_spec, any_spec, any_spec],
            out_specs=pl.BlockSpec((MOE_BLOCK, half), lambda b, *_: (b, 0)),
            scratch_shapes=[
                pltpu.VMEM((2, MOE_BLOCK // ROW_GROUP, ROW_GROUP, half), jnp.uint32),
                pltpu.VMEM((2, d, d_ff), F32),
                pltpu.VMEM((2, d, d_ff), F32),
                pltpu.VMEM((2, d_ff, d), F32),
                pltpu.SemaphoreType.DMA((2,)),
                pltpu.SemaphoreType.DMA((2,)),
            ],
        ),
        out_shape=jax.ShapeDtypeStruct((n_blocks * MOE_BLOCK, half), jnp.uint32),
        compiler_params=pltpu.CompilerParams(
            dimension_semantics=("arbitrary",), vmem_limit_bytes=VMEM_LIMIT),
        name="moe_experts",
    )(blk_exp, par, first, nxt, base, nval, tok_sorted, n_used, hn_packed, w_gate, w_up, w_down)


def _combine_kernel(pos_ref, y_hbm, h_ref, w_ref, o_ref, ybuf, sem, *, tm):
    i = pl.program_id(0)
    n_groups = tm // ROW_GROUP

    def start_gather(tile, slot):
        base = tile * (tm * TOP_K_FINE)

        def body(g, carry):
            for r in range(ROW_GROUP):
                for k in range(TOP_K_FINE):
                    row = pos_ref[base + (g * ROW_GROUP + r) * TOP_K_FINE + k]
                    pltpu.make_async_copy(y_hbm.at[pl.ds(row, 1)], ybuf.at[slot, k, g, pl.ds(r, 1)],
                                          sem.at[slot]).start(priority=k)
            return carry
        lax.fori_loop(0, n_groups, body, 0)

    @pl.when(i == 0)
    def _():
        start_gather(0, 0)

    @pl.when(i + 1 < pl.num_programs(0))
    def _():
        start_gather(i + 1, (i + 1) % 2)

    slot = i % 2
    pltpu.make_async_copy(ybuf.at[slot], ybuf.at[slot], sem.at[slot]).wait()
    w = w_ref[...]
    half = ybuf.shape[-1]
    y0 = _unpack_halves(ybuf[slot, 0].reshape(tm, half))
    y1 = _unpack_halves(ybuf[slot, 1].reshape(tm, half))
    for c in range(2):
        cols = slice(c * half, (c + 1) * half)
        o_ref[:, cols] = h_ref[:, cols] + (w[:, 0:1] * y0[c] + w[:, 1:2] * y1[c])


def _combine(y, h2, wts, pos, *, tm):
    t, d = h2.shape
    kern = functools.partial(_combine_kernel, tm=tm)
    return pl.pallas_call(
        kern,
        grid_spec=pltpu.PrefetchScalarGridSpec(
            num_scalar_prefetch=1,
            grid=(t // tm,),
            in_specs=[
                pl.BlockSpec(memory_space=pl.ANY),
                pl.BlockSpec((tm, d), lambda i, pos: (i, 0)),
                pl.BlockSpec((tm, ROUTE_LANES), lambda i, pos: (i, 0)),
            ],
            out_specs=pl.BlockSpec((tm, d), lambda i, pos: (i, 0)),
            scratch_shapes=[
                pltpu.VMEM((2, TOP_K_FINE, tm // ROW_GROUP, ROW_GROUP, y.shape[1]), jnp.uint32),
                pltpu.SemaphoreType.DMA((2,)),
            ],
        ),
        out_shape=jax.ShapeDtypeStruct((t, d), F32),
        compiler_params=pltpu.CompilerParams(
            dimension_semantics=("arbitrary",), vmem_limit_bytes=VMEM_LIMIT),
        name="moe_combine",
    )(pos, y, h2, wts)


def _lookup(table, idx):
    sel = idx[:, None] == jnp.arange(table.shape[0], dtype=jnp.int32)[None, :]
    return jnp.sum(jnp.where(sel, table[None, :], 0), axis=1).astype(jnp.int32)


def _dispatch(eid, n_tokens):
    n_assign = n_tokens * TOP_K_FINE
    experts = jnp.arange(N_EXPERTS, dtype=jnp.int32)
    e_flat = eid.reshape(n_assign)
    a_ids = jnp.arange(n_assign, dtype=jnp.int32)
    e_s, order = lax.sort_key_val(e_flat, a_ids)
    counts = jnp.sum((e_flat[:, None] == experts[None, :]).astype(jnp.int32), axis=0)
    starts = jnp.cumsum(counts) - counts
    nb = (counts + MOE_BLOCK - 1) // MOE_BLOCK
    blk_end = jnp.cumsum(nb)
    blk_start = blk_end - nb
    n_used = blk_end[-1]
    n_blocks = (n_assign + N_EXPERTS * (MOE_BLOCK - 1)) // MOE_BLOCK
    b_ids = jnp.arange(n_blocks, dtype=jnp.int32)
    used = b_ids < n_used
    blk_exp = jnp.minimum(jnp.sum((blk_end[None, :] <= b_ids[:, None]).astype(jnp.int32), axis=1),
                          N_EXPERTS - 1)
    j = b_ids - _lookup(blk_start, blk_exp)
    base = jnp.where(used, _lookup(starts, blk_exp) + j * MOE_BLOCK, 0)
    nval = jnp.where(used, jnp.clip(_lookup(counts, blk_exp) - j * MOE_BLOCK, 0, MOE_BLOCK), 0)
    first = (used & (j == 0)).astype(jnp.int32)
    active = counts > 0
    par = _lookup(jnp.cumsum(active.astype(jnp.int32)) - 1, blk_exp) & 1
    later = lax.cummin(jnp.where(active, experts, N_EXPERTS), reverse=True)
    nxt_e = jnp.concatenate([later[1:], jnp.full((1,), N_EXPERTS, jnp.int32)])
    nxt = _lookup(jnp.where(nxt_e == N_EXPERTS, -1, nxt_e), blk_exp)
    meta = tuple(v.astype(jnp.int32) for v in (blk_exp, par, first, nxt, base, nval))
    row_sorted = a_ids + _lookup(blk_start * MOE_BLOCK - starts, e_s)
    _, pos = lax.sort_key_val(order, row_sorted)
    tok_sorted = jnp.concatenate([lax.shift_right_logical(order, 1), jnp.zeros((ROW_GROUP,), jnp.int32)])
    return meta, tok_sorted, pos, n_used.astype(jnp.int32).reshape(1)


def kernel(x, mem, norm1_w, w_in, q_norm_w, k_norm_w, lambda_q1, lambda_k1, lambda_q2, lambda_k2, subln_w, ssm_lambda_re, ssm_lambda_im, ssm_log_dt, ssm_b_re, ssm_b_im, ssm_c_re, ssm_c_im, ssm_d, ssm_glu_w, ssm_glu_b, ssm_out_norm_w, w_out, norm2_w, mem_norm_w, xq_w, xkv_w, xq_norm_w, xk_norm_w, xo_w, norm3_w, router_coarse_w, router_coarse_b, router_fine_w, router_fine_b, expert_w_gate, expert_w_up, expert_w_down):
    batch, seq, d = x.shape
    mem_len = mem.shape[1]
    t = batch * seq
    depth = norm1_w.shape[0]
    d_attn = DA_HEADS * DA_V_DIM
    d_ssm = d - d_attn
    qk_cols = DA_HEADS * 2 * DA_QK_DIM
    x_hd = d // X_HEADS
    h = x.reshape(t, d)
    mem2 = mem.reshape(batch * mem_len, d)

    for l in range(depth):
        lam_init = 0.8 - 0.6 * math.exp(-0.3 * l)
        lam = (jnp.exp(jnp.sum(lambda_q1[l].astype(F32) * lambda_k1[l].astype(F32)))
               - jnp.exp(jnp.sum(lambda_q2[l].astype(F32) * lambda_k2[l].astype(F32)))
               + lam_init).reshape(1)

        n_rep = qk_cols // DA_QK_DIM
        in_gain = jnp.concatenate([
            jnp.tile(q_norm_w[l].astype(F32) * (DA_QK_DIM ** -0.5 * LOG2E), n_rep),
            jnp.tile(k_norm_w[l].astype(F32), n_rep),
            jnp.ones((d_attn + d_ssm,), F32)])
        proj = _norm_matmul(h, norm1_w[l], w_in[l].astype(BF16), in_gain,
                            n_norm_cols=2 * qk_cols, chunk=DA_QK_DIM, tm=PROJ_ROW_TILE, tn=PROJ_COL_TILE, name="in_proj")
        sub_gain = (subln_w[l].astype(F32) * (1.0 - lam_init)).reshape(1, DA_V_DIM)
        a = _diff_attn(proj, lam, sub_gain, batch=batch, seq=seq, tq=ATTN_TILE)

        bd, a_re, a_im, cd, dd = _s5_params(ssm_lambda_re[l], ssm_lambda_im[l], ssm_log_dt[l],
                                            ssm_b_re[l], ssm_b_im[l], ssm_c_re[l], ssm_c_im[l], ssm_d[l])
        y = _s5(proj, 2 * qk_cols + d_attn, d_ssm, bd, a_re, a_im, cd, dd, batch=batch, seq=seq)
        h = _mix_out(a, y, h, ssm_glu_w[l].astype(BF16), ssm_glu_b[l], ssm_out_norm_w[l],
                     w_out[l].astype(BF16), tm=ATTN_TILE, attn_tiles=seq // ATTN_TILE)

        kv_gain = jnp.concatenate([jnp.tile(xk_norm_w[l].astype(F32), X_HEADS), jnp.ones((d,), F32)])
        kv = _norm_matmul(mem2, mem_norm_w[l], xkv_w[l], kv_gain,
                          n_norm_cols=d, chunk=x_hd, tm=batch * mem_len, tn=KV_COL_TILE, name="kv_proj")
        q_gain = jnp.tile(xq_norm_w[l].astype(F32) * (x_hd ** -0.5), X_HEADS)
        q = _norm_matmul(h, norm2_w[l], xq_w[l].astype(BF16), q_gain,
                         n_norm_cols=d, chunk=x_hd, tm=PROJ_ROW_TILE, tn=PROJ_COL_TILE, name="xq_proj")
        r_w = jnp.concatenate([router_coarse_w[l].astype(F32), router_fine_w[l].astype(F32)], axis=1)
        r_w = jnp.pad(r_w, ((0, 0), (0, ROUTE_LANES - r_w.shape[1])))
        r_hi = r_w.astype(BF16)
        r_lo = (r_w - r_hi.astype(F32)).astype(BF16)
        r_b = jnp.concatenate([router_coarse_b[l].astype(F32), router_fine_b[l].astype(F32)])
        r_b = jnp.pad(r_b, (0, ROUTE_LANES - r_b.shape[0])).reshape(1, ROUTE_LANES)
        h2, hn3, eid_t, wts_t = _xattn_route(q, kv, h, xo_w[l].astype(BF16), norm3_w[l], r_hi, r_lo, r_b,
                                             batch=batch, seq=seq, mem_len=mem_len, tm=XATTN_ROW_TILE)

        def per_token(x_t):
            return x_t.reshape(-1, SUBLANES, x_t.shape[1])[:, :TOP_K_FINE, :].transpose(0, 2, 1).reshape(t, TOP_K_FINE)

        eid = per_token(eid_t)
        wts = jnp.pad(per_token(wts_t), ((0, 0), (0, ROUTE_LANES - TOP_K_FINE)))

        meta, tok_sorted, pos, n_used = _dispatch(eid, t)
        y = _moe_experts(hn3, expert_w_gate[l], expert_w_up[l], expert_w_down[l], meta, tok_sorted, n_used)
        h = _combine(y, h2, wts, pos, tm=COMBINE_ROW_TILE)

    return h.reshape(batch, seq, d)
```

```python
import functools
import math

import jax
import jax.numpy as jnp
from jax import lax
from jax.experimental import pallas as pl
from jax.experimental.pallas import tpu as pltpu

F32 = jnp.float32
BF16 = jnp.bfloat16

EPS = 1e-6
DA_HEADS = 4
DA_QK_DIM = 128
DA_V_DIM = 256
SSM_GROUP = 16
SSM_STATE = 64
X_HEADS = 4
MOE_GROUPS = 8
EXP_PER_GROUP = 8
N_EXPERTS = MOE_GROUPS * EXP_PER_GROUP
TOP_K_FINE = 2

LANES = 128
SUBLANES = 8
MXU_TILE = 256
VMEM_LIMIT = 56 * 1024 * 1024
NEG = -1e30
LOG2E = math.log2(math.e)

SSM_CHUNK_GROUPS = LANES // SSM_GROUP
SSM_CHUNK_STATE = SSM_CHUNK_GROUPS * SSM_STATE
SCAN_SEGS = SUBLANES
ATTN_TILE = 512
PROJ_ROW_TILE = 1024
PROJ_COL_TILE = 2048
KV_COL_TILE = 512
XATTN_ROW_TILE = 512
COMBINE_ROW_TILE = 256
S5_ROW_BLOCK = 256
MOE_BLOCK = 256
ROW_GROUP = SUBLANES
WEIGHT_DMA_CHUNKS = 4
WEIGHT_DMA_PRIORITY = 1
WEIGHT_SLOTS = 3
ROUTE_LANES = LANES


def _rms(x, eps=EPS):
    return x * lax.rsqrt(jnp.mean(x * x, axis=-1, keepdims=True) + eps)


def _dot(a, b):
    return jnp.dot(a, b, preferred_element_type=F32)


def _dot_nt(a, b):
    return lax.dot_general(a, b, (((1,), (1,)), ((), ())), preferred_element_type=F32)


def _pack_halves(x):
    n = x.shape[1] // 2
    hi = lax.bitcast_convert_type(x[:, :n].astype(BF16).astype(F32), jnp.uint32)
    lo = lax.bitcast_convert_type(x[:, n:].astype(BF16).astype(F32), jnp.uint32)
    return hi | lax.shift_right_logical(lo, jnp.uint32(16))


def _unpack_halves(p):
    hi = lax.bitcast_convert_type(p & jnp.uint32(0xFFFF0000), F32)
    lo = lax.bitcast_convert_type(lax.shift_left(p, jnp.uint32(16)), F32)
    return hi, lo


def _resident(shape, index_map):
    return pl.BlockSpec(shape, index_map, pipeline_mode=pl.Buffered(1))


def _norm_matmul_kernel(x_ref, nw_ref, w_ref, g_ref, o_ref, *, n_norm_tiles, chunk):
    j = pl.program_id(1)

    xn = (_rms(x_ref[...].astype(F32)) * nw_ref[...]).astype(BF16)

    normed = j < n_norm_tiles
    tn = w_ref.shape[1]
    sub = max(chunk, MXU_TILE)
    for s in range(tn // sub):
        w = w_ref[:, s * sub:(s + 1) * sub]
        if w.dtype != BF16:
            w = w.astype(BF16)
        acc = _dot(xn, w)
        for c in range(sub // chunk):
            lo = s * sub + c * chunk
            a = acc[:, c * chunk:(c + 1) * chunk]
            inv = lax.rsqrt(jnp.mean(a * a, axis=-1, keepdims=True) + EPS)
            scale = jnp.where(normed, inv, 1.0)
            o_ref[:, lo:lo + chunk] = (a * scale * g_ref[:, lo:lo + chunk]).astype(o_ref.dtype)


def _norm_matmul(x, norm_w, w, gain, *, n_norm_cols, chunk, tm, tn, name):
    m, k = x.shape
    n = w.shape[1]
    assert m % tm == 0 and n % tn == 0 and tn % max(chunk, MXU_TILE) == 0 and n_norm_cols % tn == 0
    kern = functools.partial(_norm_matmul_kernel, n_norm_tiles=n_norm_cols // tn, chunk=chunk)
    return pl.pallas_call(
        kern,
        grid=(m // tm, n // tn),
        in_specs=[
            pl.BlockSpec((tm, k), lambda i, j: (i, 0)),
            pl.BlockSpec((1, k), lambda i, j: (0, 0)),
            pl.BlockSpec((k, tn), lambda i, j: (0, j)),
            pl.BlockSpec((1, tn), lambda i, j: (0, j)),
        ],
        out_specs=pl.BlockSpec((tm, tn), lambda i, j: (i, j)),
        out_shape=jax.ShapeDtypeStruct((m, n), BF16),
        compiler_params=pltpu.CompilerParams(
            dimension_semantics=("parallel", "parallel"), vmem_limit_bytes=VMEM_LIMIT),
        name=name,
    )(x, norm_w.reshape(1, k).astype(F32), w, gain.reshape(1, n).astype(F32))


def _diff_attn_kernel(lam_ref, qa_ref, qb_ref, k_ref, v_ref, g_ref, o_ref, *stat_refs, tq, n_q):
    pair = pl.program_id(2)
    stats_a = (stat_refs[0:3], stat_refs[3:6])
    stats_b = (stat_refs[6:9], stat_refs[9:12])

    def scores(q_ref, j):
        return tuple(_dot_nt(q_ref[:, c * DA_QK_DIM:(c + 1) * DA_QK_DIM],
                             k_ref[j * tq:(j + 1) * tq, c * DA_QK_DIM:(c + 1) * DA_QK_DIM])
                     for c in range(2))

    def accumulate(stats, j, s_pair, masked):
        n_t = tq // LANES
        for s, (m_ref, l_ref, acc_ref) in zip(s_pair, stats):
            if masked:
                row = lax.broadcasted_iota(jnp.int32, s.shape, 0)
                col = lax.broadcasted_iota(jnp.int32, s.shape, 1)
                s = jnp.where(col <= row, s, NEG)
            tiles = [s[:, c * LANES:(c + 1) * LANES] for c in range(n_t)]
            fold = tiles[0]
            for t_ in tiles[1:]:
                fold = jnp.maximum(fold, t_)
            m_old = m_ref[...]
            m_new = jnp.maximum(m_old, jnp.max(fold, axis=-1, keepdims=True))
            p_tiles = [jnp.exp2(t_ - m_new) for t_ in tiles]
            psum = p_tiles[0]
            for t_ in p_tiles[1:]:
                psum = psum + t_
            alpha = jnp.exp2(m_old - m_new)
            l_ref[...] = alpha * l_ref[...] + psum
            p = jnp.concatenate([t_.astype(BF16) for t_ in p_tiles], axis=1)
            pv = _dot(p, v_ref[j * tq:(j + 1) * tq, :])
            for c in range(DA_V_DIM // LANES):
                cols = slice(c * LANES, (c + 1) * LANES)
                acc_ref[:, cols] = alpha * acc_ref[:, cols] + pv[:, cols]
            m_ref[...] = m_new

    def finish(stats, rows):
        (_, l1, acc1), (_, l2, acc2) = stats
        l1, l2 = (jnp.sum(l[...], axis=-1, keepdims=True) for l in (l1, l2))
        o = acc1[...] / l1 - lam_ref[0] * (acc2[...] / l2)
        o_ref[rows, :] = (_rms(o) * g_ref[...]).astype(o_ref.dtype)

    def run(p):
        tiles = ((qa_ref, stats_a, p), (qb_ref, stats_b, n_q - 1 - p))
        for _, stats, _ in tiles:
            for m_ref, l_ref, acc_ref in stats:
                m_ref[...] = jnp.full(m_ref.shape, NEG, F32)
                l_ref[...] = jnp.zeros(l_ref.shape, F32)
                acc_ref[...] = jnp.zeros(acc_ref.shape, F32)
        pending = [scores(q_ref, 0) for q_ref, _, _ in tiles]
        for j in range(n_q - p):
            for idx, (q_ref, stats, diag) in enumerate(tiles):
                if j > diag:
                    continue
                s_pair = pending[idx]
                if j < diag:
                    pending[idx] = scores(q_ref, j + 1)
                accumulate(stats, j, s_pair, masked=(j == diag))
        finish(stats_a, slice(0, tq))
        finish(stats_b, slice(tq, 2 * tq))

    for p in range(n_q // 2):
        pl.when(pair == p)(functools.partial(run, p))


def _attn_tile_pos(tile, nq):
    b, qt = tile // nq, tile % nq
    return b * nq + jnp.where(qt < nq // 2, 2 * qt, 2 * (nq - 1 - qt) + 1)


def _diff_attn(proj, lam, gain, *, batch, seq, tq):
    t = batch * seq
    nq = seq // tq
    assert nq % 2 == 0
    width = 2 * DA_QK_DIM
    k_blk0 = DA_HEADS
    v_blk0 = 2 * DA_HEADS
    kern = functools.partial(_diff_attn_kernel, tq=tq, n_q=nq)
    stat = [pltpu.VMEM((tq, LANES), F32), pltpu.VMEM((tq, LANES), F32), pltpu.VMEM((tq, DA_V_DIM), F32)]
    return pl.pallas_call(
        kern,
        grid_spec=pltpu.PrefetchScalarGridSpec(
            num_scalar_prefetch=1,
            grid=(batch, DA_HEADS, nq // 2),
            in_specs=[
                pl.BlockSpec((tq, width), lambda b, h, p, lam: (b * nq + p, h)),
                pl.BlockSpec((tq, width), lambda b, h, p, lam: (b * nq + nq - 1 - p, h)),
                pl.BlockSpec((seq, width), lambda b, h, p, lam: (b, k_blk0 + h)),
                pl.BlockSpec((seq, width), lambda b, h, p, lam: (b, v_blk0 + h)),
                pl.BlockSpec((1, DA_V_DIM), lambda b, h, p, lam: (0, 0)),
            ],
            out_specs=pl.BlockSpec((2 * tq, DA_V_DIM), lambda b, h, p, lam: (b * (nq // 2) + p, h)),
            scratch_shapes=stat * 4,
        ),
        out_shape=jax.ShapeDtypeStruct((t, DA_HEADS * DA_V_DIM), BF16),
        compiler_params=pltpu.CompilerParams(
            dimension_semantics=("parallel", "parallel", "arbitrary"), vmem_limit_bytes=VMEM_LIMIT),
        name="diff_attn",
    )(lam, proj, proj, proj, proj, gain)


def _s5_kernel(u_ref, bd_ref, ar_ref, ai_ref, cd_ref, d_ref, o_ref, xs_ref, us_ref, ys_ref, *, seq, rows):
    ns = SSM_CHUNK_STATE
    seg_len = seq // SCAN_SEGS
    n_row_blk = seq // rows
    steps = rows // SCAN_SEGS

    for seg in range(SCAN_SEGS):
        us_ref[pl.ds(seg, seg_len, stride=SCAN_SEGS), :] = (
            u_ref[seg * seg_len:(seg + 1) * seg_len, :].astype(F32))

    def in_map(r):
        rs = slice(r * rows, (r + 1) * rows)
        xs_ref[rs, :] = _dot(us_ref[rs, :].astype(BF16), bd_ref[...])

    def out_map(r):
        rs = slice(r * rows, (r + 1) * rows)
        y = _dot(xs_ref[rs, :].astype(BF16), cd_ref[...]) + d_ref[...] * us_ref[rs, :]
        ys_ref[rs, :] = jax.nn.gelu(y)
        for seg in range(SCAN_SEGS):
            t0 = seg * seg_len + r * steps
            o_ref[t0:t0 + steps, :] = (
                ys_ref[pl.ds(r * rows + seg, steps, stride=SCAN_SEGS), :].astype(o_ref.dtype))

    ar = jnp.broadcast_to(ar_ref[...], (SCAN_SEGS, ns))
    ai = jnp.broadcast_to(ai_ref[...], (SCAN_SEGS, ns))

    def advance(t, sr, si):
        ts = slice(t * SCAN_SEGS, (t + 1) * SCAN_SEGS)
        return ar * sr - ai * si + xs_ref[ts, 0:ns], ar * si + ai * sr + xs_ref[ts, ns:2 * ns]

    in_map(0)
    fr = fi = jnp.zeros((SCAN_SEGS, ns), F32)
    for r in range(n_row_blk):
        if r + 1 < n_row_blk:
            in_map(r + 1)
        for t in range(r * steps, (r + 1) * steps):
            fr, fi = advance(t, fr, fi)

    pr, pi = ar, ai
    for _ in range(int(math.log2(seg_len))):
        pr, pi = pr * pr - pi * pi, 2.0 * pr * pi
    seg = lax.broadcasted_iota(jnp.int32, (SCAN_SEGS, ns), 0)

    def shifted(x, k):
        return jnp.where(seg >= k, pltpu.roll(x, k, 0), 0.0)

    k = 1
    while k < SCAN_SEGS:
        gr, gi = shifted(fr, k), shifted(fi, k)
        fr, fi = fr + pr * gr - pi * gi, fi + pr * gi + pi * gr
        pr, pi = pr * pr - pi * pi, 2.0 * pr * pi
        k *= 2
    sr, si = shifted(fr, 1), shifted(fi, 1)

    for r in range(n_row_blk):
        for t in range(r * steps, (r + 1) * steps):
            sr, si = advance(t, sr, si)
            ts = slice(t * SCAN_SEGS, (t + 1) * SCAN_SEGS)
            xs_ref[ts, 0:ns] = sr
            xs_ref[ts, ns:2 * ns] = si
        if r >= 1:
            out_map(r - 1)
    out_map(n_row_blk - 1)


def _s5(proj, u_col0, d_ssm, bd, a_re, a_im, cd, d_skip, *, batch, seq, rows=S5_ROW_BLOCK):
    n_chunks = d_ssm // LANES
    u_blk0 = u_col0 // LANES
    kern = functools.partial(_s5_kernel, seq=seq, rows=rows)
    return pl.pallas_call(
        kern,
        grid=(batch, n_chunks),
        in_specs=[
            pl.BlockSpec((seq, LANES), lambda b, c: (b, u_blk0 + c)),
            pl.BlockSpec((None, LANES, 2 * SSM_CHUNK_STATE), lambda b, c: (c, 0, 0)),
            pl.BlockSpec((None, 1, SSM_CHUNK_STATE), lambda b, c: (c, 0, 0)),
            pl.BlockSpec((None, 1, SSM_CHUNK_STATE), lambda b, c: (c, 0, 0)),
            pl.BlockSpec((None, 2 * SSM_CHUNK_STATE, LANES), lambda b, c: (c, 0, 0)),
            pl.BlockSpec((None, 1, LANES), lambda b, c: (c, 0, 0)),
        ],
        out_specs=pl.BlockSpec((seq, LANES), lambda b, c: (b, c)),
        out_shape=jax.ShapeDtypeStruct((batch * seq, d_ssm), BF16),
        scratch_shapes=[pltpu.VMEM((seq, 2 * SSM_CHUNK_STATE), F32),
                        pltpu.VMEM((seq, LANES), F32), pltpu.VMEM((seq, LANES), F32)],
        compiler_params=pltpu.CompilerParams(
            dimension_semantics=("parallel", "parallel"), vmem_limit_bytes=VMEM_LIMIT),
        name="s5_scan",
    )(proj, bd, a_re, a_im, cd, d_skip)


def _s5_params(lam_re, lam_im, log_dt, b_re, b_im, c_re, c_im, d_skip):
    g = lam_re.shape[0]
    nc = g // SSM_CHUNK_GROUPS
    lr = jnp.minimum(lam_re.astype(F32), -1e-4)
    li = lam_im.astype(F32)
    dt = jnp.exp(log_dt.astype(F32))[:, None]
    mag = jnp.exp(lr * dt)
    lb_re, lb_im = mag * jnp.cos(li * dt), mag * jnp.sin(li * dt)
    den = lr * lr + li * li
    coef_re = ((lb_re - 1.0) * lr + lb_im * li) / den
    coef_im = (lb_im * lr - (lb_re - 1.0) * li) / den
    br, bi = b_re.astype(F32), b_im.astype(F32)
    bb_re = coef_re[..., None] * br - coef_im[..., None] * bi
    bb_im = coef_re[..., None] * bi + coef_im[..., None] * br
    eye = jnp.eye(SSM_CHUNK_GROUPS, dtype=F32)

    def pack_in(bb):
        bb = bb.reshape(nc, SSM_CHUNK_GROUPS, SSM_STATE, SSM_GROUP)
        return jnp.einsum('cgph,gk->cghkp', bb, eye).reshape(nc, LANES, SSM_CHUNK_STATE)

    def pack_out(cc):
        cc = cc.astype(F32).reshape(nc, SSM_CHUNK_GROUPS, SSM_GROUP, SSM_STATE)
        return jnp.einsum('cghp,gk->ckpgh', cc, eye).reshape(nc, SSM_CHUNK_STATE, LANES)

    bd = jnp.concatenate([pack_in(bb_re), pack_in(bb_im)], axis=-1).astype(BF16)
    cd = jnp.concatenate([pack_out(c_re), -pack_out(c_im)], axis=1).astype(BF16)
    a_re = lb_re.reshape(nc, 1, SSM_CHUNK_STATE)
    a_im = lb_im.reshape(nc, 1, SSM_CHUNK_STATE)
    dd = d_skip.astype(F32).reshape(nc, 1, LANES)
    return bd, a_re, a_im, cd, dd


def _mix_out_kernel(a_ref, y_ref, x_ref, gw_ref, gb_ref, nw_ref, wo_ref, o_ref):
    d_attn = a_ref.shape[1]
    y = y_ref[...]
    gate = _dot(y, gw_ref[...]) + gb_ref[...]
    s = y.astype(F32) * jax.nn.sigmoid(gate)
    sn = (_rms(s) * nw_ref[...]).astype(BF16)
    acc = _dot(a_ref[...], wo_ref[0:d_attn, :]) + _dot(sn, wo_ref[d_attn:, :])
    o_ref[...] = x_ref[...] + acc


def _mix_out(a, y, x, glu_w, glu_b, norm_w, w_out, *, tm, attn_tiles):
    t, d = x.shape
    d_attn, d_ssm = a.shape[1], y.shape[1]
    const = lambda i: (0, 0)
    return pl.pallas_call(
        _mix_out_kernel,
        grid=(t // tm,),
        in_specs=[
            pl.BlockSpec((tm, d_attn), lambda i: (_attn_tile_pos(i, attn_tiles), 0)),
            pl.BlockSpec((tm, d_ssm), lambda i: (i, 0)),
            pl.BlockSpec((tm, d), lambda i: (i, 0)),
            _resident((d_ssm, d_ssm), const),
            _resident((1, d_ssm), const),
            _resident((1, d_ssm), const),
            _resident((d, d), const),
        ],
        out_specs=pl.BlockSpec((tm, d), lambda i: (i, 0)),
        out_shape=jax.ShapeDtypeStruct((t, d), F32),
        compiler_params=pltpu.CompilerParams(
            dimension_semantics=("parallel",), vmem_limit_bytes=VMEM_LIMIT),
        name="mix_out",
    )(a, y, x, glu_w, glu_b.reshape(1, d_ssm).astype(F32), norm_w.reshape(1, d_ssm).astype(F32), w_out)


def _xattn_route_kernel(q_ref, k_ref, v_ref, h_ref, xo_ref, nw_ref, rhi_ref, rlo_ref, rb_ref,
                        h2_ref, hn_ref, eid_ref, wts_ref):
    d = h_ref.shape[1]
    hd = d // X_HEADS
    heads = [slice(h * hd, (h + 1) * hd) for h in range(X_HEADS)]
    h2 = h_ref[...]
    s_next = _dot_nt(q_ref[:, heads[0]], k_ref[:, heads[0]])
    for h, sl in enumerate(heads):
        s = s_next
        if h + 1 < X_HEADS:
            s_next = _dot_nt(q_ref[:, heads[h + 1]], k_ref[:, heads[h + 1]])
        p = jnp.exp(s - jnp.max(s, axis=-1, keepdims=True))
        p = p * (1.0 / jnp.sum(p, axis=-1, keepdims=True))
        o = _dot(p.astype(BF16), v_ref[:, sl]).astype(BF16)
        h2 = h2 + _dot(o, xo_ref[sl, :])
    h2_ref[...] = h2
    hn = _rms(h2) * nw_ref[...]
    hn_ref[...] = _pack_halves(hn)

    hi = hn.astype(BF16)
    lo = (hn - hi.astype(F32)).astype(BF16)
    hi_both = _dot(hi, jnp.concatenate([rhi_ref[...], rlo_ref[...]], axis=1))
    logits = (hi_both[:, :ROUTE_LANES] + hi_both[:, ROUTE_LANES:] + _dot(lo, rhi_ref[...])) + rb_ref[...]

    lt = logits.T
    idx = lax.broadcasted_iota(jnp.int32, (SUBLANES, lt.shape[1]), 0)

    def first_row(cond):
        return jnp.min(jnp.where(cond, idx, SUBLANES), axis=0, keepdims=True)

    def softmax_rows(x):
        e = jnp.exp(x - jnp.max(x, axis=0, keepdims=True))
        return e / jnp.sum(e, axis=0, keepdims=True)

    p_c = softmax_rows(lt[0:MOE_GROUPS, :])
    p_grp = jnp.max(p_c, axis=0, keepdims=True)
    grp = first_row(p_c == p_grp)
    lf = lt[MOE_GROUPS:MOE_GROUPS + EXP_PER_GROUP, :]
    for g in range(1, MOE_GROUPS):
        lo_row = MOE_GROUPS + g * EXP_PER_GROUP
        lf = jnp.where(grp == g, lt[lo_row:lo_row + EXP_PER_GROUP, :], lf)
    pf = softmax_rows(lf)
    v1 = jnp.max(pf, axis=0, keepdims=True)
    i1 = first_row(pf == v1)
    rest = idx != i1
    v2 = jnp.max(jnp.where(rest, pf, -1.0), axis=0, keepdims=True)
    i2 = first_row(rest & (pf == v2))
    tot = v1 + v2
    e1 = grp * EXP_PER_GROUP + i1
    e2 = grp * EXP_PER_GROUP + i2
    eid_ref[...] = jnp.where(idx == 0, e1, jnp.where(idx == 1, e2, 0))
    wts_ref[...] = jnp.where(idx == 0, v1 / tot * p_grp, jnp.where(idx == 1, v2 / tot * p_grp, 0.0))


def _xattn_route(q, kv, h1, xo_w, norm_w, r_hi, r_lo, r_b, *, batch, seq, mem_len, tm):
    t, d = h1.shape
    n = seq // tm
    const = lambda b, i: (0, 0)
    row = lambda b, i: (b * n + i, 0)
    return pl.pallas_call(
        _xattn_route_kernel,
        grid=(batch, n),
        in_specs=[
            pl.BlockSpec((tm, d), row),
            pl.BlockSpec((mem_len, d), lambda b, i: (b, 0)),
            pl.BlockSpec((mem_len, d), lambda b, i: (b, 1)),
            pl.BlockSpec((tm, d), row),
            _resident((d, d), const),
            _resident((1, d), const),
            _resident((d, ROUTE_LANES), const),
            _resident((d, ROUTE_LANES), const),
            _resident((1, ROUTE_LANES), const),
        ],
        out_specs=[
            pl.BlockSpec((tm, d), row),
            pl.BlockSpec((tm, d // 2), row),
            pl.BlockSpec((SUBLANES, tm), row),
            pl.BlockSpec((SUBLANES, tm), row),
        ],
        out_shape=[
            jax.ShapeDtypeStruct((t, d), F32),
            jax.ShapeDtypeStruct((t, d // 2), jnp.uint32),
            jax.ShapeDtypeStruct((t // tm * SUBLANES, tm), jnp.int32),
            jax.ShapeDtypeStruct((t // tm * SUBLANES, tm), F32),
        ],
        compiler_params=pltpu.CompilerParams(
            dimension_semantics=("parallel", "parallel"), vmem_limit_bytes=VMEM_LIMIT),
        name="xattn_route",
    )(q, kv, kv, h1, xo_w, norm_w.reshape(1, d).astype(F32), r_hi, r_lo, r_b)


def _moe_kernel(be_ref, par_ref, first_ref, nxt_ref, base_ref, nval_ref, tok_ref, nu_ref,
                hn_hbm, wg_hbm, wu_hbm, wd_hbm, o_ref,
                xbuf, wgb, wub, wdb, gsem, wsem):
    b = pl.program_id(0)
    n_used = nu_ref[0]

    def weight_copies(e, slot):
        copies = []
        for hbm, buf in ((wg_hbm, wgb), (wu_hbm, wub), (wd_hbm, wdb)):
            rows = hbm.shape[1] // WEIGHT_DMA_CHUNKS
            for c in range(WEIGHT_DMA_CHUNKS):
                sl = pl.ds(c * rows, rows)
                copies.append(pltpu.make_async_copy(hbm.at[e, sl], buf.at[slot, sl], wsem.at[slot]))
        return copies

    def groups(blk):
        return (nval_ref[blk] + ROW_GROUP - 1) // ROW_GROUP

    def start_gather(blk, slot):
        base = base_ref[blk]

        def body(g, carry):
            for r in range(ROW_GROUP):
                tok = tok_ref[base + g * ROW_GROUP + r]
                pltpu.make_async_copy(hn_hbm.at[pl.ds(tok, 1)], xbuf.at[slot, g, pl.ds(r, 1)],
                                      gsem.at[slot]).start()
            return carry
        lax.fori_loop(0, groups(blk), body, 0)

    def wait_gather(blk, slot):
        filled = xbuf.at[slot, pl.ds(0, groups(blk))]
        pltpu.make_async_copy(filled, filled, gsem.at[slot]).wait()

    @pl.when(b == 0)
    def _():
        xbuf[...] = jnp.zeros(xbuf.shape, xbuf.dtype)
        for c in weight_copies(be_ref[0], par_ref[0]):
            c.start(priority=WEIGHT_DMA_PRIORITY)

        @pl.when(nu_ref[1] >= 0)
        def _():
            for c in weight_copies(nu_ref[1], 1):
                c.start(priority=WEIGHT_DMA_PRIORITY)

        start_gather(0, 0)

    @pl.when(b < n_used)
    def _():
        slot = b % 2
        wslot = par_ref[b]
        is_first = first_ref[b] == 1

        @pl.when(is_first & (nxt_ref[b] >= 0))
        def _():
            for c in weight_copies(nxt_ref[b], (wslot + WEIGHT_SLOTS - 1) % WEIGHT_SLOTS):
                c.start(priority=WEIGHT_DMA_PRIORITY)

        @pl.when(b + 1 < n_used)
        def _():
            start_gather(b + 1, 1 - slot)

        @pl.when(is_first)
        def _():
            for c in weight_copies(0, wslot):
                c.wait()

        wait_gather(b, slot)
        half = xbuf.shape[-1]

        def experts(rows):
            groups_ = rows // ROW_GROUP
            x_hi, x_lo = (v.astype(BF16)
                          for v in _unpack_halves(xbuf[slot, 0:groups_].reshape(rows, half)))

            def up(w):
                return (_dot(x_hi, w[wslot, 0:half, :].astype(BF16))
                        + _dot(x_lo, w[wslot, half:, :].astype(BF16)))

            mid = (jax.nn.silu(up(wgb)) * up(wub)).astype(BF16)
            o_ref[0:rows, :] = _pack_halves(_dot(mid, wdb[wslot].astype(BF16)))
            if rows < MOE_BLOCK:
                o_ref[rows:, :] = jnp.zeros((MOE_BLOCK - rows, half), o_ref.dtype)

        small = nval_ref[b] <= MOE_BLOCK // 2
        pl.when(small)(functools.partial(experts, MOE_BLOCK // 2))
        pl.when(jnp.logical_not(small))(functools.partial(experts, MOE_BLOCK))

    @pl.when(b >= n_used)
    def _():
        o_ref[...] = jnp.zeros(o_ref.shape, o_ref.dtype)


def _moe_experts(hn_packed, w_gate, w_up, w_down, meta, tok_sorted, n_used):
    d, d_ff = w_gate.shape[1:]
    half = hn_packed.shape[1]
    blk_exp, par, first, nxt, base, nval = meta
    n_blocks = blk_exp.shape[0]
    any_spec = pl.BlockSpec(memory_space=pl.ANY)
    return pl.pallas_call(
        _moe_kernel,
        grid_spec=pltpu.PrefetchScalarGridSpec(
            num_scalar_prefetch=8,
            grid=(n_blocks,),
            in_specs=[any_spec, any_spec, any_spec, any_spec],
            out_specs=pl.BlockSpec((MOE_BLOCK, half), lambda b, *_: (b, 0)),
            scratch_shapes=[
                pltpu.VMEM((2, MOE_BLOCK // ROW_GROUP, ROW_GROUP, half), jnp.uint32),
                pltpu.VMEM((WEIGHT_SLOTS, d, d_ff), F32),
                pltpu.VMEM((WEIGHT_SLOTS, d, d_ff), F32),
                pltpu.VMEM((WEIGHT_SLOTS, d_ff, d), F32),
                pltpu.SemaphoreType.DMA((2,)),
                pltpu.SemaphoreType.DMA((WEIGHT_SLOTS,)),
            ],
        ),
        out_shape=jax.ShapeDtypeStruct((n_blocks * MOE_BLOCK, half), jnp.uint32),
        compiler_params=pltpu.CompilerParams(
            dimension_semantics=("arbitrary",), vmem_limit_bytes=VMEM_LIMIT),
        name="moe_experts",
    )(blk_exp, par, first, nxt, base, nval, tok_sorted, n_used, hn_packed, w_gate, w_up, w_down)


def _combine_kernel(pos_ref, y_hbm, h_ref, w_ref, o_ref, ybuf, sem, *, tm):
    i = pl.program_id(0)
    n_groups = tm // ROW_GROUP

    def start_gather(tile, slot):
        base = tile * (tm * TOP_K_FINE)

        def body(g, carry):
            for r in range(ROW_GROUP):
                for k in range(TOP_K_FINE):
                    row = pos_ref[base + (g * ROW_GROUP + r) * TOP_K_FINE + k]
                    pltpu.make_async_copy(y_hbm.at[pl.ds(row, 1)], ybuf.at[slot, k, g, pl.ds(r, 1)],
                                          sem.at[slot]).start(priority=k)
            return carry
        lax.fori_loop(0, n_groups, body, 0)

    @pl.when(i == 0)
    def _():
        start_gather(0, 0)

    @pl.when(i + 1 < pl.num_programs(0))
    def _():
        start_gather(i + 1, (i + 1) % 2)

    slot = i % 2
    pltpu.make_async_copy(ybuf.at[slot], ybuf.at[slot], sem.at[slot]).wait()
    w = w_ref[...]
    half = ybuf.shape[-1]
    y0 = _unpack_halves(ybuf[slot, 0].reshape(tm, half))
    y1 = _unpack_halves(ybuf[slot, 1].reshape(tm, half))
    for c in range(2):
        cols = slice(c * half, (c + 1) * half)
        o_ref[:, cols] = h_ref[:, cols] + (w[:, 0:1] * y0[c] + w[:, 1:2] * y1[c])


def _combine(y, h2, wts, pos, *, tm):
    t, d = h2.shape
    kern = functools.partial(_combine_kernel, tm=tm)
    return pl.pallas_call(
        kern,
        grid_spec=pltpu.PrefetchScalarGridSpec(
            num_scalar_prefetch=1,
            grid=(t // tm,),
            in_specs=[
                pl.BlockSpec(memory_space=pl.ANY),
                pl.BlockSpec((tm, d), lambda i, pos: (i, 0)),
                pl.BlockSpec((tm, ROUTE_LANES), lambda i, pos: (i, 0)),
            ],
            out_specs=pl.BlockSpec((tm, d), lambda i, pos: (i, 0)),
            scratch_shapes=[
                pltpu.VMEM((2, TOP_K_FINE, tm // ROW_GROUP, ROW_GROUP, y.shape[1]), jnp.uint32),
                pltpu.SemaphoreType.DMA((2,)),
            ],
        ),
        out_shape=jax.ShapeDtypeStruct((t, d), F32),
        compiler_params=pltpu.CompilerParams(
            dimension_semantics=("arbitrary",), vmem_limit_bytes=VMEM_LIMIT),
        name="moe_combine",
    )(pos, y, h2, wts)


def _lookup(table, idx):
    sel = idx[:, None] == jnp.arange(table.shape[0], dtype=jnp.int32)[None, :]
    return jnp.sum(jnp.where(sel, table[None, :], 0), axis=1).astype(jnp.int32)


def _dispatch(eid, n_tokens):
    n_assign = n_tokens * TOP_K_FINE
    experts = jnp.arange(N_EXPERTS, dtype=jnp.int32)
    e_flat = eid.reshape(n_assign)
    a_ids = jnp.arange(n_assign, dtype=jnp.int32)
    e_s, order = lax.sort_key_val(e_flat, a_ids)
    counts = jnp.sum((e_flat[:, None] == experts[None, :]).astype(jnp.int32), axis=0)
    starts = jnp.cumsum(counts) - counts
    nb = (counts + MOE_BLOCK - 1) // MOE_BLOCK
    blk_end = jnp.cumsum(nb)
    blk_start = blk_end - nb
    n_used = blk_end[-1]
    n_blocks = (n_assign + N_EXPERTS * (MOE_BLOCK - 1)) // MOE_BLOCK
    b_ids = jnp.arange(n_blocks, dtype=jnp.int32)
    used = b_ids < n_used
    blk_exp = jnp.minimum(jnp.sum((blk_end[None, :] <= b_ids[:, None]).astype(jnp.int32), axis=1),
                          N_EXPERTS - 1)
    j = b_ids - _lookup(blk_start, blk_exp)
    base = jnp.where(used, _lookup(starts, blk_exp) + j * MOE_BLOCK, 0)
    nval = jnp.where(used, jnp.clip(_lookup(counts, blk_exp) - j * MOE_BLOCK, 0, MOE_BLOCK), 0)
    first = (used & (j == 0)).astype(jnp.int32)
    active = counts > 0
    arank = jnp.cumsum(active.astype(jnp.int32)) - 1
    n_act = arank[-1] + 1
    by_rank = jnp.sum(jnp.where(active[None, :] & (arank[None, :] == experts[:, None]), experts[None, :], 0), axis=1)
    rank = _lookup(arank, blk_exp)
    par = rank % WEIGHT_SLOTS
    ahead = rank + (WEIGHT_SLOTS - 1)
    nxt = jnp.where(ahead < n_act, _lookup(by_rank, jnp.minimum(ahead, N_EXPERTS - 1)), -1)
    second = jnp.where(n_act > 1, by_rank[1], -1)
    meta = tuple(v.astype(jnp.int32) for v in (blk_exp, par, first, nxt, base, nval))
    row_sorted = a_ids + _lookup(blk_start * MOE_BLOCK - starts, e_s)
    _, pos = lax.sort_key_val(order, row_sorted)
    tok_sorted = jnp.concatenate([lax.shift_right_logical(order, 1), jnp.zeros((ROW_GROUP,), jnp.int32)])
    return meta, tok_sorted, pos, jnp.stack([n_used, second]).astype(jnp.int32)


def kernel(x, mem, norm1_w, w_in, q_norm_w, k_norm_w, lambda_q1, lambda_k1, lambda_q2, lambda_k2, subln_w, ssm_lambda_re, ssm_lambda_im, ssm_log_dt, ssm_b_re, ssm_b_im, ssm_c_re, ssm_c_im, ssm_d, ssm_glu_w, ssm_glu_b, ssm_out_norm_w, w_out, norm2_w, mem_norm_w, xq_w, xkv_w, xq_norm_w, xk_norm_w, xo_w, norm3_w, router_coarse_w, router_coarse_b, router_fine_w, router_fine_b, expert_w_gate, expert_w_up, expert_w_down):
    batch, seq, d = x.shape
    mem_len = mem.shape[1]
    t = batch * seq
    depth = norm1_w.shape[0]
    d_attn = DA_HEADS * DA_V_DIM
    d_ssm = d - d_attn
    qk_cols = DA_HEADS * 2 * DA_QK_DIM
    x_hd = d // X_HEADS
    h = x.reshape(t, d)
    mem2 = mem.reshape(batch * mem_len, d)

    for l in range(depth):
        lam_init = 0.8 - 0.6 * math.exp(-0.3 * l)
        lam = (jnp.exp(jnp.sum(lambda_q1[l].astype(F32) * lambda_k1[l].astype(F32)))
               - jnp.exp(jnp.sum(lambda_q2[l].astype(F32) * lambda_k2[l].astype(F32)))
               + lam_init).reshape(1)

        n_rep = qk_cols // DA_QK_DIM
        in_gain = jnp.concatenate([
            jnp.tile(q_norm_w[l].astype(F32) * (DA_QK_DIM ** -0.5 * LOG2E), n_rep),
            jnp.tile(k_norm_w[l].astype(F32), n_rep),
            jnp.ones((d_attn + d_ssm,), F32)])
        proj = _norm_matmul(h, norm1_w[l], w_in[l].astype(BF16), in_gain,
                            n_norm_cols=2 * qk_cols, chunk=DA_QK_DIM, tm=PROJ_ROW_TILE, tn=PROJ_COL_TILE, name="in_proj")
        sub_gain = (subln_w[l].astype(F32) * (1.0 - lam_init)).reshape(1, DA_V_DIM)
        a = _diff_attn(proj, lam, sub_gain, batch=batch, seq=seq, tq=ATTN_TILE)

        bd, a_re, a_im, cd, dd = _s5_params(ssm_lambda_re[l], ssm_lambda_im[l], ssm_log_dt[l],
                                            ssm_b_re[l], ssm_b_im[l], ssm_c_re[l], ssm_c_im[l], ssm_d[l])
        y = _s5(proj, 2 * qk_cols + d_attn, d_ssm, bd, a_re, a_im, cd, dd, batch=batch, seq=seq)
        h = _mix_out(a, y, h, ssm_glu_w[l].astype(BF16), ssm_glu_b[l], ssm_out_norm_w[l],
                     w_out[l].astype(BF16), tm=ATTN_TILE, attn_tiles=seq // ATTN_TILE)

        kv_gain = jnp.concatenate([jnp.tile(xk_norm_w[l].astype(F32), X_HEADS), jnp.ones((d,), F32)])
        kv = _norm_matmul(mem2, mem_norm_w[l], xkv_w[l], kv_gain,
                          n_norm_cols=d, chunk=x_hd, tm=batch * mem_len, tn=KV_COL_TILE, name="kv_proj")
        q_gain = jnp.tile(xq_norm_w[l].astype(F32) * (x_hd ** -0.5), X_HEADS)
        q = _norm_matmul(h, norm2_w[l], xq_w[l].astype(BF16), q_gain,
                         n_norm_cols=d, chunk=x_hd, tm=PROJ_ROW_TILE, tn=PROJ_COL_TILE, name="xq_proj")
        r_w = jnp.concatenate([router_coarse_w[l].astype(F32), router_fine_w[l].astype(F32)], axis=1)
        r_w = jnp.pad(r_w, ((0, 0), (0, ROUTE_LANES - r_w.shape[1])))
        r_hi = r_w.astype(BF16)
        r_lo = (r_w - r_hi.astype(F32)).astype(BF16)
        r_b = jnp.concatenate([router_coarse_b[l].astype(F32), router_fine_b[l].astype(F32)])
        r_b = jnp.pad(r_b, (0, ROUTE_LANES - r_b.shape[0])).reshape(1, ROUTE_LANES)
        h2, hn3, eid_t, wts_t = _xattn_route(q, kv, h, xo_w[l].astype(BF16), norm3_w[l], r_hi, r_lo, r_b,
                                             batch=batch, seq=seq, mem_len=mem_len, tm=XATTN_ROW_TILE)

        def per_token(x_t):
            return x_t.reshape(-1, SUBLANES, x_t.shape[1])[:, :TOP_K_FINE, :].transpose(0, 2, 1).reshape(t, TOP_K_FINE)

        eid = per_token(eid_t)
        wts = jnp.pad(per_token(wts_t), ((0, 0), (0, ROUTE_LANES - TOP_K_FINE)))

        meta, tok_sorted, pos, n_used = _dispatch(eid, t)
        y = _moe_experts(hn3, expert_w_gate[l], expert_w_up[l], expert_w_down[l], meta, tok_sorted, n_used)
        h = _combine(y, h2, wts, pos, tm=COMBINE_ROW_TILE)

    return h.reshape(batch, seq, d)
```

```python
import functools
import math

import jax
import jax.numpy as jnp
from jax import lax
from jax.experimental import pallas as pl
from jax.experimental.pallas import tpu as pltpu

F32 = jnp.float32
BF16 = jnp.bfloat16

EPS = 1e-6
DA_HEADS = 4
DA_QK_DIM = 128
DA_V_DIM = 256
SSM_GROUP = 16
SSM_STATE = 64
X_HEADS = 4
MOE_GROUPS = 8
EXP_PER_GROUP = 8
N_EXPERTS = MOE_GROUPS * EXP_PER_GROUP
TOP_K_FINE = 2

LANES = 128
SUBLANES = 8
MXU_TILE = 256
VMEM_LIMIT = 56 * 1024 * 1024
NEG = -1e30
LOG2E = math.log2(math.e)

SSM_CHUNK_GROUPS = LANES // SSM_GROUP
SSM_CHUNK_STATE = SSM_CHUNK_GROUPS * SSM_STATE
SCAN_SEGS = SUBLANES
ATTN_TILE = 512
PROJ_ROW_TILE = 1024
PROJ_COL_TILE = 2048
KV_COL_TILE = 512
XATTN_ROW_TILE = 512
COMBINE_ROW_TILE = 256
S5_ROW_BLOCK = 512
MOE_BLOCK = 256
ROW_GROUP = SUBLANES
WEIGHT_DMA_CHUNKS = 8
WEIGHT_DMA_PRIORITY = 1
WEIGHT_SLOTS = 3
ROUTE_LANES = LANES


def _rms(x, eps=EPS):
    return x * lax.rsqrt(jnp.mean(x * x, axis=-1, keepdims=True) + eps)


def _dot(a, b):
    return jnp.dot(a, b, preferred_element_type=F32)


def _dot_nt(a, b):
    return lax.dot_general(a, b, (((1,), (1,)), ((), ())), preferred_element_type=F32)


def _pack_halves(x):
    n = x.shape[1] // 2
    hi = lax.bitcast_convert_type(x[:, :n].astype(BF16).astype(F32), jnp.uint32)
    lo = lax.bitcast_convert_type(x[:, n:].astype(BF16).astype(F32), jnp.uint32)
    return hi | lax.shift_right_logical(lo, jnp.uint32(16))


def _unpack_halves(p):
    hi = lax.bitcast_convert_type(p & jnp.uint32(0xFFFF0000), F32)
    lo = lax.bitcast_convert_type(lax.shift_left(p, jnp.uint32(16)), F32)
    return hi, lo


def _resident(shape, index_map):
    return pl.BlockSpec(shape, index_map, pipeline_mode=pl.Buffered(1))


def _norm_matmul_kernel(x_ref, nw_ref, w_ref, g_ref, o_ref, *, n_norm_tiles, chunk):
    j = pl.program_id(1)

    xn = (_rms(x_ref[...].astype(F32)) * nw_ref[...]).astype(BF16)

    normed = j < n_norm_tiles
    tn = w_ref.shape[1]
    sub = max(chunk, MXU_TILE)
    for s in range(tn // sub):
        w = w_ref[:, s * sub:(s + 1) * sub]
        if w.dtype != BF16:
            w = w.astype(BF16)
        acc = _dot(xn, w)
        for c in range(sub // chunk):
            lo = s * sub + c * chunk
            a = acc[:, c * chunk:(c + 1) * chunk]
            inv = lax.rsqrt(jnp.mean(a * a, axis=-1, keepdims=True) + EPS)
            scale = jnp.where(normed, inv, 1.0)
            o_ref[:, lo:lo + chunk] = (a * scale * g_ref[:, lo:lo + chunk]).astype(o_ref.dtype)


def _norm_matmul(x, norm_w, w, gain, *, n_norm_cols, chunk, tm, tn, name):
    m, k = x.shape
    n = w.shape[1]
    assert m % tm == 0 and n % tn == 0 and tn % max(chunk, MXU_TILE) == 0 and n_norm_cols % tn == 0
    kern = functools.partial(_norm_matmul_kernel, n_norm_tiles=n_norm_cols // tn, chunk=chunk)
    return pl.pallas_call(
        kern,
        grid=(m // tm, n // tn),
        in_specs=[
            pl.BlockSpec((tm, k), lambda i, j: (i, 0)),
            pl.BlockSpec((1, k), lambda i, j: (0, 0)),
            pl.BlockSpec((k, tn), lambda i, j: (0, j)),
            pl.BlockSpec((1, tn), lambda i, j: (0, j)),
        ],
        out_specs=pl.BlockSpec((tm, tn), lambda i, j: (i, j)),
        out_shape=jax.ShapeDtypeStruct((m, n), BF16),
        compiler_params=pltpu.CompilerParams(
            dimension_semantics=("parallel", "parallel"), vmem_limit_bytes=VMEM_LIMIT),
        name=name,
    )(x, norm_w.reshape(1, k).astype(F32), w, gain.reshape(1, n).astype(F32))


def _diff_attn_kernel(lam_ref, qa_ref, qb_ref, k_ref, v_ref, g_ref, o_ref, *stat_refs, tq, n_q):
    pair = pl.program_id(2)
    stats_a = (stat_refs[0:3], stat_refs[3:6])
    stats_b = (stat_refs[6:9], stat_refs[9:12])

    def scores(q_ref, j):
        return tuple(_dot_nt(q_ref[:, c * DA_QK_DIM:(c + 1) * DA_QK_DIM],
                             k_ref[j * tq:(j + 1) * tq, c * DA_QK_DIM:(c + 1) * DA_QK_DIM])
                     for c in range(2))

    def accumulate(stats, j, s_pair, masked):
        n_t = tq // LANES
        for s, (m_ref, l_ref, acc_ref) in zip(s_pair, stats):
            if masked:
                row = lax.broadcasted_iota(jnp.int32, s.shape, 0)
                col = lax.broadcasted_iota(jnp.int32, s.shape, 1)
                s = jnp.where(col <= row, s, NEG)
            tiles = [s[:, c * LANES:(c + 1) * LANES] for c in range(n_t)]
            fold = tiles[0]
            for t_ in tiles[1:]:
                fold = jnp.maximum(fold, t_)
            m_old = m_ref[...]
            m_new = jnp.maximum(m_old, jnp.max(fold, axis=-1, keepdims=True))
            p_tiles = [jnp.exp2(t_ - m_new) for t_ in tiles]
            psum = p_tiles[0]
            for t_ in p_tiles[1:]:
                psum = psum + t_
            alpha = jnp.exp2(m_old - m_new)
            l_ref[...] = alpha * l_ref[...] + psum
            p = jnp.concatenate([t_.astype(BF16) for t_ in p_tiles], axis=1)
            pv = _dot(p, v_ref[j * tq:(j + 1) * tq, :])
            for c in range(DA_V_DIM // LANES):
                cols = slice(c * LANES, (c + 1) * LANES)
                acc_ref[:, cols] = alpha * acc_ref[:, cols] + pv[:, cols]
            m_ref[...] = m_new

    def finish(stats, rows):
        (_, l1, acc1), (_, l2, acc2) = stats
        l1, l2 = (jnp.sum(l[...], axis=-1, keepdims=True) for l in (l1, l2))
        o = acc1[...] / l1 - lam_ref[0] * (acc2[...] / l2)
        o_ref[rows, :] = (_rms(o) * g_ref[...]).astype(o_ref.dtype)

    def run(p):
        tiles = ((qa_ref, stats_a, p), (qb_ref, stats_b, n_q - 1 - p))
        for _, stats, _ in tiles:
            for m_ref, l_ref, acc_ref in stats:
                m_ref[...] = jnp.full(m_ref.shape, NEG, F32)
                l_ref[...] = jnp.zeros(l_ref.shape, F32)
                acc_ref[...] = jnp.zeros(acc_ref.shape, F32)
        pending = [scores(q_ref, 0) for q_ref, _, _ in tiles]
        for j in range(n_q - p):
            for idx, (q_ref, stats, diag) in enumerate(tiles):
                if j > diag:
                    continue
                s_pair = pending[idx]
                if j < diag:
                    pending[idx] = scores(q_ref, j + 1)
                accumulate(stats, j, s_pair, masked=(j == diag))
        finish(stats_a, slice(0, tq))
        finish(stats_b, slice(tq, 2 * tq))

    for p in range(n_q // 2):
        pl.when(pair == p)(functools.partial(run, p))


def _attn_tile_pos(tile, nq):
    b, qt = tile // nq, tile % nq
    return b * nq + jnp.where(qt < nq // 2, 2 * qt, 2 * (nq - 1 - qt) + 1)


def _diff_attn(proj, lam, gain, *, batch, seq, tq):
    t = batch * seq
    nq = seq // tq
    assert nq % 2 == 0
    width = 2 * DA_QK_DIM
    k_blk0 = DA_HEADS
    v_blk0 = 2 * DA_HEADS
    kern = functools.partial(_diff_attn_kernel, tq=tq, n_q=nq)
    stat = [pltpu.VMEM((tq, LANES), F32), pltpu.VMEM((tq, LANES), F32), pltpu.VMEM((tq, DA_V_DIM), F32)]
    return pl.pallas_call(
        kern,
        grid_spec=pltpu.PrefetchScalarGridSpec(
            num_scalar_prefetch=1,
            grid=(batch, DA_HEADS, nq // 2),
            in_specs=[
                pl.BlockSpec((tq, width), lambda b, h, p, lam: (b * nq + p, h)),
                pl.BlockSpec((tq, width), lambda b, h, p, lam: (b * nq + nq - 1 - p, h)),
                pl.BlockSpec((seq, width), lambda b, h, p, lam: (b, k_blk0 + h)),
                pl.BlockSpec((seq, width), lambda b, h, p, lam: (b, v_blk0 + h)),
                pl.BlockSpec((1, DA_V_DIM), lambda b, h, p, lam: (0, 0)),
            ],
            out_specs=pl.BlockSpec((2 * tq, DA_V_DIM), lambda b, h, p, lam: (b * (nq // 2) + p, h)),
            scratch_shapes=stat * 4,
        ),
        out_shape=jax.ShapeDtypeStruct((t, DA_HEADS * DA_V_DIM), BF16),
        compiler_params=pltpu.CompilerParams(
            dimension_semantics=("parallel", "parallel", "arbitrary"), vmem_limit_bytes=VMEM_LIMIT),
        name="diff_attn",
    )(lam, proj, proj, proj, proj, gain)


def _s5_kernel(u_ref, bd_ref, ar_ref, ai_ref, cd_ref, d_ref, o_ref, xs_ref, us_ref, ys_ref, *, seq, rows):
    ns = SSM_CHUNK_STATE
    seg_len = seq // SCAN_SEGS
    n_row_blk = seq // rows
    steps = rows // SCAN_SEGS

    for seg in range(SCAN_SEGS):
        us_ref[pl.ds(seg, seg_len, stride=SCAN_SEGS), :] = (
            u_ref[seg * seg_len:(seg + 1) * seg_len, :].astype(F32))

    def in_map(r):
        rs = slice(r * rows, (r + 1) * rows)
        xs_ref[rs, :] = _dot(us_ref[rs, :].astype(BF16), bd_ref[...])

    def out_map(r):
        rs = slice(r * rows, (r + 1) * rows)
        y = _dot(xs_ref[rs, :].astype(BF16), cd_ref[...]) + d_ref[...] * us_ref[rs, :]
        ys_ref[rs, :] = jax.nn.gelu(y)
        for seg in range(SCAN_SEGS):
            t0 = seg * seg_len + r * steps
            o_ref[t0:t0 + steps, :] = (
                ys_ref[pl.ds(r * rows + seg, steps, stride=SCAN_SEGS), :].astype(o_ref.dtype))

    ar = jnp.broadcast_to(ar_ref[...], (SCAN_SEGS, ns))
    ai = jnp.broadcast_to(ai_ref[...], (SCAN_SEGS, ns))

    def advance(t, sr, si):
        ts = slice(t * SCAN_SEGS, (t + 1) * SCAN_SEGS)
        return ar * sr - ai * si + xs_ref[ts, 0:ns], ar * si + ai * sr + xs_ref[ts, ns:2 * ns]

    in_map(0)
    fr = fi = jnp.zeros((SCAN_SEGS, ns), F32)
    for r in range(n_row_blk):
        if r + 1 < n_row_blk:
            in_map(r + 1)
        for t in range(r * steps, (r + 1) * steps):
            fr, fi = advance(t, fr, fi)

    pr, pi = ar, ai
    for _ in range(int(math.log2(seg_len))):
        pr, pi = pr * pr - pi * pi, 2.0 * pr * pi
    seg = lax.broadcasted_iota(jnp.int32, (SCAN_SEGS, ns), 0)

    def shifted(x, k):
        return jnp.where(seg >= k, pltpu.roll(x, k, 0), 0.0)

    k = 1
    while k < SCAN_SEGS:
        gr, gi = shifted(fr, k), shifted(fi, k)
        fr, fi = fr + pr * gr - pi * gi, fi + pr * gi + pi * gr
        pr, pi = pr * pr - pi * pi, 2.0 * pr * pi
        k *= 2
    sr, si = shifted(fr, 1), shifted(fi, 1)

    for r in range(n_row_blk):
        for t in range(r * steps, (r + 1) * steps):
            sr, si = advance(t, sr, si)
            ts = slice(t * SCAN_SEGS, (t + 1) * SCAN_SEGS)
            xs_ref[ts, 0:ns] = sr
            xs_ref[ts, ns:2 * ns] = si
        if r >= 1:
            out_map(r - 1)
    out_map(n_row_blk - 1)


def _s5(proj, u_col0, d_ssm, bd, a_re, a_im, cd, d_skip, *, batch, seq, rows=S5_ROW_BLOCK):
    n_chunks = d_ssm // LANES
    u_blk0 = u_col0 // LANES
    kern = functools.partial(_s5_kernel, seq=seq, rows=rows)
    return pl.pallas_call(
        kern,
        grid=(batch, n_chunks),
        in_specs=[
            pl.BlockSpec((seq, LANES), lambda b, c: (b, u_blk0 + c)),
            pl.BlockSpec((None, LANES, 2 * SSM_CHUNK_STATE), lambda b, c: (c, 0, 0)),
            pl.BlockSpec((None, 1, SSM_CHUNK_STATE), lambda b, c: (c, 0, 0)),
            pl.BlockSpec((None, 1, SSM_CHUNK_STATE), lambda b, c: (c, 0, 0)),
            pl.BlockSpec((None, 2 * SSM_CHUNK_STATE, LANES), lambda b, c: (c, 0, 0)),
            pl.BlockSpec((None, 1, LANES), lambda b, c: (c, 0, 0)),
        ],
        out_specs=pl.BlockSpec((seq, LANES), lambda b, c: (b, c)),
        out_shape=jax.ShapeDtypeStruct((batch * seq, d_ssm), BF16),
        scratch_shapes=[pltpu.VMEM((seq, 2 * SSM_CHUNK_STATE), F32),
                        pltpu.VMEM((seq, LANES), F32), pltpu.VMEM((seq, LANES), F32)],
        compiler_params=pltpu.CompilerParams(
            dimension_semantics=("parallel", "parallel"), vmem_limit_bytes=VMEM_LIMIT),
        name="s5_scan",
    )(proj, bd, a_re, a_im, cd, d_skip)


def _s5_params(lam_re, lam_im, log_dt, b_re, b_im, c_re, c_im, d_skip):
    g = lam_re.shape[0]
    nc = g // SSM_CHUNK_GROUPS
    lr = jnp.minimum(lam_re.astype(F32), -1e-4)
    li = lam_im.astype(F32)
    dt = jnp.exp(log_dt.astype(F32))[:, None]
    mag = jnp.exp(lr * dt)
    lb_re, lb_im = mag * jnp.cos(li * dt), mag * jnp.sin(li * dt)
    den = lr * lr + li * li
    coef_re = ((lb_re - 1.0) * lr + lb_im * li) / den
    coef_im = (lb_im * lr - (lb_re - 1.0) * li) / den
    br, bi = b_re.astype(F32), b_im.astype(F32)
    bb_re = coef_re[..., None] * br - coef_im[..., None] * bi
    bb_im = coef_re[..., None] * bi + coef_im[..., None] * br
    eye = jnp.eye(SSM_CHUNK_GROUPS, dtype=F32)

    def pack_in(bb):
        bb = bb.reshape(nc, SSM_CHUNK_GROUPS, SSM_STATE, SSM_GROUP)
        return jnp.einsum('cgph,gk->cghkp', bb, eye).reshape(nc, LANES, SSM_CHUNK_STATE)

    def pack_out(cc):
        cc = cc.astype(F32).reshape(nc, SSM_CHUNK_GROUPS, SSM_GROUP, SSM_STATE)
        return jnp.einsum('cghp,gk->ckpgh', cc, eye).reshape(nc, SSM_CHUNK_STATE, LANES)

    bd = jnp.concatenate([pack_in(bb_re), pack_in(bb_im)], axis=-1).astype(BF16)
    cd = jnp.concatenate([pack_out(c_re), -pack_out(c_im)], axis=1).astype(BF16)
    a_re = lb_re.reshape(nc, 1, SSM_CHUNK_STATE)
    a_im = lb_im.reshape(nc, 1, SSM_CHUNK_STATE)
    dd = d_skip.astype(F32).reshape(nc, 1, LANES)
    return bd, a_re, a_im, cd, dd


def _mix_out_kernel(a_ref, y_ref, x_ref, gw_ref, gb_ref, nw_ref, wo_ref, o_ref):
    d_attn = a_ref.shape[1]
    y = y_ref[...]
    gate = _dot(y, gw_ref[...]) + gb_ref[...]
    s = y.astype(F32) * jax.nn.sigmoid(gate)
    sn = (_rms(s) * nw_ref[...]).astype(BF16)
    acc = _dot(a_ref[...], wo_ref[0:d_attn, :]) + _dot(sn, wo_ref[d_attn:, :])
    o_ref[...] = x_ref[...] + acc


def _mix_out(a, y, x, glu_w, glu_b, norm_w, w_out, *, tm, attn_tiles):
    t, d = x.shape
    d_attn, d_ssm = a.shape[1], y.shape[1]
    const = lambda i: (0, 0)
    return pl.pallas_call(
        _mix_out_kernel,
        grid=(t // tm,),
        in_specs=[
            pl.BlockSpec((tm, d_attn), lambda i: (_attn_tile_pos(i, attn_tiles), 0)),
            pl.BlockSpec((tm, d_ssm), lambda i: (i, 0)),
            pl.BlockSpec((tm, d), lambda i: (i, 0)),
            _resident((d_ssm, d_ssm), const),
            _resident((1, d_ssm), const),
            _resident((1, d_ssm), const),
            _resident((d, d), const),
        ],
        out_specs=pl.BlockSpec((tm, d), lambda i: (i, 0)),
        out_shape=jax.ShapeDtypeStruct((t, d), F32),
        compiler_params=pltpu.CompilerParams(
            dimension_semantics=("parallel",), vmem_limit_bytes=VMEM_LIMIT),
        name="mix_out",
    )(a, y, x, glu_w, glu_b.reshape(1, d_ssm).astype(F32), norm_w.reshape(1, d_ssm).astype(F32), w_out)


def _xattn_route_kernel(q_ref, k_ref, v_ref, h_ref, xo_ref, nw_ref, rhi_ref, rlo_ref, rb_ref,
                        h2_ref, hn_ref, eid_ref, wts_ref):
    d = h_ref.shape[1]
    hd = d // X_HEADS
    heads = [slice(h * hd, (h + 1) * hd) for h in range(X_HEADS)]
    h2 = h_ref[...]
    s_next = _dot_nt(q_ref[:, heads[0]], k_ref[:, heads[0]])
    for h, sl in enumerate(heads):
        s = s_next
        if h + 1 < X_HEADS:
            s_next = _dot_nt(q_ref[:, heads[h + 1]], k_ref[:, heads[h + 1]])
        p = jnp.exp(s - jnp.max(s, axis=-1, keepdims=True))
        p = p * (1.0 / jnp.sum(p, axis=-1, keepdims=True))
        o = _dot(p.astype(BF16), v_ref[:, sl]).astype(BF16)
        h2 = h2 + _dot(o, xo_ref[sl, :])
    h2_ref[...] = h2
    hn = _rms(h2) * nw_ref[...]
    hn_ref[...] = _pack_halves(hn)

    hi = hn.astype(BF16)
    lo = (hn - hi.astype(F32)).astype(BF16)
    hi_both = _dot(hi, jnp.concatenate([rhi_ref[...], rlo_ref[...]], axis=1))
    logits = (hi_both[:, :ROUTE_LANES] + hi_both[:, ROUTE_LANES:] + _dot(lo, rhi_ref[...])) + rb_ref[...]

    lt = logits.T
    idx = lax.broadcasted_iota(jnp.int32, (SUBLANES, lt.shape[1]), 0)

    def first_row(cond):
        return jnp.min(jnp.where(cond, idx, SUBLANES), axis=0, keepdims=True)

    def softmax_rows(x):
        e = jnp.exp(x - jnp.max(x, axis=0, keepdims=True))
        return e / jnp.sum(e, axis=0, keepdims=True)

    p_c = softmax_rows(lt[0:MOE_GROUPS, :])
    p_grp = jnp.max(p_c, axis=0, keepdims=True)
    grp = first_row(p_c == p_grp)
    lf = lt[MOE_GROUPS:MOE_GROUPS + EXP_PER_GROUP, :]
    for g in range(1, MOE_GROUPS):
        lo_row = MOE_GROUPS + g * EXP_PER_GROUP
        lf = jnp.where(grp == g, lt[lo_row:lo_row + EXP_PER_GROUP, :], lf)
    pf = softmax_rows(lf)
    v1 = jnp.max(pf, axis=0, keepdims=True)
    i1 = first_row(pf == v1)
    rest = idx != i1
    v2 = jnp.max(jnp.where(rest, pf, -1.0), axis=0, keepdims=True)
    i2 = first_row(rest & (pf == v2))
    tot = v1 + v2
    e1 = grp * EXP_PER_GROUP + i1
    e2 = grp * EXP_PER_GROUP + i2
    eid_ref[...] = jnp.where(idx == 0, e1, jnp.where(idx == 1, e2, 0))
    wts_ref[...] = jnp.where(idx == 0, v1 / tot * p_grp, jnp.where(idx == 1, v2 / tot * p_grp, 0.0))


def _xattn_route(q, kv, h1, xo_w, norm_w, r_hi, r_lo, r_b, *, batch, seq, mem_len, tm):
    t, d = h1.shape
    n = seq // tm
    const = lambda b, i: (0, 0)
    row = lambda b, i: (b * n + i, 0)
    return pl.pallas_call(
        _xattn_route_kernel,
        grid=(batch, n),
        in_specs=[
            pl.BlockSpec((tm, d), row),
            pl.BlockSpec((mem_len, d), lambda b, i: (b, 0)),
            pl.BlockSpec((mem_len, d), lambda b, i: (b, 1)),
            pl.BlockSpec((tm, d), row),
            _resident((d, d), const),
            _resident((1, d), const),
            _resident((d, ROUTE_LANES), const),
            _resident((d, ROUTE_LANES), const),
            _resident((1, ROUTE_LANES), const),
        ],
        out_specs=[
            pl.BlockSpec((tm, d), row),
            pl.BlockSpec((tm, d // 2), row),
            pl.BlockSpec((SUBLANES, tm), row),
            pl.BlockSpec((SUBLANES, tm), row),
        ],
        out_shape=[
            jax.ShapeDtypeStruct((t, d), F32),
            jax.ShapeDtypeStruct((t, d // 2), jnp.uint32),
            jax.ShapeDtypeStruct((t // tm * SUBLANES, tm), jnp.int32),
            jax.ShapeDtypeStruct((t // tm * SUBLANES, tm), F32),
        ],
        compiler_params=pltpu.CompilerParams(
            dimension_semantics=("parallel", "parallel"), vmem_limit_bytes=VMEM_LIMIT),
        name="xattn_route",
    )(q, kv, kv, h1, xo_w, norm_w.reshape(1, d).astype(F32), r_hi, r_lo, r_b)


def _moe_kernel(be_ref, par_ref, first_ref, nxt_ref, base_ref, nval_ref, tok_ref, nu_ref,
                hn_hbm, wg_hbm, wu_hbm, wd_hbm, o_ref,
                xbuf, wgb, wub, wdb, gsem, wsem):
    b = pl.program_id(0)
    n_used = nu_ref[0]

    def weight_copies(e, slot):
        copies = []
        for hbm, buf in ((wg_hbm, wgb), (wu_hbm, wub), (wd_hbm, wdb)):
            rows = hbm.shape[1] // WEIGHT_DMA_CHUNKS
            for c in range(WEIGHT_DMA_CHUNKS):
                sl = pl.ds(c * rows, rows)
                copies.append(pltpu.make_async_copy(hbm.at[e, sl], buf.at[slot, sl], wsem.at[slot]))
        return copies

    def groups(blk):
        return (nval_ref[blk] + ROW_GROUP - 1) // ROW_GROUP

    def start_gather(blk, slot):
        base = base_ref[blk]

        def body(g, carry):
            for r in range(ROW_GROUP):
                tok = tok_ref[base + g * ROW_GROUP + r]
                pltpu.make_async_copy(hn_hbm.at[pl.ds(tok, 1)], xbuf.at[slot, g, pl.ds(r, 1)],
                                      gsem.at[slot]).start()
            return carry
        lax.fori_loop(0, groups(blk), body, 0)

    def wait_gather(blk, slot):
        filled = xbuf.at[slot, pl.ds(0, groups(blk))]
        pltpu.make_async_copy(filled, filled, gsem.at[slot]).wait()

    @pl.when(b == 0)
    def _():
        xbuf[...] = jnp.zeros(xbuf.shape, xbuf.dtype)
        for c in weight_copies(be_ref[0], par_ref[0]):
            c.start(priority=WEIGHT_DMA_PRIORITY)

        @pl.when(nu_ref[1] >= 0)
        def _():
            for c in weight_copies(nu_ref[1], 1):
                c.start(priority=WEIGHT_DMA_PRIORITY)

        start_gather(0, 0)

    @pl.when(b < n_used)
    def _():
        slot = b % 2
        wslot = par_ref[b]
        is_first = first_ref[b] == 1

        @pl.when(is_first & (nxt_ref[b] >= 0))
        def _():
            for c in weight_copies(nxt_ref[b], (wslot + WEIGHT_SLOTS - 1) % WEIGHT_SLOTS):
                c.start(priority=WEIGHT_DMA_PRIORITY)

        @pl.when(b + 1 < n_used)
        def _():
            start_gather(b + 1, 1 - slot)

        @pl.when(is_first)
        def _():
            for c in weight_copies(0, wslot):
                c.wait()

        wait_gather(b, slot)
        half = xbuf.shape[-1]

        def experts(rows):
            groups_ = rows // ROW_GROUP
            x_hi, x_lo = (v.astype(BF16)
                          for v in _unpack_halves(xbuf[slot, 0:groups_].reshape(rows, half)))

            def up(w):
                return (_dot(x_hi, w[wslot, 0:half, :].astype(BF16))
                        + _dot(x_lo, w[wslot, half:, :].astype(BF16)))

            mid = (jax.nn.silu(up(wgb)) * up(wub)).astype(BF16)
            o_ref[0:rows, :] = _pack_halves(_dot(mid, wdb[wslot].astype(BF16)))
            if rows < MOE_BLOCK:
                o_ref[rows:, :] = jnp.zeros((MOE_BLOCK - rows, half), o_ref.dtype)

        small = nval_ref[b] <= MOE_BLOCK // 2
        pl.when(small)(functools.partial(experts, MOE_BLOCK // 2))
        pl.when(jnp.logical_not(small))(functools.partial(experts, MOE_BLOCK))

    @pl.when(b >= n_used)
    def _():
        o_ref[...] = jnp.zeros(o_ref.shape, o_ref.dtype)


def _moe_experts(hn_packed, w_gate, w_up, w_down, meta, tok_sorted, n_used):
    d, d_ff = w_gate.shape[1:]
    half = hn_packed.shape[1]
    blk_exp, par, first, nxt, base, nval = meta
    n_blocks = blk_exp.shape[0]
    any_spec = pl.BlockSpec(memory_space=pl.ANY)
    return pl.pallas_call(
        _moe_kernel,
        grid_spec=pltpu.PrefetchScalarGridSpec(
            num_scalar_prefetch=8,
            grid=(n_blocks,),
            in_specs=[any_spec, any_spec, any_spec, any_spec],
            out_specs=pl.BlockSpec((MOE_BLOCK, half), lambda b, *_: (b, 0)),
            scratch_shapes=[
                pltpu.VMEM((2, MOE_BLOCK // ROW_GROUP, ROW_GROUP, half), jnp.uint32),
                pltpu.VMEM((WEIGHT_SLOTS, d, d_ff), F32),
                pltpu.VMEM((WEIGHT_SLOTS, d, d_ff), F32),
                pltpu.VMEM((WEIGHT_SLOTS, d_ff, d), F32),
                pltpu.SemaphoreType.DMA((2,)),
                pltpu.SemaphoreType.DMA((WEIGHT_SLOTS,)),
            ],
        ),
        out_shape=jax.ShapeDtypeStruct((n_blocks * MOE_BLOCK, half), jnp.uint32),
        compiler_params=pltpu.CompilerParams(
            dimension_semantics=("arbitrary",), vmem_limit_bytes=VMEM_LIMIT),
        name="moe_experts",
    )(blk_exp, par, first, nxt, base, nval, tok_sorted, n_used, hn_packed, w_gate, w_up, w_down)


def _combine_kernel(pos_ref, y_hbm, h_ref, w_ref, o_ref, ybuf, sem, *, tm):
    i = pl.program_id(0)
    n_groups = tm // ROW_GROUP

    def start_gather(tile, slot):
        base = tile * (tm * TOP_K_FINE)

        def body(g, carry):
            for r in range(ROW_GROUP):
                for k in range(TOP_K_FINE):
                    row = pos_ref[base + (g * ROW_GROUP + r) * TOP_K_FINE + k]
                    pltpu.make_async_copy(y_hbm.at[pl.ds(row, 1)], ybuf.at[slot, k, g, pl.ds(r, 1)],
                                          sem.at[slot]).start(priority=1)
            return carry
        lax.fori_loop(0, n_groups, body, 0)

    @pl.when(i == 0)
    def _():
        start_gather(0, 0)

    @pl.when(i + 1 < pl.num_programs(0))
    def _():
        start_gather(i + 1, (i + 1) % 2)

    slot = i % 2
    pltpu.make_async_copy(ybuf.at[slot], ybuf.at[slot], sem.at[slot]).wait()
    w = w_ref[...]
    half = ybuf.shape[-1]
    y0 = _unpack_halves(ybuf[slot, 0].reshape(tm, half))
    y1 = _unpack_halves(ybuf[slot, 1].reshape(tm, half))
    for c in range(2):
        cols = slice(c * half, (c + 1) * half)
        o_ref[:, cols] = h_ref[:, cols] + (w[:, 0:1] * y0[c] + w[:, 1:2] * y1[c])


def _combine(y, h2, wts, pos, *, tm):
    t, d = h2.shape
    kern = functools.partial(_combine_kernel, tm=tm)
    return pl.pallas_call(
        kern,
        grid_spec=pltpu.PrefetchScalarGridSpec(
            num_scalar_prefetch=1,
            grid=(t // tm,),
            in_specs=[
                pl.BlockSpec(memory_space=pl.ANY),
                pl.BlockSpec((tm, d), lambda i, pos: (i, 0)),
                pl.BlockSpec((tm, ROUTE_LANES), lambda i, pos: (i, 0)),
            ],
            out_specs=pl.BlockSpec((tm, d), lambda i, pos: (i, 0)),
            scratch_shapes=[
                pltpu.VMEM((2, TOP_K_FINE, tm // ROW_GROUP, ROW_GROUP, y.shape[1]), jnp.uint32),
                pltpu.SemaphoreType.DMA((2,)),
            ],
        ),
        out_shape=jax.ShapeDtypeStruct((t, d), F32),
        compiler_params=pltpu.CompilerParams(
            dimension_semantics=("arbitrary",), vmem_limit_bytes=VMEM_LIMIT),
        name="moe_combine",
    )(pos, y, h2, wts)


def _lookup(table, idx):
    sel = idx[:, None] == jnp.arange(table.shape[0], dtype=jnp.int32)[None, :]
    return jnp.sum(jnp.where(sel, table[None, :], 0), axis=1).astype(jnp.int32)


def _dispatch(eid, n_tokens):
    n_assign = n_tokens * TOP_K_FINE
    experts = jnp.arange(N_EXPERTS, dtype=jnp.int32)
    e_flat = eid.reshape(n_assign)
    a_ids = jnp.arange(n_assign, dtype=jnp.int32)
    e_s, order = lax.sort_key_val(e_flat, a_ids)
    counts = jnp.sum((e_flat[:, None] == experts[None, :]).astype(jnp.int32), axis=0)
    starts = jnp.cumsum(counts) - counts
    nb = (counts + MOE_BLOCK - 1) // MOE_BLOCK
    blk_end = jnp.cumsum(nb)
    blk_start = blk_end - nb
    n_used = blk_end[-1]
    n_blocks = (n_assign + N_EXPERTS * (MOE_BLOCK - 1)) // MOE_BLOCK
    b_ids = jnp.arange(n_blocks, dtype=jnp.int32)
    used = b_ids < n_used
    blk_exp = jnp.minimum(jnp.sum((blk_end[None, :] <= b_ids[:, None]).astype(jnp.int32), axis=1),
                          N_EXPERTS - 1)
    j = b_ids - _lookup(blk_start, blk_exp)
    base = jnp.where(used, _lookup(starts, blk_exp) + j * MOE_BLOCK, 0)
    nval = jnp.where(used, jnp.clip(_lookup(counts, blk_exp) - j * MOE_BLOCK, 0, MOE_BLOCK), 0)
    first = (used & (j == 0)).astype(jnp.int32)
    active = counts > 0
    arank = jnp.cumsum(active.astype(jnp.int32)) - 1
    n_act = arank[-1] + 1
    by_rank = jnp.sum(jnp.where(active[None, :] & (arank[None, :] == experts[:, None]), experts[None, :], 0), axis=1)
    rank = _lookup(arank, blk_exp)
    par = rank % WEIGHT_SLOTS
    ahead = rank + (WEIGHT_SLOTS - 1)
    nxt = jnp.where(ahead < n_act, _lookup(by_rank, jnp.minimum(ahead, N_EXPERTS - 1)), -1)
    second = jnp.where(n_act > 1, by_rank[1], -1)
    meta = tuple(v.astype(jnp.int32) for v in (blk_exp, par, first, nxt, base, nval))
    row_sorted = a_ids + _lookup(blk_start * MOE_BLOCK - starts, e_s)
    _, pos = lax.sort_key_val(order, row_sorted)
    tok_sorted = jnp.concatenate([lax.shift_right_logical(order, 1), jnp.zeros((ROW_GROUP,), jnp.int32)])
    return meta, tok_sorted, pos, jnp.stack([n_used, second]).astype(jnp.int32)


def kernel(x, mem, norm1_w, w_in, q_norm_w, k_norm_w, lambda_q1, lambda_k1, lambda_q2, lambda_k2, subln_w, ssm_lambda_re, ssm_lambda_im, ssm_log_dt, ssm_b_re, ssm_b_im, ssm_c_re, ssm_c_im, ssm_d, ssm_glu_w, ssm_glu_b, ssm_out_norm_w, w_out, norm2_w, mem_norm_w, xq_w, xkv_w, xq_norm_w, xk_norm_w, xo_w, norm3_w, router_coarse_w, router_coarse_b, router_fine_w, router_fine_b, expert_w_gate, expert_w_up, expert_w_down):
    batch, seq, d = x.shape
    mem_len = mem.shape[1]
    t = batch * seq
    depth = norm1_w.shape[0]
    d_attn = DA_HEADS * DA_V_DIM
    d_ssm = d - d_attn
    qk_cols = DA_HEADS * 2 * DA_QK_DIM
    x_hd = d // X_HEADS
    h = x.reshape(t, d)
    mem2 = mem.reshape(batch * mem_len, d)

    for l in range(depth):
        lam_init = 0.8 - 0.6 * math.exp(-0.3 * l)
        lam = (jnp.exp(jnp.sum(lambda_q1[l].astype(F32) * lambda_k1[l].astype(F32)))
               - jnp.exp(jnp.sum(lambda_q2[l].astype(F32) * lambda_k2[l].astype(F32)))
               + lam_init).reshape(1)

        n_rep = qk_cols // DA_QK_DIM
        in_gain = jnp.concatenate([
            jnp.tile(q_norm_w[l].astype(F32) * (DA_QK_DIM ** -0.5 * LOG2E), n_rep),
            jnp.tile(k_norm_w[l].astype(F32), n_rep),
            jnp.ones((d_attn + d_ssm,), F32)])
        proj = _norm_matmul(h, norm1_w[l], w_in[l].astype(BF16), in_gain,
                            n_norm_cols=2 * qk_cols, chunk=DA_QK_DIM, tm=PROJ_ROW_TILE, tn=PROJ_COL_TILE, name="in_proj")
        sub_gain = (subln_w[l].astype(F32) * (1.0 - lam_init)).reshape(1, DA_V_DIM)
        a = _diff_attn(proj, lam, sub_gain, batch=batch, seq=seq, tq=ATTN_TILE)

        bd, a_re, a_im, cd, dd = _s5_params(ssm_lambda_re[l], ssm_lambda_im[l], ssm_log_dt[l],
                                            ssm_b_re[l], ssm_b_im[l], ssm_c_re[l], ssm_c_im[l], ssm_d[l])
        y = _s5(proj, 2 * qk_cols + d_attn, d_ssm, bd, a_re, a_im, cd, dd, batch=batch, seq=seq)
        h = _mix_out(a, y, h, ssm_glu_w[l].astype(BF16), ssm_glu_b[l], ssm_out_norm_w[l],
                     w_out[l].astype(BF16), tm=ATTN_TILE, attn_tiles=seq // ATTN_TILE)

        kv_gain = jnp.concatenate([jnp.tile(xk_norm_w[l].astype(F32), X_HEADS), jnp.ones((d,), F32)])
        kv = _norm_matmul(mem2, mem_norm_w[l], xkv_w[l], kv_gain,
                          n_norm_cols=d, chunk=x_hd, tm=batch * mem_len, tn=KV_COL_TILE, name="kv_proj")
        q_gain = jnp.tile(xq_norm_w[l].astype(F32) * (x_hd ** -0.5), X_HEADS)
        q = _norm_matmul(h, norm2_w[l], xq_w[l].astype(BF16), q_gain,
                         n_norm_cols=d, chunk=x_hd, tm=PROJ_ROW_TILE, tn=PROJ_COL_TILE, name="xq_proj")
        r_w = jnp.concatenate([router_coarse_w[l].astype(F32), router_fine_w[l].astype(F32)], axis=1)
        r_w = jnp.pad(r_w, ((0, 0), (0, ROUTE_LANES - r_w.shape[1])))
        r_hi = r_w.astype(BF16)
        r_lo = (r_w - r_hi.astype(F32)).astype(BF16)
        r_b = jnp.concatenate([router_coarse_b[l].astype(F32), router_fine_b[l].astype(F32)])
        r_b = jnp.pad(r_b, (0, ROUTE_LANES - r_b.shape[0])).reshape(1, ROUTE_LANES)
        h2, hn3, eid_t, wts_t = _xattn_route(q, kv, h, xo_w[l].astype(BF16), norm3_w[l], r_hi, r_lo, r_b,
                                             batch=batch, seq=seq, mem_len=mem_len, tm=XATTN_ROW_TILE)

        def per_token(x_t):
            return x_t.reshape(-1, SUBLANES, x_t.shape[1])[:, :TOP_K_FINE, :].transpose(0, 2, 1).reshape(t, TOP_K_FINE)

        eid = per_token(eid_t)
        wts = jnp.pad(per_token(wts_t), ((0, 0), (0, ROUTE_LANES - TOP_K_FINE)))

        meta, tok_sorted, pos, n_used = _dispatch(eid, t)
        y = _moe_experts(hn3, expert_w_gate[l], expert_w_up[l], expert_w_down[l], meta, tok_sorted, n_used)
        h = _combine(y, h2, wts, pos, tm=COMBINE_ROW_TILE)

    return h.reshape(batch, seq, d)
```
